```python
import math
import jax, jax.numpy as jnp
from jax import lax
import numpy as np

D_MODEL = 1024
BATCH = 8
SEQ = 2048
DEPTH = 4

N_EVEN = (DEPTH + 1) // 2
N_ODD = DEPTH // 2
EPS = 1e-6
NEG_INF = -1e30
Q_BLOCK = 128
FOX_WIDTH = D_MODEL // 2
FOX_HEAD_DIM = 64
FOX_HEADS = FOX_WIDTH // FOX_HEAD_DIM
SC_WIDTH = D_MODEL - FOX_WIDTH
SC_GROUPS = SC_WIDTH // 64
SC_K = 3
AB_IN = 3 * FOX_WIDTH + FOX_HEADS + 3 * SC_WIDTH
LRU_WIDTH = D_MODEL
LRU_BW = 256
LRU_BLOCKS = LRU_WIDTH // LRU_BW
RG_CONV_K = 4
RG_C = 8.0
RG_MIN_RAD = 0.9
RG_MAX_RAD = 0.999
MEM_LEN = 256
MEM_HEADS = 4
MEM_HEAD_DIM = D_MODEL // MEM_HEADS
D_FF = ((8 * D_MODEL // 3 + 255) // 256) * 256

kernel_name = "fox_shortconv_rglru_sandwich_hybrid"


def rmsnorm(x, g):
    x32 = x.astype(jnp.float32)
    y = x32 * lax.rsqrt(jnp.mean(x32 * x32, axis=-1, keepdims=True) + EPS)
    return y.astype(x.dtype) * g


def causal_depthwise_conv(x, w):
    k_width, ch = w.shape
    return lax.conv_general_dilated(
        x, w[:, None, :], window_strides=(1,), padding=[(k_width - 1, 0)],
        dimension_numbers=("NWC", "WIO", "NWC"), feature_group_count=ch)


def forgetting_attention(q, k, v, log_f):
    seq = q.shape[1]
    scale = q.shape[-1] ** -0.5
    cum = jnp.cumsum(log_f, axis=1).transpose(0, 2, 1)
    outs = []
    for blk in range(seq // Q_BLOCK):
        lo, hi = blk * Q_BLOCK, (blk + 1) * Q_BLOCK
        s = jnp.einsum("bqhd,bkhd->bhqk", q[:, lo:hi], k[:, :hi],
                       preferred_element_type=jnp.float32) * scale
        s = s + cum[:, :, lo:hi, None] - cum[:, :, None, :hi]
        causal = (lo + jnp.arange(Q_BLOCK))[:, None] >= jnp.arange(hi)[None, :]
        s = jnp.where(causal, s, NEG_INF)
        p = jax.nn.softmax(s, axis=-1).astype(v.dtype)
        outs.append(jnp.einsum("bhqk,bkhd->bqhd", p, v[:, :hi]))
    return jnp.concatenate(outs, axis=1)


def fox_shortconv_mixer(h, w_in, b_f, conv_w, w_out):
    bsz, seq, _ = h.shape
    proj = h @ w_in
    i1 = FOX_WIDTH
    i2 = 2 * FOX_WIDTH
    i3 = 3 * FOX_WIDTH
    i4 = i3 + FOX_HEADS
    i5 = i4 + SC_WIDTH
    i6 = i5 + SC_WIDTH
    q, k, v, f_logit, b_gate, c_gate, u = jnp.split(proj, [i1, i2, i3, i4, i5, i6], axis=-1)
    heads = lambda t: t.reshape(bsz, seq, FOX_HEADS, FOX_HEAD_DIM)
    log_f = jax.nn.log_sigmoid((f_logit + b_f).astype(jnp.float32))
    y_a = forgetting_attention(heads(q), heads(k), heads(v), log_f).reshape(bsz, seq, FOX_WIDTH)
    y_b = b_gate * causal_depthwise_conv(c_gate * u, conv_w)
    return jnp.concatenate([y_a, y_b], axis=-1) @ w_out


def _lru_combine(c1, c2):
    a1, b1 = c1
    a2, b2 = c2
    return a1 * a2, a2 * b1 + b2


def rglru_mixer(h, w_in, conv_w, conv_b, w_a, b_a, w_i, b_i, lam, w_out):
    bsz, seq, _ = h.shape
    gate, u = jnp.split(h @ w_in, 2, axis=-1)
    u = causal_depthwise_conv(u, conv_w) + conv_b
    ub = u.reshape(bsz, seq, LRU_BLOCKS, LRU_BW)
    r = jax.nn.sigmoid(jnp.einsum("bsnc,ncd->bsnd", ub, w_a) + b_a).reshape(bsz, seq, LRU_WIDTH)
    i = jax.nn.sigmoid(jnp.einsum("bsnc,ncd->bsnd", ub, w_i) + b_i).reshape(bsz, seq, LRU_WIDTH)
    log_a = -RG_C * r.astype(jnp.float32) * jax.nn.softplus(-lam.astype(jnp.float32))
    a = jnp.exp(log_a)
    b = jnp.sqrt(-jnp.expm1(2.0 * log_a)) * (i * u).astype(jnp.float32)
    _, hs = lax.associative_scan(_lru_combine, (a, b), axis=1)
    y = jax.nn.gelu(gate) * hs.astype(h.dtype)
    return y @ w_out


def memory_cross_attention(h, m, w_q, w_kv, w_o):
    bsz, seq, _ = h.shape
    mlen = m.shape[1]
    q = (h @ w_q).reshape(bsz, seq, MEM_HEADS, MEM_HEAD_DIM)
    k, v = jnp.split(m @ w_kv, 2, axis=-1)
    k = k.reshape(bsz, mlen, MEM_HEADS, MEM_HEAD_DIM)
    v = v.reshape(bsz, mlen, MEM_HEADS, MEM_HEAD_DIM)
    s = jnp.einsum("bqhd,bkhd->bhqk", q, k, preferred_element_type=jnp.float32) * (MEM_HEAD_DIM ** -0.5)
    p = jax.nn.softmax(s, axis=-1).astype(v.dtype)
    o = jnp.einsum("bhqk,bkhd->bqhd", p, v).reshape(bsz, seq, D_MODEL)
    return o @ w_o


def swiglu(h, w_gu, w_down):
    g, u = jnp.split(h @ w_gu, 2, axis=-1)
    return (jax.nn.silu(g) * u) @ w_down


def _fwd_setup_inputs(seed: int = 0) -> dict:
    key = jax.random.key(seed)
    ks = iter(jax.random.split(key, 40))
    dense = lambda shape, fan_in: jax.random.normal(next(ks), shape, jnp.float32) * (fan_in ** -0.5)
    gain = lambda shape: 1.0 + 0.02 * jax.random.normal(next(ks), shape, jnp.float32)
    small = lambda shape: 0.01 * jax.random.normal(next(ks), shape, jnp.float32)
    x = jax.random.normal(next(ks), (BATCH, SEQ, D_MODEL), jnp.float32)
    mem = jax.random.normal(next(ks), (BATCH, MEM_LEN, D_MODEL), jnp.float32)
    rad = jax.random.uniform(next(ks), (N_ODD, LRU_WIDTH), jnp.float32, RG_MIN_RAD, RG_MAX_RAD)
    c_lam = -jnp.log(jnp.expm1(-jnp.log(rad) / RG_C))
    return {
        "x": x,
        "mem": mem,
        "g_mix_pre": gain((DEPTH, D_MODEL)),
        "g_mix_post": gain((DEPTH, D_MODEL)),
        "g_cross_pre": gain((DEPTH, D_MODEL)),
        "g_mem": gain((DEPTH, D_MODEL)),
        "g_cross_post": gain((DEPTH, D_MODEL)),
        "g_ffn_pre": gain((DEPTH, D_MODEL)),
        "g_ffn_post": gain((DEPTH, D_MODEL)),
        "w_xq": dense((DEPTH, D_MODEL, D_MODEL), D_MODEL),
        "w_xkv": dense((DEPTH, D_MODEL, 2 * D_MODEL), D_MODEL),
        "w_xo": dense((DEPTH, D_MODEL, D_MODEL), D_MODEL),
        "w_ffn_gu": dense((DEPTH, D_MODEL, 2 * D_FF), D_MODEL),
        "w_ffn_down": dense((DEPTH, D_FF, D_MODEL), D_FF),
        "ab_w_in": dense((N_EVEN, D_MODEL, AB_IN), D_MODEL),
        "ab_b_f": jax.random.uniform(next(ks), (N_EVEN, FOX_HEADS), jnp.float32, 2.0, 5.0),
        "ab_conv_w": dense((N_EVEN, SC_K, SC_WIDTH), SC_K),
        "ab_w_out": dense((N_EVEN, D_MODEL, D_MODEL), D_MODEL),
        "c_w_in": dense((N_ODD, D_MODEL, 2 * LRU_WIDTH), D_MODEL),
        "c_conv_w": dense((N_ODD, RG_CONV_K, LRU_WIDTH), RG_CONV_K),
        "c_conv_b": small((N_ODD, LRU_WIDTH)),
        "c_w_a": dense((N_ODD, LRU_BLOCKS, LRU_BW, LRU_BW), LRU_BW),
        "c_b_a": small((N_ODD, LRU_BLOCKS, LRU_BW)),
        "c_w_i": dense((N_ODD, LRU_BLOCKS, LRU_BW, LRU_BW), LRU_BW),
        "c_b_i": small((N_ODD, LRU_BLOCKS, LRU_BW)),
        "c_lam": c_lam,
        "c_w_out": dense((N_ODD, LRU_WIDTH, D_MODEL), LRU_WIDTH),
    }


def _fwd_reference(x, mem, g_mix_pre, g_mix_post, g_cross_pre, g_mem, g_cross_post,
              g_ffn_pre, g_ffn_post, w_xq, w_xkv, w_xo, w_ffn_gu, w_ffn_down,
              ab_w_in, ab_b_f, ab_conv_w, ab_w_out,
              c_w_in, c_conv_w, c_conv_b, c_w_a, c_b_a, c_w_i, c_b_i, c_lam, c_w_out):
    for layer in range(DEPTH):
        h = rmsnorm(x, g_mix_pre[layer])
        if layer % 2 == 0:
            e = layer // 2
            y = fox_shortconv_mixer(h, ab_w_in[e], ab_b_f[e], ab_conv_w[e], ab_w_out[e])
        else:
            o = layer // 2
            y = rglru_mixer(h, c_w_in[o], c_conv_w[o], c_conv_b[o], c_w_a[o], c_b_a[o],
                            c_w_i[o], c_b_i[o], c_lam[o], c_w_out[o])
        x = x + rmsnorm(y, g_mix_post[layer])
        h = rmsnorm(x, g_cross_pre[layer])
        m = rmsnorm(mem, g_mem[layer])
        y = memory_cross_attention(h, m, w_xq[layer], w_xkv[layer], w_xo[layer])
        x = x + rmsnorm(y, g_cross_post[layer])
        h = rmsnorm(x, g_ffn_pre[layer])
        y = swiglu(h, w_ffn_gu[layer], w_ffn_down[layer])
        x = x + rmsnorm(y, g_ffn_post[layer])
    return x


import jax as _jax
import jax.numpy as _jnp

TWIN_FORMAT = 'train_step'
FWD_PARAMS = ['x', 'mem', 'g_mix_pre', 'g_mix_post', 'g_cross_pre', 'g_mem', 'g_cross_post', 'g_ffn_pre', 'g_ffn_post', 'w_xq', 'w_xkv', 'w_xo', 'w_ffn_gu', 'w_ffn_down', 'ab_w_in', 'ab_b_f', 'ab_conv_w', 'ab_w_out', 'c_w_in', 'c_conv_w', 'c_conv_b', 'c_w_a', 'c_b_a', 'c_w_i', 'c_b_i', 'c_lam', 'c_w_out']
TWIN_WEIGHTS = ['g_mix_pre', 'g_mix_post', 'g_cross_pre', 'g_mem', 'g_cross_post', 'g_ffn_pre', 'g_ffn_post', 'w_xq', 'w_xkv', 'w_xo', 'w_ffn_gu', 'w_ffn_down', 'ab_w_in', 'ab_b_f', 'ab_conv_w', 'ab_w_out', 'c_w_in', 'c_conv_w', 'c_conv_b', 'c_w_a', 'c_b_a', 'c_w_i', 'c_b_i', 'c_lam', 'c_w_out']
TWIN_DIFF_INPUT = 'x'
TWIN_INPUTS = ['x', 'mem', 'g_mix_pre', 'g_mix_post', 'g_cross_pre', 'g_mem', 'g_cross_post', 'g_ffn_pre', 'g_ffn_post', 'w_xq', 'w_xkv', 'w_xo', 'w_ffn_gu', 'w_ffn_down', 'ab_w_in', 'ab_b_f', 'ab_conv_w', 'ab_w_out', 'c_w_in', 'c_conv_w', 'c_conv_b', 'c_w_a', 'c_b_a', 'c_w_i', 'c_b_i', 'c_lam', 'c_w_out', 'loss_target', 'm_g_mix_pre', 'm_g_mix_post', 'm_g_cross_pre', 'm_g_mem', 'm_g_cross_post', 'm_g_ffn_pre', 'm_g_ffn_post', 'm_w_xq', 'm_w_xkv', 'm_w_xo', 'm_w_ffn_gu', 'm_w_ffn_down', 'm_ab_w_in', 'm_ab_b_f', 'm_ab_conv_w', 'm_ab_w_out', 'm_c_w_in', 'm_c_conv_w', 'm_c_conv_b', 'm_c_w_a', 'm_c_b_a', 'm_c_w_i', 'm_c_b_i', 'm_c_lam', 'm_c_w_out', 'v_g_mix_pre', 'v_g_mix_post', 'v_g_cross_pre', 'v_g_mem', 'v_g_cross_post', 'v_g_ffn_pre', 'v_g_ffn_post', 'v_w_xq', 'v_w_xkv', 'v_w_xo', 'v_w_ffn_gu', 'v_w_ffn_down', 'v_ab_w_in', 'v_ab_b_f', 'v_ab_conv_w', 'v_ab_w_out', 'v_c_w_in', 'v_c_conv_w', 'v_c_conv_b', 'v_c_w_a', 'v_c_b_a', 'v_c_w_i', 'v_c_b_i', 'v_c_lam', 'v_c_w_out']
TWIN_OUTPUTS = ['loss', 'grad_x', 'grad_g_mix_pre', 'grad_g_mix_post', 'grad_g_cross_pre', 'grad_g_mem', 'grad_g_cross_post', 'grad_g_ffn_pre', 'grad_g_ffn_post', 'grad_w_xq', 'grad_w_xkv', 'grad_w_xo', 'grad_w_ffn_gu', 'grad_w_ffn_down', 'grad_ab_w_in', 'grad_ab_b_f', 'grad_ab_conv_w', 'grad_ab_w_out', 'grad_c_w_in', 'grad_c_conv_w', 'grad_c_conv_b', 'grad_c_w_a', 'grad_c_b_a', 'grad_c_w_i', 'grad_c_b_i', 'grad_c_lam', 'grad_c_w_out', 'delta_g_mix_pre', 'delta_g_mix_post', 'delta_g_cross_pre', 'delta_g_mem', 'delta_g_cross_post', 'delta_g_ffn_pre', 'delta_g_ffn_post', 'delta_w_xq', 'delta_w_xkv', 'delta_w_xo', 'delta_w_ffn_gu', 'delta_w_ffn_down', 'delta_ab_w_in', 'delta_ab_b_f', 'delta_ab_conv_w', 'delta_ab_w_out', 'delta_c_w_in', 'delta_c_conv_w', 'delta_c_conv_b', 'delta_c_w_a', 'delta_c_b_a', 'delta_c_w_i', 'delta_c_b_i', 'delta_c_lam', 'delta_c_w_out', 'new_m_g_mix_pre', 'new_m_g_mix_post', 'new_m_g_cross_pre', 'new_m_g_mem', 'new_m_g_cross_post', 'new_m_g_ffn_pre', 'new_m_g_ffn_post', 'new_m_w_xq', 'new_m_w_xkv', 'new_m_w_xo', 'new_m_w_ffn_gu', 'new_m_w_ffn_down', 'new_m_ab_w_in', 'new_m_ab_b_f', 'new_m_ab_conv_w', 'new_m_ab_w_out', 'new_m_c_w_in', 'new_m_c_conv_w', 'new_m_c_conv_b', 'new_m_c_w_a', 'new_m_c_b_a', 'new_m_c_w_i', 'new_m_c_b_i', 'new_m_c_lam', 'new_m_c_w_out', 'new_v_g_mix_pre', 'new_v_g_mix_post', 'new_v_g_cross_pre', 'new_v_g_mem', 'new_v_g_cross_post', 'new_v_g_ffn_pre', 'new_v_g_ffn_post', 'new_v_w_xq', 'new_v_w_xkv', 'new_v_w_xo', 'new_v_w_ffn_gu', 'new_v_w_ffn_down', 'new_v_ab_w_in', 'new_v_ab_b_f', 'new_v_ab_conv_w', 'new_v_ab_w_out', 'new_v_c_w_in', 'new_v_c_conv_w', 'new_v_c_conv_b', 'new_v_c_w_a', 'new_v_c_b_a', 'new_v_c_w_i', 'new_v_c_b_i', 'new_v_c_lam', 'new_v_c_w_out']
TWIN_LEAF_KINDS = {'loss': 'loss', 'grad_x': 'grad_x', 'grad_g_mix_pre': 'grad_w', 'grad_g_mix_post': 'grad_w', 'grad_g_cross_pre': 'grad_w', 'grad_g_mem': 'grad_w', 'grad_g_cross_post': 'grad_w', 'grad_g_ffn_pre': 'grad_w', 'grad_g_ffn_post': 'grad_w', 'grad_w_xq': 'grad_w', 'grad_w_xkv': 'grad_w', 'grad_w_xo': 'grad_w', 'grad_w_ffn_gu': 'grad_w', 'grad_w_ffn_down': 'grad_w', 'grad_ab_w_in': 'grad_w', 'grad_ab_b_f': 'grad_w', 'grad_ab_conv_w': 'grad_w', 'grad_ab_w_out': 'grad_w', 'grad_c_w_in': 'grad_w', 'grad_c_conv_w': 'grad_w', 'grad_c_conv_b': 'grad_w', 'grad_c_w_a': 'grad_w', 'grad_c_b_a': 'grad_w', 'grad_c_w_i': 'grad_w', 'grad_c_b_i': 'grad_w', 'grad_c_lam': 'grad_w', 'grad_c_w_out': 'grad_w', 'delta_g_mix_pre': 'delta_w', 'delta_g_mix_post': 'delta_w', 'delta_g_cross_pre': 'delta_w', 'delta_g_mem': 'delta_w', 'delta_g_cross_post': 'delta_w', 'delta_g_ffn_pre': 'delta_w', 'delta_g_ffn_post': 'delta_w', 'delta_w_xq': 'delta_w', 'delta_w_xkv': 'delta_w', 'delta_w_xo': 'delta_w', 'delta_w_ffn_gu': 'delta_w', 'delta_w_ffn_down': 'delta_w', 'delta_ab_w_in': 'delta_w', 'delta_ab_b_f': 'delta_w', 'delta_ab_conv_w': 'delta_w', 'delta_ab_w_out': 'delta_w', 'delta_c_w_in': 'delta_w', 'delta_c_conv_w': 'delta_w', 'delta_c_conv_b': 'delta_w', 'delta_c_w_a': 'delta_w', 'delta_c_b_a': 'delta_w', 'delta_c_w_i': 'delta_w', 'delta_c_b_i': 'delta_w', 'delta_c_lam': 'delta_w', 'delta_c_w_out': 'delta_w', 'new_m_g_mix_pre': 'new_m', 'new_m_g_mix_post': 'new_m', 'new_m_g_cross_pre': 'new_m', 'new_m_g_mem': 'new_m', 'new_m_g_cross_post': 'new_m', 'new_m_g_ffn_pre': 'new_m', 'new_m_g_ffn_post': 'new_m', 'new_m_w_xq': 'new_m', 'new_m_w_xkv': 'new_m', 'new_m_w_xo': 'new_m', 'new_m_w_ffn_gu': 'new_m', 'new_m_w_ffn_down': 'new_m', 'new_m_ab_w_in': 'new_m', 'new_m_ab_b_f': 'new_m', 'new_m_ab_conv_w': 'new_m', 'new_m_ab_w_out': 'new_m', 'new_m_c_w_in': 'new_m', 'new_m_c_conv_w': 'new_m', 'new_m_c_conv_b': 'new_m', 'new_m_c_w_a': 'new_m', 'new_m_c_b_a': 'new_m', 'new_m_c_w_i': 'new_m', 'new_m_c_b_i': 'new_m', 'new_m_c_lam': 'new_m', 'new_m_c_w_out': 'new_m', 'new_v_g_mix_pre': 'new_v', 'new_v_g_mix_post': 'new_v', 'new_v_g_cross_pre': 'new_v', 'new_v_g_mem': 'new_v', 'new_v_g_cross_post': 'new_v', 'new_v_g_ffn_pre': 'new_v', 'new_v_g_ffn_post': 'new_v', 'new_v_w_xq': 'new_v', 'new_v_w_xkv': 'new_v', 'new_v_w_xo': 'new_v', 'new_v_w_ffn_gu': 'new_v', 'new_v_w_ffn_down': 'new_v', 'new_v_ab_w_in': 'new_v', 'new_v_ab_b_f': 'new_v', 'new_v_ab_conv_w': 'new_v', 'new_v_ab_w_out': 'new_v', 'new_v_c_w_in': 'new_v', 'new_v_c_conv_w': 'new_v', 'new_v_c_conv_b': 'new_v', 'new_v_c_w_a': 'new_v', 'new_v_c_b_a': 'new_v', 'new_v_c_w_i': 'new_v', 'new_v_c_b_i': 'new_v', 'new_v_c_lam': 'new_v', 'new_v_c_w_out': 'new_v'}


def _forward(args):
    return _fwd_reference(*[args[k] for k in FWD_PARAMS])


def _output_shape():
    out = _jax.eval_shape(lambda: _forward(_fwd_setup_inputs(0)))
    return out.shape, out.dtype

N_MICROBATCH = 1
ADAM_LR = 0.001
ADAM_B1 = 0.9
ADAM_B2 = 0.999
ADAM_EPS = 1e-08
ADAM_WD = 0.01
ADAM_STEP = 10
PER_EXAMPLE_BATCH_AXIS = {'x': 0, 'mem': 0, 'loss_target': 0}
SHARED_INPUTS = []
_WEIGHT_DTYPES = {'g_mix_pre': _jnp.float32, 'g_mix_post': _jnp.float32, 'g_cross_pre': _jnp.float32, 'g_mem': _jnp.float32, 'g_cross_post': _jnp.float32, 'g_ffn_pre': _jnp.float32, 'g_ffn_post': _jnp.float32, 'w_xq': _jnp.float32, 'w_xkv': _jnp.float32, 'w_xo': _jnp.float32, 'w_ffn_gu': _jnp.float32, 'w_ffn_down': _jnp.float32, 'ab_w_in': _jnp.float32, 'ab_b_f': _jnp.float32, 'ab_conv_w': _jnp.float32, 'ab_w_out': _jnp.float32, 'c_w_in': _jnp.float32, 'c_conv_w': _jnp.float32, 'c_conv_b': _jnp.float32, 'c_w_a': _jnp.float32, 'c_b_a': _jnp.float32, 'c_w_i': _jnp.float32, 'c_b_i': _jnp.float32, 'c_lam': _jnp.float32, 'c_w_out': _jnp.float32}
MOMENT_SCALE = {'g_mix_pre': 5.268842e+00, 'g_mix_post': 1.673664e+01, 'g_cross_pre': 2.596095e+00, 'g_mem': 9.721508e+00, 'g_cross_post': 1.853185e+01, 'g_ffn_pre': 3.324903e+00, 'g_ffn_post': 1.608378e+01, 'w_xq': 2.624274e+00, 'w_xkv': 6.687725e+00, 'w_xo': 9.372546e+00, 'w_ffn_gu': 1.355568e+00, 'w_ffn_down': 2.652733e+00, 'ab_w_in': 1.923712e+00, 'ab_b_f': 2.414531e+00, 'ab_conv_w': 1.888714e+00, 'ab_w_out': 2.948075e+00, 'c_w_in': 5.176312e+00, 'c_conv_w': 6.330701e+00, 'c_conv_b': 1.954935e+01, 'c_w_a': 4.018768e-01, 'c_b_a': 8.071409e-01, 'c_w_i': 9.567567e-01, 'c_b_i': 2.661854e+00, 'c_lam': 1.910657e+00, 'c_w_out': 6.538436e+00}


def _to_microbatches(a, axis):
    t = _jnp.moveaxis(a, axis, 0)
    t = t.reshape((N_MICROBATCH, t.shape[0] // N_MICROBATCH) + t.shape[1:])
    return _jnp.moveaxis(t, 1, axis + 1)


def setup_inputs(seed: int = 0) -> dict:
    inp = _fwd_setup_inputs(seed)
    key = _jax.random.fold_in(_jax.random.key(seed), 7919)
    shape, _ = _output_shape()
    out = dict(inp)
    out["loss_target"] = _jax.random.normal(_jax.random.fold_in(key, 0), shape, _jnp.float32)
    for i, name in enumerate(TWIN_WEIGHTS):
        w = inp[name].astype(_jnp.float32)
        if MOMENT_SCALE is None:
            s = _jnp.sqrt(_jnp.mean(_jnp.square(w)) + 1e-30)
        else:
            s = MOMENT_SCALE[name]
        km, kv = _jax.random.split(_jax.random.fold_in(key, i + 1))
        out[name] = w
        out["m_" + name] = s * _jax.random.normal(km, w.shape, _jnp.float32)
        out["v_" + name] = (s * s) * _jax.random.uniform(kv, w.shape, _jnp.float32, 0.5, 1.5)
    if N_MICROBATCH > 1:
        for name, axis in PER_EXAMPLE_BATCH_AXIS.items():
            out[name] = _to_microbatches(out[name], axis)
    return {'x': out['x'], 'mem': out['mem'], 'g_mix_pre': out['g_mix_pre'], 'g_mix_post': out['g_mix_post'], 'g_cross_pre': out['g_cross_pre'], 'g_mem': out['g_mem'], 'g_cross_post': out['g_cross_post'], 'g_ffn_pre': out['g_ffn_pre'], 'g_ffn_post': out['g_ffn_post'], 'w_xq': out['w_xq'], 'w_xkv': out['w_xkv'], 'w_xo': out['w_xo'], 'w_ffn_gu': out['w_ffn_gu'], 'w_ffn_down': out['w_ffn_down'], 'ab_w_in': out['ab_w_in'], 'ab_b_f': out['ab_b_f'], 'ab_conv_w': out['ab_conv_w'], 'ab_w_out': out['ab_w_out'], 'c_w_in': out['c_w_in'], 'c_conv_w': out['c_conv_w'], 'c_conv_b': out['c_conv_b'], 'c_w_a': out['c_w_a'], 'c_b_a': out['c_b_a'], 'c_w_i': out['c_w_i'], 'c_b_i': out['c_b_i'], 'c_lam': out['c_lam'], 'c_w_out': out['c_w_out'], 'loss_target': out['loss_target'], 'm_g_mix_pre': out['m_g_mix_pre'], 'm_g_mix_post': out['m_g_mix_post'], 'm_g_cross_pre': out['m_g_cross_pre'], 'm_g_mem': out['m_g_mem'], 'm_g_cross_post': out['m_g_cross_post'], 'm_g_ffn_pre': out['m_g_ffn_pre'], 'm_g_ffn_post': out['m_g_ffn_post'], 'm_w_xq': out['m_w_xq'], 'm_w_xkv': out['m_w_xkv'], 'm_w_xo': out['m_w_xo'], 'm_w_ffn_gu': out['m_w_ffn_gu'], 'm_w_ffn_down': out['m_w_ffn_down'], 'm_ab_w_in': out['m_ab_w_in'], 'm_ab_b_f': out['m_ab_b_f'], 'm_ab_conv_w': out['m_ab_conv_w'], 'm_ab_w_out': out['m_ab_w_out'], 'm_c_w_in': out['m_c_w_in'], 'm_c_conv_w': out['m_c_conv_w'], 'm_c_conv_b': out['m_c_conv_b'], 'm_c_w_a': out['m_c_w_a'], 'm_c_b_a': out['m_c_b_a'], 'm_c_w_i': out['m_c_w_i'], 'm_c_b_i': out['m_c_b_i'], 'm_c_lam': out['m_c_lam'], 'm_c_w_out': out['m_c_w_out'], 'v_g_mix_pre': out['v_g_mix_pre'], 'v_g_mix_post': out['v_g_mix_post'], 'v_g_cross_pre': out['v_g_cross_pre'], 'v_g_mem': out['v_g_mem'], 'v_g_cross_post': out['v_g_cross_post'], 'v_g_ffn_pre': out['v_g_ffn_pre'], 'v_g_ffn_post': out['v_g_ffn_post'], 'v_w_xq': out['v_w_xq'], 'v_w_xkv': out['v_w_xkv'], 'v_w_xo': out['v_w_xo'], 'v_w_ffn_gu': out['v_w_ffn_gu'], 'v_w_ffn_down': out['v_w_ffn_down'], 'v_ab_w_in': out['v_ab_w_in'], 'v_ab_b_f': out['v_ab_b_f'], 'v_ab_conv_w': out['v_ab_conv_w'], 'v_ab_w_out': out['v_ab_w_out'], 'v_c_w_in': out['v_c_w_in'], 'v_c_conv_w': out['v_c_conv_w'], 'v_c_conv_b': out['v_c_conv_b'], 'v_c_w_a': out['v_c_w_a'], 'v_c_b_a': out['v_c_b_a'], 'v_c_w_i': out['v_c_w_i'], 'v_c_b_i': out['v_c_b_i'], 'v_c_lam': out['v_c_lam'], 'v_c_w_out': out['v_c_w_out']}


def _loss(weights, diff, rest, loss_target):
    with _jax.named_scope("forward"):
        args = {**rest, TWIN_DIFF_INPUT: diff, **{k: w.astype(_WEIGHT_DTYPES[k]) for k, w in weights.items()}}
        y = _forward(args)
    with _jax.named_scope("loss_head"):
        err = _jnp.square(y.astype(_jnp.float32) - loss_target)
        return 0.5 * _jnp.sum(_jnp.mean(err, axis=-1)) if err.ndim else 0.5 * err


def _adamw(w, g, m, v):
    m = ADAM_B1 * m + (1.0 - ADAM_B1) * g
    v = ADAM_B2 * v + (1.0 - ADAM_B2) * _jnp.square(g)
    m_hat = m / (1.0 - ADAM_B1 ** ADAM_STEP)
    v_hat = v / (1.0 - ADAM_B2 ** ADAM_STEP)
    delta = -ADAM_LR * (m_hat / (_jnp.sqrt(v_hat) + ADAM_EPS) + ADAM_WD * w)
    return delta, m, v


def reference(x, mem, g_mix_pre, g_mix_post, g_cross_pre, g_mem, g_cross_post, g_ffn_pre, g_ffn_post, w_xq, w_xkv, w_xo, w_ffn_gu, w_ffn_down, ab_w_in, ab_b_f, ab_conv_w, ab_w_out, c_w_in, c_conv_w, c_conv_b, c_w_a, c_b_a, c_w_i, c_b_i, c_lam, c_w_out, loss_target, m_g_mix_pre, m_g_mix_post, m_g_cross_pre, m_g_mem, m_g_cross_post, m_g_ffn_pre, m_g_ffn_post, m_w_xq, m_w_xkv, m_w_xo, m_w_ffn_gu, m_w_ffn_down, m_ab_w_in, m_ab_b_f, m_ab_conv_w, m_ab_w_out, m_c_w_in, m_c_conv_w, m_c_conv_b, m_c_w_a, m_c_b_a, m_c_w_i, m_c_b_i, m_c_lam, m_c_w_out, v_g_mix_pre, v_g_mix_post, v_g_cross_pre, v_g_mem, v_g_cross_post, v_g_ffn_pre, v_g_ffn_post, v_w_xq, v_w_xkv, v_w_xo, v_w_ffn_gu, v_w_ffn_down, v_ab_w_in, v_ab_b_f, v_ab_conv_w, v_ab_w_out, v_c_w_in, v_c_conv_w, v_c_conv_b, v_c_w_a, v_c_b_a, v_c_w_i, v_c_b_i, v_c_lam, v_c_w_out):
    given = dict(x=x, mem=mem, g_mix_pre=g_mix_pre, g_mix_post=g_mix_post, g_cross_pre=g_cross_pre, g_mem=g_mem, g_cross_post=g_cross_post, g_ffn_pre=g_ffn_pre, g_ffn_post=g_ffn_post, w_xq=w_xq, w_xkv=w_xkv, w_xo=w_xo, w_ffn_gu=w_ffn_gu, w_ffn_down=w_ffn_down, ab_w_in=ab_w_in, ab_b_f=ab_b_f, ab_conv_w=ab_conv_w, ab_w_out=ab_w_out, c_w_in=c_w_in, c_conv_w=c_conv_w, c_conv_b=c_conv_b, c_w_a=c_w_a, c_b_a=c_b_a, c_w_i=c_w_i, c_b_i=c_b_i, c_lam=c_lam, c_w_out=c_w_out, loss_target=loss_target, m_g_mix_pre=m_g_mix_pre, m_g_mix_post=m_g_mix_post, m_g_cross_pre=m_g_cross_pre, m_g_mem=m_g_mem, m_g_cross_post=m_g_cross_post, m_g_ffn_pre=m_g_ffn_pre, m_g_ffn_post=m_g_ffn_post, m_w_xq=m_w_xq, m_w_xkv=m_w_xkv, m_w_xo=m_w_xo, m_w_ffn_gu=m_w_ffn_gu, m_w_ffn_down=m_w_ffn_down, m_ab_w_in=m_ab_w_in, m_ab_b_f=m_ab_b_f, m_ab_conv_w=m_ab_conv_w, m_ab_w_out=m_ab_w_out, m_c_w_in=m_c_w_in, m_c_conv_w=m_c_conv_w, m_c_conv_b=m_c_conv_b, m_c_w_a=m_c_w_a, m_c_b_a=m_c_b_a, m_c_w_i=m_c_w_i, m_c_b_i=m_c_b_i, m_c_lam=m_c_lam, m_c_w_out=m_c_w_out, v_g_mix_pre=v_g_mix_pre, v_g_mix_post=v_g_mix_post, v_g_cross_pre=v_g_cross_pre, v_g_mem=v_g_mem, v_g_cross_post=v_g_cross_post, v_g_ffn_pre=v_g_ffn_pre, v_g_ffn_post=v_g_ffn_post, v_w_xq=v_w_xq, v_w_xkv=v_w_xkv, v_w_xo=v_w_xo, v_w_ffn_gu=v_w_ffn_gu, v_w_ffn_down=v_w_ffn_down, v_ab_w_in=v_ab_w_in, v_ab_b_f=v_ab_b_f, v_ab_conv_w=v_ab_conv_w, v_ab_w_out=v_ab_w_out, v_c_w_in=v_c_w_in, v_c_conv_w=v_c_conv_w, v_c_conv_b=v_c_conv_b, v_c_w_a=v_c_w_a, v_c_b_a=v_c_b_a, v_c_w_i=v_c_w_i, v_c_b_i=v_c_b_i, v_c_lam=v_c_lam, v_c_w_out=v_c_w_out)
    weights = {n: given[n] for n in TWIN_WEIGHTS}
    shared = {n: given[n] for n in SHARED_INPUTS}
    per_example = {n: given[n] for n in ['x', 'mem']}
    grad_fn = _jax.value_and_grad(_loss, argnums=(0, 1))

    def one_microbatch(ex, loss_target):
        ex = dict(ex)
        diff = ex.pop(TWIN_DIFF_INPUT)
        return grad_fn(weights, diff, {**shared, **ex}, loss_target)

    if N_MICROBATCH == 1:
        loss, (grad_w, grad_x) = one_microbatch(per_example, given["loss_target"])
    else:
        def body(carry, xs):
            loss_sum, grad_sum = carry
            l_k, (gw_k, gx_k) = one_microbatch(xs[0], xs[1])
            with _jax.named_scope("update"):
                return (loss_sum + l_k, _jax.tree.map(_jnp.add, grad_sum, gw_k)), gx_k

        init = (_jnp.zeros((), _jnp.float32), _jax.tree.map(_jnp.zeros_like, weights))
        (loss, grad_w), grad_x = _jax.lax.scan(body, init, (per_example, given["loss_target"]))
    with _jax.named_scope("update"):
        delta_w, new_m, new_v = {}, {}, {}
        for n in TWIN_WEIGHTS:
            delta_w[n], new_m[n], new_v[n] = _adamw(weights[n], grad_w[n], given["m_" + n], given["v_" + n])
    return (loss, grad_x, *[grad_w[n] for n in TWIN_WEIGHTS], *[delta_w[n] for n in TWIN_WEIGHTS],
            *[new_m[n] for n in TWIN_WEIGHTS], *[new_v[n] for n in TWIN_WEIGHTS])
```

```python
import functools
import math

import jax
import jax.numpy as jnp
from jax import lax
from jax.experimental import pallas as pl
from jax.experimental.pallas import tpu as pltpu

F32, BF16 = jnp.float32, jnp.bfloat16
D_MODEL = 1024
EPS = 1e-6
NEG_INF = -1e30
FOX_HEADS, FOX_HEAD_DIM, FOX_WIDTH = 8, 64, 512
SC_WIDTH = 512
AB_IN = 3 * FOX_WIDTH + FOX_HEADS + 3 * SC_WIDTH
AB_IN_PAD = 3200
LRU_BW, LRU_BLOCKS = 256, 4
RG_C = 8.0
MEM_HEADS, MEM_HEAD_DIM = 4, 256
ADAM_LR, ADAM_B1, ADAM_B2, ADAM_EPS, ADAM_WD, ADAM_STEP = 0.001, 0.9, 0.999, 1e-08, 0.01, 10
N_CHIPS = 4
MESH = pl.DeviceIdType.MESH
VMEM_LIMIT_BYTES = 48 * 1024 * 1024

NN = (((1,), (0,)), ((), ()))
NT = (((1,), (1,)), ((), ()))
TN = (((0,), (0,)), ((), ()))


def _dot(a, b, dn=NN):
    return lax.dot_general(a.astype(BF16), b.astype(BF16), dn, preferred_element_type=F32)


def _tile(n, prefs):
    for p in prefs:
        if n % p == 0:
            return p
    return n


def _pcall(name, kern, grid, ins, in_specs, out_shape, out_specs, sem):
    return pl.pallas_call(
        kern, name=name, grid=grid, in_specs=in_specs, out_specs=out_specs, out_shape=out_shape,
        compiler_params=pltpu.CompilerParams(dimension_semantics=sem, vmem_limit_bytes=VMEM_LIMIT_BYTES),
    )(*ins)


def mm(name, a, b, mode, out_dtype):
    if mode == "nn":
        (m, k), n = a.shape, b.shape[1]
    elif mode == "nt":
        (m, k), n = a.shape, b.shape[0]
    else:
        (k, m), n = a.shape, b.shape[1]
    tm = _tile(m, (512, 256, 128))
    tn = _tile(n, (512, 640, 256, 128))
    dn = {"nn": NN, "nt": NT, "tn": TN}[mode]

    def kern(a_ref, b_ref, o_ref):
        o_ref[...] = _dot(a_ref[...], b_ref[...], dn).astype(o_ref.dtype)

    a_spec = pl.BlockSpec((k, tm), lambda i, j: (0, i)) if mode == "tn" else pl.BlockSpec((tm, k), lambda i, j: (i, 0))
    b_spec = pl.BlockSpec((tn, k), lambda i, j: (j, 0)) if mode == "nt" else pl.BlockSpec((k, tn), lambda i, j: (0, j))
    return _pcall(name, kern, (m // tm, n // tn), (a, b), [a_spec, b_spec],
                  jax.ShapeDtypeStruct((m, n), out_dtype), pl.BlockSpec((tm, tn), lambda i, j: (i, j)),
                  ("parallel", "parallel"))


def rowwise(name, body, rows, params, outs, accs=(), tr=256):
    t = rows[0].shape[0]
    tr = min(tr, t)
    nr, npar, no = len(rows), len(params), len(outs)

    def kern(*refs):
        acc_refs = refs[nr + npar + no:]
        if acc_refs:
            @pl.when(pl.program_id(0) == 0)
            def _():
                for ar in acc_refs:
                    ar[...] = jnp.zeros_like(ar)
        body(refs[:nr], refs[nr:nr + npar], refs[nr + npar:nr + npar + no], acc_refs)

    in_specs = [pl.BlockSpec((tr, x.shape[1]), lambda i: (i, 0)) for x in rows]
    in_specs += [pl.BlockSpec(p.shape, lambda i: (0, 0)) for p in params]
    out_specs = [pl.BlockSpec((tr, c), lambda i: (i, 0)) for c, _ in outs]
    out_specs += [pl.BlockSpec(s, lambda i: (0, 0)) for s in accs]
    out_shape = [jax.ShapeDtypeStruct((t, c), dt) for c, dt in outs]
    out_shape += [jax.ShapeDtypeStruct(s, F32) for s in accs]
    return _pcall(name, kern, (t // tr,), (*rows, *params), in_specs, out_shape, out_specs,
                  ("arbitrary",) if accs else ("parallel",))


def _rms_stats(x):
    r = lax.rsqrt(jnp.mean(x * x, axis=-1, keepdims=True) + EPS)
    return r, x * r


def _rms_bwd(xh, r, g, dy):
    dxh = dy * g
    dx = r * (dxh - xh * jnp.mean(dxh * xh, axis=-1, keepdims=True))
    return dx, jnp.sum(dy * xh, axis=0, keepdims=True)


def rms_pre(name, x, g):
    def body(r, p, o, a):
        _, xh = _rms_stats(r[0][...])
        o[0][...] = (xh * p[0][...]).astype(BF16)
    return rowwise(name, body, [x], [g.reshape(1, -1)], [(x.shape[1], BF16)])[0]


def post_add(name, x, y, g):
    def body(r, p, o, a):
        _, yh = _rms_stats(r[1][...])
        o[0][...] = r[0][...] + yh * p[0][...]
    return rowwise(name, body, [x, y], [g.reshape(1, -1)], [(x.shape[1], F32)])[0]


def post_bwd(name, y, dx, g):
    def body(r, p, o, a):
        rr, yh = _rms_stats(r[0][...])
        dy, dg = _rms_bwd(yh, rr, p[0][...], r[1][...])
        o[0][...] = dy.astype(BF16)
        a[0][...] += dg
    c = y.shape[1]
    return rowwise(name, body, [y, dx], [g.reshape(1, -1)], [(c, BF16)], [(1, c)])


def pre_bwd(name, x, dh, dx_res, g):
    def body(r, p, o, a):
        rr, xh = _rms_stats(r[0][...])
        dx, dg = _rms_bwd(xh, rr, p[0][...], r[1][...])
        o[0][...] = r[2][...] + dx
        a[0][...] += dg
    c = x.shape[1]
    return rowwise(name, body, [x, dh, dx_res], [g.reshape(1, -1)], [(c, F32)], [(1, c)])


def gain_bwd(name, x, dh):
    def body(r, p, o, a):
        _, xh = _rms_stats(r[0][...])
        a[0][...] += jnp.sum(r[1][...] * xh, axis=0, keepdims=True)
    return rowwise(name, body, [x, dh], [], [], [(1, x.shape[1])])[0]


def _sigmoid(z):
    return 1.0 / (1.0 + jnp.exp(-z))


def swiglu_fwd(name, gu):
    f = gu.shape[1] // 2

    def body(r, p, o, a):
        g = r[0][:, :f].astype(F32)
        u = r[0][:, f:].astype(F32)
        o[0][...] = (g * _sigmoid(g) * u).astype(BF16)
    return rowwise(name, body, [gu], [], [(f, BF16)])[0]


def swiglu_bwd(name, gu, da):
    f = gu.shape[1] // 2

    def body(r, p, o, a):
        g = r[0][:, :f].astype(F32)
        u = r[0][:, f:].astype(F32)
        d = r[1][...].astype(F32)
        sg = _sigmoid(g)
        o[0][:, :f] = (d * u * sg * (1.0 + g * (1.0 - sg))).astype(BF16)
        o[0][:, f:] = (d * g * sg).astype(BF16)
    return rowwise(name, body, [gu, da], [], [(2 * f, BF16)])[0]


def loss_head(name, y, target):
    c = y.shape[1]

    def body(r, p, o, a):
        e = r[0][...] - r[1][...]
        o[0][...] = e * (1.0 / c)
        a[0][...] += jnp.sum(e * e, axis=0, keepdims=True)
    return rowwise(name, body, [y, target], [], [(c, F32)], [(1, c)])


def adamw(name, w, g, m, v):
    c = w.shape[1]

    def body(r, p, o, a):
        wv, gv, mv, vv = (x[...] for x in r)
        m2 = ADAM_B1 * mv + (1.0 - ADAM_B1) * gv
        v2 = ADAM_B2 * vv + (1.0 - ADAM_B2) * (gv * gv)
        m_hat = m2 / (1.0 - ADAM_B1 ** ADAM_STEP)
        v_hat = v2 / (1.0 - ADAM_B2 ** ADAM_STEP)
        o[0][...] = -ADAM_LR * (m_hat / (jnp.sqrt(v_hat) + ADAM_EPS) + ADAM_WD * wv)
        o[1][...] = m2
        o[2][...] = v2
    tr = _tile(w.shape[0], (256, 128, 64, 32, 16, 8))
    return rowwise(name, body, [w, g, m, v], [], [(c, F32)] * 3, tr=tr)


def colwise(name, body, cols, params, outs, pouts=(), tc=128):
    t = cols[0][0].shape[0]
    c = params[0].shape[1] if params else cols[0][0].shape[1]
    nc, npar, no = len(cols), len(params), len(outs)

    def kern(*refs):
        body(refs[:nc], refs[nc:nc + npar], refs[nc + npar:nc + npar + no], refs[nc + npar + no:])

    in_specs = [pl.BlockSpec((t, tc), functools.partial(lambda j, off: (0, j + off), off=off)) for _, off in cols]
    in_specs += [pl.BlockSpec((p.shape[0], tc), lambda j: (0, j)) for p in params]
    out_specs = [pl.BlockSpec((t, tc), lambda j: (0, j)) for _ in outs]
    out_specs += [pl.BlockSpec((r, tc), lambda j: (0, j)) for r in pouts]
    out_shape = [jax.ShapeDtypeStruct((t, c), dt) for dt in outs]
    out_shape += [jax.ShapeDtypeStruct((r, c), F32) for r in pouts]
    return _pcall(name, kern, (c // tc,), (*[x for x, _ in cols], *params), in_specs, out_shape, out_specs,
                  ("parallel",))


def _row_index(shape):
    return lax.broadcasted_iota(jnp.int32, shape, 0)


def _shift_down(x, d, rows):
    return jnp.where(rows >= d, pltpu.roll(x, d, 0), 0.0)


def _shift_up(x, d, rows):
    t = x.shape[0]
    return jnp.where(rows < t - d, pltpu.roll(x, t - d, 0), 0.0)


def sconv_fwd(name, proj, col0, conv_w, tc=128):
    nb = SC_WIDTH // tc

    def body(cl, p, o, po):
        b, c, u = (x[...] for x in cl)
        rows = _row_index(b.shape)
        w = p[0][...]
        z = c * u
        conv = w[2:3] * z + w[1:2] * _shift_down(z, 1, rows) + w[0:1] * _shift_down(z, 2, rows)
        o[0][...] = (b * conv).astype(BF16)
    return colwise(name, body, [(proj, col0), (proj, col0 + nb), (proj, col0 + 2 * nb)], [conv_w], [BF16], tc=tc)[0]


def sconv_bwd(name, proj, col0, conv_w, dyb, dcol0, tc=128):
    nb = SC_WIDTH // tc

    def body(cl, p, o, po):
        b, c, u, dy = (x[...] for x in cl)
        rows = _row_index(b.shape)
        w = p[0][...]
        z = c * u
        z1, z2 = _shift_down(z, 1, rows), _shift_down(z, 2, rows)
        conv = w[2:3] * z + w[1:2] * z1 + w[0:1] * z2
        dconv = dy * b
        dz = w[2:3] * dconv + w[1:2] * _shift_up(dconv, 1, rows) + w[0:1] * _shift_up(dconv, 2, rows)
        o[0][...] = (dy * conv).astype(BF16)
        o[1][...] = (dz * u).astype(BF16)
        o[2][...] = (dz * c).astype(BF16)
        po[0][0:1, :] = jnp.sum(dconv * z2, axis=0, keepdims=True)
        po[0][1:2, :] = jnp.sum(dconv * z1, axis=0, keepdims=True)
        po[0][2:3, :] = jnp.sum(dconv * z, axis=0, keepdims=True)
    return colwise(name, body, [(proj, col0), (proj, col0 + nb), (proj, col0 + 2 * nb), (dyb, dcol0)], [conv_w],
                   [BF16, BF16, BF16], [3], tc=tc)


def _expm1(x):
    series = x * (1.0 + 0.5 * x * (1.0 + x * (1.0 / 3.0) * (1.0 + 0.25 * x * (1.0 + 0.2 * x))))
    return jnp.where(jnp.abs(x) < 0.05, series, jnp.exp(x) - 1.0)


def _log1p(x):
    series = x * (1.0 - x * (0.5 - x * (1.0 / 3.0 - 0.25 * x)))
    return jnp.where(jnp.abs(x) < 0.01, series, jnp.log(1.0 + x))


def _softplus_neg(lam):
    sp = jnp.maximum(-lam, 0.0) + _log1p(jnp.exp(-jnp.abs(lam)))
    return sp, -_sigmoid(-lam)


GELU_C = math.sqrt(2.0 / math.pi)


def _gelu(x):
    th = jnp.tanh(GELU_C * (x + 0.044715 * x * x * x))
    val = 0.5 * x * (1.0 + th)
    grad = 0.5 * (1.0 + th) + 0.5 * x * (1.0 - th * th) * GELU_C * (1.0 + 3.0 * 0.044715 * x * x)
    return val, grad


def rg_conv_fwd(name, gu2, conv_w, conv_b, tc=128):
    nb = D_MODEL // tc

    def body(cl, p, o, po):
        u = cl[0][...]
        rows = _row_index(u.shape)
        w = p[0][...]
        o[0][...] = (w[3:4] * u + w[2:3] * _shift_down(u, 1, rows) + w[1:2] * _shift_down(u, 2, rows)
                     + w[0:1] * _shift_down(u, 3, rows) + p[1][...])
    return colwise(name, body, [(gu2, nb)], [conv_w, conv_b], [F32], tc=tc)[0]


def rg_conv_bwd(name, gu2, duc, conv_w, tc=128):
    nb = D_MODEL // tc

    def body(cl, p, o, po):
        u, d = cl[0][...], cl[1][...]
        rows = _row_index(u.shape)
        w = p[0][...]
        o[0][...] = (w[3:4] * d + w[2:3] * _shift_up(d, 1, rows) + w[1:2] * _shift_up(d, 2, rows)
                     + w[0:1] * _shift_up(d, 3, rows)).astype(BF16)
        for k in range(4):
            uk = u if k == 3 else _shift_down(u, 3 - k, rows)
            po[0][k:k + 1, :] = jnp.sum(d * uk, axis=0, keepdims=True)
        po[1][...] = jnp.sum(d, axis=0, keepdims=True)
    return colwise(name, body, [(gu2, nb), (duc, 0)], [conv_w], [BF16], [4, 1], tc=tc)


def rg_gates_fwd(name, uc, w_a, b_a, w_i, b_i, tr=512):
    t = uc.shape[0]
    tr = min(tr, t)

    def kern(u_ref, wa_ref, ba_ref, wi_ref, bi_ref, r_ref, i_ref):
        ub = u_ref[...].astype(BF16)
        r_ref[...] = _sigmoid(_dot(ub, wa_ref[...]) + ba_ref[...])
        i_ref[...] = _sigmoid(_dot(ub, wi_ref[...]) + bi_ref[...])

    blk = pl.BlockSpec((tr, LRU_BW), lambda n, i: (i, n))
    wspec = pl.BlockSpec((None, LRU_BW, LRU_BW), lambda n, i: (n, 0, 0))
    bspec = pl.BlockSpec((1, LRU_BW), lambda n, i: (0, n))
    return _pcall(name, kern, (LRU_BLOCKS, t // tr), (uc, w_a, b_a, w_i, b_i), [blk, wspec, bspec, wspec, bspec],
                  [jax.ShapeDtypeStruct(uc.shape, F32)] * 2, [blk, blk], ("parallel", "parallel"))


def rg_gates_bwd(name, uc, dzr, dzi, duc_part, w_a, w_i):
    t = uc.shape[0]

    def kern(u_ref, dr_ref, di_ref, dp_ref, wa_ref, wi_ref, duc_ref, dwa_ref, dwi_ref):
        ub = u_ref[...].astype(BF16)
        dr, di = dr_ref[...], di_ref[...]
        dwa_ref[...] = _dot(ub, dr, TN)
        dwi_ref[...] = _dot(ub, di, TN)
        duc_ref[...] = dp_ref[...] + _dot(dr, wa_ref[...], NT) + _dot(di, wi_ref[...], NT)

    blk = pl.BlockSpec((t, LRU_BW), lambda n: (0, n))
    wspec = pl.BlockSpec((None, LRU_BW, LRU_BW), lambda n: (n, 0, 0))
    wshape = jax.ShapeDtypeStruct((LRU_BLOCKS, LRU_BW, LRU_BW), F32)
    return _pcall(name, kern, (LRU_BLOCKS,), (uc, dzr, dzi, duc_part, w_a, w_i), [blk, blk, blk, blk, wspec, wspec],
                  [jax.ShapeDtypeStruct(uc.shape, F32), wshape, wshape], [blk, wspec, wspec], ("parallel",))


def _rg_decay(r, lam):
    sp, dsp = _softplus_neg(lam)
    la = -RG_C * r * sp
    a = jnp.exp(la)
    sq = jnp.sqrt(-_expm1(2.0 * la))
    return sp, dsp, a, sq


def rg_scan_fwd(name, gu2, uc, r, i, lam, tc=128):
    def body(cl, p, o, po):
        gate, ucv, rv, iv = (x[...] for x in cl)
        t = gate.shape[0]
        rows = _row_index(gate.shape)
        _, _, a, sq = _rg_decay(rv, p[0][...])
        b = sq * (iv * ucv)
        d = 1
        while d < t:
            keep = rows >= d
            b = a * jnp.where(keep, pltpu.roll(b, d, 0), 0.0) + b
            a = a * jnp.where(keep, pltpu.roll(a, d, 0), 1.0)
            d *= 2
        o[0][...] = (_gelu(gate)[0] * b).astype(BF16)
        o[1][...] = b
    return colwise(name, body, [(gu2, 0), (uc, 0), (r, 0), (i, 0)], [lam], [BF16, F32], tc=tc)


def rg_scan_bwd(name, gu2, uc, r, i, hs, dy, lam, tc=128):
    def body(cl, p, o, po):
        gate, ucv, rv, iv, h, dyv = (x[...] for x in cl)
        t = gate.shape[0]
        rows = _row_index(gate.shape)
        sp, dsp, a, sq = _rg_decay(rv, p[0][...])
        gl, dgl = _gelu(gate)
        o[0][...] = (dyv * h * dgl).astype(BF16)
        g = dyv * gl
        am = _shift_up(a, 1, rows)
        d = 1
        while d < t:
            keep = rows < t - d
            g = am * jnp.where(keep, pltpu.roll(g, t - d, 0), 0.0) + g
            am = am * jnp.where(keep, pltpu.roll(am, t - d, 0), 0.0)
            d *= 2
        da = g * _shift_down(h, 1, rows)
        iu = iv * ucv
        d_iu = g * sq
        dla = da * a - (g * iu) * (a * a) / sq
        dzr = dla * (-RG_C * sp) * rv * (1.0 - rv)
        dzi = d_iu * ucv * iv * (1.0 - iv)
        o[1][...] = dzr.astype(BF16)
        o[2][...] = dzi.astype(BF16)
        o[3][...] = d_iu * iv
        po[0][...] = jnp.sum(dzr, axis=0, keepdims=True)
        po[1][...] = jnp.sum(dzi, axis=0, keepdims=True)
        po[2][...] = jnp.sum(dla * rv, axis=0, keepdims=True) * (-RG_C) * dsp
    return colwise(name, body, [(gu2, 0), (uc, 0), (r, 0), (i, 0), (hs, 0), (dy, 0)], [lam],
                   [BF16, BF16, BF16, F32], [1, 1, 1], tc=tc)


def _split3(x):
    hi = x.astype(BF16)
    r1 = x - hi.astype(F32)
    mid = r1.astype(BF16)
    lo = (r1 - mid.astype(F32)).astype(BF16)
    return hi, mid, lo


def _tri_dot(x, tri):
    out = None
    for piece in _split3(x):
        term = lax.dot_general(piece, tri, NN, preferred_element_type=F32)
        out = term if out is None else out + term
    return out


def fox_gates_fwd(name, z_t, b_f):
    h, t = z_t.shape
    tb = min(512, t)

    def kern(z_ref, b_ref, o_ref):
        z = z_ref[...] + b_ref[...]
        logf = jnp.minimum(z, 0.0) - _log1p(jnp.exp(-jnp.abs(z)))
        src = lax.broadcasted_iota(jnp.int32, (t, tb), 0)
        dst = lax.broadcasted_iota(jnp.int32, (t, tb), 1) + pl.program_id(0) * tb
        o_ref[...] = _tri_dot(logf, (src <= dst).astype(BF16))

    return _pcall(name, kern, (t // tb,), (z_t, b_f),
                  [pl.BlockSpec((h, t), lambda j: (0, 0)), pl.BlockSpec((h, 1), lambda j: (0, 0))],
                  jax.ShapeDtypeStruct((h, t), F32), pl.BlockSpec((h, tb), lambda j: (0, j)), ("parallel",))


def fox_gates_bwd(name, z_t, b_f, dcum_t):
    h, t = z_t.shape
    tb = min(512, t)

    def kern(z_ref, b_ref, d_ref, dz_ref, db_ref):
        @pl.when(pl.program_id(0) == 0)
        def _():
            db_ref[...] = jnp.zeros_like(db_ref)
        src = lax.broadcasted_iota(jnp.int32, (t, tb), 0)
        dst = lax.broadcasted_iota(jnp.int32, (t, tb), 1) + pl.program_id(0) * tb
        dlogf = _tri_dot(d_ref[...], (src >= dst).astype(BF16))
        z = z_ref[...] + b_ref[...]
        dz = dlogf * _sigmoid(-z)
        dz_ref[...] = dz
        db_ref[...] += jnp.sum(dz, axis=1, keepdims=True)

    return _pcall(name, kern, (t // tb,), (z_t, b_f, dcum_t),
                  [pl.BlockSpec((h, tb), lambda j: (0, j)), pl.BlockSpec((h, 1), lambda j: (0, 0)),
                   pl.BlockSpec((h, t), lambda j: (0, 0))],
                  [jax.ShapeDtypeStruct((h, t), F32), jax.ShapeDtypeStruct((h, 1), F32)],
                  [pl.BlockSpec((h, tb), lambda j: (0, j)), pl.BlockSpec((h, 1), lambda j: (0, 0))], ("arbitrary",))


def _fox_logits(q, k, cum_c, cum_r, q0):
    tq, t = q.shape[0], k.shape[0]
    s = _dot(q, k, NT) * (FOX_HEAD_DIM ** -0.5) + cum_c - cum_r
    row = lax.broadcasted_iota(jnp.int32, (tq, t), 0) + q0
    col = lax.broadcasted_iota(jnp.int32, (tq, t), 1)
    return jnp.where(row >= col, s, NEG_INF)


def fox_fwd(name, q, k, v, cum_c, cum_r, tq=256):
    h, t, dh = q.shape
    tq = min(tq, t)

    def kern(q_ref, k_ref, v_ref, cc_ref, cr_ref, o_ref, lse_ref):
        s = _fox_logits(q_ref[...], k_ref[...], cc_ref[...], cr_ref[...], pl.program_id(1) * tq)
        m = jnp.max(s, axis=-1, keepdims=True)
        p = jnp.exp(s - m)
        l = jnp.sum(p, axis=-1, keepdims=True)
        o_ref[...] = _dot(p / l, v_ref[...]).astype(o_ref.dtype)
        lse_ref[...] = m + jnp.log(l)

    qspec = pl.BlockSpec((None, tq, dh), lambda a, i: (a, i, 0))
    kspec = pl.BlockSpec((None, t, dh), lambda a, i: (a, 0, 0))
    cspec = pl.BlockSpec((None, tq, 1), lambda a, i: (a, i, 0))
    rspec = pl.BlockSpec((None, 1, t), lambda a, i: (a, 0, 0))
    return _pcall(name, kern, (h, t // tq), (q, k, v, cum_c, cum_r), [qspec, kspec, kspec, cspec, rspec],
                  [jax.ShapeDtypeStruct((h, t, dh), BF16), jax.ShapeDtypeStruct((h, t, 1), F32)],
                  [qspec, cspec], ("parallel", "parallel"))


def fox_bwd(name, q, k, v, do, lse, cum_c, cum_r, tq=256):
    h, t, dh = q.shape
    tq = min(tq, t)
    scale = FOX_HEAD_DIM ** -0.5

    def kern(q_ref, k_ref, v_ref, do_ref, lse_ref, cc_ref, cr_ref, dq_ref, dk_ref, dv_ref, dc_ref):
        @pl.when(pl.program_id(1) == 0)
        def _():
            dk_ref[...] = jnp.zeros_like(dk_ref)
            dv_ref[...] = jnp.zeros_like(dv_ref)
            dc_ref[...] = jnp.zeros_like(dc_ref)
        qv, kv, dov = q_ref[...], k_ref[...], do_ref[...]
        s = _fox_logits(qv, kv, cc_ref[...], cr_ref[...], pl.program_id(1) * tq)
        p = jnp.exp(s - lse_ref[...])
        dp = _dot(dov, v_ref[...], NT)
        ds = p * (dp - jnp.sum(dp * p, axis=-1, keepdims=True))
        dq_ref[...] = (_dot(ds, kv) * scale).astype(dq_ref.dtype)
        dk_ref[...] += _dot(ds, qv, TN) * scale
        dv_ref[...] += _dot(p, dov, TN)
        dc_ref[...] -= jnp.sum(ds, axis=0, keepdims=True)

    qspec = pl.BlockSpec((None, tq, dh), lambda a, i: (a, i, 0))
    kspec = pl.BlockSpec((None, t, dh), lambda a, i: (a, 0, 0))
    cspec = pl.BlockSpec((None, tq, 1), lambda a, i: (a, i, 0))
    rspec = pl.BlockSpec((None, 1, t), lambda a, i: (a, 0, 0))
    return _pcall(name, kern, (h, t // tq), (q, k, v, do, lse, cum_c, cum_r),
                  [qspec, kspec, kspec, qspec, cspec, cspec, rspec],
                  [jax.ShapeDtypeStruct((h, t, dh), BF16), jax.ShapeDtypeStruct((h, t, dh), F32),
                   jax.ShapeDtypeStruct((h, t, dh), F32), jax.ShapeDtypeStruct((h, 1, t), F32)],
                  [qspec, kspec, kspec, rspec], ("parallel", "arbitrary"))


def _xattn_probs(q, k):
    s = _dot(q, k, NT) * (MEM_HEAD_DIM ** -0.5)
    p = jnp.exp(s - jnp.max(s, axis=-1, keepdims=True))
    return p / jnp.sum(p, axis=-1, keepdims=True)


def xattn_fwd(name, q, kv, tq=512):
    t = q.shape[0]
    tq = min(tq, t)
    ml = kv.shape[0]

    def kern(q_ref, k_ref, v_ref, o_ref):
        o_ref[...] = _dot(_xattn_probs(q_ref[...], k_ref[...]), v_ref[...]).astype(o_ref.dtype)

    qspec = pl.BlockSpec((tq, MEM_HEAD_DIM), lambda i, a: (i, a))
    return _pcall(name, kern, (t // tq, MEM_HEADS), (q, kv, kv),
                  [qspec, pl.BlockSpec((ml, MEM_HEAD_DIM), lambda i, a: (0, a)),
                   pl.BlockSpec((ml, MEM_HEAD_DIM), lambda i, a: (0, MEM_HEADS + a))],
                  jax.ShapeDtypeStruct(q.shape, BF16), qspec, ("parallel", "parallel"))


def xattn_bwd(name, q, kv, do, tq=512):
    t = q.shape[0]
    tq = min(tq, t)
    ml = kv.shape[0]
    scale = MEM_HEAD_DIM ** -0.5

    def kern(q_ref, k_ref, v_ref, do_ref, dq_ref, dk_ref, dv_ref):
        @pl.when(pl.program_id(1) == 0)
        def _():
            dk_ref[...] = jnp.zeros_like(dk_ref)
            dv_ref[...] = jnp.zeros_like(dv_ref)
        qv, kv_, dov = q_ref[...], k_ref[...], do_ref[...]
        p = _xattn_probs(qv, kv_)
        dp = _dot(dov, v_ref[...], NT)
        ds = p * (dp - jnp.sum(dp * p, axis=-1, keepdims=True)) * scale
        dq_ref[...] = _dot(ds, kv_).astype(dq_ref.dtype)
        dk_ref[...] += _dot(ds, qv, TN)
        dv_ref[...] += _dot(p, dov, TN)

    qspec = pl.BlockSpec((tq, MEM_HEAD_DIM), lambda a, i: (i, a))
    kspec = pl.BlockSpec((ml, MEM_HEAD_DIM), lambda a, i: (0, a))
    return _pcall(name, kern, (MEM_HEADS, t // tq), (q, kv, kv, do),
                  [qspec, kspec, pl.BlockSpec((ml, MEM_HEAD_DIM), lambda a, i: (0, MEM_HEADS + a)), qspec],
                  [jax.ShapeDtypeStruct(q.shape, BF16), jax.ShapeDtypeStruct((ml, D_MODEL), F32),
                   jax.ShapeDtypeStruct((ml, D_MODEL), F32)],
                  [qspec, kspec, kspec], ("parallel", "arbitrary"))


def _heads(x):
    t = x.shape[0]
    return x.reshape(t, FOX_HEADS, FOX_HEAD_DIM).transpose(1, 0, 2)


def _unheads(x):
    return x.transpose(1, 0, 2).reshape(x.shape[1], FOX_WIDTH)


def local_step(x, mem, target, w):
    depth = w["g_mix_pre"].shape[0]
    t = x.shape[0]
    saved = []
    i1, i2, i3 = 3 * FOX_WIDTH, 3 * FOX_WIDTH + FOX_HEADS, AB_IN
    ncol = 128

    for layer in range(depth):
        s = {"x0": x}
        tag = f"l{layer}"
        h1 = rms_pre(f"{tag}_mix_pre", x, w["g_mix_pre"][layer])
        s["h1"] = h1
        if layer % 2 == 0:
            e = layer // 2
            w_in = jnp.pad(w["ab_w_in"][e], ((0, 0), (0, AB_IN_PAD - AB_IN)))
            proj = mm(f"{tag}_ab_in", h1, w_in, "nn", F32)
            qkv = proj[:, :i1].astype(BF16).reshape(t, 3, FOX_HEADS, FOX_HEAD_DIM).transpose(1, 2, 0, 3)
            z_t = proj[:, i1:i2].T
            b_f = w["ab_b_f"][e].reshape(FOX_HEADS, 1)
            cum_t = fox_gates_fwd(f"{tag}_fox_gates", z_t, b_f)
            cum_c, cum_r = cum_t[:, :, None], cum_t[:, None, :]
            oh, lse = fox_fwd(f"{tag}_fox", qkv[0], qkv[1], qkv[2], cum_c, cum_r)
            bcu = proj[:, i2:i3]
            y_b = sconv_fwd(f"{tag}_sconv", bcu, 0, w["ab_conv_w"][e])
            ycat = jnp.concatenate([_unheads(oh), y_b], axis=1)
            y1 = mm(f"{tag}_ab_out", ycat, w["ab_w_out"][e], "nn", F32)
            s.update(w_in=w_in, qkv=qkv, z_t=z_t, b_f=b_f, cum_c=cum_c, cum_r=cum_r, lse=lse, bcu=bcu, ycat=ycat)
        else:
            o = layer // 2
            gu2 = mm(f"{tag}_c_in", h1, w["c_w_in"][o], "nn", F32)
            conv_b = w["c_conv_b"][o].reshape(1, -1)
            uc = rg_conv_fwd(f"{tag}_rg_conv", gu2, w["c_conv_w"][o], conv_b)
            b_a, b_i = w["c_b_a"][o].reshape(1, -1), w["c_b_i"][o].reshape(1, -1)
            r, i = rg_gates_fwd(f"{tag}_rg_gates", uc, w["c_w_a"][o], b_a, w["c_w_i"][o], b_i)
            lam = w["c_lam"][o].reshape(1, -1)
            ymix, hs = rg_scan_fwd(f"{tag}_rg_scan", gu2, uc, r, i, lam)
            y1 = mm(f"{tag}_c_out", ymix, w["c_w_out"][o], "nn", F32)
            s.update(gu2=gu2, uc=uc, r=r, i=i, lam=lam, hs=hs, ymix=ymix)
        s["y1"] = y1
        x = post_add(f"{tag}_mix_post", x, y1, w["g_mix_post"][layer])
        s["x1"] = x
        h2 = rms_pre(f"{tag}_cross_pre", x, w["g_cross_pre"][layer])
        m = rms_pre(f"{tag}_mem_pre", mem, w["g_mem"][layer])
        q = mm(f"{tag}_xq", h2, w["w_xq"][layer], "nn", BF16)
        kv = mm(f"{tag}_xkv", m, w["w_xkv"][layer], "nn", BF16)
        o_att = xattn_fwd(f"{tag}_xattn", q, kv)
        y2 = mm(f"{tag}_xo", o_att, w["w_xo"][layer], "nn", F32)
        s.update(h2=h2, m=m, q=q, kv=kv, o_att=o_att, y2=y2)
        x = post_add(f"{tag}_cross_post", x, y2, w["g_cross_post"][layer])
        s["x2"] = x
        h3 = rms_pre(f"{tag}_ffn_pre", x, w["g_ffn_pre"][layer])
        gu = mm(f"{tag}_ffn_gu", h3, w["w_ffn_gu"][layer], "nn", BF16)
        act = swiglu_fwd(f"{tag}_swiglu", gu)
        y3 = mm(f"{tag}_ffn_down", act, w["w_ffn_down"][layer], "nn", F32)
        s.update(h3=h3, gu=gu, act=act, y3=y3)
        x = post_add(f"{tag}_ffn_post", x, y3, w["g_ffn_post"][layer])
        saved.append(s)

    dx, sq_cols = loss_head("loss_head", x, target)

    grads = {k: [None] * v.shape[0] for k, v in w.items()}
    for layer in reversed(range(depth)):
        s = saved[layer]
        tag = f"b{layer}"
        dy3, grads["g_ffn_post"][layer] = post_bwd(f"{tag}_ffn_post", s["y3"], dx, w["g_ffn_post"][layer])
        dact = mm(f"{tag}_ffn_down_dx", dy3, w["w_ffn_down"][layer], "nt", BF16)
        grads["w_ffn_down"][layer] = mm(f"{tag}_ffn_down_dw", s["act"], dy3, "tn", F32)
        dgu = swiglu_bwd(f"{tag}_swiglu", s["gu"], dact)
        dh3 = mm(f"{tag}_ffn_gu_dx", dgu, w["w_ffn_gu"][layer], "nt", F32)
        grads["w_ffn_gu"][layer] = mm(f"{tag}_ffn_gu_dw", s["h3"], dgu, "tn", F32)
        dx, grads["g_ffn_pre"][layer] = pre_bwd(f"{tag}_ffn_pre", s["x2"], dh3, dx, w["g_ffn_pre"][layer])
        dy2, grads["g_cross_post"][layer] = post_bwd(f"{tag}_cross_post", s["y2"], dx, w["g_cross_post"][layer])
        do = mm(f"{tag}_xo_dx", dy2, w["w_xo"][layer], "nt", BF16)
        grads["w_xo"][layer] = mm(f"{tag}_xo_dw", s["o_att"], dy2, "tn", F32)
        dq, dk, dv = xattn_bwd(f"{tag}_xattn", s["q"], s["kv"], do)
        dh2 = mm(f"{tag}_xq_dx", dq, w["w_xq"][layer], "nt", F32)
        grads["w_xq"][layer] = mm(f"{tag}_xq_dw", s["h2"], dq, "tn", F32)
        dkv = jnp.concatenate([dk, dv], axis=1).astype(BF16)
        dm = mm(f"{tag}_xkv_dx", dkv, w["w_xkv"][layer], "nt", F32)
        grads["w_xkv"][layer] = mm(f"{tag}_xkv_dw", s["m"], dkv, "tn", F32)
        grads["g_mem"][layer] = gain_bwd(f"{tag}_mem_pre", mem, dm)
        dx, grads["g_cross_pre"][layer] = pre_bwd(f"{tag}_cross_pre", s["x1"], dh2, dx, w["g_cross_pre"][layer])
        dy1, grads["g_mix_post"][layer] = post_bwd(f"{tag}_mix_post", s["y1"], dx, w["g_mix_post"][layer])
        if layer % 2 == 0:
            e = layer // 2
            dycat = mm(f"{tag}_ab_out_dx", dy1, w["ab_w_out"][e], "nt", F32)
            grads["ab_w_out"][e] = mm(f"{tag}_ab_out_dw", s["ycat"], dy1, "tn", F32)
            do_h = _heads(dycat[:, :FOX_WIDTH].astype(BF16))
            qkv = s["qkv"]
            dqh, dkh, dvh, dcum = fox_bwd(f"{tag}_fox", qkv[0], qkv[1], qkv[2], do_h, s["lse"], s["cum_c"], s["cum_r"])
            dz_t, db_f = fox_gates_bwd(f"{tag}_fox_gates", s["z_t"], s["b_f"], dcum.reshape(FOX_HEADS, t))
            grads["ab_b_f"][e] = db_f.reshape(FOX_HEADS)
            db, dc, du, dconv_w = sconv_bwd(f"{tag}_sconv", s["bcu"], 0, w["ab_conv_w"][e], dycat, FOX_WIDTH // ncol)
            grads["ab_conv_w"][e] = dconv_w
            dproj = jnp.concatenate(
                [_unheads(dqh), _unheads(dkh).astype(BF16), _unheads(dvh).astype(BF16), dz_t.T.astype(BF16), db, dc, du,
                 jnp.zeros((t, AB_IN_PAD - AB_IN), BF16)], axis=1)
            dh1 = mm(f"{tag}_ab_in_dx", dproj, s["w_in"], "nt", F32)
            grads["ab_w_in"][e] = mm(f"{tag}_ab_in_dw", s["h1"], dproj, "tn", F32)[:, :AB_IN]
        else:
            o = layer // 2
            dymix = mm(f"{tag}_c_out_dx", dy1, w["c_w_out"][o], "nt", F32)
            grads["c_w_out"][o] = mm(f"{tag}_c_out_dw", s["ymix"], dy1, "tn", F32)
            dgate, dzr, dzi, duc_part, db_a, db_i, dlam = rg_scan_bwd(
                f"{tag}_rg_scan", s["gu2"], s["uc"], s["r"], s["i"], s["hs"], dymix, s["lam"])
            duc, dw_a, dw_i = rg_gates_bwd(f"{tag}_rg_gates", s["uc"], dzr, dzi, duc_part, w["c_w_a"][o], w["c_w_i"][o])
            du_raw, dconv_w, dconv_b = rg_conv_bwd(f"{tag}_rg_conv", s["gu2"], duc, w["c_conv_w"][o])
            grads["c_w_a"][o], grads["c_w_i"][o] = dw_a, dw_i
            grads["c_b_a"][o] = db_a.reshape(LRU_BLOCKS, LRU_BW)
            grads["c_b_i"][o] = db_i.reshape(LRU_BLOCKS, LRU_BW)
            grads["c_lam"][o] = dlam.reshape(-1)
            grads["c_conv_w"][o] = dconv_w
            grads["c_conv_b"][o] = dconv_b.reshape(-1)
            dgu2 = jnp.concatenate([dgate, du_raw], axis=1)
            dh1 = mm(f"{tag}_c_in_dx", dgu2, w["c_w_in"][o], "nt", F32)
            grads["c_w_in"][o] = mm(f"{tag}_c_in_dw", s["h1"], dgu2, "tn", F32)
        dx, grads["g_mix_pre"][layer] = pre_bwd(f"{tag}_mix_pre", s["x0"], dh1, dx, w["g_mix_pre"][layer])

    for k in list(grads):
        if k.startswith("g_"):
            grads[k] = [g.reshape(-1) for g in grads[k]]
        grads[k] = jnp.stack(grads[k])
    return sq_cols, dx, grads


CHIP_FLIPS = ((1, 0), (0, 1), (1, 1))
HBM_SPEC = pl.BlockSpec(memory_space=pltpu.HBM)
VMEM_SPEC = pl.BlockSpec(memory_space=pltpu.VMEM)


def _place():
    return lax.axis_index("x"), lax.axis_index("y"), lax.axis_index("c")


def _flip(v, f):
    return 1 - v if f else v


def _remote(src, dst, send_sem, recv_sem, target):
    return pltpu.make_async_remote_copy(src_ref=src, dst_ref=dst, send_sem=send_sem, recv_sem=recv_sem,
                                        device_id=target, device_id_type=MESH)


def _comm_call(name, body, ins, out_shape, n_sems):
    n = len(ins)
    return pl.pallas_call(
        body, name=name, in_specs=[HBM_SPEC] * n, out_specs=[HBM_SPEC] * len(out_shape), out_shape=out_shape,
        scratch_shapes=[pltpu.SemaphoreType.DMA((n, n_sems)), pltpu.SemaphoreType.DMA((n, n_sems)),
                        pltpu.SemaphoreType.DMA((n,))],
    )(*ins)


def gather_weights(shards):
    n = len(shards)

    def body(*refs):
        ins, outs = refs[:n], refs[n:2 * n]
        send_sems, recv_sems, local_sems = refs[2 * n:]
        x, y, c = _place()
        p = 2 * x + y
        sibling = (x, y, 1 - c)
        chips = [(_flip(x, fx), _flip(y, fy)) for fx, fy in CHIP_FLIPS]
        own = [pltpu.make_async_copy(ins[a], outs[a].at[p], local_sems.at[a]) for a in range(n)]
        for cp in own:
            cp.start()
        first = [[_remote(ins[a].at[c], outs[a].at[p, c], send_sems.at[a, k], recv_sems.at[a, k], (qx, qy, c))
                  for k, (qx, qy) in enumerate(chips)] for a in range(n)]
        for a in range(n):
            for cp in first[a]:
                cp.start()
        passed = []
        for a in range(n):
            for k, (qx, qy) in enumerate(chips):
                landed = outs[a].at[2 * qx + qy, c]
                _remote(landed, landed, send_sems.at[a, k], recv_sems.at[a, k], (qx, qy, c)).wait_recv()
                cp = _remote(landed, landed, send_sems.at[a, 3 + k], recv_sems.at[a, 3 + k], sibling)
                cp.start()
                passed.append(cp)
        for a in range(n):
            for k, (qx, qy) in enumerate(chips):
                theirs = outs[a].at[2 * qx + qy, 1 - c]
                _remote(theirs, theirs, send_sems.at[a, 3 + k], recv_sems.at[a, 3 + k], sibling).wait_recv()
        for a in range(n):
            for cp in first[a]:
                cp.wait_send()
        for cp in passed:
            cp.wait_send()
        for cp in own:
            cp.wait()

    out_shape = [jax.ShapeDtypeStruct((N_CHIPS, *s.shape), s.dtype) for s in shards]
    return _comm_call("gather_weights", body, shards, out_shape, 6)


def swap_with_sibling(name, blocks):
    n = len(blocks)

    def body(*refs):
        ins, outs = refs[:n], refs[n:2 * n]
        send_sems, recv_sems, _ = refs[2 * n:]
        x, y, c = _place()
        cps = [_remote(ins[a], outs[a], send_sems.at[a, 0], recv_sems.at[a, 0], (x, y, 1 - c)) for a in range(n)]
        for cp in cps:
            cp.start()
        for cp in cps:
            cp.wait()

    return _comm_call(name, body, blocks, [jax.ShapeDtypeStruct(b.shape, b.dtype) for b in blocks], 1)


def exchange_chips(blocks):
    n = len(blocks)

    def body(*refs):
        ins, outs = refs[:n], refs[n:2 * n]
        send_sems, recv_sems, local_sems = refs[2 * n:]
        x, y, c = _place()
        p = 2 * x + y
        own = [pltpu.make_async_copy(ins[a].at[p], outs[a].at[p], local_sems.at[a]) for a in range(n)]
        for cp in own:
            cp.start()
        cps = []
        for a in range(n):
            for k, (fx, fy) in enumerate(CHIP_FLIPS):
                qx, qy = _flip(x, fx), _flip(y, fy)
                cps.append(_remote(ins[a].at[2 * qx + qy], outs[a].at[p], send_sems.at[a, k], recv_sems.at[a, k],
                                   (qx, qy, c)))
        for cp in cps:
            cp.start()
        for cp in cps:
            cp.wait()
        for cp in own:
            cp.wait()

    return _comm_call("exchange_chips", body, blocks, [jax.ShapeDtypeStruct(b.shape, b.dtype) for b in blocks], 3)


def join_halves(halves):
    n = len(halves)

    def body(*refs):
        ins, outs = refs[:n], refs[n:2 * n]
        send_sems, recv_sems, local_sems = refs[2 * n:]
        x, y, c = _place()
        own = [pltpu.make_async_copy(ins[a], outs[a].at[c], local_sems.at[a]) for a in range(n)]
        cps = [_remote(ins[a], outs[a].at[c], send_sems.at[a, 0], recv_sems.at[a, 0], (x, y, 1 - c)) for a in range(n)]
        for cp in own + cps:
            cp.start()
        for cp in cps:
            cp.wait()
        for cp in own:
            cp.wait()

    return _comm_call("join_halves", body, halves, [jax.ShapeDtypeStruct((2, *h.shape), h.dtype) for h in halves], 1)


DEVICE_FLIPS = tuple((fx, fy, fc) for fx in (0, 1) for fy in (0, 1) for fc in (0, 1))[1:]


def gather_small(name, v, reduce):
    r, cdim = v.shape
    n_dev = 8

    def body(v_ref, out_ref, *scratch):
        buf = scratch[0] if reduce else out_ref
        send_sems, recv_sems = scratch[-2:]
        x, y, c = _place()
        me = 4 * x + 2 * y + c
        buf[me] = v_ref[...]
        cps = []
        for k, (fx, fy, fc) in enumerate(DEVICE_FLIPS):
            cps.append(_remote(v_ref, buf.at[me], send_sems.at[k], recv_sems.at[k],
                               (_flip(x, fx), _flip(y, fy), _flip(c, fc))))
        for cp in cps:
            cp.start()
        for cp in cps:
            cp.wait()
        if reduce:
            total = buf[0]
            for d in range(1, n_dev):
                total = total + buf[d]
            out_ref[...] = total

    scratch = [pltpu.SemaphoreType.DMA((7,)), pltpu.SemaphoreType.DMA((7,))]
    if reduce:
        scratch = [pltpu.VMEM((n_dev, r, cdim), F32)] + scratch
    out_shape = jax.ShapeDtypeStruct((r, cdim) if reduce else (n_dev, r, cdim), F32)
    return pl.pallas_call(body, name=name, in_specs=[VMEM_SPEC], out_specs=VMEM_SPEC, out_shape=out_shape,
                          scratch_shapes=scratch)(v)


def _sum_call(name, body, ins, in_blocks, out_struct, out_block, grid):
    return _pcall(name, body, grid, ins, in_blocks, out_struct, out_block, ("parallel",))


def pair_sum(name, keep, got):
    _, hx, yd = keep.shape
    tr = _tile(hx, (256, 128, 64, 32, 16))

    def kern(a_ref, b_ref, o_ref):
        o_ref[...] = (a_ref[...] + b_ref[...].astype(F32)).astype(BF16)

    blk = pl.BlockSpec((N_CHIPS, tr, yd), lambda i: (0, i, 0))
    return _sum_call(name, kern, (keep, got), [blk, blk], jax.ShapeDtypeStruct(keep.shape, BF16), blk, (hx // tr,))


def chip_sum(name, parts):
    _, hx, yd = parts.shape
    tr = _tile(hx, (256, 128, 64, 32, 16))

    def kern(p_ref, o_ref):
        total = p_ref[0].astype(F32)
        for q in range(1, N_CHIPS):
            total = total + p_ref[q].astype(F32)
        o_ref[...] = total

    return _sum_call(name, kern, (parts,), [pl.BlockSpec((N_CHIPS, tr, yd), lambda i: (0, i, 0))],
                     jax.ShapeDtypeStruct((hx, yd), F32), pl.BlockSpec((tr, yd), lambda i: (i, 0)), (hx // tr,))


WEIGHTS = ("g_mix_pre", "g_mix_post", "g_cross_pre", "g_mem", "g_cross_post", "g_ffn_pre", "g_ffn_post", "w_xq", "w_xkv",
           "w_xo", "w_ffn_gu", "w_ffn_down", "ab_w_in", "ab_b_f", "ab_conv_w", "ab_w_out", "c_w_in", "c_conv_w",
           "c_conv_b", "c_w_a", "c_b_a", "c_w_i", "c_b_i", "c_lam", "c_w_out")
SHARD_DIM = {"w_xq": 1, "w_xkv": 2, "w_xo": 1, "w_ffn_gu": 2, "w_ffn_down": 1, "ab_w_in": 2, "ab_conv_w": 2,
             "ab_w_out": 1, "c_w_in": 2, "c_conv_w": 2, "c_conv_b": 1, "c_w_a": 2, "c_b_a": 2, "c_w_i": 2, "c_b_i": 2,
             "c_lam": 1, "c_w_out": 1}
BIG = ("w_xq", "w_xkv", "w_xo", "w_ffn_gu", "w_ffn_down", "ab_w_in", "ab_w_out", "c_w_in", "c_w_a", "c_w_i", "c_w_out")
SMALL_SHARDED = ("ab_conv_w", "c_conv_w", "c_conv_b", "c_b_a", "c_b_i", "c_lam")
REPLICATED = ("g_mix_pre", "g_mix_post", "g_cross_pre", "g_mem", "g_cross_post", "g_ffn_pre", "g_ffn_post", "ab_b_f")
PACK_COLS = 1024


def _unshard(g, d):
    shard = g.shape[1:]
    return jnp.moveaxis(g, 0, d).reshape(shard[:d] + (N_CHIPS * shard[d],) + shard[d + 1:])


def _shardify(full, d):
    s = full.shape
    return jnp.moveaxis(full.reshape(s[:d] + (N_CHIPS, s[d] // N_CHIPS) + s[d + 1:]), d, 0)


def _pack(arrays, rows):
    flat = jnp.concatenate([a.reshape(-1).astype(F32) for a in arrays])
    return jnp.pad(flat, (0, rows * PACK_COLS - flat.shape[0])).reshape(rows, PACK_COLS)


def _unpack(packed, shapes):
    flat = packed.reshape(-1)
    out, at = [], 0
    for s in shapes:
        size = math.prod(s)
        out.append(flat[at:at + size].reshape(s))
        at += size
    return out


def _rows_for(shapes):
    return -(-sum(math.prod(s) for s in shapes) // (8 * PACK_COLS)) * 8


def kernel(x, mem, g_mix_pre, g_mix_post, g_cross_pre, g_mem, g_cross_post, g_ffn_pre, g_ffn_post, w_xq, w_xkv, w_xo, w_ffn_gu, w_ffn_down, ab_w_in, ab_b_f, ab_conv_w, ab_w_out, c_w_in, c_conv_w, c_conv_b, c_w_a, c_b_a, c_w_i, c_b_i, c_lam, c_w_out, loss_target, m_g_mix_pre, m_g_mix_post, m_g_cross_pre, m_g_mem, m_g_cross_post, m_g_ffn_pre, m_g_ffn_post, m_w_xq, m_w_xkv, m_w_xo, m_w_ffn_gu, m_w_ffn_down, m_ab_w_in, m_ab_b_f, m_ab_conv_w, m_ab_w_out, m_c_w_in, m_c_conv_w, m_c_conv_b, m_c_w_a, m_c_b_a, m_c_w_i, m_c_b_i, m_c_lam, m_c_w_out, v_g_mix_pre, v_g_mix_post, v_g_cross_pre, v_g_mem, v_g_cross_post, v_g_ffn_pre, v_g_ffn_post, v_w_xq, v_w_xkv, v_w_xo, v_w_ffn_gu, v_w_ffn_down, v_ab_w_in, v_ab_b_f, v_ab_conv_w, v_ab_w_out, v_c_w_in, v_c_conv_w, v_c_conv_b, v_c_w_a, v_c_b_a, v_c_w_i, v_c_b_i, v_c_lam, v_c_w_out):
    given = dict(locals())
    w = {n: given[n] for n in WEIGHTS}
    m_in = {n: given["m_" + n] for n in WEIGHTS}
    v_in = {n: given["v_" + n] for n in WEIGHTS}
    xi, yi, ci = _place()
    chip = 2 * xi + yi

    def as_halves(a):
        return a.reshape(2, -1, a.shape[-1]) if a.ndim > 1 else a

    gathered = gather_weights([as_halves(w[n].astype(BF16)) for n in BIG])
    full = {n: _unshard(g.reshape(N_CHIPS, *w[n].shape), SHARD_DIM[n]) for n, g in zip(BIG, gathered)}
    small_shapes = [w[n].shape for n in SMALL_SHARDED]
    rows_w = _rows_for(small_shapes)
    every = gather_small("gather_small_weights", _pack([w[n] for n in SMALL_SHARDED], rows_w), reduce=False)
    per_chip = every[0::2].reshape(N_CHIPS, -1)
    at = 0
    for n, s in zip(SMALL_SHARDED, small_shapes):
        size = math.prod(s)
        full[n] = _unshard(per_chip[:, at:at + size].reshape(N_CHIPS, *s), SHARD_DIM[n])
        at += size
    for n in REPLICATED:
        full[n] = w[n]

    sq_cols, dx, grads = local_step(x[0], mem[0], loss_target[0], full)
    loss = lax.psum(0.5 / D_MODEL * jnp.sum(sq_cols), ("x", "y", "c"))

    keep, send = [], []
    for n in BIG:
        g4 = _shardify(grads[n], SHARD_DIM[n])
        g4 = g4.reshape(N_CHIPS, 2, -1, g4.shape[-1])
        keep.append(lax.dynamic_index_in_dim(g4, ci, axis=1, keepdims=False))
        send.append(lax.dynamic_index_in_dim(g4, 1 - ci, axis=1, keepdims=False).astype(BF16))
    got = swap_with_sibling("swap_grad_halves", send)
    chip_parts = [pair_sum(f"pair_sum_{n}", k, g) for n, k, g in zip(BIG, keep, got)]
    parts = exchange_chips(chip_parts)
    halves = [chip_sum(f"chip_sum_{n}", p) for n, p in zip(BIG, parts)]
    joined = join_halves(halves)
    grad_out = {n: j.reshape(w[n].shape) for n, j in zip(BIG, joined)}

    small_names = REPLICATED + SMALL_SHARDED
    small_full_shapes = [grads[n].shape for n in small_names]
    total = gather_small("reduce_small_grads", _pack([grads[n] for n in small_names], _rows_for(small_full_shapes)),
                         reduce=True)
    for n, g in zip(small_names, _unpack(total, small_full_shapes)):
        if n in SHARD_DIM:
            g = lax.dynamic_index_in_dim(_shardify(g, SHARD_DIM[n]), chip, axis=0, keepdims=False)
        grad_out[n] = g

    delta, new_m, new_v = {}, {}, {}
    for n in BIG:
        two_d = lambda a: a.reshape(-1, a.shape[-1])
        d, m2, v2 = adamw(f"adamw_{n}", two_d(w[n]), two_d(grad_out[n]), two_d(m_in[n]), two_d(v_in[n]))
        delta[n], new_m[n], new_v[n] = (a.reshape(w[n].shape) for a in (d, m2, v2))
    shapes = [w[n].shape for n in small_names]
    rows = _rows_for(shapes)
    packed = [_pack([src[n] for n in small_names], rows) for src in (w, grad_out, m_in, v_in)]
    for dst, res in zip((delta, new_m, new_v), adamw("adamw_small", *packed)):
        for n, a in zip(small_names, _unpack(res, shapes)):
            dst[n] = a

    return (loss, dx[None], *[grad_out[n] for n in WEIGHTS], *[delta[n] for n in WEIGHTS],
            *[new_m[n] for n in WEIGHTS], *[new_v[n] for n in WEIGHTS])
```

```python
import functools
import math

import jax
import jax.numpy as jnp
from jax import lax
from jax.experimental import pallas as pl
from jax.experimental.pallas import tpu as pltpu

F32, BF16 = jnp.float32, jnp.bfloat16
D_MODEL = 1024
EPS = 1e-6
NEG_INF = -1e30
FOX_HEADS, FOX_HEAD_DIM, FOX_WIDTH = 8, 64, 512
SC_WIDTH = 512
AB_IN = 3 * FOX_WIDTH + FOX_HEADS + 3 * SC_WIDTH
AB_IN_PAD = 3200
LRU_BW, LRU_BLOCKS = 256, 4
RG_C = 8.0
MEM_HEADS, MEM_HEAD_DIM = 4, 256
ADAM_LR, ADAM_B1, ADAM_B2, ADAM_EPS, ADAM_WD, ADAM_STEP = 0.001, 0.9, 0.999, 1e-08, 0.01, 10
N_CHIPS = 4
MESH = pl.DeviceIdType.MESH
VMEM_LIMIT_BYTES = 48 * 1024 * 1024

NN = (((1,), (0,)), ((), ()))
NT = (((1,), (1,)), ((), ()))
TN = (((0,), (0,)), ((), ()))


def _dot(a, b, dn=NN):
    return lax.dot_general(a.astype(BF16), b.astype(BF16), dn, preferred_element_type=F32)


def _tile(n, prefs):
    for p in prefs:
        if n % p == 0:
            return p
    return n


def _pcall(name, kern, grid, ins, in_specs, out_shape, out_specs, sem):
    return pl.pallas_call(
        kern, name=name, grid=grid, in_specs=in_specs, out_specs=out_specs, out_shape=out_shape,
        compiler_params=pltpu.CompilerParams(dimension_semantics=sem, vmem_limit_bytes=VMEM_LIMIT_BYTES),
    )(*ins)


def mm(name, a, b, mode, out_dtype):
    if mode == "nn":
        (m, k), n = a.shape, b.shape[1]
    elif mode == "nt":
        (m, k), n = a.shape, b.shape[0]
    else:
        (k, m), n = a.shape, b.shape[1]
    tm = _tile(m, (512, 256, 128))
    tn = _tile(n, (512, 640, 256, 128))
    dn = {"nn": NN, "nt": NT, "tn": TN}[mode]

    def kern(a_ref, b_ref, o_ref):
        o_ref[...] = _dot(a_ref[...], b_ref[...], dn).astype(o_ref.dtype)

    a_spec = pl.BlockSpec((k, tm), lambda i, j: (0, i)) if mode == "tn" else pl.BlockSpec((tm, k), lambda i, j: (i, 0))
    b_spec = pl.BlockSpec((tn, k), lambda i, j: (j, 0)) if mode == "nt" else pl.BlockSpec((k, tn), lambda i, j: (0, j))
    return _pcall(name, kern, (m // tm, n // tn), (a, b), [a_spec, b_spec],
                  jax.ShapeDtypeStruct((m, n), out_dtype), pl.BlockSpec((tm, tn), lambda i, j: (i, j)),
                  ("parallel", "parallel"))


def rowwise(name, body, rows, params, outs, accs=(), tr=256):
    t = rows[0].shape[0]
    tr = min(tr, t)
    nr, npar, no = len(rows), len(params), len(outs)

    def kern(*refs):
        acc_refs = refs[nr + npar + no:]
        if acc_refs:
            @pl.when(pl.program_id(0) == 0)
            def _():
                for ar in acc_refs:
                    ar[...] = jnp.zeros_like(ar)
        body(refs[:nr], refs[nr:nr + npar], refs[nr + npar:nr + npar + no], acc_refs)

    in_specs = [pl.BlockSpec((tr, x.shape[1]), lambda i: (i, 0)) for x in rows]
    in_specs += [pl.BlockSpec(p.shape, lambda i: (0, 0)) for p in params]
    out_specs = [pl.BlockSpec((tr, c), lambda i: (i, 0)) for c, _ in outs]
    out_specs += [pl.BlockSpec(s, lambda i: (0, 0)) for s in accs]
    out_shape = [jax.ShapeDtypeStruct((t, c), dt) for c, dt in outs]
    out_shape += [jax.ShapeDtypeStruct(s, F32) for s in accs]
    return _pcall(name, kern, (t // tr,), (*rows, *params), in_specs, out_shape, out_specs,
                  ("arbitrary",) if accs else ("parallel",))


def _rms_stats(x):
    r = lax.rsqrt(jnp.mean(x * x, axis=-1, keepdims=True) + EPS)
    return r, x * r


def _rms_bwd(xh, r, g, dy):
    dxh = dy * g
    dx = r * (dxh - xh * jnp.mean(dxh * xh, axis=-1, keepdims=True))
    return dx, jnp.sum(dy * xh, axis=0, keepdims=True)


def rms_pre(name, x, g):
    def body(r, p, o, a):
        _, xh = _rms_stats(r[0][...])
        o[0][...] = (xh * p[0][...]).astype(BF16)
    return rowwise(name, body, [x], [g.reshape(1, -1)], [(x.shape[1], BF16)])[0]


def post_add(name, x, y, g):
    def body(r, p, o, a):
        _, yh = _rms_stats(r[1][...])
        o[0][...] = r[0][...] + yh * p[0][...]
    return rowwise(name, body, [x, y], [g.reshape(1, -1)], [(x.shape[1], F32)])[0]


def post_bwd(name, y, dx, g):
    def body(r, p, o, a):
        rr, yh = _rms_stats(r[0][...])
        dy, dg = _rms_bwd(yh, rr, p[0][...], r[1][...])
        o[0][...] = dy.astype(BF16)
        a[0][...] += dg
    c = y.shape[1]
    return rowwise(name, body, [y, dx], [g.reshape(1, -1)], [(c, BF16)], [(1, c)])


def pre_bwd(name, x, dh, dx_res, g):
    def body(r, p, o, a):
        rr, xh = _rms_stats(r[0][...])
        dx, dg = _rms_bwd(xh, rr, p[0][...], r[1][...])
        o[0][...] = r[2][...] + dx
        a[0][...] += dg
    c = x.shape[1]
    return rowwise(name, body, [x, dh, dx_res], [g.reshape(1, -1)], [(c, F32)], [(1, c)])


def gain_bwd(name, x, dh):
    def body(r, p, o, a):
        _, xh = _rms_stats(r[0][...])
        a[0][...] += jnp.sum(r[1][...] * xh, axis=0, keepdims=True)
    return rowwise(name, body, [x, dh], [], [], [(1, x.shape[1])])[0]


def _sigmoid(z):
    return 1.0 / (1.0 + jnp.exp(-z))


def swiglu_fwd(name, gu):
    f = gu.shape[1] // 2

    def body(r, p, o, a):
        g = r[0][:, :f].astype(F32)
        u = r[0][:, f:].astype(F32)
        o[0][...] = (g * _sigmoid(g) * u).astype(BF16)
    return rowwise(name, body, [gu], [], [(f, BF16)])[0]


def swiglu_bwd(name, gu, da):
    f = gu.shape[1] // 2

    def body(r, p, o, a):
        g = r[0][:, :f].astype(F32)
        u = r[0][:, f:].astype(F32)
        d = r[1][...].astype(F32)
        sg = _sigmoid(g)
        o[0][:, :f] = (d * u * sg * (1.0 + g * (1.0 - sg))).astype(BF16)
        o[0][:, f:] = (d * g * sg).astype(BF16)
    return rowwise(name, body, [gu, da], [], [(2 * f, BF16)])[0]


def loss_head(name, y, target):
    c = y.shape[1]

    def body(r, p, o, a):
        e = r[0][...] - r[1][...]
        o[0][...] = e * (1.0 / c)
        a[0][...] += jnp.sum(e * e, axis=0, keepdims=True)
    return rowwise(name, body, [y, target], [], [(c, F32)], [(1, c)])


def adamw(name, w, g, m, v):
    c = w.shape[1]

    def body(r, p, o, a):
        wv, gv, mv, vv = (x[...] for x in r)
        m2 = ADAM_B1 * mv + (1.0 - ADAM_B1) * gv
        v2 = ADAM_B2 * vv + (1.0 - ADAM_B2) * (gv * gv)
        m_hat = m2 / (1.0 - ADAM_B1 ** ADAM_STEP)
        v_hat = v2 / (1.0 - ADAM_B2 ** ADAM_STEP)
        o[0][...] = -ADAM_LR * (m_hat / (jnp.sqrt(v_hat) + ADAM_EPS) + ADAM_WD * wv)
        o[1][...] = m2
        o[2][...] = v2
    tr = _tile(w.shape[0], (256, 128, 64, 32, 16, 8))
    return rowwise(name, body, [w, g, m, v], [], [(c, F32)] * 3, tr=tr)


def colwise(name, body, cols, params, outs, pouts=(), tc=128):
    t = cols[0][0].shape[0]
    c = params[0].shape[1] if params else cols[0][0].shape[1]
    nc, npar, no = len(cols), len(params), len(outs)

    def kern(*refs):
        body(refs[:nc], refs[nc:nc + npar], refs[nc + npar:nc + npar + no], refs[nc + npar + no:])

    in_specs = [pl.BlockSpec((t, tc), functools.partial(lambda j, off: (0, j + off), off=off)) for _, off in cols]
    in_specs += [pl.BlockSpec((p.shape[0], tc), lambda j: (0, j)) for p in params]
    out_specs = [pl.BlockSpec((t, tc), lambda j: (0, j)) for _ in outs]
    out_specs += [pl.BlockSpec((r, tc), lambda j: (0, j)) for r in pouts]
    out_shape = [jax.ShapeDtypeStruct((t, c), dt) for dt in outs]
    out_shape += [jax.ShapeDtypeStruct((r, c), F32) for r in pouts]
    return _pcall(name, kern, (c // tc,), (*[x for x, _ in cols], *params), in_specs, out_shape, out_specs,
                  ("parallel",))


def _row_index(shape):
    return lax.broadcasted_iota(jnp.int32, shape, 0)


def _shift_down(x, d, rows):
    return jnp.where(rows >= d, pltpu.roll(x, d, 0), 0.0)


def _shift_up(x, d, rows):
    t = x.shape[0]
    return jnp.where(rows < t - d, pltpu.roll(x, t - d, 0), 0.0)


def sconv_fwd(name, proj, col0, conv_w, tc=128):
    nb = SC_WIDTH // tc

    def body(cl, p, o, po):
        b, c, u = (x[...] for x in cl)
        rows = _row_index(b.shape)
        w = p[0][...]
        z = c * u
        conv = w[2:3] * z + w[1:2] * _shift_down(z, 1, rows) + w[0:1] * _shift_down(z, 2, rows)
        o[0][...] = (b * conv).astype(BF16)
    return colwise(name, body, [(proj, col0), (proj, col0 + nb), (proj, col0 + 2 * nb)], [conv_w], [BF16], tc=tc)[0]


def sconv_bwd(name, proj, col0, conv_w, dyb, dcol0, tc=128):
    nb = SC_WIDTH // tc

    def body(cl, p, o, po):
        b, c, u, dy = (x[...] for x in cl)
        rows = _row_index(b.shape)
        w = p[0][...]
        z = c * u
        z1, z2 = _shift_down(z, 1, rows), _shift_down(z, 2, rows)
        conv = w[2:3] * z + w[1:2] * z1 + w[0:1] * z2
        dconv = dy * b
        dz = w[2:3] * dconv + w[1:2] * _shift_up(dconv, 1, rows) + w[0:1] * _shift_up(dconv, 2, rows)
        o[0][...] = (dy * conv).astype(BF16)
        o[1][...] = (dz * u).astype(BF16)
        o[2][...] = (dz * c).astype(BF16)
        po[0][0:1, :] = jnp.sum(dconv * z2, axis=0, keepdims=True)
        po[0][1:2, :] = jnp.sum(dconv * z1, axis=0, keepdims=True)
        po[0][2:3, :] = jnp.sum(dconv * z, axis=0, keepdims=True)
    return colwise(name, body, [(proj, col0), (proj, col0 + nb), (proj, col0 + 2 * nb), (dyb, dcol0)], [conv_w],
                   [BF16, BF16, BF16], [3], tc=tc)


def _expm1(x):
    series = x * (1.0 + 0.5 * x * (1.0 + x * (1.0 / 3.0) * (1.0 + 0.25 * x * (1.0 + 0.2 * x))))
    return jnp.where(jnp.abs(x) < 0.05, series, jnp.exp(x) - 1.0)


def _log1p(x):
    series = x * (1.0 - x * (0.5 - x * (1.0 / 3.0 - 0.25 * x)))
    return jnp.where(jnp.abs(x) < 0.01, series, jnp.log(1.0 + x))


def _softplus_neg(lam):
    sp = jnp.maximum(-lam, 0.0) + _log1p(jnp.exp(-jnp.abs(lam)))
    return sp, -_sigmoid(-lam)


GELU_C = math.sqrt(2.0 / math.pi)


def _gelu(x):
    th = jnp.tanh(GELU_C * (x + 0.044715 * x * x * x))
    val = 0.5 * x * (1.0 + th)
    grad = 0.5 * (1.0 + th) + 0.5 * x * (1.0 - th * th) * GELU_C * (1.0 + 3.0 * 0.044715 * x * x)
    return val, grad


def rg_conv_fwd(name, gu2, conv_w, conv_b, tc=128):
    nb = D_MODEL // tc

    def body(cl, p, o, po):
        u = cl[0][...]
        rows = _row_index(u.shape)
        w = p[0][...]
        o[0][...] = (w[3:4] * u + w[2:3] * _shift_down(u, 1, rows) + w[1:2] * _shift_down(u, 2, rows)
                     + w[0:1] * _shift_down(u, 3, rows) + p[1][...])
    return colwise(name, body, [(gu2, nb)], [conv_w, conv_b], [F32], tc=tc)[0]


def rg_conv_bwd(name, gu2, duc, conv_w, tc=128):
    nb = D_MODEL // tc

    def body(cl, p, o, po):
        u, d = cl[0][...], cl[1][...]
        rows = _row_index(u.shape)
        w = p[0][...]
        o[0][...] = (w[3:4] * d + w[2:3] * _shift_up(d, 1, rows) + w[1:2] * _shift_up(d, 2, rows)
                     + w[0:1] * _shift_up(d, 3, rows)).astype(BF16)
        for k in range(4):
            uk = u if k == 3 else _shift_down(u, 3 - k, rows)
            po[0][k:k + 1, :] = jnp.sum(d * uk, axis=0, keepdims=True)
        po[1][...] = jnp.sum(d, axis=0, keepdims=True)
    return colwise(name, body, [(gu2, nb), (duc, 0)], [conv_w], [BF16], [4, 1], tc=tc)


def rg_gates_fwd(name, uc, w_a, b_a, w_i, b_i, tr=512):
    t = uc.shape[0]
    tr = min(tr, t)

    def kern(u_ref, wa_ref, ba_ref, wi_ref, bi_ref, r_ref, i_ref):
        ub = u_ref[...].astype(BF16)
        r_ref[...] = _sigmoid(_dot(ub, wa_ref[...]) + ba_ref[...])
        i_ref[...] = _sigmoid(_dot(ub, wi_ref[...]) + bi_ref[...])

    blk = pl.BlockSpec((tr, LRU_BW), lambda n, i: (i, n))
    wspec = pl.BlockSpec((None, LRU_BW, LRU_BW), lambda n, i: (n, 0, 0))
    bspec = pl.BlockSpec((1, LRU_BW), lambda n, i: (0, n))
    return _pcall(name, kern, (LRU_BLOCKS, t // tr), (uc, w_a, b_a, w_i, b_i), [blk, wspec, bspec, wspec, bspec],
                  [jax.ShapeDtypeStruct(uc.shape, F32)] * 2, [blk, blk], ("parallel", "parallel"))


def rg_gates_bwd(name, uc, dzr, dzi, duc_part, w_a, w_i):
    t = uc.shape[0]

    def kern(u_ref, dr_ref, di_ref, dp_ref, wa_ref, wi_ref, duc_ref, dwa_ref, dwi_ref):
        ub = u_ref[...].astype(BF16)
        dr, di = dr_ref[...], di_ref[...]
        dwa_ref[...] = _dot(ub, dr, TN)
        dwi_ref[...] = _dot(ub, di, TN)
        duc_ref[...] = dp_ref[...] + _dot(dr, wa_ref[...], NT) + _dot(di, wi_ref[...], NT)

    blk = pl.BlockSpec((t, LRU_BW), lambda n: (0, n))
    wspec = pl.BlockSpec((None, LRU_BW, LRU_BW), lambda n: (n, 0, 0))
    wshape = jax.ShapeDtypeStruct((LRU_BLOCKS, LRU_BW, LRU_BW), F32)
    return _pcall(name, kern, (LRU_BLOCKS,), (uc, dzr, dzi, duc_part, w_a, w_i), [blk, blk, blk, blk, wspec, wspec],
                  [jax.ShapeDtypeStruct(uc.shape, F32), wshape, wshape], [blk, wspec, wspec], ("parallel",))


def _rg_decay(r, lam):
    sp, dsp = _softplus_neg(lam)
    la = -RG_C * r * sp
    a = jnp.exp(la)
    sq = jnp.sqrt(-_expm1(2.0 * la))
    return sp, dsp, a, sq


def rg_scan_fwd(name, gu2, uc, r, i, lam, tc=128):
    def body(cl, p, o, po):
        gate, ucv, rv, iv = (x[...] for x in cl)
        t = gate.shape[0]
        rows = _row_index(gate.shape)
        _, _, a, sq = _rg_decay(rv, p[0][...])
        b = sq * (iv * ucv)
        d = 1
        while d < t:
            keep = rows >= d
            b = a * jnp.where(keep, pltpu.roll(b, d, 0), 0.0) + b
            a = a * jnp.where(keep, pltpu.roll(a, d, 0), 1.0)
            d *= 2
        o[0][...] = (_gelu(gate)[0] * b).astype(BF16)
        o[1][...] = b
    return colwise(name, body, [(gu2, 0), (uc, 0), (r, 0), (i, 0)], [lam], [BF16, F32], tc=tc)


def rg_scan_bwd(name, gu2, uc, r, i, hs, dy, lam, tc=128):
    def body(cl, p, o, po):
        gate, ucv, rv, iv, h, dyv = (x[...] for x in cl)
        t = gate.shape[0]
        rows = _row_index(gate.shape)
        sp, dsp, a, sq = _rg_decay(rv, p[0][...])
        gl, dgl = _gelu(gate)
        o[0][...] = (dyv * h * dgl).astype(BF16)
        g = dyv * gl
        am = _shift_up(a, 1, rows)
        d = 1
        while d < t:
            keep = rows < t - d
            g = am * jnp.where(keep, pltpu.roll(g, t - d, 0), 0.0) + g
            am = am * jnp.where(keep, pltpu.roll(am, t - d, 0), 0.0)
            d *= 2
        da = g * _shift_down(h, 1, rows)
        iu = iv * ucv
        d_iu = g * sq
        dla = da * a - (g * iu) * (a * a) / sq
        dzr = dla * (-RG_C * sp) * rv * (1.0 - rv)
        dzi = d_iu * ucv * iv * (1.0 - iv)
        o[1][...] = dzr.astype(BF16)
        o[2][...] = dzi.astype(BF16)
        o[3][...] = d_iu * iv
        po[0][...] = jnp.sum(dzr, axis=0, keepdims=True)
        po[1][...] = jnp.sum(dzi, axis=0, keepdims=True)
        po[2][...] = jnp.sum(dla * rv, axis=0, keepdims=True) * (-RG_C) * dsp
    return colwise(name, body, [(gu2, 0), (uc, 0), (r, 0), (i, 0), (hs, 0), (dy, 0)], [lam],
                   [BF16, BF16, BF16, F32], [1, 1, 1], tc=tc)


def _split3(x):
    hi = x.astype(BF16)
    r1 = x - hi.astype(F32)
    mid = r1.astype(BF16)
    lo = (r1 - mid.astype(F32)).astype(BF16)
    return hi, mid, lo


def _tri_dot(x, tri):
    out = None
    for piece in _split3(x):
        term = lax.dot_general(piece, tri, NN, preferred_element_type=F32)
        out = term if out is None else out + term
    return out


def fox_gates_fwd(name, z_t, b_f):
    h, t = z_t.shape
    tb = min(512, t)

    def kern(z_ref, b_ref, o_ref):
        z = z_ref[...] + b_ref[...]
        logf = jnp.minimum(z, 0.0) - _log1p(jnp.exp(-jnp.abs(z)))
        src = lax.broadcasted_iota(jnp.int32, (t, tb), 0)
        dst = lax.broadcasted_iota(jnp.int32, (t, tb), 1) + pl.program_id(0) * tb
        o_ref[...] = _tri_dot(logf, (src <= dst).astype(BF16))

    return _pcall(name, kern, (t // tb,), (z_t, b_f),
                  [pl.BlockSpec((h, t), lambda j: (0, 0)), pl.BlockSpec((h, 1), lambda j: (0, 0))],
                  jax.ShapeDtypeStruct((h, t), F32), pl.BlockSpec((h, tb), lambda j: (0, j)), ("parallel",))


def fox_gates_bwd(name, z_t, b_f, dcum_t):
    h, t = z_t.shape
    tb = min(512, t)

    def kern(z_ref, b_ref, d_ref, dz_ref, db_ref):
        @pl.when(pl.program_id(0) == 0)
        def _():
            db_ref[...] = jnp.zeros_like(db_ref)
        src = lax.broadcasted_iota(jnp.int32, (t, tb), 0)
        dst = lax.broadcasted_iota(jnp.int32, (t, tb), 1) + pl.program_id(0) * tb
        dlogf = _tri_dot(d_ref[...], (src >= dst).astype(BF16))
        z = z_ref[...] + b_ref[...]
        dz = dlogf * _sigmoid(-z)
        dz_ref[...] = dz
        db_ref[...] += jnp.sum(dz, axis=1, keepdims=True)

    return _pcall(name, kern, (t // tb,), (z_t, b_f, dcum_t),
                  [pl.BlockSpec((h, tb), lambda j: (0, j)), pl.BlockSpec((h, 1), lambda j: (0, 0)),
                   pl.BlockSpec((h, t), lambda j: (0, 0))],
                  [jax.ShapeDtypeStruct((h, t), F32), jax.ShapeDtypeStruct((h, 1), F32)],
                  [pl.BlockSpec((h, tb), lambda j: (0, j)), pl.BlockSpec((h, 1), lambda j: (0, 0))], ("arbitrary",))


def _fox_logits(q, k, cum_c, cum_r, q0):
    tq, t = q.shape[0], k.shape[0]
    s = _dot(q, k, NT) * (FOX_HEAD_DIM ** -0.5) + cum_c - cum_r
    row = lax.broadcasted_iota(jnp.int32, (tq, t), 0) + q0
    col = lax.broadcasted_iota(jnp.int32, (tq, t), 1)
    return jnp.where(row >= col, s, NEG_INF)


def fox_fwd(name, q, k, v, cum_c, cum_r, tq=256):
    h, t, dh = q.shape
    tq = min(tq, t)

    def kern(q_ref, k_ref, v_ref, cc_ref, cr_ref, o_ref, lse_ref):
        s = _fox_logits(q_ref[...], k_ref[...], cc_ref[...], cr_ref[...], pl.program_id(1) * tq)
        m = jnp.max(s, axis=-1, keepdims=True)
        p = jnp.exp(s - m)
        l = jnp.sum(p, axis=-1, keepdims=True)
        o_ref[...] = _dot(p / l, v_ref[...]).astype(o_ref.dtype)
        lse_ref[...] = m + jnp.log(l)

    qspec = pl.BlockSpec((None, tq, dh), lambda a, i: (a, i, 0))
    kspec = pl.BlockSpec((None, t, dh), lambda a, i: (a, 0, 0))
    cspec = pl.BlockSpec((None, tq, 1), lambda a, i: (a, i, 0))
    rspec = pl.BlockSpec((None, 1, t), lambda a, i: (a, 0, 0))
    return _pcall(name, kern, (h, t // tq), (q, k, v, cum_c, cum_r), [qspec, kspec, kspec, cspec, rspec],
                  [jax.ShapeDtypeStruct((h, t, dh), BF16), jax.ShapeDtypeStruct((h, t, 1), F32)],
                  [qspec, cspec], ("parallel", "parallel"))


def fox_bwd(name, q, k, v, do, lse, cum_c, cum_r, tq=256):
    h, t, dh = q.shape
    tq = min(tq, t)
    scale = FOX_HEAD_DIM ** -0.5

    def kern(q_ref, k_ref, v_ref, do_ref, lse_ref, cc_ref, cr_ref, dq_ref, dk_ref, dv_ref, dc_ref):
        @pl.when(pl.program_id(1) == 0)
        def _():
            dk_ref[...] = jnp.zeros_like(dk_ref)
            dv_ref[...] = jnp.zeros_like(dv_ref)
            dc_ref[...] = jnp.zeros_like(dc_ref)
        qv, kv, dov = q_ref[...], k_ref[...], do_ref[...]
        s = _fox_logits(qv, kv, cc_ref[...], cr_ref[...], pl.program_id(1) * tq)
        p = jnp.exp(s - lse_ref[...])
        dp = _dot(dov, v_ref[...], NT)
        ds = p * (dp - jnp.sum(dp * p, axis=-1, keepdims=True))
        dq_ref[...] = (_dot(ds, kv) * scale).astype(dq_ref.dtype)
        dk_ref[...] += _dot(ds, qv, TN) * scale
        dv_ref[...] += _dot(p, dov, TN)
        dc_ref[...] -= jnp.sum(ds, axis=0, keepdims=True)

    qspec = pl.BlockSpec((None, tq, dh), lambda a, i: (a, i, 0))
    kspec = pl.BlockSpec((None, t, dh), lambda a, i: (a, 0, 0))
    cspec = pl.BlockSpec((None, tq, 1), lambda a, i: (a, i, 0))
    rspec = pl.BlockSpec((None, 1, t), lambda a, i: (a, 0, 0))
    return _pcall(name, kern, (h, t // tq), (q, k, v, do, lse, cum_c, cum_r),
                  [qspec, kspec, kspec, qspec, cspec, cspec, rspec],
                  [jax.ShapeDtypeStruct((h, t, dh), BF16), jax.ShapeDtypeStruct((h, t, dh), F32),
                   jax.ShapeDtypeStruct((h, t, dh), F32), jax.ShapeDtypeStruct((h, 1, t), F32)],
                  [qspec, kspec, kspec, rspec], ("parallel", "arbitrary"))


def _xattn_probs(q, k):
    s = _dot(q, k, NT) * (MEM_HEAD_DIM ** -0.5)
    p = jnp.exp(s - jnp.max(s, axis=-1, keepdims=True))
    return p / jnp.sum(p, axis=-1, keepdims=True)


def xattn_fwd(name, q, kv, tq=512):
    t = q.shape[0]
    tq = min(tq, t)
    ml = kv.shape[0]

    def kern(q_ref, k_ref, v_ref, o_ref):
        o_ref[...] = _dot(_xattn_probs(q_ref[...], k_ref[...]), v_ref[...]).astype(o_ref.dtype)

    qspec = pl.BlockSpec((tq, MEM_HEAD_DIM), lambda i, a: (i, a))
    return _pcall(name, kern, (t // tq, MEM_HEADS), (q, kv, kv),
                  [qspec, pl.BlockSpec((ml, MEM_HEAD_DIM), lambda i, a: (0, a)),
                   pl.BlockSpec((ml, MEM_HEAD_DIM), lambda i, a: (0, MEM_HEADS + a))],
                  jax.ShapeDtypeStruct(q.shape, BF16), qspec, ("parallel", "parallel"))


def xattn_bwd(name, q, kv, do, tq=512):
    t = q.shape[0]
    tq = min(tq, t)
    ml = kv.shape[0]
    scale = MEM_HEAD_DIM ** -0.5

    def kern(q_ref, k_ref, v_ref, do_ref, dq_ref, dk_ref, dv_ref):
        @pl.when(pl.program_id(1) == 0)
        def _():
            dk_ref[...] = jnp.zeros_like(dk_ref)
            dv_ref[...] = jnp.zeros_like(dv_ref)
        qv, kv_, dov = q_ref[...], k_ref[...], do_ref[...]
        p = _xattn_probs(qv, kv_)
        dp = _dot(dov, v_ref[...], NT)
        ds = p * (dp - jnp.sum(dp * p, axis=-1, keepdims=True)) * scale
        dq_ref[...] = _dot(ds, kv_).astype(dq_ref.dtype)
        dk_ref[...] += _dot(ds, qv, TN)
        dv_ref[...] += _dot(p, dov, TN)

    qspec = pl.BlockSpec((tq, MEM_HEAD_DIM), lambda a, i: (i, a))
    kspec = pl.BlockSpec((ml, MEM_HEAD_DIM), lambda a, i: (0, a))
    return _pcall(name, kern, (MEM_HEADS, t // tq), (q, kv, kv, do),
                  [qspec, kspec, pl.BlockSpec((ml, MEM_HEAD_DIM), lambda a, i: (0, MEM_HEADS + a)), qspec],
                  [jax.ShapeDtypeStruct(q.shape, BF16), jax.ShapeDtypeStruct((ml, D_MODEL), F32),
                   jax.ShapeDtypeStruct((ml, D_MODEL), F32)],
                  [qspec, kspec, kspec], ("parallel", "arbitrary"))


def _heads(x):
    t = x.shape[0]
    return x.reshape(t, FOX_HEADS, FOX_HEAD_DIM).transpose(1, 0, 2)


def _unheads(x):
    return x.transpose(1, 0, 2).reshape(x.shape[1], FOX_WIDTH)


def local_step(x, mem, target, w):
    depth = w["g_mix_pre"].shape[0]
    t = x.shape[0]
    saved = []
    i1, i2, i3 = 3 * FOX_WIDTH, 3 * FOX_WIDTH + FOX_HEADS, AB_IN
    ncol = 128

    for layer in range(depth):
        s = {"x0": x}
        tag = f"l{layer}"
        h1 = rms_pre(f"{tag}_mix_pre", x, w["g_mix_pre"][layer])
        s["h1"] = h1
        if layer % 2 == 0:
            e = layer // 2
            w_in = jnp.pad(w["ab_w_in"][e], ((0, 0), (0, AB_IN_PAD - AB_IN)))
            proj = mm(f"{tag}_ab_in", h1, w_in, "nn", F32)
            qkv = proj[:, :i1].astype(BF16).reshape(t, 3, FOX_HEADS, FOX_HEAD_DIM).transpose(1, 2, 0, 3)
            z_t = proj[:, i1:i2].T
            b_f = w["ab_b_f"][e].reshape(FOX_HEADS, 1)
            cum_t = fox_gates_fwd(f"{tag}_fox_gates", z_t, b_f)
            cum_c, cum_r = cum_t[:, :, None], cum_t[:, None, :]
            oh, lse = fox_fwd(f"{tag}_fox", qkv[0], qkv[1], qkv[2], cum_c, cum_r)
            bcu = proj[:, i2:i3]
            y_b = sconv_fwd(f"{tag}_sconv", bcu, 0, w["ab_conv_w"][e])
            ycat = jnp.concatenate([_unheads(oh), y_b], axis=1)
            y1 = mm(f"{tag}_ab_out", ycat, w["ab_w_out"][e], "nn", F32)
            s.update(w_in=w_in, qkv=qkv, z_t=z_t, b_f=b_f, cum_c=cum_c, cum_r=cum_r, lse=lse, bcu=bcu, ycat=ycat)
        else:
            o = layer // 2
            gu2 = mm(f"{tag}_c_in", h1, w["c_w_in"][o], "nn", F32)
            conv_b = w["c_conv_b"][o].reshape(1, -1)
            uc = rg_conv_fwd(f"{tag}_rg_conv", gu2, w["c_conv_w"][o], conv_b)
            b_a, b_i = w["c_b_a"][o].reshape(1, -1), w["c_b_i"][o].reshape(1, -1)
            r, i = rg_gates_fwd(f"{tag}_rg_gates", uc, w["c_w_a"][o], b_a, w["c_w_i"][o], b_i)
            lam = w["c_lam"][o].reshape(1, -1)
            ymix, hs = rg_scan_fwd(f"{tag}_rg_scan", gu2, uc, r, i, lam)
            y1 = mm(f"{tag}_c_out", ymix, w["c_w_out"][o], "nn", F32)
            s.update(gu2=gu2, uc=uc, r=r, i=i, lam=lam, hs=hs, ymix=ymix)
        s["y1"] = y1
        x = post_add(f"{tag}_mix_post", x, y1, w["g_mix_post"][layer])
        s["x1"] = x
        h2 = rms_pre(f"{tag}_cross_pre", x, w["g_cross_pre"][layer])
        m = rms_pre(f"{tag}_mem_pre", mem, w["g_mem"][layer])
        q = mm(f"{tag}_xq", h2, w["w_xq"][layer], "nn", BF16)
        kv = mm(f"{tag}_xkv", m, w["w_xkv"][layer], "nn", BF16)
        o_att = xattn_fwd(f"{tag}_xattn", q, kv)
        y2 = mm(f"{tag}_xo", o_att, w["w_xo"][layer], "nn", F32)
        s.update(h2=h2, m=m, q=q, kv=kv, o_att=o_att, y2=y2)
        x = post_add(f"{tag}_cross_post", x, y2, w["g_cross_post"][layer])
        s["x2"] = x
        h3 = rms_pre(f"{tag}_ffn_pre", x, w["g_ffn_pre"][layer])
        gu = mm(f"{tag}_ffn_gu", h3, w["w_ffn_gu"][layer], "nn", BF16)
        act = swiglu_fwd(f"{tag}_swiglu", gu)
        y3 = mm(f"{tag}_ffn_down", act, w["w_ffn_down"][layer], "nn", F32)
        s.update(h3=h3, gu=gu, act=act, y3=y3)
        x = post_add(f"{tag}_ffn_post", x, y3, w["g_ffn_post"][layer])
        saved.append(s)

    dx, sq_cols = loss_head("loss_head", x, target)

    grads = {k: [None] * v.shape[0] for k, v in w.items()}
    for layer in reversed(range(depth)):
        s = saved[layer]
        tag = f"b{layer}"
        dy3, grads["g_ffn_post"][layer] = post_bwd(f"{tag}_ffn_post", s["y3"], dx, w["g_ffn_post"][layer])
        dact = mm(f"{tag}_ffn_down_dx", dy3, w["w_ffn_down"][layer], "nt", BF16)
        grads["w_ffn_down"][layer] = mm(f"{tag}_ffn_down_dw", s["act"], dy3, "tn", F32)
        dgu = swiglu_bwd(f"{tag}_swiglu", s["gu"], dact)
        dh3 = mm(f"{tag}_ffn_gu_dx", dgu, w["w_ffn_gu"][layer], "nt", F32)
        grads["w_ffn_gu"][layer] = mm(f"{tag}_ffn_gu_dw", s["h3"], dgu, "tn", F32)
        dx, grads["g_ffn_pre"][layer] = pre_bwd(f"{tag}_ffn_pre", s["x2"], dh3, dx, w["g_ffn_pre"][layer])
        dy2, grads["g_cross_post"][layer] = post_bwd(f"{tag}_cross_post", s["y2"], dx, w["g_cross_post"][layer])
        do = mm(f"{tag}_xo_dx", dy2, w["w_xo"][layer], "nt", BF16)
        grads["w_xo"][layer] = mm(f"{tag}_xo_dw", s["o_att"], dy2, "tn", F32)
        dq, dk, dv = xattn_bwd(f"{tag}_xattn", s["q"], s["kv"], do)
        dh2 = mm(f"{tag}_xq_dx", dq, w["w_xq"][layer], "nt", F32)
        grads["w_xq"][layer] = mm(f"{tag}_xq_dw", s["h2"], dq, "tn", F32)
        dkv = jnp.concatenate([dk, dv], axis=1).astype(BF16)
        dm = mm(f"{tag}_xkv_dx", dkv, w["w_xkv"][layer], "nt", F32)
        grads["w_xkv"][layer] = mm(f"{tag}_xkv_dw", s["m"], dkv, "tn", F32)
        grads["g_mem"][layer] = gain_bwd(f"{tag}_mem_pre", mem, dm)
        dx, grads["g_cross_pre"][layer] = pre_bwd(f"{tag}_cross_pre", s["x1"], dh2, dx, w["g_cross_pre"][layer])
        dy1, grads["g_mix_post"][layer] = post_bwd(f"{tag}_mix_post", s["y1"], dx, w["g_mix_post"][layer])
        if layer % 2 == 0:
            e = layer // 2
            dycat = mm(f"{tag}_ab_out_dx", dy1, w["ab_w_out"][e], "nt", F32)
            grads["ab_w_out"][e] = mm(f"{tag}_ab_out_dw", s["ycat"], dy1, "tn", F32)
            do_h = _heads(dycat[:, :FOX_WIDTH].astype(BF16))
            qkv = s["qkv"]
            dqh, dkh, dvh, dcum = fox_bwd(f"{tag}_fox", qkv[0], qkv[1], qkv[2], do_h, s["lse"], s["cum_c"], s["cum_r"])
            dz_t, db_f = fox_gates_bwd(f"{tag}_fox_gates", s["z_t"], s["b_f"], dcum.reshape(FOX_HEADS, t))
            grads["ab_b_f"][e] = db_f.reshape(FOX_HEADS)
            db, dc, du, dconv_w = sconv_bwd(f"{tag}_sconv", s["bcu"], 0, w["ab_conv_w"][e], dycat, FOX_WIDTH // ncol)
            grads["ab_conv_w"][e] = dconv_w
            dproj = jnp.concatenate(
                [_unheads(dqh), _unheads(dkh).astype(BF16), _unheads(dvh).astype(BF16), dz_t.T.astype(BF16), db, dc, du,
                 jnp.zeros((t, AB_IN_PAD - AB_IN), BF16)], axis=1)
            dh1 = mm(f"{tag}_ab_in_dx", dproj, s["w_in"], "nt", F32)
            grads["ab_w_in"][e] = mm(f"{tag}_ab_in_dw", s["h1"], dproj, "tn", F32)[:, :AB_IN]
        else:
            o = layer // 2
            dymix = mm(f"{tag}_c_out_dx", dy1, w["c_w_out"][o], "nt", F32)
            grads["c_w_out"][o] = mm(f"{tag}_c_out_dw", s["ymix"], dy1, "tn", F32)
            dgate, dzr, dzi, duc_part, db_a, db_i, dlam = rg_scan_bwd(
                f"{tag}_rg_scan", s["gu2"], s["uc"], s["r"], s["i"], s["hs"], dymix, s["lam"])
            duc, dw_a, dw_i = rg_gates_bwd(f"{tag}_rg_gates", s["uc"], dzr, dzi, duc_part, w["c_w_a"][o], w["c_w_i"][o])
            du_raw, dconv_w, dconv_b = rg_conv_bwd(f"{tag}_rg_conv", s["gu2"], duc, w["c_conv_w"][o])
            grads["c_w_a"][o], grads["c_w_i"][o] = dw_a, dw_i
            grads["c_b_a"][o] = db_a.reshape(LRU_BLOCKS, LRU_BW)
            grads["c_b_i"][o] = db_i.reshape(LRU_BLOCKS, LRU_BW)
            grads["c_lam"][o] = dlam.reshape(-1)
            grads["c_conv_w"][o] = dconv_w
            grads["c_conv_b"][o] = dconv_b.reshape(-1)
            dgu2 = jnp.concatenate([dgate, du_raw], axis=1)
            dh1 = mm(f"{tag}_c_in_dx", dgu2, w["c_w_in"][o], "nt", F32)
            grads["c_w_in"][o] = mm(f"{tag}_c_in_dw", s["h1"], dgu2, "tn", F32)
        dx, grads["g_mix_pre"][layer] = pre_bwd(f"{tag}_mix_pre", s["x0"], dh1, dx, w["g_mix_pre"][layer])

    for k in list(grads):
        if k.startswith("g_"):
            grads[k] = [g.reshape(-1) for g in grads[k]]
        grads[k] = jnp.stack(grads[k])
    return sq_cols, dx, grads


CHIP_FLIPS = ((1, 0), (0, 1), (1, 1))
HBM_SPEC = pl.BlockSpec(memory_space=pltpu.HBM)
VMEM_SPEC = pl.BlockSpec(memory_space=pltpu.VMEM)


def _place():
    return lax.axis_index("x"), lax.axis_index("y"), lax.axis_index("c")


def _flip(v, f):
    return 1 - v if f else v


def _remote(src, dst, send_sem, recv_sem, target):
    return pltpu.make_async_remote_copy(src_ref=src, dst_ref=dst, send_sem=send_sem, recv_sem=recv_sem,
                                        device_id=target, device_id_type=MESH)


def _comm_call(name, body, ins, out_shape, n_sems):
    n = len(ins)
    return pl.pallas_call(
        body, name=name, in_specs=[HBM_SPEC] * n, out_specs=[HBM_SPEC] * len(out_shape), out_shape=out_shape,
        scratch_shapes=[pltpu.SemaphoreType.DMA((n, n_sems)), pltpu.SemaphoreType.DMA((n, n_sems))],
    )(*ins)


def gather_weights(shards):
    n = len(shards)

    def body(*refs):
        ins, outs = refs[:n], refs[n:2 * n]
        send_sems, recv_sems = refs[2 * n:]
        x, y, c = _place()
        p = 2 * x + y
        sibling = (x, y, 1 - c)
        chips = [(_flip(x, fx), _flip(y, fy)) for fx, fy in CHIP_FLIPS]
        first = [[_remote(ins[a].at[c], outs[a].at[p, c], send_sems.at[a, k], recv_sems.at[a, k], (qx, qy, c))
                  for k, (qx, qy) in enumerate(chips)] for a in range(n)]
        for a in range(n):
            for cp in first[a]:
                cp.start()
        passed = []
        for a in range(n):
            for k, (qx, qy) in enumerate(chips):
                landed = outs[a].at[2 * qx + qy, c]
                _remote(landed, landed, send_sems.at[a, k], recv_sems.at[a, k], (qx, qy, c)).wait_recv()
                cp = _remote(landed, landed, send_sems.at[a, 3 + k], recv_sems.at[a, 3 + k], sibling)
                cp.start()
                passed.append(cp)
        for a in range(n):
            for k, (qx, qy) in enumerate(chips):
                theirs = outs[a].at[2 * qx + qy, 1 - c]
                _remote(theirs, theirs, send_sems.at[a, 3 + k], recv_sems.at[a, 3 + k], sibling).wait_recv()
        for a in range(n):
            for cp in first[a]:
                cp.wait_send()
        for cp in passed:
            cp.wait_send()

    out_shape = [jax.ShapeDtypeStruct((N_CHIPS, *s.shape), s.dtype) for s in shards]
    return _comm_call("gather_weights", body, shards, out_shape, 6)


def swap_with_sibling(name, blocks):
    n = len(blocks)

    def body(*refs):
        ins, outs = refs[:n], refs[n:2 * n]
        send_sems, recv_sems = refs[2 * n:]
        x, y, c = _place()
        cps = [_remote(ins[a], outs[a], send_sems.at[a, 0], recv_sems.at[a, 0], (x, y, 1 - c)) for a in range(n)]
        for cp in cps:
            cp.start()
        for cp in cps:
            cp.wait()

    return _comm_call(name, body, blocks, [jax.ShapeDtypeStruct(b.shape, b.dtype) for b in blocks], 1)


def exchange_chips(blocks):
    n = len(blocks)

    def body(*refs):
        ins, outs = refs[:n], refs[n:2 * n]
        send_sems, recv_sems = refs[2 * n:]
        x, y, c = _place()
        p = 2 * x + y
        cps = []
        for a in range(n):
            for k, (fx, fy) in enumerate(CHIP_FLIPS):
                qx, qy = _flip(x, fx), _flip(y, fy)
                cps.append(_remote(ins[a].at[2 * qx + qy], outs[a].at[p], send_sems.at[a, k], recv_sems.at[a, k],
                                   (qx, qy, c)))
        for cp in cps:
            cp.start()
        for cp in cps:
            cp.wait()

    return _comm_call("exchange_chips", body, blocks, [jax.ShapeDtypeStruct(b.shape, b.dtype) for b in blocks], 3)


DEVICE_FLIPS = tuple((fx, fy, fc) for fx in (0, 1) for fy in (0, 1) for fc in (0, 1))[1:]


def gather_small(name, v, reduce):
    r, cdim = v.shape
    n_dev = 8

    def body(v_ref, out_ref, *scratch):
        buf = scratch[0] if reduce else out_ref
        send_sems, recv_sems = scratch[-2:]
        x, y, c = _place()
        me = 4 * x + 2 * y + c
        buf[me] = v_ref[...]
        cps = []
        for k, (fx, fy, fc) in enumerate(DEVICE_FLIPS):
            cps.append(_remote(v_ref, buf.at[me], send_sems.at[k], recv_sems.at[k],
                               (_flip(x, fx), _flip(y, fy), _flip(c, fc))))
        for cp in cps:
            cp.start()
        for cp in cps:
            cp.wait()
        if reduce:
            total = buf[0]
            for d in range(1, n_dev):
                total = total + buf[d]
            out_ref[...] = total

    scratch = [pltpu.SemaphoreType.DMA((7,)), pltpu.SemaphoreType.DMA((7,))]
    if reduce:
        scratch = [pltpu.VMEM((n_dev, r, cdim), F32)] + scratch
    out_shape = jax.ShapeDtypeStruct((r, cdim) if reduce else (n_dev, r, cdim), F32)
    return pl.pallas_call(body, name=name, in_specs=[VMEM_SPEC], out_specs=VMEM_SPEC, out_shape=out_shape,
                          scratch_shapes=scratch)(v)


def _sum_call(name, body, ins, in_blocks, out_struct, out_block, grid):
    return _pcall(name, body, grid, ins, in_blocks, out_struct, out_block, ("parallel",))


def pair_sum(name, keep, got):
    _, hx, yd = keep.shape
    tr = _tile(hx, (256, 128, 64, 32, 16))

    def kern(a_ref, b_ref, o_ref):
        o_ref[...] = (a_ref[...] + b_ref[...].astype(F32)).astype(BF16)

    blk = pl.BlockSpec((N_CHIPS, tr, yd), lambda i: (0, i, 0))
    return _sum_call(name, kern, (keep, got), [blk, blk], jax.ShapeDtypeStruct(keep.shape, BF16), blk, (hx // tr,))


def chip_sum(name, mine, parts, chip):
    _, hx, yd = parts.shape
    tr = _tile(hx, (256, 128, 64, 32, 16))

    def kern(chip_ref, m_ref, p_ref, o_ref):
        total = None
        for q in range(N_CHIPS):
            term = jnp.where(chip_ref[0] == q, m_ref[...], p_ref[q]).astype(F32)
            total = term if total is None else total + term
        o_ref[...] = total

    grid_spec = pltpu.PrefetchScalarGridSpec(
        num_scalar_prefetch=1, grid=(hx // tr,),
        in_specs=[pl.BlockSpec((None, tr, yd), lambda i, cr: (cr[0], i, 0)),
                  pl.BlockSpec((N_CHIPS, tr, yd), lambda i, cr: (0, i, 0))],
        out_specs=pl.BlockSpec((tr, yd), lambda i, cr: (i, 0)))
    return pl.pallas_call(
        kern, name=name, grid_spec=grid_spec, out_shape=jax.ShapeDtypeStruct((hx, yd), F32),
        compiler_params=pltpu.CompilerParams(dimension_semantics=("parallel",), vmem_limit_bytes=VMEM_LIMIT_BYTES),
    )(chip, mine, parts)


WEIGHTS = ("g_mix_pre", "g_mix_post", "g_cross_pre", "g_mem", "g_cross_post", "g_ffn_pre", "g_ffn_post", "w_xq", "w_xkv",
           "w_xo", "w_ffn_gu", "w_ffn_down", "ab_w_in", "ab_b_f", "ab_conv_w", "ab_w_out", "c_w_in", "c_conv_w",
           "c_conv_b", "c_w_a", "c_b_a", "c_w_i", "c_b_i", "c_lam", "c_w_out")
SHARD_DIM = {"w_xq": 1, "w_xkv": 2, "w_xo": 1, "w_ffn_gu": 2, "w_ffn_down": 1, "ab_w_in": 2, "ab_conv_w": 2,
             "ab_w_out": 1, "c_w_in": 2, "c_conv_w": 2, "c_conv_b": 1, "c_w_a": 2, "c_b_a": 2, "c_w_i": 2, "c_b_i": 2,
             "c_lam": 1, "c_w_out": 1}
BIG = ("w_xq", "w_xkv", "w_xo", "w_ffn_gu", "w_ffn_down", "ab_w_in", "ab_w_out", "c_w_in", "c_w_a", "c_w_i", "c_w_out")
SMALL_SHARDED = ("ab_conv_w", "c_conv_w", "c_conv_b", "c_b_a", "c_b_i", "c_lam")
REPLICATED = ("g_mix_pre", "g_mix_post", "g_cross_pre", "g_mem", "g_cross_post", "g_ffn_pre", "g_ffn_post", "ab_b_f")
PACK_COLS = 1024


def _unshard(g, d):
    shard = g.shape[1:]
    return jnp.moveaxis(g, 0, d).reshape(shard[:d] + (N_CHIPS * shard[d],) + shard[d + 1:])


def _shardify(full, d):
    s = full.shape
    return jnp.moveaxis(full.reshape(s[:d] + (N_CHIPS, s[d] // N_CHIPS) + s[d + 1:]), d, 0)


def _pack(arrays, rows):
    flat = jnp.concatenate([a.reshape(-1).astype(F32) for a in arrays])
    return jnp.pad(flat, (0, rows * PACK_COLS - flat.shape[0])).reshape(rows, PACK_COLS)


def _unpack(packed, shapes):
    flat = packed.reshape(-1)
    out, at = [], 0
    for s in shapes:
        size = math.prod(s)
        out.append(flat[at:at + size].reshape(s))
        at += size
    return out


def _rows_for(shapes):
    return -(-sum(math.prod(s) for s in shapes) // (8 * PACK_COLS)) * 8


def kernel(x, mem, g_mix_pre, g_mix_post, g_cross_pre, g_mem, g_cross_post, g_ffn_pre, g_ffn_post, w_xq, w_xkv, w_xo, w_ffn_gu, w_ffn_down, ab_w_in, ab_b_f, ab_conv_w, ab_w_out, c_w_in, c_conv_w, c_conv_b, c_w_a, c_b_a, c_w_i, c_b_i, c_lam, c_w_out, loss_target, m_g_mix_pre, m_g_mix_post, m_g_cross_pre, m_g_mem, m_g_cross_post, m_g_ffn_pre, m_g_ffn_post, m_w_xq, m_w_xkv, m_w_xo, m_w_ffn_gu, m_w_ffn_down, m_ab_w_in, m_ab_b_f, m_ab_conv_w, m_ab_w_out, m_c_w_in, m_c_conv_w, m_c_conv_b, m_c_w_a, m_c_b_a, m_c_w_i, m_c_b_i, m_c_lam, m_c_w_out, v_g_mix_pre, v_g_mix_post, v_g_cross_pre, v_g_mem, v_g_cross_post, v_g_ffn_pre, v_g_ffn_post, v_w_xq, v_w_xkv, v_w_xo, v_w_ffn_gu, v_w_ffn_down, v_ab_w_in, v_ab_b_f, v_ab_conv_w, v_ab_w_out, v_c_w_in, v_c_conv_w, v_c_conv_b, v_c_w_a, v_c_b_a, v_c_w_i, v_c_b_i, v_c_lam, v_c_w_out):
    given = dict(locals())
    w = {n: given[n] for n in WEIGHTS}
    m_in = {n: given["m_" + n] for n in WEIGHTS}
    v_in = {n: given["v_" + n] for n in WEIGHTS}
    xi, yi, ci = _place()
    chip = 2 * xi + yi

    def as_halves(a):
        return a.reshape(2, -1, a.shape[-1]) if a.ndim > 1 else a

    gathered = gather_weights([as_halves(w[n].astype(BF16)) for n in BIG])
    full = {}
    for n, g in zip(BIG, gathered):
        slot = lax.broadcasted_iota(jnp.int32, (N_CHIPS,) + (1,) * w[n].ndim, 0)
        g = jnp.where(slot == chip, w[n].astype(BF16)[None], g.reshape(N_CHIPS, *w[n].shape))
        full[n] = _unshard(g, SHARD_DIM[n])
    small_shapes = [w[n].shape for n in SMALL_SHARDED]
    rows_w = _rows_for(small_shapes)
    every = gather_small("gather_small_weights", _pack([w[n] for n in SMALL_SHARDED], rows_w), reduce=False)
    per_chip = every[0::2].reshape(N_CHIPS, -1)
    at = 0
    for n, s in zip(SMALL_SHARDED, small_shapes):
        size = math.prod(s)
        full[n] = _unshard(per_chip[:, at:at + size].reshape(N_CHIPS, *s), SHARD_DIM[n])
        at += size
    for n in REPLICATED:
        full[n] = w[n]

    sq_cols, dx, grads = local_step(x[0], mem[0], loss_target[0], full)
    loss = lax.psum(0.5 / D_MODEL * jnp.sum(sq_cols), ("x", "y", "c"))

    keep, send = [], []
    for n in BIG:
        g4 = _shardify(grads[n], SHARD_DIM[n])
        g4 = g4.reshape(N_CHIPS, 2, -1, g4.shape[-1])
        keep.append(lax.dynamic_index_in_dim(g4, ci, axis=1, keepdims=False))
        send.append(lax.dynamic_index_in_dim(g4, 1 - ci, axis=1, keepdims=False).astype(BF16))
    got = swap_with_sibling("swap_grad_halves", send)
    chip_parts = [pair_sum(f"pair_sum_{n}", k, g) for n, k, g in zip(BIG, keep, got)]
    parts = exchange_chips(chip_parts)
    chip_arr = chip.reshape(1).astype(jnp.int32)
    halves = [chip_sum(f"chip_sum_{n}", mine, p, chip_arr) for n, mine, p in zip(BIG, chip_parts, parts)]
    others = swap_with_sibling("swap_reduced_halves", halves)
    grad_out = {}
    for n, mine, theirs in zip(BIG, halves, others):
        lo, hi = jnp.where(ci == 0, mine, theirs), jnp.where(ci == 0, theirs, mine)
        grad_out[n] = jnp.concatenate([lo, hi], axis=0).reshape(w[n].shape)

    small_names = REPLICATED + SMALL_SHARDED
    small_full_shapes = [grads[n].shape for n in small_names]
    total = gather_small("reduce_small_grads", _pack([grads[n] for n in small_names], _rows_for(small_full_shapes)),
                         reduce=True)
    for n, g in zip(small_names, _unpack(total, small_full_shapes)):
        if n in SHARD_DIM:
            g = lax.dynamic_index_in_dim(_shardify(g, SHARD_DIM[n]), chip, axis=0, keepdims=False)
        grad_out[n] = g

    delta, new_m, new_v = {}, {}, {}
    for n in BIG:
        two_d = lambda a: a.reshape(-1, a.shape[-1])
        d, m2, v2 = adamw(f"adamw_{n}", two_d(w[n]), two_d(grad_out[n]), two_d(m_in[n]), two_d(v_in[n]))
        delta[n], new_m[n], new_v[n] = (a.reshape(w[n].shape) for a in (d, m2, v2))
    shapes = [w[n].shape for n in small_names]
    rows = _rows_for(shapes)
    packed = [_pack([src[n] for n in small_names], rows) for src in (w, grad_out, m_in, v_in)]
    for dst, res in zip((delta, new_m, new_v), adamw("adamw_small", *packed)):
        for n, a in zip(small_names, _unpack(res, shapes)):
            dst[n] = a

    return (loss, dx[None], *[grad_out[n] for n in WEIGHTS], *[delta[n] for n in WEIGHTS],
            *[new_m[n] for n in WEIGHTS], *[new_v[n] for n in WEIGHTS])
```

```python
import functools
import math

import jax
import jax.numpy as jnp
from jax import lax
from jax.experimental import pallas as pl
from jax.experimental.pallas import tpu as pltpu

F32, BF16 = jnp.float32, jnp.bfloat16
D_MODEL = 1024
EPS = 1e-6
NEG_INF = -1e30
FOX_HEADS, FOX_HEAD_DIM, FOX_WIDTH = 8, 64, 512
SC_WIDTH = 512
AB_IN = 3 * FOX_WIDTH + FOX_HEADS + 3 * SC_WIDTH
AB_IN_PAD = 3200
LRU_BW, LRU_BLOCKS = 256, 4
RG_C = 8.0
MEM_HEADS, MEM_HEAD_DIM = 4, 256
ADAM_LR, ADAM_B1, ADAM_B2, ADAM_EPS, ADAM_WD, ADAM_STEP = 0.001, 0.9, 0.999, 1e-08, 0.01, 10
N_CHIPS = 4
MESH = pl.DeviceIdType.MESH
VMEM_LIMIT_BYTES = 48 * 1024 * 1024
MM_OPERAND_TILE_BYTES = 7 * 1024 * 1024

NN = (((1,), (0,)), ((), ()))
NT = (((1,), (1,)), ((), ()))
TN = (((0,), (0,)), ((), ()))


def _dot(a, b, dn=NN):
    return lax.dot_general(a.astype(BF16), b.astype(BF16), dn, preferred_element_type=F32)


def _tile(n, prefs):
    for p in prefs:
        if n % p == 0:
            return p
    return n


def _pcall(name, kern, grid, ins, in_specs, out_shape, out_specs, sem):
    return pl.pallas_call(
        kern, name=name, grid=grid, in_specs=in_specs, out_specs=out_specs, out_shape=out_shape,
        compiler_params=pltpu.CompilerParams(dimension_semantics=sem, vmem_limit_bytes=VMEM_LIMIT_BYTES),
    )(*ins)


def mm(name, a, b, mode, out_dtype=None, into=None):
    if mode == "nn":
        (m, k), n = a.shape, b.shape[1]
    elif mode == "nt":
        (m, k), n = a.shape, b.shape[0]
    else:
        (k, m), n = a.shape, b.shape[1]
    kind = into[2] if into else None
    tn = n // N_CHIPS if kind == "col" else _tile(n, ((1024,) if mode == "tn" else ()) + (512, 640, 256, 128))
    tm = m // N_CHIPS if kind == "row" else next(
        c for c in (2048, 1024, 512, 256, 128, m)
        if m % c == 0 and 2 * c * k <= MM_OPERAND_TILE_BYTES and 4 * c * tn <= MM_OPERAND_TILE_BYTES)
    dn = {"nn": NN, "nt": NT, "tn": TN}[mode]

    def kern(a_ref, b_ref, *rest):
        o_ref = rest[-1]
        o_ref[...] = _dot(a_ref[...], b_ref[...], dn).astype(o_ref.dtype)

    a_spec = pl.BlockSpec((k, tm), lambda i, j: (0, i)) if mode == "tn" else pl.BlockSpec((tm, k), lambda i, j: (i, 0))
    b_spec = pl.BlockSpec((tn, k), lambda i, j: (j, 0)) if mode == "nt" else pl.BlockSpec((k, tn), lambda i, j: (0, j))
    if into is None:
        return _pcall(name, kern, (m // tm, n // tn), (a, b), [a_spec, b_spec],
                      jax.ShapeDtypeStruct((m, n), out_dtype), pl.BlockSpec((tm, tn), lambda i, j: (i, j)),
                      ("parallel", "parallel"))
    buf, layer, _ = into
    o_spec = {"col": pl.BlockSpec((None, None, tm, tn), lambda i, j: (j, layer, i, 0)),
              "row": pl.BlockSpec((None, None, tm, tn), lambda i, j: (i, layer, 0, j)),
              "nat": pl.BlockSpec((None, tm, tn), lambda i, j: (layer, i, j))}[kind]
    return pl.pallas_call(
        kern, name=name, grid=(m // tm, n // tn), in_specs=[a_spec, b_spec, pl.BlockSpec(memory_space=pl.ANY)],
        out_specs=o_spec, out_shape=jax.ShapeDtypeStruct(buf.shape, buf.dtype), input_output_aliases={2: 0},
        compiler_params=pltpu.CompilerParams(dimension_semantics=("parallel", "parallel"),
                                             vmem_limit_bytes=VMEM_LIMIT_BYTES),
    )(a, b, buf)


def rowwise(name, body, rows, params, outs, accs=(), tr=256):
    t = rows[0].shape[0]
    tr = min(tr, t)
    nr, npar, no = len(rows), len(params), len(outs)

    def kern(*refs):
        acc_refs = refs[nr + npar + no:]
        if acc_refs:
            @pl.when(pl.program_id(0) == 0)
            def _():
                for ar in acc_refs:
                    ar[...] = jnp.zeros_like(ar)
        body(refs[:nr], refs[nr:nr + npar], refs[nr + npar:nr + npar + no], acc_refs)

    in_specs = [pl.BlockSpec((tr, x.shape[1]), lambda i: (i, 0)) for x in rows]
    in_specs += [pl.BlockSpec(p.shape, lambda i: (0, 0)) for p in params]
    out_specs = [pl.BlockSpec((tr, c), lambda i: (i, 0)) for c, _ in outs]
    out_specs += [pl.BlockSpec(s, lambda i: (0, 0)) for s in accs]
    out_shape = [jax.ShapeDtypeStruct((t, c), dt) for c, dt in outs]
    out_shape += [jax.ShapeDtypeStruct(s, F32) for s in accs]
    return _pcall(name, kern, (t // tr,), (*rows, *params), in_specs, out_shape, out_specs,
                  ("arbitrary",) if accs else ("parallel",))


def _rms_stats(x):
    r = lax.rsqrt(jnp.mean(x * x, axis=-1, keepdims=True) + EPS)
    return r, x * r


def _rms_bwd(xh, r, g, dy):
    dxh = dy * g
    dx = r * (dxh - xh * jnp.mean(dxh * xh, axis=-1, keepdims=True))
    return dx, jnp.sum(dy * xh, axis=0, keepdims=True)


def rms_pre(name, x, gains, layer):
    def body(r, p, o, a):
        _, xh = _rms_stats(r[0][...])
        o[0][...] = (xh * p[0][layer:layer + 1, :]).astype(BF16)
    return rowwise(name, body, [x], [gains], [(x.shape[1], BF16)])[0]


def post_add(name, x, y, gains, layer):
    def body(r, p, o, a):
        _, yh = _rms_stats(r[1][...])
        o[0][...] = r[0][...] + yh * p[0][layer:layer + 1, :]
    return rowwise(name, body, [x, y], [gains], [(x.shape[1], F32)])[0]


def post_bwd(name, y, dx, gains, layer):
    def body(r, p, o, a):
        rr, yh = _rms_stats(r[0][...])
        dy, dg = _rms_bwd(yh, rr, p[0][layer:layer + 1, :], r[1][...])
        o[0][...] = dy.astype(BF16)
        a[0][...] += dg
    c = y.shape[1]
    return rowwise(name, body, [y, dx], [gains], [(c, BF16)], [(1, c)])


def pre_bwd(name, x, dh, dx_res, gains, layer):
    def body(r, p, o, a):
        rr, xh = _rms_stats(r[0][...])
        dx, dg = _rms_bwd(xh, rr, p[0][layer:layer + 1, :], r[1][...])
        o[0][...] = r[2][...] + dx
        a[0][...] += dg
    c = x.shape[1]
    return rowwise(name, body, [x, dh, dx_res], [gains], [(c, F32)], [(1, c)])


def gain_bwd(name, x, dh):
    def body(r, p, o, a):
        _, xh = _rms_stats(r[0][...])
        a[0][...] += jnp.sum(r[1][...] * xh, axis=0, keepdims=True)
    return rowwise(name, body, [x, dh], [], [], [(1, x.shape[1])])[0]


def _sigmoid(z):
    return 1.0 / (1.0 + jnp.exp(-z))


def swiglu_fwd(name, gu):
    f = gu.shape[1] // 2

    def body(r, p, o, a):
        g = r[0][:, :f].astype(F32)
        u = r[0][:, f:].astype(F32)
        o[0][...] = (g * _sigmoid(g) * u).astype(BF16)
    return rowwise(name, body, [gu], [], [(f, BF16)])[0]


def swiglu_bwd(name, gu, da):
    f = gu.shape[1] // 2

    def body(r, p, o, a):
        g = r[0][:, :f].astype(F32)
        u = r[0][:, f:].astype(F32)
        d = r[1][...].astype(F32)
        sg = _sigmoid(g)
        o[0][:, :f] = (d * u * sg * (1.0 + g * (1.0 - sg))).astype(BF16)
        o[0][:, f:] = (d * g * sg).astype(BF16)
    return rowwise(name, body, [gu, da], [], [(2 * f, BF16)])[0]


def loss_head(name, y, target):
    c = y.shape[1]

    def body(r, p, o, a):
        e = r[0][...] - r[1][...]
        o[0][...] = e * (1.0 / c)
        a[0][...] += jnp.sum(e * e, axis=0, keepdims=True)
    return rowwise(name, body, [y, target], [], [(c, F32)], [(1, c)])


def adamw(name, w, g, m, v):
    c = w.shape[1]

    def body(r, p, o, a):
        wv, gv, mv, vv = (x[...] for x in r)
        m2 = ADAM_B1 * mv + (1.0 - ADAM_B1) * gv
        v2 = ADAM_B2 * vv + (1.0 - ADAM_B2) * (gv * gv)
        m_hat = m2 / (1.0 - ADAM_B1 ** ADAM_STEP)
        v_hat = v2 / (1.0 - ADAM_B2 ** ADAM_STEP)
        o[0][...] = -ADAM_LR * (m_hat / (jnp.sqrt(v_hat) + ADAM_EPS) + ADAM_WD * wv)
        o[1][...] = m2
        o[2][...] = v2
    tr = _tile(w.shape[0], (256, 128, 64, 32, 16, 8))
    return rowwise(name, body, [w, g, m, v], [], [(c, F32)] * 3, tr=tr)


def colwise(name, body, cols, params, outs, pouts=(), tc=128):
    t = cols[0][0].shape[0]
    c = params[0].shape[1] if params else cols[0][0].shape[1]
    nc, npar, no = len(cols), len(params), len(outs)

    def kern(*refs):
        body(refs[:nc], refs[nc:nc + npar], refs[nc + npar:nc + npar + no], refs[nc + npar + no:])

    in_specs = [pl.BlockSpec((t, tc), functools.partial(lambda j, off: (0, j + off), off=off)) for _, off in cols]
    in_specs += [pl.BlockSpec((p.shape[0], tc), lambda j: (0, j)) for p in params]
    out_specs = [pl.BlockSpec((t, tc), lambda j: (0, j)) for _ in outs]
    out_specs += [pl.BlockSpec((r, tc), lambda j: (0, j)) for r in pouts]
    out_shape = [jax.ShapeDtypeStruct((t, c), dt) for dt in outs]
    out_shape += [jax.ShapeDtypeStruct((r, c), F32) for r in pouts]
    return _pcall(name, kern, (c // tc,), (*[x for x, _ in cols], *params), in_specs, out_shape, out_specs,
                  ("parallel",))


def _row_index(shape):
    return lax.broadcasted_iota(jnp.int32, shape, 0)


def _shift_down(x, d, rows):
    return jnp.where(rows >= d, pltpu.roll(x, d, 0), 0.0)


def _shift_up(x, d, rows):
    t = x.shape[0]
    return jnp.where(rows < t - d, pltpu.roll(x, t - d, 0), 0.0)


def sconv_fwd(name, proj, col0, conv_w, tc=128):
    nb = SC_WIDTH // tc

    def body(cl, p, o, po):
        b, c, u = (x[...] for x in cl)
        rows = _row_index(b.shape)
        w = p[0][...]
        z = c * u
        conv = w[2:3] * z + w[1:2] * _shift_down(z, 1, rows) + w[0:1] * _shift_down(z, 2, rows)
        o[0][...] = (b * conv).astype(BF16)
    return colwise(name, body, [(proj, col0), (proj, col0 + nb), (proj, col0 + 2 * nb)], [conv_w], [BF16], tc=tc)[0]


def sconv_bwd(name, proj, col0, conv_w, dyb, dcol0, tc=128):
    nb = SC_WIDTH // tc

    def body(cl, p, o, po):
        b, c, u, dy = (x[...] for x in cl)
        rows = _row_index(b.shape)
        w = p[0][...]
        z = c * u
        z1, z2 = _shift_down(z, 1, rows), _shift_down(z, 2, rows)
        conv = w[2:3] * z + w[1:2] * z1 + w[0:1] * z2
        dconv = dy * b
        dz = w[2:3] * dconv + w[1:2] * _shift_up(dconv, 1, rows) + w[0:1] * _shift_up(dconv, 2, rows)
        o[0][...] = (dy * conv).astype(BF16)
        o[1][...] = (dz * u).astype(BF16)
        o[2][...] = (dz * c).astype(BF16)
        po[0][0:1, :] = jnp.sum(dconv * z2, axis=0, keepdims=True)
        po[0][1:2, :] = jnp.sum(dconv * z1, axis=0, keepdims=True)
        po[0][2:3, :] = jnp.sum(dconv * z, axis=0, keepdims=True)
    return colwise(name, body, [(proj, col0), (proj, col0 + nb), (proj, col0 + 2 * nb), (dyb, dcol0)], [conv_w],
                   [BF16, BF16, BF16], [3], tc=tc)


def _expm1(x):
    series = x * (1.0 + 0.5 * x * (1.0 + x * (1.0 / 3.0) * (1.0 + 0.25 * x * (1.0 + 0.2 * x))))
    return jnp.where(jnp.abs(x) < 0.05, series, jnp.exp(x) - 1.0)


def _log1p(x):
    series = x * (1.0 - x * (0.5 - x * (1.0 / 3.0 - 0.25 * x)))
    return jnp.where(jnp.abs(x) < 0.01, series, jnp.log(1.0 + x))


def _softplus_neg(lam):
    sp = jnp.maximum(-lam, 0.0) + _log1p(jnp.exp(-jnp.abs(lam)))
    return sp, -_sigmoid(-lam)


GELU_C = math.sqrt(2.0 / math.pi)


def _gelu(x):
    th = jnp.tanh(GELU_C * (x + 0.044715 * x * x * x))
    val = 0.5 * x * (1.0 + th)
    grad = 0.5 * (1.0 + th) + 0.5 * x * (1.0 - th * th) * GELU_C * (1.0 + 3.0 * 0.044715 * x * x)
    return val, grad


def rg_conv_fwd(name, gu2, conv_w, conv_b, tc=128):
    nb = D_MODEL // tc

    def body(cl, p, o, po):
        u = cl[0][...]
        rows = _row_index(u.shape)
        w = p[0][...]
        o[0][...] = (w[3:4] * u + w[2:3] * _shift_down(u, 1, rows) + w[1:2] * _shift_down(u, 2, rows)
                     + w[0:1] * _shift_down(u, 3, rows) + p[1][...])
    return colwise(name, body, [(gu2, nb)], [conv_w, conv_b], [F32], tc=tc)[0]


def rg_conv_bwd(name, gu2, duc, conv_w, tc=128):
    nb = D_MODEL // tc

    def body(cl, p, o, po):
        u, d = cl[0][...], cl[1][...]
        rows = _row_index(u.shape)
        w = p[0][...]
        o[0][...] = (w[3:4] * d + w[2:3] * _shift_up(d, 1, rows) + w[1:2] * _shift_up(d, 2, rows)
                     + w[0:1] * _shift_up(d, 3, rows)).astype(BF16)
        for k in range(4):
            uk = u if k == 3 else _shift_down(u, 3 - k, rows)
            po[0][k:k + 1, :] = jnp.sum(d * uk, axis=0, keepdims=True)
        po[1][...] = jnp.sum(d, axis=0, keepdims=True)
    return colwise(name, body, [(gu2, nb), (duc, 0)], [conv_w], [BF16], [4, 1], tc=tc)


def rg_gates_fwd(name, uc, w_a, b_a, w_i, b_i, tr=512):
    t = uc.shape[0]
    tr = min(tr, t)

    def kern(u_ref, wa_ref, ba_ref, wi_ref, bi_ref, r_ref, i_ref):
        ub = u_ref[...].astype(BF16)
        r_ref[...] = _sigmoid(_dot(ub, wa_ref[...]) + ba_ref[...])
        i_ref[...] = _sigmoid(_dot(ub, wi_ref[...]) + bi_ref[...])

    blk = pl.BlockSpec((tr, LRU_BW), lambda n, i: (i, n))
    wspec = pl.BlockSpec((None, LRU_BW, LRU_BW), lambda n, i: (n, 0, 0))
    bspec = pl.BlockSpec((1, LRU_BW), lambda n, i: (0, n))
    return _pcall(name, kern, (LRU_BLOCKS, t // tr), (uc, w_a, b_a, w_i, b_i), [blk, wspec, bspec, wspec, bspec],
                  [jax.ShapeDtypeStruct(uc.shape, F32)] * 2, [blk, blk], ("parallel", "parallel"))


def rg_gates_bwd(name, uc, dzr, dzi, duc_part, w_a, w_i, buf_a, buf_i, layer):
    t = uc.shape[0]
    rows = LRU_BW // N_CHIPS

    def kern(u_ref, dr_ref, di_ref, dp_ref, wa_ref, wi_ref, _a, _i, duc_ref, dwa_ref, dwi_ref):
        ub = u_ref[...].astype(BF16)
        dr, di = dr_ref[...], di_ref[...]
        dwa, dwi = _dot(ub, dr, TN), _dot(ub, di, TN)
        for p in range(N_CHIPS):
            dwa_ref[p] = dwa[p * rows:(p + 1) * rows].astype(dwa_ref.dtype)
            dwi_ref[p] = dwi[p * rows:(p + 1) * rows].astype(dwi_ref.dtype)
        duc_ref[...] = dp_ref[...] + _dot(dr, wa_ref[...], NT) + _dot(di, wi_ref[...], NT)

    blk = pl.BlockSpec((t, LRU_BW), lambda n: (0, n))
    wspec = pl.BlockSpec((None, LRU_BW, LRU_BW), lambda n: (n, 0, 0))
    gspec = pl.BlockSpec((N_CHIPS, None, None, rows, LRU_BW), lambda n: (0, layer, n, 0, 0))
    anyspec = pl.BlockSpec(memory_space=pl.ANY)
    return pl.pallas_call(
        kern, name=name, grid=(LRU_BLOCKS,), in_specs=[blk, blk, blk, blk, wspec, wspec, anyspec, anyspec],
        out_specs=[blk, gspec, gspec],
        out_shape=[jax.ShapeDtypeStruct(uc.shape, F32), jax.ShapeDtypeStruct(buf_a.shape, buf_a.dtype),
                   jax.ShapeDtypeStruct(buf_i.shape, buf_i.dtype)],
        input_output_aliases={6: 1, 7: 2},
        compiler_params=pltpu.CompilerParams(dimension_semantics=("parallel",), vmem_limit_bytes=VMEM_LIMIT_BYTES),
    )(uc, dzr, dzi, duc_part, w_a, w_i, buf_a, buf_i)


def _rg_decay(r, lam):
    sp, dsp = _softplus_neg(lam)
    la = -RG_C * r * sp
    a = jnp.exp(la)
    sq = jnp.sqrt(-_expm1(2.0 * la))
    return sp, dsp, a, sq


def rg_scan_fwd(name, gu2, uc, r, i, lam, tc=128):
    def body(cl, p, o, po):
        gate, ucv, rv, iv = (x[...] for x in cl)
        t = gate.shape[0]
        rows = _row_index(gate.shape)
        _, _, a, sq = _rg_decay(rv, p[0][...])
        b = sq * (iv * ucv)
        d = 1
        while d < t:
            keep = rows >= d
            b = a * jnp.where(keep, pltpu.roll(b, d, 0), 0.0) + b
            a = a * jnp.where(keep, pltpu.roll(a, d, 0), 1.0)
            d *= 2
        o[0][...] = (_gelu(gate)[0] * b).astype(BF16)
        o[1][...] = b
    return colwise(name, body, [(gu2, 0), (uc, 0), (r, 0), (i, 0)], [lam], [BF16, F32], tc=tc)


def rg_scan_bwd(name, gu2, uc, r, i, hs, dy, lam, tc=128):
    def body(cl, p, o, po):
        gate, ucv, rv, iv, h, dyv = (x[...] for x in cl)
        t = gate.shape[0]
        rows = _row_index(gate.shape)
        sp, dsp, a, sq = _rg_decay(rv, p[0][...])
        gl, dgl = _gelu(gate)
        o[0][...] = (dyv * h * dgl).astype(BF16)
        g = dyv * gl
        am = _shift_up(a, 1, rows)
        d = 1
        while d < t:
            keep = rows < t - d
            g = am * jnp.where(keep, pltpu.roll(g, t - d, 0), 0.0) + g
            am = am * jnp.where(keep, pltpu.roll(am, t - d, 0), 0.0)
            d *= 2
        da = g * _shift_down(h, 1, rows)
        iu = iv * ucv
        d_iu = g * sq
        dla = da * a - (g * iu) * (a * a) / sq
        dzr = dla * (-RG_C * sp) * rv * (1.0 - rv)
        dzi = d_iu * ucv * iv * (1.0 - iv)
        o[1][...] = dzr.astype(BF16)
        o[2][...] = dzi.astype(BF16)
        o[3][...] = d_iu * iv
        po[0][...] = jnp.sum(dzr, axis=0, keepdims=True)
        po[1][...] = jnp.sum(dzi, axis=0, keepdims=True)
        po[2][...] = jnp.sum(dla * rv, axis=0, keepdims=True) * (-RG_C) * dsp
    return colwise(name, body, [(gu2, 0), (uc, 0), (r, 0), (i, 0), (hs, 0), (dy, 0)], [lam],
                   [BF16, BF16, BF16, F32], [1, 1, 1], tc=tc)


def _split3(x):
    hi = x.astype(BF16)
    r1 = x - hi.astype(F32)
    mid = r1.astype(BF16)
    lo = (r1 - mid.astype(F32)).astype(BF16)
    return hi, mid, lo


def _tri_dot(x, tri):
    out = None
    for piece in _split3(x):
        term = lax.dot_general(piece, tri, NN, preferred_element_type=F32)
        out = term if out is None else out + term
    return out


def fox_gates_fwd(name, z_t, b_f):
    h, t = z_t.shape
    tb = min(512, t)

    def kern(z_ref, b_ref, o_ref):
        z = z_ref[...] + b_ref[...]
        logf = jnp.minimum(z, 0.0) - _log1p(jnp.exp(-jnp.abs(z)))
        src = lax.broadcasted_iota(jnp.int32, (t, tb), 0)
        dst = lax.broadcasted_iota(jnp.int32, (t, tb), 1) + pl.program_id(0) * tb
        o_ref[...] = _tri_dot(logf, (src <= dst).astype(BF16))

    return _pcall(name, kern, (t // tb,), (z_t, b_f),
                  [pl.BlockSpec((h, t), lambda j: (0, 0)), pl.BlockSpec((h, 1), lambda j: (0, 0))],
                  jax.ShapeDtypeStruct((h, t), F32), pl.BlockSpec((h, tb), lambda j: (0, j)), ("parallel",))


def fox_gates_bwd(name, z_t, b_f, dcum_t):
    h, t = z_t.shape
    tb = min(512, t)

    def kern(z_ref, b_ref, d_ref, dz_ref, db_ref):
        @pl.when(pl.program_id(0) == 0)
        def _():
            db_ref[...] = jnp.zeros_like(db_ref)
        src = lax.broadcasted_iota(jnp.int32, (t, tb), 0)
        dst = lax.broadcasted_iota(jnp.int32, (t, tb), 1) + pl.program_id(0) * tb
        dlogf = _tri_dot(d_ref[...], (src >= dst).astype(BF16))
        z = z_ref[...] + b_ref[...]
        dz = dlogf * _sigmoid(-z)
        dz_ref[...] = dz
        db_ref[...] += jnp.sum(dz, axis=1, keepdims=True)

    return _pcall(name, kern, (t // tb,), (z_t, b_f, dcum_t),
                  [pl.BlockSpec((h, tb), lambda j: (0, j)), pl.BlockSpec((h, 1), lambda j: (0, 0)),
                   pl.BlockSpec((h, t), lambda j: (0, 0))],
                  [jax.ShapeDtypeStruct((h, t), F32), jax.ShapeDtypeStruct((h, 1), F32)],
                  [pl.BlockSpec((h, tb), lambda j: (0, j)), pl.BlockSpec((h, 1), lambda j: (0, 0))], ("arbitrary",))


def _fox_spans(qs, k_ref, cr_ref, i, tq):
    n0 = i * tq
    sd = _dot(qs, k_ref[n0:n0 + tq, :], NT) - cr_ref[:, n0:n0 + tq]
    row = lax.broadcasted_iota(jnp.int32, (tq, tq), 0)
    col = lax.broadcasted_iota(jnp.int32, (tq, tq), 1)
    spans = [(n0, tq, jnp.where(row >= col, sd, NEG_INF))]
    if i > 0:
        spans.append((0, n0, _dot(qs, k_ref[0:n0, :], NT) - cr_ref[:, 0:n0]))
    return spans


def fox_fwd(name, q, k, v, cum_r, tq=256):
    h, t, dh = q.shape
    tq = min(tq, t)
    scale = FOX_HEAD_DIM ** -0.5

    def kern(q_ref, k_ref, v_ref, cr_ref, o_ref, lse_ref):
        for i in range(t // tq):
            rows = slice(i * tq, (i + 1) * tq)
            spans = _fox_spans(q_ref[rows, :] * scale, k_ref, cr_ref, i, tq)
            m = functools.reduce(jnp.maximum, [jnp.max(s, axis=-1, keepdims=True) for _, _, s in spans])
            l, acc = 0.0, 0.0
            for k0, kn, s in spans:
                p = jnp.exp(s - m)
                l = l + jnp.sum(p, axis=-1, keepdims=True)
                acc = acc + _dot(p, v_ref[k0:k0 + kn, :])
            o_ref[rows, :] = (acc / l).astype(o_ref.dtype)
            lse_ref[rows, :] = m + jnp.log(l)

    hspec = pl.BlockSpec((None, t, dh), lambda a: (a, 0, 0))
    cspec = pl.BlockSpec((None, t, 1), lambda a: (a, 0, 0))
    rspec = pl.BlockSpec((None, 1, t), lambda a: (a, 0, 0))
    return _pcall(name, kern, (h,), (q, k, v, cum_r), [hspec, hspec, hspec, rspec],
                  [jax.ShapeDtypeStruct((h, t, dh), BF16), jax.ShapeDtypeStruct((h, t, 1), F32)],
                  [hspec, cspec], ("parallel",))


def fox_bwd(name, q, k, v, do, lse, cum_r, tq=256):
    h, t, dh = q.shape
    tq = min(tq, t)
    scale = FOX_HEAD_DIM ** -0.5

    def kern(q_ref, k_ref, v_ref, do_ref, lse_ref, cr_ref, dq_ref, dk_ref, dv_ref, dc_ref):
        dk_ref[...] = jnp.zeros_like(dk_ref)
        dv_ref[...] = jnp.zeros_like(dv_ref)
        dc_ref[...] = jnp.zeros_like(dc_ref)
        for i in range(t // tq):
            rows = slice(i * tq, (i + 1) * tq)
            qs, dov, lse_v = q_ref[rows, :] * scale, do_ref[rows, :], lse_ref[rows, :]
            spans = _fox_spans(qs, k_ref, cr_ref, i, tq)
            probs = [jnp.exp(s - lse_v) for _, _, s in spans]
            dps = [_dot(dov, v_ref[k0:k0 + kn, :], NT) for k0, kn, _ in spans]
            rowdot = sum(jnp.sum(dp * p, axis=-1, keepdims=True) for dp, p in zip(dps, probs))
            dq = 0.0
            for (k0, kn, _), p, dp in zip(spans, probs, dps):
                ds = p * (dp - rowdot)
                dq = dq + _dot(ds, k_ref[k0:k0 + kn, :])
                dk_ref[k0:k0 + kn, :] += _dot(ds, qs, TN)
                dv_ref[k0:k0 + kn, :] += _dot(p, dov, TN)
                dc_ref[:, k0:k0 + kn] -= jnp.sum(ds, axis=0, keepdims=True)
            dq_ref[rows, :] = (dq * scale).astype(dq_ref.dtype)

    hspec = pl.BlockSpec((None, t, dh), lambda a: (a, 0, 0))
    cspec = pl.BlockSpec((None, t, 1), lambda a: (a, 0, 0))
    rspec = pl.BlockSpec((None, 1, t), lambda a: (a, 0, 0))
    return _pcall(name, kern, (h,), (q, k, v, do, lse, cum_r), [hspec, hspec, hspec, hspec, cspec, rspec],
                  [jax.ShapeDtypeStruct((h, t, dh), BF16), jax.ShapeDtypeStruct((h, t, dh), F32),
                   jax.ShapeDtypeStruct((h, t, dh), F32), jax.ShapeDtypeStruct((h, 1, t), F32)],
                  [hspec, hspec, hspec, rspec], ("parallel",))


def _xattn_probs(q, k):
    s = _dot(q, k, NT) * (MEM_HEAD_DIM ** -0.5)
    p = jnp.exp(s - jnp.max(s, axis=-1, keepdims=True))
    return p / jnp.sum(p, axis=-1, keepdims=True)


def xattn_fwd(name, q, kv, tq=512):
    t = q.shape[0]
    tq = min(tq, t)
    ml = kv.shape[0]

    def kern(q_ref, k_ref, v_ref, o_ref):
        o_ref[...] = _dot(_xattn_probs(q_ref[...], k_ref[...]), v_ref[...]).astype(o_ref.dtype)

    qspec = pl.BlockSpec((tq, MEM_HEAD_DIM), lambda i, a: (i, a))
    return _pcall(name, kern, (t // tq, MEM_HEADS), (q, kv, kv),
                  [qspec, pl.BlockSpec((ml, MEM_HEAD_DIM), lambda i, a: (0, a)),
                   pl.BlockSpec((ml, MEM_HEAD_DIM), lambda i, a: (0, MEM_HEADS + a))],
                  jax.ShapeDtypeStruct(q.shape, BF16), qspec, ("parallel", "parallel"))


def xattn_bwd(name, q, kv, do, tq=512):
    t = q.shape[0]
    tq = min(tq, t)
    ml = kv.shape[0]
    scale = MEM_HEAD_DIM ** -0.5

    def kern(q_ref, k_ref, v_ref, do_ref, dq_ref, dk_ref, dv_ref):
        @pl.when(pl.program_id(1) == 0)
        def _():
            dk_ref[...] = jnp.zeros_like(dk_ref)
            dv_ref[...] = jnp.zeros_like(dv_ref)
        qv, kv_, dov = q_ref[...], k_ref[...], do_ref[...]
        p = _xattn_probs(qv, kv_)
        dp = _dot(dov, v_ref[...], NT)
        ds = p * (dp - jnp.sum(dp * p, axis=-1, keepdims=True)) * scale
        dq_ref[...] = _dot(ds, kv_).astype(dq_ref.dtype)
        dk_ref[...] += _dot(ds, qv, TN)
        dv_ref[...] += _dot(p, dov, TN)

    qspec = pl.BlockSpec((tq, MEM_HEAD_DIM), lambda a, i: (i, a))
    kspec = pl.BlockSpec((ml, MEM_HEAD_DIM), lambda a, i: (0, a))
    return _pcall(name, kern, (MEM_HEADS, t // tq), (q, kv, kv, do),
                  [qspec, kspec, pl.BlockSpec((ml, MEM_HEAD_DIM), lambda a, i: (0, MEM_HEADS + a)), qspec],
                  [jax.ShapeDtypeStruct(q.shape, BF16), jax.ShapeDtypeStruct((ml, D_MODEL), F32),
                   jax.ShapeDtypeStruct((ml, D_MODEL), F32)],
                  [qspec, kspec, kspec], ("parallel", "arbitrary"))


def _heads(x):
    t = x.shape[0]
    return x.reshape(t, FOX_HEADS, FOX_HEAD_DIM).transpose(1, 0, 2)


def _unheads(x):
    return x.transpose(1, 0, 2).reshape(x.shape[1], FOX_WIDTH)


GRAD_KIND = {"w_xq": "row", "w_xkv": "col", "w_xo": "row", "w_ffn_gu": "col", "w_ffn_down": "nat", "ab_w_out": "row",
             "c_w_in": "col", "c_w_a": "blk", "c_w_i": "blk", "c_w_out": "row"}


def local_step(x, mem, target, w):
    depth = w["g_mix_pre"].shape[0]
    t = x.shape[0]
    saved = []
    i1, i2, i3 = 3 * FOX_WIDTH, 3 * FOX_WIDTH + FOX_HEADS, AB_IN
    ncol = 128

    for layer in range(depth):
        s = {"x0": x}
        tag = f"l{layer}"
        h1 = rms_pre(f"{tag}_mix_pre", x, w["g_mix_pre"], layer)
        s["h1"] = h1
        if layer % 2 == 0:
            e = layer // 2
            w_in = jnp.pad(w["ab_w_in"][e], ((0, 0), (0, AB_IN_PAD - AB_IN)))
            proj = mm(f"{tag}_ab_in", h1, w_in, "nn", F32)
            qkv = proj[:, :i1].astype(BF16).reshape(t, 3, FOX_HEADS, FOX_HEAD_DIM).transpose(1, 2, 0, 3)
            z_t = proj[:, i1:i2].T
            b_f = w["ab_b_f"][e].reshape(FOX_HEADS, 1)
            cum_t = fox_gates_fwd(f"{tag}_fox_gates", z_t, b_f)
            cum_r = cum_t[:, None, :]
            oh, lse = fox_fwd(f"{tag}_fox", qkv[0], qkv[1], qkv[2], cum_r)
            bcu = proj[:, i2:i3]
            y_b = sconv_fwd(f"{tag}_sconv", bcu, 0, w["ab_conv_w"][e])
            ycat = jnp.concatenate([_unheads(oh), y_b], axis=1)
            y1 = mm(f"{tag}_ab_out", ycat, w["ab_w_out"][e], "nn", F32)
            s.update(w_in=w_in, qkv=qkv, z_t=z_t, b_f=b_f, cum_r=cum_r, lse=lse, bcu=bcu, ycat=ycat)
        else:
            o = layer // 2
            gu2 = mm(f"{tag}_c_in", h1, w["c_w_in"][o], "nn", F32)
            conv_b = w["c_conv_b"][o].reshape(1, -1)
            uc = rg_conv_fwd(f"{tag}_rg_conv", gu2, w["c_conv_w"][o], conv_b)
            b_a, b_i = w["c_b_a"][o].reshape(1, -1), w["c_b_i"][o].reshape(1, -1)
            r, i = rg_gates_fwd(f"{tag}_rg_gates", uc, w["c_w_a"][o], b_a, w["c_w_i"][o], b_i)
            lam = w["c_lam"][o].reshape(1, -1)
            ymix, hs = rg_scan_fwd(f"{tag}_rg_scan", gu2, uc, r, i, lam)
            y1 = mm(f"{tag}_c_out", ymix, w["c_w_out"][o], "nn", F32)
            s.update(gu2=gu2, uc=uc, r=r, i=i, lam=lam, hs=hs, ymix=ymix)
        s["y1"] = y1
        x = post_add(f"{tag}_mix_post", x, y1, w["g_mix_post"], layer)
        s["x1"] = x
        h2 = rms_pre(f"{tag}_cross_pre", x, w["g_cross_pre"], layer)
        m = rms_pre(f"{tag}_mem_pre", mem, w["g_mem"], layer)
        q = mm(f"{tag}_xq", h2, w["w_xq"][layer], "nn", BF16)
        kv = mm(f"{tag}_xkv", m, w["w_xkv"][layer], "nn", BF16)
        o_att = xattn_fwd(f"{tag}_xattn", q, kv)
        y2 = mm(f"{tag}_xo", o_att, w["w_xo"][layer], "nn", F32)
        s.update(h2=h2, m=m, q=q, kv=kv, o_att=o_att, y2=y2)
        x = post_add(f"{tag}_cross_post", x, y2, w["g_cross_post"], layer)
        s["x2"] = x
        h3 = rms_pre(f"{tag}_ffn_pre", x, w["g_ffn_pre"], layer)
        gu = mm(f"{tag}_ffn_gu", h3, w["w_ffn_gu"][layer], "nn", BF16)
        act = swiglu_fwd(f"{tag}_swiglu", gu)
        y3 = mm(f"{tag}_ffn_down", act, w["w_ffn_down"][layer], "nn", F32)
        s.update(h3=h3, gu=gu, act=act, y3=y3)
        x = post_add(f"{tag}_ffn_post", x, y3, w["g_ffn_post"], layer)
        saved.append(s)

    dx, sq_cols = loss_head("loss_head", x, target)

    grads = {k: [None] * v.shape[0] for k, v in w.items() if k not in GRAD_KIND}
    gb = {}
    for n, kind in GRAD_KIND.items():
        nl, rows, cols = w[n].shape[0], math.prod(w[n].shape[1:-1]), w[n].shape[-1]
        shape = {"row": (N_CHIPS, nl, rows // N_CHIPS, cols), "col": (N_CHIPS, nl, rows, cols // N_CHIPS),
                 "nat": (nl, rows, cols), "blk": (N_CHIPS, nl, LRU_BLOCKS, LRU_BW // N_CHIPS, LRU_BW)}[kind]
        gb[n] = lax.empty(shape, BF16)

    def dw(name, n, layer, a, b):
        gb[n] = mm(name, a, b, "tn", into=(gb[n], layer, GRAD_KIND[n]))

    for layer in reversed(range(depth)):
        s = saved[layer]
        tag = f"b{layer}"
        dy3, grads["g_ffn_post"][layer] = post_bwd(f"{tag}_ffn_post", s["y3"], dx, w["g_ffn_post"], layer)
        dact = mm(f"{tag}_ffn_down_dx", dy3, w["w_ffn_down"][layer], "nt", BF16)
        dw(f"{tag}_ffn_down_dw", "w_ffn_down", layer, s["act"], dy3)
        dgu = swiglu_bwd(f"{tag}_swiglu", s["gu"], dact)
        dh3 = mm(f"{tag}_ffn_gu_dx", dgu, w["w_ffn_gu"][layer], "nt", F32)
        dw(f"{tag}_ffn_gu_dw", "w_ffn_gu", layer, s["h3"], dgu)
        dx, grads["g_ffn_pre"][layer] = pre_bwd(f"{tag}_ffn_pre", s["x2"], dh3, dx, w["g_ffn_pre"], layer)
        dy2, grads["g_cross_post"][layer] = post_bwd(f"{tag}_cross_post", s["y2"], dx, w["g_cross_post"], layer)
        do = mm(f"{tag}_xo_dx", dy2, w["w_xo"][layer], "nt", BF16)
        dw(f"{tag}_xo_dw", "w_xo", layer, s["o_att"], dy2)
        dq, dk, dv = xattn_bwd(f"{tag}_xattn", s["q"], s["kv"], do)
        dh2 = mm(f"{tag}_xq_dx", dq, w["w_xq"][layer], "nt", F32)
        dw(f"{tag}_xq_dw", "w_xq", layer, s["h2"], dq)
        dkv = jnp.concatenate([dk, dv], axis=1).astype(BF16)
        dm = mm(f"{tag}_xkv_dx", dkv, w["w_xkv"][layer], "nt", F32)
        dw(f"{tag}_xkv_dw", "w_xkv", layer, s["m"], dkv)
        grads["g_mem"][layer] = gain_bwd(f"{tag}_mem_pre", mem, dm)
        dx, grads["g_cross_pre"][layer] = pre_bwd(f"{tag}_cross_pre", s["x1"], dh2, dx, w["g_cross_pre"], layer)
        dy1, grads["g_mix_post"][layer] = post_bwd(f"{tag}_mix_post", s["y1"], dx, w["g_mix_post"], layer)
        if layer % 2 == 0:
            e = layer // 2
            dycat = mm(f"{tag}_ab_out_dx", dy1, w["ab_w_out"][e], "nt", F32)
            dw(f"{tag}_ab_out_dw", "ab_w_out", e, s["ycat"], dy1)
            do_h = _heads(dycat[:, :FOX_WIDTH].astype(BF16))
            qkv = s["qkv"]
            dqh, dkh, dvh, dcum = fox_bwd(f"{tag}_fox", qkv[0], qkv[1], qkv[2], do_h, s["lse"], s["cum_r"])
            dz_t, db_f = fox_gates_bwd(f"{tag}_fox_gates", s["z_t"], s["b_f"], dcum.reshape(FOX_HEADS, t))
            grads["ab_b_f"][e] = db_f.reshape(FOX_HEADS)
            db, dc, du, dconv_w = sconv_bwd(f"{tag}_sconv", s["bcu"], 0, w["ab_conv_w"][e], dycat, FOX_WIDTH // ncol)
            grads["ab_conv_w"][e] = dconv_w
            dproj = jnp.concatenate(
                [_unheads(dqh), _unheads(dkh).astype(BF16), _unheads(dvh).astype(BF16), dz_t.T.astype(BF16), db, dc, du,
                 jnp.zeros((t, AB_IN_PAD - AB_IN), BF16)], axis=1)
            dh1 = mm(f"{tag}_ab_in_dx", dproj, s["w_in"], "nt", F32)
            grads["ab_w_in"][e] = mm(f"{tag}_ab_in_dw", s["h1"], dproj, "tn", F32)[:, :AB_IN]
        else:
            o = layer // 2
            dymix = mm(f"{tag}_c_out_dx", dy1, w["c_w_out"][o], "nt", F32)
            dw(f"{tag}_c_out_dw", "c_w_out", o, s["ymix"], dy1)
            dgate, dzr, dzi, duc_part, db_a, db_i, dlam = rg_scan_bwd(
                f"{tag}_rg_scan", s["gu2"], s["uc"], s["r"], s["i"], s["hs"], dymix, s["lam"])
            duc, gb["c_w_a"], gb["c_w_i"] = rg_gates_bwd(f"{tag}_rg_gates", s["uc"], dzr, dzi, duc_part, w["c_w_a"][o],
                                                         w["c_w_i"][o], gb["c_w_a"], gb["c_w_i"], o)
            du_raw, dconv_w, dconv_b = rg_conv_bwd(f"{tag}_rg_conv", s["gu2"], duc, w["c_conv_w"][o])
            grads["c_b_a"][o] = db_a.reshape(LRU_BLOCKS, LRU_BW)
            grads["c_b_i"][o] = db_i.reshape(LRU_BLOCKS, LRU_BW)
            grads["c_lam"][o] = dlam.reshape(-1)
            grads["c_conv_w"][o] = dconv_w
            grads["c_conv_b"][o] = dconv_b.reshape(-1)
            dgu2 = jnp.concatenate([dgate, du_raw], axis=1)
            dh1 = mm(f"{tag}_c_in_dx", dgu2, w["c_w_in"][o], "nt", F32)
            dw(f"{tag}_c_in_dw", "c_w_in", o, s["h1"], dgu2)
        dx, grads["g_mix_pre"][layer] = pre_bwd(f"{tag}_mix_pre", s["x0"], dh1, dx, w["g_mix_pre"], layer)

    for k in list(grads):
        if k.startswith("g_"):
            grads[k] = [g.reshape(-1) for g in grads[k]]
        grads[k] = jnp.stack(grads[k])
    nl, rows, cols = gb["w_ffn_down"].shape
    gb["w_ffn_down"] = gb["w_ffn_down"].reshape(nl, N_CHIPS, rows // N_CHIPS, cols).transpose(1, 0, 2, 3)
    gb["ab_w_in"] = _shardify(grads.pop("ab_w_in"), SHARD_DIM["ab_w_in"]).astype(BF16)
    return sq_cols, dx, grads, gb


CHIP_FLIPS = ((1, 0), (0, 1), (1, 1))
HBM_SPEC = pl.BlockSpec(memory_space=pltpu.HBM)
VMEM_SPEC = pl.BlockSpec(memory_space=pltpu.VMEM)


def _place():
    return lax.axis_index("x"), lax.axis_index("y"), lax.axis_index("c")


def _flip(v, f):
    return 1 - v if f else v


def _remote(src, dst, send_sem, recv_sem, target):
    return pltpu.make_async_remote_copy(src_ref=src, dst_ref=dst, send_sem=send_sem, recv_sem=recv_sem,
                                        device_id=target, device_id_type=MESH)


def _comm_call(name, body, ins, out_shape, n_sems):
    n = len(ins)
    return pl.pallas_call(
        body, name=name, in_specs=[HBM_SPEC] * n, out_specs=[HBM_SPEC] * len(out_shape), out_shape=out_shape,
        scratch_shapes=[pltpu.SemaphoreType.DMA((n, n_sems)), pltpu.SemaphoreType.DMA((n, n_sems))],
    )(*ins)


def gather_weights(shards):
    n = len(shards)

    def body(*refs):
        ins, outs = refs[:n], refs[n:2 * n]
        send_sems, recv_sems = refs[2 * n:]
        x, y, c = _place()
        p = 2 * x + y
        sibling = (x, y, 1 - c)
        chips = [(_flip(x, fx), _flip(y, fy)) for fx, fy in CHIP_FLIPS]
        first = [[_remote(ins[a].at[c], outs[a].at[p, c], send_sems.at[a, k], recv_sems.at[a, k], (qx, qy, c))
                  for k, (qx, qy) in enumerate(chips)] for a in range(n)]
        for a in range(n):
            for cp in first[a]:
                cp.start()
        passed = []
        for a in range(n):
            for k, (qx, qy) in enumerate(chips):
                landed = outs[a].at[2 * qx + qy, c]
                _remote(landed, landed, send_sems.at[a, k], recv_sems.at[a, k], (qx, qy, c)).wait_recv()
                cp = _remote(landed, landed, send_sems.at[a, 3 + k], recv_sems.at[a, 3 + k], sibling)
                cp.start()
                passed.append(cp)
        for a in range(n):
            for k, (qx, qy) in enumerate(chips):
                theirs = outs[a].at[2 * qx + qy, 1 - c]
                _remote(theirs, theirs, send_sems.at[a, 3 + k], recv_sems.at[a, 3 + k], sibling).wait_recv()
        for a in range(n):
            for cp in first[a]:
                cp.wait_send()
        for cp in passed:
            cp.wait_send()

    out_shape = [jax.ShapeDtypeStruct((N_CHIPS, *s.shape), s.dtype) for s in shards]
    return _comm_call("gather_weights", body, shards, out_shape, 6)


def swap_with_sibling(name, blocks):
    n = len(blocks)

    def body(*refs):
        ins, outs = refs[:n], refs[n:2 * n]
        send_sems, recv_sems = refs[2 * n:]
        x, y, c = _place()
        cps = [_remote(ins[a], outs[a], send_sems.at[a, 0], recv_sems.at[a, 0], (x, y, 1 - c)) for a in range(n)]
        for cp in cps:
            cp.start()
        for cp in cps:
            cp.wait()

    return _comm_call(name, body, blocks, [jax.ShapeDtypeStruct(b.shape, b.dtype) for b in blocks], 1)


def swap_layer_halves(bufs):
    n = len(bufs)

    def body(*refs):
        ins, outs = refs[:n], refs[n:2 * n]
        send_sems, recv_sems = refs[2 * n:]
        x, y, c = _place()
        cps = []
        for a in range(n):
            hl = bufs[a].shape[1] // 2
            cps.append(_remote(ins[a].at[:, pl.ds((1 - c) * hl, hl)], outs[a], send_sems.at[a, 0], recv_sems.at[a, 0],
                               (x, y, 1 - c)))
        for cp in cps:
            cp.start()
        for cp in cps:
            cp.wait()

    out_shape = [jax.ShapeDtypeStruct((b.shape[0], b.shape[1] // 2, *b.shape[2:]), b.dtype) for b in bufs]
    return _comm_call("swap_grad_halves", body, bufs, out_shape, 1)


def exchange_chips(blocks):
    n = len(blocks)

    def body(*refs):
        ins, outs = refs[:n], refs[n:2 * n]
        send_sems, recv_sems = refs[2 * n:]
        x, y, c = _place()
        p = 2 * x + y
        cps = []
        for a in range(n):
            for k, (fx, fy) in enumerate(CHIP_FLIPS):
                qx, qy = _flip(x, fx), _flip(y, fy)
                cps.append(_remote(ins[a].at[2 * qx + qy], outs[a].at[p], send_sems.at[a, k], recv_sems.at[a, k],
                                   (qx, qy, c)))
        for cp in cps:
            cp.start()
        for cp in cps:
            cp.wait()

    return _comm_call("exchange_chips", body, blocks, [jax.ShapeDtypeStruct(b.shape, b.dtype) for b in blocks], 3)


DEVICE_FLIPS = tuple((fx, fy, fc) for fx in (0, 1) for fy in (0, 1) for fc in (0, 1))[1:]


def gather_small(name, v, reduce):
    r, cdim = v.shape
    n_dev = 8

    def body(v_ref, out_ref, *scratch):
        buf = scratch[0] if reduce else out_ref
        send_sems, recv_sems = scratch[-2:]
        x, y, c = _place()
        me = 4 * x + 2 * y + c
        buf[me] = v_ref[...]
        cps = []
        for k, (fx, fy, fc) in enumerate(DEVICE_FLIPS):
            cps.append(_remote(v_ref, buf.at[me], send_sems.at[k], recv_sems.at[k],
                               (_flip(x, fx), _flip(y, fy), _flip(c, fc))))
        for cp in cps:
            cp.start()
        for cp in cps:
            cp.wait()
        if reduce:
            total = buf[0]
            for d in range(1, n_dev):
                total = total + buf[d]
            out_ref[...] = total

    scratch = [pltpu.SemaphoreType.DMA((7,)), pltpu.SemaphoreType.DMA((7,))]
    if reduce:
        scratch = [pltpu.VMEM((n_dev, r, cdim), F32)] + scratch
    out_shape = jax.ShapeDtypeStruct((r, cdim) if reduce else (n_dev, r, cdim), F32)
    return pl.pallas_call(body, name=name, in_specs=[VMEM_SPEC], out_specs=VMEM_SPEC, out_shape=out_shape,
                          scratch_shapes=scratch)(v)


def pair_sum(name, buf, got, core):
    _, nl, rows, cols = buf.shape
    hl = nl // 2
    tr = _tile(rows, (256, 128, 64, 32, 16))

    def kern(core_ref, a_ref, b_ref, o_ref):
        o_ref[...] = (a_ref[...].astype(F32) + b_ref[...].astype(F32)).astype(BF16)

    grid_spec = pltpu.PrefetchScalarGridSpec(
        num_scalar_prefetch=1, grid=(hl, rows // tr),
        in_specs=[pl.BlockSpec((N_CHIPS, None, tr, cols), lambda l, i, cr: (0, cr[0] * hl + l, i, 0)),
                  pl.BlockSpec((N_CHIPS, None, tr, cols), lambda l, i, cr: (0, l, i, 0))],
        out_specs=pl.BlockSpec((N_CHIPS, None, tr, cols), lambda l, i, cr: (0, l, i, 0)))
    return pl.pallas_call(
        kern, name=name, grid_spec=grid_spec, out_shape=jax.ShapeDtypeStruct(got.shape, BF16),
        compiler_params=pltpu.CompilerParams(dimension_semantics=("parallel", "parallel"),
                                             vmem_limit_bytes=VMEM_LIMIT_BYTES),
    )(core, buf, got)


def chip_sum(name, mine, parts, chip):
    _, hx, yd = parts.shape
    tr = _tile(hx, (256, 128, 64, 32, 16))

    def kern(chip_ref, m_ref, p_ref, o_ref):
        total = None
        for q in range(N_CHIPS):
            term = jnp.where(chip_ref[0] == q, m_ref[...], p_ref[q]).astype(F32)
            total = term if total is None else total + term
        o_ref[...] = total

    grid_spec = pltpu.PrefetchScalarGridSpec(
        num_scalar_prefetch=1, grid=(hx // tr,),
        in_specs=[pl.BlockSpec((None, tr, yd), lambda i, cr: (cr[0], i, 0)),
                  pl.BlockSpec((N_CHIPS, tr, yd), lambda i, cr: (0, i, 0))],
        out_specs=pl.BlockSpec((tr, yd), lambda i, cr: (i, 0)))
    return pl.pallas_call(
        kern, name=name, grid_spec=grid_spec, out_shape=jax.ShapeDtypeStruct((hx, yd), F32),
        compiler_params=pltpu.CompilerParams(dimension_semantics=("parallel",), vmem_limit_bytes=VMEM_LIMIT_BYTES),
    )(chip, mine, parts)


WEIGHTS = ("g_mix_pre", "g_mix_post", "g_cross_pre", "g_mem", "g_cross_post", "g_ffn_pre", "g_ffn_post", "w_xq", "w_xkv",
           "w_xo", "w_ffn_gu", "w_ffn_down", "ab_w_in", "ab_b_f", "ab_conv_w", "ab_w_out", "c_w_in", "c_conv_w",
           "c_conv_b", "c_w_a", "c_b_a", "c_w_i", "c_b_i", "c_lam", "c_w_out")
SHARD_DIM = {"w_xq": 1, "w_xkv": 2, "w_xo": 1, "w_ffn_gu": 2, "w_ffn_down": 1, "ab_w_in": 2, "ab_conv_w": 2,
             "ab_w_out": 1, "c_w_in": 2, "c_conv_w": 2, "c_conv_b": 1, "c_w_a": 2, "c_b_a": 2, "c_w_i": 2, "c_b_i": 2,
             "c_lam": 1, "c_w_out": 1}
BIG = ("w_xq", "w_xkv", "w_xo", "w_ffn_gu", "w_ffn_down", "ab_w_in", "ab_w_out", "c_w_in", "c_w_a", "c_w_i", "c_w_out")
SMALL_SHARDED = ("ab_conv_w", "c_conv_w", "c_conv_b", "c_b_a", "c_b_i", "c_lam")
REPLICATED = ("g_mix_pre", "g_mix_post", "g_cross_pre", "g_mem", "g_cross_post", "g_ffn_pre", "g_ffn_post", "ab_b_f")
PACK_COLS = 1024


def _unshard(g, d):
    shard = g.shape[1:]
    return jnp.moveaxis(g, 0, d).reshape(shard[:d] + (N_CHIPS * shard[d],) + shard[d + 1:])


def _shardify(full, d):
    s = full.shape
    return jnp.moveaxis(full.reshape(s[:d] + (N_CHIPS, s[d] // N_CHIPS) + s[d + 1:]), d, 0)


def _pack(arrays, rows):
    flat = jnp.concatenate([a.reshape(-1).astype(F32) for a in arrays])
    return jnp.pad(flat, (0, rows * PACK_COLS - flat.shape[0])).reshape(rows, PACK_COLS)


def _unpack(packed, shapes):
    flat = packed.reshape(-1)
    out, at = [], 0
    for s in shapes:
        size = math.prod(s)
        out.append(flat[at:at + size].reshape(s))
        at += size
    return out


def _rows_for(shapes):
    return -(-sum(math.prod(s) for s in shapes) // (8 * PACK_COLS)) * 8


def kernel(x, mem, g_mix_pre, g_mix_post, g_cross_pre, g_mem, g_cross_post, g_ffn_pre, g_ffn_post, w_xq, w_xkv, w_xo, w_ffn_gu, w_ffn_down, ab_w_in, ab_b_f, ab_conv_w, ab_w_out, c_w_in, c_conv_w, c_conv_b, c_w_a, c_b_a, c_w_i, c_b_i, c_lam, c_w_out, loss_target, m_g_mix_pre, m_g_mix_post, m_g_cross_pre, m_g_mem, m_g_cross_post, m_g_ffn_pre, m_g_ffn_post, m_w_xq, m_w_xkv, m_w_xo, m_w_ffn_gu, m_w_ffn_down, m_ab_w_in, m_ab_b_f, m_ab_conv_w, m_ab_w_out, m_c_w_in, m_c_conv_w, m_c_conv_b, m_c_w_a, m_c_b_a, m_c_w_i, m_c_b_i, m_c_lam, m_c_w_out, v_g_mix_pre, v_g_mix_post, v_g_cross_pre, v_g_mem, v_g_cross_post, v_g_ffn_pre, v_g_ffn_post, v_w_xq, v_w_xkv, v_w_xo, v_w_ffn_gu, v_w_ffn_down, v_ab_w_in, v_ab_b_f, v_ab_conv_w, v_ab_w_out, v_c_w_in, v_c_conv_w, v_c_conv_b, v_c_w_a, v_c_b_a, v_c_w_i, v_c_b_i, v_c_lam, v_c_w_out):
    given = dict(locals())
    w = {n: given[n] for n in WEIGHTS}
    m_in = {n: given["m_" + n] for n in WEIGHTS}
    v_in = {n: given["v_" + n] for n in WEIGHTS}
    xi, yi, ci = _place()
    chip = 2 * xi + yi

    def as_halves(a):
        return a.reshape(2, -1, a.shape[-1]) if a.ndim > 1 else a

    gathered = gather_weights([as_halves(w[n].astype(BF16)) for n in BIG])
    full = {}
    for n, g in zip(BIG, gathered):
        slot = lax.broadcasted_iota(jnp.int32, (N_CHIPS,) + (1,) * w[n].ndim, 0)
        g = jnp.where(slot == chip, w[n].astype(BF16)[None], g.reshape(N_CHIPS, *w[n].shape))
        full[n] = _unshard(g, SHARD_DIM[n])
    small_shapes = [w[n].shape for n in SMALL_SHARDED]
    rows_w = _rows_for(small_shapes)
    every = gather_small("gather_small_weights", _pack([w[n] for n in SMALL_SHARDED], rows_w), reduce=False)
    per_chip = every[0::2].reshape(N_CHIPS, -1)
    at = 0
    for n, s in zip(SMALL_SHARDED, small_shapes):
        size = math.prod(s)
        full[n] = _unshard(per_chip[:, at:at + size].reshape(N_CHIPS, *s), SHARD_DIM[n])
        at += size
    for n in REPLICATED:
        full[n] = w[n]

    sq_cols, dx, grads, big = local_step(x[0], mem[0], loss_target[0], full)
    loss = lax.psum(0.5 / D_MODEL * jnp.sum(sq_cols), ("x", "y", "c"))

    core_arr, chip_arr = ci.reshape(1).astype(jnp.int32), chip.reshape(1).astype(jnp.int32)
    bufs = [big[n].reshape(N_CHIPS, w[n].shape[0], -1, w[n].shape[-1]) for n in BIG]
    got = swap_layer_halves(bufs)
    chip_parts = [pair_sum(f"pair_sum_{n}", b, g, core_arr).reshape(N_CHIPS, -1, b.shape[-1])
                  for n, b, g in zip(BIG, bufs, got)]
    parts = exchange_chips(chip_parts)
    halves = [chip_sum(f"chip_sum_{n}", mine, p, chip_arr) for n, mine, p in zip(BIG, chip_parts, parts)]
    others = swap_with_sibling("swap_reduced_halves", halves)
    grad_out = {}
    for n, mine, theirs in zip(BIG, halves, others):
        lo, hi = jnp.where(ci == 0, mine, theirs), jnp.where(ci == 0, theirs, mine)
        grad_out[n] = jnp.concatenate([lo, hi], axis=0).reshape(w[n].shape)

    small_names = REPLICATED + SMALL_SHARDED
    small_full_shapes = [grads[n].shape for n in small_names]
    total = gather_small("reduce_small_grads", _pack([grads[n] for n in small_names], _rows_for(small_full_shapes)),
                         reduce=True)
    for n, g in zip(small_names, _unpack(total, small_full_shapes)):
        if n in SHARD_DIM:
            g = lax.dynamic_index_in_dim(_shardify(g, SHARD_DIM[n]), chip, axis=0, keepdims=False)
        grad_out[n] = g

    delta, new_m, new_v = {}, {}, {}
    for n in BIG:
        two_d = lambda a: a.reshape(-1, a.shape[-1])
        d, m2, v2 = adamw(f"adamw_{n}", two_d(w[n]), two_d(grad_out[n]), two_d(m_in[n]), two_d(v_in[n]))
        delta[n], new_m[n], new_v[n] = (a.reshape(w[n].shape) for a in (d, m2, v2))
    shapes = [w[n].shape for n in small_names]
    rows = _rows_for(shapes)
    packed = [_pack([src[n] for n in small_names], rows) for src in (w, grad_out, m_in, v_in)]
    for dst, res in zip((delta, new_m, new_v), adamw("adamw_small", *packed)):
        for n, a in zip(small_names, _unpack(res, shapes)):
            dst[n] = a

    return (loss, dx[None], *[grad_out[n] for n in WEIGHTS], *[delta[n] for n in WEIGHTS],
            *[new_m[n] for n in WEIGHTS], *[new_v[n] for n in WEIGHTS])
```

```python
import functools
import math

import jax
import jax.numpy as jnp
from jax import lax
from jax.experimental import pallas as pl
from jax.experimental.pallas import tpu as pltpu

F32, BF16 = jnp.float32, jnp.bfloat16
D_MODEL = 1024
EPS = 1e-6
NEG_INF = -1e30
FOX_HEADS, FOX_HEAD_DIM, FOX_WIDTH = 8, 64, 512
SC_WIDTH = 512
AB_IN = 3 * FOX_WIDTH + FOX_HEADS + 3 * SC_WIDTH
AB_IN_PAD = 3200
LRU_BW, LRU_BLOCKS = 256, 4
RG_C = 8.0
MEM_HEADS, MEM_HEAD_DIM = 4, 256
ADAM_LR, ADAM_B1, ADAM_B2, ADAM_EPS, ADAM_WD, ADAM_STEP = 0.001, 0.9, 0.999, 1e-08, 0.01, 10
N_CHIPS = 4
MESH = pl.DeviceIdType.MESH
VMEM_LIMIT_BYTES = 48 * 1024 * 1024
MM_OPERAND_TILE_BYTES = 7 * 1024 * 1024

NN = (((1,), (0,)), ((), ()))
NT = (((1,), (1,)), ((), ()))
TN = (((0,), (0,)), ((), ()))


def _dot(a, b, dn=NN):
    return lax.dot_general(a.astype(BF16), b.astype(BF16), dn, preferred_element_type=F32)


def _tile(n, prefs):
    for p in prefs:
        if n % p == 0:
            return p
    return n


def _pcall(name, kern, grid, ins, in_specs, out_shape, out_specs, sem):
    return pl.pallas_call(
        kern, name=name, grid=grid, in_specs=in_specs, out_specs=out_specs, out_shape=out_shape,
        compiler_params=pltpu.CompilerParams(dimension_semantics=sem, vmem_limit_bytes=VMEM_LIMIT_BYTES),
    )(*ins)


def mm(name, a, b, mode, out_dtype, reduce_layout=False):
    if mode == "nn":
        (m, k), n = a.shape, b.shape[1]
    elif mode == "nt":
        (m, k), n = a.shape, b.shape[0]
    else:
        (k, m), n = a.shape, b.shape[1]
    if reduce_layout:
        tm, tn = m // 2, n // N_CHIPS
    else:
        tn = _tile(n, ((1024,) if mode == "tn" else ()) + (512, 640, 256, 128))
        tm = next(c for c in (2048, 1024, 512, 256, 128, m)
                  if m % c == 0 and 2 * c * k <= MM_OPERAND_TILE_BYTES and 4 * c * tn <= MM_OPERAND_TILE_BYTES)
    dn = {"nn": NN, "nt": NT, "tn": TN}[mode]

    def kern(a_ref, b_ref, o_ref):
        o_ref[...] = _dot(a_ref[...], b_ref[...], dn).astype(o_ref.dtype)

    a_spec = pl.BlockSpec((k, tm), lambda i, j: (0, i)) if mode == "tn" else pl.BlockSpec((tm, k), lambda i, j: (i, 0))
    b_spec = pl.BlockSpec((tn, k), lambda i, j: (j, 0)) if mode == "nt" else pl.BlockSpec((k, tn), lambda i, j: (0, j))
    if reduce_layout:
        out_shape = jax.ShapeDtypeStruct((N_CHIPS, 2, tm, tn), out_dtype)
        o_spec = pl.BlockSpec((None, None, tm, tn), lambda i, j: (j, i, 0, 0))
    else:
        out_shape = jax.ShapeDtypeStruct((m, n), out_dtype)
        o_spec = pl.BlockSpec((tm, tn), lambda i, j: (i, j))
    return _pcall(name, kern, (m // tm, n // tn), (a, b), [a_spec, b_spec], out_shape, o_spec, ("parallel", "parallel"))


def rowwise(name, body, rows, params, outs, accs=(), tr=256):
    t = rows[0].shape[0]
    tr = min(tr, t)
    nr, npar, no = len(rows), len(params), len(outs)

    def kern(*refs):
        acc_refs = refs[nr + npar + no:]
        if acc_refs:
            @pl.when(pl.program_id(0) == 0)
            def _():
                for ar in acc_refs:
                    ar[...] = jnp.zeros_like(ar)
        body(refs[:nr], refs[nr:nr + npar], refs[nr + npar:nr + npar + no], acc_refs)

    in_specs = [pl.BlockSpec((tr, x.shape[1]), lambda i: (i, 0)) for x in rows]
    in_specs += [pl.BlockSpec(p.shape, lambda i: (0, 0)) for p in params]
    out_specs = [pl.BlockSpec((tr, c), lambda i: (i, 0)) for c, _ in outs]
    out_specs += [pl.BlockSpec(s, lambda i: (0, 0)) for s in accs]
    out_shape = [jax.ShapeDtypeStruct((t, c), dt) for c, dt in outs]
    out_shape += [jax.ShapeDtypeStruct(s, F32) for s in accs]
    return _pcall(name, kern, (t // tr,), (*rows, *params), in_specs, out_shape, out_specs,
                  ("arbitrary",) if accs else ("parallel",))


def _rms_stats(x):
    r = lax.rsqrt(jnp.mean(x * x, axis=-1, keepdims=True) + EPS)
    return r, x * r


def _rms_bwd(xh, r, g, dy):
    dxh = dy * g
    dx = r * (dxh - xh * jnp.mean(dxh * xh, axis=-1, keepdims=True))
    return dx, jnp.sum(dy * xh, axis=0, keepdims=True)


def rms_pre(name, x, gains, layer):
    def body(r, p, o, a):
        _, xh = _rms_stats(r[0][...])
        o[0][...] = (xh * p[0][layer:layer + 1, :]).astype(BF16)
    return rowwise(name, body, [x], [gains], [(x.shape[1], BF16)])[0]


def post_add(name, x, y, gains, layer):
    def body(r, p, o, a):
        _, yh = _rms_stats(r[1][...])
        o[0][...] = r[0][...] + yh * p[0][layer:layer + 1, :]
    return rowwise(name, body, [x, y], [gains], [(x.shape[1], F32)])[0]


def post_bwd(name, y, dx, gains, layer):
    def body(r, p, o, a):
        rr, yh = _rms_stats(r[0][...])
        dy, dg = _rms_bwd(yh, rr, p[0][layer:layer + 1, :], r[1][...])
        o[0][...] = dy.astype(BF16)
        a[0][...] += dg
    c = y.shape[1]
    return rowwise(name, body, [y, dx], [gains], [(c, BF16)], [(1, c)])


def pre_bwd(name, x, dh, dx_res, gains, layer):
    def body(r, p, o, a):
        rr, xh = _rms_stats(r[0][...])
        dx, dg = _rms_bwd(xh, rr, p[0][layer:layer + 1, :], r[1][...])
        o[0][...] = r[2][...] + dx
        a[0][...] += dg
    c = x.shape[1]
    return rowwise(name, body, [x, dh, dx_res], [gains], [(c, F32)], [(1, c)])


def gain_bwd(name, x, dh):
    def body(r, p, o, a):
        _, xh = _rms_stats(r[0][...])
        a[0][...] += jnp.sum(r[1][...] * xh, axis=0, keepdims=True)
    return rowwise(name, body, [x, dh], [], [], [(1, x.shape[1])])[0]


def _sigmoid(z):
    return 1.0 / (1.0 + jnp.exp(-z))


def swiglu_fwd(name, gu):
    f = gu.shape[1] // 2

    def body(r, p, o, a):
        g = r[0][:, :f].astype(F32)
        u = r[0][:, f:].astype(F32)
        o[0][...] = (g * _sigmoid(g) * u).astype(BF16)
    return rowwise(name, body, [gu], [], [(f, BF16)])[0]


def swiglu_bwd(name, gu, da):
    f = gu.shape[1] // 2

    def body(r, p, o, a):
        g = r[0][:, :f].astype(F32)
        u = r[0][:, f:].astype(F32)
        d = r[1][...].astype(F32)
        sg = _sigmoid(g)
        o[0][:, :f] = (d * u * sg * (1.0 + g * (1.0 - sg))).astype(BF16)
        o[0][:, f:] = (d * g * sg).astype(BF16)
    return rowwise(name, body, [gu, da], [], [(2 * f, BF16)])[0]


def loss_head(name, y, target):
    c = y.shape[1]

    def body(r, p, o, a):
        e = r[0][...] - r[1][...]
        o[0][...] = e * (1.0 / c)
        a[0][...] += jnp.sum(e * e, axis=0, keepdims=True)
    return rowwise(name, body, [y, target], [], [(c, F32)], [(1, c)])


def adamw(name, w, g, m, v):
    c = w.shape[1]

    def body(r, p, o, a):
        wv, gv, mv, vv = (x[...] for x in r)
        m2 = ADAM_B1 * mv + (1.0 - ADAM_B1) * gv
        v2 = ADAM_B2 * vv + (1.0 - ADAM_B2) * (gv * gv)
        m_hat = m2 / (1.0 - ADAM_B1 ** ADAM_STEP)
        v_hat = v2 / (1.0 - ADAM_B2 ** ADAM_STEP)
        o[0][...] = -ADAM_LR * (m_hat / (jnp.sqrt(v_hat) + ADAM_EPS) + ADAM_WD * wv)
        o[1][...] = m2
        o[2][...] = v2
    tr = _tile(w.shape[0], (256, 128, 64, 32, 16, 8))
    return rowwise(name, body, [w, g, m, v], [], [(c, F32)] * 3, tr=tr)


def colwise(name, body, cols, params, outs, pouts=(), tc=128):
    t = cols[0][0].shape[0]
    c = params[0].shape[1] if params else cols[0][0].shape[1]
    nc, npar, no = len(cols), len(params), len(outs)

    def kern(*refs):
        body(refs[:nc], refs[nc:nc + npar], refs[nc + npar:nc + npar + no], refs[nc + npar + no:])

    in_specs = [pl.BlockSpec((t, tc), functools.partial(lambda j, off: (0, j + off), off=off)) for _, off in cols]
    in_specs += [pl.BlockSpec((p.shape[0], tc), lambda j: (0, j)) for p in params]
    out_specs = [pl.BlockSpec((t, tc), lambda j: (0, j)) for _ in outs]
    out_specs += [pl.BlockSpec((r, tc), lambda j: (0, j)) for r in pouts]
    out_shape = [jax.ShapeDtypeStruct((t, c), dt) for dt in outs]
    out_shape += [jax.ShapeDtypeStruct((r, c), F32) for r in pouts]
    return _pcall(name, kern, (c // tc,), (*[x for x, _ in cols], *params), in_specs, out_shape, out_specs,
                  ("parallel",))


def _row_index(shape):
    return lax.broadcasted_iota(jnp.int32, shape, 0)


def _shift_down(x, d, rows):
    return jnp.where(rows >= d, pltpu.roll(x, d, 0), 0.0)


def _shift_up(x, d, rows):
    t = x.shape[0]
    return jnp.where(rows < t - d, pltpu.roll(x, t - d, 0), 0.0)


def sconv_fwd(name, proj, col0, conv_w, tc=128):
    nb = SC_WIDTH // tc

    def body(cl, p, o, po):
        b, c, u = (x[...] for x in cl)
        rows = _row_index(b.shape)
        w = p[0][...]
        z = c * u
        conv = w[2:3] * z + w[1:2] * _shift_down(z, 1, rows) + w[0:1] * _shift_down(z, 2, rows)
        o[0][...] = (b * conv).astype(BF16)
    return colwise(name, body, [(proj, col0), (proj, col0 + nb), (proj, col0 + 2 * nb)], [conv_w], [BF16], tc=tc)[0]


def sconv_bwd(name, proj, col0, conv_w, dyb, dcol0, tc=128):
    nb = SC_WIDTH // tc

    def body(cl, p, o, po):
        b, c, u, dy = (x[...] for x in cl)
        rows = _row_index(b.shape)
        w = p[0][...]
        z = c * u
        z1, z2 = _shift_down(z, 1, rows), _shift_down(z, 2, rows)
        conv = w[2:3] * z + w[1:2] * z1 + w[0:1] * z2
        dconv = dy * b
        dz = w[2:3] * dconv + w[1:2] * _shift_up(dconv, 1, rows) + w[0:1] * _shift_up(dconv, 2, rows)
        o[0][...] = (dy * conv).astype(BF16)
        o[1][...] = (dz * u).astype(BF16)
        o[2][...] = (dz * c).astype(BF16)
        po[0][0:1, :] = jnp.sum(dconv * z2, axis=0, keepdims=True)
        po[0][1:2, :] = jnp.sum(dconv * z1, axis=0, keepdims=True)
        po[0][2:3, :] = jnp.sum(dconv * z, axis=0, keepdims=True)
    return colwise(name, body, [(proj, col0), (proj, col0 + nb), (proj, col0 + 2 * nb), (dyb, dcol0)], [conv_w],
                   [BF16, BF16, BF16], [3], tc=tc)


def _expm1(x):
    series = x * (1.0 + 0.5 * x * (1.0 + x * (1.0 / 3.0) * (1.0 + 0.25 * x * (1.0 + 0.2 * x))))
    return jnp.where(jnp.abs(x) < 0.05, series, jnp.exp(x) - 1.0)


def _log1p(x):
    series = x * (1.0 - x * (0.5 - x * (1.0 / 3.0 - 0.25 * x)))
    return jnp.where(jnp.abs(x) < 0.01, series, jnp.log(1.0 + x))


def _softplus_neg(lam):
    sp = jnp.maximum(-lam, 0.0) + _log1p(jnp.exp(-jnp.abs(lam)))
    return sp, -_sigmoid(-lam)


GELU_C = math.sqrt(2.0 / math.pi)


def _gelu(x):
    th = jnp.tanh(GELU_C * (x + 0.044715 * x * x * x))
    val = 0.5 * x * (1.0 + th)
    grad = 0.5 * (1.0 + th) + 0.5 * x * (1.0 - th * th) * GELU_C * (1.0 + 3.0 * 0.044715 * x * x)
    return val, grad


def rg_conv_fwd(name, gu2, conv_w, conv_b, tc=128):
    nb = D_MODEL // tc

    def body(cl, p, o, po):
        u = cl[0][...]
        rows = _row_index(u.shape)
        w = p[0][...]
        o[0][...] = (w[3:4] * u + w[2:3] * _shift_down(u, 1, rows) + w[1:2] * _shift_down(u, 2, rows)
                     + w[0:1] * _shift_down(u, 3, rows) + p[1][...])
    return colwise(name, body, [(gu2, nb)], [conv_w, conv_b], [F32], tc=tc)[0]


def rg_conv_bwd(name, gu2, duc, conv_w, tc=128):
    nb = D_MODEL // tc

    def body(cl, p, o, po):
        u, d = cl[0][...], cl[1][...]
        rows = _row_index(u.shape)
        w = p[0][...]
        o[0][...] = (w[3:4] * d + w[2:3] * _shift_up(d, 1, rows) + w[1:2] * _shift_up(d, 2, rows)
                     + w[0:1] * _shift_up(d, 3, rows)).astype(BF16)
        for k in range(4):
            uk = u if k == 3 else _shift_down(u, 3 - k, rows)
            po[0][k:k + 1, :] = jnp.sum(d * uk, axis=0, keepdims=True)
        po[1][...] = jnp.sum(d, axis=0, keepdims=True)
    return colwise(name, body, [(gu2, nb), (duc, 0)], [conv_w], [BF16], [4, 1], tc=tc)


def rg_gates_fwd(name, uc, w_a, b_a, w_i, b_i, tr=512):
    t = uc.shape[0]
    tr = min(tr, t)

    def kern(u_ref, wa_ref, ba_ref, wi_ref, bi_ref, r_ref, i_ref):
        ub = u_ref[...].astype(BF16)
        r_ref[...] = _sigmoid(_dot(ub, wa_ref[...]) + ba_ref[...])
        i_ref[...] = _sigmoid(_dot(ub, wi_ref[...]) + bi_ref[...])

    blk = pl.BlockSpec((tr, LRU_BW), lambda n, i: (i, n))
    wspec = pl.BlockSpec((None, LRU_BW, LRU_BW), lambda n, i: (n, 0, 0))
    bspec = pl.BlockSpec((1, LRU_BW), lambda n, i: (0, n))
    return _pcall(name, kern, (LRU_BLOCKS, t // tr), (uc, w_a, b_a, w_i, b_i), [blk, wspec, bspec, wspec, bspec],
                  [jax.ShapeDtypeStruct(uc.shape, F32)] * 2, [blk, blk], ("parallel", "parallel"))


def rg_gates_bwd(name, uc, dzr, dzi, duc_part, w_a, w_i):
    t = uc.shape[0]
    rows = LRU_BW // N_CHIPS

    def kern(u_ref, dr_ref, di_ref, dp_ref, wa_ref, wi_ref, duc_ref, dwa_ref, dwi_ref):
        ub = u_ref[...].astype(BF16)
        dr, di = dr_ref[...], di_ref[...]
        dwa, dwi = _dot(ub, dr, TN), _dot(ub, di, TN)
        for p in range(N_CHIPS):
            dwa_ref[p] = dwa[p * rows:(p + 1) * rows].astype(dwa_ref.dtype)
            dwi_ref[p] = dwi[p * rows:(p + 1) * rows].astype(dwi_ref.dtype)
        duc_ref[...] = dp_ref[...] + _dot(dr, wa_ref[...], NT) + _dot(di, wi_ref[...], NT)

    blk = pl.BlockSpec((t, LRU_BW), lambda n: (0, n))
    wspec = pl.BlockSpec((None, LRU_BW, LRU_BW), lambda n: (n, 0, 0))
    gspec = pl.BlockSpec((N_CHIPS, None, rows, LRU_BW), lambda n: (0, n, 0, 0))
    gshape = jax.ShapeDtypeStruct((N_CHIPS, LRU_BLOCKS, rows, LRU_BW), BF16)
    return _pcall(name, kern, (LRU_BLOCKS,), (uc, dzr, dzi, duc_part, w_a, w_i), [blk, blk, blk, blk, wspec, wspec],
                  [jax.ShapeDtypeStruct(uc.shape, F32), gshape, gshape], [blk, gspec, gspec], ("parallel",))


def _rg_decay(r, lam):
    sp, dsp = _softplus_neg(lam)
    la = -RG_C * r * sp
    a = jnp.exp(la)
    sq = jnp.sqrt(-_expm1(2.0 * la))
    return sp, dsp, a, sq


def rg_scan_fwd(name, gu2, uc, r, i, lam, tc=128):
    def body(cl, p, o, po):
        gate, ucv, rv, iv = (x[...] for x in cl)
        t = gate.shape[0]
        rows = _row_index(gate.shape)
        _, _, a, sq = _rg_decay(rv, p[0][...])
        b = sq * (iv * ucv)
        d = 1
        while d < t:
            keep = rows >= d
            b = a * jnp.where(keep, pltpu.roll(b, d, 0), 0.0) + b
            a = a * jnp.where(keep, pltpu.roll(a, d, 0), 1.0)
            d *= 2
        o[0][...] = (_gelu(gate)[0] * b).astype(BF16)
        o[1][...] = b
    return colwise(name, body, [(gu2, 0), (uc, 0), (r, 0), (i, 0)], [lam], [BF16, F32], tc=tc)


def rg_scan_bwd(name, gu2, uc, r, i, hs, dy, lam, tc=128):
    def body(cl, p, o, po):
        gate, ucv, rv, iv, h, dyv = (x[...] for x in cl)
        t = gate.shape[0]
        rows = _row_index(gate.shape)
        sp, dsp, a, sq = _rg_decay(rv, p[0][...])
        gl, dgl = _gelu(gate)
        o[0][...] = (dyv * h * dgl).astype(BF16)
        g = dyv * gl
        am = _shift_up(a, 1, rows)
        d = 1
        while d < t:
            keep = rows < t - d
            g = am * jnp.where(keep, pltpu.roll(g, t - d, 0), 0.0) + g
            am = am * jnp.where(keep, pltpu.roll(am, t - d, 0), 0.0)
            d *= 2
        da = g * _shift_down(h, 1, rows)
        iu = iv * ucv
        d_iu = g * sq
        dla = da * a - (g * iu) * (a * a) / sq
        dzr = dla * (-RG_C * sp) * rv * (1.0 - rv)
        dzi = d_iu * ucv * iv * (1.0 - iv)
        o[1][...] = dzr.astype(BF16)
        o[2][...] = dzi.astype(BF16)
        o[3][...] = d_iu * iv
        po[0][...] = jnp.sum(dzr, axis=0, keepdims=True)
        po[1][...] = jnp.sum(dzi, axis=0, keepdims=True)
        po[2][...] = jnp.sum(dla * rv, axis=0, keepdims=True) * (-RG_C) * dsp
    return colwise(name, body, [(gu2, 0), (uc, 0), (r, 0), (i, 0), (hs, 0), (dy, 0)], [lam],
                   [BF16, BF16, BF16, F32], [1, 1, 1], tc=tc)


def _split3(x):
    hi = x.astype(BF16)
    r1 = x - hi.astype(F32)
    mid = r1.astype(BF16)
    lo = (r1 - mid.astype(F32)).astype(BF16)
    return hi, mid, lo


def _tri_dot(x, tri):
    out = None
    for piece in _split3(x):
        term = lax.dot_general(piece, tri, NN, preferred_element_type=F32)
        out = term if out is None else out + term
    return out


def fox_gates_fwd(name, z_t, b_f):
    h, t = z_t.shape
    tb = min(512, t)

    def kern(z_ref, b_ref, o_ref):
        z = z_ref[...] + b_ref[...]
        logf = jnp.minimum(z, 0.0) - _log1p(jnp.exp(-jnp.abs(z)))
        src = lax.broadcasted_iota(jnp.int32, (t, tb), 0)
        dst = lax.broadcasted_iota(jnp.int32, (t, tb), 1) + pl.program_id(0) * tb
        o_ref[...] = _tri_dot(logf, (src <= dst).astype(BF16))

    return _pcall(name, kern, (t // tb,), (z_t, b_f),
                  [pl.BlockSpec((h, t), lambda j: (0, 0)), pl.BlockSpec((h, 1), lambda j: (0, 0))],
                  jax.ShapeDtypeStruct((h, t), F32), pl.BlockSpec((h, tb), lambda j: (0, j)), ("parallel",))


def fox_gates_bwd(name, z_t, b_f, dcum_t):
    h, t = z_t.shape
    tb = min(512, t)

    def kern(z_ref, b_ref, d_ref, dz_ref, db_ref):
        @pl.when(pl.program_id(0) == 0)
        def _():
            db_ref[...] = jnp.zeros_like(db_ref)
        src = lax.broadcasted_iota(jnp.int32, (t, tb), 0)
        dst = lax.broadcasted_iota(jnp.int32, (t, tb), 1) + pl.program_id(0) * tb
        dlogf = _tri_dot(d_ref[...], (src >= dst).astype(BF16))
        z = z_ref[...] + b_ref[...]
        dz = dlogf * _sigmoid(-z)
        dz_ref[...] = dz
        db_ref[...] += jnp.sum(dz, axis=1, keepdims=True)

    return _pcall(name, kern, (t // tb,), (z_t, b_f, dcum_t),
                  [pl.BlockSpec((h, tb), lambda j: (0, j)), pl.BlockSpec((h, 1), lambda j: (0, 0)),
                   pl.BlockSpec((h, t), lambda j: (0, 0))],
                  [jax.ShapeDtypeStruct((h, t), F32), jax.ShapeDtypeStruct((h, 1), F32)],
                  [pl.BlockSpec((h, tb), lambda j: (0, j)), pl.BlockSpec((h, 1), lambda j: (0, 0))], ("arbitrary",))


def _fox_spans(qs, k_ref, cr_ref, i, tq):
    n0 = i * tq
    sd = _dot(qs, k_ref[n0:n0 + tq, :], NT) - cr_ref[:, n0:n0 + tq]
    row = lax.broadcasted_iota(jnp.int32, (tq, tq), 0)
    col = lax.broadcasted_iota(jnp.int32, (tq, tq), 1)
    spans = [(n0, tq, jnp.where(row >= col, sd, NEG_INF))]
    if i > 0:
        spans.append((0, n0, _dot(qs, k_ref[0:n0, :], NT) - cr_ref[:, 0:n0]))
    return spans


def fox_fwd(name, q, k, v, cum_r, tq=256):
    h, t, dh = q.shape
    tq = min(tq, t)
    scale = FOX_HEAD_DIM ** -0.5

    def kern(q_ref, k_ref, v_ref, cr_ref, o_ref, lse_ref):
        for i in range(t // tq):
            rows = slice(i * tq, (i + 1) * tq)
            spans = _fox_spans(q_ref[rows, :] * scale, k_ref, cr_ref, i, tq)
            m = functools.reduce(jnp.maximum, [jnp.max(s, axis=-1, keepdims=True) for _, _, s in spans])
            l, acc = 0.0, 0.0
            for k0, kn, s in spans:
                p = jnp.exp(s - m)
                l = l + jnp.sum(p, axis=-1, keepdims=True)
                acc = acc + _dot(p, v_ref[k0:k0 + kn, :])
            o_ref[rows, :] = (acc / l).astype(o_ref.dtype)
            lse_ref[rows, :] = m + jnp.log(l)

    hspec = pl.BlockSpec((None, t, dh), lambda a: (a, 0, 0))
    cspec = pl.BlockSpec((None, t, 1), lambda a: (a, 0, 0))
    rspec = pl.BlockSpec((None, 1, t), lambda a: (a, 0, 0))
    return _pcall(name, kern, (h,), (q, k, v, cum_r), [hspec, hspec, hspec, rspec],
                  [jax.ShapeDtypeStruct((h, t, dh), BF16), jax.ShapeDtypeStruct((h, t, 1), F32)],
                  [hspec, cspec], ("parallel",))


def fox_bwd(name, q, k, v, do, lse, cum_r, tq=256):
    h, t, dh = q.shape
    tq = min(tq, t)
    scale = FOX_HEAD_DIM ** -0.5

    def kern(q_ref, k_ref, v_ref, do_ref, lse_ref, cr_ref, dq_ref, dk_ref, dv_ref, dc_ref):
        dk_ref[...] = jnp.zeros_like(dk_ref)
        dv_ref[...] = jnp.zeros_like(dv_ref)
        dc_ref[...] = jnp.zeros_like(dc_ref)
        for i in range(t // tq):
            rows = slice(i * tq, (i + 1) * tq)
            qs, dov, lse_v = q_ref[rows, :] * scale, do_ref[rows, :], lse_ref[rows, :]
            spans = _fox_spans(qs, k_ref, cr_ref, i, tq)
            probs = [jnp.exp(s - lse_v) for _, _, s in spans]
            dps = [_dot(dov, v_ref[k0:k0 + kn, :], NT) for k0, kn, _ in spans]
            rowdot = sum(jnp.sum(dp * p, axis=-1, keepdims=True) for dp, p in zip(dps, probs))
            dq = 0.0
            for (k0, kn, _), p, dp in zip(spans, probs, dps):
                ds = p * (dp - rowdot)
                dq = dq + _dot(ds, k_ref[k0:k0 + kn, :])
                dk_ref[k0:k0 + kn, :] += _dot(ds, qs, TN)
                dv_ref[k0:k0 + kn, :] += _dot(p, dov, TN)
                dc_ref[:, k0:k0 + kn] -= jnp.sum(ds, axis=0, keepdims=True)
            dq_ref[rows, :] = (dq * scale).astype(dq_ref.dtype)

    hspec = pl.BlockSpec((None, t, dh), lambda a: (a, 0, 0))
    cspec = pl.BlockSpec((None, t, 1), lambda a: (a, 0, 0))
    rspec = pl.BlockSpec((None, 1, t), lambda a: (a, 0, 0))
    return _pcall(name, kern, (h,), (q, k, v, do, lse, cum_r), [hspec, hspec, hspec, hspec, cspec, rspec],
                  [jax.ShapeDtypeStruct((h, t, dh), BF16), jax.ShapeDtypeStruct((h, t, dh), F32),
                   jax.ShapeDtypeStruct((h, t, dh), F32), jax.ShapeDtypeStruct((h, 1, t), F32)],
                  [hspec, hspec, hspec, rspec], ("parallel",))


def _xattn_probs(q, k):
    s = _dot(q, k, NT) * (MEM_HEAD_DIM ** -0.5)
    p = jnp.exp(s - jnp.max(s, axis=-1, keepdims=True))
    return p / jnp.sum(p, axis=-1, keepdims=True)


def xattn_fwd(name, q, kv, tq=512):
    t = q.shape[0]
    tq = min(tq, t)
    ml = kv.shape[0]

    def kern(q_ref, k_ref, v_ref, o_ref):
        o_ref[...] = _dot(_xattn_probs(q_ref[...], k_ref[...]), v_ref[...]).astype(o_ref.dtype)

    qspec = pl.BlockSpec((tq, MEM_HEAD_DIM), lambda i, a: (i, a))
    return _pcall(name, kern, (t // tq, MEM_HEADS), (q, kv, kv),
                  [qspec, pl.BlockSpec((ml, MEM_HEAD_DIM), lambda i, a: (0, a)),
                   pl.BlockSpec((ml, MEM_HEAD_DIM), lambda i, a: (0, MEM_HEADS + a))],
                  jax.ShapeDtypeStruct(q.shape, BF16), qspec, ("parallel", "parallel"))


def xattn_bwd(name, q, kv, do, tq=512):
    t = q.shape[0]
    tq = min(tq, t)
    ml = kv.shape[0]
    scale = MEM_HEAD_DIM ** -0.5

    def kern(q_ref, k_ref, v_ref, do_ref, dq_ref, dk_ref, dv_ref):
        @pl.when(pl.program_id(1) == 0)
        def _():
            dk_ref[...] = jnp.zeros_like(dk_ref)
            dv_ref[...] = jnp.zeros_like(dv_ref)
        qv, kv_, dov = q_ref[...], k_ref[...], do_ref[...]
        p = _xattn_probs(qv, kv_)
        dp = _dot(dov, v_ref[...], NT)
        ds = p * (dp - jnp.sum(dp * p, axis=-1, keepdims=True)) * scale
        dq_ref[...] = _dot(ds, kv_).astype(dq_ref.dtype)
        dk_ref[...] += _dot(ds, qv, TN)
        dv_ref[...] += _dot(p, dov, TN)

    qspec = pl.BlockSpec((tq, MEM_HEAD_DIM), lambda a, i: (i, a))
    kspec = pl.BlockSpec((ml, MEM_HEAD_DIM), lambda a, i: (0, a))
    return _pcall(name, kern, (MEM_HEADS, t // tq), (q, kv, kv, do),
                  [qspec, kspec, pl.BlockSpec((ml, MEM_HEAD_DIM), lambda a, i: (0, MEM_HEADS + a)), qspec],
                  [jax.ShapeDtypeStruct(q.shape, BF16), jax.ShapeDtypeStruct((ml, D_MODEL), F32),
                   jax.ShapeDtypeStruct((ml, D_MODEL), F32)],
                  [qspec, kspec, kspec], ("parallel", "arbitrary"))


def _heads(x):
    t = x.shape[0]
    return x.reshape(t, FOX_HEADS, FOX_HEAD_DIM).transpose(1, 0, 2)


def _unheads(x):
    return x.transpose(1, 0, 2).reshape(x.shape[1], FOX_WIDTH)


def _row_cut(dw):
    return dw.reshape(N_CHIPS, 2, dw.shape[0] // (2 * N_CHIPS), dw.shape[1])


def local_step(x, mem, target, w, reduce_hook=None):
    depth = w["g_mix_pre"].shape[0]
    t = x.shape[0]
    saved = []
    i1, i2, i3 = 3 * FOX_WIDTH, 3 * FOX_WIDTH + FOX_HEADS, AB_IN
    ncol = 128

    for layer in range(depth):
        s = {"x0": x}
        tag = f"l{layer}"
        h1 = rms_pre(f"{tag}_mix_pre", x, w["g_mix_pre"], layer)
        s["h1"] = h1
        if layer % 2 == 0:
            e = layer // 2
            w_in = jnp.pad(w["ab_w_in"][e], ((0, 0), (0, AB_IN_PAD - AB_IN)))
            proj = mm(f"{tag}_ab_in", h1, w_in, "nn", F32)
            qkv = proj[:, :i1].astype(BF16).reshape(t, 3, FOX_HEADS, FOX_HEAD_DIM).transpose(1, 2, 0, 3)
            z_t = proj[:, i1:i2].T
            b_f = w["ab_b_f"][e].reshape(FOX_HEADS, 1)
            cum_t = fox_gates_fwd(f"{tag}_fox_gates", z_t, b_f)
            cum_r = cum_t[:, None, :]
            oh, lse = fox_fwd(f"{tag}_fox", qkv[0], qkv[1], qkv[2], cum_r)
            bcu = proj[:, i2:i3]
            y_b = sconv_fwd(f"{tag}_sconv", bcu, 0, w["ab_conv_w"][e])
            ycat = jnp.concatenate([_unheads(oh), y_b], axis=1)
            y1 = mm(f"{tag}_ab_out", ycat, w["ab_w_out"][e], "nn", F32)
            s.update(w_in=w_in, qkv=qkv, z_t=z_t, b_f=b_f, cum_r=cum_r, lse=lse, bcu=bcu, ycat=ycat)
        else:
            o = layer // 2
            gu2 = mm(f"{tag}_c_in", h1, w["c_w_in"][o], "nn", F32)
            conv_b = w["c_conv_b"][o].reshape(1, -1)
            uc = rg_conv_fwd(f"{tag}_rg_conv", gu2, w["c_conv_w"][o], conv_b)
            b_a, b_i = w["c_b_a"][o].reshape(1, -1), w["c_b_i"][o].reshape(1, -1)
            r, i = rg_gates_fwd(f"{tag}_rg_gates", uc, w["c_w_a"][o], b_a, w["c_w_i"][o], b_i)
            lam = w["c_lam"][o].reshape(1, -1)
            ymix, hs = rg_scan_fwd(f"{tag}_rg_scan", gu2, uc, r, i, lam)
            y1 = mm(f"{tag}_c_out", ymix, w["c_w_out"][o], "nn", F32)
            s.update(gu2=gu2, uc=uc, r=r, i=i, lam=lam, hs=hs, ymix=ymix)
        s["y1"] = y1
        x = post_add(f"{tag}_mix_post", x, y1, w["g_mix_post"], layer)
        s["x1"] = x
        h2 = rms_pre(f"{tag}_cross_pre", x, w["g_cross_pre"], layer)
        m = rms_pre(f"{tag}_mem_pre", mem, w["g_mem"], layer)
        q = mm(f"{tag}_xq", h2, w["w_xq"][layer], "nn", BF16)
        kv = mm(f"{tag}_xkv", m, w["w_xkv"][layer], "nn", BF16)
        o_att = xattn_fwd(f"{tag}_xattn", q, kv)
        y2 = mm(f"{tag}_xo", o_att, w["w_xo"][layer], "nn", F32)
        s.update(h2=h2, m=m, q=q, kv=kv, o_att=o_att, y2=y2)
        x = post_add(f"{tag}_cross_post", x, y2, w["g_cross_post"], layer)
        s["x2"] = x
        h3 = rms_pre(f"{tag}_ffn_pre", x, w["g_ffn_pre"], layer)
        gu = mm(f"{tag}_ffn_gu", h3, w["w_ffn_gu"][layer], "nn", BF16)
        act = swiglu_fwd(f"{tag}_swiglu", gu)
        y3 = mm(f"{tag}_ffn_down", act, w["w_ffn_down"][layer], "nn", F32)
        s.update(h3=h3, gu=gu, act=act, y3=y3)
        x = post_add(f"{tag}_ffn_post", x, y3, w["g_ffn_post"], layer)
        saved.append(s)

    dx, sq_cols = loss_head("loss_head", x, target)

    grads = {k: [None] * v.shape[0] for k, v in w.items() if k not in BIG}
    big, token = {}, None

    def dw(name, a, b, cols_cut=False):
        return mm(name, a, b, "tn", BF16, reduce_layout=True) if cols_cut else _row_cut(mm(name, a, b, "tn", BF16))

    for layer in reversed(range(depth)):
        s = saved[layer]
        tag = f"b{layer}"
        lg = {}
        g_ffn_post = w["g_ffn_post"] if token is None else w["g_ffn_post"] + token
        dy3, grads["g_ffn_post"][layer] = post_bwd(f"{tag}_ffn_post", s["y3"], dx, g_ffn_post, layer)
        dact = mm(f"{tag}_ffn_down_dx", dy3, w["w_ffn_down"][layer], "nt", BF16)
        lg["w_ffn_down"] = dw(f"{tag}_ffn_down_dw", s["act"], dy3)
        dgu = swiglu_bwd(f"{tag}_swiglu", s["gu"], dact)
        dh3 = mm(f"{tag}_ffn_gu_dx", dgu, w["w_ffn_gu"][layer], "nt", F32)
        lg["w_ffn_gu"] = dw(f"{tag}_ffn_gu_dw", s["h3"], dgu, cols_cut=True)
        dx, grads["g_ffn_pre"][layer] = pre_bwd(f"{tag}_ffn_pre", s["x2"], dh3, dx, w["g_ffn_pre"], layer)
        dy2, grads["g_cross_post"][layer] = post_bwd(f"{tag}_cross_post", s["y2"], dx, w["g_cross_post"], layer)
        do = mm(f"{tag}_xo_dx", dy2, w["w_xo"][layer], "nt", BF16)
        lg["w_xo"] = dw(f"{tag}_xo_dw", s["o_att"], dy2)
        dq, dk, dv = xattn_bwd(f"{tag}_xattn", s["q"], s["kv"], do)
        dh2 = mm(f"{tag}_xq_dx", dq, w["w_xq"][layer], "nt", F32)
        lg["w_xq"] = dw(f"{tag}_xq_dw", s["h2"], dq)
        dkv = jnp.concatenate([dk, dv], axis=1).astype(BF16)
        dm = mm(f"{tag}_xkv_dx", dkv, w["w_xkv"][layer], "nt", F32)
        lg["w_xkv"] = dw(f"{tag}_xkv_dw", s["m"], dkv, cols_cut=True)
        grads["g_mem"][layer] = gain_bwd(f"{tag}_mem_pre", mem, dm)
        dx, grads["g_cross_pre"][layer] = pre_bwd(f"{tag}_cross_pre", s["x1"], dh2, dx, w["g_cross_pre"], layer)
        dy1, grads["g_mix_post"][layer] = post_bwd(f"{tag}_mix_post", s["y1"], dx, w["g_mix_post"], layer)
        if layer % 2 == 0:
            e = layer // 2
            dycat = mm(f"{tag}_ab_out_dx", dy1, w["ab_w_out"][e], "nt", F32)
            lg["ab_w_out"] = dw(f"{tag}_ab_out_dw", s["ycat"], dy1)
            do_h = _heads(dycat[:, :FOX_WIDTH].astype(BF16))
            qkv = s["qkv"]
            dqh, dkh, dvh, dcum = fox_bwd(f"{tag}_fox", qkv[0], qkv[1], qkv[2], do_h, s["lse"], s["cum_r"])
            dz_t, db_f = fox_gates_bwd(f"{tag}_fox_gates", s["z_t"], s["b_f"], dcum.reshape(FOX_HEADS, t))
            grads["ab_b_f"][e] = db_f.reshape(FOX_HEADS)
            db, dc, du, dconv_w = sconv_bwd(f"{tag}_sconv", s["bcu"], 0, w["ab_conv_w"][e], dycat, FOX_WIDTH // ncol)
            grads["ab_conv_w"][e] = dconv_w
            dproj = jnp.concatenate(
                [_unheads(dqh), _unheads(dkh).astype(BF16), _unheads(dvh).astype(BF16), dz_t.T.astype(BF16), db, dc, du,
                 jnp.zeros((t, AB_IN_PAD - AB_IN), BF16)], axis=1)
            dh1 = mm(f"{tag}_ab_in_dx", dproj, s["w_in"], "nt", F32)
            dw_in = mm(f"{tag}_ab_in_dw", s["h1"], dproj, "tn", F32)[:, :AB_IN]
            lg["ab_w_in"] = dw_in.reshape(2, D_MODEL // 2, N_CHIPS, AB_IN // N_CHIPS).transpose(2, 0, 1, 3).astype(BF16)
        else:
            o = layer // 2
            dymix = mm(f"{tag}_c_out_dx", dy1, w["c_w_out"][o], "nt", F32)
            lg["c_w_out"] = dw(f"{tag}_c_out_dw", s["ymix"], dy1)
            dgate, dzr, dzi, duc_part, db_a, db_i, dlam = rg_scan_bwd(
                f"{tag}_rg_scan", s["gu2"], s["uc"], s["r"], s["i"], s["hs"], dymix, s["lam"])
            duc, dw_a, dw_i = rg_gates_bwd(f"{tag}_rg_gates", s["uc"], dzr, dzi, duc_part, w["c_w_a"][o], w["c_w_i"][o])
            lg["c_w_a"] = dw_a.reshape(N_CHIPS, 2, LRU_BW // 2, LRU_BW)
            lg["c_w_i"] = dw_i.reshape(N_CHIPS, 2, LRU_BW // 2, LRU_BW)
            du_raw, dconv_w, dconv_b = rg_conv_bwd(f"{tag}_rg_conv", s["gu2"], duc, w["c_conv_w"][o])
            grads["c_b_a"][o] = db_a.reshape(LRU_BLOCKS, LRU_BW)
            grads["c_b_i"][o] = db_i.reshape(LRU_BLOCKS, LRU_BW)
            grads["c_lam"][o] = dlam.reshape(-1)
            grads["c_conv_w"][o] = dconv_w
            grads["c_conv_b"][o] = dconv_b.reshape(-1)
            dgu2 = jnp.concatenate([dgate, du_raw], axis=1)
            dh1 = mm(f"{tag}_c_in_dx", dgu2, w["c_w_in"][o], "nt", F32)
            lg["c_w_in"] = dw(f"{tag}_c_in_dw", s["h1"], dgu2, cols_cut=True)
        dx, grads["g_mix_pre"][layer] = pre_bwd(f"{tag}_mix_pre", s["x0"], dh1, dx, w["g_mix_pre"], layer)
        if reduce_hook is None:
            big[layer] = lg
        else:
            token = reduce_hook(layer, lg)

    for k in list(grads):
        if k.startswith("g_"):
            grads[k] = [g.reshape(-1) for g in grads[k]]
        grads[k] = jnp.stack(grads[k])
    return sq_cols, dx, grads, big


CHIP_FLIPS = ((1, 0), (0, 1), (1, 1))
HBM_SPEC = pl.BlockSpec(memory_space=pltpu.HBM)
VMEM_SPEC = pl.BlockSpec(memory_space=pltpu.VMEM)


def _place():
    return lax.axis_index("x"), lax.axis_index("y"), lax.axis_index("c")


def _flip(v, f):
    return 1 - v if f else v


def _remote(src, dst, send_sem, recv_sem, target):
    return pltpu.make_async_remote_copy(src_ref=src, dst_ref=dst, send_sem=send_sem, recv_sem=recv_sem,
                                        device_id=target, device_id_type=MESH)


def _comm_call(name, body, ins, out_shape, n_sems):
    n = len(ins)
    return pl.pallas_call(
        body, name=name, in_specs=[HBM_SPEC] * n, out_specs=[HBM_SPEC] * len(out_shape), out_shape=out_shape,
        scratch_shapes=[pltpu.SemaphoreType.DMA((n, n_sems)), pltpu.SemaphoreType.DMA((n, n_sems))],
    )(*ins)


def gather_weights(shards):
    n = len(shards)

    def body(*refs):
        ins, outs = refs[:n], refs[n:2 * n]
        send_sems, recv_sems = refs[2 * n:]
        x, y, c = _place()
        p = 2 * x + y
        sibling = (x, y, 1 - c)
        chips = [(_flip(x, fx), _flip(y, fy)) for fx, fy in CHIP_FLIPS]
        first = [[_remote(ins[a].at[c], outs[a].at[p, c], send_sems.at[a, k], recv_sems.at[a, k], (qx, qy, c))
                  for k, (qx, qy) in enumerate(chips)] for a in range(n)]
        for a in range(n):
            for cp in first[a]:
                cp.start()
        passed = []
        for a in range(n):
            for k, (qx, qy) in enumerate(chips):
                landed = outs[a].at[2 * qx + qy, c]
                _remote(landed, landed, send_sems.at[a, k], recv_sems.at[a, k], (qx, qy, c)).wait_recv()
                cp = _remote(landed, landed, send_sems.at[a, 3 + k], recv_sems.at[a, 3 + k], sibling)
                cp.start()
                passed.append(cp)
        for a in range(n):
            for k, (qx, qy) in enumerate(chips):
                theirs = outs[a].at[2 * qx + qy, 1 - c]
                _remote(theirs, theirs, send_sems.at[a, 3 + k], recv_sems.at[a, 3 + k], sibling).wait_recv()
        for a in range(n):
            for cp in first[a]:
                cp.wait_send()
        for cp in passed:
            cp.wait_send()

    out_shape = [jax.ShapeDtypeStruct((N_CHIPS, *s.shape), s.dtype) for s in shards]
    return _comm_call("gather_weights", body, shards, out_shape, 6)


def swap_with_sibling(name, blocks):
    n = len(blocks)

    def body(*refs):
        ins, outs = refs[:n], refs[n:2 * n]
        send_sems, recv_sems = refs[2 * n:]
        x, y, c = _place()
        cps = [_remote(ins[a], outs[a], send_sems.at[a, 0], recv_sems.at[a, 0], (x, y, 1 - c)) for a in range(n)]
        for cp in cps:
            cp.start()
        for cp in cps:
            cp.wait()

    return _comm_call(name, body, blocks, [jax.ShapeDtypeStruct(b.shape, b.dtype) for b in blocks], 1)


def swap_other_half(name, arrays):
    n = len(arrays)

    def body(*refs):
        ins, outs = refs[:n], refs[n:2 * n]
        send_sems, recv_sems = refs[2 * n:]
        x, y, c = _place()
        cps = [_remote(ins[a].at[:, 1 - c], outs[a], send_sems.at[a, 0], recv_sems.at[a, 0], (x, y, 1 - c))
               for a in range(n)]
        for cp in cps:
            cp.start()
        for cp in cps:
            cp.wait()

    out_shape = [jax.ShapeDtypeStruct((b.shape[0], *b.shape[2:]), b.dtype) for b in arrays]
    return _comm_call(name, body, arrays, out_shape, 1)


SEM_SPEC = pl.BlockSpec(memory_space=pltpu.SEMAPHORE)


def _chip_copies(srcs, lands, send_sems, recv_sems):
    x, y, c = _place()
    p = 2 * x + y
    cps = []
    for a, (src, land) in enumerate(zip(srcs, lands)):
        for k, (fx, fy) in enumerate(CHIP_FLIPS):
            qx, qy = _flip(x, fx), _flip(y, fy)
            sem = len(CHIP_FLIPS) * a + k
            cps.append(_remote(src.at[2 * qx + qy], land.at[p], send_sems.at[sem], recv_sems.at[sem], (qx, qy, c)))
    return cps


def exchange_start(name, blocks):
    n = len(blocks)

    def body(*refs):
        srcs, lands = refs[:n], refs[n:2 * n]
        send_sems, recv_sems = refs[2 * n], refs[2 * n + 1]
        token = refs[-1]
        for cp in _chip_copies(srcs, lands, send_sems, recv_sems):
            cp.start()
        token[...] = jnp.zeros_like(token)

    hbm = [pltpu.HBM(b.shape, b.dtype) for b in blocks]
    outs = pl.pallas_call(
        body, name=name, in_specs=[HBM_SPEC] * (2 * n),
        out_shape=(pltpu.SemaphoreType.DMA((3 * n,)), pltpu.SemaphoreType.DMA((3 * n,)), *hbm, *hbm,
                   jax.ShapeDtypeStruct((8, 128), F32)),
        out_specs=(SEM_SPEC, SEM_SPEC, *[HBM_SPEC] * (2 * n), VMEM_SPEC),
        input_output_aliases={i: 2 + i for i in range(2 * n)},
        compiler_params=pltpu.CompilerParams(has_side_effects=pltpu.SideEffectType.DATAFLOW_SIDE_EFFECTING),
    )(*[pltpu.with_memory_space_constraint(b, pltpu.HBM) for b in blocks],
      *[pltpu.with_memory_space_constraint(lax.empty(b.shape, b.dtype), pltpu.HBM) for b in blocks])
    return outs[:-1], outs[-1]


def exchange_wait(name, state, after):
    send_sems, recv_sems, *thru = state
    n = len(thru) // 2

    def body(*refs):
        srcs, lands = refs[:n], refs[n:2 * n]
        for cp in _chip_copies(srcs, lands, refs[2 * n], refs[2 * n + 1]):
            cp.wait_send()
            cp.wait_recv()

    outs = pl.pallas_call(
        body, name=name, in_specs=[HBM_SPEC] * (2 * n) + [SEM_SPEC, SEM_SPEC, pl.BlockSpec(memory_space=pl.ANY)],
        out_shape=tuple(pltpu.HBM(t.shape, t.dtype) for t in thru), out_specs=tuple([HBM_SPEC] * (2 * n)),
        input_output_aliases={i: i for i in range(2 * n)},
        compiler_params=pltpu.CompilerParams(has_side_effects=pltpu.SideEffectType.DATAFLOW_SIDE_EFFECTING),
    )(*thru, send_sems, recv_sems, after)
    return outs[n:]


DEVICE_FLIPS = tuple((fx, fy, fc) for fx in (0, 1) for fy in (0, 1) for fc in (0, 1))[1:]


def gather_small(name, v, reduce):
    r, cdim = v.shape
    n_dev = 8

    def body(v_ref, out_ref, *scratch):
        buf = scratch[0] if reduce else out_ref
        send_sems, recv_sems = scratch[-2:]
        x, y, c = _place()
        me = 4 * x + 2 * y + c
        buf[me] = v_ref[...]
        cps = []
        for k, (fx, fy, fc) in enumerate(DEVICE_FLIPS):
            cps.append(_remote(v_ref, buf.at[me], send_sems.at[k], recv_sems.at[k],
                               (_flip(x, fx), _flip(y, fy), _flip(c, fc))))
        for cp in cps:
            cp.start()
        for cp in cps:
            cp.wait()
        if reduce:
            total = buf[0]
            for d in range(1, n_dev):
                total = total + buf[d]
            out_ref[...] = total

    scratch = [pltpu.SemaphoreType.DMA((7,)), pltpu.SemaphoreType.DMA((7,))]
    if reduce:
        scratch = [pltpu.VMEM((n_dev, r, cdim), F32)] + scratch
    out_shape = jax.ShapeDtypeStruct((r, cdim) if reduce else (n_dev, r, cdim), F32)
    return pl.pallas_call(body, name=name, in_specs=[VMEM_SPEC], out_specs=VMEM_SPEC, out_shape=out_shape,
                          scratch_shapes=scratch)(v)


def pair_sum(name, own, got, core):
    _, hx, cols = got.shape
    tr = _tile(hx, (256, 128, 64, 32, 16))

    def kern(core_ref, a_ref, b_ref, o_ref):
        o_ref[...] = (a_ref[...].astype(F32) + b_ref[...].astype(F32)).astype(BF16)

    grid_spec = pltpu.PrefetchScalarGridSpec(
        num_scalar_prefetch=1, grid=(hx // tr,),
        in_specs=[pl.BlockSpec((N_CHIPS, None, tr, cols), lambda i, cr: (0, cr[0], i, 0)),
                  pl.BlockSpec((N_CHIPS, tr, cols), lambda i, cr: (0, i, 0))],
        out_specs=pl.BlockSpec((N_CHIPS, tr, cols), lambda i, cr: (0, i, 0)))
    return pl.pallas_call(
        kern, name=name, grid_spec=grid_spec, out_shape=jax.ShapeDtypeStruct(got.shape, BF16),
        compiler_params=pltpu.CompilerParams(dimension_semantics=("parallel",), vmem_limit_bytes=VMEM_LIMIT_BYTES),
    )(core, own, got)


def chip_sum(name, mine, parts, chip):
    _, hx, yd = parts.shape
    tr = _tile(hx, (256, 128, 64, 32, 16))

    def kern(chip_ref, m_ref, p_ref, o_ref):
        total = None
        for q in range(N_CHIPS):
            term = jnp.where(chip_ref[0] == q, m_ref[...], p_ref[q]).astype(F32)
            total = term if total is None else total + term
        o_ref[...] = total

    grid_spec = pltpu.PrefetchScalarGridSpec(
        num_scalar_prefetch=1, grid=(hx // tr,),
        in_specs=[pl.BlockSpec((None, tr, yd), lambda i, cr: (cr[0], i, 0)),
                  pl.BlockSpec((N_CHIPS, tr, yd), lambda i, cr: (0, i, 0))],
        out_specs=pl.BlockSpec((tr, yd), lambda i, cr: (i, 0)))
    return pl.pallas_call(
        kern, name=name, grid_spec=grid_spec, out_shape=jax.ShapeDtypeStruct((hx, yd), F32),
        compiler_params=pltpu.CompilerParams(dimension_semantics=("parallel",), vmem_limit_bytes=VMEM_LIMIT_BYTES),
    )(chip, mine, parts)


WEIGHTS = ("g_mix_pre", "g_mix_post", "g_cross_pre", "g_mem", "g_cross_post", "g_ffn_pre", "g_ffn_post", "w_xq", "w_xkv",
           "w_xo", "w_ffn_gu", "w_ffn_down", "ab_w_in", "ab_b_f", "ab_conv_w", "ab_w_out", "c_w_in", "c_conv_w",
           "c_conv_b", "c_w_a", "c_b_a", "c_w_i", "c_b_i", "c_lam", "c_w_out")
SHARD_DIM = {"w_xq": 1, "w_xkv": 2, "w_xo": 1, "w_ffn_gu": 2, "w_ffn_down": 1, "ab_w_in": 2, "ab_conv_w": 2,
             "ab_w_out": 1, "c_w_in": 2, "c_conv_w": 2, "c_conv_b": 1, "c_w_a": 2, "c_b_a": 2, "c_w_i": 2, "c_b_i": 2,
             "c_lam": 1, "c_w_out": 1}
BIG = ("w_xq", "w_xkv", "w_xo", "w_ffn_gu", "w_ffn_down", "ab_w_in", "ab_w_out", "c_w_in", "c_w_a", "c_w_i", "c_w_out")
SMALL_SHARDED = ("ab_conv_w", "c_conv_w", "c_conv_b", "c_b_a", "c_b_i", "c_lam")
REPLICATED = ("g_mix_pre", "g_mix_post", "g_cross_pre", "g_mem", "g_cross_post", "g_ffn_pre", "g_ffn_post", "ab_b_f")
PACK_COLS = 1024


def _unshard(g, d):
    shard = g.shape[1:]
    return jnp.moveaxis(g, 0, d).reshape(shard[:d] + (N_CHIPS * shard[d],) + shard[d + 1:])


def _shardify(full, d):
    s = full.shape
    return jnp.moveaxis(full.reshape(s[:d] + (N_CHIPS, s[d] // N_CHIPS) + s[d + 1:]), d, 0)


def _pack(arrays, rows):
    flat = jnp.concatenate([a.reshape(-1).astype(F32) for a in arrays])
    return jnp.pad(flat, (0, rows * PACK_COLS - flat.shape[0])).reshape(rows, PACK_COLS)


def _unpack(packed, shapes):
    flat = packed.reshape(-1)
    out, at = [], 0
    for s in shapes:
        size = math.prod(s)
        out.append(flat[at:at + size].reshape(s))
        at += size
    return out


def _rows_for(shapes):
    return -(-sum(math.prod(s) for s in shapes) // (8 * PACK_COLS)) * 8


def kernel(x, mem, g_mix_pre, g_mix_post, g_cross_pre, g_mem, g_cross_post, g_ffn_pre, g_ffn_post, w_xq, w_xkv, w_xo, w_ffn_gu, w_ffn_down, ab_w_in, ab_b_f, ab_conv_w, ab_w_out, c_w_in, c_conv_w, c_conv_b, c_w_a, c_b_a, c_w_i, c_b_i, c_lam, c_w_out, loss_target, m_g_mix_pre, m_g_mix_post, m_g_cross_pre, m_g_mem, m_g_cross_post, m_g_ffn_pre, m_g_ffn_post, m_w_xq, m_w_xkv, m_w_xo, m_w_ffn_gu, m_w_ffn_down, m_ab_w_in, m_ab_b_f, m_ab_conv_w, m_ab_w_out, m_c_w_in, m_c_conv_w, m_c_conv_b, m_c_w_a, m_c_b_a, m_c_w_i, m_c_b_i, m_c_lam, m_c_w_out, v_g_mix_pre, v_g_mix_post, v_g_cross_pre, v_g_mem, v_g_cross_post, v_g_ffn_pre, v_g_ffn_post, v_w_xq, v_w_xkv, v_w_xo, v_w_ffn_gu, v_w_ffn_down, v_ab_w_in, v_ab_b_f, v_ab_conv_w, v_ab_w_out, v_c_w_in, v_c_conv_w, v_c_conv_b, v_c_w_a, v_c_b_a, v_c_w_i, v_c_b_i, v_c_lam, v_c_w_out):
    given = dict(locals())
    w = {n: given[n] for n in WEIGHTS}
    m_in = {n: given["m_" + n] for n in WEIGHTS}
    v_in = {n: given["v_" + n] for n in WEIGHTS}
    xi, yi, ci = _place()
    chip = 2 * xi + yi

    def as_halves(a):
        return a.reshape(2, -1, a.shape[-1]) if a.ndim > 1 else a

    gathered = gather_weights([as_halves(w[n].astype(BF16)) for n in BIG])
    full = {}
    for n, g in zip(BIG, gathered):
        slot = lax.broadcasted_iota(jnp.int32, (N_CHIPS,) + (1,) * w[n].ndim, 0)
        g = jnp.where(slot == chip, w[n].astype(BF16)[None], g.reshape(N_CHIPS, *w[n].shape))
        full[n] = _unshard(g, SHARD_DIM[n])
    small_shapes = [w[n].shape for n in SMALL_SHARDED]
    rows_w = _rows_for(small_shapes)
    every = gather_small("gather_small_weights", _pack([w[n] for n in SMALL_SHARDED], rows_w), reduce=False)
    per_chip = every[0::2].reshape(N_CHIPS, -1)
    at = 0
    for n, s in zip(SMALL_SHARDED, small_shapes):
        size = math.prod(s)
        full[n] = _unshard(per_chip[:, at:at + size].reshape(N_CHIPS, *s), SHARD_DIM[n])
        at += size
    for n in REPLICATED:
        full[n] = w[n]

    core_arr, chip_arr = ci.reshape(1).astype(jnp.int32), chip.reshape(1).astype(jnp.int32)
    in_flight = []

    def reduce_hook(layer, layer_grads):
        names = list(layer_grads)
        own = [layer_grads[n] for n in names]
        got = swap_other_half(f"swap_grads_l{layer}", own)
        sums = [pair_sum(f"pair_sum_l{layer}_{n}", o, g, core_arr) for n, o, g in zip(names, own, got)]
        state, token = exchange_start(f"exchange_start_l{layer}", sums)
        in_flight.append((layer, names, sums, state))
        return token[0, 0]

    sq_cols, dx, grads, _ = local_step(x[0], mem[0], loss_target[0], full, reduce_hook)
    loss = lax.psum(0.5 / D_MODEL * jnp.sum(sq_cols), ("x", "y", "c"))

    keys, halves = [], []
    for layer, names, sums, state in in_flight:
        parts = exchange_wait(f"exchange_wait_l{layer}", state, dx)
        for n, mine, p in zip(names, sums, parts):
            keys.append((n, layer))
            halves.append(chip_sum(f"chip_sum_l{layer}_{n}", mine, p, chip_arr))
    others = swap_with_sibling("swap_reduced_halves", halves)
    reduced = {k: (mine, theirs) for k, mine, theirs in zip(keys, halves, others)}
    grad_out = {}
    for n in BIG:
        pieces = []
        for layer in sorted(l for k, l in keys if k == n):
            mine, theirs = reduced[(n, layer)]
            pieces += [jnp.where(ci == 0, mine, theirs), jnp.where(ci == 0, theirs, mine)]
        grad_out[n] = jnp.concatenate(pieces, axis=0).reshape(w[n].shape)

    small_names = REPLICATED + SMALL_SHARDED
    small_full_shapes = [grads[n].shape for n in small_names]
    total = gather_small("reduce_small_grads", _pack([grads[n] for n in small_names], _rows_for(small_full_shapes)),
                         reduce=True)
    for n, g in zip(small_names, _unpack(total, small_full_shapes)):
        if n in SHARD_DIM:
            g = lax.dynamic_index_in_dim(_shardify(g, SHARD_DIM[n]), chip, axis=0, keepdims=False)
        grad_out[n] = g

    delta, new_m, new_v = {}, {}, {}
    for n in BIG:
        two_d = lambda a: a.reshape(-1, a.shape[-1])
        d, m2, v2 = adamw(f"adamw_{n}", two_d(w[n]), two_d(grad_out[n]), two_d(m_in[n]), two_d(v_in[n]))
        delta[n], new_m[n], new_v[n] = (a.reshape(w[n].shape) for a in (d, m2, v2))
    shapes = [w[n].shape for n in small_names]
    rows = _rows_for(shapes)
    packed = [_pack([src[n] for n in small_names], rows) for src in (w, grad_out, m_in, v_in)]
    for dst, res in zip((delta, new_m, new_v), adamw("adamw_small", *packed)):
        for n, a in zip(small_names, _unpack(res, shapes)):
            dst[n] = a

    return (loss, dx[None], *[grad_out[n] for n in WEIGHTS], *[delta[n] for n in WEIGHTS],
            *[new_m[n] for n in WEIGHTS], *[new_v[n] for n in WEIGHTS])
```

```python
import functools
import math

import jax
import jax.numpy as jnp
from jax import lax
from jax.experimental import pallas as pl
from jax.experimental.pallas import tpu as pltpu

F32, BF16 = jnp.float32, jnp.bfloat16
D_MODEL = 1024
EPS = 1e-6
NEG_INF = -1e30
FOX_HEADS, FOX_HEAD_DIM, FOX_WIDTH = 8, 64, 512
SC_WIDTH = 512
AB_IN = 3 * FOX_WIDTH + FOX_HEADS + 3 * SC_WIDTH
AB_IN_PAD = 3200
LRU_BW, LRU_BLOCKS = 256, 4
RG_C = 8.0
MEM_HEADS, MEM_HEAD_DIM = 4, 256
ADAM_LR, ADAM_B1, ADAM_B2, ADAM_EPS, ADAM_WD, ADAM_STEP = 0.001, 0.9, 0.999, 1e-08, 0.01, 10
N_CHIPS = 4
MESH = pl.DeviceIdType.MESH
VMEM_LIMIT_BYTES = 48 * 1024 * 1024
MM_OPERAND_TILE_BYTES = 7 * 1024 * 1024

NN = (((1,), (0,)), ((), ()))
NT = (((1,), (1,)), ((), ()))
TN = (((0,), (0,)), ((), ()))


def _dot(a, b, dn=NN):
    return lax.dot_general(a.astype(BF16), b.astype(BF16), dn, preferred_element_type=F32)


def _tile(n, prefs):
    for p in prefs:
        if n % p == 0:
            return p
    return n


def _pcall(name, kern, grid, ins, in_specs, out_shape, out_specs, sem):
    return pl.pallas_call(
        kern, name=name, grid=grid, in_specs=in_specs, out_specs=out_specs, out_shape=out_shape,
        compiler_params=pltpu.CompilerParams(dimension_semantics=sem, vmem_limit_bytes=VMEM_LIMIT_BYTES),
    )(*ins)


def mm(name, a, b, mode, out_dtype, reduce_layout=False):
    if mode == "nn":
        (m, k), n = a.shape, b.shape[1]
    elif mode == "nt":
        (m, k), n = a.shape, b.shape[0]
    else:
        (k, m), n = a.shape, b.shape[1]
    if reduce_layout:
        tm, tn = m // 2, n // N_CHIPS
    else:
        tn = _tile(n, ((1024,) if mode == "tn" else ()) + (512, 640, 256, 128))
        tm = next(c for c in (2048, 1024, 512, 256, 128, m)
                  if m % c == 0 and 2 * c * k <= MM_OPERAND_TILE_BYTES and 4 * c * tn <= MM_OPERAND_TILE_BYTES)
    dn = {"nn": NN, "nt": NT, "tn": TN}[mode]

    def kern(a_ref, b_ref, o_ref):
        o_ref[...] = _dot(a_ref[...], b_ref[...], dn).astype(o_ref.dtype)

    a_spec = pl.BlockSpec((k, tm), lambda i, j: (0, i)) if mode == "tn" else pl.BlockSpec((tm, k), lambda i, j: (i, 0))
    b_spec = pl.BlockSpec((tn, k), lambda i, j: (j, 0)) if mode == "nt" else pl.BlockSpec((k, tn), lambda i, j: (0, j))
    if reduce_layout:
        out_shape = jax.ShapeDtypeStruct((N_CHIPS, 2, tm, tn), out_dtype)
        o_spec = pl.BlockSpec((None, None, tm, tn), lambda i, j: (j, i, 0, 0))
    else:
        out_shape = jax.ShapeDtypeStruct((m, n), out_dtype)
        o_spec = pl.BlockSpec((tm, tn), lambda i, j: (i, j))
    return _pcall(name, kern, (m // tm, n // tn), (a, b), [a_spec, b_spec], out_shape, o_spec, ("parallel", "parallel"))


def rowwise(name, body, rows, params, outs, accs=(), tr=256):
    t = rows[0].shape[0]
    tr = min(tr, t)
    nr, npar, no = len(rows), len(params), len(outs)

    def kern(*refs):
        acc_refs = refs[nr + npar + no:]
        if acc_refs:
            @pl.when(pl.program_id(0) == 0)
            def _():
                for ar in acc_refs:
                    ar[...] = jnp.zeros_like(ar)
        body(refs[:nr], refs[nr:nr + npar], refs[nr + npar:nr + npar + no], acc_refs)

    in_specs = [pl.BlockSpec((tr, x.shape[1]), lambda i: (i, 0)) for x in rows]
    in_specs += [pl.BlockSpec(p.shape, lambda i: (0, 0)) for p in params]
    out_specs = [pl.BlockSpec((tr, c), lambda i: (i, 0)) for c, _ in outs]
    out_specs += [pl.BlockSpec(s, lambda i: (0, 0)) for s in accs]
    out_shape = [jax.ShapeDtypeStruct((t, c), dt) for c, dt in outs]
    out_shape += [jax.ShapeDtypeStruct(s, F32) for s in accs]
    return _pcall(name, kern, (t // tr,), (*rows, *params), in_specs, out_shape, out_specs,
                  ("arbitrary",) if accs else ("parallel",))


def _rms_stats(x):
    r = lax.rsqrt(jnp.mean(x * x, axis=-1, keepdims=True) + EPS)
    return r, x * r


def _rms_bwd(xh, r, g, dy):
    dxh = dy * g
    dx = r * (dxh - xh * jnp.mean(dxh * xh, axis=-1, keepdims=True))
    return dx, jnp.sum(dy * xh, axis=0, keepdims=True)


def rms_pre(name, x, gains, layer):
    def body(r, p, o, a):
        _, xh = _rms_stats(r[0][...])
        o[0][...] = (xh * p[0][layer:layer + 1, :]).astype(BF16)
    return rowwise(name, body, [x], [gains], [(x.shape[1], BF16)])[0]


def post_add(name, x, y, gains, layer):
    def body(r, p, o, a):
        _, yh = _rms_stats(r[1][...])
        o[0][...] = r[0][...] + yh * p[0][layer:layer + 1, :]
    return rowwise(name, body, [x, y], [gains], [(x.shape[1], F32)])[0]


def post_bwd(name, y, dx, gains, layer):
    def body(r, p, o, a):
        rr, yh = _rms_stats(r[0][...])
        dy, dg = _rms_bwd(yh, rr, p[0][layer:layer + 1, :], r[1][...])
        o[0][...] = dy.astype(BF16)
        a[0][...] += dg
    c = y.shape[1]
    return rowwise(name, body, [y, dx], [gains], [(c, BF16)], [(1, c)])


def pre_bwd(name, x, dh, dx_res, gains, layer):
    def body(r, p, o, a):
        rr, xh = _rms_stats(r[0][...])
        dx, dg = _rms_bwd(xh, rr, p[0][layer:layer + 1, :], r[1][...])
        o[0][...] = r[2][...] + dx
        a[0][...] += dg
    c = x.shape[1]
    return rowwise(name, body, [x, dh, dx_res], [gains], [(c, F32)], [(1, c)])


def gain_bwd(name, x, dh):
    def body(r, p, o, a):
        _, xh = _rms_stats(r[0][...])
        a[0][...] += jnp.sum(r[1][...] * xh, axis=0, keepdims=True)
    return rowwise(name, body, [x, dh], [], [], [(1, x.shape[1])])[0]


def _sigmoid(z):
    return 1.0 / (1.0 + jnp.exp(-z))


def swiglu_fwd(name, gu):
    f = gu.shape[1] // 2

    def body(r, p, o, a):
        g = r[0][:, :f].astype(F32)
        u = r[0][:, f:].astype(F32)
        o[0][...] = (g * _sigmoid(g) * u).astype(BF16)
    return rowwise(name, body, [gu], [], [(f, BF16)])[0]


def swiglu_bwd(name, gu, da):
    f = gu.shape[1] // 2

    def body(r, p, o, a):
        g = r[0][:, :f].astype(F32)
        u = r[0][:, f:].astype(F32)
        d = r[1][...].astype(F32)
        sg = _sigmoid(g)
        o[0][:, :f] = (d * u * sg * (1.0 + g * (1.0 - sg))).astype(BF16)
        o[0][:, f:] = (d * g * sg).astype(BF16)
    return rowwise(name, body, [gu, da], [], [(2 * f, BF16)])[0]


def loss_head(name, y, target):
    c = y.shape[1]

    def body(r, p, o, a):
        e = r[0][...] - r[1][...]
        o[0][...] = e * (1.0 / c)
        a[0][...] += jnp.sum(e * e, axis=0, keepdims=True)
    return rowwise(name, body, [y, target], [], [(c, F32)], [(1, c)])


def adamw(name, w, g, m, v):
    c = w.shape[1]

    def body(r, p, o, a):
        wv, gv, mv, vv = (x[...] for x in r)
        m2 = ADAM_B1 * mv + (1.0 - ADAM_B1) * gv
        v2 = ADAM_B2 * vv + (1.0 - ADAM_B2) * (gv * gv)
        m_hat = m2 / (1.0 - ADAM_B1 ** ADAM_STEP)
        v_hat = v2 / (1.0 - ADAM_B2 ** ADAM_STEP)
        o[0][...] = -ADAM_LR * (m_hat / (jnp.sqrt(v_hat) + ADAM_EPS) + ADAM_WD * wv)
        o[1][...] = m2
        o[2][...] = v2
    tr = _tile(w.shape[0], (256, 128, 64, 32, 16, 8))
    return rowwise(name, body, [w, g, m, v], [], [(c, F32)] * 3, tr=tr)


def colwise(name, body, cols, params, outs, pouts=(), tc=128):
    t = cols[0][0].shape[0]
    c = params[0].shape[1] if params else cols[0][0].shape[1]
    nc, npar, no = len(cols), len(params), len(outs)

    def kern(*refs):
        body(refs[:nc], refs[nc:nc + npar], refs[nc + npar:nc + npar + no], refs[nc + npar + no:])

    in_specs = [pl.BlockSpec((t, tc), functools.partial(lambda j, off: (0, j + off), off=off)) for _, off in cols]
    in_specs += [pl.BlockSpec((p.shape[0], tc), lambda j: (0, j)) for p in params]
    out_specs = [pl.BlockSpec((t, tc), lambda j: (0, j)) for _ in outs]
    out_specs += [pl.BlockSpec((r, tc), lambda j: (0, j)) for r in pouts]
    out_shape = [jax.ShapeDtypeStruct((t, c), dt) for dt in outs]
    out_shape += [jax.ShapeDtypeStruct((r, c), F32) for r in pouts]
    return _pcall(name, kern, (c // tc,), (*[x for x, _ in cols], *params), in_specs, out_shape, out_specs,
                  ("parallel",))


def _row_index(shape):
    return lax.broadcasted_iota(jnp.int32, shape, 0)


def _shift_down(x, d, rows):
    return jnp.where(rows >= d, pltpu.roll(x, d, 0), 0.0)


def _shift_up(x, d, rows):
    t = x.shape[0]
    return jnp.where(rows < t - d, pltpu.roll(x, t - d, 0), 0.0)


def sconv_fwd(name, proj, col0, conv_w, tc=128):
    nb = SC_WIDTH // tc

    def body(cl, p, o, po):
        b, c, u = (x[...] for x in cl)
        rows = _row_index(b.shape)
        w = p[0][...]
        z = c * u
        conv = w[2:3] * z + w[1:2] * _shift_down(z, 1, rows) + w[0:1] * _shift_down(z, 2, rows)
        o[0][...] = (b * conv).astype(BF16)
    return colwise(name, body, [(proj, col0), (proj, col0 + nb), (proj, col0 + 2 * nb)], [conv_w], [BF16], tc=tc)[0]


def sconv_bwd(name, proj, col0, conv_w, dyb, dcol0, tc=128):
    nb = SC_WIDTH // tc

    def body(cl, p, o, po):
        b, c, u, dy = (x[...] for x in cl)
        rows = _row_index(b.shape)
        w = p[0][...]
        z = c * u
        z1, z2 = _shift_down(z, 1, rows), _shift_down(z, 2, rows)
        conv = w[2:3] * z + w[1:2] * z1 + w[0:1] * z2
        dconv = dy * b
        dz = w[2:3] * dconv + w[1:2] * _shift_up(dconv, 1, rows) + w[0:1] * _shift_up(dconv, 2, rows)
        o[0][...] = (dy * conv).astype(BF16)
        o[1][...] = (dz * u).astype(BF16)
        o[2][...] = (dz * c).astype(BF16)
        po[0][0:1, :] = jnp.sum(dconv * z2, axis=0, keepdims=True)
        po[0][1:2, :] = jnp.sum(dconv * z1, axis=0, keepdims=True)
        po[0][2:3, :] = jnp.sum(dconv * z, axis=0, keepdims=True)
    return colwise(name, body, [(proj, col0), (proj, col0 + nb), (proj, col0 + 2 * nb), (dyb, dcol0)], [conv_w],
                   [BF16, BF16, BF16], [3], tc=tc)


def _expm1(x):
    series = x * (1.0 + 0.5 * x * (1.0 + x * (1.0 / 3.0) * (1.0 + 0.25 * x * (1.0 + 0.2 * x))))
    return jnp.where(jnp.abs(x) < 0.05, series, jnp.exp(x) - 1.0)


def _log1p(x):
    series = x * (1.0 - x * (0.5 - x * (1.0 / 3.0 - 0.25 * x)))
    return jnp.where(jnp.abs(x) < 0.01, series, jnp.log(1.0 + x))


def _softplus_neg(lam):
    sp = jnp.maximum(-lam, 0.0) + _log1p(jnp.exp(-jnp.abs(lam)))
    return sp, -_sigmoid(-lam)


GELU_C = math.sqrt(2.0 / math.pi)


def _gelu(x):
    th = jnp.tanh(GELU_C * (x + 0.044715 * x * x * x))
    val = 0.5 * x * (1.0 + th)
    grad = 0.5 * (1.0 + th) + 0.5 * x * (1.0 - th * th) * GELU_C * (1.0 + 3.0 * 0.044715 * x * x)
    return val, grad


def rg_conv_fwd(name, gu2, conv_w, conv_b, tc=128):
    nb = D_MODEL // tc

    def body(cl, p, o, po):
        u = cl[0][...]
        rows = _row_index(u.shape)
        w = p[0][...]
        o[0][...] = (w[3:4] * u + w[2:3] * _shift_down(u, 1, rows) + w[1:2] * _shift_down(u, 2, rows)
                     + w[0:1] * _shift_down(u, 3, rows) + p[1][...])
    return colwise(name, body, [(gu2, nb)], [conv_w, conv_b], [F32], tc=tc)[0]


def rg_conv_bwd(name, gu2, duc, conv_w, tc=128):
    nb = D_MODEL // tc

    def body(cl, p, o, po):
        u, d = cl[0][...], cl[1][...]
        rows = _row_index(u.shape)
        w = p[0][...]
        o[0][...] = (w[3:4] * d + w[2:3] * _shift_up(d, 1, rows) + w[1:2] * _shift_up(d, 2, rows)
                     + w[0:1] * _shift_up(d, 3, rows)).astype(BF16)
        for k in range(4):
            uk = u if k == 3 else _shift_down(u, 3 - k, rows)
            po[0][k:k + 1, :] = jnp.sum(d * uk, axis=0, keepdims=True)
        po[1][...] = jnp.sum(d, axis=0, keepdims=True)
    return colwise(name, body, [(gu2, nb), (duc, 0)], [conv_w], [BF16], [4, 1], tc=tc)


def rg_gates_fwd(name, uc, w_a, b_a, w_i, b_i, tr=512):
    t = uc.shape[0]
    tr = min(tr, t)

    def kern(u_ref, wa_ref, ba_ref, wi_ref, bi_ref, r_ref, i_ref):
        ub = u_ref[...].astype(BF16)
        r_ref[...] = _sigmoid(_dot(ub, wa_ref[...]) + ba_ref[...])
        i_ref[...] = _sigmoid(_dot(ub, wi_ref[...]) + bi_ref[...])

    blk = pl.BlockSpec((tr, LRU_BW), lambda n, i: (i, n))
    wspec = pl.BlockSpec((None, LRU_BW, LRU_BW), lambda n, i: (n, 0, 0))
    bspec = pl.BlockSpec((1, LRU_BW), lambda n, i: (0, n))
    return _pcall(name, kern, (LRU_BLOCKS, t // tr), (uc, w_a, b_a, w_i, b_i), [blk, wspec, bspec, wspec, bspec],
                  [jax.ShapeDtypeStruct(uc.shape, F32)] * 2, [blk, blk], ("parallel", "parallel"))


def rg_gates_bwd(name, uc, dzr, dzi, duc_part, w_a, w_i):
    t = uc.shape[0]
    rows = LRU_BW // N_CHIPS

    def kern(u_ref, dr_ref, di_ref, dp_ref, wa_ref, wi_ref, duc_ref, dwa_ref, dwi_ref):
        ub = u_ref[...].astype(BF16)
        dr, di = dr_ref[...], di_ref[...]
        dwa, dwi = _dot(ub, dr, TN), _dot(ub, di, TN)
        for p in range(N_CHIPS):
            dwa_ref[p] = dwa[p * rows:(p + 1) * rows].astype(dwa_ref.dtype)
            dwi_ref[p] = dwi[p * rows:(p + 1) * rows].astype(dwi_ref.dtype)
        duc_ref[...] = dp_ref[...] + _dot(dr, wa_ref[...], NT) + _dot(di, wi_ref[...], NT)

    blk = pl.BlockSpec((t, LRU_BW), lambda n: (0, n))
    wspec = pl.BlockSpec((None, LRU_BW, LRU_BW), lambda n: (n, 0, 0))
    gspec = pl.BlockSpec((N_CHIPS, None, rows, LRU_BW), lambda n: (0, n, 0, 0))
    gshape = jax.ShapeDtypeStruct((N_CHIPS, LRU_BLOCKS, rows, LRU_BW), BF16)
    return _pcall(name, kern, (LRU_BLOCKS,), (uc, dzr, dzi, duc_part, w_a, w_i), [blk, blk, blk, blk, wspec, wspec],
                  [jax.ShapeDtypeStruct(uc.shape, F32), gshape, gshape], [blk, gspec, gspec], ("parallel",))


def _rg_decay(r, lam):
    sp, dsp = _softplus_neg(lam)
    la = -RG_C * r * sp
    a = jnp.exp(la)
    sq = jnp.sqrt(-_expm1(2.0 * la))
    return sp, dsp, a, sq


def rg_scan_fwd(name, gu2, uc, r, i, lam, tc=128):
    def body(cl, p, o, po):
        gate, ucv, rv, iv = (x[...] for x in cl)
        t = gate.shape[0]
        rows = _row_index(gate.shape)
        _, _, a, sq = _rg_decay(rv, p[0][...])
        b = sq * (iv * ucv)
        d = 1
        while d < t:
            keep = rows >= d
            b = a * jnp.where(keep, pltpu.roll(b, d, 0), 0.0) + b
            a = a * jnp.where(keep, pltpu.roll(a, d, 0), 1.0)
            d *= 2
        o[0][...] = (_gelu(gate)[0] * b).astype(BF16)
        o[1][...] = b
    return colwise(name, body, [(gu2, 0), (uc, 0), (r, 0), (i, 0)], [lam], [BF16, F32], tc=tc)


def rg_scan_bwd(name, gu2, uc, r, i, hs, dy, lam, tc=128):
    def body(cl, p, o, po):
        gate, ucv, rv, iv, h, dyv = (x[...] for x in cl)
        t = gate.shape[0]
        rows = _row_index(gate.shape)
        sp, dsp, a, sq = _rg_decay(rv, p[0][...])
        gl, dgl = _gelu(gate)
        o[0][...] = (dyv * h * dgl).astype(BF16)
        g = dyv * gl
        am = _shift_up(a, 1, rows)
        d = 1
        while d < t:
            keep = rows < t - d
            g = am * jnp.where(keep, pltpu.roll(g, t - d, 0), 0.0) + g
            am = am * jnp.where(keep, pltpu.roll(am, t - d, 0), 0.0)
            d *= 2
        da = g * _shift_down(h, 1, rows)
        iu = iv * ucv
        d_iu = g * sq
        dla = da * a - (g * iu) * (a * a) / sq
        dzr = dla * (-RG_C * sp) * rv * (1.0 - rv)
        dzi = d_iu * ucv * iv * (1.0 - iv)
        o[1][...] = dzr.astype(BF16)
        o[2][...] = dzi.astype(BF16)
        o[3][...] = d_iu * iv
        po[0][...] = jnp.sum(dzr, axis=0, keepdims=True)
        po[1][...] = jnp.sum(dzi, axis=0, keepdims=True)
        po[2][...] = jnp.sum(dla * rv, axis=0, keepdims=True) * (-RG_C) * dsp
    return colwise(name, body, [(gu2, 0), (uc, 0), (r, 0), (i, 0), (hs, 0), (dy, 0)], [lam],
                   [BF16, BF16, BF16, F32], [1, 1, 1], tc=tc)


def _split3(x):
    hi = x.astype(BF16)
    r1 = x - hi.astype(F32)
    mid = r1.astype(BF16)
    lo = (r1 - mid.astype(F32)).astype(BF16)
    return hi, mid, lo


def _tri_dot(x, tri):
    out = None
    for piece in _split3(x):
        term = lax.dot_general(piece, tri, NN, preferred_element_type=F32)
        out = term if out is None else out + term
    return out


def fox_gates_fwd(name, z_t, b_f):
    h, t = z_t.shape
    tb = min(512, t)

    def kern(z_ref, b_ref, o_ref):
        z = z_ref[...] + b_ref[...]
        logf = jnp.minimum(z, 0.0) - _log1p(jnp.exp(-jnp.abs(z)))
        src = lax.broadcasted_iota(jnp.int32, (t, tb), 0)
        dst = lax.broadcasted_iota(jnp.int32, (t, tb), 1) + pl.program_id(0) * tb
        o_ref[...] = _tri_dot(logf, (src <= dst).astype(BF16))

    return _pcall(name, kern, (t // tb,), (z_t, b_f),
                  [pl.BlockSpec((h, t), lambda j: (0, 0)), pl.BlockSpec((h, 1), lambda j: (0, 0))],
                  jax.ShapeDtypeStruct((h, t), F32), pl.BlockSpec((h, tb), lambda j: (0, j)), ("parallel",))


def fox_gates_bwd(name, z_t, b_f, dcum_t):
    h, t = z_t.shape
    tb = min(512, t)

    def kern(z_ref, b_ref, d_ref, dz_ref, db_ref):
        @pl.when(pl.program_id(0) == 0)
        def _():
            db_ref[...] = jnp.zeros_like(db_ref)
        src = lax.broadcasted_iota(jnp.int32, (t, tb), 0)
        dst = lax.broadcasted_iota(jnp.int32, (t, tb), 1) + pl.program_id(0) * tb
        dlogf = _tri_dot(d_ref[...], (src >= dst).astype(BF16))
        z = z_ref[...] + b_ref[...]
        dz = dlogf * _sigmoid(-z)
        dz_ref[...] = dz
        db_ref[...] += jnp.sum(dz, axis=1, keepdims=True)

    return _pcall(name, kern, (t // tb,), (z_t, b_f, dcum_t),
                  [pl.BlockSpec((h, tb), lambda j: (0, j)), pl.BlockSpec((h, 1), lambda j: (0, 0)),
                   pl.BlockSpec((h, t), lambda j: (0, 0))],
                  [jax.ShapeDtypeStruct((h, t), F32), jax.ShapeDtypeStruct((h, 1), F32)],
                  [pl.BlockSpec((h, tb), lambda j: (0, j)), pl.BlockSpec((h, 1), lambda j: (0, 0))], ("arbitrary",))


def _fox_spans(qs, k_ref, cr_ref, i, tq):
    n0 = i * tq
    sd = _dot(qs, k_ref[n0:n0 + tq, :], NT) - cr_ref[:, n0:n0 + tq]
    row = lax.broadcasted_iota(jnp.int32, (tq, tq), 0)
    col = lax.broadcasted_iota(jnp.int32, (tq, tq), 1)
    spans = [(n0, tq, jnp.where(row >= col, sd, NEG_INF))]
    if i > 0:
        spans.append((0, n0, _dot(qs, k_ref[0:n0, :], NT) - cr_ref[:, 0:n0]))
    return spans


def fox_fwd(name, q, k, v, cum_r, tq=256):
    h, t, dh = q.shape
    tq = min(tq, t)
    scale = FOX_HEAD_DIM ** -0.5

    def kern(q_ref, k_ref, v_ref, cr_ref, o_ref, lse_ref):
        for i in range(t // tq):
            rows = slice(i * tq, (i + 1) * tq)
            spans = _fox_spans(q_ref[rows, :] * scale, k_ref, cr_ref, i, tq)
            m = functools.reduce(jnp.maximum, [jnp.max(s, axis=-1, keepdims=True) for _, _, s in spans])
            l, acc = 0.0, 0.0
            for k0, kn, s in spans:
                p = jnp.exp(s - m)
                l = l + jnp.sum(p, axis=-1, keepdims=True)
                acc = acc + _dot(p, v_ref[k0:k0 + kn, :])
            o_ref[rows, :] = (acc / l).astype(o_ref.dtype)
            lse_ref[rows, :] = m + jnp.log(l)

    hspec = pl.BlockSpec((None, t, dh), lambda a: (a, 0, 0))
    cspec = pl.BlockSpec((None, t, 1), lambda a: (a, 0, 0))
    rspec = pl.BlockSpec((None, 1, t), lambda a: (a, 0, 0))
    return _pcall(name, kern, (h,), (q, k, v, cum_r), [hspec, hspec, hspec, rspec],
                  [jax.ShapeDtypeStruct((h, t, dh), BF16), jax.ShapeDtypeStruct((h, t, 1), F32)],
                  [hspec, cspec], ("parallel",))


def fox_bwd(name, q, k, v, do, lse, cum_r, tq=256):
    h, t, dh = q.shape
    tq = min(tq, t)
    scale = FOX_HEAD_DIM ** -0.5

    def kern(q_ref, k_ref, v_ref, do_ref, lse_ref, cr_ref, dq_ref, dk_ref, dv_ref, dc_ref):
        dk_ref[...] = jnp.zeros_like(dk_ref)
        dv_ref[...] = jnp.zeros_like(dv_ref)
        dc_ref[...] = jnp.zeros_like(dc_ref)
        for i in range(t // tq):
            rows = slice(i * tq, (i + 1) * tq)
            qs, dov, lse_v = q_ref[rows, :] * scale, do_ref[rows, :], lse_ref[rows, :]
            spans = _fox_spans(qs, k_ref, cr_ref, i, tq)
            probs = [jnp.exp(s - lse_v) for _, _, s in spans]
            dps = [_dot(dov, v_ref[k0:k0 + kn, :], NT) for k0, kn, _ in spans]
            rowdot = sum(jnp.sum(dp * p, axis=-1, keepdims=True) for dp, p in zip(dps, probs))
            dq = 0.0
            for (k0, kn, _), p, dp in zip(spans, probs, dps):
                ds = p * (dp - rowdot)
                dq = dq + _dot(ds, k_ref[k0:k0 + kn, :])
                dk_ref[k0:k0 + kn, :] += _dot(ds, qs, TN)
                dv_ref[k0:k0 + kn, :] += _dot(p, dov, TN)
                dc_ref[:, k0:k0 + kn] -= jnp.sum(ds, axis=0, keepdims=True)
            dq_ref[rows, :] = (dq * scale).astype(dq_ref.dtype)

    hspec = pl.BlockSpec((None, t, dh), lambda a: (a, 0, 0))
    cspec = pl.BlockSpec((None, t, 1), lambda a: (a, 0, 0))
    rspec = pl.BlockSpec((None, 1, t), lambda a: (a, 0, 0))
    return _pcall(name, kern, (h,), (q, k, v, do, lse, cum_r), [hspec, hspec, hspec, hspec, cspec, rspec],
                  [jax.ShapeDtypeStruct((h, t, dh), BF16), jax.ShapeDtypeStruct((h, t, dh), F32),
                   jax.ShapeDtypeStruct((h, t, dh), F32), jax.ShapeDtypeStruct((h, 1, t), F32)],
                  [hspec, hspec, hspec, rspec], ("parallel",))


def _xattn_probs(q, k):
    s = _dot(q, k, NT) * (MEM_HEAD_DIM ** -0.5)
    p = jnp.exp(s - jnp.max(s, axis=-1, keepdims=True))
    return p / jnp.sum(p, axis=-1, keepdims=True)


def xattn_fwd(name, q, kv, tq=512):
    t = q.shape[0]
    tq = min(tq, t)
    ml = kv.shape[0]

    def kern(q_ref, k_ref, v_ref, o_ref):
        o_ref[...] = _dot(_xattn_probs(q_ref[...], k_ref[...]), v_ref[...]).astype(o_ref.dtype)

    qspec = pl.BlockSpec((tq, MEM_HEAD_DIM), lambda i, a: (i, a))
    return _pcall(name, kern, (t // tq, MEM_HEADS), (q, kv, kv),
                  [qspec, pl.BlockSpec((ml, MEM_HEAD_DIM), lambda i, a: (0, a)),
                   pl.BlockSpec((ml, MEM_HEAD_DIM), lambda i, a: (0, MEM_HEADS + a))],
                  jax.ShapeDtypeStruct(q.shape, BF16), qspec, ("parallel", "parallel"))


def xattn_bwd(name, q, kv, do, tq=512):
    t = q.shape[0]
    tq = min(tq, t)
    ml = kv.shape[0]
    scale = MEM_HEAD_DIM ** -0.5

    def kern(q_ref, k_ref, v_ref, do_ref, dq_ref, dk_ref, dv_ref):
        @pl.when(pl.program_id(1) == 0)
        def _():
            dk_ref[...] = jnp.zeros_like(dk_ref)
            dv_ref[...] = jnp.zeros_like(dv_ref)
        qv, kv_, dov = q_ref[...], k_ref[...], do_ref[...]
        p = _xattn_probs(qv, kv_)
        dp = _dot(dov, v_ref[...], NT)
        ds = p * (dp - jnp.sum(dp * p, axis=-1, keepdims=True)) * scale
        dq_ref[...] = _dot(ds, kv_).astype(dq_ref.dtype)
        dk_ref[...] += _dot(ds, qv, TN)
        dv_ref[...] += _dot(p, dov, TN)

    qspec = pl.BlockSpec((tq, MEM_HEAD_DIM), lambda a, i: (i, a))
    kspec = pl.BlockSpec((ml, MEM_HEAD_DIM), lambda a, i: (0, a))
    return _pcall(name, kern, (MEM_HEADS, t // tq), (q, kv, kv, do),
                  [qspec, kspec, pl.BlockSpec((ml, MEM_HEAD_DIM), lambda a, i: (0, MEM_HEADS + a)), qspec],
                  [jax.ShapeDtypeStruct(q.shape, BF16), jax.ShapeDtypeStruct((ml, D_MODEL), F32),
                   jax.ShapeDtypeStruct((ml, D_MODEL), F32)],
                  [qspec, kspec, kspec], ("parallel", "arbitrary"))


def _heads(x):
    t = x.shape[0]
    return x.reshape(t, FOX_HEADS, FOX_HEAD_DIM).transpose(1, 0, 2)


def _unheads(x):
    return x.transpose(1, 0, 2).reshape(x.shape[1], FOX_WIDTH)


def _row_cut(dw):
    return dw.reshape(N_CHIPS, 2, dw.shape[0] // (2 * N_CHIPS), dw.shape[1])


def local_step(x, mem, target, w, layer_weights=None, reduce_hook=None):
    depth = w["g_mix_pre"].shape[0]
    t = x.shape[0]
    saved = []
    i1, i2, i3 = 3 * FOX_WIDTH, 3 * FOX_WIDTH + FOX_HEADS, AB_IN
    ncol = 128

    def stacked_weights(layer, _):
        return {n: w[n][layer if n in COMMON_BIG else layer // 2] for n in layer_big(layer)}

    for layer in range(depth):
        lw = (layer_weights or stacked_weights)(layer, x)
        s = {"x0": x, "lw": lw}
        tag = f"l{layer}"
        h1 = rms_pre(f"{tag}_mix_pre", x, w["g_mix_pre"], layer)
        s["h1"] = h1
        if layer % 2 == 0:
            e = layer // 2
            w_in = jnp.pad(lw["ab_w_in"], ((0, 0), (0, AB_IN_PAD - AB_IN)))
            proj = mm(f"{tag}_ab_in", h1, w_in, "nn", F32)
            qkv = proj[:, :i1].astype(BF16).reshape(t, 3, FOX_HEADS, FOX_HEAD_DIM).transpose(1, 2, 0, 3)
            z_t = proj[:, i1:i2].T
            b_f = w["ab_b_f"][e].reshape(FOX_HEADS, 1)
            cum_t = fox_gates_fwd(f"{tag}_fox_gates", z_t, b_f)
            cum_r = cum_t[:, None, :]
            oh, lse = fox_fwd(f"{tag}_fox", qkv[0], qkv[1], qkv[2], cum_r)
            bcu = proj[:, i2:i3]
            y_b = sconv_fwd(f"{tag}_sconv", bcu, 0, w["ab_conv_w"][e])
            ycat = jnp.concatenate([_unheads(oh), y_b], axis=1)
            y1 = mm(f"{tag}_ab_out", ycat, lw["ab_w_out"], "nn", F32)
            s.update(w_in=w_in, qkv=qkv, z_t=z_t, b_f=b_f, cum_r=cum_r, lse=lse, bcu=bcu, ycat=ycat)
        else:
            o = layer // 2
            gu2 = mm(f"{tag}_c_in", h1, lw["c_w_in"], "nn", F32)
            conv_b = w["c_conv_b"][o].reshape(1, -1)
            uc = rg_conv_fwd(f"{tag}_rg_conv", gu2, w["c_conv_w"][o], conv_b)
            b_a, b_i = w["c_b_a"][o].reshape(1, -1), w["c_b_i"][o].reshape(1, -1)
            r, i = rg_gates_fwd(f"{tag}_rg_gates", uc, lw["c_w_a"], b_a, lw["c_w_i"], b_i)
            lam = w["c_lam"][o].reshape(1, -1)
            ymix, hs = rg_scan_fwd(f"{tag}_rg_scan", gu2, uc, r, i, lam)
            y1 = mm(f"{tag}_c_out", ymix, lw["c_w_out"], "nn", F32)
            s.update(gu2=gu2, uc=uc, r=r, i=i, lam=lam, hs=hs, ymix=ymix)
        s["y1"] = y1
        x = post_add(f"{tag}_mix_post", x, y1, w["g_mix_post"], layer)
        s["x1"] = x
        h2 = rms_pre(f"{tag}_cross_pre", x, w["g_cross_pre"], layer)
        m = rms_pre(f"{tag}_mem_pre", mem, w["g_mem"], layer)
        q = mm(f"{tag}_xq", h2, lw["w_xq"], "nn", BF16)
        kv = mm(f"{tag}_xkv", m, lw["w_xkv"], "nn", BF16)
        o_att = xattn_fwd(f"{tag}_xattn", q, kv)
        y2 = mm(f"{tag}_xo", o_att, lw["w_xo"], "nn", F32)
        s.update(h2=h2, m=m, q=q, kv=kv, o_att=o_att, y2=y2)
        x = post_add(f"{tag}_cross_post", x, y2, w["g_cross_post"], layer)
        s["x2"] = x
        h3 = rms_pre(f"{tag}_ffn_pre", x, w["g_ffn_pre"], layer)
        gu = mm(f"{tag}_ffn_gu", h3, lw["w_ffn_gu"], "nn", BF16)
        act = swiglu_fwd(f"{tag}_swiglu", gu)
        y3 = mm(f"{tag}_ffn_down", act, lw["w_ffn_down"], "nn", F32)
        s.update(h3=h3, gu=gu, act=act, y3=y3)
        x = post_add(f"{tag}_ffn_post", x, y3, w["g_ffn_post"], layer)
        saved.append(s)

    dx, sq_cols = loss_head("loss_head", x, target)

    grads = {k: [None] * v.shape[0] for k, v in w.items() if k not in BIG}
    big, token = {}, None

    def dw(name, a, b, cols_cut=False):
        return mm(name, a, b, "tn", BF16, reduce_layout=True) if cols_cut else _row_cut(mm(name, a, b, "tn", BF16))

    for layer in reversed(range(depth)):
        s = saved[layer]
        lw = s["lw"]
        tag = f"b{layer}"
        lg = {}
        g_ffn_post = w["g_ffn_post"] if token is None else w["g_ffn_post"] + token
        dy3, grads["g_ffn_post"][layer] = post_bwd(f"{tag}_ffn_post", s["y3"], dx, g_ffn_post, layer)
        dact = mm(f"{tag}_ffn_down_dx", dy3, lw["w_ffn_down"], "nt", BF16)
        lg["w_ffn_down"] = dw(f"{tag}_ffn_down_dw", s["act"], dy3)
        dgu = swiglu_bwd(f"{tag}_swiglu", s["gu"], dact)
        dh3 = mm(f"{tag}_ffn_gu_dx", dgu, lw["w_ffn_gu"], "nt", F32)
        lg["w_ffn_gu"] = dw(f"{tag}_ffn_gu_dw", s["h3"], dgu, cols_cut=True)
        dx, grads["g_ffn_pre"][layer] = pre_bwd(f"{tag}_ffn_pre", s["x2"], dh3, dx, w["g_ffn_pre"], layer)
        dy2, grads["g_cross_post"][layer] = post_bwd(f"{tag}_cross_post", s["y2"], dx, w["g_cross_post"], layer)
        do = mm(f"{tag}_xo_dx", dy2, lw["w_xo"], "nt", BF16)
        lg["w_xo"] = dw(f"{tag}_xo_dw", s["o_att"], dy2)
        dq, dk, dv = xattn_bwd(f"{tag}_xattn", s["q"], s["kv"], do)
        dh2 = mm(f"{tag}_xq_dx", dq, lw["w_xq"], "nt", F32)
        lg["w_xq"] = dw(f"{tag}_xq_dw", s["h2"], dq)
        dkv = jnp.concatenate([dk, dv], axis=1).astype(BF16)
        dm = mm(f"{tag}_xkv_dx", dkv, lw["w_xkv"], "nt", F32)
        lg["w_xkv"] = dw(f"{tag}_xkv_dw", s["m"], dkv, cols_cut=True)
        grads["g_mem"][layer] = gain_bwd(f"{tag}_mem_pre", mem, dm)
        dx, grads["g_cross_pre"][layer] = pre_bwd(f"{tag}_cross_pre", s["x1"], dh2, dx, w["g_cross_pre"], layer)
        dy1, grads["g_mix_post"][layer] = post_bwd(f"{tag}_mix_post", s["y1"], dx, w["g_mix_post"], layer)
        if layer % 2 == 0:
            e = layer // 2
            dycat = mm(f"{tag}_ab_out_dx", dy1, lw["ab_w_out"], "nt", F32)
            lg["ab_w_out"] = dw(f"{tag}_ab_out_dw", s["ycat"], dy1)
            do_h = _heads(dycat[:, :FOX_WIDTH].astype(BF16))
            qkv = s["qkv"]
            dqh, dkh, dvh, dcum = fox_bwd(f"{tag}_fox", qkv[0], qkv[1], qkv[2], do_h, s["lse"], s["cum_r"])
            dz_t, db_f = fox_gates_bwd(f"{tag}_fox_gates", s["z_t"], s["b_f"], dcum.reshape(FOX_HEADS, t))
            grads["ab_b_f"][e] = db_f.reshape(FOX_HEADS)
            db, dc, du, dconv_w = sconv_bwd(f"{tag}_sconv", s["bcu"], 0, w["ab_conv_w"][e], dycat, FOX_WIDTH // ncol)
            grads["ab_conv_w"][e] = dconv_w
            dproj = jnp.concatenate(
                [_unheads(dqh), _unheads(dkh).astype(BF16), _unheads(dvh).astype(BF16), dz_t.T.astype(BF16), db, dc, du,
                 jnp.zeros((t, AB_IN_PAD - AB_IN), BF16)], axis=1)
            dh1 = mm(f"{tag}_ab_in_dx", dproj, s["w_in"], "nt", F32)
            dw_in = mm(f"{tag}_ab_in_dw", s["h1"], dproj, "tn", F32)[:, :AB_IN]
            lg["ab_w_in"] = dw_in.reshape(2, D_MODEL // 2, N_CHIPS, AB_IN // N_CHIPS).transpose(2, 0, 1, 3).astype(BF16)
        else:
            o = layer // 2
            dymix = mm(f"{tag}_c_out_dx", dy1, lw["c_w_out"], "nt", F32)
            lg["c_w_out"] = dw(f"{tag}_c_out_dw", s["ymix"], dy1)
            dgate, dzr, dzi, duc_part, db_a, db_i, dlam = rg_scan_bwd(
                f"{tag}_rg_scan", s["gu2"], s["uc"], s["r"], s["i"], s["hs"], dymix, s["lam"])
            duc, dw_a, dw_i = rg_gates_bwd(f"{tag}_rg_gates", s["uc"], dzr, dzi, duc_part, lw["c_w_a"], lw["c_w_i"])
            lg["c_w_a"] = dw_a.reshape(N_CHIPS, 2, LRU_BW // 2, LRU_BW)
            lg["c_w_i"] = dw_i.reshape(N_CHIPS, 2, LRU_BW // 2, LRU_BW)
            du_raw, dconv_w, dconv_b = rg_conv_bwd(f"{tag}_rg_conv", s["gu2"], duc, w["c_conv_w"][o])
            grads["c_b_a"][o] = db_a.reshape(LRU_BLOCKS, LRU_BW)
            grads["c_b_i"][o] = db_i.reshape(LRU_BLOCKS, LRU_BW)
            grads["c_lam"][o] = dlam.reshape(-1)
            grads["c_conv_w"][o] = dconv_w
            grads["c_conv_b"][o] = dconv_b.reshape(-1)
            dgu2 = jnp.concatenate([dgate, du_raw], axis=1)
            dh1 = mm(f"{tag}_c_in_dx", dgu2, lw["c_w_in"], "nt", F32)
            lg["c_w_in"] = dw(f"{tag}_c_in_dw", s["h1"], dgu2, cols_cut=True)
        dx, grads["g_mix_pre"][layer] = pre_bwd(f"{tag}_mix_pre", s["x0"], dh1, dx, w["g_mix_pre"], layer)
        if reduce_hook is None:
            big[layer] = lg
        else:
            token = reduce_hook(layer, lg)

    for k in list(grads):
        if k.startswith("g_"):
            grads[k] = [g.reshape(-1) for g in grads[k]]
        grads[k] = jnp.stack(grads[k])
    return sq_cols, dx, grads, big


CHIP_FLIPS = ((1, 0), (0, 1), (1, 1))
HBM_SPEC = pl.BlockSpec(memory_space=pltpu.HBM)
VMEM_SPEC = pl.BlockSpec(memory_space=pltpu.VMEM)


def _place():
    return lax.axis_index("x"), lax.axis_index("y"), lax.axis_index("c")


def _flip(v, f):
    return 1 - v if f else v


def _remote(src, dst, send_sem, recv_sem, target):
    return pltpu.make_async_remote_copy(src_ref=src, dst_ref=dst, send_sem=send_sem, recv_sem=recv_sem,
                                        device_id=target, device_id_type=MESH)


def _comm_call(name, body, ins, out_shape, n_sems):
    n = len(ins)
    return pl.pallas_call(
        body, name=name, in_specs=[HBM_SPEC] * n, out_specs=[HBM_SPEC] * len(out_shape), out_shape=out_shape,
        scratch_shapes=[pltpu.SemaphoreType.DMA((n, n_sems)), pltpu.SemaphoreType.DMA((n, n_sems))],
    )(*ins)


def swap_with_sibling(name, blocks):
    n = len(blocks)

    def body(*refs):
        ins, outs = refs[:n], refs[n:2 * n]
        send_sems, recv_sems = refs[2 * n:]
        x, y, c = _place()
        cps = [_remote(ins[a], outs[a], send_sems.at[a, 0], recv_sems.at[a, 0], (x, y, 1 - c)) for a in range(n)]
        for cp in cps:
            cp.start()
        for cp in cps:
            cp.wait()

    return _comm_call(name, body, blocks, [jax.ShapeDtypeStruct(b.shape, b.dtype) for b in blocks], 1)


def swap_other_half(name, arrays):
    n = len(arrays)

    def body(*refs):
        ins, outs = refs[:n], refs[n:2 * n]
        send_sems, recv_sems = refs[2 * n:]
        x, y, c = _place()
        cps = [_remote(ins[a].at[:, 1 - c], outs[a], send_sems.at[a, 0], recv_sems.at[a, 0], (x, y, 1 - c))
               for a in range(n)]
        for cp in cps:
            cp.start()
        for cp in cps:
            cp.wait()

    out_shape = [jax.ShapeDtypeStruct((b.shape[0], *b.shape[2:]), b.dtype) for b in arrays]
    return _comm_call(name, body, arrays, out_shape, 1)


SEM_SPEC = pl.BlockSpec(memory_space=pltpu.SEMAPHORE)


def _exchange_copies(srcs, lands, send_sems, recv_sems):
    x, y, c = _place()
    p = 2 * x + y
    cps = []
    for a, (src, land) in enumerate(zip(srcs, lands)):
        for k, (fx, fy) in enumerate(CHIP_FLIPS):
            qx, qy = _flip(x, fx), _flip(y, fy)
            sem = len(CHIP_FLIPS) * a + k
            cps.append(_remote(src.at[2 * qx + qy], land.at[p], send_sems.at[sem], recv_sems.at[sem], (qx, qy, c)))
    return cps


def _gather_copies(srcs, lands, send_sems, recv_sems):
    x, y, c = _place()
    p = 2 * x + y
    cps = []
    for a, (src, land) in enumerate(zip(srcs, lands)):
        for k, (fx, fy) in enumerate(CHIP_FLIPS):
            sem = len(CHIP_FLIPS) * a + k
            cps.append(_remote(src.at[c], land.at[p, c], send_sems.at[sem], recv_sems.at[sem],
                               (_flip(x, fx), _flip(y, fy), c)))
    return cps


def copies_start(name, make_copies, srcs, land_shapes):
    n = len(srcs)

    def body(*refs):
        for cp in make_copies(refs[:n], refs[n:2 * n], refs[2 * n], refs[2 * n + 1]):
            cp.start()
        refs[-1][...] = jnp.zeros_like(refs[-1])

    thru = [pltpu.HBM(b.shape, b.dtype) for b in srcs] + [pltpu.HBM(sh, b.dtype) for sh, b in zip(land_shapes, srcs)]
    outs = pl.pallas_call(
        body, name=name, in_specs=[HBM_SPEC] * (2 * n),
        out_shape=(pltpu.SemaphoreType.DMA((3 * n,)), pltpu.SemaphoreType.DMA((3 * n,)), *thru,
                   jax.ShapeDtypeStruct((8, 128), F32)),
        out_specs=(SEM_SPEC, SEM_SPEC, *[HBM_SPEC] * (2 * n), VMEM_SPEC),
        input_output_aliases={i: 2 + i for i in range(2 * n)},
        compiler_params=pltpu.CompilerParams(has_side_effects=pltpu.SideEffectType.DATAFLOW_SIDE_EFFECTING),
    )(*[pltpu.with_memory_space_constraint(b, pltpu.HBM) for b in srcs],
      *[pltpu.with_memory_space_constraint(lax.empty(sh, b.dtype), pltpu.HBM) for sh, b in zip(land_shapes, srcs)])
    return outs[:-1], outs[-1]


def copies_wait(name, make_copies, state, after):
    send_sems, recv_sems, *thru = state
    n = len(thru) // 2

    def body(*refs):
        for cp in make_copies(refs[:n], refs[n:2 * n], refs[2 * n], refs[2 * n + 1]):
            cp.wait_send()
            cp.wait_recv()

    outs = pl.pallas_call(
        body, name=name, in_specs=[HBM_SPEC] * (2 * n) + [SEM_SPEC, SEM_SPEC, pl.BlockSpec(memory_space=pl.ANY)],
        out_shape=tuple(pltpu.HBM(t.shape, t.dtype) for t in thru), out_specs=tuple([HBM_SPEC] * (2 * n)),
        input_output_aliases={i: i for i in range(2 * n)},
        compiler_params=pltpu.CompilerParams(has_side_effects=pltpu.SideEffectType.DATAFLOW_SIDE_EFFECTING),
    )(*thru, send_sems, recv_sems, after)
    return outs[n:]


def pass_to_sibling(name, lands):
    n = len(lands)

    def body(*refs):
        ins, outs = refs[:n], refs[n:2 * n]
        send_sems, recv_sems = refs[2 * n:]
        x, y, c = _place()
        cps = []
        for a in range(n):
            for k, (fx, fy) in enumerate(CHIP_FLIPS):
                q = 2 * _flip(x, fx) + _flip(y, fy)
                cps.append(_remote(ins[a].at[q, c], outs[a].at[q, c], send_sems.at[a, k], recv_sems.at[a, k],
                                   (x, y, 1 - c)))
        for cp in cps:
            cp.start()
        for cp in cps:
            cp.wait()

    return pl.pallas_call(
        body, name=name, in_specs=[HBM_SPEC] * n, out_specs=[HBM_SPEC] * n,
        out_shape=[jax.ShapeDtypeStruct(b.shape, b.dtype) for b in lands],
        scratch_shapes=[pltpu.SemaphoreType.DMA((n, 3)), pltpu.SemaphoreType.DMA((n, 3))],
        input_output_aliases={i: i for i in range(n)},
    )(*lands)


DEVICE_FLIPS = tuple((fx, fy, fc) for fx in (0, 1) for fy in (0, 1) for fc in (0, 1))[1:]


def gather_small(name, v, reduce):
    r, cdim = v.shape
    n_dev = 8

    def body(v_ref, out_ref, *scratch):
        buf = scratch[0] if reduce else out_ref
        send_sems, recv_sems = scratch[-2:]
        x, y, c = _place()
        me = 4 * x + 2 * y + c
        buf[me] = v_ref[...]
        cps = []
        for k, (fx, fy, fc) in enumerate(DEVICE_FLIPS):
            cps.append(_remote(v_ref, buf.at[me], send_sems.at[k], recv_sems.at[k],
                               (_flip(x, fx), _flip(y, fy), _flip(c, fc))))
        for cp in cps:
            cp.start()
        for cp in cps:
            cp.wait()
        if reduce:
            total = buf[0]
            for d in range(1, n_dev):
                total = total + buf[d]
            out_ref[...] = total

    scratch = [pltpu.SemaphoreType.DMA((7,)), pltpu.SemaphoreType.DMA((7,))]
    if reduce:
        scratch = [pltpu.VMEM((n_dev, r, cdim), F32)] + scratch
    out_shape = jax.ShapeDtypeStruct((r, cdim) if reduce else (n_dev, r, cdim), F32)
    return pl.pallas_call(body, name=name, in_specs=[VMEM_SPEC], out_specs=VMEM_SPEC, out_shape=out_shape,
                          scratch_shapes=scratch)(v)


def pair_sum(name, own, got, core):
    _, hx, cols = got.shape
    tr = _tile(hx, (256, 128, 64, 32, 16))

    def kern(core_ref, a_ref, b_ref, o_ref):
        o_ref[...] = (a_ref[...].astype(F32) + b_ref[...].astype(F32)).astype(BF16)

    grid_spec = pltpu.PrefetchScalarGridSpec(
        num_scalar_prefetch=1, grid=(hx // tr,),
        in_specs=[pl.BlockSpec((N_CHIPS, None, tr, cols), lambda i, cr: (0, cr[0], i, 0)),
                  pl.BlockSpec((N_CHIPS, tr, cols), lambda i, cr: (0, i, 0))],
        out_specs=pl.BlockSpec((N_CHIPS, tr, cols), lambda i, cr: (0, i, 0)))
    return pl.pallas_call(
        kern, name=name, grid_spec=grid_spec, out_shape=jax.ShapeDtypeStruct(got.shape, BF16),
        compiler_params=pltpu.CompilerParams(dimension_semantics=("parallel",), vmem_limit_bytes=VMEM_LIMIT_BYTES),
    )(core, own, got)


def chip_sum(name, mine, parts, chip):
    _, hx, yd = parts.shape
    tr = _tile(hx, (256, 128, 64, 32, 16))

    def kern(chip_ref, m_ref, p_ref, o_ref):
        total = None
        for q in range(N_CHIPS):
            term = jnp.where(chip_ref[0] == q, m_ref[...], p_ref[q]).astype(F32)
            total = term if total is None else total + term
        o_ref[...] = total

    grid_spec = pltpu.PrefetchScalarGridSpec(
        num_scalar_prefetch=1, grid=(hx // tr,),
        in_specs=[pl.BlockSpec((None, tr, yd), lambda i, cr: (cr[0], i, 0)),
                  pl.BlockSpec((N_CHIPS, tr, yd), lambda i, cr: (0, i, 0))],
        out_specs=pl.BlockSpec((tr, yd), lambda i, cr: (i, 0)))
    return pl.pallas_call(
        kern, name=name, grid_spec=grid_spec, out_shape=jax.ShapeDtypeStruct((hx, yd), F32),
        compiler_params=pltpu.CompilerParams(dimension_semantics=("parallel",), vmem_limit_bytes=VMEM_LIMIT_BYTES),
    )(chip, mine, parts)


WEIGHTS = ("g_mix_pre", "g_mix_post", "g_cross_pre", "g_mem", "g_cross_post", "g_ffn_pre", "g_ffn_post", "w_xq", "w_xkv",
           "w_xo", "w_ffn_gu", "w_ffn_down", "ab_w_in", "ab_b_f", "ab_conv_w", "ab_w_out", "c_w_in", "c_conv_w",
           "c_conv_b", "c_w_a", "c_b_a", "c_w_i", "c_b_i", "c_lam", "c_w_out")
SHARD_DIM = {"w_xq": 1, "w_xkv": 2, "w_xo": 1, "w_ffn_gu": 2, "w_ffn_down": 1, "ab_w_in": 2, "ab_conv_w": 2,
             "ab_w_out": 1, "c_w_in": 2, "c_conv_w": 2, "c_conv_b": 1, "c_w_a": 2, "c_b_a": 2, "c_w_i": 2, "c_b_i": 2,
             "c_lam": 1, "c_w_out": 1}
COMMON_BIG = ("w_xq", "w_xkv", "w_xo", "w_ffn_gu", "w_ffn_down")
EVEN_BIG, ODD_BIG = ("ab_w_in", "ab_w_out"), ("c_w_in", "c_w_a", "c_w_i", "c_w_out")
BIG = COMMON_BIG + EVEN_BIG + ODD_BIG


def layer_big(layer):
    return COMMON_BIG + (ODD_BIG if layer % 2 else EVEN_BIG)


SMALL_SHARDED = ("ab_conv_w", "c_conv_w", "c_conv_b", "c_b_a", "c_b_i", "c_lam")
REPLICATED = ("g_mix_pre", "g_mix_post", "g_cross_pre", "g_mem", "g_cross_post", "g_ffn_pre", "g_ffn_post", "ab_b_f")
PACK_COLS = 1024


def _unshard(g, d):
    shard = g.shape[1:]
    return jnp.moveaxis(g, 0, d).reshape(shard[:d] + (N_CHIPS * shard[d],) + shard[d + 1:])


def _shardify(full, d):
    s = full.shape
    return jnp.moveaxis(full.reshape(s[:d] + (N_CHIPS, s[d] // N_CHIPS) + s[d + 1:]), d, 0)


def _pack(arrays, rows):
    flat = jnp.concatenate([a.reshape(-1).astype(F32) for a in arrays])
    return jnp.pad(flat, (0, rows * PACK_COLS - flat.shape[0])).reshape(rows, PACK_COLS)


def _unpack(packed, shapes):
    flat = packed.reshape(-1)
    out, at = [], 0
    for s in shapes:
        size = math.prod(s)
        out.append(flat[at:at + size].reshape(s))
        at += size
    return out


def _rows_for(shapes):
    return -(-sum(math.prod(s) for s in shapes) // (8 * PACK_COLS)) * 8


def kernel(x, mem, g_mix_pre, g_mix_post, g_cross_pre, g_mem, g_cross_post, g_ffn_pre, g_ffn_post, w_xq, w_xkv, w_xo, w_ffn_gu, w_ffn_down, ab_w_in, ab_b_f, ab_conv_w, ab_w_out, c_w_in, c_conv_w, c_conv_b, c_w_a, c_b_a, c_w_i, c_b_i, c_lam, c_w_out, loss_target, m_g_mix_pre, m_g_mix_post, m_g_cross_pre, m_g_mem, m_g_cross_post, m_g_ffn_pre, m_g_ffn_post, m_w_xq, m_w_xkv, m_w_xo, m_w_ffn_gu, m_w_ffn_down, m_ab_w_in, m_ab_b_f, m_ab_conv_w, m_ab_w_out, m_c_w_in, m_c_conv_w, m_c_conv_b, m_c_w_a, m_c_b_a, m_c_w_i, m_c_b_i, m_c_lam, m_c_w_out, v_g_mix_pre, v_g_mix_post, v_g_cross_pre, v_g_mem, v_g_cross_post, v_g_ffn_pre, v_g_ffn_post, v_w_xq, v_w_xkv, v_w_xo, v_w_ffn_gu, v_w_ffn_down, v_ab_w_in, v_ab_b_f, v_ab_conv_w, v_ab_w_out, v_c_w_in, v_c_conv_w, v_c_conv_b, v_c_w_a, v_c_b_a, v_c_w_i, v_c_b_i, v_c_lam, v_c_w_out):
    given = dict(locals())
    w = {n: given[n] for n in WEIGHTS}
    m_in = {n: given["m_" + n] for n in WEIGHTS}
    v_in = {n: given["v_" + n] for n in WEIGHTS}
    xi, yi, ci = _place()
    chip = 2 * xi + yi

    depth = g_mix_pre.shape[0]
    own, gathers, tokens = [], [], []
    for layer in range(depth):
        own.append({n: w[n][layer if n in COMMON_BIG else layer // 2].astype(BF16) for n in layer_big(layer)})
        halves = [a.reshape(2, -1, a.shape[-1]) for a in own[layer].values()]
        state, token = copies_start(f"gather_start_l{layer}", _gather_copies, halves, [(N_CHIPS, *h.shape) for h in halves])
        gathers.append(state)
        tokens.append(token[0, 0])

    def layer_weights(layer, x_in):
        lands = copies_wait(f"gather_wait_l{layer}", _gather_copies, gathers[layer], x_in)
        lands = pass_to_sibling(f"gather_pass_l{layer}", lands)
        out = {}
        for (n, mine), g in zip(own[layer].items(), lands):
            slot = lax.broadcasted_iota(jnp.int32, (N_CHIPS,) + (1,) * mine.ndim, 0)
            g = jnp.where(slot == chip, mine[None], g.reshape(N_CHIPS, *mine.shape))
            out[n] = _unshard(g, SHARD_DIM[n] - 1)
        return out

    full = {}
    small_shapes = [w[n].shape for n in SMALL_SHARDED]
    rows_w = _rows_for(small_shapes)
    every = gather_small("gather_small_weights", _pack([w[n] for n in SMALL_SHARDED], rows_w), reduce=False)
    per_chip = every[0::2].reshape(N_CHIPS, -1)
    at = 0
    for n, s in zip(SMALL_SHARDED, small_shapes):
        size = math.prod(s)
        full[n] = _unshard(per_chip[:, at:at + size].reshape(N_CHIPS, *s), SHARD_DIM[n])
        at += size
    for n in REPLICATED:
        full[n] = w[n]

    core_arr, chip_arr = ci.reshape(1).astype(jnp.int32), chip.reshape(1).astype(jnp.int32)
    in_flight = []

    def reduce_hook(layer, layer_grads):
        names = list(layer_grads)
        own = [layer_grads[n] for n in names]
        got = swap_other_half(f"swap_grads_l{layer}", own)
        sums = [pair_sum(f"pair_sum_l{layer}_{n}", o, g, core_arr) for n, o, g in zip(names, own, got)]
        state, token = copies_start(f"exchange_start_l{layer}", _exchange_copies, sums, [b.shape for b in sums])
        in_flight.append((layer, names, sums, state))
        return token[0, 0]

    sq_cols, dx, grads, _ = local_step(x[0] + sum(tokens), mem[0], loss_target[0], full, layer_weights, reduce_hook)
    loss = lax.psum(0.5 / D_MODEL * jnp.sum(sq_cols), ("x", "y", "c"))

    keys, halves = [], []
    for layer, names, sums, state in in_flight:
        parts = copies_wait(f"exchange_wait_l{layer}", _exchange_copies, state, dx)
        for n, mine, p in zip(names, sums, parts):
            keys.append((n, layer))
            halves.append(chip_sum(f"chip_sum_l{layer}_{n}", mine, p, chip_arr))
    others = swap_with_sibling("swap_reduced_halves", halves)
    reduced = {k: (mine, theirs) for k, mine, theirs in zip(keys, halves, others)}
    grad_out = {}
    for n in BIG:
        pieces = []
        for layer in sorted(l for k, l in keys if k == n):
            mine, theirs = reduced[(n, layer)]
            pieces += [jnp.where(ci == 0, mine, theirs), jnp.where(ci == 0, theirs, mine)]
        grad_out[n] = jnp.concatenate(pieces, axis=0).reshape(w[n].shape)

    small_names = REPLICATED + SMALL_SHARDED
    small_full_shapes = [grads[n].shape for n in small_names]
    total = gather_small("reduce_small_grads", _pack([grads[n] for n in small_names], _rows_for(small_full_shapes)),
                         reduce=True)
    for n, g in zip(small_names, _unpack(total, small_full_shapes)):
        if n in SHARD_DIM:
            g = lax.dynamic_index_in_dim(_shardify(g, SHARD_DIM[n]), chip, axis=0, keepdims=False)
        grad_out[n] = g

    delta, new_m, new_v = {}, {}, {}
    for n in BIG:
        two_d = lambda a: a.reshape(-1, a.shape[-1])
        d, m2, v2 = adamw(f"adamw_{n}", two_d(w[n]), two_d(grad_out[n]), two_d(m_in[n]), two_d(v_in[n]))
        delta[n], new_m[n], new_v[n] = (a.reshape(w[n].shape) for a in (d, m2, v2))
    shapes = [w[n].shape for n in small_names]
    rows = _rows_for(shapes)
    packed = [_pack([src[n] for n in small_names], rows) for src in (w, grad_out, m_in, v_in)]
    for dst, res in zip((delta, new_m, new_v), adamw("adamw_small", *packed)):
        for n, a in zip(small_names, _unpack(res, shapes)):
            dst[n] = a

    return (loss, dx[None], *[grad_out[n] for n in WEIGHTS], *[delta[n] for n in WEIGHTS],
            *[new_m[n] for n in WEIGHTS], *[new_v[n] for n in WEIGHTS])
```

```python
import functools
import math

import jax
import jax.numpy as jnp
from jax import lax
from jax.experimental import pallas as pl
from jax.experimental.pallas import tpu as pltpu

F32, BF16 = jnp.float32, jnp.bfloat16
D_MODEL = 1024
EPS = 1e-6
NEG_INF = -1e30
FOX_HEADS, FOX_HEAD_DIM, FOX_WIDTH = 8, 64, 512
SC_WIDTH = 512
AB_IN = 3 * FOX_WIDTH + FOX_HEADS + 3 * SC_WIDTH
AB_IN_PAD = 3200
LRU_BW, LRU_BLOCKS = 256, 4
RG_C = 8.0
MEM_HEADS, MEM_HEAD_DIM = 4, 256
ADAM_LR, ADAM_B1, ADAM_B2, ADAM_EPS, ADAM_WD, ADAM_STEP = 0.001, 0.9, 0.999, 1e-08, 0.01, 10
N_CHIPS = 4
MESH = pl.DeviceIdType.MESH
VMEM_LIMIT_BYTES = 48 * 1024 * 1024
MM_OPERAND_TILE_BYTES = 7 * 1024 * 1024

NN = (((1,), (0,)), ((), ()))
NT = (((1,), (1,)), ((), ()))
TN = (((0,), (0,)), ((), ()))


def _dot(a, b, dn=NN):
    return lax.dot_general(a.astype(BF16), b.astype(BF16), dn, preferred_element_type=F32)


def _tile(n, prefs):
    for p in prefs:
        if n % p == 0:
            return p
    return n


def _pcall(name, kern, grid, ins, in_specs, out_shape, out_specs, sem):
    return pl.pallas_call(
        kern, name=name, grid=grid, in_specs=in_specs, out_specs=out_specs, out_shape=out_shape,
        compiler_params=pltpu.CompilerParams(dimension_semantics=sem, vmem_limit_bytes=VMEM_LIMIT_BYTES),
    )(*ins)


def mm(name, a, b, mode, out_dtype, reduce_layout=False):
    if mode == "nn":
        (m, k), n = a.shape, b.shape[1]
    elif mode == "nt":
        (m, k), n = a.shape, b.shape[0]
    else:
        (k, m), n = a.shape, b.shape[1]
    if reduce_layout:
        tm, tn = m // 2, n // N_CHIPS
    else:
        tn = _tile(n, ((1024,) if mode == "tn" else ()) + (512, 640, 256, 128))
        tm = next(c for c in (2048, 1024, 512, 256, 128, m)
                  if m % c == 0 and 2 * c * k <= MM_OPERAND_TILE_BYTES and 4 * c * tn <= MM_OPERAND_TILE_BYTES)
    dn = {"nn": NN, "nt": NT, "tn": TN}[mode]

    def kern(a_ref, b_ref, o_ref):
        o_ref[...] = _dot(a_ref[...], b_ref[...], dn).astype(o_ref.dtype)

    a_spec = pl.BlockSpec((k, tm), lambda i, j: (0, i)) if mode == "tn" else pl.BlockSpec((tm, k), lambda i, j: (i, 0))
    b_spec = pl.BlockSpec((tn, k), lambda i, j: (j, 0)) if mode == "nt" else pl.BlockSpec((k, tn), lambda i, j: (0, j))
    if reduce_layout:
        out_shape = jax.ShapeDtypeStruct((N_CHIPS, 2, tm, tn), out_dtype)
        o_spec = pl.BlockSpec((None, None, tm, tn), lambda i, j: (j, i, 0, 0))
    else:
        out_shape = jax.ShapeDtypeStruct((m, n), out_dtype)
        o_spec = pl.BlockSpec((tm, tn), lambda i, j: (i, j))
    return _pcall(name, kern, (m // tm, n // tn), (a, b), [a_spec, b_spec], out_shape, o_spec, ("parallel", "parallel"))


def rowwise(name, body, rows, params, outs, accs=(), tr=256):
    t = rows[0].shape[0]
    tr = min(tr, t)
    nr, npar, no = len(rows), len(params), len(outs)

    def kern(*refs):
        acc_refs = refs[nr + npar + no:]
        if acc_refs:
            @pl.when(pl.program_id(0) == 0)
            def _():
                for ar in acc_refs:
                    ar[...] = jnp.zeros_like(ar)
        body(refs[:nr], refs[nr:nr + npar], refs[nr + npar:nr + npar + no], acc_refs)

    in_specs = [pl.BlockSpec((tr, x.shape[1]), lambda i: (i, 0)) for x in rows]
    in_specs += [pl.BlockSpec(p.shape, lambda i: (0, 0)) for p in params]
    out_specs = [pl.BlockSpec((tr, c), lambda i: (i, 0)) for c, _ in outs]
    out_specs += [pl.BlockSpec(s, lambda i: (0, 0)) for s in accs]
    out_shape = [jax.ShapeDtypeStruct((t, c), dt) for c, dt in outs]
    out_shape += [jax.ShapeDtypeStruct(s, F32) for s in accs]
    return _pcall(name, kern, (t // tr,), (*rows, *params), in_specs, out_shape, out_specs,
                  ("arbitrary",) if accs else ("parallel",))


def _rms_stats(x):
    r = lax.rsqrt(jnp.mean(x * x, axis=-1, keepdims=True) + EPS)
    return r, x * r


def _rms_bwd(xh, r, g, dy):
    dxh = dy * g
    dx = r * (dxh - xh * jnp.mean(dxh * xh, axis=-1, keepdims=True))
    return dx, jnp.sum(dy * xh, axis=0, keepdims=True)


def rms_pre(name, x, gains, layer):
    def body(r, p, o, a):
        _, xh = _rms_stats(r[0][...])
        o[0][...] = (xh * p[0][layer:layer + 1, :]).astype(BF16)
    return rowwise(name, body, [x], [gains], [(x.shape[1], BF16)])[0]


def post_add(name, x, y, gains, layer):
    def body(r, p, o, a):
        _, yh = _rms_stats(r[1][...])
        o[0][...] = r[0][...] + yh * p[0][layer:layer + 1, :]
    return rowwise(name, body, [x, y], [gains], [(x.shape[1], F32)])[0]


def post_bwd(name, y, dx, gains, layer):
    def body(r, p, o, a):
        rr, yh = _rms_stats(r[0][...])
        dy, dg = _rms_bwd(yh, rr, p[0][layer:layer + 1, :], r[1][...])
        o[0][...] = dy.astype(BF16)
        a[0][...] += dg
    c = y.shape[1]
    return rowwise(name, body, [y, dx], [gains], [(c, BF16)], [(1, c)])


def pre_bwd(name, x, dh, dx_res, gains, layer):
    def body(r, p, o, a):
        rr, xh = _rms_stats(r[0][...])
        dx, dg = _rms_bwd(xh, rr, p[0][layer:layer + 1, :], r[1][...])
        o[0][...] = r[2][...] + dx
        a[0][...] += dg
    c = x.shape[1]
    return rowwise(name, body, [x, dh, dx_res], [gains], [(c, F32)], [(1, c)])


def gain_bwd(name, x, dh):
    def body(r, p, o, a):
        _, xh = _rms_stats(r[0][...])
        a[0][...] += jnp.sum(r[1][...] * xh, axis=0, keepdims=True)
    return rowwise(name, body, [x, dh], [], [], [(1, x.shape[1])])[0]


def _sigmoid(z):
    return 1.0 / (1.0 + jnp.exp(-z))


def swiglu_fwd(name, gu):
    f = gu.shape[1] // 2

    def body(r, p, o, a):
        g = r[0][:, :f].astype(F32)
        u = r[0][:, f:].astype(F32)
        o[0][...] = (g * _sigmoid(g) * u).astype(BF16)
    return rowwise(name, body, [gu], [], [(f, BF16)])[0]


def swiglu_bwd(name, gu, da):
    f = gu.shape[1] // 2

    def body(r, p, o, a):
        g = r[0][:, :f].astype(F32)
        u = r[0][:, f:].astype(F32)
        d = r[1][...].astype(F32)
        sg = _sigmoid(g)
        o[0][:, :f] = (d * u * sg * (1.0 + g * (1.0 - sg))).astype(BF16)
        o[0][:, f:] = (d * g * sg).astype(BF16)
    return rowwise(name, body, [gu, da], [], [(2 * f, BF16)])[0]


def loss_head(name, y, target):
    c = y.shape[1]

    def body(r, p, o, a):
        e = r[0][...] - r[1][...]
        o[0][...] = e * (1.0 / c)
        a[0][...] += jnp.sum(e * e, axis=0, keepdims=True)
    return rowwise(name, body, [y, target], [], [(c, F32)], [(1, c)])


def adamw(name, w, g, m, v):
    c = w.shape[1]

    def body(r, p, o, a):
        wv, gv, mv, vv = (x[...] for x in r)
        m2 = ADAM_B1 * mv + (1.0 - ADAM_B1) * gv
        v2 = ADAM_B2 * vv + (1.0 - ADAM_B2) * (gv * gv)
        m_hat = m2 / (1.0 - ADAM_B1 ** ADAM_STEP)
        v_hat = v2 / (1.0 - ADAM_B2 ** ADAM_STEP)
        o[0][...] = -ADAM_LR * (m_hat / (jnp.sqrt(v_hat) + ADAM_EPS) + ADAM_WD * wv)
        o[1][...] = m2
        o[2][...] = v2
    tr = _tile(w.shape[0], (256, 128, 64, 32, 16, 8))
    return rowwise(name, body, [w, g, m, v], [], [(c, F32)] * 3, tr=tr)


def colwise(name, body, cols, params, outs, pouts=(), tc=128):
    t = cols[0][0].shape[0]
    c = params[0].shape[1] if params else cols[0][0].shape[1]
    nc, npar, no = len(cols), len(params), len(outs)

    def kern(*refs):
        body(refs[:nc], refs[nc:nc + npar], refs[nc + npar:nc + npar + no], refs[nc + npar + no:])

    in_specs = [pl.BlockSpec((t, tc), functools.partial(lambda j, off: (0, j + off), off=off)) for _, off in cols]
    in_specs += [pl.BlockSpec((p.shape[0], tc), lambda j: (0, j)) for p in params]
    out_specs = [pl.BlockSpec((t, tc), lambda j: (0, j)) for _ in outs]
    out_specs += [pl.BlockSpec((r, tc), lambda j: (0, j)) for r in pouts]
    out_shape = [jax.ShapeDtypeStruct((t, c), dt) for dt in outs]
    out_shape += [jax.ShapeDtypeStruct((r, c), F32) for r in pouts]
    return _pcall(name, kern, (c // tc,), (*[x for x, _ in cols], *params), in_specs, out_shape, out_specs,
                  ("parallel",))


def _row_index(shape):
    return lax.broadcasted_iota(jnp.int32, shape, 0)


def _shift_down(x, d, rows):
    return jnp.where(rows >= d, pltpu.roll(x, d, 0), 0.0)


def _shift_up(x, d, rows):
    t = x.shape[0]
    return jnp.where(rows < t - d, pltpu.roll(x, t - d, 0), 0.0)


def sconv_fwd(name, proj, col0, conv_w, tc=128):
    nb = SC_WIDTH // tc

    def body(cl, p, o, po):
        b, c, u = (x[...] for x in cl)
        rows = _row_index(b.shape)
        w = p[0][...]
        z = c * u
        conv = w[2:3] * z + w[1:2] * _shift_down(z, 1, rows) + w[0:1] * _shift_down(z, 2, rows)
        o[0][...] = (b * conv).astype(BF16)
    return colwise(name, body, [(proj, col0), (proj, col0 + nb), (proj, col0 + 2 * nb)], [conv_w], [BF16], tc=tc)[0]


def sconv_bwd(name, proj, col0, conv_w, dyb, dcol0, tc=128):
    nb = SC_WIDTH // tc

    def body(cl, p, o, po):
        b, c, u, dy = (x[...] for x in cl)
        rows = _row_index(b.shape)
        w = p[0][...]
        z = c * u
        z1, z2 = _shift_down(z, 1, rows), _shift_down(z, 2, rows)
        conv = w[2:3] * z + w[1:2] * z1 + w[0:1] * z2
        dconv = dy * b
        dz = w[2:3] * dconv + w[1:2] * _shift_up(dconv, 1, rows) + w[0:1] * _shift_up(dconv, 2, rows)
        o[0][...] = (dy * conv).astype(BF16)
        o[1][...] = (dz * u).astype(BF16)
        o[2][...] = (dz * c).astype(BF16)
        po[0][0:1, :] = jnp.sum(dconv * z2, axis=0, keepdims=True)
        po[0][1:2, :] = jnp.sum(dconv * z1, axis=0, keepdims=True)
        po[0][2:3, :] = jnp.sum(dconv * z, axis=0, keepdims=True)
    return colwise(name, body, [(proj, col0), (proj, col0 + nb), (proj, col0 + 2 * nb), (dyb, dcol0)], [conv_w],
                   [BF16, BF16, BF16], [3], tc=tc)


def _expm1(x):
    series = x * (1.0 + 0.5 * x * (1.0 + x * (1.0 / 3.0) * (1.0 + 0.25 * x * (1.0 + 0.2 * x))))
    return jnp.where(jnp.abs(x) < 0.05, series, jnp.exp(x) - 1.0)


def _log1p(x):
    series = x * (1.0 - x * (0.5 - x * (1.0 / 3.0 - 0.25 * x)))
    return jnp.where(jnp.abs(x) < 0.01, series, jnp.log(1.0 + x))


def _softplus_neg(lam):
    sp = jnp.maximum(-lam, 0.0) + _log1p(jnp.exp(-jnp.abs(lam)))
    return sp, -_sigmoid(-lam)


GELU_C = math.sqrt(2.0 / math.pi)


def _gelu(x):
    th = jnp.tanh(GELU_C * (x + 0.044715 * x * x * x))
    val = 0.5 * x * (1.0 + th)
    grad = 0.5 * (1.0 + th) + 0.5 * x * (1.0 - th * th) * GELU_C * (1.0 + 3.0 * 0.044715 * x * x)
    return val, grad


def rg_conv_fwd(name, gu2, conv_w, conv_b, tc=128):
    nb = D_MODEL // tc

    def body(cl, p, o, po):
        u = cl[0][...]
        rows = _row_index(u.shape)
        w = p[0][...]
        o[0][...] = (w[3:4] * u + w[2:3] * _shift_down(u, 1, rows) + w[1:2] * _shift_down(u, 2, rows)
                     + w[0:1] * _shift_down(u, 3, rows) + p[1][...])
    return colwise(name, body, [(gu2, nb)], [conv_w, conv_b], [F32], tc=tc)[0]


def rg_conv_bwd(name, gu2, duc, conv_w, tc=128):
    nb = D_MODEL // tc

    def body(cl, p, o, po):
        u, d = cl[0][...], cl[1][...]
        rows = _row_index(u.shape)
        w = p[0][...]
        o[0][...] = (w[3:4] * d + w[2:3] * _shift_up(d, 1, rows) + w[1:2] * _shift_up(d, 2, rows)
                     + w[0:1] * _shift_up(d, 3, rows)).astype(BF16)
        for k in range(4):
            uk = u if k == 3 else _shift_down(u, 3 - k, rows)
            po[0][k:k + 1, :] = jnp.sum(d * uk, axis=0, keepdims=True)
        po[1][...] = jnp.sum(d, axis=0, keepdims=True)
    return colwise(name, body, [(gu2, nb), (duc, 0)], [conv_w], [BF16], [4, 1], tc=tc)


def rg_gates_fwd(name, uc, w_a, b_a, w_i, b_i, tr=512):
    t = uc.shape[0]
    tr = min(tr, t)

    def kern(u_ref, wa_ref, ba_ref, wi_ref, bi_ref, r_ref, i_ref):
        ub = u_ref[...].astype(BF16)
        r_ref[...] = _sigmoid(_dot(ub, wa_ref[...]) + ba_ref[...])
        i_ref[...] = _sigmoid(_dot(ub, wi_ref[...]) + bi_ref[...])

    blk = pl.BlockSpec((tr, LRU_BW), lambda n, i: (i, n))
    wspec = pl.BlockSpec((None, LRU_BW, LRU_BW), lambda n, i: (n, 0, 0))
    bspec = pl.BlockSpec((1, LRU_BW), lambda n, i: (0, n))
    return _pcall(name, kern, (LRU_BLOCKS, t // tr), (uc, w_a, b_a, w_i, b_i), [blk, wspec, bspec, wspec, bspec],
                  [jax.ShapeDtypeStruct(uc.shape, F32)] * 2, [blk, blk], ("parallel", "parallel"))


def rg_gates_bwd(name, uc, dzr, dzi, duc_part, w_a, w_i):
    t = uc.shape[0]
    rows = LRU_BW // N_CHIPS

    def kern(u_ref, dr_ref, di_ref, dp_ref, wa_ref, wi_ref, duc_ref, dwa_ref, dwi_ref):
        ub = u_ref[...].astype(BF16)
        dr, di = dr_ref[...], di_ref[...]
        dwa, dwi = _dot(ub, dr, TN), _dot(ub, di, TN)
        for p in range(N_CHIPS):
            dwa_ref[p] = dwa[p * rows:(p + 1) * rows].astype(dwa_ref.dtype)
            dwi_ref[p] = dwi[p * rows:(p + 1) * rows].astype(dwi_ref.dtype)
        duc_ref[...] = dp_ref[...] + _dot(dr, wa_ref[...], NT) + _dot(di, wi_ref[...], NT)

    blk = pl.BlockSpec((t, LRU_BW), lambda n: (0, n))
    wspec = pl.BlockSpec((None, LRU_BW, LRU_BW), lambda n: (n, 0, 0))
    gspec = pl.BlockSpec((N_CHIPS, None, rows, LRU_BW), lambda n: (0, n, 0, 0))
    gshape = jax.ShapeDtypeStruct((N_CHIPS, LRU_BLOCKS, rows, LRU_BW), BF16)
    return _pcall(name, kern, (LRU_BLOCKS,), (uc, dzr, dzi, duc_part, w_a, w_i), [blk, blk, blk, blk, wspec, wspec],
                  [jax.ShapeDtypeStruct(uc.shape, F32), gshape, gshape], [blk, gspec, gspec], ("parallel",))


def _rg_decay(r, lam):
    sp, dsp = _softplus_neg(lam)
    la = -RG_C * r * sp
    a = jnp.exp(la)
    sq = jnp.sqrt(-_expm1(2.0 * la))
    return sp, dsp, a, sq


def rg_scan_fwd(name, gu2, uc, r, i, lam, tc=128):
    def body(cl, p, o, po):
        gate, ucv, rv, iv = (x[...] for x in cl)
        t = gate.shape[0]
        rows = _row_index(gate.shape)
        _, _, a, sq = _rg_decay(rv, p[0][...])
        b = sq * (iv * ucv)
        d = 1
        while d < t:
            keep = rows >= d
            b = a * jnp.where(keep, pltpu.roll(b, d, 0), 0.0) + b
            a = a * jnp.where(keep, pltpu.roll(a, d, 0), 1.0)
            d *= 2
        o[0][...] = (_gelu(gate)[0] * b).astype(BF16)
        o[1][...] = b
    return colwise(name, body, [(gu2, 0), (uc, 0), (r, 0), (i, 0)], [lam], [BF16, F32], tc=tc)


def rg_scan_bwd(name, gu2, uc, r, i, hs, dy, lam, tc=128):
    def body(cl, p, o, po):
        gate, ucv, rv, iv, h, dyv = (x[...] for x in cl)
        t = gate.shape[0]
        rows = _row_index(gate.shape)
        sp, dsp, a, sq = _rg_decay(rv, p[0][...])
        gl, dgl = _gelu(gate)
        o[0][...] = (dyv * h * dgl).astype(BF16)
        g = dyv * gl
        am = _shift_up(a, 1, rows)
        d = 1
        while d < t:
            keep = rows < t - d
            g = am * jnp.where(keep, pltpu.roll(g, t - d, 0), 0.0) + g
            am = am * jnp.where(keep, pltpu.roll(am, t - d, 0), 0.0)
            d *= 2
        da = g * _shift_down(h, 1, rows)
        iu = iv * ucv
        d_iu = g * sq
        dla = da * a - (g * iu) * (a * a) / sq
        dzr = dla * (-RG_C * sp) * rv * (1.0 - rv)
        dzi = d_iu * ucv * iv * (1.0 - iv)
        o[1][...] = dzr.astype(BF16)
        o[2][...] = dzi.astype(BF16)
        o[3][...] = d_iu * iv
        po[0][...] = jnp.sum(dzr, axis=0, keepdims=True)
        po[1][...] = jnp.sum(dzi, axis=0, keepdims=True)
        po[2][...] = jnp.sum(dla * rv, axis=0, keepdims=True) * (-RG_C) * dsp
    return colwise(name, body, [(gu2, 0), (uc, 0), (r, 0), (i, 0), (hs, 0), (dy, 0)], [lam],
                   [BF16, BF16, BF16, F32], [1, 1, 1], tc=tc)


def _split3(x):
    hi = x.astype(BF16)
    r1 = x - hi.astype(F32)
    mid = r1.astype(BF16)
    lo = (r1 - mid.astype(F32)).astype(BF16)
    return hi, mid, lo


def _tri_dot(x, tri):
    out = None
    for piece in _split3(x):
        term = lax.dot_general(piece, tri, NN, preferred_element_type=F32)
        out = term if out is None else out + term
    return out


def fox_gates_fwd(name, z_t, b_f):
    h, t = z_t.shape
    tb = min(512, t)

    def kern(z_ref, b_ref, o_ref):
        z = z_ref[...] + b_ref[...]
        logf = jnp.minimum(z, 0.0) - _log1p(jnp.exp(-jnp.abs(z)))
        src = lax.broadcasted_iota(jnp.int32, (t, tb), 0)
        dst = lax.broadcasted_iota(jnp.int32, (t, tb), 1) + pl.program_id(0) * tb
        o_ref[...] = _tri_dot(logf, (src <= dst).astype(BF16))

    return _pcall(name, kern, (t // tb,), (z_t, b_f),
                  [pl.BlockSpec((h, t), lambda j: (0, 0)), pl.BlockSpec((h, 1), lambda j: (0, 0))],
                  jax.ShapeDtypeStruct((h, t), F32), pl.BlockSpec((h, tb), lambda j: (0, j)), ("parallel",))


def fox_gates_bwd(name, z_t, b_f, dcum_t):
    h, t = z_t.shape
    tb = min(512, t)

    def kern(z_ref, b_ref, d_ref, dz_ref, db_ref):
        @pl.when(pl.program_id(0) == 0)
        def _():
            db_ref[...] = jnp.zeros_like(db_ref)
        src = lax.broadcasted_iota(jnp.int32, (t, tb), 0)
        dst = lax.broadcasted_iota(jnp.int32, (t, tb), 1) + pl.program_id(0) * tb
        dlogf = _tri_dot(d_ref[...], (src >= dst).astype(BF16))
        z = z_ref[...] + b_ref[...]
        dz = dlogf * _sigmoid(-z)
        dz_ref[...] = dz
        db_ref[...] += jnp.sum(dz, axis=1, keepdims=True)

    return _pcall(name, kern, (t // tb,), (z_t, b_f, dcum_t),
                  [pl.BlockSpec((h, tb), lambda j: (0, j)), pl.BlockSpec((h, 1), lambda j: (0, 0)),
                   pl.BlockSpec((h, t), lambda j: (0, 0))],
                  [jax.ShapeDtypeStruct((h, t), F32), jax.ShapeDtypeStruct((h, 1), F32)],
                  [pl.BlockSpec((h, tb), lambda j: (0, j)), pl.BlockSpec((h, 1), lambda j: (0, 0))], ("arbitrary",))


def _fox_spans(qs, k_ref, cr_ref, i, tq):
    n0 = i * tq
    sd = _dot(qs, k_ref[n0:n0 + tq, :], NT) - cr_ref[:, n0:n0 + tq]
    row = lax.broadcasted_iota(jnp.int32, (tq, tq), 0)
    col = lax.broadcasted_iota(jnp.int32, (tq, tq), 1)
    spans = [(n0, tq, jnp.where(row >= col, sd, NEG_INF))]
    if i > 0:
        spans.append((0, n0, _dot(qs, k_ref[0:n0, :], NT) - cr_ref[:, 0:n0]))
    return spans


def fox_fwd(name, q, k, v, cum_r, tq=256):
    h, t, dh = q.shape
    tq = min(tq, t)
    scale = FOX_HEAD_DIM ** -0.5

    def kern(q_ref, k_ref, v_ref, cr_ref, o_ref, lse_ref):
        for i in range(t // tq):
            rows = slice(i * tq, (i + 1) * tq)
            spans = _fox_spans(q_ref[rows, :] * scale, k_ref, cr_ref, i, tq)
            m = functools.reduce(jnp.maximum, [jnp.max(s, axis=-1, keepdims=True) for _, _, s in spans])
            l, acc = 0.0, 0.0
            for k0, kn, s in spans:
                p = jnp.exp(s - m)
                l = l + jnp.sum(p, axis=-1, keepdims=True)
                acc = acc + _dot(p, v_ref[k0:k0 + kn, :])
            o_ref[rows, :] = (acc / l).astype(o_ref.dtype)
            lse_ref[rows, :] = m + jnp.log(l)

    hspec = pl.BlockSpec((None, t, dh), lambda a: (a, 0, 0))
    cspec = pl.BlockSpec((None, t, 1), lambda a: (a, 0, 0))
    rspec = pl.BlockSpec((None, 1, t), lambda a: (a, 0, 0))
    return _pcall(name, kern, (h,), (q, k, v, cum_r), [hspec, hspec, hspec, rspec],
                  [jax.ShapeDtypeStruct((h, t, dh), BF16), jax.ShapeDtypeStruct((h, t, 1), F32)],
                  [hspec, cspec], ("parallel",))


def fox_bwd(name, q, k, v, do, lse, cum_r, tq=256):
    h, t, dh = q.shape
    tq = min(tq, t)
    scale = FOX_HEAD_DIM ** -0.5

    def kern(q_ref, k_ref, v_ref, do_ref, lse_ref, cr_ref, dq_ref, dk_ref, dv_ref, dc_ref):
        dk_ref[...] = jnp.zeros_like(dk_ref)
        dv_ref[...] = jnp.zeros_like(dv_ref)
        dc_ref[...] = jnp.zeros_like(dc_ref)
        for i in range(t // tq):
            rows = slice(i * tq, (i + 1) * tq)
            qs, dov, lse_v = q_ref[rows, :] * scale, do_ref[rows, :], lse_ref[rows, :]
            spans = _fox_spans(qs, k_ref, cr_ref, i, tq)
            probs = [jnp.exp(s - lse_v) for _, _, s in spans]
            dps = [_dot(dov, v_ref[k0:k0 + kn, :], NT) for k0, kn, _ in spans]
            rowdot = sum(jnp.sum(dp * p, axis=-1, keepdims=True) for dp, p in zip(dps, probs))
            dq = 0.0
            for (k0, kn, _), p, dp in zip(spans, probs, dps):
                ds = p * (dp - rowdot)
                dq = dq + _dot(ds, k_ref[k0:k0 + kn, :])
                dk_ref[k0:k0 + kn, :] += _dot(ds, qs, TN)
                dv_ref[k0:k0 + kn, :] += _dot(p, dov, TN)
                dc_ref[:, k0:k0 + kn] -= jnp.sum(ds, axis=0, keepdims=True)
            dq_ref[rows, :] = (dq * scale).astype(dq_ref.dtype)

    hspec = pl.BlockSpec((None, t, dh), lambda a: (a, 0, 0))
    cspec = pl.BlockSpec((None, t, 1), lambda a: (a, 0, 0))
    rspec = pl.BlockSpec((None, 1, t), lambda a: (a, 0, 0))
    return _pcall(name, kern, (h,), (q, k, v, do, lse, cum_r), [hspec, hspec, hspec, hspec, cspec, rspec],
                  [jax.ShapeDtypeStruct((h, t, dh), BF16), jax.ShapeDtypeStruct((h, t, dh), F32),
                   jax.ShapeDtypeStruct((h, t, dh), F32), jax.ShapeDtypeStruct((h, 1, t), F32)],
                  [hspec, hspec, hspec, rspec], ("parallel",))


def _xattn_probs(q, k):
    s = _dot(q, k, NT) * (MEM_HEAD_DIM ** -0.5)
    p = jnp.exp(s - jnp.max(s, axis=-1, keepdims=True))
    return p / jnp.sum(p, axis=-1, keepdims=True)


def xattn_fwd(name, q, kv, tq=512):
    t = q.shape[0]
    tq = min(tq, t)
    ml = kv.shape[0]

    def kern(q_ref, k_ref, v_ref, o_ref):
        o_ref[...] = _dot(_xattn_probs(q_ref[...], k_ref[...]), v_ref[...]).astype(o_ref.dtype)

    qspec = pl.BlockSpec((tq, MEM_HEAD_DIM), lambda i, a: (i, a))
    return _pcall(name, kern, (t // tq, MEM_HEADS), (q, kv, kv),
                  [qspec, pl.BlockSpec((ml, MEM_HEAD_DIM), lambda i, a: (0, a)),
                   pl.BlockSpec((ml, MEM_HEAD_DIM), lambda i, a: (0, MEM_HEADS + a))],
                  jax.ShapeDtypeStruct(q.shape, BF16), qspec, ("parallel", "parallel"))


def xattn_bwd(name, q, kv, do, tq=512):
    t = q.shape[0]
    tq = min(tq, t)
    ml = kv.shape[0]
    scale = MEM_HEAD_DIM ** -0.5

    def kern(q_ref, k_ref, v_ref, do_ref, dq_ref, dk_ref, dv_ref):
        @pl.when(pl.program_id(1) == 0)
        def _():
            dk_ref[...] = jnp.zeros_like(dk_ref)
            dv_ref[...] = jnp.zeros_like(dv_ref)
        qv, kv_, dov = q_ref[...], k_ref[...], do_ref[...]
        p = _xattn_probs(qv, kv_)
        dp = _dot(dov, v_ref[...], NT)
        ds = p * (dp - jnp.sum(dp * p, axis=-1, keepdims=True)) * scale
        dq_ref[...] = _dot(ds, kv_).astype(dq_ref.dtype)
        dk_ref[...] += _dot(ds, qv, TN)
        dv_ref[...] += _dot(p, dov, TN)

    qspec = pl.BlockSpec((tq, MEM_HEAD_DIM), lambda a, i: (i, a))
    kspec = pl.BlockSpec((ml, MEM_HEAD_DIM), lambda a, i: (0, a))
    return _pcall(name, kern, (MEM_HEADS, t // tq), (q, kv, kv, do),
                  [qspec, kspec, pl.BlockSpec((ml, MEM_HEAD_DIM), lambda a, i: (0, MEM_HEADS + a)), qspec],
                  [jax.ShapeDtypeStruct(q.shape, BF16), jax.ShapeDtypeStruct((ml, D_MODEL), F32),
                   jax.ShapeDtypeStruct((ml, D_MODEL), F32)],
                  [qspec, kspec, kspec], ("parallel", "arbitrary"))


def _heads(x):
    t = x.shape[0]
    return x.reshape(t, FOX_HEADS, FOX_HEAD_DIM).transpose(1, 0, 2)


def _unheads(x):
    return x.transpose(1, 0, 2).reshape(x.shape[1], FOX_WIDTH)


def _row_cut(dw):
    return dw.reshape(N_CHIPS, 2, dw.shape[0] // (2 * N_CHIPS), dw.shape[1])


def local_step(x, mem, target, w, layer_weights=None, reduce_hook=None):
    depth = w["g_mix_pre"].shape[0]
    t = x.shape[0]
    saved = []
    i1, i2, i3 = 3 * FOX_WIDTH, 3 * FOX_WIDTH + FOX_HEADS, AB_IN
    ncol = 128

    def stacked_weights(layer, part, _):
        names = COMMON_BIG if part == "rest" else layer_big(layer)[len(COMMON_BIG):]
        return {n: w[n][layer if n in COMMON_BIG else layer // 2] for n in names}

    get_weights = layer_weights or stacked_weights
    for layer in range(depth):
        lw = dict(get_weights(layer, "mix", x))
        s = {"x0": x, "lw": lw}
        tag = f"l{layer}"
        h1 = rms_pre(f"{tag}_mix_pre", x, w["g_mix_pre"], layer)
        s["h1"] = h1
        if layer % 2 == 0:
            e = layer // 2
            w_in = jnp.pad(lw["ab_w_in"], ((0, 0), (0, AB_IN_PAD - AB_IN)))
            proj = mm(f"{tag}_ab_in", h1, w_in, "nn", F32)
            qkv = proj[:, :i1].astype(BF16).reshape(t, 3, FOX_HEADS, FOX_HEAD_DIM).transpose(1, 2, 0, 3)
            z_t = proj[:, i1:i2].T
            b_f = w["ab_b_f"][e].reshape(FOX_HEADS, 1)
            cum_t = fox_gates_fwd(f"{tag}_fox_gates", z_t, b_f)
            cum_r = cum_t[:, None, :]
            oh, lse = fox_fwd(f"{tag}_fox", qkv[0], qkv[1], qkv[2], cum_r)
            bcu = proj[:, i2:i3]
            y_b = sconv_fwd(f"{tag}_sconv", bcu, 0, w["ab_conv_w"][e])
            ycat = jnp.concatenate([_unheads(oh), y_b], axis=1)
            y1 = mm(f"{tag}_ab_out", ycat, lw["ab_w_out"], "nn", F32)
            s.update(w_in=w_in, qkv=qkv, z_t=z_t, b_f=b_f, cum_r=cum_r, lse=lse, bcu=bcu, ycat=ycat)
        else:
            o = layer // 2
            gu2 = mm(f"{tag}_c_in", h1, lw["c_w_in"], "nn", F32)
            conv_b = w["c_conv_b"][o].reshape(1, -1)
            uc = rg_conv_fwd(f"{tag}_rg_conv", gu2, w["c_conv_w"][o], conv_b)
            b_a, b_i = w["c_b_a"][o].reshape(1, -1), w["c_b_i"][o].reshape(1, -1)
            r, i = rg_gates_fwd(f"{tag}_rg_gates", uc, lw["c_w_a"], b_a, lw["c_w_i"], b_i)
            lam = w["c_lam"][o].reshape(1, -1)
            ymix, hs = rg_scan_fwd(f"{tag}_rg_scan", gu2, uc, r, i, lam)
            y1 = mm(f"{tag}_c_out", ymix, lw["c_w_out"], "nn", F32)
            s.update(gu2=gu2, uc=uc, r=r, i=i, lam=lam, hs=hs, ymix=ymix)
        s["y1"] = y1
        x = post_add(f"{tag}_mix_post", x, y1, w["g_mix_post"], layer)
        lw.update(get_weights(layer, "rest", x))
        s["x1"] = x
        h2 = rms_pre(f"{tag}_cross_pre", x, w["g_cross_pre"], layer)
        m = rms_pre(f"{tag}_mem_pre", mem, w["g_mem"], layer)
        q = mm(f"{tag}_xq", h2, lw["w_xq"], "nn", BF16)
        kv = mm(f"{tag}_xkv", m, lw["w_xkv"], "nn", BF16)
        o_att = xattn_fwd(f"{tag}_xattn", q, kv)
        y2 = mm(f"{tag}_xo", o_att, lw["w_xo"], "nn", F32)
        s.update(h2=h2, m=m, q=q, kv=kv, o_att=o_att, y2=y2)
        x = post_add(f"{tag}_cross_post", x, y2, w["g_cross_post"], layer)
        s["x2"] = x
        h3 = rms_pre(f"{tag}_ffn_pre", x, w["g_ffn_pre"], layer)
        gu = mm(f"{tag}_ffn_gu", h3, lw["w_ffn_gu"], "nn", BF16)
        act = swiglu_fwd(f"{tag}_swiglu", gu)
        y3 = mm(f"{tag}_ffn_down", act, lw["w_ffn_down"], "nn", F32)
        s.update(h3=h3, gu=gu, act=act, y3=y3)
        x = post_add(f"{tag}_ffn_post", x, y3, w["g_ffn_post"], layer)
        saved.append(s)

    dx, sq_cols = loss_head("loss_head", x, target)

    grads = {k: [None] * v.shape[0] for k, v in w.items() if k not in BIG}
    big, token = {}, None

    def dw(name, a, b, cols_cut=False):
        return mm(name, a, b, "tn", BF16, reduce_layout=True) if cols_cut else _row_cut(mm(name, a, b, "tn", BF16))

    for layer in reversed(range(depth)):
        s = saved[layer]
        lw = s["lw"]
        tag = f"b{layer}"
        lg = {}
        g_ffn_post = w["g_ffn_post"] if token is None else w["g_ffn_post"] + token
        dy3, grads["g_ffn_post"][layer] = post_bwd(f"{tag}_ffn_post", s["y3"], dx, g_ffn_post, layer)
        dact = mm(f"{tag}_ffn_down_dx", dy3, lw["w_ffn_down"], "nt", BF16)
        lg["w_ffn_down"] = dw(f"{tag}_ffn_down_dw", s["act"], dy3)
        dgu = swiglu_bwd(f"{tag}_swiglu", s["gu"], dact)
        dh3 = mm(f"{tag}_ffn_gu_dx", dgu, lw["w_ffn_gu"], "nt", F32)
        lg["w_ffn_gu"] = dw(f"{tag}_ffn_gu_dw", s["h3"], dgu, cols_cut=True)
        dx, grads["g_ffn_pre"][layer] = pre_bwd(f"{tag}_ffn_pre", s["x2"], dh3, dx, w["g_ffn_pre"], layer)
        dy2, grads["g_cross_post"][layer] = post_bwd(f"{tag}_cross_post", s["y2"], dx, w["g_cross_post"], layer)
        do = mm(f"{tag}_xo_dx", dy2, lw["w_xo"], "nt", BF16)
        lg["w_xo"] = dw(f"{tag}_xo_dw", s["o_att"], dy2)
        dq, dk, dv = xattn_bwd(f"{tag}_xattn", s["q"], s["kv"], do)
        dh2 = mm(f"{tag}_xq_dx", dq, lw["w_xq"], "nt", F32)
        lg["w_xq"] = dw(f"{tag}_xq_dw", s["h2"], dq)
        dkv = jnp.concatenate([dk, dv], axis=1).astype(BF16)
        dm = mm(f"{tag}_xkv_dx", dkv, lw["w_xkv"], "nt", F32)
        lg["w_xkv"] = dw(f"{tag}_xkv_dw", s["m"], dkv, cols_cut=True)
        grads["g_mem"][layer] = gain_bwd(f"{tag}_mem_pre", mem, dm)
        dx, grads["g_cross_pre"][layer] = pre_bwd(f"{tag}_cross_pre", s["x1"], dh2, dx, w["g_cross_pre"], layer)
        token = None if reduce_hook is None else reduce_hook(layer, "rest", lg)
        g_mix_post = w["g_mix_post"] if token is None else w["g_mix_post"] + token
        dy1, grads["g_mix_post"][layer] = post_bwd(f"{tag}_mix_post", s["y1"], dx, g_mix_post, layer)
        rest_grads, lg = lg, {}
        if layer % 2 == 0:
            e = layer // 2
            dycat = mm(f"{tag}_ab_out_dx", dy1, lw["ab_w_out"], "nt", F32)
            lg["ab_w_out"] = dw(f"{tag}_ab_out_dw", s["ycat"], dy1)
            do_h = _heads(dycat[:, :FOX_WIDTH].astype(BF16))
            qkv = s["qkv"]
            dqh, dkh, dvh, dcum = fox_bwd(f"{tag}_fox", qkv[0], qkv[1], qkv[2], do_h, s["lse"], s["cum_r"])
            dz_t, db_f = fox_gates_bwd(f"{tag}_fox_gates", s["z_t"], s["b_f"], dcum.reshape(FOX_HEADS, t))
            grads["ab_b_f"][e] = db_f.reshape(FOX_HEADS)
            db, dc, du, dconv_w = sconv_bwd(f"{tag}_sconv", s["bcu"], 0, w["ab_conv_w"][e], dycat, FOX_WIDTH // ncol)
            grads["ab_conv_w"][e] = dconv_w
            dproj = jnp.concatenate(
                [_unheads(dqh), _unheads(dkh).astype(BF16), _unheads(dvh).astype(BF16), dz_t.T.astype(BF16), db, dc, du,
                 jnp.zeros((t, AB_IN_PAD - AB_IN), BF16)], axis=1)
            dh1 = mm(f"{tag}_ab_in_dx", dproj, s["w_in"], "nt", F32)
            dw_in = mm(f"{tag}_ab_in_dw", s["h1"], dproj, "tn", F32)[:, :AB_IN]
            lg["ab_w_in"] = dw_in.reshape(2, D_MODEL // 2, N_CHIPS, AB_IN // N_CHIPS).transpose(2, 0, 1, 3).astype(BF16)
        else:
            o = layer // 2
            dymix = mm(f"{tag}_c_out_dx", dy1, lw["c_w_out"], "nt", F32)
            lg["c_w_out"] = dw(f"{tag}_c_out_dw", s["ymix"], dy1)
            dgate, dzr, dzi, duc_part, db_a, db_i, dlam = rg_scan_bwd(
                f"{tag}_rg_scan", s["gu2"], s["uc"], s["r"], s["i"], s["hs"], dymix, s["lam"])
            duc, dw_a, dw_i = rg_gates_bwd(f"{tag}_rg_gates", s["uc"], dzr, dzi, duc_part, lw["c_w_a"], lw["c_w_i"])
            lg["c_w_a"] = dw_a.reshape(N_CHIPS, 2, LRU_BW // 2, LRU_BW)
            lg["c_w_i"] = dw_i.reshape(N_CHIPS, 2, LRU_BW // 2, LRU_BW)
            du_raw, dconv_w, dconv_b = rg_conv_bwd(f"{tag}_rg_conv", s["gu2"], duc, w["c_conv_w"][o])
            grads["c_b_a"][o] = db_a.reshape(LRU_BLOCKS, LRU_BW)
            grads["c_b_i"][o] = db_i.reshape(LRU_BLOCKS, LRU_BW)
            grads["c_lam"][o] = dlam.reshape(-1)
            grads["c_conv_w"][o] = dconv_w
            grads["c_conv_b"][o] = dconv_b.reshape(-1)
            dgu2 = jnp.concatenate([dgate, du_raw], axis=1)
            dh1 = mm(f"{tag}_c_in_dx", dgu2, lw["c_w_in"], "nt", F32)
            lg["c_w_in"] = dw(f"{tag}_c_in_dw", s["h1"], dgu2, cols_cut=True)
        dx, grads["g_mix_pre"][layer] = pre_bwd(f"{tag}_mix_pre", s["x0"], dh1, dx, w["g_mix_pre"], layer)
        if reduce_hook is None:
            big[layer] = {**rest_grads, **lg}
        else:
            token = reduce_hook(layer, "mix", lg)

    for k in list(grads):
        if k.startswith("g_"):
            grads[k] = [g.reshape(-1) for g in grads[k]]
        grads[k] = jnp.stack(grads[k])
    return sq_cols, dx, grads, big


CHIP_FLIPS = ((1, 0), (0, 1), (1, 1))
HBM_SPEC = pl.BlockSpec(memory_space=pltpu.HBM)
VMEM_SPEC = pl.BlockSpec(memory_space=pltpu.VMEM)


def _place():
    return lax.axis_index("x"), lax.axis_index("y"), lax.axis_index("c")


def _flip(v, f):
    return 1 - v if f else v


def _remote(src, dst, send_sem, recv_sem, target):
    return pltpu.make_async_remote_copy(src_ref=src, dst_ref=dst, send_sem=send_sem, recv_sem=recv_sem,
                                        device_id=target, device_id_type=MESH)


def _comm_call(name, body, ins, out_shape, n_sems):
    n = len(ins)
    return pl.pallas_call(
        body, name=name, in_specs=[HBM_SPEC] * n, out_specs=[HBM_SPEC] * len(out_shape), out_shape=out_shape,
        scratch_shapes=[pltpu.SemaphoreType.DMA((n, n_sems)), pltpu.SemaphoreType.DMA((n, n_sems))],
    )(*ins)


def swap_with_sibling(name, blocks):
    n = len(blocks)

    def body(*refs):
        ins, outs = refs[:n], refs[n:2 * n]
        send_sems, recv_sems = refs[2 * n:]
        x, y, c = _place()
        cps = [_remote(ins[a], outs[a], send_sems.at[a, 0], recv_sems.at[a, 0], (x, y, 1 - c)) for a in range(n)]
        for cp in cps:
            cp.start()
        for cp in cps:
            cp.wait()

    return _comm_call(name, body, blocks, [jax.ShapeDtypeStruct(b.shape, b.dtype) for b in blocks], 1)


def swap_other_half(name, arrays):
    n = len(arrays)

    def body(*refs):
        ins, outs = refs[:n], refs[n:2 * n]
        send_sems, recv_sems = refs[2 * n:]
        x, y, c = _place()
        cps = [_remote(ins[a].at[:, 1 - c], outs[a], send_sems.at[a, 0], recv_sems.at[a, 0], (x, y, 1 - c))
               for a in range(n)]
        for cp in cps:
            cp.start()
        for cp in cps:
            cp.wait()

    out_shape = [jax.ShapeDtypeStruct((b.shape[0], *b.shape[2:]), b.dtype) for b in arrays]
    return _comm_call(name, body, arrays, out_shape, 1)


SEM_SPEC = pl.BlockSpec(memory_space=pltpu.SEMAPHORE)


def _exchange_copies(srcs, lands, send_sems, recv_sems):
    x, y, c = _place()
    p = 2 * x + y
    cps = []
    for a, (src, land) in enumerate(zip(srcs, lands)):
        for k, (fx, fy) in enumerate(CHIP_FLIPS):
            qx, qy = _flip(x, fx), _flip(y, fy)
            sem = len(CHIP_FLIPS) * a + k
            cps.append(_remote(src.at[2 * qx + qy], land.at[p], send_sems.at[sem], recv_sems.at[sem], (qx, qy, c)))
    return cps


def _gather_copies(srcs, lands, send_sems, recv_sems):
    x, y, c = _place()
    p = 2 * x + y
    cps = []
    for a, (src, land) in enumerate(zip(srcs, lands)):
        for k, (fx, fy) in enumerate(CHIP_FLIPS):
            sem = len(CHIP_FLIPS) * a + k
            cps.append(_remote(src.at[c], land.at[p, c], send_sems.at[sem], recv_sems.at[sem],
                               (_flip(x, fx), _flip(y, fy), c)))
    return cps


def copies_start(name, make_copies, srcs, land_shapes):
    n = len(srcs)

    def body(*refs):
        for cp in make_copies(refs[:n], refs[n:2 * n], refs[2 * n], refs[2 * n + 1]):
            cp.start()
        refs[-1][...] = jnp.zeros_like(refs[-1])

    thru = [pltpu.HBM(b.shape, b.dtype) for b in srcs] + [pltpu.HBM(sh, b.dtype) for sh, b in zip(land_shapes, srcs)]
    outs = pl.pallas_call(
        body, name=name, in_specs=[HBM_SPEC] * (2 * n),
        out_shape=(pltpu.SemaphoreType.DMA((3 * n,)), pltpu.SemaphoreType.DMA((3 * n,)), *thru,
                   jax.ShapeDtypeStruct((8, 128), F32)),
        out_specs=(SEM_SPEC, SEM_SPEC, *[HBM_SPEC] * (2 * n), VMEM_SPEC),
        input_output_aliases={i: 2 + i for i in range(2 * n)},
        compiler_params=pltpu.CompilerParams(has_side_effects=pltpu.SideEffectType.DATAFLOW_SIDE_EFFECTING),
    )(*[pltpu.with_memory_space_constraint(b, pltpu.HBM) for b in srcs],
      *[pltpu.with_memory_space_constraint(lax.empty(sh, b.dtype), pltpu.HBM) for sh, b in zip(land_shapes, srcs)])
    return outs[:-1], outs[-1]


def copies_wait(name, make_copies, state, after):
    send_sems, recv_sems, *thru = state
    n = len(thru) // 2

    def body(*refs):
        for cp in make_copies(refs[:n], refs[n:2 * n], refs[2 * n], refs[2 * n + 1]):
            cp.wait_send()
            cp.wait_recv()

    outs = pl.pallas_call(
        body, name=name, in_specs=[HBM_SPEC] * (2 * n) + [SEM_SPEC, SEM_SPEC, pl.BlockSpec(memory_space=pl.ANY)],
        out_shape=tuple(pltpu.HBM(t.shape, t.dtype) for t in thru), out_specs=tuple([HBM_SPEC] * (2 * n)),
        input_output_aliases={i: i for i in range(2 * n)},
        compiler_params=pltpu.CompilerParams(has_side_effects=pltpu.SideEffectType.DATAFLOW_SIDE_EFFECTING),
    )(*thru, send_sems, recv_sems, after)
    return outs[n:]


def pass_to_sibling(name, lands):
    n = len(lands)

    def body(*refs):
        ins, outs = refs[:n], refs[n:2 * n]
        send_sems, recv_sems = refs[2 * n:]
        x, y, c = _place()
        cps = []
        for a in range(n):
            for k, (fx, fy) in enumerate(CHIP_FLIPS):
                q = 2 * _flip(x, fx) + _flip(y, fy)
                cps.append(_remote(ins[a].at[q, c], outs[a].at[q, c], send_sems.at[a, k], recv_sems.at[a, k],
                                   (x, y, 1 - c)))
        for cp in cps:
            cp.start()
        for cp in cps:
            cp.wait()

    return pl.pallas_call(
        body, name=name, in_specs=[HBM_SPEC] * n, out_specs=[HBM_SPEC] * n,
        out_shape=[jax.ShapeDtypeStruct(b.shape, b.dtype) for b in lands],
        scratch_shapes=[pltpu.SemaphoreType.DMA((n, 3)), pltpu.SemaphoreType.DMA((n, 3))],
        input_output_aliases={i: i for i in range(n)},
    )(*lands)


DEVICE_FLIPS = tuple((fx, fy, fc) for fx in (0, 1) for fy in (0, 1) for fc in (0, 1))[1:]


def gather_small(name, v, reduce):
    r, cdim = v.shape
    n_dev = 8

    def body(v_ref, out_ref, *scratch):
        buf = scratch[0] if reduce else out_ref
        send_sems, recv_sems = scratch[-2:]
        x, y, c = _place()
        me = 4 * x + 2 * y + c
        buf[me] = v_ref[...]
        cps = []
        for k, (fx, fy, fc) in enumerate(DEVICE_FLIPS):
            cps.append(_remote(v_ref, buf.at[me], send_sems.at[k], recv_sems.at[k],
                               (_flip(x, fx), _flip(y, fy), _flip(c, fc))))
        for cp in cps:
            cp.start()
        for cp in cps:
            cp.wait()
        if reduce:
            total = buf[0]
            for d in range(1, n_dev):
                total = total + buf[d]
            out_ref[...] = total

    scratch = [pltpu.SemaphoreType.DMA((7,)), pltpu.SemaphoreType.DMA((7,))]
    if reduce:
        scratch = [pltpu.VMEM((n_dev, r, cdim), F32)] + scratch
    out_shape = jax.ShapeDtypeStruct((r, cdim) if reduce else (n_dev, r, cdim), F32)
    return pl.pallas_call(body, name=name, in_specs=[VMEM_SPEC], out_specs=VMEM_SPEC, out_shape=out_shape,
                          scratch_shapes=scratch)(v)


def pair_sum(name, own, got, core):
    _, hx, cols = got.shape
    tr = _tile(hx, (256, 128, 64, 32, 16))

    def kern(core_ref, a_ref, b_ref, o_ref):
        o_ref[...] = (a_ref[...].astype(F32) + b_ref[...].astype(F32)).astype(BF16)

    grid_spec = pltpu.PrefetchScalarGridSpec(
        num_scalar_prefetch=1, grid=(hx // tr,),
        in_specs=[pl.BlockSpec((N_CHIPS, None, tr, cols), lambda i, cr: (0, cr[0], i, 0)),
                  pl.BlockSpec((N_CHIPS, tr, cols), lambda i, cr: (0, i, 0))],
        out_specs=pl.BlockSpec((N_CHIPS, tr, cols), lambda i, cr: (0, i, 0)))
    return pl.pallas_call(
        kern, name=name, grid_spec=grid_spec, out_shape=jax.ShapeDtypeStruct(got.shape, BF16),
        compiler_params=pltpu.CompilerParams(dimension_semantics=("parallel",), vmem_limit_bytes=VMEM_LIMIT_BYTES),
    )(core, own, got)


def chip_sum(name, mine, parts, chip):
    _, hx, yd = parts.shape
    tr = _tile(hx, (256, 128, 64, 32, 16))

    def kern(chip_ref, m_ref, p_ref, o_ref):
        total = None
        for q in range(N_CHIPS):
            term = jnp.where(chip_ref[0] == q, m_ref[...], p_ref[q]).astype(F32)
            total = term if total is None else total + term
        o_ref[...] = total

    grid_spec = pltpu.PrefetchScalarGridSpec(
        num_scalar_prefetch=1, grid=(hx // tr,),
        in_specs=[pl.BlockSpec((None, tr, yd), lambda i, cr: (cr[0], i, 0)),
                  pl.BlockSpec((N_CHIPS, tr, yd), lambda i, cr: (0, i, 0))],
        out_specs=pl.BlockSpec((tr, yd), lambda i, cr: (i, 0)))
    return pl.pallas_call(
        kern, name=name, grid_spec=grid_spec, out_shape=jax.ShapeDtypeStruct((hx, yd), F32),
        compiler_params=pltpu.CompilerParams(dimension_semantics=("parallel",), vmem_limit_bytes=VMEM_LIMIT_BYTES),
    )(chip, mine, parts)


WEIGHTS = ("g_mix_pre", "g_mix_post", "g_cross_pre", "g_mem", "g_cross_post", "g_ffn_pre", "g_ffn_post", "w_xq", "w_xkv",
           "w_xo", "w_ffn_gu", "w_ffn_down", "ab_w_in", "ab_b_f", "ab_conv_w", "ab_w_out", "c_w_in", "c_conv_w",
           "c_conv_b", "c_w_a", "c_b_a", "c_w_i", "c_b_i", "c_lam", "c_w_out")
SHARD_DIM = {"w_xq": 1, "w_xkv": 2, "w_xo": 1, "w_ffn_gu": 2, "w_ffn_down": 1, "ab_w_in": 2, "ab_conv_w": 2,
             "ab_w_out": 1, "c_w_in": 2, "c_conv_w": 2, "c_conv_b": 1, "c_w_a": 2, "c_b_a": 2, "c_w_i": 2, "c_b_i": 2,
             "c_lam": 1, "c_w_out": 1}
COMMON_BIG = ("w_xq", "w_xkv", "w_xo", "w_ffn_gu", "w_ffn_down")
EVEN_BIG, ODD_BIG = ("ab_w_in", "ab_w_out"), ("c_w_in", "c_w_a", "c_w_i", "c_w_out")
BIG = COMMON_BIG + EVEN_BIG + ODD_BIG


def layer_big(layer):
    return COMMON_BIG + (ODD_BIG if layer % 2 else EVEN_BIG)


SPLIT_LAYERS = (0,)


def chunk_names(layer, part):
    mixer = layer_big(layer)[len(COMMON_BIG):]
    if layer in SPLIT_LAYERS:
        return mixer if part == "mix" else COMMON_BIG
    return layer_big(layer) if part == "mix" else ()


SMALL_SHARDED = ("ab_conv_w", "c_conv_w", "c_conv_b", "c_b_a", "c_b_i", "c_lam")
REPLICATED = ("g_mix_pre", "g_mix_post", "g_cross_pre", "g_mem", "g_cross_post", "g_ffn_pre", "g_ffn_post", "ab_b_f")
PACK_COLS = 1024


def _unshard(g, d):
    shard = g.shape[1:]
    return jnp.moveaxis(g, 0, d).reshape(shard[:d] + (N_CHIPS * shard[d],) + shard[d + 1:])


def _shardify(full, d):
    s = full.shape
    return jnp.moveaxis(full.reshape(s[:d] + (N_CHIPS, s[d] // N_CHIPS) + s[d + 1:]), d, 0)


def _pack(arrays, rows):
    flat = jnp.concatenate([a.reshape(-1).astype(F32) for a in arrays])
    return jnp.pad(flat, (0, rows * PACK_COLS - flat.shape[0])).reshape(rows, PACK_COLS)


def _unpack(packed, shapes):
    flat = packed.reshape(-1)
    out, at = [], 0
    for s in shapes:
        size = math.prod(s)
        out.append(flat[at:at + size].reshape(s))
        at += size
    return out


def _rows_for(shapes):
    return -(-sum(math.prod(s) for s in shapes) // (8 * PACK_COLS)) * 8


def kernel(x, mem, g_mix_pre, g_mix_post, g_cross_pre, g_mem, g_cross_post, g_ffn_pre, g_ffn_post, w_xq, w_xkv, w_xo, w_ffn_gu, w_ffn_down, ab_w_in, ab_b_f, ab_conv_w, ab_w_out, c_w_in, c_conv_w, c_conv_b, c_w_a, c_b_a, c_w_i, c_b_i, c_lam, c_w_out, loss_target, m_g_mix_pre, m_g_mix_post, m_g_cross_pre, m_g_mem, m_g_cross_post, m_g_ffn_pre, m_g_ffn_post, m_w_xq, m_w_xkv, m_w_xo, m_w_ffn_gu, m_w_ffn_down, m_ab_w_in, m_ab_b_f, m_ab_conv_w, m_ab_w_out, m_c_w_in, m_c_conv_w, m_c_conv_b, m_c_w_a, m_c_b_a, m_c_w_i, m_c_b_i, m_c_lam, m_c_w_out, v_g_mix_pre, v_g_mix_post, v_g_cross_pre, v_g_mem, v_g_cross_post, v_g_ffn_pre, v_g_ffn_post, v_w_xq, v_w_xkv, v_w_xo, v_w_ffn_gu, v_w_ffn_down, v_ab_w_in, v_ab_b_f, v_ab_conv_w, v_ab_w_out, v_c_w_in, v_c_conv_w, v_c_conv_b, v_c_w_a, v_c_b_a, v_c_w_i, v_c_b_i, v_c_lam, v_c_w_out):
    given = dict(locals())
    w = {n: given[n] for n in WEIGHTS}
    m_in = {n: given["m_" + n] for n in WEIGHTS}
    v_in = {n: given["v_" + n] for n in WEIGHTS}
    xi, yi, ci = _place()
    chip = 2 * xi + yi

    full = {}
    small_shapes = [w[n].shape for n in SMALL_SHARDED]
    rows_w = _rows_for(small_shapes)
    assert rows_w * PACK_COLS > sum(math.prod(s) for s in small_shapes)
    every = gather_small("gather_small_weights", _pack([w[n] for n in SMALL_SHARDED], rows_w), reduce=False)
    per_chip = every[0::2].reshape(N_CHIPS, -1)
    at = 0
    for n, s in zip(SMALL_SHARDED, small_shapes):
        size = math.prod(s)
        full[n] = _unshard(per_chip[:, at:at + size].reshape(N_CHIPS, *s), SHARD_DIM[n])
        at += size
    for n in REPLICATED:
        full[n] = w[n]
    after_small = every[0, -1, -1].astype(BF16)

    depth = g_mix_pre.shape[0]
    own, gathers, tokens = {}, {}, []
    for layer in range(depth):
        for part in ("mix", "rest"):
            names = chunk_names(layer, part)
            if names:
                tagp = f"l{layer}_{part}"
                own[tagp] = {n: w[n][layer if n in COMMON_BIG else layer // 2].astype(BF16) + after_small for n in names}
                halves = [a.reshape(2, -1, a.shape[-1]) for a in own[tagp].values()]
                gathers[tagp], token = copies_start(f"gather_start_{tagp}", _gather_copies, halves,
                                                    [(N_CHIPS, *h.shape) for h in halves])
                tokens.append(token[0, 0])

    def layer_weights(layer, part, x_in):
        tagp = f"l{layer}_{part}"
        if tagp not in gathers:
            return {}
        lands = copies_wait(f"gather_wait_{tagp}", _gather_copies, gathers[tagp], x_in)
        lands = pass_to_sibling(f"gather_pass_{tagp}", lands)
        out = {}
        for (n, mine), g in zip(own[tagp].items(), lands):
            slot = lax.broadcasted_iota(jnp.int32, (N_CHIPS,) + (1,) * mine.ndim, 0)
            g = jnp.where(slot == chip, mine[None], g.reshape(N_CHIPS, *mine.shape))
            out[n] = _unshard(g, SHARD_DIM[n] - 1)
        return out

    core_arr, chip_arr = ci.reshape(1).astype(jnp.int32), chip.reshape(1).astype(jnp.int32)
    in_flight, held = [], {}

    def reduce_hook(layer, part, part_grads):
        held.update(part_grads)
        names = chunk_names(layer, part)
        if not names:
            return None
        tagp = f"l{layer}_{part}"
        mine = [held.pop(n) for n in names]
        got = swap_other_half(f"swap_grads_{tagp}", mine)
        sums = [pair_sum(f"pair_sum_{tagp}_{n}", o, g, core_arr) for n, o, g in zip(names, mine, got)]
        state, token = copies_start(f"exchange_start_{tagp}", _exchange_copies, sums, [b.shape for b in sums])
        in_flight.append((layer, tagp, names, sums, state))
        return token[0, 0]

    sq_cols, dx, grads, _ = local_step(x[0] + sum(tokens), mem[0], loss_target[0], full, layer_weights, reduce_hook)
    loss = lax.psum(0.5 / D_MODEL * jnp.sum(sq_cols), ("x", "y", "c"))

    keys, halves = [], []
    for layer, tagp, names, sums, state in in_flight:
        parts = copies_wait(f"exchange_wait_{tagp}", _exchange_copies, state, dx)
        for n, mine, p in zip(names, sums, parts):
            keys.append((n, layer))
            halves.append(chip_sum(f"chip_sum_{tagp}_{n}", mine, p, chip_arr))
    others = swap_with_sibling("swap_reduced_halves", halves)
    reduced = {k: (mine, theirs) for k, mine, theirs in zip(keys, halves, others)}
    grad_out = {}
    for n in BIG:
        pieces = []
        for layer in sorted(l for k, l in keys if k == n):
            mine, theirs = reduced[(n, layer)]
            pieces += [jnp.where(ci == 0, mine, theirs), jnp.where(ci == 0, theirs, mine)]
        grad_out[n] = jnp.concatenate(pieces, axis=0).reshape(w[n].shape)

    small_names = REPLICATED + SMALL_SHARDED
    small_full_shapes = [grads[n].shape for n in small_names]
    total = gather_small("reduce_small_grads", _pack([grads[n] for n in small_names], _rows_for(small_full_shapes)),
                         reduce=True)
    for n, g in zip(small_names, _unpack(total, small_full_shapes)):
        if n in SHARD_DIM:
            g = lax.dynamic_index_in_dim(_shardify(g, SHARD_DIM[n]), chip, axis=0, keepdims=False)
        grad_out[n] = g

    delta, new_m, new_v = {}, {}, {}
    for n in BIG:
        two_d = lambda a: a.reshape(-1, a.shape[-1])
        d, m2, v2 = adamw(f"adamw_{n}", two_d(w[n]), two_d(grad_out[n]), two_d(m_in[n]), two_d(v_in[n]))
        delta[n], new_m[n], new_v[n] = (a.reshape(w[n].shape) for a in (d, m2, v2))
    shapes = [w[n].shape for n in small_names]
    rows = _rows_for(shapes)
    packed = [_pack([src[n] for n in small_names], rows) for src in (w, grad_out, m_in, v_in)]
    for dst, res in zip((delta, new_m, new_v), adamw("adamw_small", *packed)):
        for n, a in zip(small_names, _unpack(res, shapes)):
            dst[n] = a

    return (loss, dx[None], *[grad_out[n] for n in WEIGHTS], *[delta[n] for n in WEIGHTS],
            *[new_m[n] for n in WEIGHTS], *[new_v[n] for n in WEIGHTS])
```

```python
import functools
import math

import jax
import jax.numpy as jnp
from jax import lax
from jax.experimental import pallas as pl
from jax.experimental.pallas import tpu as pltpu

F32, BF16 = jnp.float32, jnp.bfloat16
D_MODEL = 1024
EPS = 1e-6
NEG_INF = -1e30
FOX_HEADS, FOX_HEAD_DIM, FOX_WIDTH = 8, 64, 512
SC_WIDTH = 512
AB_IN = 3 * FOX_WIDTH + FOX_HEADS + 3 * SC_WIDTH
AB_IN_PAD = 3200
LRU_BW, LRU_BLOCKS = 256, 4
RG_C = 8.0
MEM_HEADS, MEM_HEAD_DIM = 4, 256
ADAM_LR, ADAM_B1, ADAM_B2, ADAM_EPS, ADAM_WD, ADAM_STEP = 0.001, 0.9, 0.999, 1e-08, 0.01, 10
N_CHIPS = 4
MESH = pl.DeviceIdType.MESH
VMEM_LIMIT_BYTES = 48 * 1024 * 1024
MM_OPERAND_TILE_BYTES = 7 * 1024 * 1024

NN = (((1,), (0,)), ((), ()))
NT = (((1,), (1,)), ((), ()))
TN = (((0,), (0,)), ((), ()))


def _dot(a, b, dn=NN):
    return lax.dot_general(a.astype(BF16), b.astype(BF16), dn, preferred_element_type=F32)


def _tile(n, prefs):
    for p in prefs:
        if n % p == 0:
            return p
    return n


def _pcall(name, kern, grid, ins, in_specs, out_shape, out_specs, sem):
    return pl.pallas_call(
        kern, name=name, grid=grid, in_specs=in_specs, out_specs=out_specs, out_shape=out_shape,
        compiler_params=pltpu.CompilerParams(dimension_semantics=sem, vmem_limit_bytes=VMEM_LIMIT_BYTES),
    )(*ins)


def mm(name, a, b, mode, out_dtype, reduce_layout=False):
    if mode == "nn":
        (m, k), n = a.shape, b.shape[1]
    elif mode == "nt":
        (m, k), n = a.shape, b.shape[0]
    else:
        (k, m), n = a.shape, b.shape[1]
    if reduce_layout:
        tm, tn = m // 2, n // N_CHIPS
    else:
        tn = _tile(n, ((1024,) if mode == "tn" else ()) + (512, 640, 256, 128))
        tm = next(c for c in (2048, 1024, 512, 256, 128, m)
                  if m % c == 0 and 2 * c * k <= MM_OPERAND_TILE_BYTES and 4 * c * tn <= MM_OPERAND_TILE_BYTES)
    dn = {"nn": NN, "nt": NT, "tn": TN}[mode]

    def kern(a_ref, b_ref, o_ref):
        o_ref[...] = _dot(a_ref[...], b_ref[...], dn).astype(o_ref.dtype)

    a_spec = pl.BlockSpec((k, tm), lambda i, j: (0, i)) if mode == "tn" else pl.BlockSpec((tm, k), lambda i, j: (i, 0))
    b_spec = pl.BlockSpec((tn, k), lambda i, j: (j, 0)) if mode == "nt" else pl.BlockSpec((k, tn), lambda i, j: (0, j))
    if reduce_layout:
        out_shape = jax.ShapeDtypeStruct((N_CHIPS, 2, tm, tn), out_dtype)
        o_spec = pl.BlockSpec((None, None, tm, tn), lambda i, j: (j, i, 0, 0))
    else:
        out_shape = jax.ShapeDtypeStruct((m, n), out_dtype)
        o_spec = pl.BlockSpec((tm, tn), lambda i, j: (i, j))
    return _pcall(name, kern, (m // tm, n // tn), (a, b), [a_spec, b_spec], out_shape, o_spec, ("parallel", "parallel"))


def rowwise(name, body, rows, params, outs, accs=(), tr=256):
    t = rows[0].shape[0]
    tr = min(tr, t)
    nr, npar, no = len(rows), len(params), len(outs)

    def kern(*refs):
        acc_refs = refs[nr + npar + no:]
        if acc_refs:
            @pl.when(pl.program_id(0) == 0)
            def _():
                for ar in acc_refs:
                    ar[...] = jnp.zeros_like(ar)
        body(refs[:nr], refs[nr:nr + npar], refs[nr + npar:nr + npar + no], acc_refs)

    in_specs = [pl.BlockSpec((tr, x.shape[1]), lambda i: (i, 0)) for x in rows]
    in_specs += [pl.BlockSpec(p.shape, lambda i: (0, 0)) for p in params]
    out_specs = [pl.BlockSpec((tr, c), lambda i: (i, 0)) for c, _ in outs]
    out_specs += [pl.BlockSpec(s, lambda i: (0, 0)) for s in accs]
    out_shape = [jax.ShapeDtypeStruct((t, c), dt) for c, dt in outs]
    out_shape += [jax.ShapeDtypeStruct(s, F32) for s in accs]
    return _pcall(name, kern, (t // tr,), (*rows, *params), in_specs, out_shape, out_specs,
                  ("arbitrary",) if accs else ("parallel",))


def _rms_stats(x):
    r = lax.rsqrt(jnp.mean(x * x, axis=-1, keepdims=True) + EPS)
    return r, x * r


def _rms_bwd(xh, r, g, dy):
    dxh = dy * g
    dx = r * (dxh - xh * jnp.mean(dxh * xh, axis=-1, keepdims=True))
    return dx, jnp.sum(dy * xh, axis=0, keepdims=True)


def rms_pre(name, x, gains, layer):
    def body(r, p, o, a):
        _, xh = _rms_stats(r[0][...])
        o[0][...] = (xh * p[0][layer:layer + 1, :]).astype(BF16)
    return rowwise(name, body, [x], [gains], [(x.shape[1], BF16)])[0]


def post_add(name, x, y, gains, layer):
    def body(r, p, o, a):
        _, yh = _rms_stats(r[1][...])
        o[0][...] = r[0][...] + yh * p[0][layer:layer + 1, :]
    return rowwise(name, body, [x, y], [gains], [(x.shape[1], F32)])[0]


def post_bwd(name, y, dx, gains, layer):
    def body(r, p, o, a):
        rr, yh = _rms_stats(r[0][...])
        dy, dg = _rms_bwd(yh, rr, p[0][layer:layer + 1, :], r[1][...])
        o[0][...] = dy.astype(BF16)
        a[0][...] += dg
    c = y.shape[1]
    return rowwise(name, body, [y, dx], [gains], [(c, BF16)], [(1, c)])


def pre_bwd(name, x, dh, dx_res, gains, layer):
    def body(r, p, o, a):
        rr, xh = _rms_stats(r[0][...])
        dx, dg = _rms_bwd(xh, rr, p[0][layer:layer + 1, :], r[1][...])
        o[0][...] = r[2][...] + dx
        a[0][...] += dg
    c = x.shape[1]
    return rowwise(name, body, [x, dh, dx_res], [gains], [(c, F32)], [(1, c)])


def gain_bwd(name, x, dh):
    def body(r, p, o, a):
        _, xh = _rms_stats(r[0][...])
        a[0][...] += jnp.sum(r[1][...] * xh, axis=0, keepdims=True)
    return rowwise(name, body, [x, dh], [], [], [(1, x.shape[1])])[0]


def _sigmoid(z):
    return 1.0 / (1.0 + jnp.exp(-z))


def swiglu_fwd(name, gu):
    f = gu.shape[1] // 2

    def body(r, p, o, a):
        g = r[0][:, :f].astype(F32)
        u = r[0][:, f:].astype(F32)
        o[0][...] = (g * _sigmoid(g) * u).astype(BF16)
    return rowwise(name, body, [gu], [], [(f, BF16)])[0]


def swiglu_bwd(name, gu, da):
    f = gu.shape[1] // 2

    def body(r, p, o, a):
        g = r[0][:, :f].astype(F32)
        u = r[0][:, f:].astype(F32)
        d = r[1][...].astype(F32)
        sg = _sigmoid(g)
        o[0][:, :f] = (d * u * sg * (1.0 + g * (1.0 - sg))).astype(BF16)
        o[0][:, f:] = (d * g * sg).astype(BF16)
    return rowwise(name, body, [gu, da], [], [(2 * f, BF16)])[0]


def loss_head(name, y, target):
    c = y.shape[1]

    def body(r, p, o, a):
        e = r[0][...] - r[1][...]
        o[0][...] = e * (1.0 / c)
        a[0][...] += jnp.sum(e * e, axis=0, keepdims=True)
    return rowwise(name, body, [y, target], [], [(c, F32)], [(1, c)])


def adamw(name, w, g, m, v):
    c = w.shape[1]

    def body(r, p, o, a):
        wv, gv, mv, vv = (x[...] for x in r)
        m2 = ADAM_B1 * mv + (1.0 - ADAM_B1) * gv
        v2 = ADAM_B2 * vv + (1.0 - ADAM_B2) * (gv * gv)
        m_hat = m2 / (1.0 - ADAM_B1 ** ADAM_STEP)
        v_hat = v2 / (1.0 - ADAM_B2 ** ADAM_STEP)
        o[0][...] = -ADAM_LR * (m_hat / (jnp.sqrt(v_hat) + ADAM_EPS) + ADAM_WD * wv)
        o[1][...] = m2
        o[2][...] = v2
    tr = _tile(w.shape[0], (256, 128, 64, 32, 16, 8))
    return rowwise(name, body, [w, g, m, v], [], [(c, F32)] * 3, tr=tr)


def colwise(name, body, cols, params, outs, pouts=(), tc=128):
    t = cols[0][0].shape[0]
    c = params[0].shape[1] if params else cols[0][0].shape[1]
    nc, npar, no = len(cols), len(params), len(outs)

    def kern(*refs):
        body(refs[:nc], refs[nc:nc + npar], refs[nc + npar:nc + npar + no], refs[nc + npar + no:])

    in_specs = [pl.BlockSpec((t, tc), functools.partial(lambda j, off: (0, j + off), off=off)) for _, off in cols]
    in_specs += [pl.BlockSpec((p.shape[0], tc), lambda j: (0, j)) for p in params]
    out_specs = [pl.BlockSpec((t, tc), lambda j: (0, j)) for _ in outs]
    out_specs += [pl.BlockSpec((r, tc), lambda j: (0, j)) for r in pouts]
    out_shape = [jax.ShapeDtypeStruct((t, c), dt) for dt in outs]
    out_shape += [jax.ShapeDtypeStruct((r, c), F32) for r in pouts]
    return _pcall(name, kern, (c // tc,), (*[x for x, _ in cols], *params), in_specs, out_shape, out_specs,
                  ("parallel",))


def _row_index(shape):
    return lax.broadcasted_iota(jnp.int32, shape, 0)


def _shift_down(x, d, rows):
    return jnp.where(rows >= d, pltpu.roll(x, d, 0), 0.0)


def _shift_up(x, d, rows):
    t = x.shape[0]
    return jnp.where(rows < t - d, pltpu.roll(x, t - d, 0), 0.0)


def sconv_fwd(name, proj, col0, conv_w, tc=128):
    nb = SC_WIDTH // tc

    def body(cl, p, o, po):
        b, c, u = (x[...] for x in cl)
        rows = _row_index(b.shape)
        w = p[0][...]
        z = c * u
        conv = w[2:3] * z + w[1:2] * _shift_down(z, 1, rows) + w[0:1] * _shift_down(z, 2, rows)
        o[0][...] = (b * conv).astype(BF16)
    return colwise(name, body, [(proj, col0), (proj, col0 + nb), (proj, col0 + 2 * nb)], [conv_w], [BF16], tc=tc)[0]


def sconv_bwd(name, proj, col0, conv_w, dyb, dcol0, tc=128):
    nb = SC_WIDTH // tc

    def body(cl, p, o, po):
        b, c, u, dy = (x[...] for x in cl)
        rows = _row_index(b.shape)
        w = p[0][...]
        z = c * u
        z1, z2 = _shift_down(z, 1, rows), _shift_down(z, 2, rows)
        conv = w[2:3] * z + w[1:2] * z1 + w[0:1] * z2
        dconv = dy * b
        dz = w[2:3] * dconv + w[1:2] * _shift_up(dconv, 1, rows) + w[0:1] * _shift_up(dconv, 2, rows)
        o[0][...] = (dy * conv).astype(BF16)
        o[1][...] = (dz * u).astype(BF16)
        o[2][...] = (dz * c).astype(BF16)
        po[0][0:1, :] = jnp.sum(dconv * z2, axis=0, keepdims=True)
        po[0][1:2, :] = jnp.sum(dconv * z1, axis=0, keepdims=True)
        po[0][2:3, :] = jnp.sum(dconv * z, axis=0, keepdims=True)
    return colwise(name, body, [(proj, col0), (proj, col0 + nb), (proj, col0 + 2 * nb), (dyb, dcol0)], [conv_w],
                   [BF16, BF16, BF16], [3], tc=tc)


def _expm1(x):
    series = x * (1.0 + 0.5 * x * (1.0 + x * (1.0 / 3.0) * (1.0 + 0.25 * x * (1.0 + 0.2 * x))))
    return jnp.where(jnp.abs(x) < 0.05, series, jnp.exp(x) - 1.0)


def _log1p(x):
    series = x * (1.0 - x * (0.5 - x * (1.0 / 3.0 - 0.25 * x)))
    return jnp.where(jnp.abs(x) < 0.01, series, jnp.log(1.0 + x))


def _softplus_neg(lam):
    sp = jnp.maximum(-lam, 0.0) + _log1p(jnp.exp(-jnp.abs(lam)))
    return sp, -_sigmoid(-lam)


GELU_C = math.sqrt(2.0 / math.pi)


def _gelu(x):
    th = jnp.tanh(GELU_C * (x + 0.044715 * x * x * x))
    val = 0.5 * x * (1.0 + th)
    grad = 0.5 * (1.0 + th) + 0.5 * x * (1.0 - th * th) * GELU_C * (1.0 + 3.0 * 0.044715 * x * x)
    return val, grad


def rg_conv_fwd(name, gu2, conv_w, conv_b, tc=128):
    nb = D_MODEL // tc

    def body(cl, p, o, po):
        u = cl[0][...]
        rows = _row_index(u.shape)
        w = p[0][...]
        o[0][...] = (w[3:4] * u + w[2:3] * _shift_down(u, 1, rows) + w[1:2] * _shift_down(u, 2, rows)
                     + w[0:1] * _shift_down(u, 3, rows) + p[1][...])
    return colwise(name, body, [(gu2, nb)], [conv_w, conv_b], [F32], tc=tc)[0]


def rg_conv_bwd(name, gu2, duc, conv_w, tc=128):
    nb = D_MODEL // tc

    def body(cl, p, o, po):
        u, d = cl[0][...], cl[1][...]
        rows = _row_index(u.shape)
        w = p[0][...]
        o[0][...] = (w[3:4] * d + w[2:3] * _shift_up(d, 1, rows) + w[1:2] * _shift_up(d, 2, rows)
                     + w[0:1] * _shift_up(d, 3, rows)).astype(BF16)
        for k in range(4):
            uk = u if k == 3 else _shift_down(u, 3 - k, rows)
            po[0][k:k + 1, :] = jnp.sum(d * uk, axis=0, keepdims=True)
        po[1][...] = jnp.sum(d, axis=0, keepdims=True)
    return colwise(name, body, [(gu2, nb), (duc, 0)], [conv_w], [BF16], [4, 1], tc=tc)


def rg_gates_fwd(name, uc, w_a, b_a, w_i, b_i, tr=512):
    t = uc.shape[0]
    tr = min(tr, t)

    def kern(u_ref, wa_ref, ba_ref, wi_ref, bi_ref, r_ref, i_ref):
        ub = u_ref[...].astype(BF16)
        r_ref[...] = _sigmoid(_dot(ub, wa_ref[...]) + ba_ref[...])
        i_ref[...] = _sigmoid(_dot(ub, wi_ref[...]) + bi_ref[...])

    blk = pl.BlockSpec((tr, LRU_BW), lambda n, i: (i, n))
    wspec = pl.BlockSpec((None, LRU_BW, LRU_BW), lambda n, i: (n, 0, 0))
    bspec = pl.BlockSpec((1, LRU_BW), lambda n, i: (0, n))
    return _pcall(name, kern, (LRU_BLOCKS, t // tr), (uc, w_a, b_a, w_i, b_i), [blk, wspec, bspec, wspec, bspec],
                  [jax.ShapeDtypeStruct(uc.shape, F32)] * 2, [blk, blk], ("parallel", "parallel"))


def rg_gates_bwd(name, uc, dzr, dzi, duc_part, w_a, w_i):
    t = uc.shape[0]
    rows = LRU_BW // N_CHIPS

    def kern(u_ref, dr_ref, di_ref, dp_ref, wa_ref, wi_ref, duc_ref, dwa_ref, dwi_ref):
        ub = u_ref[...].astype(BF16)
        dr, di = dr_ref[...], di_ref[...]
        dwa, dwi = _dot(ub, dr, TN), _dot(ub, di, TN)
        for p in range(N_CHIPS):
            dwa_ref[p] = dwa[p * rows:(p + 1) * rows].astype(dwa_ref.dtype)
            dwi_ref[p] = dwi[p * rows:(p + 1) * rows].astype(dwi_ref.dtype)
        duc_ref[...] = dp_ref[...] + _dot(dr, wa_ref[...], NT) + _dot(di, wi_ref[...], NT)

    blk = pl.BlockSpec((t, LRU_BW), lambda n: (0, n))
    wspec = pl.BlockSpec((None, LRU_BW, LRU_BW), lambda n: (n, 0, 0))
    gspec = pl.BlockSpec((N_CHIPS, None, rows, LRU_BW), lambda n: (0, n, 0, 0))
    gshape = jax.ShapeDtypeStruct((N_CHIPS, LRU_BLOCKS, rows, LRU_BW), BF16)
    return _pcall(name, kern, (LRU_BLOCKS,), (uc, dzr, dzi, duc_part, w_a, w_i), [blk, blk, blk, blk, wspec, wspec],
                  [jax.ShapeDtypeStruct(uc.shape, F32), gshape, gshape], [blk, gspec, gspec], ("parallel",))


def _rg_decay(r, lam):
    sp, dsp = _softplus_neg(lam)
    la = -RG_C * r * sp
    a = jnp.exp(la)
    sq = jnp.sqrt(-_expm1(2.0 * la))
    return sp, dsp, a, sq


def rg_scan_fwd(name, gu2, uc, r, i, lam, tc=128):
    def body(cl, p, o, po):
        gate, ucv, rv, iv = (x[...] for x in cl)
        t = gate.shape[0]
        rows = _row_index(gate.shape)
        _, _, a, sq = _rg_decay(rv, p[0][...])
        b = sq * (iv * ucv)
        d = 1
        while d < t:
            keep = rows >= d
            b = a * jnp.where(keep, pltpu.roll(b, d, 0), 0.0) + b
            a = a * jnp.where(keep, pltpu.roll(a, d, 0), 1.0)
            d *= 2
        o[0][...] = (_gelu(gate)[0] * b).astype(BF16)
        o[1][...] = b
    return colwise(name, body, [(gu2, 0), (uc, 0), (r, 0), (i, 0)], [lam], [BF16, F32], tc=tc)


def rg_scan_bwd(name, gu2, uc, r, i, hs, dy, lam, tc=128):
    def body(cl, p, o, po):
        gate, ucv, rv, iv, h, dyv = (x[...] for x in cl)
        t = gate.shape[0]
        rows = _row_index(gate.shape)
        sp, dsp, a, sq = _rg_decay(rv, p[0][...])
        gl, dgl = _gelu(gate)
        o[0][...] = (dyv * h * dgl).astype(BF16)
        g = dyv * gl
        am = _shift_up(a, 1, rows)
        d = 1
        while d < t:
            keep = rows < t - d
            g = am * jnp.where(keep, pltpu.roll(g, t - d, 0), 0.0) + g
            am = am * jnp.where(keep, pltpu.roll(am, t - d, 0), 0.0)
            d *= 2
        da = g * _shift_down(h, 1, rows)
        iu = iv * ucv
        d_iu = g * sq
        dla = da * a - (g * iu) * (a * a) / sq
        dzr = dla * (-RG_C * sp) * rv * (1.0 - rv)
        dzi = d_iu * ucv * iv * (1.0 - iv)
        o[1][...] = dzr.astype(BF16)
        o[2][...] = dzi.astype(BF16)
        o[3][...] = d_iu * iv
        po[0][...] = jnp.sum(dzr, axis=0, keepdims=True)
        po[1][...] = jnp.sum(dzi, axis=0, keepdims=True)
        po[2][...] = jnp.sum(dla * rv, axis=0, keepdims=True) * (-RG_C) * dsp
    return colwise(name, body, [(gu2, 0), (uc, 0), (r, 0), (i, 0), (hs, 0), (dy, 0)], [lam],
                   [BF16, BF16, BF16, F32], [1, 1, 1], tc=tc)


def _split3(x):
    hi = x.astype(BF16)
    r1 = x - hi.astype(F32)
    mid = r1.astype(BF16)
    lo = (r1 - mid.astype(F32)).astype(BF16)
    return hi, mid, lo


def _tri_dot(x, tri):
    out = None
    for piece in _split3(x):
        term = lax.dot_general(piece, tri, NN, preferred_element_type=F32)
        out = term if out is None else out + term
    return out


def fox_gates_fwd(name, z_t, b_f):
    h, t = z_t.shape
    tb = min(512, t)

    def kern(z_ref, b_ref, o_ref):
        z = z_ref[...] + b_ref[...]
        logf = jnp.minimum(z, 0.0) - _log1p(jnp.exp(-jnp.abs(z)))
        src = lax.broadcasted_iota(jnp.int32, (t, tb), 0)
        dst = lax.broadcasted_iota(jnp.int32, (t, tb), 1) + pl.program_id(0) * tb
        o_ref[...] = _tri_dot(logf, (src <= dst).astype(BF16))

    return _pcall(name, kern, (t // tb,), (z_t, b_f),
                  [pl.BlockSpec((h, t), lambda j: (0, 0)), pl.BlockSpec((h, 1), lambda j: (0, 0))],
                  jax.ShapeDtypeStruct((h, t), F32), pl.BlockSpec((h, tb), lambda j: (0, j)), ("parallel",))


def fox_gates_bwd(name, z_t, b_f, dcum_t):
    h, t = z_t.shape
    tb = min(512, t)

    def kern(z_ref, b_ref, d_ref, dz_ref, db_ref):
        @pl.when(pl.program_id(0) == 0)
        def _():
            db_ref[...] = jnp.zeros_like(db_ref)
        src = lax.broadcasted_iota(jnp.int32, (t, tb), 0)
        dst = lax.broadcasted_iota(jnp.int32, (t, tb), 1) + pl.program_id(0) * tb
        dlogf = _tri_dot(d_ref[...], (src >= dst).astype(BF16))
        z = z_ref[...] + b_ref[...]
        dz = dlogf * _sigmoid(-z)
        dz_ref[...] = dz
        db_ref[...] += jnp.sum(dz, axis=1, keepdims=True)

    return _pcall(name, kern, (t // tb,), (z_t, b_f, dcum_t),
                  [pl.BlockSpec((h, tb), lambda j: (0, j)), pl.BlockSpec((h, 1), lambda j: (0, 0)),
                   pl.BlockSpec((h, t), lambda j: (0, 0))],
                  [jax.ShapeDtypeStruct((h, t), F32), jax.ShapeDtypeStruct((h, 1), F32)],
                  [pl.BlockSpec((h, tb), lambda j: (0, j)), pl.BlockSpec((h, 1), lambda j: (0, 0))], ("arbitrary",))


def _fox_spans(qs, k_ref, cr_ref, i, tq):
    n0 = i * tq
    sd = _dot(qs, k_ref[n0:n0 + tq, :], NT) - cr_ref[:, n0:n0 + tq]
    row = lax.broadcasted_iota(jnp.int32, (tq, tq), 0)
    col = lax.broadcasted_iota(jnp.int32, (tq, tq), 1)
    spans = [(n0, tq, jnp.where(row >= col, sd, NEG_INF))]
    if i > 0:
        spans.append((0, n0, _dot(qs, k_ref[0:n0, :], NT) - cr_ref[:, 0:n0]))
    return spans


def fox_fwd(name, q, k, v, cum_r, tq=256):
    h, t, dh = q.shape
    tq = min(tq, t)
    scale = FOX_HEAD_DIM ** -0.5

    def kern(q_ref, k_ref, v_ref, cr_ref, o_ref, lse_ref):
        for i in range(t // tq):
            rows = slice(i * tq, (i + 1) * tq)
            spans = _fox_spans(q_ref[rows, :] * scale, k_ref, cr_ref, i, tq)
            m = functools.reduce(jnp.maximum, [jnp.max(s, axis=-1, keepdims=True) for _, _, s in spans])
            l, acc = 0.0, 0.0
            for k0, kn, s in spans:
                p = jnp.exp(s - m)
                l = l + jnp.sum(p, axis=-1, keepdims=True)
                acc = acc + _dot(p, v_ref[k0:k0 + kn, :])
            o_ref[rows, :] = (acc / l).astype(o_ref.dtype)
            lse_ref[rows, :] = m + jnp.log(l)

    hspec = pl.BlockSpec((None, t, dh), lambda a: (a, 0, 0))
    cspec = pl.BlockSpec((None, t, 1), lambda a: (a, 0, 0))
    rspec = pl.BlockSpec((None, 1, t), lambda a: (a, 0, 0))
    return _pcall(name, kern, (h,), (q, k, v, cum_r), [hspec, hspec, hspec, rspec],
                  [jax.ShapeDtypeStruct((h, t, dh), BF16), jax.ShapeDtypeStruct((h, t, 1), F32)],
                  [hspec, cspec], ("parallel",))


def fox_bwd(name, q, k, v, do, lse, cum_r, tq=256):
    h, t, dh = q.shape
    tq = min(tq, t)
    scale = FOX_HEAD_DIM ** -0.5

    def kern(q_ref, k_ref, v_ref, do_ref, lse_ref, cr_ref, dq_ref, dk_ref, dv_ref, dc_ref):
        dk_ref[...] = jnp.zeros_like(dk_ref)
        dv_ref[...] = jnp.zeros_like(dv_ref)
        dc_ref[...] = jnp.zeros_like(dc_ref)
        for i in range(t // tq):
            rows = slice(i * tq, (i + 1) * tq)
            qs, dov, lse_v = q_ref[rows, :] * scale, do_ref[rows, :], lse_ref[rows, :]
            spans = _fox_spans(qs, k_ref, cr_ref, i, tq)
            probs = [jnp.exp(s - lse_v) for _, _, s in spans]
            dps = [_dot(dov, v_ref[k0:k0 + kn, :], NT) for k0, kn, _ in spans]
            rowdot = sum(jnp.sum(dp * p, axis=-1, keepdims=True) for dp, p in zip(dps, probs))
            dq = 0.0
            for (k0, kn, _), p, dp in zip(spans, probs, dps):
                ds = p * (dp - rowdot)
                dq = dq + _dot(ds, k_ref[k0:k0 + kn, :])
                dk_ref[k0:k0 + kn, :] += _dot(ds, qs, TN)
                dv_ref[k0:k0 + kn, :] += _dot(p, dov, TN)
                dc_ref[:, k0:k0 + kn] -= jnp.sum(ds, axis=0, keepdims=True)
            dq_ref[rows, :] = (dq * scale).astype(dq_ref.dtype)

    hspec = pl.BlockSpec((None, t, dh), lambda a: (a, 0, 0))
    cspec = pl.BlockSpec((None, t, 1), lambda a: (a, 0, 0))
    rspec = pl.BlockSpec((None, 1, t), lambda a: (a, 0, 0))
    return _pcall(name, kern, (h,), (q, k, v, do, lse, cum_r), [hspec, hspec, hspec, hspec, cspec, rspec],
                  [jax.ShapeDtypeStruct((h, t, dh), BF16), jax.ShapeDtypeStruct((h, t, dh), F32),
                   jax.ShapeDtypeStruct((h, t, dh), F32), jax.ShapeDtypeStruct((h, 1, t), F32)],
                  [hspec, hspec, hspec, rspec], ("parallel",))


def _xattn_probs(q, k):
    s = _dot(q, k, NT) * (MEM_HEAD_DIM ** -0.5)
    p = jnp.exp(s - jnp.max(s, axis=-1, keepdims=True))
    return p / jnp.sum(p, axis=-1, keepdims=True)


def xattn_fwd(name, q, kv, tq=512):
    t = q.shape[0]
    tq = min(tq, t)
    ml = kv.shape[0]

    def kern(q_ref, k_ref, v_ref, o_ref):
        o_ref[...] = _dot(_xattn_probs(q_ref[...], k_ref[...]), v_ref[...]).astype(o_ref.dtype)

    qspec = pl.BlockSpec((tq, MEM_HEAD_DIM), lambda i, a: (i, a))
    return _pcall(name, kern, (t // tq, MEM_HEADS), (q, kv, kv),
                  [qspec, pl.BlockSpec((ml, MEM_HEAD_DIM), lambda i, a: (0, a)),
                   pl.BlockSpec((ml, MEM_HEAD_DIM), lambda i, a: (0, MEM_HEADS + a))],
                  jax.ShapeDtypeStruct(q.shape, BF16), qspec, ("parallel", "parallel"))


def xattn_bwd(name, q, kv, do, tq=512):
    t = q.shape[0]
    tq = min(tq, t)
    ml = kv.shape[0]
    scale = MEM_HEAD_DIM ** -0.5

    def kern(q_ref, k_ref, v_ref, do_ref, dq_ref, dk_ref, dv_ref):
        @pl.when(pl.program_id(1) == 0)
        def _():
            dk_ref[...] = jnp.zeros_like(dk_ref)
            dv_ref[...] = jnp.zeros_like(dv_ref)
        qv, kv_, dov = q_ref[...], k_ref[...], do_ref[...]
        p = _xattn_probs(qv, kv_)
        dp = _dot(dov, v_ref[...], NT)
        ds = p * (dp - jnp.sum(dp * p, axis=-1, keepdims=True)) * scale
        dq_ref[...] = _dot(ds, kv_).astype(dq_ref.dtype)
        dk_ref[...] += _dot(ds, qv, TN)
        dv_ref[...] += _dot(p, dov, TN)

    qspec = pl.BlockSpec((tq, MEM_HEAD_DIM), lambda a, i: (i, a))
    kspec = pl.BlockSpec((ml, MEM_HEAD_DIM), lambda a, i: (0, a))
    return _pcall(name, kern, (MEM_HEADS, t // tq), (q, kv, kv, do),
                  [qspec, kspec, pl.BlockSpec((ml, MEM_HEAD_DIM), lambda a, i: (0, MEM_HEADS + a)), qspec],
                  [jax.ShapeDtypeStruct(q.shape, BF16), jax.ShapeDtypeStruct((ml, D_MODEL), F32),
                   jax.ShapeDtypeStruct((ml, D_MODEL), F32)],
                  [qspec, kspec, kspec], ("parallel", "arbitrary"))


def _heads(x):
    t = x.shape[0]
    return x.reshape(t, FOX_HEADS, FOX_HEAD_DIM).transpose(1, 0, 2)


def _unheads(x):
    return x.transpose(1, 0, 2).reshape(x.shape[1], FOX_WIDTH)


def _row_cut(dw):
    return dw.reshape(N_CHIPS, 2, dw.shape[0] // (2 * N_CHIPS), dw.shape[1])


def local_step(x, mem, target, w, layer_weights=None, reduce_hook=None):
    depth = w["g_mix_pre"].shape[0]
    t = x.shape[0]
    saved = []
    i1, i2, i3 = 3 * FOX_WIDTH, 3 * FOX_WIDTH + FOX_HEADS, AB_IN
    ncol = 128

    def stacked_weights(layer, part, _):
        names = COMMON_BIG if part == "rest" else layer_big(layer)[len(COMMON_BIG):]
        return {n: w[n][layer if n in COMMON_BIG else layer // 2] for n in names}

    get_weights = layer_weights or stacked_weights
    for layer in range(depth):
        lw = dict(get_weights(layer, "mix", x))
        s = {"x0": x, "lw": lw}
        tag = f"l{layer}"
        h1 = rms_pre(f"{tag}_mix_pre", x, w["g_mix_pre"], layer)
        s["h1"] = h1
        if layer % 2 == 0:
            e = layer // 2
            w_in = jnp.pad(lw["ab_w_in"], ((0, 0), (0, AB_IN_PAD - AB_IN)))
            proj = mm(f"{tag}_ab_in", h1, w_in, "nn", F32)
            qkv = proj[:, :i1].astype(BF16).reshape(t, 3, FOX_HEADS, FOX_HEAD_DIM).transpose(1, 2, 0, 3)
            z_t = proj[:, i1:i2].T
            b_f = w["ab_b_f"][e].reshape(FOX_HEADS, 1)
            cum_t = fox_gates_fwd(f"{tag}_fox_gates", z_t, b_f)
            cum_r = cum_t[:, None, :]
            oh, lse = fox_fwd(f"{tag}_fox", qkv[0], qkv[1], qkv[2], cum_r)
            bcu = proj[:, i2:i3]
            y_b = sconv_fwd(f"{tag}_sconv", bcu, 0, w["ab_conv_w"][e])
            ycat = jnp.concatenate([_unheads(oh), y_b], axis=1)
            y1 = mm(f"{tag}_ab_out", ycat, lw["ab_w_out"], "nn", F32)
            s.update(w_in=w_in, qkv=qkv, z_t=z_t, b_f=b_f, cum_r=cum_r, lse=lse, bcu=bcu, ycat=ycat)
        else:
            o = layer // 2
            gu2 = mm(f"{tag}_c_in", h1, lw["c_w_in"], "nn", F32)
            conv_b = w["c_conv_b"][o].reshape(1, -1)
            uc = rg_conv_fwd(f"{tag}_rg_conv", gu2, w["c_conv_w"][o], conv_b)
            b_a, b_i = w["c_b_a"][o].reshape(1, -1), w["c_b_i"][o].reshape(1, -1)
            r, i = rg_gates_fwd(f"{tag}_rg_gates", uc, lw["c_w_a"], b_a, lw["c_w_i"], b_i)
            lam = w["c_lam"][o].reshape(1, -1)
            ymix, hs = rg_scan_fwd(f"{tag}_rg_scan", gu2, uc, r, i, lam)
            y1 = mm(f"{tag}_c_out", ymix, lw["c_w_out"], "nn", F32)
            s.update(gu2=gu2, uc=uc, r=r, i=i, lam=lam, hs=hs, ymix=ymix)
        s["y1"] = y1
        x = post_add(f"{tag}_mix_post", x, y1, w["g_mix_post"], layer)
        lw.update(get_weights(layer, "rest", x))
        s["x1"] = x
        h2 = rms_pre(f"{tag}_cross_pre", x, w["g_cross_pre"], layer)
        m = rms_pre(f"{tag}_mem_pre", mem, w["g_mem"], layer)
        q = mm(f"{tag}_xq", h2, lw["w_xq"], "nn", BF16)
        kv = mm(f"{tag}_xkv", m, lw["w_xkv"], "nn", BF16)
        o_att = xattn_fwd(f"{tag}_xattn", q, kv)
        y2 = mm(f"{tag}_xo", o_att, lw["w_xo"], "nn", F32)
        s.update(h2=h2, m=m, q=q, kv=kv, o_att=o_att, y2=y2)
        x = post_add(f"{tag}_cross_post", x, y2, w["g_cross_post"], layer)
        s["x2"] = x
        h3 = rms_pre(f"{tag}_ffn_pre", x, w["g_ffn_pre"], layer)
        gu = mm(f"{tag}_ffn_gu", h3, lw["w_ffn_gu"], "nn", BF16)
        act = swiglu_fwd(f"{tag}_swiglu", gu)
        y3 = mm(f"{tag}_ffn_down", act, lw["w_ffn_down"], "nn", F32)
        s.update(h3=h3, gu=gu, act=act, y3=y3)
        x = post_add(f"{tag}_ffn_post", x, y3, w["g_ffn_post"], layer)
        saved.append(s)

    dx, sq_cols = loss_head("loss_head", x, target)

    grads = {k: [None] * v.shape[0] for k, v in w.items() if k not in BIG}
    big, token = {}, None

    def dw(name, a, b, cols_cut=False):
        return mm(name, a, b, "tn", BF16, reduce_layout=True) if cols_cut else _row_cut(mm(name, a, b, "tn", BF16))

    for layer in reversed(range(depth)):
        s = saved[layer]
        lw = s["lw"]
        tag = f"b{layer}"
        lg = {}
        g_ffn_post = w["g_ffn_post"] if token is None else w["g_ffn_post"] + token
        dy3, grads["g_ffn_post"][layer] = post_bwd(f"{tag}_ffn_post", s["y3"], dx, g_ffn_post, layer)
        dact = mm(f"{tag}_ffn_down_dx", dy3, lw["w_ffn_down"], "nt", BF16)
        lg["w_ffn_down"] = dw(f"{tag}_ffn_down_dw", s["act"], dy3)
        dgu = swiglu_bwd(f"{tag}_swiglu", s["gu"], dact)
        dh3 = mm(f"{tag}_ffn_gu_dx", dgu, lw["w_ffn_gu"], "nt", F32)
        lg["w_ffn_gu"] = dw(f"{tag}_ffn_gu_dw", s["h3"], dgu, cols_cut=True)
        dx, grads["g_ffn_pre"][layer] = pre_bwd(f"{tag}_ffn_pre", s["x2"], dh3, dx, w["g_ffn_pre"], layer)
        dy2, grads["g_cross_post"][layer] = post_bwd(f"{tag}_cross_post", s["y2"], dx, w["g_cross_post"], layer)
        do = mm(f"{tag}_xo_dx", dy2, lw["w_xo"], "nt", BF16)
        lg["w_xo"] = dw(f"{tag}_xo_dw", s["o_att"], dy2)
        dq, dk, dv = xattn_bwd(f"{tag}_xattn", s["q"], s["kv"], do)
        dh2 = mm(f"{tag}_xq_dx", dq, lw["w_xq"], "nt", F32)
        lg["w_xq"] = dw(f"{tag}_xq_dw", s["h2"], dq)
        dkv = jnp.concatenate([dk, dv], axis=1).astype(BF16)
        dm = mm(f"{tag}_xkv_dx", dkv, lw["w_xkv"], "nt", F32)
        lg["w_xkv"] = dw(f"{tag}_xkv_dw", s["m"], dkv, cols_cut=True)
        grads["g_mem"][layer] = gain_bwd(f"{tag}_mem_pre", mem, dm)
        dx, grads["g_cross_pre"][layer] = pre_bwd(f"{tag}_cross_pre", s["x1"], dh2, dx, w["g_cross_pre"], layer)
        token = None if reduce_hook is None else reduce_hook(layer, "rest", lg)
        g_mix_post = w["g_mix_post"] if token is None else w["g_mix_post"] + token
        dy1, grads["g_mix_post"][layer] = post_bwd(f"{tag}_mix_post", s["y1"], dx, g_mix_post, layer)
        rest_grads, lg = lg, {}
        if layer % 2 == 0:
            e = layer // 2
            dycat = mm(f"{tag}_ab_out_dx", dy1, lw["ab_w_out"], "nt", F32)
            lg["ab_w_out"] = dw(f"{tag}_ab_out_dw", s["ycat"], dy1)
            do_h = _heads(dycat[:, :FOX_WIDTH].astype(BF16))
            qkv = s["qkv"]
            dqh, dkh, dvh, dcum = fox_bwd(f"{tag}_fox", qkv[0], qkv[1], qkv[2], do_h, s["lse"], s["cum_r"])
            dz_t, db_f = fox_gates_bwd(f"{tag}_fox_gates", s["z_t"], s["b_f"], dcum.reshape(FOX_HEADS, t))
            grads["ab_b_f"][e] = db_f.reshape(FOX_HEADS)
            db, dc, du, dconv_w = sconv_bwd(f"{tag}_sconv", s["bcu"], 0, w["ab_conv_w"][e], dycat, FOX_WIDTH // ncol)
            grads["ab_conv_w"][e] = dconv_w
            dproj = jnp.concatenate(
                [_unheads(dqh), _unheads(dkh).astype(BF16), _unheads(dvh).astype(BF16), dz_t.T.astype(BF16), db, dc, du,
                 jnp.zeros((t, AB_IN_PAD - AB_IN), BF16)], axis=1)
            dh1 = mm(f"{tag}_ab_in_dx", dproj, s["w_in"], "nt", F32)
            dw_in = mm(f"{tag}_ab_in_dw", s["h1"], dproj, "tn", F32)[:, :AB_IN]
            lg["ab_w_in"] = dw_in.reshape(2, D_MODEL // 2, N_CHIPS, AB_IN // N_CHIPS).transpose(2, 0, 1, 3).astype(BF16)
        else:
            o = layer // 2
            dymix = mm(f"{tag}_c_out_dx", dy1, lw["c_w_out"], "nt", F32)
            lg["c_w_out"] = dw(f"{tag}_c_out_dw", s["ymix"], dy1)
            dgate, dzr, dzi, duc_part, db_a, db_i, dlam = rg_scan_bwd(
                f"{tag}_rg_scan", s["gu2"], s["uc"], s["r"], s["i"], s["hs"], dymix, s["lam"])
            duc, dw_a, dw_i = rg_gates_bwd(f"{tag}_rg_gates", s["uc"], dzr, dzi, duc_part, lw["c_w_a"], lw["c_w_i"])
            lg["c_w_a"] = dw_a.reshape(N_CHIPS, 2, LRU_BW // 2, LRU_BW)
            lg["c_w_i"] = dw_i.reshape(N_CHIPS, 2, LRU_BW // 2, LRU_BW)
            du_raw, dconv_w, dconv_b = rg_conv_bwd(f"{tag}_rg_conv", s["gu2"], duc, w["c_conv_w"][o])
            grads["c_b_a"][o] = db_a.reshape(LRU_BLOCKS, LRU_BW)
            grads["c_b_i"][o] = db_i.reshape(LRU_BLOCKS, LRU_BW)
            grads["c_lam"][o] = dlam.reshape(-1)
            grads["c_conv_w"][o] = dconv_w
            grads["c_conv_b"][o] = dconv_b.reshape(-1)
            dgu2 = jnp.concatenate([dgate, du_raw], axis=1)
            dh1 = mm(f"{tag}_c_in_dx", dgu2, lw["c_w_in"], "nt", F32)
            lg["c_w_in"] = dw(f"{tag}_c_in_dw", s["h1"], dgu2, cols_cut=True)
        dx, grads["g_mix_pre"][layer] = pre_bwd(f"{tag}_mix_pre", s["x0"], dh1, dx, w["g_mix_pre"], layer)
        if reduce_hook is None:
            big[layer] = {**rest_grads, **lg}
        else:
            token = reduce_hook(layer, "mix", lg)

    for k in list(grads):
        if k.startswith("g_"):
            grads[k] = [g.reshape(-1) for g in grads[k]]
        grads[k] = jnp.stack(grads[k])
    return sq_cols, dx, grads, big


CHIP_FLIPS = ((1, 0), (0, 1), (1, 1))
HBM_SPEC = pl.BlockSpec(memory_space=pltpu.HBM)
VMEM_SPEC = pl.BlockSpec(memory_space=pltpu.VMEM)


def _place():
    return lax.axis_index("x"), lax.axis_index("y"), lax.axis_index("c")


def _flip(v, f):
    return 1 - v if f else v


def _remote(src, dst, send_sem, recv_sem, target):
    return pltpu.make_async_remote_copy(src_ref=src, dst_ref=dst, send_sem=send_sem, recv_sem=recv_sem,
                                        device_id=target, device_id_type=MESH)


def _comm_call(name, body, ins, out_shape, n_sems):
    n = len(ins)
    return pl.pallas_call(
        body, name=name, in_specs=[HBM_SPEC] * n, out_specs=[HBM_SPEC] * len(out_shape), out_shape=out_shape,
        scratch_shapes=[pltpu.SemaphoreType.DMA((n, n_sems)), pltpu.SemaphoreType.DMA((n, n_sems))],
    )(*ins)


def swap_other_half(name, arrays):
    n = len(arrays)

    def body(*refs):
        ins, outs = refs[:n], refs[n:2 * n]
        send_sems, recv_sems = refs[2 * n:]
        x, y, c = _place()
        cps = [_remote(ins[a].at[:, 1 - c], outs[a], send_sems.at[a, 0], recv_sems.at[a, 0], (x, y, 1 - c))
               for a in range(n)]
        for cp in cps:
            cp.start()
        for cp in cps:
            cp.wait()

    out_shape = [jax.ShapeDtypeStruct((b.shape[0], *b.shape[2:]), b.dtype) for b in arrays]
    return _comm_call(name, body, arrays, out_shape, 1)


SEM_SPEC = pl.BlockSpec(memory_space=pltpu.SEMAPHORE)


def _exchange_copies(srcs, lands, send_sems, recv_sems):
    x, y, c = _place()
    p = 2 * x + y
    cps = []
    for a, (src, land) in enumerate(zip(srcs, lands)):
        for k, (fx, fy) in enumerate(CHIP_FLIPS):
            qx, qy = _flip(x, fx), _flip(y, fy)
            sem = len(CHIP_FLIPS) * a + k
            cps.append(_remote(src.at[2 * qx + qy], land.at[p], send_sems.at[sem], recv_sems.at[sem], (qx, qy, c)))
    return cps


def _gather_copies(srcs, lands, send_sems, recv_sems):
    x, y, c = _place()
    p = 2 * x + y
    cps = []
    for a, (src, land) in enumerate(zip(srcs, lands)):
        for k, (fx, fy) in enumerate(CHIP_FLIPS):
            sem = len(CHIP_FLIPS) * a + k
            cps.append(_remote(src.at[c], land.at[p, c], send_sems.at[sem], recv_sems.at[sem],
                               (_flip(x, fx), _flip(y, fy), c)))
    return cps


def copies_start(name, make_copies, srcs, land_shapes):
    n = len(srcs)

    def body(*refs):
        for cp in make_copies(refs[:n], refs[n:2 * n], refs[2 * n], refs[2 * n + 1]):
            cp.start()
        refs[-1][...] = jnp.zeros_like(refs[-1])

    thru = [pltpu.HBM(b.shape, b.dtype) for b in srcs] + [pltpu.HBM(sh, b.dtype) for sh, b in zip(land_shapes, srcs)]
    outs = pl.pallas_call(
        body, name=name, in_specs=[HBM_SPEC] * (2 * n),
        out_shape=(pltpu.SemaphoreType.DMA((3 * n,)), pltpu.SemaphoreType.DMA((3 * n,)), *thru,
                   jax.ShapeDtypeStruct((8, 128), F32)),
        out_specs=(SEM_SPEC, SEM_SPEC, *[HBM_SPEC] * (2 * n), VMEM_SPEC),
        input_output_aliases={i: 2 + i for i in range(2 * n)},
        compiler_params=pltpu.CompilerParams(has_side_effects=pltpu.SideEffectType.DATAFLOW_SIDE_EFFECTING),
    )(*[pltpu.with_memory_space_constraint(b, pltpu.HBM) for b in srcs],
      *[pltpu.with_memory_space_constraint(lax.empty(sh, b.dtype), pltpu.HBM) for sh, b in zip(land_shapes, srcs)])
    return outs[:-1], outs[-1]


def copies_wait(name, make_copies, state, after):
    send_sems, recv_sems, *thru = state
    n = len(thru) // 2

    def body(*refs):
        for cp in make_copies(refs[:n], refs[n:2 * n], refs[2 * n], refs[2 * n + 1]):
            cp.wait_send()
            cp.wait_recv()

    outs = pl.pallas_call(
        body, name=name, in_specs=[HBM_SPEC] * (2 * n) + [SEM_SPEC, SEM_SPEC, pl.BlockSpec(memory_space=pl.ANY)],
        out_shape=tuple(pltpu.HBM(t.shape, t.dtype) for t in thru), out_specs=tuple([HBM_SPEC] * (2 * n)),
        input_output_aliases={i: i for i in range(2 * n)},
        compiler_params=pltpu.CompilerParams(has_side_effects=pltpu.SideEffectType.DATAFLOW_SIDE_EFFECTING),
    )(*thru, send_sems, recv_sems, after)
    return outs[n:]


def pass_to_sibling(name, shards, lands):
    n = len(lands)

    def body(*refs):
        own, ins, outs = refs[:n], refs[n:2 * n], refs[2 * n:3 * n]
        send_sems, recv_sems = refs[3 * n:]
        x, y, c = _place()
        sibling = (x, y, 1 - c)
        cps = []
        for a in range(n):
            for k, (fx, fy) in enumerate(CHIP_FLIPS):
                q = 2 * _flip(x, fx) + _flip(y, fy)
                cps.append(_remote(ins[a].at[q, c], outs[a].at[q, c], send_sems.at[a, k], recv_sems.at[a, k], sibling))
            cps.append(_remote(own[a], outs[a].at[2 * x + y], send_sems.at[a, 3], recv_sems.at[a, 3], sibling))
        for cp in cps:
            cp.start()
        for cp in cps:
            cp.wait()

    return pl.pallas_call(
        body, name=name, in_specs=[HBM_SPEC] * (2 * n), out_specs=[HBM_SPEC] * n,
        out_shape=[jax.ShapeDtypeStruct(b.shape, b.dtype) for b in lands],
        scratch_shapes=[pltpu.SemaphoreType.DMA((n, 4)), pltpu.SemaphoreType.DMA((n, 4))],
        input_output_aliases={n + i: i for i in range(n)},
    )(*shards, *lands)


def share_halves(bufs):
    n = len(bufs)

    def body(*refs):
        ins, outs = refs[:n], refs[n:2 * n]
        send_sems, recv_sems = refs[2 * n:]
        x, y, c = _place()
        cps = [_remote(ins[a].at[:, c], outs[a].at[:, c], send_sems.at[a], recv_sems.at[a], (x, y, 1 - c))
               for a in range(n)]
        for cp in cps:
            cp.start()
        for cp in cps:
            cp.wait()

    return pl.pallas_call(
        body, name="share_reduced_halves", in_specs=[HBM_SPEC] * n, out_specs=[HBM_SPEC] * n,
        out_shape=[jax.ShapeDtypeStruct(b.shape, b.dtype) for b in bufs],
        scratch_shapes=[pltpu.SemaphoreType.DMA((n,)), pltpu.SemaphoreType.DMA((n,))],
        input_output_aliases={i: i for i in range(n)},
    )(*bufs)


DEVICE_FLIPS = tuple((fx, fy, fc) for fx in (0, 1) for fy in (0, 1) for fc in (0, 1))[1:]


def gather_small(name, v, reduce):
    r, cdim = v.shape
    n_dev = 8

    def body(v_ref, out_ref, *scratch):
        buf = scratch[0] if reduce else out_ref
        send_sems, recv_sems = scratch[-2:]
        x, y, c = _place()
        me = 4 * x + 2 * y + c
        buf[me] = v_ref[...]
        cps = []
        for k, (fx, fy, fc) in enumerate(DEVICE_FLIPS):
            cps.append(_remote(v_ref, buf.at[me], send_sems.at[k], recv_sems.at[k],
                               (_flip(x, fx), _flip(y, fy), _flip(c, fc))))
        for cp in cps:
            cp.start()
        for cp in cps:
            cp.wait()
        if reduce:
            total = buf[0]
            for d in range(1, n_dev):
                total = total + buf[d]
            out_ref[...] = total

    scratch = [pltpu.SemaphoreType.DMA((7,)), pltpu.SemaphoreType.DMA((7,))]
    if reduce:
        scratch = [pltpu.VMEM((n_dev, r, cdim), F32)] + scratch
    out_shape = jax.ShapeDtypeStruct((r, cdim) if reduce else (n_dev, r, cdim), F32)
    return pl.pallas_call(body, name=name, in_specs=[VMEM_SPEC], out_specs=VMEM_SPEC, out_shape=out_shape,
                          scratch_shapes=scratch)(v)


def pair_sum(name, own, got, core):
    _, hx, cols = got.shape
    tr = _tile(hx, (256, 128, 64, 32, 16))

    def kern(core_ref, a_ref, b_ref, o_ref):
        o_ref[...] = (a_ref[...].astype(F32) + b_ref[...].astype(F32)).astype(BF16)

    grid_spec = pltpu.PrefetchScalarGridSpec(
        num_scalar_prefetch=1, grid=(hx // tr,),
        in_specs=[pl.BlockSpec((N_CHIPS, None, tr, cols), lambda i, cr: (0, cr[0], i, 0)),
                  pl.BlockSpec((N_CHIPS, tr, cols), lambda i, cr: (0, i, 0))],
        out_specs=pl.BlockSpec((N_CHIPS, tr, cols), lambda i, cr: (0, i, 0)))
    return pl.pallas_call(
        kern, name=name, grid_spec=grid_spec, out_shape=jax.ShapeDtypeStruct(got.shape, BF16),
        compiler_params=pltpu.CompilerParams(dimension_semantics=("parallel",), vmem_limit_bytes=VMEM_LIMIT_BYTES),
    )(core, own, got)


def chip_sum(name, mine, parts, place, buf, layer):
    _, hx, yd = parts.shape
    tr = _tile(hx, (256, 128, 64, 32, 16))

    def kern(place_ref, m_ref, p_ref, _, o_ref):
        total = None
        for q in range(N_CHIPS):
            term = jnp.where(place_ref[0] == q, m_ref[...], p_ref[q]).astype(F32)
            total = term if total is None else total + term
        o_ref[...] = total

    grid_spec = pltpu.PrefetchScalarGridSpec(
        num_scalar_prefetch=1, grid=(hx // tr,),
        in_specs=[pl.BlockSpec((None, tr, yd), lambda i, pr: (pr[0], i, 0)),
                  pl.BlockSpec((N_CHIPS, tr, yd), lambda i, pr: (0, i, 0)),
                  pl.BlockSpec(memory_space=pl.ANY)],
        out_specs=pl.BlockSpec((None, None, tr, yd), lambda i, pr: (layer, pr[1], i, 0)))
    return pl.pallas_call(
        kern, name=name, grid_spec=grid_spec, out_shape=jax.ShapeDtypeStruct(buf.shape, buf.dtype),
        input_output_aliases={3: 0},
        compiler_params=pltpu.CompilerParams(dimension_semantics=("parallel",), vmem_limit_bytes=VMEM_LIMIT_BYTES),
    )(place, mine, parts, buf)


WEIGHTS = ("g_mix_pre", "g_mix_post", "g_cross_pre", "g_mem", "g_cross_post", "g_ffn_pre", "g_ffn_post", "w_xq", "w_xkv",
           "w_xo", "w_ffn_gu", "w_ffn_down", "ab_w_in", "ab_b_f", "ab_conv_w", "ab_w_out", "c_w_in", "c_conv_w",
           "c_conv_b", "c_w_a", "c_b_a", "c_w_i", "c_b_i", "c_lam", "c_w_out")
SHARD_DIM = {"w_xq": 1, "w_xkv": 2, "w_xo": 1, "w_ffn_gu": 2, "w_ffn_down": 1, "ab_w_in": 2, "ab_conv_w": 2,
             "ab_w_out": 1, "c_w_in": 2, "c_conv_w": 2, "c_conv_b": 1, "c_w_a": 2, "c_b_a": 2, "c_w_i": 2, "c_b_i": 2,
             "c_lam": 1, "c_w_out": 1}
COMMON_BIG = ("w_xq", "w_xkv", "w_xo", "w_ffn_gu", "w_ffn_down")
EVEN_BIG, ODD_BIG = ("ab_w_in", "ab_w_out"), ("c_w_in", "c_w_a", "c_w_i", "c_w_out")
BIG = COMMON_BIG + EVEN_BIG + ODD_BIG


def layer_big(layer):
    return COMMON_BIG + (ODD_BIG if layer % 2 else EVEN_BIG)


SPLIT_LAYERS = (0,)


def chunk_names(layer, part):
    mixer = layer_big(layer)[len(COMMON_BIG):]
    if layer in SPLIT_LAYERS:
        return mixer if part == "mix" else COMMON_BIG
    return layer_big(layer) if part == "mix" else ()


SMALL_SHARDED = ("ab_conv_w", "c_conv_w", "c_conv_b", "c_b_a", "c_b_i", "c_lam")
REPLICATED = ("g_mix_pre", "g_mix_post", "g_cross_pre", "g_mem", "g_cross_post", "g_ffn_pre", "g_ffn_post", "ab_b_f")
PACK_COLS = 1024


def _unshard(g, d):
    shard = g.shape[1:]
    return jnp.moveaxis(g, 0, d).reshape(shard[:d] + (N_CHIPS * shard[d],) + shard[d + 1:])


def _shardify(full, d):
    s = full.shape
    return jnp.moveaxis(full.reshape(s[:d] + (N_CHIPS, s[d] // N_CHIPS) + s[d + 1:]), d, 0)


def _pack(arrays, rows):
    flat = jnp.concatenate([a.reshape(-1).astype(F32) for a in arrays])
    return jnp.pad(flat, (0, rows * PACK_COLS - flat.shape[0])).reshape(rows, PACK_COLS)


def _unpack(packed, shapes):
    flat = packed.reshape(-1)
    out, at = [], 0
    for s in shapes:
        size = math.prod(s)
        out.append(flat[at:at + size].reshape(s))
        at += size
    return out


def _rows_for(shapes):
    return -(-sum(math.prod(s) for s in shapes) // (8 * PACK_COLS)) * 8


def kernel(x, mem, g_mix_pre, g_mix_post, g_cross_pre, g_mem, g_cross_post, g_ffn_pre, g_ffn_post, w_xq, w_xkv, w_xo, w_ffn_gu, w_ffn_down, ab_w_in, ab_b_f, ab_conv_w, ab_w_out, c_w_in, c_conv_w, c_conv_b, c_w_a, c_b_a, c_w_i, c_b_i, c_lam, c_w_out, loss_target, m_g_mix_pre, m_g_mix_post, m_g_cross_pre, m_g_mem, m_g_cross_post, m_g_ffn_pre, m_g_ffn_post, m_w_xq, m_w_xkv, m_w_xo, m_w_ffn_gu, m_w_ffn_down, m_ab_w_in, m_ab_b_f, m_ab_conv_w, m_ab_w_out, m_c_w_in, m_c_conv_w, m_c_conv_b, m_c_w_a, m_c_b_a, m_c_w_i, m_c_b_i, m_c_lam, m_c_w_out, v_g_mix_pre, v_g_mix_post, v_g_cross_pre, v_g_mem, v_g_cross_post, v_g_ffn_pre, v_g_ffn_post, v_w_xq, v_w_xkv, v_w_xo, v_w_ffn_gu, v_w_ffn_down, v_ab_w_in, v_ab_b_f, v_ab_conv_w, v_ab_w_out, v_c_w_in, v_c_conv_w, v_c_conv_b, v_c_w_a, v_c_b_a, v_c_w_i, v_c_b_i, v_c_lam, v_c_w_out):
    given = dict(locals())
    w = {n: given[n] for n in WEIGHTS}
    m_in = {n: given["m_" + n] for n in WEIGHTS}
    v_in = {n: given["v_" + n] for n in WEIGHTS}
    xi, yi, ci = _place()
    chip = 2 * xi + yi

    full = {}
    small_shapes = [w[n].shape for n in SMALL_SHARDED]
    rows_w = _rows_for(small_shapes)
    assert rows_w * PACK_COLS > sum(math.prod(s) for s in small_shapes)
    every = gather_small("gather_small_weights", _pack([w[n] for n in SMALL_SHARDED], rows_w), reduce=False)
    per_chip = every[0::2].reshape(N_CHIPS, -1)
    at = 0
    for n, s in zip(SMALL_SHARDED, small_shapes):
        size = math.prod(s)
        full[n] = _unshard(per_chip[:, at:at + size].reshape(N_CHIPS, *s), SHARD_DIM[n])
        at += size
    for n in REPLICATED:
        full[n] = w[n]
    after_small = every[0, -1, -1].astype(BF16)

    depth = g_mix_pre.shape[0]
    own, gathers, tokens = {}, {}, []
    for layer in range(depth):
        for part in ("mix", "rest"):
            names = chunk_names(layer, part)
            if names:
                tagp = f"l{layer}_{part}"
                own[tagp] = {n: w[n][layer if n in COMMON_BIG else layer // 2].astype(BF16) + after_small for n in names}
                halves = [a.reshape(2, -1, a.shape[-1]) for a in own[tagp].values()]
                gathers[tagp], token = copies_start(f"gather_start_{tagp}", _gather_copies, halves,
                                                    [(N_CHIPS, *h.shape) for h in halves])
                tokens.append(token[0, 0])

    def layer_weights(layer, part, x_in):
        tagp = f"l{layer}_{part}"
        if tagp not in gathers:
            return {}
        lands = copies_wait(f"gather_wait_{tagp}", _gather_copies, gathers[tagp], x_in)
        lands = pass_to_sibling(f"gather_pass_{tagp}", [a.reshape(2, -1, a.shape[-1]) for a in own[tagp].values()], lands)
        return {n: _unshard(g.reshape(N_CHIPS, *mine.shape), SHARD_DIM[n] - 1)
                for (n, mine), g in zip(own[tagp].items(), lands)}

    core_arr = ci.reshape(1).astype(jnp.int32)
    place_arr = jnp.stack([chip, ci]).astype(jnp.int32)
    in_flight, held = [], {}

    def reduce_hook(layer, part, part_grads):
        held.update(part_grads)
        names = chunk_names(layer, part)
        if not names:
            return None
        tagp = f"l{layer}_{part}"
        mine = [held.pop(n) for n in names]
        got = swap_other_half(f"swap_grads_{tagp}", mine)
        sums = [pair_sum(f"pair_sum_{tagp}_{n}", o, g, core_arr) for n, o, g in zip(names, mine, got)]
        state, token = copies_start(f"exchange_start_{tagp}", _exchange_copies, sums, [b.shape for b in sums])
        in_flight.append((layer, tagp, names, sums, state))
        return token[0, 0]

    sq_cols, dx, grads, _ = local_step(x[0] + sum(tokens), mem[0], loss_target[0], full, layer_weights, reduce_hook)
    loss = lax.psum(0.5 / D_MODEL * jnp.sum(sq_cols), ("x", "y", "c"))

    reduced = {n: lax.empty((w[n].shape[0], 2, math.prod(w[n].shape[1:-1]) // 2, w[n].shape[-1]), F32) for n in BIG}
    for layer, tagp, names, sums, state in in_flight:
        parts = copies_wait(f"exchange_wait_{tagp}", _exchange_copies, state, dx)
        for n, mine, p in zip(names, sums, parts):
            index = layer if n in COMMON_BIG else layer // 2
            reduced[n] = chip_sum(f"chip_sum_{tagp}_{n}", mine, p, place_arr, reduced[n], index)
    grad_out = {n: g.reshape(w[n].shape) for n, g in zip(BIG, share_halves([reduced[n] for n in BIG]))}

    small_names = REPLICATED + SMALL_SHARDED
    small_full_shapes = [grads[n].shape for n in small_names]
    total = gather_small("reduce_small_grads", _pack([grads[n] for n in small_names], _rows_for(small_full_shapes)),
                         reduce=True)
    for n, g in zip(small_names, _unpack(total, small_full_shapes)):
        if n in SHARD_DIM:
            g = lax.dynamic_index_in_dim(_shardify(g, SHARD_DIM[n]), chip, axis=0, keepdims=False)
        grad_out[n] = g

    delta, new_m, new_v = {}, {}, {}
    for n in BIG:
        two_d = lambda a: a.reshape(-1, a.shape[-1])
        d, m2, v2 = adamw(f"adamw_{n}", two_d(w[n]), two_d(grad_out[n]), two_d(m_in[n]), two_d(v_in[n]))
        delta[n], new_m[n], new_v[n] = (a.reshape(w[n].shape) for a in (d, m2, v2))
    shapes = [w[n].shape for n in small_names]
    rows = _rows_for(shapes)
    packed = [_pack([src[n] for n in small_names], rows) for src in (w, grad_out, m_in, v_in)]
    for dst, res in zip((delta, new_m, new_v), adamw("adamw_small", *packed)):
        for n, a in zip(small_names, _unpack(res, shapes)):
            dst[n] = a

    return (loss, dx[None], *[grad_out[n] for n in WEIGHTS], *[delta[n] for n in WEIGHTS],
            *[new_m[n] for n in WEIGHTS], *[new_v[n] for n in WEIGHTS])
```

```python
import functools
import math

import jax
import jax.numpy as jnp
from jax import lax
from jax.experimental import pallas as pl
from jax.experimental.pallas import tpu as pltpu

F32, BF16 = jnp.float32, jnp.bfloat16
D_MODEL = 1024
EPS = 1e-6
NEG_INF = -1e30
FOX_HEADS, FOX_HEAD_DIM, FOX_WIDTH = 8, 64, 512
SC_WIDTH = 512
AB_IN = 3 * FOX_WIDTH + FOX_HEADS + 3 * SC_WIDTH
AB_IN_PAD = 3200
LRU_BW, LRU_BLOCKS = 256, 4
RG_C = 8.0
MEM_HEADS, MEM_HEAD_DIM = 4, 256
ADAM_LR, ADAM_B1, ADAM_B2, ADAM_EPS, ADAM_WD, ADAM_STEP = 0.001, 0.9, 0.999, 1e-08, 0.01, 10
N_CHIPS = 4
MESH = pl.DeviceIdType.MESH
VMEM_LIMIT_BYTES = 48 * 1024 * 1024
MM_OPERAND_TILE_BYTES = 7 * 1024 * 1024

NN = (((1,), (0,)), ((), ()))
NT = (((1,), (1,)), ((), ()))
TN = (((0,), (0,)), ((), ()))


def _dot(a, b, dn=NN):
    return lax.dot_general(a.astype(BF16), b.astype(BF16), dn, preferred_element_type=F32)


def _tile(n, prefs):
    for p in prefs:
        if n % p == 0:
            return p
    return n


def _pcall(name, kern, grid, ins, in_specs, out_shape, out_specs, sem):
    return pl.pallas_call(
        kern, name=name, grid=grid, in_specs=in_specs, out_specs=out_specs, out_shape=out_shape,
        compiler_params=pltpu.CompilerParams(dimension_semantics=sem, vmem_limit_bytes=VMEM_LIMIT_BYTES),
    )(*ins)


def mm(name, a, b, mode, out_dtype, reduce_layout=False):
    if mode == "nn":
        (m, k), n = a.shape, b.shape[1]
    elif mode == "nt":
        (m, k), n = a.shape, b.shape[0]
    else:
        (k, m), n = a.shape, b.shape[1]
    if reduce_layout:
        tm, tn = m // 2, n // N_CHIPS
    else:
        tn = _tile(n, ((1024,) if mode == "tn" else ()) + (512, 640, 256, 128))
        tm = next(c for c in (2048, 1024, 512, 256, 128, m)
                  if m % c == 0 and 2 * c * k <= MM_OPERAND_TILE_BYTES and 4 * c * tn <= MM_OPERAND_TILE_BYTES)
    dn = {"nn": NN, "nt": NT, "tn": TN}[mode]

    def kern(a_ref, b_ref, o_ref):
        o_ref[...] = _dot(a_ref[...], b_ref[...], dn).astype(o_ref.dtype)

    a_spec = pl.BlockSpec((k, tm), lambda i, j: (0, i)) if mode == "tn" else pl.BlockSpec((tm, k), lambda i, j: (i, 0))
    b_spec = pl.BlockSpec((tn, k), lambda i, j: (j, 0)) if mode == "nt" else pl.BlockSpec((k, tn), lambda i, j: (0, j))
    if reduce_layout:
        out_shape = jax.ShapeDtypeStruct((N_CHIPS, 2, tm, tn), out_dtype)
        o_spec = pl.BlockSpec((None, None, tm, tn), lambda i, j: (j, i, 0, 0))
    else:
        out_shape = jax.ShapeDtypeStruct((m, n), out_dtype)
        o_spec = pl.BlockSpec((tm, tn), lambda i, j: (i, j))
    return _pcall(name, kern, (m // tm, n // tn), (a, b), [a_spec, b_spec], out_shape, o_spec, ("parallel", "parallel"))


def rowwise(name, body, rows, params, outs, accs=(), tr=256):
    t = rows[0].shape[0]
    tr = min(tr, t)
    nr, npar, no = len(rows), len(params), len(outs)

    def kern(*refs):
        acc_refs = refs[nr + npar + no:]
        if acc_refs:
            @pl.when(pl.program_id(0) == 0)
            def _():
                for ar in acc_refs:
                    ar[...] = jnp.zeros_like(ar)
        body(refs[:nr], refs[nr:nr + npar], refs[nr + npar:nr + npar + no], acc_refs)

    in_specs = [pl.BlockSpec((tr, x.shape[1]), lambda i: (i, 0)) for x in rows]
    in_specs += [pl.BlockSpec(p.shape, lambda i: (0, 0)) for p in params]
    out_specs = [pl.BlockSpec((tr, c), lambda i: (i, 0)) for c, _ in outs]
    out_specs += [pl.BlockSpec(s, lambda i: (0, 0)) for s in accs]
    out_shape = [jax.ShapeDtypeStruct((t, c), dt) for c, dt in outs]
    out_shape += [jax.ShapeDtypeStruct(s, F32) for s in accs]
    return _pcall(name, kern, (t // tr,), (*rows, *params), in_specs, out_shape, out_specs,
                  ("arbitrary",) if accs else ("parallel",))


def _rms_stats(x):
    r = lax.rsqrt(jnp.mean(x * x, axis=-1, keepdims=True) + EPS)
    return r, x * r


def _rms_bwd(xh, r, g, dy):
    dxh = dy * g
    dx = r * (dxh - xh * jnp.mean(dxh * xh, axis=-1, keepdims=True))
    return dx, jnp.sum(dy * xh, axis=0, keepdims=True)


def rms_pre(name, x, gains, layer):
    def body(r, p, o, a):
        _, xh = _rms_stats(r[0][...])
        o[0][...] = (xh * p[0][layer:layer + 1, :]).astype(BF16)
    return rowwise(name, body, [x], [gains], [(x.shape[1], BF16)])[0]


def post_add(name, x, y, gains, layer):
    def body(r, p, o, a):
        _, yh = _rms_stats(r[1][...])
        o[0][...] = r[0][...] + yh * p[0][layer:layer + 1, :]
    return rowwise(name, body, [x, y], [gains], [(x.shape[1], F32)])[0]


def post_bwd(name, y, dx, gains, layer):
    def body(r, p, o, a):
        rr, yh = _rms_stats(r[0][...])
        dy, dg = _rms_bwd(yh, rr, p[0][layer:layer + 1, :], r[1][...])
        o[0][...] = dy.astype(BF16)
        a[0][...] += dg
    c = y.shape[1]
    return rowwise(name, body, [y, dx], [gains], [(c, BF16)], [(1, c)])


def pre_bwd(name, x, dh, dx_res, gains, layer):
    def body(r, p, o, a):
        rr, xh = _rms_stats(r[0][...])
        dx, dg = _rms_bwd(xh, rr, p[0][layer:layer + 1, :], r[1][...])
        o[0][...] = r[2][...] + dx
        a[0][...] += dg
    c = x.shape[1]
    return rowwise(name, body, [x, dh, dx_res], [gains], [(c, F32)], [(1, c)])


def gain_bwd(name, x, dh):
    def body(r, p, o, a):
        _, xh = _rms_stats(r[0][...])
        a[0][...] += jnp.sum(r[1][...] * xh, axis=0, keepdims=True)
    return rowwise(name, body, [x, dh], [], [], [(1, x.shape[1])])[0]


def _sigmoid(z):
    return 1.0 / (1.0 + jnp.exp(-z))


def swiglu_fwd(name, gu):
    f = gu.shape[1] // 2

    def body(r, p, o, a):
        g = r[0][:, :f].astype(F32)
        u = r[0][:, f:].astype(F32)
        o[0][...] = (g * _sigmoid(g) * u).astype(BF16)
    return rowwise(name, body, [gu], [], [(f, BF16)])[0]


def swiglu_bwd(name, gu, da):
    f = gu.shape[1] // 2

    def body(r, p, o, a):
        g = r[0][:, :f].astype(F32)
        u = r[0][:, f:].astype(F32)
        d = r[1][...].astype(F32)
        sg = _sigmoid(g)
        o[0][:, :f] = (d * u * sg * (1.0 + g * (1.0 - sg))).astype(BF16)
        o[0][:, f:] = (d * g * sg).astype(BF16)
    return rowwise(name, body, [gu, da], [], [(2 * f, BF16)])[0]


def loss_head(name, y, target):
    c = y.shape[1]

    def body(r, p, o, a):
        e = r[0][...] - r[1][...]
        o[0][...] = e * (1.0 / c)
        a[0][...] += jnp.sum(e * e, axis=0, keepdims=True)
    return rowwise(name, body, [y, target], [], [(c, F32)], [(1, c)])


def adamw(name, w, g, m, v):
    c = w.shape[1]

    def body(r, p, o, a):
        wv, gv, mv, vv = (x[...] for x in r)
        m2 = ADAM_B1 * mv + (1.0 - ADAM_B1) * gv
        v2 = ADAM_B2 * vv + (1.0 - ADAM_B2) * (gv * gv)
        m_hat = m2 / (1.0 - ADAM_B1 ** ADAM_STEP)
        v_hat = v2 / (1.0 - ADAM_B2 ** ADAM_STEP)
        o[0][...] = -ADAM_LR * (m_hat / (jnp.sqrt(v_hat) + ADAM_EPS) + ADAM_WD * wv)
        o[1][...] = m2
        o[2][...] = v2
    tr = _tile(w.shape[0], (256, 128, 64, 32, 16, 8))
    return rowwise(name, body, [w, g, m, v], [], [(c, F32)] * 3, tr=tr)


def colwise(name, body, cols, params, outs, pouts=(), tc=128):
    t = cols[0][0].shape[0]
    c = params[0].shape[1] if params else cols[0][0].shape[1]
    nc, npar, no = len(cols), len(params), len(outs)

    def kern(*refs):
        body(refs[:nc], refs[nc:nc + npar], refs[nc + npar:nc + npar + no], refs[nc + npar + no:])

    in_specs = [pl.BlockSpec((t, tc), functools.partial(lambda j, off: (0, j + off), off=off)) for _, off in cols]
    in_specs += [pl.BlockSpec((p.shape[0], tc), lambda j: (0, j)) for p in params]
    out_specs = [pl.BlockSpec((t, tc), lambda j: (0, j)) for _ in outs]
    out_specs += [pl.BlockSpec((r, tc), lambda j: (0, j)) for r in pouts]
    out_shape = [jax.ShapeDtypeStruct((t, c), dt) for dt in outs]
    out_shape += [jax.ShapeDtypeStruct((r, c), F32) for r in pouts]
    return _pcall(name, kern, (c // tc,), (*[x for x, _ in cols], *params), in_specs, out_shape, out_specs,
                  ("parallel",))


def _row_index(shape):
    return lax.broadcasted_iota(jnp.int32, shape, 0)


def _shift_down(x, d, rows):
    return jnp.where(rows >= d, pltpu.roll(x, d, 0), 0.0)


def _shift_up(x, d, rows):
    t = x.shape[0]
    return jnp.where(rows < t - d, pltpu.roll(x, t - d, 0), 0.0)


def sconv_fwd(name, proj, col0, conv_w, tc=128):
    nb = SC_WIDTH // tc

    def body(cl, p, o, po):
        b, c, u = (x[...] for x in cl)
        rows = _row_index(b.shape)
        w = p[0][...]
        z = c * u
        conv = w[2:3] * z + w[1:2] * _shift_down(z, 1, rows) + w[0:1] * _shift_down(z, 2, rows)
        o[0][...] = (b * conv).astype(BF16)
    return colwise(name, body, [(proj, col0), (proj, col0 + nb), (proj, col0 + 2 * nb)], [conv_w], [BF16], tc=tc)[0]


def sconv_bwd(name, proj, col0, conv_w, dyb, dcol0, tc=128):
    nb = SC_WIDTH // tc

    def body(cl, p, o, po):
        b, c, u, dy = (x[...] for x in cl)
        rows = _row_index(b.shape)
        w = p[0][...]
        z = c * u
        z1, z2 = _shift_down(z, 1, rows), _shift_down(z, 2, rows)
        conv = w[2:3] * z + w[1:2] * z1 + w[0:1] * z2
        dconv = dy * b
        dz = w[2:3] * dconv + w[1:2] * _shift_up(dconv, 1, rows) + w[0:1] * _shift_up(dconv, 2, rows)
        o[0][...] = (dy * conv).astype(BF16)
        o[1][...] = (dz * u).astype(BF16)
        o[2][...] = (dz * c).astype(BF16)
        po[0][0:1, :] = jnp.sum(dconv * z2, axis=0, keepdims=True)
        po[0][1:2, :] = jnp.sum(dconv * z1, axis=0, keepdims=True)
        po[0][2:3, :] = jnp.sum(dconv * z, axis=0, keepdims=True)
    return colwise(name, body, [(proj, col0), (proj, col0 + nb), (proj, col0 + 2 * nb), (dyb, dcol0)], [conv_w],
                   [BF16, BF16, BF16], [3], tc=tc)


def _expm1(x):
    series = x * (1.0 + 0.5 * x * (1.0 + x * (1.0 / 3.0) * (1.0 + 0.25 * x * (1.0 + 0.2 * x))))
    return jnp.where(jnp.abs(x) < 0.05, series, jnp.exp(x) - 1.0)


def _log1p(x):
    series = x * (1.0 - x * (0.5 - x * (1.0 / 3.0 - 0.25 * x)))
    return jnp.where(jnp.abs(x) < 0.01, series, jnp.log(1.0 + x))


def _softplus_neg(lam):
    sp = jnp.maximum(-lam, 0.0) + _log1p(jnp.exp(-jnp.abs(lam)))
    return sp, -_sigmoid(-lam)


GELU_C = math.sqrt(2.0 / math.pi)


def _gelu(x):
    th = jnp.tanh(GELU_C * (x + 0.044715 * x * x * x))
    val = 0.5 * x * (1.0 + th)
    grad = 0.5 * (1.0 + th) + 0.5 * x * (1.0 - th * th) * GELU_C * (1.0 + 3.0 * 0.044715 * x * x)
    return val, grad


def rg_conv_fwd(name, gu2, conv_w, conv_b, tc=128):
    nb = D_MODEL // tc

    def body(cl, p, o, po):
        u = cl[0][...]
        rows = _row_index(u.shape)
        w = p[0][...]
        o[0][...] = (w[3:4] * u + w[2:3] * _shift_down(u, 1, rows) + w[1:2] * _shift_down(u, 2, rows)
                     + w[0:1] * _shift_down(u, 3, rows) + p[1][...])
    return colwise(name, body, [(gu2, nb)], [conv_w, conv_b], [F32], tc=tc)[0]


def rg_conv_bwd(name, gu2, duc, conv_w, tc=128):
    nb = D_MODEL // tc

    def body(cl, p, o, po):
        u, d = cl[0][...], cl[1][...]
        rows = _row_index(u.shape)
        w = p[0][...]
        o[0][...] = (w[3:4] * d + w[2:3] * _shift_up(d, 1, rows) + w[1:2] * _shift_up(d, 2, rows)
                     + w[0:1] * _shift_up(d, 3, rows)).astype(BF16)
        for k in range(4):
            uk = u if k == 3 else _shift_down(u, 3 - k, rows)
            po[0][k:k + 1, :] = jnp.sum(d * uk, axis=0, keepdims=True)
        po[1][...] = jnp.sum(d, axis=0, keepdims=True)
    return colwise(name, body, [(gu2, nb), (duc, 0)], [conv_w], [BF16], [4, 1], tc=tc)


def rg_gates_fwd(name, uc, w_a, b_a, w_i, b_i, tr=512):
    t = uc.shape[0]
    tr = min(tr, t)

    def kern(u_ref, wa_ref, ba_ref, wi_ref, bi_ref, r_ref, i_ref):
        ub = u_ref[...].astype(BF16)
        r_ref[...] = _sigmoid(_dot(ub, wa_ref[...]) + ba_ref[...])
        i_ref[...] = _sigmoid(_dot(ub, wi_ref[...]) + bi_ref[...])

    blk = pl.BlockSpec((tr, LRU_BW), lambda n, i: (i, n))
    wspec = pl.BlockSpec((None, LRU_BW, LRU_BW), lambda n, i: (n, 0, 0))
    bspec = pl.BlockSpec((1, LRU_BW), lambda n, i: (0, n))
    return _pcall(name, kern, (LRU_BLOCKS, t // tr), (uc, w_a, b_a, w_i, b_i), [blk, wspec, bspec, wspec, bspec],
                  [jax.ShapeDtypeStruct(uc.shape, F32)] * 2, [blk, blk], ("parallel", "parallel"))


def rg_gates_bwd(name, uc, dzr, dzi, duc_part, w_a, w_i):
    t = uc.shape[0]
    rows = LRU_BW // N_CHIPS

    def kern(u_ref, dr_ref, di_ref, dp_ref, wa_ref, wi_ref, duc_ref, dwa_ref, dwi_ref):
        ub = u_ref[...].astype(BF16)
        dr, di = dr_ref[...], di_ref[...]
        dwa, dwi = _dot(ub, dr, TN), _dot(ub, di, TN)
        for p in range(N_CHIPS):
            dwa_ref[p] = dwa[p * rows:(p + 1) * rows].astype(dwa_ref.dtype)
            dwi_ref[p] = dwi[p * rows:(p + 1) * rows].astype(dwi_ref.dtype)
        duc_ref[...] = dp_ref[...] + _dot(dr, wa_ref[...], NT) + _dot(di, wi_ref[...], NT)

    blk = pl.BlockSpec((t, LRU_BW), lambda n: (0, n))
    wspec = pl.BlockSpec((None, LRU_BW, LRU_BW), lambda n: (n, 0, 0))
    gspec = pl.BlockSpec((N_CHIPS, None, rows, LRU_BW), lambda n: (0, n, 0, 0))
    gshape = jax.ShapeDtypeStruct((N_CHIPS, LRU_BLOCKS, rows, LRU_BW), BF16)
    return _pcall(name, kern, (LRU_BLOCKS,), (uc, dzr, dzi, duc_part, w_a, w_i), [blk, blk, blk, blk, wspec, wspec],
                  [jax.ShapeDtypeStruct(uc.shape, F32), gshape, gshape], [blk, gspec, gspec], ("parallel",))


def _rg_decay(r, lam):
    sp, dsp = _softplus_neg(lam)
    la = -RG_C * r * sp
    a = jnp.exp(la)
    sq = jnp.sqrt(-_expm1(2.0 * la))
    return sp, dsp, a, sq


def rg_scan_fwd(name, gu2, uc, r, i, lam, tc=128):
    def body(cl, p, o, po):
        gate, ucv, rv, iv = (x[...] for x in cl)
        t = gate.shape[0]
        rows = _row_index(gate.shape)
        _, _, a, sq = _rg_decay(rv, p[0][...])
        b = sq * (iv * ucv)
        d = 1
        while d < t:
            keep = rows >= d
            b = a * jnp.where(keep, pltpu.roll(b, d, 0), 0.0) + b
            a = a * jnp.where(keep, pltpu.roll(a, d, 0), 1.0)
            d *= 2
        o[0][...] = (_gelu(gate)[0] * b).astype(BF16)
        o[1][...] = b
    return colwise(name, body, [(gu2, 0), (uc, 0), (r, 0), (i, 0)], [lam], [BF16, F32], tc=tc)


def rg_scan_bwd(name, gu2, uc, r, i, hs, dy, lam, tc=128):
    def body(cl, p, o, po):
        gate, ucv, rv, iv, h, dyv = (x[...] for x in cl)
        t = gate.shape[0]
        rows = _row_index(gate.shape)
        sp, dsp, a, sq = _rg_decay(rv, p[0][...])
        gl, dgl = _gelu(gate)
        o[0][...] = (dyv * h * dgl).astype(BF16)
        g = dyv * gl
        am = _shift_up(a, 1, rows)
        d = 1
        while d < t:
            keep = rows < t - d
            g = am * jnp.where(keep, pltpu.roll(g, t - d, 0), 0.0) + g
            am = am * jnp.where(keep, pltpu.roll(am, t - d, 0), 0.0)
            d *= 2
        da = g * _shift_down(h, 1, rows)
        iu = iv * ucv
        d_iu = g * sq
        dla = da * a - (g * iu) * (a * a) / sq
        dzr = dla * (-RG_C * sp) * rv * (1.0 - rv)
        dzi = d_iu * ucv * iv * (1.0 - iv)
        o[1][...] = dzr.astype(BF16)
        o[2][...] = dzi.astype(BF16)
        o[3][...] = d_iu * iv
        po[0][...] = jnp.sum(dzr, axis=0, keepdims=True)
        po[1][...] = jnp.sum(dzi, axis=0, keepdims=True)
        po[2][...] = jnp.sum(dla * rv, axis=0, keepdims=True) * (-RG_C) * dsp
    return colwise(name, body, [(gu2, 0), (uc, 0), (r, 0), (i, 0), (hs, 0), (dy, 0)], [lam],
                   [BF16, BF16, BF16, F32], [1, 1, 1], tc=tc)


def _split3(x):
    hi = x.astype(BF16)
    r1 = x - hi.astype(F32)
    mid = r1.astype(BF16)
    lo = (r1 - mid.astype(F32)).astype(BF16)
    return hi, mid, lo


def _tri_dot(x, tri):
    out = None
    for piece in _split3(x):
        term = lax.dot_general(piece, tri, NN, preferred_element_type=F32)
        out = term if out is None else out + term
    return out


def fox_gates_fwd(name, z_t, b_f):
    h, t = z_t.shape
    tb = min(512, t)

    def kern(z_ref, b_ref, o_ref):
        z = z_ref[...] + b_ref[...]
        logf = jnp.minimum(z, 0.0) - _log1p(jnp.exp(-jnp.abs(z)))
        src = lax.broadcasted_iota(jnp.int32, (t, tb), 0)
        dst = lax.broadcasted_iota(jnp.int32, (t, tb), 1) + pl.program_id(0) * tb
        o_ref[...] = _tri_dot(logf, (src <= dst).astype(BF16))

    return _pcall(name, kern, (t // tb,), (z_t, b_f),
                  [pl.BlockSpec((h, t), lambda j: (0, 0)), pl.BlockSpec((h, 1), lambda j: (0, 0))],
                  jax.ShapeDtypeStruct((h, t), F32), pl.BlockSpec((h, tb), lambda j: (0, j)), ("parallel",))


def fox_gates_bwd(name, z_t, b_f, dcum_t):
    h, t = z_t.shape
    tb = min(512, t)

    def kern(z_ref, b_ref, d_ref, dz_ref, db_ref):
        @pl.when(pl.program_id(0) == 0)
        def _():
            db_ref[...] = jnp.zeros_like(db_ref)
        src = lax.broadcasted_iota(jnp.int32, (t, tb), 0)
        dst = lax.broadcasted_iota(jnp.int32, (t, tb), 1) + pl.program_id(0) * tb
        dlogf = _tri_dot(d_ref[...], (src >= dst).astype(BF16))
        z = z_ref[...] + b_ref[...]
        dz = dlogf * _sigmoid(-z)
        dz_ref[...] = dz
        db_ref[...] += jnp.sum(dz, axis=1, keepdims=True)

    return _pcall(name, kern, (t // tb,), (z_t, b_f, dcum_t),
                  [pl.BlockSpec((h, tb), lambda j: (0, j)), pl.BlockSpec((h, 1), lambda j: (0, 0)),
                   pl.BlockSpec((h, t), lambda j: (0, 0))],
                  [jax.ShapeDtypeStruct((h, t), F32), jax.ShapeDtypeStruct((h, 1), F32)],
                  [pl.BlockSpec((h, tb), lambda j: (0, j)), pl.BlockSpec((h, 1), lambda j: (0, 0))], ("arbitrary",))


def _fox_spans(qs, k_ref, cr_ref, i, tq):
    n0 = i * tq
    sd = _dot(qs, k_ref[n0:n0 + tq, :], NT) - cr_ref[:, n0:n0 + tq]
    row = lax.broadcasted_iota(jnp.int32, (tq, tq), 0)
    col = lax.broadcasted_iota(jnp.int32, (tq, tq), 1)
    spans = [(n0, tq, jnp.where(row >= col, sd, NEG_INF))]
    if i > 0:
        spans.append((0, n0, _dot(qs, k_ref[0:n0, :], NT) - cr_ref[:, 0:n0]))
    return spans


def fox_fwd(name, q, k, v, cum_r, tq=256):
    h, t, dh = q.shape
    tq = min(tq, t)
    scale = FOX_HEAD_DIM ** -0.5

    def kern(q_ref, k_ref, v_ref, cr_ref, o_ref, lse_ref):
        for i in range(t // tq):
            rows = slice(i * tq, (i + 1) * tq)
            spans = _fox_spans(q_ref[rows, :] * scale, k_ref, cr_ref, i, tq)
            m = functools.reduce(jnp.maximum, [jnp.max(s, axis=-1, keepdims=True) for _, _, s in spans])
            l, acc = 0.0, 0.0
            for k0, kn, s in spans:
                p = jnp.exp(s - m)
                l = l + jnp.sum(p, axis=-1, keepdims=True)
                acc = acc + _dot(p, v_ref[k0:k0 + kn, :])
            o_ref[rows, :] = (acc / l).astype(o_ref.dtype)
            lse_ref[rows, :] = m + jnp.log(l)

    hspec = pl.BlockSpec((None, t, dh), lambda a: (a, 0, 0))
    cspec = pl.BlockSpec((None, t, 1), lambda a: (a, 0, 0))
    rspec = pl.BlockSpec((None, 1, t), lambda a: (a, 0, 0))
    return _pcall(name, kern, (h,), (q, k, v, cum_r), [hspec, hspec, hspec, rspec],
                  [jax.ShapeDtypeStruct((h, t, dh), BF16), jax.ShapeDtypeStruct((h, t, 1), F32)],
                  [hspec, cspec], ("parallel",))


def fox_bwd(name, q, k, v, do, lse, cum_r, tq=256):
    h, t, dh = q.shape
    tq = min(tq, t)
    scale = FOX_HEAD_DIM ** -0.5

    def kern(q_ref, k_ref, v_ref, do_ref, lse_ref, cr_ref, dq_ref, dk_ref, dv_ref, dc_ref):
        dk_ref[...] = jnp.zeros_like(dk_ref)
        dv_ref[...] = jnp.zeros_like(dv_ref)
        dc_ref[...] = jnp.zeros_like(dc_ref)
        for i in range(t // tq):
            rows = slice(i * tq, (i + 1) * tq)
            qs, dov, lse_v = q_ref[rows, :] * scale, do_ref[rows, :], lse_ref[rows, :]
            spans = _fox_spans(qs, k_ref, cr_ref, i, tq)
            probs = [jnp.exp(s - lse_v) for _, _, s in spans]
            dps = [_dot(dov, v_ref[k0:k0 + kn, :], NT) for k0, kn, _ in spans]
            rowdot = sum(jnp.sum(dp * p, axis=-1, keepdims=True) for dp, p in zip(dps, probs))
            dq = 0.0
            for (k0, kn, _), p, dp in zip(spans, probs, dps):
                ds = p * (dp - rowdot)
                dq = dq + _dot(ds, k_ref[k0:k0 + kn, :])
                dk_ref[k0:k0 + kn, :] += _dot(ds, qs, TN)
                dv_ref[k0:k0 + kn, :] += _dot(p, dov, TN)
                dc_ref[:, k0:k0 + kn] -= jnp.sum(ds, axis=0, keepdims=True)
            dq_ref[rows, :] = (dq * scale).astype(dq_ref.dtype)

    hspec = pl.BlockSpec((None, t, dh), lambda a: (a, 0, 0))
    cspec = pl.BlockSpec((None, t, 1), lambda a: (a, 0, 0))
    rspec = pl.BlockSpec((None, 1, t), lambda a: (a, 0, 0))
    return _pcall(name, kern, (h,), (q, k, v, do, lse, cum_r), [hspec, hspec, hspec, hspec, cspec, rspec],
                  [jax.ShapeDtypeStruct((h, t, dh), BF16), jax.ShapeDtypeStruct((h, t, dh), F32),
                   jax.ShapeDtypeStruct((h, t, dh), F32), jax.ShapeDtypeStruct((h, 1, t), F32)],
                  [hspec, hspec, hspec, rspec], ("parallel",))


def _xattn_probs(q, k):
    s = _dot(q, k, NT) * (MEM_HEAD_DIM ** -0.5)
    p = jnp.exp(s - jnp.max(s, axis=-1, keepdims=True))
    return p / jnp.sum(p, axis=-1, keepdims=True)


def xattn_fwd(name, q, kv, tq=512):
    t = q.shape[0]
    tq = min(tq, t)
    ml = kv.shape[0]

    def kern(q_ref, k_ref, v_ref, o_ref):
        o_ref[...] = _dot(_xattn_probs(q_ref[...], k_ref[...]), v_ref[...]).astype(o_ref.dtype)

    qspec = pl.BlockSpec((tq, MEM_HEAD_DIM), lambda i, a: (i, a))
    return _pcall(name, kern, (t // tq, MEM_HEADS), (q, kv, kv),
                  [qspec, pl.BlockSpec((ml, MEM_HEAD_DIM), lambda i, a: (0, a)),
                   pl.BlockSpec((ml, MEM_HEAD_DIM), lambda i, a: (0, MEM_HEADS + a))],
                  jax.ShapeDtypeStruct(q.shape, BF16), qspec, ("parallel", "parallel"))


def xattn_bwd(name, q, kv, do, tq=512):
    t = q.shape[0]
    tq = min(tq, t)
    ml = kv.shape[0]
    scale = MEM_HEAD_DIM ** -0.5

    def kern(q_ref, k_ref, v_ref, do_ref, dq_ref, dk_ref, dv_ref):
        @pl.when(pl.program_id(1) == 0)
        def _():
            dk_ref[...] = jnp.zeros_like(dk_ref)
            dv_ref[...] = jnp.zeros_like(dv_ref)
        qv, kv_, dov = q_ref[...], k_ref[...], do_ref[...]
        p = _xattn_probs(qv, kv_)
        dp = _dot(dov, v_ref[...], NT)
        ds = p * (dp - jnp.sum(dp * p, axis=-1, keepdims=True)) * scale
        dq_ref[...] = _dot(ds, kv_).astype(dq_ref.dtype)
        dk_ref[...] += _dot(ds, qv, TN)
        dv_ref[...] += _dot(p, dov, TN)

    qspec = pl.BlockSpec((tq, MEM_HEAD_DIM), lambda a, i: (i, a))
    kspec = pl.BlockSpec((ml, MEM_HEAD_DIM), lambda a, i: (0, a))
    return _pcall(name, kern, (MEM_HEADS, t // tq), (q, kv, kv, do),
                  [qspec, kspec, pl.BlockSpec((ml, MEM_HEAD_DIM), lambda a, i: (0, MEM_HEADS + a)), qspec],
                  [jax.ShapeDtypeStruct(q.shape, BF16), jax.ShapeDtypeStruct((ml, D_MODEL), F32),
                   jax.ShapeDtypeStruct((ml, D_MODEL), F32)],
                  [qspec, kspec, kspec], ("parallel", "arbitrary"))


def _heads(x):
    t = x.shape[0]
    return x.reshape(t, FOX_HEADS, FOX_HEAD_DIM).transpose(1, 0, 2)


def _unheads(x):
    return x.transpose(1, 0, 2).reshape(x.shape[1], FOX_WIDTH)


def _row_cut(dw):
    return dw.reshape(N_CHIPS, 2, dw.shape[0] // (2 * N_CHIPS), dw.shape[1])


def local_step(x, mem, target, w, layer_weights=None, reduce_hook=None):
    depth = w["g_mix_pre"].shape[0]
    t = x.shape[0]
    saved = []
    i1, i2, i3 = 3 * FOX_WIDTH, 3 * FOX_WIDTH + FOX_HEADS, AB_IN
    ncol = 128

    def stacked_weights(layer, part, _):
        names = COMMON_BIG if part == "rest" else layer_big(layer)[len(COMMON_BIG):]
        return {n: w[n][layer if n in COMMON_BIG else layer // 2] for n in names}

    get_weights = layer_weights or stacked_weights
    for layer in range(depth):
        lw = dict(get_weights(layer, "mix", x))
        s = {"x0": x, "lw": lw}
        tag = f"l{layer}"
        h1 = rms_pre(f"{tag}_mix_pre", x, w["g_mix_pre"], layer)
        s["h1"] = h1
        if layer % 2 == 0:
            e = layer // 2
            w_in = jnp.pad(lw["ab_w_in"], ((0, 0), (0, AB_IN_PAD - AB_IN)))
            proj = mm(f"{tag}_ab_in", h1, w_in, "nn", F32)
            qkv = proj[:, :i1].astype(BF16).reshape(t, 3, FOX_HEADS, FOX_HEAD_DIM).transpose(1, 2, 0, 3)
            z_t = proj[:, i1:i2].T
            b_f = w["ab_b_f"][e].reshape(FOX_HEADS, 1)
            cum_t = fox_gates_fwd(f"{tag}_fox_gates", z_t, b_f)
            cum_r = cum_t[:, None, :]
            oh, lse = fox_fwd(f"{tag}_fox", qkv[0], qkv[1], qkv[2], cum_r)
            bcu = proj[:, i2:i3]
            y_b = sconv_fwd(f"{tag}_sconv", bcu, 0, w["ab_conv_w"][e])
            ycat = jnp.concatenate([_unheads(oh), y_b], axis=1)
            y1 = mm(f"{tag}_ab_out", ycat, lw["ab_w_out"], "nn", F32)
            s.update(w_in=w_in, qkv=qkv, z_t=z_t, b_f=b_f, cum_r=cum_r, lse=lse, bcu=bcu, ycat=ycat)
        else:
            o = layer // 2
            gu2 = mm(f"{tag}_c_in", h1, lw["c_w_in"], "nn", F32)
            conv_b = w["c_conv_b"][o].reshape(1, -1)
            uc = rg_conv_fwd(f"{tag}_rg_conv", gu2, w["c_conv_w"][o], conv_b)
            b_a, b_i = w["c_b_a"][o].reshape(1, -1), w["c_b_i"][o].reshape(1, -1)
            r, i = rg_gates_fwd(f"{tag}_rg_gates", uc, lw["c_w_a"], b_a, lw["c_w_i"], b_i)
            lam = w["c_lam"][o].reshape(1, -1)
            ymix, hs = rg_scan_fwd(f"{tag}_rg_scan", gu2, uc, r, i, lam)
            y1 = mm(f"{tag}_c_out", ymix, lw["c_w_out"], "nn", F32)
            s.update(gu2=gu2, uc=uc, r=r, i=i, lam=lam, hs=hs, ymix=ymix)
        s["y1"] = y1
        x = post_add(f"{tag}_mix_post", x, y1, w["g_mix_post"], layer)
        lw.update(get_weights(layer, "rest", x))
        s["x1"] = x
        h2 = rms_pre(f"{tag}_cross_pre", x, w["g_cross_pre"], layer)
        m = rms_pre(f"{tag}_mem_pre", mem, w["g_mem"], layer)
        q = mm(f"{tag}_xq", h2, lw["w_xq"], "nn", BF16)
        kv = mm(f"{tag}_xkv", m, lw["w_xkv"], "nn", BF16)
        o_att = xattn_fwd(f"{tag}_xattn", q, kv)
        y2 = mm(f"{tag}_xo", o_att, lw["w_xo"], "nn", F32)
        s.update(h2=h2, m=m, q=q, kv=kv, o_att=o_att, y2=y2)
        x = post_add(f"{tag}_cross_post", x, y2, w["g_cross_post"], layer)
        s["x2"] = x
        h3 = rms_pre(f"{tag}_ffn_pre", x, w["g_ffn_pre"], layer)
        gu = mm(f"{tag}_ffn_gu", h3, lw["w_ffn_gu"], "nn", BF16)
        act = swiglu_fwd(f"{tag}_swiglu", gu)
        y3 = mm(f"{tag}_ffn_down", act, lw["w_ffn_down"], "nn", F32)
        s.update(h3=h3, gu=gu, act=act, y3=y3)
        x = post_add(f"{tag}_ffn_post", x, y3, w["g_ffn_post"], layer)
        saved.append(s)

    dx, sq_cols = loss_head("loss_head", x, target)

    grads = {k: [None] * v.shape[0] for k, v in w.items() if k not in BIG}
    big, token = {}, None

    def dw(name, a, b, cols_cut=False):
        return mm(name, a, b, "tn", BF16, reduce_layout=True) if cols_cut else _row_cut(mm(name, a, b, "tn", BF16))

    for layer in reversed(range(depth)):
        s = saved[layer]
        lw = s["lw"]
        tag = f"b{layer}"
        lg = {}
        g_ffn_post = w["g_ffn_post"] if token is None else w["g_ffn_post"] + token
        dy3, grads["g_ffn_post"][layer] = post_bwd(f"{tag}_ffn_post", s["y3"], dx, g_ffn_post, layer)
        dact = mm(f"{tag}_ffn_down_dx", dy3, lw["w_ffn_down"], "nt", BF16)
        lg["w_ffn_down"] = dw(f"{tag}_ffn_down_dw", s["act"], dy3)
        dgu = swiglu_bwd(f"{tag}_swiglu", s["gu"], dact)
        dh3 = mm(f"{tag}_ffn_gu_dx", dgu, lw["w_ffn_gu"], "nt", F32)
        lg["w_ffn_gu"] = dw(f"{tag}_ffn_gu_dw", s["h3"], dgu, cols_cut=True)
        dx, grads["g_ffn_pre"][layer] = pre_bwd(f"{tag}_ffn_pre", s["x2"], dh3, dx, w["g_ffn_pre"], layer)
        dy2, grads["g_cross_post"][layer] = post_bwd(f"{tag}_cross_post", s["y2"], dx, w["g_cross_post"], layer)
        do = mm(f"{tag}_xo_dx", dy2, lw["w_xo"], "nt", BF16)
        lg["w_xo"] = dw(f"{tag}_xo_dw", s["o_att"], dy2)
        dq, dk, dv = xattn_bwd(f"{tag}_xattn", s["q"], s["kv"], do)
        dh2 = mm(f"{tag}_xq_dx", dq, lw["w_xq"], "nt", F32)
        lg["w_xq"] = dw(f"{tag}_xq_dw", s["h2"], dq)
        dkv = jnp.concatenate([dk, dv], axis=1).astype(BF16)
        dm = mm(f"{tag}_xkv_dx", dkv, lw["w_xkv"], "nt", F32)
        lg["w_xkv"] = dw(f"{tag}_xkv_dw", s["m"], dkv, cols_cut=True)
        grads["g_mem"][layer] = gain_bwd(f"{tag}_mem_pre", mem, dm)
        dx, grads["g_cross_pre"][layer] = pre_bwd(f"{tag}_cross_pre", s["x1"], dh2, dx, w["g_cross_pre"], layer)
        token = None if reduce_hook is None else reduce_hook(layer, "rest", lg)
        g_mix_post = w["g_mix_post"] if token is None else w["g_mix_post"] + token
        dy1, grads["g_mix_post"][layer] = post_bwd(f"{tag}_mix_post", s["y1"], dx, g_mix_post, layer)
        rest_grads, lg = lg, {}
        if layer % 2 == 0:
            e = layer // 2
            dycat = mm(f"{tag}_ab_out_dx", dy1, lw["ab_w_out"], "nt", F32)
            lg["ab_w_out"] = dw(f"{tag}_ab_out_dw", s["ycat"], dy1)
            do_h = _heads(dycat[:, :FOX_WIDTH].astype(BF16))
            qkv = s["qkv"]
            dqh, dkh, dvh, dcum = fox_bwd(f"{tag}_fox", qkv[0], qkv[1], qkv[2], do_h, s["lse"], s["cum_r"])
            dz_t, db_f = fox_gates_bwd(f"{tag}_fox_gates", s["z_t"], s["b_f"], dcum.reshape(FOX_HEADS, t))
            grads["ab_b_f"][e] = db_f.reshape(FOX_HEADS)
            db, dc, du, dconv_w = sconv_bwd(f"{tag}_sconv", s["bcu"], 0, w["ab_conv_w"][e], dycat, FOX_WIDTH // ncol)
            grads["ab_conv_w"][e] = dconv_w
            dproj = jnp.concatenate(
                [_unheads(dqh), _unheads(dkh).astype(BF16), _unheads(dvh).astype(BF16), dz_t.T.astype(BF16), db, dc, du,
                 jnp.zeros((t, AB_IN_PAD - AB_IN), BF16)], axis=1)
            dh1 = mm(f"{tag}_ab_in_dx", dproj, s["w_in"], "nt", F32)
            dw_in = mm(f"{tag}_ab_in_dw", s["h1"], dproj, "tn", F32)[:, :AB_IN]
            lg["ab_w_in"] = dw_in.reshape(2, D_MODEL // 2, N_CHIPS, AB_IN // N_CHIPS).transpose(2, 0, 1, 3).astype(BF16)
        else:
            o = layer // 2
            dymix = mm(f"{tag}_c_out_dx", dy1, lw["c_w_out"], "nt", F32)
            lg["c_w_out"] = dw(f"{tag}_c_out_dw", s["ymix"], dy1)
            dgate, dzr, dzi, duc_part, db_a, db_i, dlam = rg_scan_bwd(
                f"{tag}_rg_scan", s["gu2"], s["uc"], s["r"], s["i"], s["hs"], dymix, s["lam"])
            duc, dw_a, dw_i = rg_gates_bwd(f"{tag}_rg_gates", s["uc"], dzr, dzi, duc_part, lw["c_w_a"], lw["c_w_i"])
            lg["c_w_a"] = dw_a.reshape(N_CHIPS, 2, LRU_BW // 2, LRU_BW)
            lg["c_w_i"] = dw_i.reshape(N_CHIPS, 2, LRU_BW // 2, LRU_BW)
            du_raw, dconv_w, dconv_b = rg_conv_bwd(f"{tag}_rg_conv", s["gu2"], duc, w["c_conv_w"][o])
            grads["c_b_a"][o] = db_a.reshape(LRU_BLOCKS, LRU_BW)
            grads["c_b_i"][o] = db_i.reshape(LRU_BLOCKS, LRU_BW)
            grads["c_lam"][o] = dlam.reshape(-1)
            grads["c_conv_w"][o] = dconv_w
            grads["c_conv_b"][o] = dconv_b.reshape(-1)
            dgu2 = jnp.concatenate([dgate, du_raw], axis=1)
            dh1 = mm(f"{tag}_c_in_dx", dgu2, lw["c_w_in"], "nt", F32)
            lg["c_w_in"] = dw(f"{tag}_c_in_dw", s["h1"], dgu2, cols_cut=True)
        dx, grads["g_mix_pre"][layer] = pre_bwd(f"{tag}_mix_pre", s["x0"], dh1, dx, w["g_mix_pre"], layer)
        if reduce_hook is None:
            big[layer] = {**rest_grads, **lg}
        else:
            token = reduce_hook(layer, "mix", lg)

    for k in list(grads):
        if k.startswith("g_"):
            grads[k] = [g.reshape(-1) for g in grads[k]]
        grads[k] = jnp.stack(grads[k])
    return sq_cols, dx, grads, big


CHIP_FLIPS = ((1, 0), (0, 1), (1, 1))
HBM_SPEC = pl.BlockSpec(memory_space=pltpu.HBM)
VMEM_SPEC = pl.BlockSpec(memory_space=pltpu.VMEM)


def _place():
    return lax.axis_index("x"), lax.axis_index("y"), lax.axis_index("c")


def _flip(v, f):
    return 1 - v if f else v


def _remote(src, dst, send_sem, recv_sem, target):
    return pltpu.make_async_remote_copy(src_ref=src, dst_ref=dst, send_sem=send_sem, recv_sem=recv_sem,
                                        device_id=target, device_id_type=MESH)


SEM_SPEC = pl.BlockSpec(memory_space=pltpu.SEMAPHORE)


def _swap_copies(srcs, lands, send_sems, recv_sems):
    x, y, c = _place()
    return [_remote(src.at[:, 1 - c], land, send_sems.at[len(CHIP_FLIPS) * a], recv_sems.at[len(CHIP_FLIPS) * a],
                    (x, y, 1 - c)) for a, (src, land) in enumerate(zip(srcs, lands))]


def _exchange_copies(srcs, lands, send_sems, recv_sems):
    x, y, c = _place()
    p = 2 * x + y
    cps = []
    for a, (src, land) in enumerate(zip(srcs, lands)):
        for k, (fx, fy) in enumerate(CHIP_FLIPS):
            qx, qy = _flip(x, fx), _flip(y, fy)
            sem = len(CHIP_FLIPS) * a + k
            cps.append(_remote(src.at[2 * qx + qy], land.at[p], send_sems.at[sem], recv_sems.at[sem], (qx, qy, c)))
    return cps


def _gather_copies(srcs, lands, send_sems, recv_sems):
    x, y, c = _place()
    p = 2 * x + y
    cps = []
    for a, (src, land) in enumerate(zip(srcs, lands)):
        for k, (fx, fy) in enumerate(CHIP_FLIPS):
            sem = len(CHIP_FLIPS) * a + k
            cps.append(_remote(src.at[c], land.at[p, c], send_sems.at[sem], recv_sems.at[sem],
                               (_flip(x, fx), _flip(y, fy), c)))
    return cps


def copies_start(name, make_copies, srcs, land_shapes):
    n = len(srcs)

    def body(*refs):
        for cp in make_copies(refs[:n], refs[n:2 * n], refs[2 * n], refs[2 * n + 1]):
            cp.start()
        refs[-1][...] = jnp.zeros_like(refs[-1])

    thru = [pltpu.HBM(b.shape, b.dtype) for b in srcs] + [pltpu.HBM(sh, b.dtype) for sh, b in zip(land_shapes, srcs)]
    outs = pl.pallas_call(
        body, name=name, in_specs=[HBM_SPEC] * (2 * n),
        out_shape=(pltpu.SemaphoreType.DMA((3 * n,)), pltpu.SemaphoreType.DMA((3 * n,)), *thru,
                   jax.ShapeDtypeStruct((8, 128), F32)),
        out_specs=(SEM_SPEC, SEM_SPEC, *[HBM_SPEC] * (2 * n), VMEM_SPEC),
        input_output_aliases={i: 2 + i for i in range(2 * n)},
        compiler_params=pltpu.CompilerParams(has_side_effects=pltpu.SideEffectType.DATAFLOW_SIDE_EFFECTING),
    )(*[pltpu.with_memory_space_constraint(b, pltpu.HBM) for b in srcs],
      *[pltpu.with_memory_space_constraint(lax.empty(sh, b.dtype), pltpu.HBM) for sh, b in zip(land_shapes, srcs)])
    return outs[:-1], outs[-1]


def copies_wait(name, make_copies, state, after):
    send_sems, recv_sems, *thru = state
    n = len(thru) // 2

    def body(*refs):
        for cp in make_copies(refs[:n], refs[n:2 * n], refs[2 * n], refs[2 * n + 1]):
            cp.wait_send()
            cp.wait_recv()

    outs = pl.pallas_call(
        body, name=name, in_specs=[HBM_SPEC] * (2 * n) + [SEM_SPEC, SEM_SPEC, pl.BlockSpec(memory_space=pl.ANY)],
        out_shape=tuple(pltpu.HBM(t.shape, t.dtype) for t in thru), out_specs=tuple([HBM_SPEC] * (2 * n)),
        input_output_aliases={i: i for i in range(2 * n)},
        compiler_params=pltpu.CompilerParams(has_side_effects=pltpu.SideEffectType.DATAFLOW_SIDE_EFFECTING),
    )(*thru, send_sems, recv_sems, after)
    return outs[:n], outs[n:]


def pass_to_sibling(name, shards, lands):
    n = len(lands)

    def body(*refs):
        own, ins, outs = refs[:n], refs[n:2 * n], refs[2 * n:3 * n]
        send_sems, recv_sems = refs[3 * n:]
        x, y, c = _place()
        sibling = (x, y, 1 - c)
        cps = []
        for a in range(n):
            for k, (fx, fy) in enumerate(CHIP_FLIPS):
                q = 2 * _flip(x, fx) + _flip(y, fy)
                cps.append(_remote(ins[a].at[q, c], outs[a].at[q, c], send_sems.at[a, k], recv_sems.at[a, k], sibling))
            cps.append(_remote(own[a], outs[a].at[2 * x + y], send_sems.at[a, 3], recv_sems.at[a, 3], sibling))
        for cp in cps:
            cp.start()
        for cp in cps:
            cp.wait()

    return pl.pallas_call(
        body, name=name, in_specs=[HBM_SPEC] * (2 * n), out_specs=[HBM_SPEC] * n,
        out_shape=[jax.ShapeDtypeStruct(b.shape, b.dtype) for b in lands],
        scratch_shapes=[pltpu.SemaphoreType.DMA((n, 4)), pltpu.SemaphoreType.DMA((n, 4))],
        input_output_aliases={n + i: i for i in range(n)},
    )(*shards, *lands)


def share_halves(bufs):
    n = len(bufs)

    def body(*refs):
        ins, outs = refs[:n], refs[n:2 * n]
        send_sems, recv_sems = refs[2 * n:]
        x, y, c = _place()
        cps = [_remote(ins[a].at[:, c], outs[a].at[:, c], send_sems.at[a], recv_sems.at[a], (x, y, 1 - c))
               for a in range(n)]
        for cp in cps:
            cp.start()
        for cp in cps:
            cp.wait()

    return pl.pallas_call(
        body, name="share_reduced_halves", in_specs=[HBM_SPEC] * n, out_specs=[HBM_SPEC] * n,
        out_shape=[jax.ShapeDtypeStruct(b.shape, b.dtype) for b in bufs],
        scratch_shapes=[pltpu.SemaphoreType.DMA((n,)), pltpu.SemaphoreType.DMA((n,))],
        input_output_aliases={i: i for i in range(n)},
    )(*bufs)


DEVICE_FLIPS = tuple((fx, fy, fc) for fx in (0, 1) for fy in (0, 1) for fc in (0, 1))[1:]


def gather_small(name, v, reduce):
    r, cdim = v.shape
    n_dev = 8

    def body(v_ref, out_ref, *scratch):
        buf = scratch[0] if reduce else out_ref
        send_sems, recv_sems = scratch[-2:]
        x, y, c = _place()
        me = 4 * x + 2 * y + c
        buf[me] = v_ref[...]
        cps = []
        for k, (fx, fy, fc) in enumerate(DEVICE_FLIPS):
            cps.append(_remote(v_ref, buf.at[me], send_sems.at[k], recv_sems.at[k],
                               (_flip(x, fx), _flip(y, fy), _flip(c, fc))))
        for cp in cps:
            cp.start()
        for cp in cps:
            cp.wait()
        if reduce:
            total = buf[0]
            for d in range(1, n_dev):
                total = total + buf[d]
            out_ref[...] = total

    scratch = [pltpu.SemaphoreType.DMA((7,)), pltpu.SemaphoreType.DMA((7,))]
    if reduce:
        scratch = [pltpu.VMEM((n_dev, r, cdim), F32)] + scratch
    out_shape = jax.ShapeDtypeStruct((r, cdim) if reduce else (n_dev, r, cdim), F32)
    return pl.pallas_call(body, name=name, in_specs=[VMEM_SPEC], out_specs=VMEM_SPEC, out_shape=out_shape,
                          scratch_shapes=scratch)(v)


def pair_sum(name, own, got, core):
    _, hx, cols = got.shape
    tr = _tile(hx, (256, 128, 64, 32, 16))

    def kern(core_ref, a_ref, b_ref, o_ref):
        o_ref[...] = (a_ref[...].astype(F32) + b_ref[...].astype(F32)).astype(BF16)

    grid_spec = pltpu.PrefetchScalarGridSpec(
        num_scalar_prefetch=1, grid=(hx // tr,),
        in_specs=[pl.BlockSpec((N_CHIPS, None, tr, cols), lambda i, cr: (0, cr[0], i, 0)),
                  pl.BlockSpec((N_CHIPS, tr, cols), lambda i, cr: (0, i, 0))],
        out_specs=pl.BlockSpec((N_CHIPS, tr, cols), lambda i, cr: (0, i, 0)))
    return pl.pallas_call(
        kern, name=name, grid_spec=grid_spec, out_shape=jax.ShapeDtypeStruct(got.shape, BF16),
        compiler_params=pltpu.CompilerParams(dimension_semantics=("parallel",), vmem_limit_bytes=VMEM_LIMIT_BYTES),
    )(core, own, got)


def chip_sum(name, mine, parts, place, buf, layer):
    _, hx, yd = parts.shape
    tr = _tile(hx, (256, 128, 64, 32, 16))

    def kern(place_ref, m_ref, p_ref, _, o_ref):
        total = None
        for q in range(N_CHIPS):
            term = jnp.where(place_ref[0] == q, m_ref[...], p_ref[q]).astype(F32)
            total = term if total is None else total + term
        o_ref[...] = total

    grid_spec = pltpu.PrefetchScalarGridSpec(
        num_scalar_prefetch=1, grid=(hx // tr,),
        in_specs=[pl.BlockSpec((None, tr, yd), lambda i, pr: (pr[0], i, 0)),
                  pl.BlockSpec((N_CHIPS, tr, yd), lambda i, pr: (0, i, 0)),
                  pl.BlockSpec(memory_space=pl.ANY)],
        out_specs=pl.BlockSpec((None, None, tr, yd), lambda i, pr: (layer, pr[1], i, 0)))
    return pl.pallas_call(
        kern, name=name, grid_spec=grid_spec, out_shape=jax.ShapeDtypeStruct(buf.shape, buf.dtype),
        input_output_aliases={3: 0},
        compiler_params=pltpu.CompilerParams(dimension_semantics=("parallel",), vmem_limit_bytes=VMEM_LIMIT_BYTES),
    )(place, mine, parts, buf)


WEIGHTS = ("g_mix_pre", "g_mix_post", "g_cross_pre", "g_mem", "g_cross_post", "g_ffn_pre", "g_ffn_post", "w_xq", "w_xkv",
           "w_xo", "w_ffn_gu", "w_ffn_down", "ab_w_in", "ab_b_f", "ab_conv_w", "ab_w_out", "c_w_in", "c_conv_w",
           "c_conv_b", "c_w_a", "c_b_a", "c_w_i", "c_b_i", "c_lam", "c_w_out")
SHARD_DIM = {"w_xq": 1, "w_xkv": 2, "w_xo": 1, "w_ffn_gu": 2, "w_ffn_down": 1, "ab_w_in": 2, "ab_conv_w": 2,
             "ab_w_out": 1, "c_w_in": 2, "c_conv_w": 2, "c_conv_b": 1, "c_w_a": 2, "c_b_a": 2, "c_w_i": 2, "c_b_i": 2,
             "c_lam": 1, "c_w_out": 1}
COMMON_BIG = ("w_xq", "w_xkv", "w_xo", "w_ffn_gu", "w_ffn_down")
EVEN_BIG, ODD_BIG = ("ab_w_in", "ab_w_out"), ("c_w_in", "c_w_a", "c_w_i", "c_w_out")
BIG = COMMON_BIG + EVEN_BIG + ODD_BIG


def layer_big(layer):
    return COMMON_BIG + (ODD_BIG if layer % 2 else EVEN_BIG)


SPLIT_LAYERS = (0,)


def chunk_names(layer, part):
    mixer = layer_big(layer)[len(COMMON_BIG):]
    if layer in SPLIT_LAYERS:
        return mixer if part == "mix" else COMMON_BIG
    return layer_big(layer) if part == "mix" else ()


SMALL_SHARDED = ("ab_conv_w", "c_conv_w", "c_conv_b", "c_b_a", "c_b_i", "c_lam")
REPLICATED = ("g_mix_pre", "g_mix_post", "g_cross_pre", "g_mem", "g_cross_post", "g_ffn_pre", "g_ffn_post", "ab_b_f")
PACK_COLS = 1024


def _unshard(g, d):
    shard = g.shape[1:]
    return jnp.moveaxis(g, 0, d).reshape(shard[:d] + (N_CHIPS * shard[d],) + shard[d + 1:])


def _shardify(full, d):
    s = full.shape
    return jnp.moveaxis(full.reshape(s[:d] + (N_CHIPS, s[d] // N_CHIPS) + s[d + 1:]), d, 0)


def _pack(arrays, rows):
    flat = jnp.concatenate([a.reshape(-1).astype(F32) for a in arrays])
    return jnp.pad(flat, (0, rows * PACK_COLS - flat.shape[0])).reshape(rows, PACK_COLS)


def _unpack(packed, shapes):
    flat = packed.reshape(-1)
    out, at = [], 0
    for s in shapes:
        size = math.prod(s)
        out.append(flat[at:at + size].reshape(s))
        at += size
    return out


def _rows_for(shapes):
    return -(-sum(math.prod(s) for s in shapes) // (8 * PACK_COLS)) * 8


def kernel(x, mem, g_mix_pre, g_mix_post, g_cross_pre, g_mem, g_cross_post, g_ffn_pre, g_ffn_post, w_xq, w_xkv, w_xo, w_ffn_gu, w_ffn_down, ab_w_in, ab_b_f, ab_conv_w, ab_w_out, c_w_in, c_conv_w, c_conv_b, c_w_a, c_b_a, c_w_i, c_b_i, c_lam, c_w_out, loss_target, m_g_mix_pre, m_g_mix_post, m_g_cross_pre, m_g_mem, m_g_cross_post, m_g_ffn_pre, m_g_ffn_post, m_w_xq, m_w_xkv, m_w_xo, m_w_ffn_gu, m_w_ffn_down, m_ab_w_in, m_ab_b_f, m_ab_conv_w, m_ab_w_out, m_c_w_in, m_c_conv_w, m_c_conv_b, m_c_w_a, m_c_b_a, m_c_w_i, m_c_b_i, m_c_lam, m_c_w_out, v_g_mix_pre, v_g_mix_post, v_g_cross_pre, v_g_mem, v_g_cross_post, v_g_ffn_pre, v_g_ffn_post, v_w_xq, v_w_xkv, v_w_xo, v_w_ffn_gu, v_w_ffn_down, v_ab_w_in, v_ab_b_f, v_ab_conv_w, v_ab_w_out, v_c_w_in, v_c_conv_w, v_c_conv_b, v_c_w_a, v_c_b_a, v_c_w_i, v_c_b_i, v_c_lam, v_c_w_out):
    given = dict(locals())
    w = {n: given[n] for n in WEIGHTS}
    m_in = {n: given["m_" + n] for n in WEIGHTS}
    v_in = {n: given["v_" + n] for n in WEIGHTS}
    xi, yi, ci = _place()
    chip = 2 * xi + yi

    full = {}
    small_shapes = [w[n].shape for n in SMALL_SHARDED]
    rows_w = _rows_for(small_shapes)
    assert rows_w * PACK_COLS > sum(math.prod(s) for s in small_shapes)
    every = gather_small("gather_small_weights", _pack([w[n] for n in SMALL_SHARDED], rows_w), reduce=False)
    per_chip = every[0::2].reshape(N_CHIPS, -1)
    at = 0
    for n, s in zip(SMALL_SHARDED, small_shapes):
        size = math.prod(s)
        full[n] = _unshard(per_chip[:, at:at + size].reshape(N_CHIPS, *s), SHARD_DIM[n])
        at += size
    for n in REPLICATED:
        full[n] = w[n]
    after_small = every[0, -1, -1].astype(BF16)

    depth = g_mix_pre.shape[0]
    own, gathers, tokens = {}, {}, []
    for layer in range(depth):
        for part in ("mix", "rest"):
            names = chunk_names(layer, part)
            if names:
                tagp = f"l{layer}_{part}"
                own[tagp] = {n: w[n][layer if n in COMMON_BIG else layer // 2].astype(BF16) + after_small for n in names}
                halves = [a.reshape(2, -1, a.shape[-1]) for a in own[tagp].values()]
                gathers[tagp], token = copies_start(f"gather_start_{tagp}", _gather_copies, halves,
                                                    [(N_CHIPS, *h.shape) for h in halves])
                tokens.append(token[0, 0])

    def layer_weights(layer, part, x_in):
        tagp = f"l{layer}_{part}"
        if tagp not in gathers:
            return {}
        shards, lands = copies_wait(f"gather_wait_{tagp}", _gather_copies, gathers[tagp], x_in)
        lands = pass_to_sibling(f"gather_pass_{tagp}", shards, lands)
        return {n: _unshard(g.reshape(N_CHIPS, *mine.shape), SHARD_DIM[n] - 1)
                for (n, mine), g in zip(own[tagp].items(), lands)}

    core_arr = ci.reshape(1).astype(jnp.int32)
    place_arr = jnp.stack([chip, ci]).astype(jnp.int32)
    in_flight, swapping, held = [], [], {}

    def exchange(after):
        layer, tagp, names, state = swapping.pop()
        mine, got = copies_wait(f"swap_wait_{tagp}", _swap_copies, state, after)
        sums = [pair_sum(f"pair_sum_{tagp}_{n}", o, g, core_arr) for n, o, g in zip(names, mine, got)]
        state, token = copies_start(f"exchange_start_{tagp}", _exchange_copies, sums, [b.shape for b in sums])
        in_flight.append((layer, tagp, names, state))
        return token

    def reduce_hook(layer, part, part_grads):
        held.update(part_grads)
        names = chunk_names(layer, part)
        if not names:
            return None
        tagp = f"l{layer}_{part}"
        mine = [held.pop(n) for n in names]
        state, token = copies_start(f"swap_start_{tagp}", _swap_copies, mine, [(m.shape[0], *m.shape[2:]) for m in mine])
        if swapping:
            token = exchange(token)
        swapping.append((layer, tagp, names, state))
        return token[0, 0]

    sq_cols, dx, grads, _ = local_step(x[0] + sum(tokens), mem[0], loss_target[0], full, layer_weights, reduce_hook)
    exchange(dx)
    loss = lax.psum(0.5 / D_MODEL * jnp.sum(sq_cols), ("x", "y", "c"))

    reduced = {n: lax.empty((w[n].shape[0], 2, math.prod(w[n].shape[1:-1]) // 2, w[n].shape[-1]), F32) for n in BIG}
    for layer, tagp, names, state in in_flight:
        sums, parts = copies_wait(f"exchange_wait_{tagp}", _exchange_copies, state, dx)
        for n, mine, p in zip(names, sums, parts):
            index = layer if n in COMMON_BIG else layer // 2
            reduced[n] = chip_sum(f"chip_sum_{tagp}_{n}", mine, p, place_arr, reduced[n], index)
    grad_out = {n: g.reshape(w[n].shape) for n, g in zip(BIG, share_halves([reduced[n] for n in BIG]))}

    small_names = REPLICATED + SMALL_SHARDED
    small_full_shapes = [grads[n].shape for n in small_names]
    total = gather_small("reduce_small_grads", _pack([grads[n] for n in small_names], _rows_for(small_full_shapes)),
                         reduce=True)
    for n, g in zip(small_names, _unpack(total, small_full_shapes)):
        if n in SHARD_DIM:
            g = lax.dynamic_index_in_dim(_shardify(g, SHARD_DIM[n]), chip, axis=0, keepdims=False)
        grad_out[n] = g

    delta, new_m, new_v = {}, {}, {}
    for n in BIG:
        two_d = lambda a: a.reshape(-1, a.shape[-1])
        d, m2, v2 = adamw(f"adamw_{n}", two_d(w[n]), two_d(grad_out[n]), two_d(m_in[n]), two_d(v_in[n]))
        delta[n], new_m[n], new_v[n] = (a.reshape(w[n].shape) for a in (d, m2, v2))
    shapes = [w[n].shape for n in small_names]
    rows = _rows_for(shapes)
    packed = [_pack([src[n] for n in small_names], rows) for src in (w, grad_out, m_in, v_in)]
    for dst, res in zip((delta, new_m, new_v), adamw("adamw_small", *packed)):
        for n, a in zip(small_names, _unpack(res, shapes)):
            dst[n] = a

    return (loss, dx[None], *[grad_out[n] for n in WEIGHTS], *[delta[n] for n in WEIGHTS],
            *[new_m[n] for n in WEIGHTS], *[new_v[n] for n in WEIGHTS])
```

```python
import functools
import math

import jax
import jax.numpy as jnp
from jax import lax
from jax.experimental import pallas as pl
from jax.experimental.pallas import tpu as pltpu

F32, BF16 = jnp.float32, jnp.bfloat16
D_MODEL = 1024
EPS = 1e-6
NEG_INF = -1e30
FOX_HEADS, FOX_HEAD_DIM, FOX_WIDTH = 8, 64, 512
SC_WIDTH = 512
AB_IN = 3 * FOX_WIDTH + FOX_HEADS + 3 * SC_WIDTH
AB_IN_PAD = 3200
LRU_BW, LRU_BLOCKS = 256, 4
RG_C = 8.0
MEM_HEADS, MEM_HEAD_DIM = 4, 256
ADAM_LR, ADAM_B1, ADAM_B2, ADAM_EPS, ADAM_WD, ADAM_STEP = 0.001, 0.9, 0.999, 1e-08, 0.01, 10
N_CHIPS = 4
MESH = pl.DeviceIdType.MESH
VMEM_LIMIT_BYTES = 48 * 1024 * 1024
MM_OPERAND_TILE_BYTES = 7 * 1024 * 1024

NN = (((1,), (0,)), ((), ()))
NT = (((1,), (1,)), ((), ()))
TN = (((0,), (0,)), ((), ()))


def _dot(a, b, dn=NN):
    return lax.dot_general(a.astype(BF16), b.astype(BF16), dn, preferred_element_type=F32)


def _tile(n, prefs):
    for p in prefs:
        if n % p == 0:
            return p
    return n


def _pcall(name, kern, grid, ins, in_specs, out_shape, out_specs, sem):
    return pl.pallas_call(
        kern, name=name, grid=grid, in_specs=in_specs, out_specs=out_specs, out_shape=out_shape,
        compiler_params=pltpu.CompilerParams(dimension_semantics=sem, vmem_limit_bytes=VMEM_LIMIT_BYTES),
    )(*ins)


def mm(name, a, b, mode, out_dtype, reduce_layout=False):
    if mode == "nn":
        (m, k), n = a.shape, b.shape[1]
    elif mode == "nt":
        (m, k), n = a.shape, b.shape[0]
    else:
        (k, m), n = a.shape, b.shape[1]
    if reduce_layout:
        tm, tn = m // 2, n // N_CHIPS
    else:
        tn = _tile(n, ((1024,) if mode == "tn" else ()) + (512, 640, 256, 128))
        tm = next(c for c in (2048, 1024, 512, 256, 128, m)
                  if m % c == 0 and 2 * c * k <= MM_OPERAND_TILE_BYTES and 4 * c * tn <= MM_OPERAND_TILE_BYTES)
    dn = {"nn": NN, "nt": NT, "tn": TN}[mode]

    def kern(a_ref, b_ref, o_ref):
        o_ref[...] = _dot(a_ref[...], b_ref[...], dn).astype(o_ref.dtype)

    a_spec = pl.BlockSpec((k, tm), lambda i, j: (0, i)) if mode == "tn" else pl.BlockSpec((tm, k), lambda i, j: (i, 0))
    b_spec = pl.BlockSpec((tn, k), lambda i, j: (j, 0)) if mode == "nt" else pl.BlockSpec((k, tn), lambda i, j: (0, j))
    if reduce_layout:
        out_shape = jax.ShapeDtypeStruct((N_CHIPS, 2, tm, tn), out_dtype)
        o_spec = pl.BlockSpec((None, None, tm, tn), lambda i, j: (j, i, 0, 0))
    else:
        out_shape = jax.ShapeDtypeStruct((m, n), out_dtype)
        o_spec = pl.BlockSpec((tm, tn), lambda i, j: (i, j))
    return _pcall(name, kern, (m // tm, n // tn), (a, b), [a_spec, b_spec], out_shape, o_spec, ("parallel", "parallel"))


def rowwise(name, body, rows, params, outs, accs=(), tr=256):
    t = rows[0].shape[0]
    tr = min(tr, t)
    nr, npar, no = len(rows), len(params), len(outs)

    def kern(*refs):
        acc_refs = refs[nr + npar + no:]
        if acc_refs:
            @pl.when(pl.program_id(0) == 0)
            def _():
                for ar in acc_refs:
                    ar[...] = jnp.zeros_like(ar)
        body(refs[:nr], refs[nr:nr + npar], refs[nr + npar:nr + npar + no], acc_refs)

    in_specs = [pl.BlockSpec((tr, x.shape[1]), lambda i: (i, 0)) for x in rows]
    in_specs += [pl.BlockSpec(p.shape, lambda i: (0, 0)) for p in params]
    out_specs = [pl.BlockSpec((tr, c), lambda i: (i, 0)) for c, _ in outs]
    out_specs += [pl.BlockSpec(s, lambda i: (0, 0)) for s in accs]
    out_shape = [jax.ShapeDtypeStruct((t, c), dt) for c, dt in outs]
    out_shape += [jax.ShapeDtypeStruct(s, F32) for s in accs]
    return _pcall(name, kern, (t // tr,), (*rows, *params), in_specs, out_shape, out_specs,
                  ("arbitrary",) if accs else ("parallel",))


def _rms_stats(x):
    r = lax.rsqrt(jnp.mean(x * x, axis=-1, keepdims=True) + EPS)
    return r, x * r


def _rms_bwd(xh, r, g, dy):
    dxh = dy * g
    dx = r * (dxh - xh * jnp.mean(dxh * xh, axis=-1, keepdims=True))
    return dx, jnp.sum(dy * xh, axis=0, keepdims=True)


def rms_pre(name, x, gains, layer):
    def body(r, p, o, a):
        _, xh = _rms_stats(r[0][...])
        o[0][...] = (xh * p[0][layer:layer + 1, :]).astype(BF16)
    return rowwise(name, body, [x], [gains], [(x.shape[1], BF16)])[0]


def post_add(name, x, y, gains, layer):
    def body(r, p, o, a):
        _, yh = _rms_stats(r[1][...])
        o[0][...] = r[0][...] + yh * p[0][layer:layer + 1, :]
    return rowwise(name, body, [x, y], [gains], [(x.shape[1], F32)])[0]


def post_bwd(name, y, dx, gains, layer):
    def body(r, p, o, a):
        rr, yh = _rms_stats(r[0][...])
        dy, dg = _rms_bwd(yh, rr, p[0][layer:layer + 1, :], r[1][...])
        o[0][...] = dy.astype(BF16)
        a[0][...] += dg
    c = y.shape[1]
    return rowwise(name, body, [y, dx], [gains], [(c, BF16)], [(1, c)])


def pre_bwd(name, x, dh, dx_res, gains, layer):
    def body(r, p, o, a):
        rr, xh = _rms_stats(r[0][...])
        dx, dg = _rms_bwd(xh, rr, p[0][layer:layer + 1, :], r[1][...])
        o[0][...] = r[2][...] + dx
        a[0][...] += dg
    c = x.shape[1]
    return rowwise(name, body, [x, dh, dx_res], [gains], [(c, F32)], [(1, c)])


def gain_bwd(name, x, dh):
    def body(r, p, o, a):
        _, xh = _rms_stats(r[0][...])
        a[0][...] += jnp.sum(r[1][...] * xh, axis=0, keepdims=True)
    return rowwise(name, body, [x, dh], [], [], [(1, x.shape[1])])[0]


def _sigmoid(z):
    return 1.0 / (1.0 + jnp.exp(-z))


def swiglu_fwd(name, gu):
    f = gu.shape[1] // 2

    def body(r, p, o, a):
        g = r[0][:, :f].astype(F32)
        u = r[0][:, f:].astype(F32)
        o[0][...] = (g * _sigmoid(g) * u).astype(BF16)
    return rowwise(name, body, [gu], [], [(f, BF16)])[0]


def swiglu_bwd(name, gu, da):
    f = gu.shape[1] // 2

    def body(r, p, o, a):
        g = r[0][:, :f].astype(F32)
        u = r[0][:, f:].astype(F32)
        d = r[1][...].astype(F32)
        sg = _sigmoid(g)
        o[0][:, :f] = (d * u * sg * (1.0 + g * (1.0 - sg))).astype(BF16)
        o[0][:, f:] = (d * g * sg).astype(BF16)
    return rowwise(name, body, [gu, da], [], [(2 * f, BF16)])[0]


def loss_head(name, y, target):
    c = y.shape[1]

    def body(r, p, o, a):
        e = r[0][...] - r[1][...]
        o[0][...] = e * (1.0 / c)
        a[0][...] += jnp.sum(e * e, axis=0, keepdims=True)
    return rowwise(name, body, [y, target], [], [(c, F32)], [(1, c)])


def adamw(name, w, g, m, v):
    c = w.shape[1]

    def body(r, p, o, a):
        wv, gv, mv, vv = (x[...] for x in r)
        m2 = ADAM_B1 * mv + (1.0 - ADAM_B1) * gv
        v2 = ADAM_B2 * vv + (1.0 - ADAM_B2) * (gv * gv)
        m_hat = m2 / (1.0 - ADAM_B1 ** ADAM_STEP)
        v_hat = v2 / (1.0 - ADAM_B2 ** ADAM_STEP)
        o[0][...] = -ADAM_LR * (m_hat / (jnp.sqrt(v_hat) + ADAM_EPS) + ADAM_WD * wv)
        o[1][...] = m2
        o[2][...] = v2
    tr = _tile(w.shape[0], (256, 128, 64, 32, 16, 8))
    return rowwise(name, body, [w, g, m, v], [], [(c, F32)] * 3, tr=tr)


def colwise(name, body, cols, params, outs, pouts=(), tc=128):
    t = cols[0][0].shape[0]
    c = params[0].shape[1] if params else cols[0][0].shape[1]
    nc, npar, no = len(cols), len(params), len(outs)

    def kern(*refs):
        body(refs[:nc], refs[nc:nc + npar], refs[nc + npar:nc + npar + no], refs[nc + npar + no:])

    in_specs = [pl.BlockSpec((t, tc), functools.partial(lambda j, off: (0, j + off), off=off)) for _, off in cols]
    in_specs += [pl.BlockSpec((p.shape[0], tc), lambda j: (0, j)) for p in params]
    out_specs = [pl.BlockSpec((t, tc), lambda j: (0, j)) for _ in outs]
    out_specs += [pl.BlockSpec((r, tc), lambda j: (0, j)) for r in pouts]
    out_shape = [jax.ShapeDtypeStruct((t, c), dt) for dt in outs]
    out_shape += [jax.ShapeDtypeStruct((r, c), F32) for r in pouts]
    return _pcall(name, kern, (c // tc,), (*[x for x, _ in cols], *params), in_specs, out_shape, out_specs,
                  ("parallel",))


def _row_index(shape):
    return lax.broadcasted_iota(jnp.int32, shape, 0)


def _shift_down(x, d, rows):
    return jnp.where(rows >= d, pltpu.roll(x, d, 0), 0.0)


def _shift_up(x, d, rows):
    t = x.shape[0]
    return jnp.where(rows < t - d, pltpu.roll(x, t - d, 0), 0.0)


def sconv_fwd(name, proj, col0, conv_w, tc=128):
    nb = SC_WIDTH // tc

    def body(cl, p, o, po):
        b, c, u = (x[...] for x in cl)
        rows = _row_index(b.shape)
        w = p[0][...]
        z = c * u
        conv = w[2:3] * z + w[1:2] * _shift_down(z, 1, rows) + w[0:1] * _shift_down(z, 2, rows)
        o[0][...] = (b * conv).astype(BF16)
    return colwise(name, body, [(proj, col0), (proj, col0 + nb), (proj, col0 + 2 * nb)], [conv_w], [BF16], tc=tc)[0]


def sconv_bwd(name, proj, col0, conv_w, dyb, dcol0, tc=128):
    nb = SC_WIDTH // tc

    def body(cl, p, o, po):
        b, c, u, dy = (x[...] for x in cl)
        rows = _row_index(b.shape)
        w = p[0][...]
        z = c * u
        z1, z2 = _shift_down(z, 1, rows), _shift_down(z, 2, rows)
        conv = w[2:3] * z + w[1:2] * z1 + w[0:1] * z2
        dconv = dy * b
        dz = w[2:3] * dconv + w[1:2] * _shift_up(dconv, 1, rows) + w[0:1] * _shift_up(dconv, 2, rows)
        o[0][...] = (dy * conv).astype(BF16)
        o[1][...] = (dz * u).astype(BF16)
        o[2][...] = (dz * c).astype(BF16)
        po[0][0:1, :] = jnp.sum(dconv * z2, axis=0, keepdims=True)
        po[0][1:2, :] = jnp.sum(dconv * z1, axis=0, keepdims=True)
        po[0][2:3, :] = jnp.sum(dconv * z, axis=0, keepdims=True)
    return colwise(name, body, [(proj, col0), (proj, col0 + nb), (proj, col0 + 2 * nb), (dyb, dcol0)], [conv_w],
                   [BF16, BF16, BF16], [3], tc=tc)


def _expm1(x):
    series = x * (1.0 + 0.5 * x * (1.0 + x * (1.0 / 3.0) * (1.0 + 0.25 * x * (1.0 + 0.2 * x))))
    return jnp.where(jnp.abs(x) < 0.05, series, jnp.exp(x) - 1.0)


def _log1p(x):
    series = x * (1.0 - x * (0.5 - x * (1.0 / 3.0 - 0.25 * x)))
    return jnp.where(jnp.abs(x) < 0.01, series, jnp.log(1.0 + x))


def _softplus_neg(lam):
    sp = jnp.maximum(-lam, 0.0) + _log1p(jnp.exp(-jnp.abs(lam)))
    return sp, -_sigmoid(-lam)


GELU_C = math.sqrt(2.0 / math.pi)


def _gelu(x):
    th = jnp.tanh(GELU_C * (x + 0.044715 * x * x * x))
    val = 0.5 * x * (1.0 + th)
    grad = 0.5 * (1.0 + th) + 0.5 * x * (1.0 - th * th) * GELU_C * (1.0 + 3.0 * 0.044715 * x * x)
    return val, grad


def rg_conv_fwd(name, gu2, conv_w, conv_b, tc=128):
    nb = D_MODEL // tc

    def body(cl, p, o, po):
        u = cl[0][...]
        rows = _row_index(u.shape)
        w = p[0][...]
        o[0][...] = (w[3:4] * u + w[2:3] * _shift_down(u, 1, rows) + w[1:2] * _shift_down(u, 2, rows)
                     + w[0:1] * _shift_down(u, 3, rows) + p[1][...])
    return colwise(name, body, [(gu2, nb)], [conv_w, conv_b], [F32], tc=tc)[0]


def rg_conv_bwd(name, gu2, duc, conv_w, tc=128):
    nb = D_MODEL // tc

    def body(cl, p, o, po):
        u, d = cl[0][...], cl[1][...]
        rows = _row_index(u.shape)
        w = p[0][...]
        o[0][...] = (w[3:4] * d + w[2:3] * _shift_up(d, 1, rows) + w[1:2] * _shift_up(d, 2, rows)
                     + w[0:1] * _shift_up(d, 3, rows)).astype(BF16)
        for k in range(4):
            uk = u if k == 3 else _shift_down(u, 3 - k, rows)
            po[0][k:k + 1, :] = jnp.sum(d * uk, axis=0, keepdims=True)
        po[1][...] = jnp.sum(d, axis=0, keepdims=True)
    return colwise(name, body, [(gu2, nb), (duc, 0)], [conv_w], [BF16], [4, 1], tc=tc)


def rg_gates_fwd(name, uc, w_a, b_a, w_i, b_i, tr=512):
    t = uc.shape[0]
    tr = min(tr, t)

    def kern(u_ref, wa_ref, ba_ref, wi_ref, bi_ref, r_ref, i_ref):
        ub = u_ref[...].astype(BF16)
        r_ref[...] = _sigmoid(_dot(ub, wa_ref[...]) + ba_ref[...])
        i_ref[...] = _sigmoid(_dot(ub, wi_ref[...]) + bi_ref[...])

    blk = pl.BlockSpec((tr, LRU_BW), lambda n, i: (i, n))
    wspec = pl.BlockSpec((None, LRU_BW, LRU_BW), lambda n, i: (n, 0, 0))
    bspec = pl.BlockSpec((1, LRU_BW), lambda n, i: (0, n))
    return _pcall(name, kern, (LRU_BLOCKS, t // tr), (uc, w_a, b_a, w_i, b_i), [blk, wspec, bspec, wspec, bspec],
                  [jax.ShapeDtypeStruct(uc.shape, F32)] * 2, [blk, blk], ("parallel", "parallel"))


def rg_gates_bwd(name, uc, dzr, dzi, duc_part, w_a, w_i):
    t = uc.shape[0]
    rows = LRU_BW // N_CHIPS

    def kern(u_ref, dr_ref, di_ref, dp_ref, wa_ref, wi_ref, duc_ref, dwa_ref, dwi_ref):
        ub = u_ref[...].astype(BF16)
        dr, di = dr_ref[...], di_ref[...]
        dwa, dwi = _dot(ub, dr, TN), _dot(ub, di, TN)
        for p in range(N_CHIPS):
            dwa_ref[p] = dwa[p * rows:(p + 1) * rows].astype(dwa_ref.dtype)
            dwi_ref[p] = dwi[p * rows:(p + 1) * rows].astype(dwi_ref.dtype)
        duc_ref[...] = dp_ref[...] + _dot(dr, wa_ref[...], NT) + _dot(di, wi_ref[...], NT)

    blk = pl.BlockSpec((t, LRU_BW), lambda n: (0, n))
    wspec = pl.BlockSpec((None, LRU_BW, LRU_BW), lambda n: (n, 0, 0))
    gspec = pl.BlockSpec((N_CHIPS, None, rows, LRU_BW), lambda n: (0, n, 0, 0))
    gshape = jax.ShapeDtypeStruct((N_CHIPS, LRU_BLOCKS, rows, LRU_BW), BF16)
    return _pcall(name, kern, (LRU_BLOCKS,), (uc, dzr, dzi, duc_part, w_a, w_i), [blk, blk, blk, blk, wspec, wspec],
                  [jax.ShapeDtypeStruct(uc.shape, F32), gshape, gshape], [blk, gspec, gspec], ("parallel",))


def _rg_decay(r, lam):
    sp, dsp = _softplus_neg(lam)
    la = -RG_C * r * sp
    a = jnp.exp(la)
    sq = jnp.sqrt(-_expm1(2.0 * la))
    return sp, dsp, a, sq


def rg_scan_fwd(name, gu2, uc, r, i, lam, tc=128):
    def body(cl, p, o, po):
        gate, ucv, rv, iv = (x[...] for x in cl)
        t = gate.shape[0]
        rows = _row_index(gate.shape)
        _, _, a, sq = _rg_decay(rv, p[0][...])
        b = sq * (iv * ucv)
        d = 1
        while d < t:
            keep = rows >= d
            b = a * jnp.where(keep, pltpu.roll(b, d, 0), 0.0) + b
            a = a * jnp.where(keep, pltpu.roll(a, d, 0), 1.0)
            d *= 2
        o[0][...] = (_gelu(gate)[0] * b).astype(BF16)
        o[1][...] = b
    return colwise(name, body, [(gu2, 0), (uc, 0), (r, 0), (i, 0)], [lam], [BF16, F32], tc=tc)


def rg_scan_bwd(name, gu2, uc, r, i, hs, dy, lam, tc=128):
    def body(cl, p, o, po):
        gate, ucv, rv, iv, h, dyv = (x[...] for x in cl)
        t = gate.shape[0]
        rows = _row_index(gate.shape)
        sp, dsp, a, sq = _rg_decay(rv, p[0][...])
        gl, dgl = _gelu(gate)
        o[0][...] = (dyv * h * dgl).astype(BF16)
        g = dyv * gl
        am = _shift_up(a, 1, rows)
        d = 1
        while d < t:
            keep = rows < t - d
            g = am * jnp.where(keep, pltpu.roll(g, t - d, 0), 0.0) + g
            am = am * jnp.where(keep, pltpu.roll(am, t - d, 0), 0.0)
            d *= 2
        da = g * _shift_down(h, 1, rows)
        iu = iv * ucv
        d_iu = g * sq
        dla = da * a - (g * iu) * (a * a) / sq
        dzr = dla * (-RG_C * sp) * rv * (1.0 - rv)
        dzi = d_iu * ucv * iv * (1.0 - iv)
        o[1][...] = dzr.astype(BF16)
        o[2][...] = dzi.astype(BF16)
        o[3][...] = d_iu * iv
        po[0][...] = jnp.sum(dzr, axis=0, keepdims=True)
        po[1][...] = jnp.sum(dzi, axis=0, keepdims=True)
        po[2][...] = jnp.sum(dla * rv, axis=0, keepdims=True) * (-RG_C) * dsp
    return colwise(name, body, [(gu2, 0), (uc, 0), (r, 0), (i, 0), (hs, 0), (dy, 0)], [lam],
                   [BF16, BF16, BF16, F32], [1, 1, 1], tc=tc)


def _split3(x):
    hi = x.astype(BF16)
    r1 = x - hi.astype(F32)
    mid = r1.astype(BF16)
    lo = (r1 - mid.astype(F32)).astype(BF16)
    return hi, mid, lo


def _tri_dot(x, tri):
    out = None
    for piece in _split3(x):
        term = lax.dot_general(piece, tri, NN, preferred_element_type=F32)
        out = term if out is None else out + term
    return out


def fox_gates_fwd(name, z_t, b_f):
    h, t = z_t.shape
    tb = min(512, t)

    def kern(z_ref, b_ref, o_ref):
        z = z_ref[...] + b_ref[...]
        logf = jnp.minimum(z, 0.0) - _log1p(jnp.exp(-jnp.abs(z)))
        src = lax.broadcasted_iota(jnp.int32, (t, tb), 0)
        dst = lax.broadcasted_iota(jnp.int32, (t, tb), 1) + pl.program_id(0) * tb
        o_ref[...] = _tri_dot(logf, (src <= dst).astype(BF16))

    return _pcall(name, kern, (t // tb,), (z_t, b_f),
                  [pl.BlockSpec((h, t), lambda j: (0, 0)), pl.BlockSpec((h, 1), lambda j: (0, 0))],
                  jax.ShapeDtypeStruct((h, t), F32), pl.BlockSpec((h, tb), lambda j: (0, j)), ("parallel",))


def fox_gates_bwd(name, z_t, b_f, dcum_t):
    h, t = z_t.shape
    tb = min(512, t)

    def kern(z_ref, b_ref, d_ref, dz_ref, db_ref):
        @pl.when(pl.program_id(0) == 0)
        def _():
            db_ref[...] = jnp.zeros_like(db_ref)
        src = lax.broadcasted_iota(jnp.int32, (t, tb), 0)
        dst = lax.broadcasted_iota(jnp.int32, (t, tb), 1) + pl.program_id(0) * tb
        dlogf = _tri_dot(d_ref[...], (src >= dst).astype(BF16))
        z = z_ref[...] + b_ref[...]
        dz = dlogf * _sigmoid(-z)
        dz_ref[...] = dz
        db_ref[...] += jnp.sum(dz, axis=1, keepdims=True)

    return _pcall(name, kern, (t // tb,), (z_t, b_f, dcum_t),
                  [pl.BlockSpec((h, tb), lambda j: (0, j)), pl.BlockSpec((h, 1), lambda j: (0, 0)),
                   pl.BlockSpec((h, t), lambda j: (0, 0))],
                  [jax.ShapeDtypeStruct((h, t), F32), jax.ShapeDtypeStruct((h, 1), F32)],
                  [pl.BlockSpec((h, tb), lambda j: (0, j)), pl.BlockSpec((h, 1), lambda j: (0, 0))], ("arbitrary",))


def _fox_spans(qs, k_ref, cr_ref, i, tq):
    n0 = i * tq
    sd = _dot(qs, k_ref[n0:n0 + tq, :], NT) - cr_ref[:, n0:n0 + tq]
    row = lax.broadcasted_iota(jnp.int32, (tq, tq), 0)
    col = lax.broadcasted_iota(jnp.int32, (tq, tq), 1)
    spans = [(n0, tq, jnp.where(row >= col, sd, NEG_INF))]
    if i > 0:
        spans.append((0, n0, _dot(qs, k_ref[0:n0, :], NT) - cr_ref[:, 0:n0]))
    return spans


def fox_fwd(name, q, k, v, cum_r, tq=256):
    h, t, dh = q.shape
    tq = min(tq, t)
    scale = FOX_HEAD_DIM ** -0.5

    def kern(q_ref, k_ref, v_ref, cr_ref, o_ref, lse_ref):
        for i in range(t // tq):
            rows = slice(i * tq, (i + 1) * tq)
            spans = _fox_spans(q_ref[rows, :] * scale, k_ref, cr_ref, i, tq)
            m = functools.reduce(jnp.maximum, [jnp.max(s, axis=-1, keepdims=True) for _, _, s in spans])
            l, acc = 0.0, 0.0
            for k0, kn, s in spans:
                p = jnp.exp(s - m)
                l = l + jnp.sum(p, axis=-1, keepdims=True)
                acc = acc + _dot(p, v_ref[k0:k0 + kn, :])
            o_ref[rows, :] = (acc / l).astype(o_ref.dtype)
            lse_ref[rows, :] = m + jnp.log(l)

    hspec = pl.BlockSpec((None, t, dh), lambda a: (a, 0, 0))
    cspec = pl.BlockSpec((None, t, 1), lambda a: (a, 0, 0))
    rspec = pl.BlockSpec((None, 1, t), lambda a: (a, 0, 0))
    return _pcall(name, kern, (h,), (q, k, v, cum_r), [hspec, hspec, hspec, rspec],
                  [jax.ShapeDtypeStruct((h, t, dh), BF16), jax.ShapeDtypeStruct((h, t, 1), F32)],
                  [hspec, cspec], ("parallel",))


def fox_bwd(name, q, k, v, do, lse, cum_r, tq=256):
    h, t, dh = q.shape
    tq = min(tq, t)
    scale = FOX_HEAD_DIM ** -0.5

    def kern(q_ref, k_ref, v_ref, do_ref, lse_ref, cr_ref, dq_ref, dk_ref, dv_ref, dc_ref):
        dk_ref[...] = jnp.zeros_like(dk_ref)
        dv_ref[...] = jnp.zeros_like(dv_ref)
        dc_ref[...] = jnp.zeros_like(dc_ref)
        for i in range(t // tq):
            rows = slice(i * tq, (i + 1) * tq)
            qs, dov, lse_v = q_ref[rows, :] * scale, do_ref[rows, :], lse_ref[rows, :]
            spans = _fox_spans(qs, k_ref, cr_ref, i, tq)
            probs = [jnp.exp(s - lse_v) for _, _, s in spans]
            dps = [_dot(dov, v_ref[k0:k0 + kn, :], NT) for k0, kn, _ in spans]
            rowdot = sum(jnp.sum(dp * p, axis=-1, keepdims=True) for dp, p in zip(dps, probs))
            dq = 0.0
            for (k0, kn, _), p, dp in zip(spans, probs, dps):
                ds = p * (dp - rowdot)
                dq = dq + _dot(ds, k_ref[k0:k0 + kn, :])
                dk_ref[k0:k0 + kn, :] += _dot(ds, qs, TN)
                dv_ref[k0:k0 + kn, :] += _dot(p, dov, TN)
                dc_ref[:, k0:k0 + kn] -= jnp.sum(ds, axis=0, keepdims=True)
            dq_ref[rows, :] = (dq * scale).astype(dq_ref.dtype)

    hspec = pl.BlockSpec((None, t, dh), lambda a: (a, 0, 0))
    cspec = pl.BlockSpec((None, t, 1), lambda a: (a, 0, 0))
    rspec = pl.BlockSpec((None, 1, t), lambda a: (a, 0, 0))
    return _pcall(name, kern, (h,), (q, k, v, do, lse, cum_r), [hspec, hspec, hspec, hspec, cspec, rspec],
                  [jax.ShapeDtypeStruct((h, t, dh), BF16), jax.ShapeDtypeStruct((h, t, dh), F32),
                   jax.ShapeDtypeStruct((h, t, dh), F32), jax.ShapeDtypeStruct((h, 1, t), F32)],
                  [hspec, hspec, hspec, rspec], ("parallel",))


def _xattn_probs(q, k):
    s = _dot(q, k, NT) * (MEM_HEAD_DIM ** -0.5)
    p = jnp.exp(s - jnp.max(s, axis=-1, keepdims=True))
    return p / jnp.sum(p, axis=-1, keepdims=True)


def xattn_fwd(name, q, kv, tq=512):
    t = q.shape[0]
    tq = min(tq, t)
    ml = kv.shape[0]

    def kern(q_ref, k_ref, v_ref, o_ref):
        o_ref[...] = _dot(_xattn_probs(q_ref[...], k_ref[...]), v_ref[...]).astype(o_ref.dtype)

    qspec = pl.BlockSpec((tq, MEM_HEAD_DIM), lambda i, a: (i, a))
    return _pcall(name, kern, (t // tq, MEM_HEADS), (q, kv, kv),
                  [qspec, pl.BlockSpec((ml, MEM_HEAD_DIM), lambda i, a: (0, a)),
                   pl.BlockSpec((ml, MEM_HEAD_DIM), lambda i, a: (0, MEM_HEADS + a))],
                  jax.ShapeDtypeStruct(q.shape, BF16), qspec, ("parallel", "parallel"))


def xattn_bwd(name, q, kv, do, tq=512):
    t = q.shape[0]
    tq = min(tq, t)
    ml = kv.shape[0]
    scale = MEM_HEAD_DIM ** -0.5

    def kern(q_ref, k_ref, v_ref, do_ref, dq_ref, dk_ref, dv_ref):
        @pl.when(pl.program_id(1) == 0)
        def _():
            dk_ref[...] = jnp.zeros_like(dk_ref)
            dv_ref[...] = jnp.zeros_like(dv_ref)
        qv, kv_, dov = q_ref[...], k_ref[...], do_ref[...]
        p = _xattn_probs(qv, kv_)
        dp = _dot(dov, v_ref[...], NT)
        ds = p * (dp - jnp.sum(dp * p, axis=-1, keepdims=True)) * scale
        dq_ref[...] = _dot(ds, kv_).astype(dq_ref.dtype)
        dk_ref[...] += _dot(ds, qv, TN)
        dv_ref[...] += _dot(p, dov, TN)

    qspec = pl.BlockSpec((tq, MEM_HEAD_DIM), lambda a, i: (i, a))
    kspec = pl.BlockSpec((ml, MEM_HEAD_DIM), lambda a, i: (0, a))
    return _pcall(name, kern, (MEM_HEADS, t // tq), (q, kv, kv, do),
                  [qspec, kspec, pl.BlockSpec((ml, MEM_HEAD_DIM), lambda a, i: (0, MEM_HEADS + a)), qspec],
                  [jax.ShapeDtypeStruct(q.shape, BF16), jax.ShapeDtypeStruct((ml, D_MODEL), F32),
                   jax.ShapeDtypeStruct((ml, D_MODEL), F32)],
                  [qspec, kspec, kspec], ("parallel", "arbitrary"))


def _heads(x):
    t = x.shape[0]
    return x.reshape(t, FOX_HEADS, FOX_HEAD_DIM).transpose(1, 0, 2)


def _unheads(x):
    return x.transpose(1, 0, 2).reshape(x.shape[1], FOX_WIDTH)


def _row_cut(dw):
    return dw.reshape(N_CHIPS, 2, dw.shape[0] // (2 * N_CHIPS), dw.shape[1])


def local_step(x, mem, target, w, layer_weights=None, reduce_hook=None):
    depth = w["g_mix_pre"].shape[0]
    t = x.shape[0]
    saved = []
    i1, i2, i3 = 3 * FOX_WIDTH, 3 * FOX_WIDTH + FOX_HEADS, AB_IN
    ncol = 128

    def stacked_weights(layer, part, _):
        names = COMMON_BIG if part == "rest" else layer_big(layer)[len(COMMON_BIG):]
        return {n: w[n][layer if n in COMMON_BIG else layer // 2] for n in names}

    get_weights = layer_weights or stacked_weights
    for layer in range(depth):
        lw = dict(get_weights(layer, "mix", x))
        s = {"x0": x, "lw": lw}
        tag = f"l{layer}"
        h1 = rms_pre(f"{tag}_mix_pre", x, w["g_mix_pre"], layer)
        s["h1"] = h1
        if layer % 2 == 0:
            e = layer // 2
            w_in = jnp.pad(lw["ab_w_in"], ((0, 0), (0, AB_IN_PAD - AB_IN)))
            proj = mm(f"{tag}_ab_in", h1, w_in, "nn", F32)
            qkv = proj[:, :i1].astype(BF16).reshape(t, 3, FOX_HEADS, FOX_HEAD_DIM).transpose(1, 2, 0, 3)
            z_t = proj[:, i1:i2].T
            b_f = w["ab_b_f"][e].reshape(FOX_HEADS, 1)
            cum_t = fox_gates_fwd(f"{tag}_fox_gates", z_t, b_f)
            cum_r = cum_t[:, None, :]
            oh, lse = fox_fwd(f"{tag}_fox", qkv[0], qkv[1], qkv[2], cum_r)
            bcu = proj[:, i2:i3]
            y_b = sconv_fwd(f"{tag}_sconv", bcu, 0, w["ab_conv_w"][e])
            ycat = jnp.concatenate([_unheads(oh), y_b], axis=1)
            y1 = mm(f"{tag}_ab_out", ycat, lw["ab_w_out"], "nn", F32)
            s.update(w_in=w_in, qkv=qkv, z_t=z_t, b_f=b_f, cum_r=cum_r, lse=lse, bcu=bcu, ycat=ycat)
        else:
            o = layer // 2
            gu2 = mm(f"{tag}_c_in", h1, lw["c_w_in"], "nn", F32)
            conv_b = w["c_conv_b"][o].reshape(1, -1)
            uc = rg_conv_fwd(f"{tag}_rg_conv", gu2, w["c_conv_w"][o], conv_b)
            b_a, b_i = w["c_b_a"][o].reshape(1, -1), w["c_b_i"][o].reshape(1, -1)
            r, i = rg_gates_fwd(f"{tag}_rg_gates", uc, lw["c_w_a"], b_a, lw["c_w_i"], b_i)
            lam = w["c_lam"][o].reshape(1, -1)
            ymix, hs = rg_scan_fwd(f"{tag}_rg_scan", gu2, uc, r, i, lam)
            y1 = mm(f"{tag}_c_out", ymix, lw["c_w_out"], "nn", F32)
            s.update(gu2=gu2, uc=uc, r=r, i=i, lam=lam, hs=hs, ymix=ymix)
        s["y1"] = y1
        x = post_add(f"{tag}_mix_post", x, y1, w["g_mix_post"], layer)
        lw.update(get_weights(layer, "rest", x))
        s["x1"] = x
        h2 = rms_pre(f"{tag}_cross_pre", x, w["g_cross_pre"], layer)
        m = rms_pre(f"{tag}_mem_pre", mem, w["g_mem"], layer)
        q = mm(f"{tag}_xq", h2, lw["w_xq"], "nn", BF16)
        kv = mm(f"{tag}_xkv", m, lw["w_xkv"], "nn", BF16)
        o_att = xattn_fwd(f"{tag}_xattn", q, kv)
        y2 = mm(f"{tag}_xo", o_att, lw["w_xo"], "nn", F32)
        s.update(h2=h2, m=m, q=q, kv=kv, o_att=o_att, y2=y2)
        x = post_add(f"{tag}_cross_post", x, y2, w["g_cross_post"], layer)
        s["x2"] = x
        h3 = rms_pre(f"{tag}_ffn_pre", x, w["g_ffn_pre"], layer)
        gu = mm(f"{tag}_ffn_gu", h3, lw["w_ffn_gu"], "nn", BF16)
        act = swiglu_fwd(f"{tag}_swiglu", gu)
        y3 = mm(f"{tag}_ffn_down", act, lw["w_ffn_down"], "nn", F32)
        s.update(h3=h3, gu=gu, act=act, y3=y3)
        x = post_add(f"{tag}_ffn_post", x, y3, w["g_ffn_post"], layer)
        saved.append(s)

    dx, sq_cols = loss_head("loss_head", x, target)

    grads = {k: [None] * v.shape[0] for k, v in w.items() if k not in BIG}
    big, token = {}, None

    def dw(name, a, b, cols_cut=False):
        return mm(name, a, b, "tn", BF16, reduce_layout=True) if cols_cut else _row_cut(mm(name, a, b, "tn", BF16))

    for layer in reversed(range(depth)):
        s = saved[layer]
        lw = s["lw"]
        tag = f"b{layer}"
        lg = {}
        g_ffn_post = w["g_ffn_post"] if token is None else w["g_ffn_post"] + token
        dy3, grads["g_ffn_post"][layer] = post_bwd(f"{tag}_ffn_post", s["y3"], dx, g_ffn_post, layer)
        dact = mm(f"{tag}_ffn_down_dx", dy3, lw["w_ffn_down"], "nt", BF16)
        lg["w_ffn_down"] = dw(f"{tag}_ffn_down_dw", s["act"], dy3)
        dgu = swiglu_bwd(f"{tag}_swiglu", s["gu"], dact)
        dh3 = mm(f"{tag}_ffn_gu_dx", dgu, lw["w_ffn_gu"], "nt", F32)
        lg["w_ffn_gu"] = dw(f"{tag}_ffn_gu_dw", s["h3"], dgu, cols_cut=True)
        dx, grads["g_ffn_pre"][layer] = pre_bwd(f"{tag}_ffn_pre", s["x2"], dh3, dx, w["g_ffn_pre"], layer)
        token = None if reduce_hook is None else reduce_hook(layer, "ffn", {}, dx)
        g_cross_post = w["g_cross_post"] if token is None else w["g_cross_post"] + token
        dy2, grads["g_cross_post"][layer] = post_bwd(f"{tag}_cross_post", s["y2"], dx, g_cross_post, layer)
        do = mm(f"{tag}_xo_dx", dy2, lw["w_xo"], "nt", BF16)
        lg["w_xo"] = dw(f"{tag}_xo_dw", s["o_att"], dy2)
        dq, dk, dv = xattn_bwd(f"{tag}_xattn", s["q"], s["kv"], do)
        dh2 = mm(f"{tag}_xq_dx", dq, lw["w_xq"], "nt", F32)
        lg["w_xq"] = dw(f"{tag}_xq_dw", s["h2"], dq)
        dkv = jnp.concatenate([dk, dv], axis=1).astype(BF16)
        dm = mm(f"{tag}_xkv_dx", dkv, lw["w_xkv"], "nt", F32)
        lg["w_xkv"] = dw(f"{tag}_xkv_dw", s["m"], dkv, cols_cut=True)
        grads["g_mem"][layer] = gain_bwd(f"{tag}_mem_pre", mem, dm)
        dx, grads["g_cross_pre"][layer] = pre_bwd(f"{tag}_cross_pre", s["x1"], dh2, dx, w["g_cross_pre"], layer)
        token = None if reduce_hook is None else reduce_hook(layer, "rest", lg, dx)
        g_mix_post = w["g_mix_post"] if token is None else w["g_mix_post"] + token
        dy1, grads["g_mix_post"][layer] = post_bwd(f"{tag}_mix_post", s["y1"], dx, g_mix_post, layer)
        rest_grads, lg = lg, {}
        if layer % 2 == 0:
            e = layer // 2
            dycat = mm(f"{tag}_ab_out_dx", dy1, lw["ab_w_out"], "nt", F32)
            token = None if reduce_hook is None else reduce_hook(layer, "mixer", {}, dycat)
            lg["ab_w_out"] = dw(f"{tag}_ab_out_dw", s["ycat"], dy1)
            do_h = _heads(dycat[:, :FOX_WIDTH].astype(BF16))
            qkv = s["qkv"]
            dqh, dkh, dvh, dcum = fox_bwd(f"{tag}_fox", qkv[0], qkv[1], qkv[2], do_h, s["lse"], s["cum_r"])
            dz_t, db_f = fox_gates_bwd(f"{tag}_fox_gates", s["z_t"], s["b_f"], dcum.reshape(FOX_HEADS, t))
            grads["ab_b_f"][e] = db_f.reshape(FOX_HEADS)
            db, dc, du, dconv_w = sconv_bwd(f"{tag}_sconv", s["bcu"], 0, w["ab_conv_w"][e], dycat, FOX_WIDTH // ncol)
            grads["ab_conv_w"][e] = dconv_w
            dproj = jnp.concatenate(
                [_unheads(dqh), _unheads(dkh).astype(BF16), _unheads(dvh).astype(BF16), dz_t.T.astype(BF16), db, dc, du,
                 jnp.zeros((t, AB_IN_PAD - AB_IN), BF16)], axis=1)
            dh1 = mm(f"{tag}_ab_in_dx", dproj, s["w_in"], "nt", F32)
            dw_in = mm(f"{tag}_ab_in_dw", s["h1"], dproj, "tn", F32)[:, :AB_IN]
            lg["ab_w_in"] = dw_in.reshape(2, D_MODEL // 2, N_CHIPS, AB_IN // N_CHIPS).transpose(2, 0, 1, 3).astype(BF16)
        else:
            o = layer // 2
            dymix = mm(f"{tag}_c_out_dx", dy1, lw["c_w_out"], "nt", F32)
            token = None if reduce_hook is None else reduce_hook(layer, "mixer", {}, dymix)
            lg["c_w_out"] = dw(f"{tag}_c_out_dw", s["ymix"], dy1)
            dgate, dzr, dzi, duc_part, db_a, db_i, dlam = rg_scan_bwd(
                f"{tag}_rg_scan", s["gu2"], s["uc"], s["r"], s["i"], s["hs"], dymix, s["lam"])
            duc, dw_a, dw_i = rg_gates_bwd(f"{tag}_rg_gates", s["uc"], dzr, dzi, duc_part, lw["c_w_a"], lw["c_w_i"])
            lg["c_w_a"] = dw_a.reshape(N_CHIPS, 2, LRU_BW // 2, LRU_BW)
            lg["c_w_i"] = dw_i.reshape(N_CHIPS, 2, LRU_BW // 2, LRU_BW)
            du_raw, dconv_w, dconv_b = rg_conv_bwd(f"{tag}_rg_conv", s["gu2"], duc, w["c_conv_w"][o])
            grads["c_b_a"][o] = db_a.reshape(LRU_BLOCKS, LRU_BW)
            grads["c_b_i"][o] = db_i.reshape(LRU_BLOCKS, LRU_BW)
            grads["c_lam"][o] = dlam.reshape(-1)
            grads["c_conv_w"][o] = dconv_w
            grads["c_conv_b"][o] = dconv_b.reshape(-1)
            dgu2 = jnp.concatenate([dgate, du_raw], axis=1)
            dh1 = mm(f"{tag}_c_in_dx", dgu2, lw["c_w_in"], "nt", F32)
            lg["c_w_in"] = dw(f"{tag}_c_in_dw", s["h1"], dgu2, cols_cut=True)
        g_mix_pre = w["g_mix_pre"] if token is None else w["g_mix_pre"] + token
        dx, grads["g_mix_pre"][layer] = pre_bwd(f"{tag}_mix_pre", s["x0"], dh1, dx, g_mix_pre, layer)
        if reduce_hook is None:
            big[layer] = {**rest_grads, **lg}
        else:
            token = reduce_hook(layer, "mix", lg, dx)

    for k in list(grads):
        if k.startswith("g_"):
            grads[k] = [g.reshape(-1) for g in grads[k]]
        grads[k] = jnp.stack(grads[k])
    return sq_cols, dx, grads, big


CHIP_FLIPS = ((1, 0), (0, 1), (1, 1))
HBM_SPEC = pl.BlockSpec(memory_space=pltpu.HBM)
VMEM_SPEC = pl.BlockSpec(memory_space=pltpu.VMEM)


def _place():
    return lax.axis_index("x"), lax.axis_index("y"), lax.axis_index("c")


def _flip(v, f):
    return 1 - v if f else v


def _remote(src, dst, send_sem, recv_sem, target):
    return pltpu.make_async_remote_copy(src_ref=src, dst_ref=dst, send_sem=send_sem, recv_sem=recv_sem,
                                        device_id=target, device_id_type=MESH)


SEM_SPEC = pl.BlockSpec(memory_space=pltpu.SEMAPHORE)


def _swap_copies(srcs, lands, send_sems, recv_sems):
    x, y, c = _place()
    return [_remote(src.at[:, 1 - c], land, send_sems.at[len(CHIP_FLIPS) * a], recv_sems.at[len(CHIP_FLIPS) * a],
                    (x, y, 1 - c)) for a, (src, land) in enumerate(zip(srcs, lands))]


def _exchange_copies(srcs, lands, send_sems, recv_sems):
    x, y, c = _place()
    p = 2 * x + y
    cps = []
    for a, (src, land) in enumerate(zip(srcs, lands)):
        for k, (fx, fy) in enumerate(CHIP_FLIPS):
            qx, qy = _flip(x, fx), _flip(y, fy)
            sem = len(CHIP_FLIPS) * a + k
            cps.append(_remote(src.at[2 * qx + qy], land.at[p], send_sems.at[sem], recv_sems.at[sem], (qx, qy, c)))
    return cps


def _gather_copies(srcs, lands, send_sems, recv_sems):
    x, y, c = _place()
    p = 2 * x + y
    cps = []
    for a, (src, land) in enumerate(zip(srcs, lands)):
        for k, (fx, fy) in enumerate(CHIP_FLIPS):
            sem = len(CHIP_FLIPS) * a + k
            cps.append(_remote(src.at[c], land.at[p, c], send_sems.at[sem], recv_sems.at[sem],
                               (_flip(x, fx), _flip(y, fy), c)))
    return cps


def copies_start(name, make_copies, srcs, land_shapes):
    n = len(srcs)

    def body(*refs):
        for cp in make_copies(refs[:n], refs[n:2 * n], refs[2 * n], refs[2 * n + 1]):
            cp.start()
        refs[-1][...] = jnp.zeros_like(refs[-1])

    thru = [pltpu.HBM(b.shape, b.dtype) for b in srcs] + [pltpu.HBM(sh, b.dtype) for sh, b in zip(land_shapes, srcs)]
    outs = pl.pallas_call(
        body, name=name, in_specs=[HBM_SPEC] * (2 * n),
        out_shape=(pltpu.SemaphoreType.DMA((3 * n,)), pltpu.SemaphoreType.DMA((3 * n,)), *thru,
                   jax.ShapeDtypeStruct((8, 128), F32)),
        out_specs=(SEM_SPEC, SEM_SPEC, *[HBM_SPEC] * (2 * n), VMEM_SPEC),
        input_output_aliases={i: 2 + i for i in range(2 * n)},
        compiler_params=pltpu.CompilerParams(has_side_effects=pltpu.SideEffectType.DATAFLOW_SIDE_EFFECTING),
    )(*[pltpu.with_memory_space_constraint(b, pltpu.HBM) for b in srcs],
      *[pltpu.with_memory_space_constraint(lax.empty(sh, b.dtype), pltpu.HBM) for sh, b in zip(land_shapes, srcs)])
    return outs[:-1], outs[-1]


def copies_wait(name, make_copies, state, after):
    send_sems, recv_sems, *thru = state
    n = len(thru) // 2

    def body(*refs):
        for cp in make_copies(refs[:n], refs[n:2 * n], refs[2 * n], refs[2 * n + 1]):
            cp.wait_send()
            cp.wait_recv()

    outs = pl.pallas_call(
        body, name=name, in_specs=[HBM_SPEC] * (2 * n) + [SEM_SPEC, SEM_SPEC, pl.BlockSpec(memory_space=pl.ANY)],
        out_shape=tuple(pltpu.HBM(t.shape, t.dtype) for t in thru), out_specs=tuple([HBM_SPEC] * (2 * n)),
        input_output_aliases={i: i for i in range(2 * n)},
        compiler_params=pltpu.CompilerParams(has_side_effects=pltpu.SideEffectType.DATAFLOW_SIDE_EFFECTING),
    )(*thru, send_sems, recv_sems, after)
    return outs[:n], outs[n:]


def pass_to_sibling(name, shards, lands):
    n = len(lands)

    def body(*refs):
        own, ins, outs = refs[:n], refs[n:2 * n], refs[2 * n:3 * n]
        send_sems, recv_sems = refs[3 * n:]
        x, y, c = _place()
        sibling = (x, y, 1 - c)
        cps = []
        for a in range(n):
            for k, (fx, fy) in enumerate(CHIP_FLIPS):
                q = 2 * _flip(x, fx) + _flip(y, fy)
                cps.append(_remote(ins[a].at[q, c], outs[a].at[q, c], send_sems.at[a, k], recv_sems.at[a, k], sibling))
            cps.append(_remote(own[a], outs[a].at[2 * x + y], send_sems.at[a, 3], recv_sems.at[a, 3], sibling))
        for cp in cps:
            cp.start()
        for cp in cps:
            cp.wait()

    return pl.pallas_call(
        body, name=name, in_specs=[HBM_SPEC] * (2 * n), out_specs=[HBM_SPEC] * n,
        out_shape=[jax.ShapeDtypeStruct(b.shape, b.dtype) for b in lands],
        scratch_shapes=[pltpu.SemaphoreType.DMA((n, 4)), pltpu.SemaphoreType.DMA((n, 4))],
        input_output_aliases={n + i: i for i in range(n)},
    )(*shards, *lands)


def share_halves(bufs):
    n = len(bufs)

    def body(*refs):
        ins, outs = refs[:n], refs[n:2 * n]
        send_sems, recv_sems = refs[2 * n:]
        x, y, c = _place()
        cps = [_remote(ins[a].at[:, c], outs[a].at[:, c], send_sems.at[a], recv_sems.at[a], (x, y, 1 - c))
               for a in range(n)]
        for cp in cps:
            cp.start()
        for cp in cps:
            cp.wait()

    return pl.pallas_call(
        body, name="share_reduced_halves", in_specs=[HBM_SPEC] * n, out_specs=[HBM_SPEC] * n,
        out_shape=[jax.ShapeDtypeStruct(b.shape, b.dtype) for b in bufs],
        scratch_shapes=[pltpu.SemaphoreType.DMA((n,)), pltpu.SemaphoreType.DMA((n,))],
        input_output_aliases={i: i for i in range(n)},
    )(*bufs)


DEVICE_FLIPS = tuple((fx, fy, fc) for fx in (0, 1) for fy in (0, 1) for fc in (0, 1))[1:]


def gather_small(name, v, reduce):
    r, cdim = v.shape
    n_dev = 8

    def body(v_ref, out_ref, *scratch):
        buf = scratch[0] if reduce else out_ref
        send_sems, recv_sems = scratch[-2:]
        x, y, c = _place()
        me = 4 * x + 2 * y + c
        buf[me] = v_ref[...]
        cps = []
        for k, (fx, fy, fc) in enumerate(DEVICE_FLIPS):
            cps.append(_remote(v_ref, buf.at[me], send_sems.at[k], recv_sems.at[k],
                               (_flip(x, fx), _flip(y, fy), _flip(c, fc))))
        for cp in cps:
            cp.start()
        for cp in cps:
            cp.wait()
        if reduce:
            total = buf[0]
            for d in range(1, n_dev):
                total = total + buf[d]
            out_ref[...] = total

    scratch = [pltpu.SemaphoreType.DMA((7,)), pltpu.SemaphoreType.DMA((7,))]
    if reduce:
        scratch = [pltpu.VMEM((n_dev, r, cdim), F32)] + scratch
    out_shape = jax.ShapeDtypeStruct((r, cdim) if reduce else (n_dev, r, cdim), F32)
    return pl.pallas_call(body, name=name, in_specs=[VMEM_SPEC], out_specs=VMEM_SPEC, out_shape=out_shape,
                          scratch_shapes=scratch)(v)


def pair_sum(name, own, got, core):
    _, hx, cols = got.shape
    tr = _tile(hx, (256, 128, 64, 32, 16))

    def kern(core_ref, a_ref, b_ref, o_ref):
        o_ref[...] = (a_ref[...].astype(F32) + b_ref[...].astype(F32)).astype(BF16)

    grid_spec = pltpu.PrefetchScalarGridSpec(
        num_scalar_prefetch=1, grid=(hx // tr,),
        in_specs=[pl.BlockSpec((N_CHIPS, None, tr, cols), lambda i, cr: (0, cr[0], i, 0)),
                  pl.BlockSpec((N_CHIPS, tr, cols), lambda i, cr: (0, i, 0))],
        out_specs=pl.BlockSpec((N_CHIPS, tr, cols), lambda i, cr: (0, i, 0)))
    return pl.pallas_call(
        kern, name=name, grid_spec=grid_spec, out_shape=jax.ShapeDtypeStruct(got.shape, BF16),
        compiler_params=pltpu.CompilerParams(dimension_semantics=("parallel",), vmem_limit_bytes=VMEM_LIMIT_BYTES),
    )(core, own, got)


def chip_sum(name, mine, parts, place, buf, layer):
    _, hx, yd = parts.shape
    tr = _tile(hx, (256, 128, 64, 32, 16))

    def kern(place_ref, m_ref, p_ref, _, o_ref):
        total = None
        for q in range(N_CHIPS):
            term = jnp.where(place_ref[0] == q, m_ref[...], p_ref[q]).astype(F32)
            total = term if total is None else total + term
        o_ref[...] = total

    grid_spec = pltpu.PrefetchScalarGridSpec(
        num_scalar_prefetch=1, grid=(hx // tr,),
        in_specs=[pl.BlockSpec((None, tr, yd), lambda i, pr: (pr[0], i, 0)),
                  pl.BlockSpec((N_CHIPS, tr, yd), lambda i, pr: (0, i, 0)),
                  pl.BlockSpec(memory_space=pl.ANY)],
        out_specs=pl.BlockSpec((None, None, tr, yd), lambda i, pr: (layer, pr[1], i, 0)))
    return pl.pallas_call(
        kern, name=name, grid_spec=grid_spec, out_shape=jax.ShapeDtypeStruct(buf.shape, buf.dtype),
        input_output_aliases={3: 0},
        compiler_params=pltpu.CompilerParams(dimension_semantics=("parallel",), vmem_limit_bytes=VMEM_LIMIT_BYTES),
    )(place, mine, parts, buf)


WEIGHTS = ("g_mix_pre", "g_mix_post", "g_cross_pre", "g_mem", "g_cross_post", "g_ffn_pre", "g_ffn_post", "w_xq", "w_xkv",
           "w_xo", "w_ffn_gu", "w_ffn_down", "ab_w_in", "ab_b_f", "ab_conv_w", "ab_w_out", "c_w_in", "c_conv_w",
           "c_conv_b", "c_w_a", "c_b_a", "c_w_i", "c_b_i", "c_lam", "c_w_out")
SHARD_DIM = {"w_xq": 1, "w_xkv": 2, "w_xo": 1, "w_ffn_gu": 2, "w_ffn_down": 1, "ab_w_in": 2, "ab_conv_w": 2,
             "ab_w_out": 1, "c_w_in": 2, "c_conv_w": 2, "c_conv_b": 1, "c_w_a": 2, "c_b_a": 2, "c_w_i": 2, "c_b_i": 2,
             "c_lam": 1, "c_w_out": 1}
COMMON_BIG = ("w_xq", "w_xkv", "w_xo", "w_ffn_gu", "w_ffn_down")
EVEN_BIG, ODD_BIG = ("ab_w_in", "ab_w_out"), ("c_w_in", "c_w_a", "c_w_i", "c_w_out")
BIG = COMMON_BIG + EVEN_BIG + ODD_BIG


def layer_big(layer):
    return COMMON_BIG + (ODD_BIG if layer % 2 else EVEN_BIG)


SPLIT_LAYERS = (0,)


def chunk_names(layer, part):
    mixer = layer_big(layer)[len(COMMON_BIG):]
    if layer in SPLIT_LAYERS:
        return mixer if part == "mix" else COMMON_BIG
    return layer_big(layer) if part == "mix" else ()


SMALL_SHARDED = ("ab_conv_w", "c_conv_w", "c_conv_b", "c_b_a", "c_b_i", "c_lam")
REPLICATED = ("g_mix_pre", "g_mix_post", "g_cross_pre", "g_mem", "g_cross_post", "g_ffn_pre", "g_ffn_post", "ab_b_f")
PACK_COLS = 1024


def _unshard(g, d):
    shard = g.shape[1:]
    return jnp.moveaxis(g, 0, d).reshape(shard[:d] + (N_CHIPS * shard[d],) + shard[d + 1:])


def _shardify(full, d):
    s = full.shape
    return jnp.moveaxis(full.reshape(s[:d] + (N_CHIPS, s[d] // N_CHIPS) + s[d + 1:]), d, 0)


def _pack(arrays, rows):
    flat = jnp.concatenate([a.reshape(-1).astype(F32) for a in arrays])
    return jnp.pad(flat, (0, rows * PACK_COLS - flat.shape[0])).reshape(rows, PACK_COLS)


def _unpack(packed, shapes):
    flat = packed.reshape(-1)
    out, at = [], 0
    for s in shapes:
        size = math.prod(s)
        out.append(flat[at:at + size].reshape(s))
        at += size
    return out


def _rows_for(shapes):
    return -(-sum(math.prod(s) for s in shapes) // (8 * PACK_COLS)) * 8


def kernel(x, mem, g_mix_pre, g_mix_post, g_cross_pre, g_mem, g_cross_post, g_ffn_pre, g_ffn_post, w_xq, w_xkv, w_xo, w_ffn_gu, w_ffn_down, ab_w_in, ab_b_f, ab_conv_w, ab_w_out, c_w_in, c_conv_w, c_conv_b, c_w_a, c_b_a, c_w_i, c_b_i, c_lam, c_w_out, loss_target, m_g_mix_pre, m_g_mix_post, m_g_cross_pre, m_g_mem, m_g_cross_post, m_g_ffn_pre, m_g_ffn_post, m_w_xq, m_w_xkv, m_w_xo, m_w_ffn_gu, m_w_ffn_down, m_ab_w_in, m_ab_b_f, m_ab_conv_w, m_ab_w_out, m_c_w_in, m_c_conv_w, m_c_conv_b, m_c_w_a, m_c_b_a, m_c_w_i, m_c_b_i, m_c_lam, m_c_w_out, v_g_mix_pre, v_g_mix_post, v_g_cross_pre, v_g_mem, v_g_cross_post, v_g_ffn_pre, v_g_ffn_post, v_w_xq, v_w_xkv, v_w_xo, v_w_ffn_gu, v_w_ffn_down, v_ab_w_in, v_ab_b_f, v_ab_conv_w, v_ab_w_out, v_c_w_in, v_c_conv_w, v_c_conv_b, v_c_w_a, v_c_b_a, v_c_w_i, v_c_b_i, v_c_lam, v_c_w_out):
    given = dict(locals())
    w = {n: given[n] for n in WEIGHTS}
    m_in = {n: given["m_" + n] for n in WEIGHTS}
    v_in = {n: given["v_" + n] for n in WEIGHTS}
    xi, yi, ci = _place()
    chip = 2 * xi + yi

    full = {}
    small_shapes = [w[n].shape for n in SMALL_SHARDED]
    rows_w = _rows_for(small_shapes)
    assert rows_w * PACK_COLS > sum(math.prod(s) for s in small_shapes)
    every = gather_small("gather_small_weights", _pack([w[n] for n in SMALL_SHARDED], rows_w), reduce=False)
    per_chip = every[0::2].reshape(N_CHIPS, -1)
    at = 0
    for n, s in zip(SMALL_SHARDED, small_shapes):
        size = math.prod(s)
        full[n] = _unshard(per_chip[:, at:at + size].reshape(N_CHIPS, *s), SHARD_DIM[n])
        at += size
    for n in REPLICATED:
        full[n] = w[n]
    after_small = every[0, -1, -1].astype(BF16)

    depth = g_mix_pre.shape[0]
    own, gathers, tokens = {}, {}, []
    for layer in range(depth):
        for part in ("mix", "rest"):
            names = chunk_names(layer, part)
            if names:
                tagp = f"l{layer}_{part}"
                own[tagp] = {n: w[n][layer if n in COMMON_BIG else layer // 2].astype(BF16) + after_small for n in names}
                halves = [a.reshape(2, -1, a.shape[-1]) for a in own[tagp].values()]
                gathers[tagp], token = copies_start(f"gather_start_{tagp}", _gather_copies, halves,
                                                    [(N_CHIPS, *h.shape) for h in halves])
                tokens.append(token[0, 0])

    def layer_weights(layer, part, x_in):
        tagp = f"l{layer}_{part}"
        if tagp not in gathers:
            return {}
        shards, lands = copies_wait(f"gather_wait_{tagp}", _gather_copies, gathers[tagp], x_in)
        lands = pass_to_sibling(f"gather_pass_{tagp}", shards, lands)
        return {n: _unshard(g.reshape(N_CHIPS, *mine.shape), SHARD_DIM[n] - 1)
                for (n, mine), g in zip(own[tagp].items(), lands)}

    core_arr = ci.reshape(1).astype(jnp.int32)
    place_arr = jnp.stack([chip, ci]).astype(jnp.int32)
    in_flight, swapping, held = [], [], {}

    def exchange(after):
        layer, tagp, names, state = swapping.pop()
        mine, got = copies_wait(f"swap_wait_{tagp}", _swap_copies, state, after)
        sums = [pair_sum(f"pair_sum_{tagp}_{n}", o, g, core_arr) for n, o, g in zip(names, mine, got)]
        state, token = copies_start(f"exchange_start_{tagp}", _exchange_copies, sums, [b.shape for b in sums])
        in_flight.append((layer, tagp, names, state))
        return token

    def reduce_hook(layer, part, part_grads, after):
        token = exchange(after)[0, 0] if swapping else None
        held.update(part_grads)
        names = chunk_names(layer, part) if part in ("rest", "mix") else ()
        if names:
            tagp = f"l{layer}_{part}"
            mine = [held.pop(n) for n in names]
            state, started = copies_start(f"swap_start_{tagp}", _swap_copies, mine,
                                          [(m.shape[0], *m.shape[2:]) for m in mine])
            swapping.append((layer, tagp, names, state))
            token = started[0, 0] if token is None else token + started[0, 0]
        return token

    sq_cols, dx, grads, _ = local_step(x[0] + sum(tokens), mem[0], loss_target[0], full, layer_weights, reduce_hook)
    exchange(dx)
    loss = lax.psum(0.5 / D_MODEL * jnp.sum(sq_cols), ("x", "y", "c"))

    reduced = {n: lax.empty((w[n].shape[0], 2, math.prod(w[n].shape[1:-1]) // 2, w[n].shape[-1]), F32) for n in BIG}
    for layer, tagp, names, state in in_flight:
        sums, parts = copies_wait(f"exchange_wait_{tagp}", _exchange_copies, state, dx)
        for n, mine, p in zip(names, sums, parts):
            index = layer if n in COMMON_BIG else layer // 2
            reduced[n] = chip_sum(f"chip_sum_{tagp}_{n}", mine, p, place_arr, reduced[n], index)
    grad_out = {n: g.reshape(w[n].shape) for n, g in zip(BIG, share_halves([reduced[n] for n in BIG]))}

    small_names = REPLICATED + SMALL_SHARDED
    small_full_shapes = [grads[n].shape for n in small_names]
    total = gather_small("reduce_small_grads", _pack([grads[n] for n in small_names], _rows_for(small_full_shapes)),
                         reduce=True)
    for n, g in zip(small_names, _unpack(total, small_full_shapes)):
        if n in SHARD_DIM:
            g = lax.dynamic_index_in_dim(_shardify(g, SHARD_DIM[n]), chip, axis=0, keepdims=False)
        grad_out[n] = g

    delta, new_m, new_v = {}, {}, {}
    for n in BIG:
        two_d = lambda a: a.reshape(-1, a.shape[-1])
        d, m2, v2 = adamw(f"adamw_{n}", two_d(w[n]), two_d(grad_out[n]), two_d(m_in[n]), two_d(v_in[n]))
        delta[n], new_m[n], new_v[n] = (a.reshape(w[n].shape) for a in (d, m2, v2))
    shapes = [w[n].shape for n in small_names]
    rows = _rows_for(shapes)
    packed = [_pack([src[n] for n in small_names], rows) for src in (w, grad_out, m_in, v_in)]
    for dst, res in zip((delta, new_m, new_v), adamw("adamw_small", *packed)):
        for n, a in zip(small_names, _unpack(res, shapes)):
            dst[n] = a

    return (loss, dx[None], *[grad_out[n] for n in WEIGHTS], *[delta[n] for n in WEIGHTS],
            *[new_m[n] for n in WEIGHTS], *[new_v[n] for n in WEIGHTS])
```

```python
import functools
import math

import jax
import jax.numpy as jnp
from jax import lax
from jax.experimental import pallas as pl
from jax.experimental.pallas import tpu as pltpu

F32, BF16 = jnp.float32, jnp.bfloat16
D_MODEL = 1024
EPS = 1e-6
NEG_INF = -1e30
FOX_HEADS, FOX_HEAD_DIM, FOX_WIDTH = 8, 64, 512
SC_WIDTH = 512
AB_IN = 3 * FOX_WIDTH + FOX_HEADS + 3 * SC_WIDTH
AB_IN_PAD = 3200
LRU_BW, LRU_BLOCKS = 256, 4
RG_C = 8.0
MEM_HEADS, MEM_HEAD_DIM = 4, 256
ADAM_LR, ADAM_B1, ADAM_B2, ADAM_EPS, ADAM_WD, ADAM_STEP = 0.001, 0.9, 0.999, 1e-08, 0.01, 10
N_CHIPS = 4
MESH = pl.DeviceIdType.MESH
VMEM_LIMIT_BYTES = 48 * 1024 * 1024
MM_OPERAND_TILE_BYTES = 7 * 1024 * 1024

NN = (((1,), (0,)), ((), ()))
NT = (((1,), (1,)), ((), ()))
TN = (((0,), (0,)), ((), ()))


def _dot(a, b, dn=NN):
    return lax.dot_general(a.astype(BF16), b.astype(BF16), dn, preferred_element_type=F32)


def _tile(n, prefs):
    for p in prefs:
        if n % p == 0:
            return p
    return n


def _pcall(name, kern, grid, ins, in_specs, out_shape, out_specs, sem):
    in_hbm = lambda s: pltpu.HBM(s.shape, s.dtype)
    out_shape = [in_hbm(s) for s in out_shape] if isinstance(out_shape, (list, tuple)) else in_hbm(out_shape)
    return pl.pallas_call(
        kern, name=name, grid=grid, in_specs=in_specs, out_specs=out_specs, out_shape=out_shape,
        compiler_params=pltpu.CompilerParams(dimension_semantics=sem, vmem_limit_bytes=VMEM_LIMIT_BYTES),
    )(*[pltpu.with_memory_space_constraint(x, pltpu.HBM) for x in ins])


def mm(name, a, b, mode, out_dtype, reduce_layout=False):
    if mode == "nn":
        (m, k), n = a.shape, b.shape[1]
    elif mode == "nt":
        (m, k), n = a.shape, b.shape[0]
    else:
        (k, m), n = a.shape, b.shape[1]
    if reduce_layout:
        tm, tn = m // 2, n // N_CHIPS
    else:
        tn = _tile(n, ((1024,) if mode == "tn" else ()) + (512, 640, 256, 128))
        tm = next(c for c in (2048, 1024, 512, 256, 128, m)
                  if m % c == 0 and 2 * c * k <= MM_OPERAND_TILE_BYTES and 4 * c * tn <= MM_OPERAND_TILE_BYTES)
    dn = {"nn": NN, "nt": NT, "tn": TN}[mode]

    def kern(a_ref, b_ref, o_ref):
        o_ref[...] = _dot(a_ref[...], b_ref[...], dn).astype(o_ref.dtype)

    a_spec = pl.BlockSpec((k, tm), lambda i, j: (0, i)) if mode == "tn" else pl.BlockSpec((tm, k), lambda i, j: (i, 0))
    b_spec = pl.BlockSpec((tn, k), lambda i, j: (j, 0)) if mode == "nt" else pl.BlockSpec((k, tn), lambda i, j: (0, j))
    if reduce_layout:
        out_shape = jax.ShapeDtypeStruct((N_CHIPS, 2, tm, tn), out_dtype)
        o_spec = pl.BlockSpec((None, None, tm, tn), lambda i, j: (j, i, 0, 0))
    else:
        out_shape = jax.ShapeDtypeStruct((m, n), out_dtype)
        o_spec = pl.BlockSpec((tm, tn), lambda i, j: (i, j))
    return _pcall(name, kern, (m // tm, n // tn), (a, b), [a_spec, b_spec], out_shape, o_spec, ("parallel", "parallel"))


def rowwise(name, body, rows, params, outs, accs=(), tr=256):
    t = rows[0].shape[0]
    tr = min(tr, t)
    nr, npar, no = len(rows), len(params), len(outs)

    def kern(*refs):
        acc_refs = refs[nr + npar + no:]
        if acc_refs:
            @pl.when(pl.program_id(0) == 0)
            def _():
                for ar in acc_refs:
                    ar[...] = jnp.zeros_like(ar)
        body(refs[:nr], refs[nr:nr + npar], refs[nr + npar:nr + npar + no], acc_refs)

    in_specs = [pl.BlockSpec((tr, x.shape[1]), lambda i: (i, 0)) for x in rows]
    in_specs += [pl.BlockSpec(p.shape, lambda i: (0, 0)) for p in params]
    out_specs = [pl.BlockSpec((tr, c), lambda i: (i, 0)) for c, _ in outs]
    out_specs += [pl.BlockSpec(s, lambda i: (0, 0)) for s in accs]
    out_shape = [jax.ShapeDtypeStruct((t, c), dt) for c, dt in outs]
    out_shape += [jax.ShapeDtypeStruct(s, F32) for s in accs]
    return _pcall(name, kern, (t // tr,), (*rows, *params), in_specs, out_shape, out_specs,
                  ("arbitrary",) if accs else ("parallel",))


def _rms_stats(x):
    r = lax.rsqrt(jnp.mean(x * x, axis=-1, keepdims=True) + EPS)
    return r, x * r


def _rms_bwd(xh, r, g, dy):
    dxh = dy * g
    dx = r * (dxh - xh * jnp.mean(dxh * xh, axis=-1, keepdims=True))
    return dx, jnp.sum(dy * xh, axis=0, keepdims=True)


def rms_pre(name, x, gains, layer):
    def body(r, p, o, a):
        _, xh = _rms_stats(r[0][...])
        o[0][...] = (xh * p[0][layer:layer + 1, :]).astype(BF16)
    return rowwise(name, body, [x], [gains], [(x.shape[1], BF16)])[0]


def post_add(name, x, y, gains, layer):
    def body(r, p, o, a):
        _, yh = _rms_stats(r[1][...])
        o[0][...] = r[0][...] + yh * p[0][layer:layer + 1, :]
    return rowwise(name, body, [x, y], [gains], [(x.shape[1], F32)])[0]


def post_bwd(name, y, dx, gains, layer):
    def body(r, p, o, a):
        rr, yh = _rms_stats(r[0][...])
        dy, dg = _rms_bwd(yh, rr, p[0][layer:layer + 1, :], r[1][...])
        o[0][...] = dy.astype(BF16)
        a[0][...] += dg
    c = y.shape[1]
    return rowwise(name, body, [y, dx], [gains], [(c, BF16)], [(1, c)])


def pre_bwd(name, x, dh, dx_res, gains, layer):
    def body(r, p, o, a):
        rr, xh = _rms_stats(r[0][...])
        dx, dg = _rms_bwd(xh, rr, p[0][layer:layer + 1, :], r[1][...])
        o[0][...] = r[2][...] + dx
        a[0][...] += dg
    c = x.shape[1]
    return rowwise(name, body, [x, dh, dx_res], [gains], [(c, F32)], [(1, c)])


def gain_bwd(name, x, dh):
    def body(r, p, o, a):
        _, xh = _rms_stats(r[0][...])
        a[0][...] += jnp.sum(r[1][...] * xh, axis=0, keepdims=True)
    return rowwise(name, body, [x, dh], [], [], [(1, x.shape[1])])[0]


def _sigmoid(z):
    return 1.0 / (1.0 + jnp.exp(-z))


def swiglu_fwd(name, gu):
    f = gu.shape[1] // 2

    def body(r, p, o, a):
        g = r[0][:, :f].astype(F32)
        u = r[0][:, f:].astype(F32)
        o[0][...] = (g * _sigmoid(g) * u).astype(BF16)
    return rowwise(name, body, [gu], [], [(f, BF16)])[0]


def swiglu_bwd(name, gu, da):
    f = gu.shape[1] // 2

    def body(r, p, o, a):
        g = r[0][:, :f].astype(F32)
        u = r[0][:, f:].astype(F32)
        d = r[1][...].astype(F32)
        sg = _sigmoid(g)
        o[0][:, :f] = (d * u * sg * (1.0 + g * (1.0 - sg))).astype(BF16)
        o[0][:, f:] = (d * g * sg).astype(BF16)
    return rowwise(name, body, [gu, da], [], [(2 * f, BF16)])[0]


def loss_head(name, y, target):
    c = y.shape[1]

    def body(r, p, o, a):
        e = r[0][...] - r[1][...]
        o[0][...] = e * (1.0 / c)
        a[0][...] += jnp.sum(e * e, axis=0, keepdims=True)
    return rowwise(name, body, [y, target], [], [(c, F32)], [(1, c)])


def adamw(name, w, g, m, v):
    c = w.shape[1]

    def body(r, p, o, a):
        wv, gv, mv, vv = (x[...] for x in r)
        m2 = ADAM_B1 * mv + (1.0 - ADAM_B1) * gv
        v2 = ADAM_B2 * vv + (1.0 - ADAM_B2) * (gv * gv)
        m_hat = m2 / (1.0 - ADAM_B1 ** ADAM_STEP)
        v_hat = v2 / (1.0 - ADAM_B2 ** ADAM_STEP)
        o[0][...] = -ADAM_LR * (m_hat / (jnp.sqrt(v_hat) + ADAM_EPS) + ADAM_WD * wv)
        o[1][...] = m2
        o[2][...] = v2
    tr = _tile(w.shape[0], (256, 128, 64, 32, 16, 8))
    return rowwise(name, body, [w, g, m, v], [], [(c, F32)] * 3, tr=tr)


def colwise(name, body, cols, params, outs, pouts=(), tc=128):
    t = cols[0][0].shape[0]
    c = params[0].shape[1] if params else cols[0][0].shape[1]
    nc, npar, no = len(cols), len(params), len(outs)

    def kern(*refs):
        body(refs[:nc], refs[nc:nc + npar], refs[nc + npar:nc + npar + no], refs[nc + npar + no:])

    in_specs = [pl.BlockSpec((t, tc), functools.partial(lambda j, off: (0, j + off), off=off)) for _, off in cols]
    in_specs += [pl.BlockSpec((p.shape[0], tc), lambda j: (0, j)) for p in params]
    out_specs = [pl.BlockSpec((t, tc), lambda j: (0, j)) for _ in outs]
    out_specs += [pl.BlockSpec((r, tc), lambda j: (0, j)) for r in pouts]
    out_shape = [jax.ShapeDtypeStruct((t, c), dt) for dt in outs]
    out_shape += [jax.ShapeDtypeStruct((r, c), F32) for r in pouts]
    return _pcall(name, kern, (c // tc,), (*[x for x, _ in cols], *params), in_specs, out_shape, out_specs,
                  ("parallel",))


def _row_index(shape):
    return lax.broadcasted_iota(jnp.int32, shape, 0)


def _shift_down(x, d, rows):
    return jnp.where(rows >= d, pltpu.roll(x, d, 0), 0.0)


def _shift_up(x, d, rows):
    t = x.shape[0]
    return jnp.where(rows < t - d, pltpu.roll(x, t - d, 0), 0.0)


def sconv_fwd(name, proj, col0, conv_w, tc=128):
    nb = SC_WIDTH // tc

    def body(cl, p, o, po):
        b, c, u = (x[...] for x in cl)
        rows = _row_index(b.shape)
        w = p[0][...]
        z = c * u
        conv = w[2:3] * z + w[1:2] * _shift_down(z, 1, rows) + w[0:1] * _shift_down(z, 2, rows)
        o[0][...] = (b * conv).astype(BF16)
    return colwise(name, body, [(proj, col0), (proj, col0 + nb), (proj, col0 + 2 * nb)], [conv_w], [BF16], tc=tc)[0]


def sconv_bwd(name, proj, col0, conv_w, dyb, dcol0, tc=128):
    nb = SC_WIDTH // tc

    def body(cl, p, o, po):
        b, c, u, dy = (x[...] for x in cl)
        rows = _row_index(b.shape)
        w = p[0][...]
        z = c * u
        z1, z2 = _shift_down(z, 1, rows), _shift_down(z, 2, rows)
        conv = w[2:3] * z + w[1:2] * z1 + w[0:1] * z2
        dconv = dy * b
        dz = w[2:3] * dconv + w[1:2] * _shift_up(dconv, 1, rows) + w[0:1] * _shift_up(dconv, 2, rows)
        o[0][...] = (dy * conv).astype(BF16)
        o[1][...] = (dz * u).astype(BF16)
        o[2][...] = (dz * c).astype(BF16)
        po[0][0:1, :] = jnp.sum(dconv * z2, axis=0, keepdims=True)
        po[0][1:2, :] = jnp.sum(dconv * z1, axis=0, keepdims=True)
        po[0][2:3, :] = jnp.sum(dconv * z, axis=0, keepdims=True)
    return colwise(name, body, [(proj, col0), (proj, col0 + nb), (proj, col0 + 2 * nb), (dyb, dcol0)], [conv_w],
                   [BF16, BF16, BF16], [3], tc=tc)


def _expm1(x):
    series = x * (1.0 + 0.5 * x * (1.0 + x * (1.0 / 3.0) * (1.0 + 0.25 * x * (1.0 + 0.2 * x))))
    return jnp.where(jnp.abs(x) < 0.05, series, jnp.exp(x) - 1.0)


def _log1p(x):
    series = x * (1.0 - x * (0.5 - x * (1.0 / 3.0 - 0.25 * x)))
    return jnp.where(jnp.abs(x) < 0.01, series, jnp.log(1.0 + x))


def _softplus_neg(lam):
    sp = jnp.maximum(-lam, 0.0) + _log1p(jnp.exp(-jnp.abs(lam)))
    return sp, -_sigmoid(-lam)


GELU_C = math.sqrt(2.0 / math.pi)


def _gelu(x):
    th = jnp.tanh(GELU_C * (x + 0.044715 * x * x * x))
    val = 0.5 * x * (1.0 + th)
    grad = 0.5 * (1.0 + th) + 0.5 * x * (1.0 - th * th) * GELU_C * (1.0 + 3.0 * 0.044715 * x * x)
    return val, grad


def rg_conv_fwd(name, gu2, conv_w, conv_b, tc=128):
    nb = D_MODEL // tc

    def body(cl, p, o, po):
        u = cl[0][...]
        rows = _row_index(u.shape)
        w = p[0][...]
        o[0][...] = (w[3:4] * u + w[2:3] * _shift_down(u, 1, rows) + w[1:2] * _shift_down(u, 2, rows)
                     + w[0:1] * _shift_down(u, 3, rows) + p[1][...])
    return colwise(name, body, [(gu2, nb)], [conv_w, conv_b], [F32], tc=tc)[0]


def rg_conv_bwd(name, gu2, duc, conv_w, tc=128):
    nb = D_MODEL // tc

    def body(cl, p, o, po):
        u, d = cl[0][...], cl[1][...]
        rows = _row_index(u.shape)
        w = p[0][...]
        o[0][...] = (w[3:4] * d + w[2:3] * _shift_up(d, 1, rows) + w[1:2] * _shift_up(d, 2, rows)
                     + w[0:1] * _shift_up(d, 3, rows)).astype(BF16)
        for k in range(4):
            uk = u if k == 3 else _shift_down(u, 3 - k, rows)
            po[0][k:k + 1, :] = jnp.sum(d * uk, axis=0, keepdims=True)
        po[1][...] = jnp.sum(d, axis=0, keepdims=True)
    return colwise(name, body, [(gu2, nb), (duc, 0)], [conv_w], [BF16], [4, 1], tc=tc)


def rg_gates_fwd(name, uc, w_a, b_a, w_i, b_i, tr=512):
    t = uc.shape[0]
    tr = min(tr, t)

    def kern(u_ref, wa_ref, ba_ref, wi_ref, bi_ref, r_ref, i_ref):
        ub = u_ref[...].astype(BF16)
        r_ref[...] = _sigmoid(_dot(ub, wa_ref[...]) + ba_ref[...])
        i_ref[...] = _sigmoid(_dot(ub, wi_ref[...]) + bi_ref[...])

    blk = pl.BlockSpec((tr, LRU_BW), lambda n, i: (i, n))
    wspec = pl.BlockSpec((None, LRU_BW, LRU_BW), lambda n, i: (n, 0, 0))
    bspec = pl.BlockSpec((1, LRU_BW), lambda n, i: (0, n))
    return _pcall(name, kern, (LRU_BLOCKS, t // tr), (uc, w_a, b_a, w_i, b_i), [blk, wspec, bspec, wspec, bspec],
                  [jax.ShapeDtypeStruct(uc.shape, F32)] * 2, [blk, blk], ("parallel", "parallel"))


def rg_gates_bwd(name, uc, dzr, dzi, duc_part, w_a, w_i):
    t = uc.shape[0]
    rows = LRU_BW // N_CHIPS

    def kern(u_ref, dr_ref, di_ref, dp_ref, wa_ref, wi_ref, duc_ref, dwa_ref, dwi_ref):
        ub = u_ref[...].astype(BF16)
        dr, di = dr_ref[...], di_ref[...]
        dwa, dwi = _dot(ub, dr, TN), _dot(ub, di, TN)
        for p in range(N_CHIPS):
            dwa_ref[p] = dwa[p * rows:(p + 1) * rows].astype(dwa_ref.dtype)
            dwi_ref[p] = dwi[p * rows:(p + 1) * rows].astype(dwi_ref.dtype)
        duc_ref[...] = dp_ref[...] + _dot(dr, wa_ref[...], NT) + _dot(di, wi_ref[...], NT)

    blk = pl.BlockSpec((t, LRU_BW), lambda n: (0, n))
    wspec = pl.BlockSpec((None, LRU_BW, LRU_BW), lambda n: (n, 0, 0))
    gspec = pl.BlockSpec((N_CHIPS, None, rows, LRU_BW), lambda n: (0, n, 0, 0))
    gshape = jax.ShapeDtypeStruct((N_CHIPS, LRU_BLOCKS, rows, LRU_BW), BF16)
    return _pcall(name, kern, (LRU_BLOCKS,), (uc, dzr, dzi, duc_part, w_a, w_i), [blk, blk, blk, blk, wspec, wspec],
                  [jax.ShapeDtypeStruct(uc.shape, F32), gshape, gshape], [blk, gspec, gspec], ("parallel",))


def _rg_decay(r, lam):
    sp, dsp = _softplus_neg(lam)
    la = -RG_C * r * sp
    a = jnp.exp(la)
    sq = jnp.sqrt(-_expm1(2.0 * la))
    return sp, dsp, a, sq


def rg_scan_fwd(name, gu2, uc, r, i, lam, tc=128):
    def body(cl, p, o, po):
        gate, ucv, rv, iv = (x[...] for x in cl)
        t = gate.shape[0]
        rows = _row_index(gate.shape)
        _, _, a, sq = _rg_decay(rv, p[0][...])
        b = sq * (iv * ucv)
        d = 1
        while d < t:
            keep = rows >= d
            b = a * jnp.where(keep, pltpu.roll(b, d, 0), 0.0) + b
            a = a * jnp.where(keep, pltpu.roll(a, d, 0), 1.0)
            d *= 2
        o[0][...] = (_gelu(gate)[0] * b).astype(BF16)
        o[1][...] = b
    return colwise(name, body, [(gu2, 0), (uc, 0), (r, 0), (i, 0)], [lam], [BF16, F32], tc=tc)


def rg_scan_bwd(name, gu2, uc, r, i, hs, dy, lam, tc=128):
    def body(cl, p, o, po):
        gate, ucv, rv, iv, h, dyv = (x[...] for x in cl)
        t = gate.shape[0]
        rows = _row_index(gate.shape)
        sp, dsp, a, sq = _rg_decay(rv, p[0][...])
        gl, dgl = _gelu(gate)
        o[0][...] = (dyv * h * dgl).astype(BF16)
        g = dyv * gl
        am = _shift_up(a, 1, rows)
        d = 1
        while d < t:
            keep = rows < t - d
            g = am * jnp.where(keep, pltpu.roll(g, t - d, 0), 0.0) + g
            am = am * jnp.where(keep, pltpu.roll(am, t - d, 0), 0.0)
            d *= 2
        da = g * _shift_down(h, 1, rows)
        iu = iv * ucv
        d_iu = g * sq
        dla = da * a - (g * iu) * (a * a) / sq
        dzr = dla * (-RG_C * sp) * rv * (1.0 - rv)
        dzi = d_iu * ucv * iv * (1.0 - iv)
        o[1][...] = dzr.astype(BF16)
        o[2][...] = dzi.astype(BF16)
        o[3][...] = d_iu * iv
        po[0][...] = jnp.sum(dzr, axis=0, keepdims=True)
        po[1][...] = jnp.sum(dzi, axis=0, keepdims=True)
        po[2][...] = jnp.sum(dla * rv, axis=0, keepdims=True) * (-RG_C) * dsp
    return colwise(name, body, [(gu2, 0), (uc, 0), (r, 0), (i, 0), (hs, 0), (dy, 0)], [lam],
                   [BF16, BF16, BF16, F32], [1, 1, 1], tc=tc)


def _split3(x):
    hi = x.astype(BF16)
    r1 = x - hi.astype(F32)
    mid = r1.astype(BF16)
    lo = (r1 - mid.astype(F32)).astype(BF16)
    return hi, mid, lo


def _tri_dot(x, tri):
    out = None
    for piece in _split3(x):
        term = lax.dot_general(piece, tri, NN, preferred_element_type=F32)
        out = term if out is None else out + term
    return out


def fox_gates_fwd(name, z_t, b_f):
    h, t = z_t.shape
    tb = min(512, t)

    def kern(z_ref, b_ref, o_ref):
        z = z_ref[...] + b_ref[...]
        logf = jnp.minimum(z, 0.0) - _log1p(jnp.exp(-jnp.abs(z)))
        src = lax.broadcasted_iota(jnp.int32, (t, tb), 0)
        dst = lax.broadcasted_iota(jnp.int32, (t, tb), 1) + pl.program_id(0) * tb
        o_ref[...] = _tri_dot(logf, (src <= dst).astype(BF16))

    return _pcall(name, kern, (t // tb,), (z_t, b_f),
                  [pl.BlockSpec((h, t), lambda j: (0, 0)), pl.BlockSpec((h, 1), lambda j: (0, 0))],
                  jax.ShapeDtypeStruct((h, t), F32), pl.BlockSpec((h, tb), lambda j: (0, j)), ("parallel",))


def fox_gates_bwd(name, z_t, b_f, dcum_t):
    h, t = z_t.shape
    tb = min(512, t)

    def kern(z_ref, b_ref, d_ref, dz_ref, db_ref):
        @pl.when(pl.program_id(0) == 0)
        def _():
            db_ref[...] = jnp.zeros_like(db_ref)
        src = lax.broadcasted_iota(jnp.int32, (t, tb), 0)
        dst = lax.broadcasted_iota(jnp.int32, (t, tb), 1) + pl.program_id(0) * tb
        dlogf = _tri_dot(d_ref[...], (src >= dst).astype(BF16))
        z = z_ref[...] + b_ref[...]
        dz = dlogf * _sigmoid(-z)
        dz_ref[...] = dz
        db_ref[...] += jnp.sum(dz, axis=1, keepdims=True)

    return _pcall(name, kern, (t // tb,), (z_t, b_f, dcum_t),
                  [pl.BlockSpec((h, tb), lambda j: (0, j)), pl.BlockSpec((h, 1), lambda j: (0, 0)),
                   pl.BlockSpec((h, t), lambda j: (0, 0))],
                  [jax.ShapeDtypeStruct((h, t), F32), jax.ShapeDtypeStruct((h, 1), F32)],
                  [pl.BlockSpec((h, tb), lambda j: (0, j)), pl.BlockSpec((h, 1), lambda j: (0, 0))], ("arbitrary",))


def _fox_spans(qs, k_ref, cr_ref, i, tq):
    n0 = i * tq
    sd = _dot(qs, k_ref[n0:n0 + tq, :], NT) - cr_ref[:, n0:n0 + tq]
    row = lax.broadcasted_iota(jnp.int32, (tq, tq), 0)
    col = lax.broadcasted_iota(jnp.int32, (tq, tq), 1)
    spans = [(n0, tq, jnp.where(row >= col, sd, NEG_INF))]
    if i > 0:
        spans.append((0, n0, _dot(qs, k_ref[0:n0, :], NT) - cr_ref[:, 0:n0]))
    return spans


def fox_fwd(name, q, k, v, cum_r, tq=256):
    h, t, dh = q.shape
    tq = min(tq, t)
    scale = FOX_HEAD_DIM ** -0.5

    def kern(q_ref, k_ref, v_ref, cr_ref, o_ref, lse_ref):
        for i in range(t // tq):
            rows = slice(i * tq, (i + 1) * tq)
            spans = _fox_spans(q_ref[rows, :] * scale, k_ref, cr_ref, i, tq)
            m = functools.reduce(jnp.maximum, [jnp.max(s, axis=-1, keepdims=True) for _, _, s in spans])
            l, acc = 0.0, 0.0
            for k0, kn, s in spans:
                p = jnp.exp(s - m)
                l = l + jnp.sum(p, axis=-1, keepdims=True)
                acc = acc + _dot(p, v_ref[k0:k0 + kn, :])
            o_ref[rows, :] = (acc / l).astype(o_ref.dtype)
            lse_ref[rows, :] = m + jnp.log(l)

    hspec = pl.BlockSpec((None, t, dh), lambda a: (a, 0, 0))
    cspec = pl.BlockSpec((None, t, 1), lambda a: (a, 0, 0))
    rspec = pl.BlockSpec((None, 1, t), lambda a: (a, 0, 0))
    return _pcall(name, kern, (h,), (q, k, v, cum_r), [hspec, hspec, hspec, rspec],
                  [jax.ShapeDtypeStruct((h, t, dh), BF16), jax.ShapeDtypeStruct((h, t, 1), F32)],
                  [hspec, cspec], ("parallel",))


def fox_bwd(name, q, k, v, do, lse, cum_r, tq=256):
    h, t, dh = q.shape
    tq = min(tq, t)
    scale = FOX_HEAD_DIM ** -0.5

    def kern(q_ref, k_ref, v_ref, do_ref, lse_ref, cr_ref, dq_ref, dk_ref, dv_ref, dc_ref):
        dk_ref[...] = jnp.zeros_like(dk_ref)
        dv_ref[...] = jnp.zeros_like(dv_ref)
        dc_ref[...] = jnp.zeros_like(dc_ref)
        for i in range(t // tq):
            rows = slice(i * tq, (i + 1) * tq)
            qs, dov, lse_v = q_ref[rows, :] * scale, do_ref[rows, :], lse_ref[rows, :]
            spans = _fox_spans(qs, k_ref, cr_ref, i, tq)
            probs = [jnp.exp(s - lse_v) for _, _, s in spans]
            dps = [_dot(dov, v_ref[k0:k0 + kn, :], NT) for k0, kn, _ in spans]
            rowdot = sum(jnp.sum(dp * p, axis=-1, keepdims=True) for dp, p in zip(dps, probs))
            dq = 0.0
            for (k0, kn, _), p, dp in zip(spans, probs, dps):
                ds = p * (dp - rowdot)
                dq = dq + _dot(ds, k_ref[k0:k0 + kn, :])
                dk_ref[k0:k0 + kn, :] += _dot(ds, qs, TN)
                dv_ref[k0:k0 + kn, :] += _dot(p, dov, TN)
                dc_ref[:, k0:k0 + kn] -= jnp.sum(ds, axis=0, keepdims=True)
            dq_ref[rows, :] = (dq * scale).astype(dq_ref.dtype)

    hspec = pl.BlockSpec((None, t, dh), lambda a: (a, 0, 0))
    cspec = pl.BlockSpec((None, t, 1), lambda a: (a, 0, 0))
    rspec = pl.BlockSpec((None, 1, t), lambda a: (a, 0, 0))
    return _pcall(name, kern, (h,), (q, k, v, do, lse, cum_r), [hspec, hspec, hspec, hspec, cspec, rspec],
                  [jax.ShapeDtypeStruct((h, t, dh), BF16), jax.ShapeDtypeStruct((h, t, dh), F32),
                   jax.ShapeDtypeStruct((h, t, dh), F32), jax.ShapeDtypeStruct((h, 1, t), F32)],
                  [hspec, hspec, hspec, rspec], ("parallel",))


def _xattn_probs(q, k):
    s = _dot(q, k, NT) * (MEM_HEAD_DIM ** -0.5)
    p = jnp.exp(s - jnp.max(s, axis=-1, keepdims=True))
    return p / jnp.sum(p, axis=-1, keepdims=True)


def xattn_fwd(name, q, kv, tq=512):
    t = q.shape[0]
    tq = min(tq, t)
    ml = kv.shape[0]

    def kern(q_ref, k_ref, v_ref, o_ref):
        o_ref[...] = _dot(_xattn_probs(q_ref[...], k_ref[...]), v_ref[...]).astype(o_ref.dtype)

    qspec = pl.BlockSpec((tq, MEM_HEAD_DIM), lambda i, a: (i, a))
    return _pcall(name, kern, (t // tq, MEM_HEADS), (q, kv, kv),
                  [qspec, pl.BlockSpec((ml, MEM_HEAD_DIM), lambda i, a: (0, a)),
                   pl.BlockSpec((ml, MEM_HEAD_DIM), lambda i, a: (0, MEM_HEADS + a))],
                  jax.ShapeDtypeStruct(q.shape, BF16), qspec, ("parallel", "parallel"))


def xattn_bwd(name, q, kv, do, tq=512):
    t = q.shape[0]
    tq = min(tq, t)
    ml = kv.shape[0]
    scale = MEM_HEAD_DIM ** -0.5

    def kern(q_ref, k_ref, v_ref, do_ref, dq_ref, dk_ref, dv_ref):
        @pl.when(pl.program_id(1) == 0)
        def _():
            dk_ref[...] = jnp.zeros_like(dk_ref)
            dv_ref[...] = jnp.zeros_like(dv_ref)
        qv, kv_, dov = q_ref[...], k_ref[...], do_ref[...]
        p = _xattn_probs(qv, kv_)
        dp = _dot(dov, v_ref[...], NT)
        ds = p * (dp - jnp.sum(dp * p, axis=-1, keepdims=True)) * scale
        dq_ref[...] = _dot(ds, kv_).astype(dq_ref.dtype)
        dk_ref[...] += _dot(ds, qv, TN)
        dv_ref[...] += _dot(p, dov, TN)

    qspec = pl.BlockSpec((tq, MEM_HEAD_DIM), lambda a, i: (i, a))
    kspec = pl.BlockSpec((ml, MEM_HEAD_DIM), lambda a, i: (0, a))
    return _pcall(name, kern, (MEM_HEADS, t // tq), (q, kv, kv, do),
                  [qspec, kspec, pl.BlockSpec((ml, MEM_HEAD_DIM), lambda a, i: (0, MEM_HEADS + a)), qspec],
                  [jax.ShapeDtypeStruct(q.shape, BF16), jax.ShapeDtypeStruct((ml, D_MODEL), F32),
                   jax.ShapeDtypeStruct((ml, D_MODEL), F32)],
                  [qspec, kspec, kspec], ("parallel", "arbitrary"))


def _heads(x):
    t = x.shape[0]
    return x.reshape(t, FOX_HEADS, FOX_HEAD_DIM).transpose(1, 0, 2)


def _unheads(x):
    return x.transpose(1, 0, 2).reshape(x.shape[1], FOX_WIDTH)


def _row_cut(dw):
    return dw.reshape(N_CHIPS, 2, dw.shape[0] // (2 * N_CHIPS), dw.shape[1])


def local_step(x, mem, target, w, layer_weights=None, reduce_hook=None):
    depth = w["g_mix_pre"].shape[0]
    t = x.shape[0]
    saved = []
    i1, i2, i3 = 3 * FOX_WIDTH, 3 * FOX_WIDTH + FOX_HEADS, AB_IN
    ncol = 128

    def stacked_weights(layer, part, _):
        names = COMMON_BIG if part == "rest" else layer_big(layer)[len(COMMON_BIG):]
        return {n: w[n][layer if n in COMMON_BIG else layer // 2] for n in names}

    get_weights = layer_weights or stacked_weights
    for layer in range(depth):
        lw = dict(get_weights(layer, "mix", x))
        s = {"x0": x, "lw": lw}
        tag = f"l{layer}"
        h1 = rms_pre(f"{tag}_mix_pre", x, w["g_mix_pre"], layer)
        s["h1"] = h1
        if layer % 2 == 0:
            e = layer // 2
            w_in = jnp.pad(lw["ab_w_in"], ((0, 0), (0, AB_IN_PAD - AB_IN)))
            proj = mm(f"{tag}_ab_in", h1, w_in, "nn", F32)
            qkv = proj[:, :i1].astype(BF16).reshape(t, 3, FOX_HEADS, FOX_HEAD_DIM).transpose(1, 2, 0, 3)
            z_t = proj[:, i1:i2].T
            b_f = w["ab_b_f"][e].reshape(FOX_HEADS, 1)
            cum_t = fox_gates_fwd(f"{tag}_fox_gates", z_t, b_f)
            cum_r = cum_t[:, None, :]
            oh, lse = fox_fwd(f"{tag}_fox", qkv[0], qkv[1], qkv[2], cum_r)
            bcu = proj[:, i2:i3]
            y_b = sconv_fwd(f"{tag}_sconv", bcu, 0, w["ab_conv_w"][e])
            ycat = jnp.concatenate([_unheads(oh), y_b], axis=1)
            y1 = mm(f"{tag}_ab_out", ycat, lw["ab_w_out"], "nn", F32)
            s.update(w_in=w_in, qkv=qkv, z_t=z_t, b_f=b_f, cum_r=cum_r, lse=lse, bcu=bcu, ycat=ycat)
        else:
            o = layer // 2
            gu2 = mm(f"{tag}_c_in", h1, lw["c_w_in"], "nn", F32)
            conv_b = w["c_conv_b"][o].reshape(1, -1)
            uc = rg_conv_fwd(f"{tag}_rg_conv", gu2, w["c_conv_w"][o], conv_b)
            b_a, b_i = w["c_b_a"][o].reshape(1, -1), w["c_b_i"][o].reshape(1, -1)
            r, i = rg_gates_fwd(f"{tag}_rg_gates", uc, lw["c_w_a"], b_a, lw["c_w_i"], b_i)
            lam = w["c_lam"][o].reshape(1, -1)
            ymix, hs = rg_scan_fwd(f"{tag}_rg_scan", gu2, uc, r, i, lam)
            y1 = mm(f"{tag}_c_out", ymix, lw["c_w_out"], "nn", F32)
            s.update(gu2=gu2, uc=uc, r=r, i=i, lam=lam, hs=hs, ymix=ymix)
        s["y1"] = y1
        x = post_add(f"{tag}_mix_post", x, y1, w["g_mix_post"], layer)
        lw.update(get_weights(layer, "rest", x))
        s["x1"] = x
        h2 = rms_pre(f"{tag}_cross_pre", x, w["g_cross_pre"], layer)
        m = rms_pre(f"{tag}_mem_pre", mem, w["g_mem"], layer)
        q = mm(f"{tag}_xq", h2, lw["w_xq"], "nn", BF16)
        kv = mm(f"{tag}_xkv", m, lw["w_xkv"], "nn", BF16)
        o_att = xattn_fwd(f"{tag}_xattn", q, kv)
        y2 = mm(f"{tag}_xo", o_att, lw["w_xo"], "nn", F32)
        s.update(h2=h2, m=m, q=q, kv=kv, o_att=o_att, y2=y2)
        x = post_add(f"{tag}_cross_post", x, y2, w["g_cross_post"], layer)
        s["x2"] = x
        h3 = rms_pre(f"{tag}_ffn_pre", x, w["g_ffn_pre"], layer)
        gu = mm(f"{tag}_ffn_gu", h3, lw["w_ffn_gu"], "nn", BF16)
        act = swiglu_fwd(f"{tag}_swiglu", gu)
        y3 = mm(f"{tag}_ffn_down", act, lw["w_ffn_down"], "nn", F32)
        s.update(h3=h3, gu=gu, act=act, y3=y3)
        x = post_add(f"{tag}_ffn_post", x, y3, w["g_ffn_post"], layer)
        saved.append(s)

    dx, sq_cols = loss_head("loss_head", x, target)

    grads = {k: [None] * v.shape[0] for k, v in w.items() if k not in BIG}
    big, token = {}, None

    def dw(name, a, b, cols_cut=False):
        return mm(name, a, b, "tn", BF16, reduce_layout=True) if cols_cut else _row_cut(mm(name, a, b, "tn", BF16))

    for layer in reversed(range(depth)):
        s = saved[layer]
        lw = s["lw"]
        tag = f"b{layer}"
        lg = {}
        g_ffn_post = w["g_ffn_post"] if token is None else w["g_ffn_post"] + token
        dy3, grads["g_ffn_post"][layer] = post_bwd(f"{tag}_ffn_post", s["y3"], dx, g_ffn_post, layer)
        dact = mm(f"{tag}_ffn_down_dx", dy3, lw["w_ffn_down"], "nt", BF16)
        lg["w_ffn_down"] = dw(f"{tag}_ffn_down_dw", s["act"], dy3)
        dgu = swiglu_bwd(f"{tag}_swiglu", s["gu"], dact)
        dh3 = mm(f"{tag}_ffn_gu_dx", dgu, lw["w_ffn_gu"], "nt", F32)
        lg["w_ffn_gu"] = dw(f"{tag}_ffn_gu_dw", s["h3"], dgu, cols_cut=True)
        dx, grads["g_ffn_pre"][layer] = pre_bwd(f"{tag}_ffn_pre", s["x2"], dh3, dx, w["g_ffn_pre"], layer)
        token = None if reduce_hook is None else reduce_hook(layer, "ffn", {}, dx)
        g_cross_post = w["g_cross_post"] if token is None else w["g_cross_post"] + token
        dy2, grads["g_cross_post"][layer] = post_bwd(f"{tag}_cross_post", s["y2"], dx, g_cross_post, layer)
        do = mm(f"{tag}_xo_dx", dy2, lw["w_xo"], "nt", BF16)
        lg["w_xo"] = dw(f"{tag}_xo_dw", s["o_att"], dy2)
        dq, dk, dv = xattn_bwd(f"{tag}_xattn", s["q"], s["kv"], do)
        dh2 = mm(f"{tag}_xq_dx", dq, lw["w_xq"], "nt", F32)
        lg["w_xq"] = dw(f"{tag}_xq_dw", s["h2"], dq)
        dkv = jnp.concatenate([dk, dv], axis=1).astype(BF16)
        dm = mm(f"{tag}_xkv_dx", dkv, lw["w_xkv"], "nt", F32)
        lg["w_xkv"] = dw(f"{tag}_xkv_dw", s["m"], dkv, cols_cut=True)
        grads["g_mem"][layer] = gain_bwd(f"{tag}_mem_pre", mem, dm)
        dx, grads["g_cross_pre"][layer] = pre_bwd(f"{tag}_cross_pre", s["x1"], dh2, dx, w["g_cross_pre"], layer)
        token = None if reduce_hook is None else reduce_hook(layer, "rest", lg, dx)
        g_mix_post = w["g_mix_post"] if token is None else w["g_mix_post"] + token
        dy1, grads["g_mix_post"][layer] = post_bwd(f"{tag}_mix_post", s["y1"], dx, g_mix_post, layer)
        rest_grads, lg = lg, {}
        if layer % 2 == 0:
            e = layer // 2
            dycat = mm(f"{tag}_ab_out_dx", dy1, lw["ab_w_out"], "nt", F32)
            token = None if reduce_hook is None else reduce_hook(layer, "mixer", {}, dycat)
            lg["ab_w_out"] = dw(f"{tag}_ab_out_dw", s["ycat"], dy1)
            do_h = _heads(dycat[:, :FOX_WIDTH].astype(BF16))
            qkv = s["qkv"]
            dqh, dkh, dvh, dcum = fox_bwd(f"{tag}_fox", qkv[0], qkv[1], qkv[2], do_h, s["lse"], s["cum_r"])
            dz_t, db_f = fox_gates_bwd(f"{tag}_fox_gates", s["z_t"], s["b_f"], dcum.reshape(FOX_HEADS, t))
            grads["ab_b_f"][e] = db_f.reshape(FOX_HEADS)
            db, dc, du, dconv_w = sconv_bwd(f"{tag}_sconv", s["bcu"], 0, w["ab_conv_w"][e], dycat, FOX_WIDTH // ncol)
            grads["ab_conv_w"][e] = dconv_w
            dproj = jnp.concatenate(
                [_unheads(dqh), _unheads(dkh).astype(BF16), _unheads(dvh).astype(BF16), dz_t.T.astype(BF16), db, dc, du,
                 jnp.zeros((t, AB_IN_PAD - AB_IN), BF16)], axis=1)
            dh1 = mm(f"{tag}_ab_in_dx", dproj, s["w_in"], "nt", F32)
            dw_in = mm(f"{tag}_ab_in_dw", s["h1"], dproj, "tn", F32)[:, :AB_IN]
            lg["ab_w_in"] = dw_in.reshape(2, D_MODEL // 2, N_CHIPS, AB_IN // N_CHIPS).transpose(2, 0, 1, 3).astype(BF16)
        else:
            o = layer // 2
            dymix = mm(f"{tag}_c_out_dx", dy1, lw["c_w_out"], "nt", F32)
            token = None if reduce_hook is None else reduce_hook(layer, "mixer", {}, dymix)
            lg["c_w_out"] = dw(f"{tag}_c_out_dw", s["ymix"], dy1)
            dgate, dzr, dzi, duc_part, db_a, db_i, dlam = rg_scan_bwd(
                f"{tag}_rg_scan", s["gu2"], s["uc"], s["r"], s["i"], s["hs"], dymix, s["lam"])
            duc, dw_a, dw_i = rg_gates_bwd(f"{tag}_rg_gates", s["uc"], dzr, dzi, duc_part, lw["c_w_a"], lw["c_w_i"])
            lg["c_w_a"] = dw_a.reshape(N_CHIPS, 2, LRU_BW // 2, LRU_BW)
            lg["c_w_i"] = dw_i.reshape(N_CHIPS, 2, LRU_BW // 2, LRU_BW)
            du_raw, dconv_w, dconv_b = rg_conv_bwd(f"{tag}_rg_conv", s["gu2"], duc, w["c_conv_w"][o])
            grads["c_b_a"][o] = db_a.reshape(LRU_BLOCKS, LRU_BW)
            grads["c_b_i"][o] = db_i.reshape(LRU_BLOCKS, LRU_BW)
            grads["c_lam"][o] = dlam.reshape(-1)
            grads["c_conv_w"][o] = dconv_w
            grads["c_conv_b"][o] = dconv_b.reshape(-1)
            dgu2 = jnp.concatenate([dgate, du_raw], axis=1)
            dh1 = mm(f"{tag}_c_in_dx", dgu2, lw["c_w_in"], "nt", F32)
            lg["c_w_in"] = dw(f"{tag}_c_in_dw", s["h1"], dgu2, cols_cut=True)
        g_mix_pre = w["g_mix_pre"] if token is None else w["g_mix_pre"] + token
        dx, grads["g_mix_pre"][layer] = pre_bwd(f"{tag}_mix_pre", s["x0"], dh1, dx, g_mix_pre, layer)
        if reduce_hook is None:
            big[layer] = {**rest_grads, **lg}
        else:
            token = reduce_hook(layer, "mix", lg, dx)

    for k in list(grads):
        if k.startswith("g_"):
            grads[k] = [g.reshape(-1) for g in grads[k]]
        grads[k] = jnp.stack(grads[k])
    return sq_cols, dx, grads, big


CHIP_FLIPS = ((1, 0), (0, 1), (1, 1))
HBM_SPEC = pl.BlockSpec(memory_space=pltpu.HBM)
VMEM_SPEC = pl.BlockSpec(memory_space=pltpu.VMEM)


def _place():
    return lax.axis_index("x"), lax.axis_index("y"), lax.axis_index("c")


def _flip(v, f):
    return 1 - v if f else v


def _remote(src, dst, send_sem, recv_sem, target):
    return pltpu.make_async_remote_copy(src_ref=src, dst_ref=dst, send_sem=send_sem, recv_sem=recv_sem,
                                        device_id=target, device_id_type=MESH)


SEM_SPEC = pl.BlockSpec(memory_space=pltpu.SEMAPHORE)


def _swap_copies(srcs, lands, send_sems, recv_sems):
    x, y, c = _place()
    return [_remote(src.at[:, 1 - c], land, send_sems.at[len(CHIP_FLIPS) * a], recv_sems.at[len(CHIP_FLIPS) * a],
                    (x, y, 1 - c)) for a, (src, land) in enumerate(zip(srcs, lands))]


def _exchange_copies(srcs, lands, send_sems, recv_sems):
    x, y, c = _place()
    p = 2 * x + y
    cps = []
    for a, (src, land) in enumerate(zip(srcs, lands)):
        for k, (fx, fy) in enumerate(CHIP_FLIPS):
            qx, qy = _flip(x, fx), _flip(y, fy)
            sem = len(CHIP_FLIPS) * a + k
            cps.append(_remote(src.at[2 * qx + qy], land.at[p], send_sems.at[sem], recv_sems.at[sem], (qx, qy, c)))
    return cps


def _gather_copies(srcs, lands, send_sems, recv_sems):
    x, y, c = _place()
    p = 2 * x + y
    cps = []
    for a, (src, land) in enumerate(zip(srcs, lands)):
        for k, (fx, fy) in enumerate(CHIP_FLIPS):
            sem = len(CHIP_FLIPS) * a + k
            cps.append(_remote(src.at[c], land.at[p, c], send_sems.at[sem], recv_sems.at[sem],
                               (_flip(x, fx), _flip(y, fy), c)))
    return cps


def copies_start(name, make_copies, srcs, land_shapes):
    n = len(srcs)

    def body(*refs):
        for cp in make_copies(refs[:n], refs[n:2 * n], refs[2 * n], refs[2 * n + 1]):
            cp.start()
        refs[-1][...] = jnp.zeros_like(refs[-1])

    thru = [pltpu.HBM(b.shape, b.dtype) for b in srcs] + [pltpu.HBM(sh, b.dtype) for sh, b in zip(land_shapes, srcs)]
    outs = pl.pallas_call(
        body, name=name, in_specs=[HBM_SPEC] * (2 * n),
        out_shape=(pltpu.SemaphoreType.DMA((3 * n,)), pltpu.SemaphoreType.DMA((3 * n,)), *thru,
                   jax.ShapeDtypeStruct((8, 128), F32)),
        out_specs=(SEM_SPEC, SEM_SPEC, *[HBM_SPEC] * (2 * n), VMEM_SPEC),
        input_output_aliases={i: 2 + i for i in range(2 * n)},
        compiler_params=pltpu.CompilerParams(has_side_effects=pltpu.SideEffectType.DATAFLOW_SIDE_EFFECTING),
    )(*[pltpu.with_memory_space_constraint(b, pltpu.HBM) for b in srcs],
      *[pltpu.with_memory_space_constraint(lax.empty(sh, b.dtype), pltpu.HBM) for sh, b in zip(land_shapes, srcs)])
    return outs[:-1], outs[-1]


def copies_wait(name, make_copies, state, after):
    send_sems, recv_sems, *thru = state
    n = len(thru) // 2

    def body(*refs):
        for cp in make_copies(refs[:n], refs[n:2 * n], refs[2 * n], refs[2 * n + 1]):
            cp.wait_send()
            cp.wait_recv()

    outs = pl.pallas_call(
        body, name=name, in_specs=[HBM_SPEC] * (2 * n) + [SEM_SPEC, SEM_SPEC, pl.BlockSpec(memory_space=pl.ANY)],
        out_shape=tuple(pltpu.HBM(t.shape, t.dtype) for t in thru), out_specs=tuple([HBM_SPEC] * (2 * n)),
        input_output_aliases={i: i for i in range(2 * n)},
        compiler_params=pltpu.CompilerParams(has_side_effects=pltpu.SideEffectType.DATAFLOW_SIDE_EFFECTING),
    )(*thru, send_sems, recv_sems, after)
    return outs[:n], outs[n:]


def pass_to_sibling(name, shards, lands):
    n = len(lands)

    def body(*refs):
        own, ins, outs = refs[:n], refs[n:2 * n], refs[2 * n:3 * n]
        send_sems, recv_sems = refs[3 * n:]
        x, y, c = _place()
        sibling = (x, y, 1 - c)
        cps = []
        for a in range(n):
            for k, (fx, fy) in enumerate(CHIP_FLIPS):
                q = 2 * _flip(x, fx) + _flip(y, fy)
                cps.append(_remote(ins[a].at[q, c], outs[a].at[q, c], send_sems.at[a, k], recv_sems.at[a, k], sibling))
            cps.append(_remote(own[a], outs[a].at[2 * x + y], send_sems.at[a, 3], recv_sems.at[a, 3], sibling))
        for cp in cps:
            cp.start()
        for cp in cps:
            cp.wait()

    return pl.pallas_call(
        body, name=name, in_specs=[HBM_SPEC] * (2 * n), out_specs=[HBM_SPEC] * n,
        out_shape=[jax.ShapeDtypeStruct(b.shape, b.dtype) for b in lands],
        scratch_shapes=[pltpu.SemaphoreType.DMA((n, 4)), pltpu.SemaphoreType.DMA((n, 4))],
        input_output_aliases={n + i: i for i in range(n)},
    )(*shards, *lands)


def share_halves(bufs):
    n = len(bufs)

    def body(*refs):
        ins, outs = refs[:n], refs[n:2 * n]
        send_sems, recv_sems = refs[2 * n:]
        x, y, c = _place()
        cps = [_remote(ins[a].at[:, c], outs[a].at[:, c], send_sems.at[a], recv_sems.at[a], (x, y, 1 - c))
               for a in range(n)]
        for cp in cps:
            cp.start()
        for cp in cps:
            cp.wait()

    return pl.pallas_call(
        body, name="share_reduced_halves", in_specs=[HBM_SPEC] * n, out_specs=[HBM_SPEC] * n,
        out_shape=[jax.ShapeDtypeStruct(b.shape, b.dtype) for b in bufs],
        scratch_shapes=[pltpu.SemaphoreType.DMA((n,)), pltpu.SemaphoreType.DMA((n,))],
        input_output_aliases={i: i for i in range(n)},
    )(*bufs)


DEVICE_FLIPS = tuple((fx, fy, fc) for fx in (0, 1) for fy in (0, 1) for fc in (0, 1))[1:]


def gather_small(name, v, reduce):
    r, cdim = v.shape
    n_dev = 8

    def body(v_ref, out_ref, *scratch):
        buf = scratch[0] if reduce else out_ref
        send_sems, recv_sems = scratch[-2:]
        x, y, c = _place()
        me = 4 * x + 2 * y + c
        buf[me] = v_ref[...]
        cps = []
        for k, (fx, fy, fc) in enumerate(DEVICE_FLIPS):
            cps.append(_remote(v_ref, buf.at[me], send_sems.at[k], recv_sems.at[k],
                               (_flip(x, fx), _flip(y, fy), _flip(c, fc))))
        for cp in cps:
            cp.start()
        for cp in cps:
            cp.wait()
        if reduce:
            total = buf[0]
            for d in range(1, n_dev):
                total = total + buf[d]
            out_ref[...] = total

    scratch = [pltpu.SemaphoreType.DMA((7,)), pltpu.SemaphoreType.DMA((7,))]
    if reduce:
        scratch = [pltpu.VMEM((n_dev, r, cdim), F32)] + scratch
    out_shape = jax.ShapeDtypeStruct((r, cdim) if reduce else (n_dev, r, cdim), F32)
    return pl.pallas_call(body, name=name, in_specs=[VMEM_SPEC], out_specs=VMEM_SPEC, out_shape=out_shape,
                          scratch_shapes=scratch)(v)


def pair_sum(name, own, got, core):
    _, hx, cols = got.shape
    tr = _tile(hx, (256, 128, 64, 32, 16))

    def kern(core_ref, a_ref, b_ref, o_ref):
        o_ref[...] = (a_ref[...].astype(F32) + b_ref[...].astype(F32)).astype(BF16)

    grid_spec = pltpu.PrefetchScalarGridSpec(
        num_scalar_prefetch=1, grid=(hx // tr,),
        in_specs=[pl.BlockSpec((N_CHIPS, None, tr, cols), lambda i, cr: (0, cr[0], i, 0)),
                  pl.BlockSpec((N_CHIPS, tr, cols), lambda i, cr: (0, i, 0))],
        out_specs=pl.BlockSpec((N_CHIPS, tr, cols), lambda i, cr: (0, i, 0)))
    return pl.pallas_call(
        kern, name=name, grid_spec=grid_spec, out_shape=pltpu.HBM(got.shape, BF16),
        compiler_params=pltpu.CompilerParams(dimension_semantics=("parallel",), vmem_limit_bytes=VMEM_LIMIT_BYTES),
    )(core, pltpu.with_memory_space_constraint(own, pltpu.HBM), pltpu.with_memory_space_constraint(got, pltpu.HBM))


def chip_sum(name, mine, parts, place, buf, layer):
    _, hx, yd = parts.shape
    tr = _tile(hx, (256, 128, 64, 32, 16))

    def kern(place_ref, m_ref, p_ref, _, o_ref):
        total = None
        for q in range(N_CHIPS):
            term = jnp.where(place_ref[0] == q, m_ref[...], p_ref[q]).astype(F32)
            total = term if total is None else total + term
        o_ref[...] = total

    grid_spec = pltpu.PrefetchScalarGridSpec(
        num_scalar_prefetch=1, grid=(hx // tr,),
        in_specs=[pl.BlockSpec((None, tr, yd), lambda i, pr: (pr[0], i, 0)),
                  pl.BlockSpec((N_CHIPS, tr, yd), lambda i, pr: (0, i, 0)),
                  pl.BlockSpec(memory_space=pl.ANY)],
        out_specs=pl.BlockSpec((None, None, tr, yd), lambda i, pr: (layer, pr[1], i, 0)))
    return pl.pallas_call(
        kern, name=name, grid_spec=grid_spec, out_shape=pltpu.HBM(buf.shape, buf.dtype),
        input_output_aliases={3: 0},
        compiler_params=pltpu.CompilerParams(dimension_semantics=("parallel",), vmem_limit_bytes=VMEM_LIMIT_BYTES),
    )(place, *[pltpu.with_memory_space_constraint(a, pltpu.HBM) for a in (mine, parts, buf)])


WEIGHTS = ("g_mix_pre", "g_mix_post", "g_cross_pre", "g_mem", "g_cross_post", "g_ffn_pre", "g_ffn_post", "w_xq", "w_xkv",
           "w_xo", "w_ffn_gu", "w_ffn_down", "ab_w_in", "ab_b_f", "ab_conv_w", "ab_w_out", "c_w_in", "c_conv_w",
           "c_conv_b", "c_w_a", "c_b_a", "c_w_i", "c_b_i", "c_lam", "c_w_out")
SHARD_DIM = {"w_xq": 1, "w_xkv": 2, "w_xo": 1, "w_ffn_gu": 2, "w_ffn_down": 1, "ab_w_in": 2, "ab_conv_w": 2,
             "ab_w_out": 1, "c_w_in": 2, "c_conv_w": 2, "c_conv_b": 1, "c_w_a": 2, "c_b_a": 2, "c_w_i": 2, "c_b_i": 2,
             "c_lam": 1, "c_w_out": 1}
COMMON_BIG = ("w_xq", "w_xkv", "w_xo", "w_ffn_gu", "w_ffn_down")
EVEN_BIG, ODD_BIG = ("ab_w_in", "ab_w_out"), ("c_w_in", "c_w_a", "c_w_i", "c_w_out")
BIG = COMMON_BIG + EVEN_BIG + ODD_BIG


def layer_big(layer):
    return COMMON_BIG + (ODD_BIG if layer % 2 else EVEN_BIG)


SPLIT_LAYERS = (0,)


def chunk_names(layer, part):
    mixer = layer_big(layer)[len(COMMON_BIG):]
    if layer in SPLIT_LAYERS:
        return mixer if part == "mix" else COMMON_BIG
    return layer_big(layer) if part == "mix" else ()


SMALL_SHARDED = ("ab_conv_w", "c_conv_w", "c_conv_b", "c_b_a", "c_b_i", "c_lam")
REPLICATED = ("g_mix_pre", "g_mix_post", "g_cross_pre", "g_mem", "g_cross_post", "g_ffn_pre", "g_ffn_post", "ab_b_f")
PACK_COLS = 1024


def _unshard(g, d):
    shard = g.shape[1:]
    return jnp.moveaxis(g, 0, d).reshape(shard[:d] + (N_CHIPS * shard[d],) + shard[d + 1:])


def _shardify(full, d):
    s = full.shape
    return jnp.moveaxis(full.reshape(s[:d] + (N_CHIPS, s[d] // N_CHIPS) + s[d + 1:]), d, 0)


def _pack(arrays, rows):
    flat = jnp.concatenate([a.reshape(-1).astype(F32) for a in arrays])
    return jnp.pad(flat, (0, rows * PACK_COLS - flat.shape[0])).reshape(rows, PACK_COLS)


def _unpack(packed, shapes):
    flat = packed.reshape(-1)
    out, at = [], 0
    for s in shapes:
        size = math.prod(s)
        out.append(flat[at:at + size].reshape(s))
        at += size
    return out


def _rows_for(shapes):
    return -(-sum(math.prod(s) for s in shapes) // (8 * PACK_COLS)) * 8


def kernel(x, mem, g_mix_pre, g_mix_post, g_cross_pre, g_mem, g_cross_post, g_ffn_pre, g_ffn_post, w_xq, w_xkv, w_xo, w_ffn_gu, w_ffn_down, ab_w_in, ab_b_f, ab_conv_w, ab_w_out, c_w_in, c_conv_w, c_conv_b, c_w_a, c_b_a, c_w_i, c_b_i, c_lam, c_w_out, loss_target, m_g_mix_pre, m_g_mix_post, m_g_cross_pre, m_g_mem, m_g_cross_post, m_g_ffn_pre, m_g_ffn_post, m_w_xq, m_w_xkv, m_w_xo, m_w_ffn_gu, m_w_ffn_down, m_ab_w_in, m_ab_b_f, m_ab_conv_w, m_ab_w_out, m_c_w_in, m_c_conv_w, m_c_conv_b, m_c_w_a, m_c_b_a, m_c_w_i, m_c_b_i, m_c_lam, m_c_w_out, v_g_mix_pre, v_g_mix_post, v_g_cross_pre, v_g_mem, v_g_cross_post, v_g_ffn_pre, v_g_ffn_post, v_w_xq, v_w_xkv, v_w_xo, v_w_ffn_gu, v_w_ffn_down, v_ab_w_in, v_ab_b_f, v_ab_conv_w, v_ab_w_out, v_c_w_in, v_c_conv_w, v_c_conv_b, v_c_w_a, v_c_b_a, v_c_w_i, v_c_b_i, v_c_lam, v_c_w_out):
    given = dict(locals())
    w = {n: given[n] for n in WEIGHTS}
    m_in = {n: given["m_" + n] for n in WEIGHTS}
    v_in = {n: given["v_" + n] for n in WEIGHTS}
    xi, yi, ci = _place()
    chip = 2 * xi + yi

    full = {}
    small_shapes = [w[n].shape for n in SMALL_SHARDED]
    rows_w = _rows_for(small_shapes)
    assert rows_w * PACK_COLS > sum(math.prod(s) for s in small_shapes)
    every = gather_small("gather_small_weights", _pack([w[n] for n in SMALL_SHARDED], rows_w), reduce=False)
    per_chip = every[0::2].reshape(N_CHIPS, -1)
    at = 0
    for n, s in zip(SMALL_SHARDED, small_shapes):
        size = math.prod(s)
        full[n] = _unshard(per_chip[:, at:at + size].reshape(N_CHIPS, *s), SHARD_DIM[n])
        at += size
    for n in REPLICATED:
        full[n] = w[n]
    after_small = every[0, -1, -1].astype(BF16)

    depth = g_mix_pre.shape[0]
    own, gathers, tokens = {}, {}, []
    for layer in range(depth):
        for part in ("mix", "rest"):
            names = chunk_names(layer, part)
            if names:
                tagp = f"l{layer}_{part}"
                own[tagp] = {n: w[n][layer if n in COMMON_BIG else layer // 2].astype(BF16) + after_small for n in names}
                halves = [a.reshape(2, -1, a.shape[-1]) for a in own[tagp].values()]
                gathers[tagp], token = copies_start(f"gather_start_{tagp}", _gather_copies, halves,
                                                    [(N_CHIPS, *h.shape) for h in halves])
                tokens.append(token[0, 0])

    def layer_weights(layer, part, x_in):
        tagp = f"l{layer}_{part}"
        if tagp not in gathers:
            return {}
        shards, lands = copies_wait(f"gather_wait_{tagp}", _gather_copies, gathers[tagp], x_in)
        lands = pass_to_sibling(f"gather_pass_{tagp}", shards, lands)
        return {n: _unshard(g.reshape(N_CHIPS, *mine.shape), SHARD_DIM[n] - 1)
                for (n, mine), g in zip(own[tagp].items(), lands)}

    core_arr = ci.reshape(1).astype(jnp.int32)
    place_arr = jnp.stack([chip, ci]).astype(jnp.int32)
    in_flight, swapping, held = [], [], {}

    def exchange(after):
        layer, tagp, names, state = swapping.pop()
        mine, got = copies_wait(f"swap_wait_{tagp}", _swap_copies, state, after)
        sums = [pair_sum(f"pair_sum_{tagp}_{n}", o, g, core_arr) for n, o, g in zip(names, mine, got)]
        state, token = copies_start(f"exchange_start_{tagp}", _exchange_copies, sums, [b.shape for b in sums])
        in_flight.append((layer, tagp, names, state))
        return token

    def reduce_hook(layer, part, part_grads, after):
        token = exchange(after)[0, 0] if swapping else None
        held.update(part_grads)
        names = chunk_names(layer, part) if part in ("rest", "mix") else ()
        if names:
            tagp = f"l{layer}_{part}"
            mine = [held.pop(n) for n in names]
            state, started = copies_start(f"swap_start_{tagp}", _swap_copies, mine,
                                          [(m.shape[0], *m.shape[2:]) for m in mine])
            swapping.append((layer, tagp, names, state))
            token = started[0, 0] if token is None else token + started[0, 0]
        return token

    sq_cols, dx, grads, _ = local_step(x[0] + sum(tokens), mem[0], loss_target[0], full, layer_weights, reduce_hook)
    exchange(dx)
    loss = lax.psum(0.5 / D_MODEL * jnp.sum(sq_cols), ("x", "y", "c"))

    reduced = {n: lax.empty((w[n].shape[0], 2, math.prod(w[n].shape[1:-1]) // 2, w[n].shape[-1]), F32) for n in BIG}
    for layer, tagp, names, state in in_flight:
        sums, parts = copies_wait(f"exchange_wait_{tagp}", _exchange_copies, state, dx)
        for n, mine, p in zip(names, sums, parts):
            index = layer if n in COMMON_BIG else layer // 2
            reduced[n] = chip_sum(f"chip_sum_{tagp}_{n}", mine, p, place_arr, reduced[n], index)
    grad_out = {n: g.reshape(w[n].shape) for n, g in zip(BIG, share_halves([reduced[n] for n in BIG]))}

    small_names = REPLICATED + SMALL_SHARDED
    small_full_shapes = [grads[n].shape for n in small_names]
    total = gather_small("reduce_small_grads", _pack([grads[n] for n in small_names], _rows_for(small_full_shapes)),
                         reduce=True)
    for n, g in zip(small_names, _unpack(total, small_full_shapes)):
        if n in SHARD_DIM:
            g = lax.dynamic_index_in_dim(_shardify(g, SHARD_DIM[n]), chip, axis=0, keepdims=False)
        grad_out[n] = g

    delta, new_m, new_v = {}, {}, {}
    for n in BIG:
        two_d = lambda a: a.reshape(-1, a.shape[-1])
        d, m2, v2 = adamw(f"adamw_{n}", two_d(w[n]), two_d(grad_out[n]), two_d(m_in[n]), two_d(v_in[n]))
        delta[n], new_m[n], new_v[n] = (a.reshape(w[n].shape) for a in (d, m2, v2))
    shapes = [w[n].shape for n in small_names]
    rows = _rows_for(shapes)
    packed = [_pack([src[n] for n in small_names], rows) for src in (w, grad_out, m_in, v_in)]
    for dst, res in zip((delta, new_m, new_v), adamw("adamw_small", *packed)):
        for n, a in zip(small_names, _unpack(res, shapes)):
            dst[n] = a

    return (loss, dx[None], *[grad_out[n] for n in WEIGHTS], *[delta[n] for n in WEIGHTS],
            *[new_m[n] for n in WEIGHTS], *[new_v[n] for n in WEIGHTS])
```

```python
import functools
import math

import jax
import jax.numpy as jnp
from jax import lax
from jax.experimental import pallas as pl
from jax.experimental.pallas import tpu as pltpu

F32, BF16 = jnp.float32, jnp.bfloat16
D_MODEL = 1024
EPS = 1e-6
NEG_INF = -1e30
FOX_HEADS, FOX_HEAD_DIM, FOX_WIDTH = 8, 64, 512
SC_WIDTH = 512
AB_IN = 3 * FOX_WIDTH + FOX_HEADS + 3 * SC_WIDTH
AB_IN_PAD = 3200
LRU_BW, LRU_BLOCKS = 256, 4
RG_C = 8.0
MEM_HEADS, MEM_HEAD_DIM = 4, 256
ADAM_LR, ADAM_B1, ADAM_B2, ADAM_EPS, ADAM_WD, ADAM_STEP = 0.001, 0.9, 0.999, 1e-08, 0.01, 10
N_CHIPS = 4
MESH = pl.DeviceIdType.MESH
VMEM_LIMIT_BYTES = 48 * 1024 * 1024
MM_OPERAND_TILE_BYTES = 7 * 1024 * 1024

NN = (((1,), (0,)), ((), ()))
NT = (((1,), (1,)), ((), ()))
TN = (((0,), (0,)), ((), ()))


def _dot(a, b, dn=NN):
    return lax.dot_general(a.astype(BF16), b.astype(BF16), dn, preferred_element_type=F32)


def _tile(n, prefs):
    for p in prefs:
        if n % p == 0:
            return p
    return n


def _pcall(name, kern, grid, ins, in_specs, out_shape, out_specs, sem):
    return pl.pallas_call(
        kern, name=name, grid=grid, in_specs=in_specs, out_specs=out_specs, out_shape=out_shape,
        compiler_params=pltpu.CompilerParams(dimension_semantics=sem, vmem_limit_bytes=VMEM_LIMIT_BYTES),
    )(*ins)


def mm(name, a, b, mode, out_dtype, reduce_layout=False):
    if mode == "nn":
        (m, k), n = a.shape, b.shape[1]
    elif mode == "nt":
        (m, k), n = a.shape, b.shape[0]
    else:
        (k, m), n = a.shape, b.shape[1]
    if reduce_layout:
        tm, tn = m // 2, n // N_CHIPS
    else:
        tn = _tile(n, ((1024,) if mode == "tn" else ()) + (512, 640, 256, 128))
        tm = next(c for c in (2048, 1024, 512, 256, 128, m)
                  if m % c == 0 and 2 * c * k <= MM_OPERAND_TILE_BYTES and 4 * c * tn <= MM_OPERAND_TILE_BYTES)
    dn = {"nn": NN, "nt": NT, "tn": TN}[mode]

    def kern(a_ref, b_ref, o_ref):
        o_ref[...] = _dot(a_ref[...], b_ref[...], dn).astype(o_ref.dtype)

    a_spec = pl.BlockSpec((k, tm), lambda i, j: (0, i)) if mode == "tn" else pl.BlockSpec((tm, k), lambda i, j: (i, 0))
    b_spec = pl.BlockSpec((tn, k), lambda i, j: (j, 0)) if mode == "nt" else pl.BlockSpec((k, tn), lambda i, j: (0, j))
    if reduce_layout:
        out_shape = jax.ShapeDtypeStruct((N_CHIPS, 2, tm, tn), out_dtype)
        o_spec = pl.BlockSpec((None, None, tm, tn), lambda i, j: (j, i, 0, 0))
    else:
        out_shape = jax.ShapeDtypeStruct((m, n), out_dtype)
        o_spec = pl.BlockSpec((tm, tn), lambda i, j: (i, j))
    return _pcall(name, kern, (m // tm, n // tn), (a, b), [a_spec, b_spec], out_shape, o_spec, ("parallel", "parallel"))


def rowwise(name, body, rows, params, outs, accs=(), tr=256):
    t = rows[0].shape[0]
    tr = min(tr, t)
    nr, npar, no = len(rows), len(params), len(outs)

    def kern(*refs):
        acc_refs = refs[nr + npar + no:]
        if acc_refs:
            @pl.when(pl.program_id(0) == 0)
            def _():
                for ar in acc_refs:
                    ar[...] = jnp.zeros_like(ar)
        body(refs[:nr], refs[nr:nr + npar], refs[nr + npar:nr + npar + no], acc_refs)

    in_specs = [pl.BlockSpec((tr, x.shape[1]), lambda i: (i, 0)) for x in rows]
    in_specs += [pl.BlockSpec(p.shape, lambda i: (0, 0)) for p in params]
    out_specs = [pl.BlockSpec((tr, c), lambda i: (i, 0)) for c, _ in outs]
    out_specs += [pl.BlockSpec(s, lambda i: (0, 0)) for s in accs]
    out_shape = [jax.ShapeDtypeStruct((t, c), dt) for c, dt in outs]
    out_shape += [jax.ShapeDtypeStruct(s, F32) for s in accs]
    return _pcall(name, kern, (t // tr,), (*rows, *params), in_specs, out_shape, out_specs,
                  ("arbitrary",) if accs else ("parallel",))


def _rms_stats(x):
    r = lax.rsqrt(jnp.mean(x * x, axis=-1, keepdims=True) + EPS)
    return r, x * r


def _rms_bwd(xh, r, g, dy):
    dxh = dy * g
    dx = r * (dxh - xh * jnp.mean(dxh * xh, axis=-1, keepdims=True))
    return dx, jnp.sum(dy * xh, axis=0, keepdims=True)


def rms_pre(name, x, gains, layer):
    def body(r, p, o, a):
        _, xh = _rms_stats(r[0][...])
        o[0][...] = (xh * p[0][layer:layer + 1, :]).astype(BF16)
    return rowwise(name, body, [x], [gains], [(x.shape[1], BF16)])[0]


def post_add(name, x, y, gains, layer):
    def body(r, p, o, a):
        _, yh = _rms_stats(r[1][...])
        o[0][...] = r[0][...] + yh * p[0][layer:layer + 1, :]
    return rowwise(name, body, [x, y], [gains], [(x.shape[1], F32)])[0]


def post_bwd(name, y, dx, gains, layer):
    def body(r, p, o, a):
        rr, yh = _rms_stats(r[0][...])
        dy, dg = _rms_bwd(yh, rr, p[0][layer:layer + 1, :], r[1][...])
        o[0][...] = dy.astype(BF16)
        a[0][...] += dg
    c = y.shape[1]
    return rowwise(name, body, [y, dx], [gains], [(c, BF16)], [(1, c)])


def pre_bwd(name, x, dh, dx_res, gains, layer):
    def body(r, p, o, a):
        rr, xh = _rms_stats(r[0][...])
        dx, dg = _rms_bwd(xh, rr, p[0][layer:layer + 1, :], r[1][...])
        o[0][...] = r[2][...] + dx
        a[0][...] += dg
    c = x.shape[1]
    return rowwise(name, body, [x, dh, dx_res], [gains], [(c, F32)], [(1, c)])


def post_add_pre(name, x, y, gains_post, layer_post, gains_pre, layer_pre):
    def body(r, p, o, a):
        _, yh = _rms_stats(r[1][...])
        x_new = r[0][...] + yh * p[0][layer_post:layer_post + 1, :]
        o[0][...] = x_new
        _, xh = _rms_stats(x_new)
        o[1][...] = (xh * p[1][layer_pre:layer_pre + 1, :]).astype(BF16)
    c = x.shape[1]
    return rowwise(name, body, [x, y], [gains_post, gains_pre], [(c, F32), (c, BF16)])


def pre_post_bwd(name, x, dh, dx_res, gains_pre, layer_pre, y, gains_post, layer_post):
    def body(r, p, o, a):
        rr, xh = _rms_stats(r[0][...])
        dx_norm, dg_pre = _rms_bwd(xh, rr, p[0][layer_pre:layer_pre + 1, :], r[1][...])
        dx = r[2][...] + dx_norm
        o[0][...] = dx
        a[0][...] += dg_pre
        ry, yh = _rms_stats(r[3][...])
        dy, dg_post = _rms_bwd(yh, ry, p[1][layer_post:layer_post + 1, :], dx)
        o[1][...] = dy.astype(BF16)
        a[1][...] += dg_post
    c = x.shape[1]
    return rowwise(name, body, [x, dh, dx_res, y], [gains_pre, gains_post], [(c, F32), (c, BF16)], [(1, c), (1, c)])


def gain_bwd(name, x, dh):
    def body(r, p, o, a):
        _, xh = _rms_stats(r[0][...])
        a[0][...] += jnp.sum(r[1][...] * xh, axis=0, keepdims=True)
    return rowwise(name, body, [x, dh], [], [], [(1, x.shape[1])])[0]


def _sigmoid(z):
    return 1.0 / (1.0 + jnp.exp(-z))


def swiglu_fwd(name, gu):
    f = gu.shape[1] // 2

    def body(r, p, o, a):
        g = r[0][:, :f].astype(F32)
        u = r[0][:, f:].astype(F32)
        o[0][...] = (g * _sigmoid(g) * u).astype(BF16)
    return rowwise(name, body, [gu], [], [(f, BF16)])[0]


def swiglu_bwd(name, gu, da):
    f = gu.shape[1] // 2

    def body(r, p, o, a):
        g = r[0][:, :f].astype(F32)
        u = r[0][:, f:].astype(F32)
        d = r[1][...].astype(F32)
        sg = _sigmoid(g)
        o[0][:, :f] = (d * u * sg * (1.0 + g * (1.0 - sg))).astype(BF16)
        o[0][:, f:] = (d * g * sg).astype(BF16)
    return rowwise(name, body, [gu, da], [], [(2 * f, BF16)])[0]


def loss_head(name, y, target):
    c = y.shape[1]

    def body(r, p, o, a):
        e = r[0][...] - r[1][...]
        o[0][...] = e * (1.0 / c)
        a[0][...] += jnp.sum(e * e, axis=0, keepdims=True)
    return rowwise(name, body, [y, target], [], [(c, F32)], [(1, c)])


def adamw(name, w, g, m, v):
    c = w.shape[1]

    def body(r, p, o, a):
        wv, gv, mv, vv = (x[...] for x in r)
        m2 = ADAM_B1 * mv + (1.0 - ADAM_B1) * gv
        v2 = ADAM_B2 * vv + (1.0 - ADAM_B2) * (gv * gv)
        m_hat = m2 / (1.0 - ADAM_B1 ** ADAM_STEP)
        v_hat = v2 / (1.0 - ADAM_B2 ** ADAM_STEP)
        o[0][...] = -ADAM_LR * (m_hat / (jnp.sqrt(v_hat) + ADAM_EPS) + ADAM_WD * wv)
        o[1][...] = m2
        o[2][...] = v2
    tr = _tile(w.shape[0], (256, 128, 64, 32, 16, 8))
    return rowwise(name, body, [w, g, m, v], [], [(c, F32)] * 3, tr=tr)


def colwise(name, body, cols, params, outs, pouts=(), tc=128):
    t = cols[0][0].shape[0]
    c = params[0].shape[1] if params else cols[0][0].shape[1]
    nc, npar, no = len(cols), len(params), len(outs)

    def kern(*refs):
        body(refs[:nc], refs[nc:nc + npar], refs[nc + npar:nc + npar + no], refs[nc + npar + no:])

    in_specs = [pl.BlockSpec((t, tc), functools.partial(lambda j, off: (0, j + off), off=off)) for _, off in cols]
    in_specs += [pl.BlockSpec((p.shape[0], tc), lambda j: (0, j)) for p in params]
    out_specs = [pl.BlockSpec((t, tc), lambda j: (0, j)) for _ in outs]
    out_specs += [pl.BlockSpec((r, tc), lambda j: (0, j)) for r in pouts]
    out_shape = [jax.ShapeDtypeStruct((t, c), dt) for dt in outs]
    out_shape += [jax.ShapeDtypeStruct((r, c), F32) for r in pouts]
    return _pcall(name, kern, (c // tc,), (*[x for x, _ in cols], *params), in_specs, out_shape, out_specs,
                  ("parallel",))


def _row_index(shape):
    return lax.broadcasted_iota(jnp.int32, shape, 0)


def _shift_down(x, d, rows):
    return jnp.where(rows >= d, pltpu.roll(x, d, 0), 0.0)


def _shift_up(x, d, rows):
    t = x.shape[0]
    return jnp.where(rows < t - d, pltpu.roll(x, t - d, 0), 0.0)


def sconv_fwd(name, proj, col0, conv_w, tc=128):
    nb = SC_WIDTH // tc

    def body(cl, p, o, po):
        b, c, u = (x[...] for x in cl)
        rows = _row_index(b.shape)
        w = p[0][...]
        z = c * u
        conv = w[2:3] * z + w[1:2] * _shift_down(z, 1, rows) + w[0:1] * _shift_down(z, 2, rows)
        o[0][...] = (b * conv).astype(BF16)
    return colwise(name, body, [(proj, col0), (proj, col0 + nb), (proj, col0 + 2 * nb)], [conv_w], [BF16], tc=tc)[0]


def sconv_bwd(name, proj, col0, conv_w, dyb, dcol0, tc=128):
    nb = SC_WIDTH // tc

    def body(cl, p, o, po):
        b, c, u, dy = (x[...] for x in cl)
        rows = _row_index(b.shape)
        w = p[0][...]
        z = c * u
        z1, z2 = _shift_down(z, 1, rows), _shift_down(z, 2, rows)
        conv = w[2:3] * z + w[1:2] * z1 + w[0:1] * z2
        dconv = dy * b
        dz = w[2:3] * dconv + w[1:2] * _shift_up(dconv, 1, rows) + w[0:1] * _shift_up(dconv, 2, rows)
        o[0][...] = (dy * conv).astype(BF16)
        o[1][...] = (dz * u).astype(BF16)
        o[2][...] = (dz * c).astype(BF16)
        po[0][0:1, :] = jnp.sum(dconv * z2, axis=0, keepdims=True)
        po[0][1:2, :] = jnp.sum(dconv * z1, axis=0, keepdims=True)
        po[0][2:3, :] = jnp.sum(dconv * z, axis=0, keepdims=True)
    return colwise(name, body, [(proj, col0), (proj, col0 + nb), (proj, col0 + 2 * nb), (dyb, dcol0)], [conv_w],
                   [BF16, BF16, BF16], [3], tc=tc)


def _expm1(x):
    series = x * (1.0 + 0.5 * x * (1.0 + x * (1.0 / 3.0) * (1.0 + 0.25 * x * (1.0 + 0.2 * x))))
    return jnp.where(jnp.abs(x) < 0.05, series, jnp.exp(x) - 1.0)


def _log1p(x):
    series = x * (1.0 - x * (0.5 - x * (1.0 / 3.0 - 0.25 * x)))
    return jnp.where(jnp.abs(x) < 0.01, series, jnp.log(1.0 + x))


def _softplus_neg(lam):
    sp = jnp.maximum(-lam, 0.0) + _log1p(jnp.exp(-jnp.abs(lam)))
    return sp, -_sigmoid(-lam)


GELU_C = math.sqrt(2.0 / math.pi)


def _gelu(x):
    th = jnp.tanh(GELU_C * (x + 0.044715 * x * x * x))
    val = 0.5 * x * (1.0 + th)
    grad = 0.5 * (1.0 + th) + 0.5 * x * (1.0 - th * th) * GELU_C * (1.0 + 3.0 * 0.044715 * x * x)
    return val, grad


def rg_conv_fwd(name, gu2, conv_w, conv_b, tc=128):
    nb = D_MODEL // tc

    def body(cl, p, o, po):
        u = cl[0][...]
        rows = _row_index(u.shape)
        w = p[0][...]
        o[0][...] = (w[3:4] * u + w[2:3] * _shift_down(u, 1, rows) + w[1:2] * _shift_down(u, 2, rows)
                     + w[0:1] * _shift_down(u, 3, rows) + p[1][...])
    return colwise(name, body, [(gu2, nb)], [conv_w, conv_b], [F32], tc=tc)[0]


def rg_conv_bwd(name, gu2, duc, conv_w, tc=128):
    nb = D_MODEL // tc

    def body(cl, p, o, po):
        u, d = cl[0][...], cl[1][...]
        rows = _row_index(u.shape)
        w = p[0][...]
        o[0][...] = (w[3:4] * d + w[2:3] * _shift_up(d, 1, rows) + w[1:2] * _shift_up(d, 2, rows)
                     + w[0:1] * _shift_up(d, 3, rows)).astype(BF16)
        for k in range(4):
            uk = u if k == 3 else _shift_down(u, 3 - k, rows)
            po[0][k:k + 1, :] = jnp.sum(d * uk, axis=0, keepdims=True)
        po[1][...] = jnp.sum(d, axis=0, keepdims=True)
    return colwise(name, body, [(gu2, nb), (duc, 0)], [conv_w], [BF16], [4, 1], tc=tc)


def rg_gates_fwd(name, uc, w_a, b_a, w_i, b_i, tr=512):
    t = uc.shape[0]
    tr = min(tr, t)

    def kern(u_ref, wa_ref, ba_ref, wi_ref, bi_ref, r_ref, i_ref):
        ub = u_ref[...].astype(BF16)
        r_ref[...] = _sigmoid(_dot(ub, wa_ref[...]) + ba_ref[...])
        i_ref[...] = _sigmoid(_dot(ub, wi_ref[...]) + bi_ref[...])

    blk = pl.BlockSpec((tr, LRU_BW), lambda n, i: (i, n))
    wspec = pl.BlockSpec((None, LRU_BW, LRU_BW), lambda n, i: (n, 0, 0))
    bspec = pl.BlockSpec((1, LRU_BW), lambda n, i: (0, n))
    return _pcall(name, kern, (LRU_BLOCKS, t // tr), (uc, w_a, b_a, w_i, b_i), [blk, wspec, bspec, wspec, bspec],
                  [jax.ShapeDtypeStruct(uc.shape, F32)] * 2, [blk, blk], ("parallel", "parallel"))


def rg_gates_bwd(name, uc, dzr, dzi, duc_part, w_a, w_i):
    t = uc.shape[0]
    rows = LRU_BW // N_CHIPS

    def kern(u_ref, dr_ref, di_ref, dp_ref, wa_ref, wi_ref, duc_ref, dwa_ref, dwi_ref):
        ub = u_ref[...].astype(BF16)
        dr, di = dr_ref[...], di_ref[...]
        dwa, dwi = _dot(ub, dr, TN), _dot(ub, di, TN)
        for p in range(N_CHIPS):
            dwa_ref[p] = dwa[p * rows:(p + 1) * rows].astype(dwa_ref.dtype)
            dwi_ref[p] = dwi[p * rows:(p + 1) * rows].astype(dwi_ref.dtype)
        duc_ref[...] = dp_ref[...] + _dot(dr, wa_ref[...], NT) + _dot(di, wi_ref[...], NT)

    blk = pl.BlockSpec((t, LRU_BW), lambda n: (0, n))
    wspec = pl.BlockSpec((None, LRU_BW, LRU_BW), lambda n: (n, 0, 0))
    gspec = pl.BlockSpec((N_CHIPS, None, rows, LRU_BW), lambda n: (0, n, 0, 0))
    gshape = jax.ShapeDtypeStruct((N_CHIPS, LRU_BLOCKS, rows, LRU_BW), BF16)
    return _pcall(name, kern, (LRU_BLOCKS,), (uc, dzr, dzi, duc_part, w_a, w_i), [blk, blk, blk, blk, wspec, wspec],
                  [jax.ShapeDtypeStruct(uc.shape, F32), gshape, gshape], [blk, gspec, gspec], ("parallel",))


def _rg_decay(r, lam):
    sp, dsp = _softplus_neg(lam)
    la = -RG_C * r * sp
    a = jnp.exp(la)
    sq = jnp.sqrt(-_expm1(2.0 * la))
    return sp, dsp, a, sq


def rg_scan_fwd(name, gu2, uc, r, i, lam, tc=128):
    def body(cl, p, o, po):
        gate, ucv, rv, iv = (x[...] for x in cl)
        t = gate.shape[0]
        rows = _row_index(gate.shape)
        _, _, a, sq = _rg_decay(rv, p[0][...])
        b = sq * (iv * ucv)
        d = 1
        while d < t:
            keep = rows >= d
            b = a * jnp.where(keep, pltpu.roll(b, d, 0), 0.0) + b
            a = a * jnp.where(keep, pltpu.roll(a, d, 0), 1.0)
            d *= 2
        o[0][...] = (_gelu(gate)[0] * b).astype(BF16)
        o[1][...] = b
    return colwise(name, body, [(gu2, 0), (uc, 0), (r, 0), (i, 0)], [lam], [BF16, F32], tc=tc)


def rg_scan_bwd(name, gu2, uc, r, i, hs, dy, lam, tc=128):
    def body(cl, p, o, po):
        gate, ucv, rv, iv, h, dyv = (x[...] for x in cl)
        t = gate.shape[0]
        rows = _row_index(gate.shape)
        sp, dsp, a, sq = _rg_decay(rv, p[0][...])
        gl, dgl = _gelu(gate)
        o[0][...] = (dyv * h * dgl).astype(BF16)
        g = dyv * gl
        am = _shift_up(a, 1, rows)
        d = 1
        while d < t:
            keep = rows < t - d
            g = am * jnp.where(keep, pltpu.roll(g, t - d, 0), 0.0) + g
            am = am * jnp.where(keep, pltpu.roll(am, t - d, 0), 0.0)
            d *= 2
        da = g * _shift_down(h, 1, rows)
        iu = iv * ucv
        d_iu = g * sq
        dla = da * a - (g * iu) * (a * a) / sq
        dzr = dla * (-RG_C * sp) * rv * (1.0 - rv)
        dzi = d_iu * ucv * iv * (1.0 - iv)
        o[1][...] = dzr.astype(BF16)
        o[2][...] = dzi.astype(BF16)
        o[3][...] = d_iu * iv
        po[0][...] = jnp.sum(dzr, axis=0, keepdims=True)
        po[1][...] = jnp.sum(dzi, axis=0, keepdims=True)
        po[2][...] = jnp.sum(dla * rv, axis=0, keepdims=True) * (-RG_C) * dsp
    return colwise(name, body, [(gu2, 0), (uc, 0), (r, 0), (i, 0), (hs, 0), (dy, 0)], [lam],
                   [BF16, BF16, BF16, F32], [1, 1, 1], tc=tc)


def _split3(x):
    hi = x.astype(BF16)
    r1 = x - hi.astype(F32)
    mid = r1.astype(BF16)
    lo = (r1 - mid.astype(F32)).astype(BF16)
    return hi, mid, lo


def _tri_dot(x, tri):
    out = None
    for piece in _split3(x):
        term = lax.dot_general(piece, tri, NN, preferred_element_type=F32)
        out = term if out is None else out + term
    return out


def fox_gates_fwd(name, z_t, b_f):
    h, t = z_t.shape
    tb = min(512, t)

    def kern(z_ref, b_ref, o_ref):
        z = z_ref[...] + b_ref[...]
        logf = jnp.minimum(z, 0.0) - _log1p(jnp.exp(-jnp.abs(z)))
        src = lax.broadcasted_iota(jnp.int32, (t, tb), 0)
        dst = lax.broadcasted_iota(jnp.int32, (t, tb), 1) + pl.program_id(0) * tb
        o_ref[...] = _tri_dot(logf, (src <= dst).astype(BF16))

    return _pcall(name, kern, (t // tb,), (z_t, b_f),
                  [pl.BlockSpec((h, t), lambda j: (0, 0)), pl.BlockSpec((h, 1), lambda j: (0, 0))],
                  jax.ShapeDtypeStruct((h, t), F32), pl.BlockSpec((h, tb), lambda j: (0, j)), ("parallel",))


def fox_gates_bwd(name, z_t, b_f, dcum_t):
    h, t = z_t.shape
    tb = min(512, t)

    def kern(z_ref, b_ref, d_ref, dz_ref, db_ref):
        @pl.when(pl.program_id(0) == 0)
        def _():
            db_ref[...] = jnp.zeros_like(db_ref)
        src = lax.broadcasted_iota(jnp.int32, (t, tb), 0)
        dst = lax.broadcasted_iota(jnp.int32, (t, tb), 1) + pl.program_id(0) * tb
        dlogf = _tri_dot(d_ref[...], (src >= dst).astype(BF16))
        z = z_ref[...] + b_ref[...]
        dz = dlogf * _sigmoid(-z)
        dz_ref[...] = dz
        db_ref[...] += jnp.sum(dz, axis=1, keepdims=True)

    return _pcall(name, kern, (t // tb,), (z_t, b_f, dcum_t),
                  [pl.BlockSpec((h, tb), lambda j: (0, j)), pl.BlockSpec((h, 1), lambda j: (0, 0)),
                   pl.BlockSpec((h, t), lambda j: (0, 0))],
                  [jax.ShapeDtypeStruct((h, t), F32), jax.ShapeDtypeStruct((h, 1), F32)],
                  [pl.BlockSpec((h, tb), lambda j: (0, j)), pl.BlockSpec((h, 1), lambda j: (0, 0))], ("arbitrary",))


def _fox_spans(qs, k_ref, cr_ref, i, tq):
    n0 = i * tq
    sd = _dot(qs, k_ref[n0:n0 + tq, :], NT) - cr_ref[:, n0:n0 + tq]
    row = lax.broadcasted_iota(jnp.int32, (tq, tq), 0)
    col = lax.broadcasted_iota(jnp.int32, (tq, tq), 1)
    spans = [(n0, tq, jnp.where(row >= col, sd, NEG_INF))]
    if i > 0:
        spans.append((0, n0, _dot(qs, k_ref[0:n0, :], NT) - cr_ref[:, 0:n0]))
    return spans


def fox_fwd(name, q, k, v, cum_r, tq=256):
    h, t, dh = q.shape
    tq = min(tq, t)
    scale = FOX_HEAD_DIM ** -0.5

    def kern(q_ref, k_ref, v_ref, cr_ref, o_ref, lse_ref):
        for i in range(t // tq):
            rows = slice(i * tq, (i + 1) * tq)
            spans = _fox_spans(q_ref[rows, :] * scale, k_ref, cr_ref, i, tq)
            m = functools.reduce(jnp.maximum, [jnp.max(s, axis=-1, keepdims=True) for _, _, s in spans])
            l, acc = 0.0, 0.0
            for k0, kn, s in spans:
                p = jnp.exp(s - m)
                l = l + jnp.sum(p, axis=-1, keepdims=True)
                acc = acc + _dot(p, v_ref[k0:k0 + kn, :])
            o_ref[rows, :] = (acc / l).astype(o_ref.dtype)
            lse_ref[rows, :] = m + jnp.log(l)

    hspec = pl.BlockSpec((None, t, dh), lambda a: (a, 0, 0))
    cspec = pl.BlockSpec((None, t, 1), lambda a: (a, 0, 0))
    rspec = pl.BlockSpec((None, 1, t), lambda a: (a, 0, 0))
    return _pcall(name, kern, (h,), (q, k, v, cum_r), [hspec, hspec, hspec, rspec],
                  [jax.ShapeDtypeStruct((h, t, dh), BF16), jax.ShapeDtypeStruct((h, t, 1), F32)],
                  [hspec, cspec], ("parallel",))


def fox_bwd(name, q, k, v, do, lse, cum_r, tq=256):
    h, t, dh = q.shape
    tq = min(tq, t)
    scale = FOX_HEAD_DIM ** -0.5

    def kern(q_ref, k_ref, v_ref, do_ref, lse_ref, cr_ref, dq_ref, dk_ref, dv_ref, dc_ref):
        dk_ref[...] = jnp.zeros_like(dk_ref)
        dv_ref[...] = jnp.zeros_like(dv_ref)
        dc_ref[...] = jnp.zeros_like(dc_ref)
        for i in range(t // tq):
            rows = slice(i * tq, (i + 1) * tq)
            qs, dov, lse_v = q_ref[rows, :] * scale, do_ref[rows, :], lse_ref[rows, :]
            spans = _fox_spans(qs, k_ref, cr_ref, i, tq)
            probs = [jnp.exp(s - lse_v) for _, _, s in spans]
            dps = [_dot(dov, v_ref[k0:k0 + kn, :], NT) for k0, kn, _ in spans]
            rowdot = sum(jnp.sum(dp * p, axis=-1, keepdims=True) for dp, p in zip(dps, probs))
            dq = 0.0
            for (k0, kn, _), p, dp in zip(spans, probs, dps):
                ds = p * (dp - rowdot)
                dq = dq + _dot(ds, k_ref[k0:k0 + kn, :])
                dk_ref[k0:k0 + kn, :] += _dot(ds, qs, TN)
                dv_ref[k0:k0 + kn, :] += _dot(p, dov, TN)
                dc_ref[:, k0:k0 + kn] -= jnp.sum(ds, axis=0, keepdims=True)
            dq_ref[rows, :] = (dq * scale).astype(dq_ref.dtype)

    hspec = pl.BlockSpec((None, t, dh), lambda a: (a, 0, 0))
    cspec = pl.BlockSpec((None, t, 1), lambda a: (a, 0, 0))
    rspec = pl.BlockSpec((None, 1, t), lambda a: (a, 0, 0))
    return _pcall(name, kern, (h,), (q, k, v, do, lse, cum_r), [hspec, hspec, hspec, hspec, cspec, rspec],
                  [jax.ShapeDtypeStruct((h, t, dh), BF16), jax.ShapeDtypeStruct((h, t, dh), F32),
                   jax.ShapeDtypeStruct((h, t, dh), F32), jax.ShapeDtypeStruct((h, 1, t), F32)],
                  [hspec, hspec, hspec, rspec], ("parallel",))


def _xattn_probs(q, k):
    s = _dot(q, k, NT) * (MEM_HEAD_DIM ** -0.5)
    p = jnp.exp(s - jnp.max(s, axis=-1, keepdims=True))
    return p / jnp.sum(p, axis=-1, keepdims=True)


def xattn_fwd(name, q, kv, tq=512):
    t = q.shape[0]
    tq = min(tq, t)
    ml = kv.shape[0]

    def kern(q_ref, k_ref, v_ref, o_ref):
        o_ref[...] = _dot(_xattn_probs(q_ref[...], k_ref[...]), v_ref[...]).astype(o_ref.dtype)

    qspec = pl.BlockSpec((tq, MEM_HEAD_DIM), lambda i, a: (i, a))
    return _pcall(name, kern, (t // tq, MEM_HEADS), (q, kv, kv),
                  [qspec, pl.BlockSpec((ml, MEM_HEAD_DIM), lambda i, a: (0, a)),
                   pl.BlockSpec((ml, MEM_HEAD_DIM), lambda i, a: (0, MEM_HEADS + a))],
                  jax.ShapeDtypeStruct(q.shape, BF16), qspec, ("parallel", "parallel"))


def xattn_bwd(name, q, kv, do, tq=512):
    t = q.shape[0]
    tq = min(tq, t)
    ml = kv.shape[0]
    scale = MEM_HEAD_DIM ** -0.5

    def kern(q_ref, k_ref, v_ref, do_ref, dq_ref, dk_ref, dv_ref):
        @pl.when(pl.program_id(1) == 0)
        def _():
            dk_ref[...] = jnp.zeros_like(dk_ref)
            dv_ref[...] = jnp.zeros_like(dv_ref)
        qv, kv_, dov = q_ref[...], k_ref[...], do_ref[...]
        p = _xattn_probs(qv, kv_)
        dp = _dot(dov, v_ref[...], NT)
        ds = p * (dp - jnp.sum(dp * p, axis=-1, keepdims=True)) * scale
        dq_ref[...] = _dot(ds, kv_).astype(dq_ref.dtype)
        dk_ref[...] += _dot(ds, qv, TN)
        dv_ref[...] += _dot(p, dov, TN)

    qspec = pl.BlockSpec((tq, MEM_HEAD_DIM), lambda a, i: (i, a))
    kspec = pl.BlockSpec((ml, MEM_HEAD_DIM), lambda a, i: (0, a))
    return _pcall(name, kern, (MEM_HEADS, t // tq), (q, kv, kv, do),
                  [qspec, kspec, pl.BlockSpec((ml, MEM_HEAD_DIM), lambda a, i: (0, MEM_HEADS + a)), qspec],
                  [jax.ShapeDtypeStruct(q.shape, BF16), jax.ShapeDtypeStruct((ml, D_MODEL), F32),
                   jax.ShapeDtypeStruct((ml, D_MODEL), F32)],
                  [qspec, kspec, kspec], ("parallel", "arbitrary"))


def _heads(x):
    t = x.shape[0]
    return x.reshape(t, FOX_HEADS, FOX_HEAD_DIM).transpose(1, 0, 2)


def _unheads(x):
    return x.transpose(1, 0, 2).reshape(x.shape[1], FOX_WIDTH)


def _row_cut(dw):
    return dw.reshape(N_CHIPS, 2, dw.shape[0] // (2 * N_CHIPS), dw.shape[1])


def local_step(x, mem, target, w, layer_weights=None, reduce_hook=None):
    depth = w["g_mix_pre"].shape[0]
    t = x.shape[0]
    saved = []
    i1, i2, i3 = 3 * FOX_WIDTH, 3 * FOX_WIDTH + FOX_HEADS, AB_IN
    ncol = 128

    def stacked_weights(layer, part, _):
        names = COMMON_BIG if part == "rest" else layer_big(layer)[len(COMMON_BIG):]
        return {n: w[n][layer if n in COMMON_BIG else layer // 2] for n in names}

    get_weights = layer_weights or stacked_weights
    h1 = rms_pre("l0_mix_pre", x, w["g_mix_pre"], 0)
    for layer in range(depth):
        lw = dict(get_weights(layer, "mix", x))
        s = {"x0": x, "lw": lw}
        tag = f"l{layer}"
        s["h1"] = h1
        if layer % 2 == 0:
            e = layer // 2
            w_in = jnp.pad(lw["ab_w_in"], ((0, 0), (0, AB_IN_PAD - AB_IN)))
            proj = mm(f"{tag}_ab_in", h1, w_in, "nn", F32)
            qkv = proj[:, :i1].astype(BF16).reshape(t, 3, FOX_HEADS, FOX_HEAD_DIM).transpose(1, 2, 0, 3)
            z_t = proj[:, i1:i2].T
            b_f = w["ab_b_f"][e].reshape(FOX_HEADS, 1)
            cum_t = fox_gates_fwd(f"{tag}_fox_gates", z_t, b_f)
            cum_r = cum_t[:, None, :]
            oh, lse = fox_fwd(f"{tag}_fox", qkv[0], qkv[1], qkv[2], cum_r)
            bcu = proj[:, i2:i3]
            y_b = sconv_fwd(f"{tag}_sconv", bcu, 0, w["ab_conv_w"][e])
            ycat = jnp.concatenate([_unheads(oh), y_b], axis=1)
            y1 = mm(f"{tag}_ab_out", ycat, lw["ab_w_out"], "nn", F32)
            s.update(w_in=w_in, qkv=qkv, z_t=z_t, b_f=b_f, cum_r=cum_r, lse=lse, bcu=bcu, ycat=ycat)
        else:
            o = layer // 2
            gu2 = mm(f"{tag}_c_in", h1, lw["c_w_in"], "nn", F32)
            conv_b = w["c_conv_b"][o].reshape(1, -1)
            uc = rg_conv_fwd(f"{tag}_rg_conv", gu2, w["c_conv_w"][o], conv_b)
            b_a, b_i = w["c_b_a"][o].reshape(1, -1), w["c_b_i"][o].reshape(1, -1)
            r, i = rg_gates_fwd(f"{tag}_rg_gates", uc, lw["c_w_a"], b_a, lw["c_w_i"], b_i)
            lam = w["c_lam"][o].reshape(1, -1)
            ymix, hs = rg_scan_fwd(f"{tag}_rg_scan", gu2, uc, r, i, lam)
            y1 = mm(f"{tag}_c_out", ymix, lw["c_w_out"], "nn", F32)
            s.update(gu2=gu2, uc=uc, r=r, i=i, lam=lam, hs=hs, ymix=ymix)
        s["y1"] = y1
        x, h2 = post_add_pre(f"{tag}_mix_post", x, y1, w["g_mix_post"], layer, w["g_cross_pre"], layer)
        lw.update(get_weights(layer, "rest", x))
        s["x1"] = x
        m = rms_pre(f"{tag}_mem_pre", mem, w["g_mem"], layer)
        q = mm(f"{tag}_xq", h2, lw["w_xq"], "nn", BF16)
        kv = mm(f"{tag}_xkv", m, lw["w_xkv"], "nn", BF16)
        o_att = xattn_fwd(f"{tag}_xattn", q, kv)
        y2 = mm(f"{tag}_xo", o_att, lw["w_xo"], "nn", F32)
        s.update(h2=h2, m=m, q=q, kv=kv, o_att=o_att, y2=y2)
        x, h3 = post_add_pre(f"{tag}_cross_post", x, y2, w["g_cross_post"], layer, w["g_ffn_pre"], layer)
        s["x2"] = x
        gu = mm(f"{tag}_ffn_gu", h3, lw["w_ffn_gu"], "nn", BF16)
        act = swiglu_fwd(f"{tag}_swiglu", gu)
        y3 = mm(f"{tag}_ffn_down", act, lw["w_ffn_down"], "nn", F32)
        s.update(h3=h3, gu=gu, act=act, y3=y3)
        if layer + 1 < depth:
            x, h1 = post_add_pre(f"{tag}_ffn_post", x, y3, w["g_ffn_post"], layer, w["g_mix_pre"], layer + 1)
        else:
            x = post_add(f"{tag}_ffn_post", x, y3, w["g_ffn_post"], layer)
        saved.append(s)

    dx, sq_cols = loss_head("loss_head", x, target)

    grads = {k: [None] * v.shape[0] for k, v in w.items() if k not in BIG}
    big = {}

    def dw(name, a, b, cols_cut=False):
        return mm(name, a, b, "tn", BF16, reduce_layout=True) if cols_cut else _row_cut(mm(name, a, b, "tn", BF16))

    def hook(layer, part, part_grads, after, gains):
        token = None if reduce_hook is None else reduce_hook(layer, part, part_grads, after)
        return gains if token is None else gains + token

    dy3, grads["g_ffn_post"][depth - 1] = post_bwd(f"b{depth - 1}_ffn_post", saved[-1]["y3"], dx, w["g_ffn_post"], depth - 1)
    for layer in reversed(range(depth)):
        s = saved[layer]
        lw = s["lw"]
        tag = f"b{layer}"
        lg = {}
        dact = mm(f"{tag}_ffn_down_dx", dy3, lw["w_ffn_down"], "nt", BF16)
        lg["w_ffn_down"] = dw(f"{tag}_ffn_down_dw", s["act"], dy3)
        dgu = swiglu_bwd(f"{tag}_swiglu", s["gu"], dact)
        dh3 = mm(f"{tag}_ffn_gu_dx", dgu, lw["w_ffn_gu"], "nt", F32)
        lg["w_ffn_gu"] = dw(f"{tag}_ffn_gu_dw", s["h3"], dgu, cols_cut=True)
        g_ffn_pre = hook(layer, "ffn", {}, dh3, w["g_ffn_pre"])
        dx, dy2, grads["g_ffn_pre"][layer], grads["g_cross_post"][layer] = pre_post_bwd(
            f"{tag}_ffn_pre", s["x2"], dh3, dx, g_ffn_pre, layer, s["y2"], w["g_cross_post"], layer)
        do = mm(f"{tag}_xo_dx", dy2, lw["w_xo"], "nt", BF16)
        lg["w_xo"] = dw(f"{tag}_xo_dw", s["o_att"], dy2)
        dq, dk, dv = xattn_bwd(f"{tag}_xattn", s["q"], s["kv"], do)
        dh2 = mm(f"{tag}_xq_dx", dq, lw["w_xq"], "nt", F32)
        lg["w_xq"] = dw(f"{tag}_xq_dw", s["h2"], dq)
        dkv = jnp.concatenate([dk, dv], axis=1).astype(BF16)
        dm = mm(f"{tag}_xkv_dx", dkv, lw["w_xkv"], "nt", F32)
        lg["w_xkv"] = dw(f"{tag}_xkv_dw", s["m"], dkv, cols_cut=True)
        grads["g_mem"][layer] = gain_bwd(f"{tag}_mem_pre", mem, dm)
        g_cross_pre = hook(layer, "rest", lg, dh2, w["g_cross_pre"])
        dx, dy1, grads["g_cross_pre"][layer], grads["g_mix_post"][layer] = pre_post_bwd(
            f"{tag}_cross_pre", s["x1"], dh2, dx, g_cross_pre, layer, s["y1"], w["g_mix_post"], layer)
        rest_grads, lg = lg, {}
        if layer % 2 == 0:
            e = layer // 2
            dycat = mm(f"{tag}_ab_out_dx", dy1, lw["ab_w_out"], "nt", F32)
            g_mix_pre = hook(layer, "mixer", {}, dycat, w["g_mix_pre"])
            lg["ab_w_out"] = dw(f"{tag}_ab_out_dw", s["ycat"], dy1)
            do_h = _heads(dycat[:, :FOX_WIDTH].astype(BF16))
            qkv = s["qkv"]
            dqh, dkh, dvh, dcum = fox_bwd(f"{tag}_fox", qkv[0], qkv[1], qkv[2], do_h, s["lse"], s["cum_r"])
            dz_t, db_f = fox_gates_bwd(f"{tag}_fox_gates", s["z_t"], s["b_f"], dcum.reshape(FOX_HEADS, t))
            grads["ab_b_f"][e] = db_f.reshape(FOX_HEADS)
            db, dc, du, dconv_w = sconv_bwd(f"{tag}_sconv", s["bcu"], 0, w["ab_conv_w"][e], dycat, FOX_WIDTH // ncol)
            grads["ab_conv_w"][e] = dconv_w
            dproj = jnp.concatenate(
                [_unheads(dqh), _unheads(dkh).astype(BF16), _unheads(dvh).astype(BF16), dz_t.T.astype(BF16), db, dc, du,
                 jnp.zeros((t, AB_IN_PAD - AB_IN), BF16)], axis=1)
            dh1 = mm(f"{tag}_ab_in_dx", dproj, s["w_in"], "nt", F32)
            dw_in = mm(f"{tag}_ab_in_dw", s["h1"], dproj, "tn", F32)[:, :AB_IN]
            lg["ab_w_in"] = dw_in.reshape(2, D_MODEL // 2, N_CHIPS, AB_IN // N_CHIPS).transpose(2, 0, 1, 3).astype(BF16)
        else:
            o = layer // 2
            dymix = mm(f"{tag}_c_out_dx", dy1, lw["c_w_out"], "nt", F32)
            g_mix_pre = hook(layer, "mixer", {}, dymix, w["g_mix_pre"])
            lg["c_w_out"] = dw(f"{tag}_c_out_dw", s["ymix"], dy1)
            dgate, dzr, dzi, duc_part, db_a, db_i, dlam = rg_scan_bwd(
                f"{tag}_rg_scan", s["gu2"], s["uc"], s["r"], s["i"], s["hs"], dymix, s["lam"])
            duc, dw_a, dw_i = rg_gates_bwd(f"{tag}_rg_gates", s["uc"], dzr, dzi, duc_part, lw["c_w_a"], lw["c_w_i"])
            lg["c_w_a"] = dw_a.reshape(N_CHIPS, 2, LRU_BW // 2, LRU_BW)
            lg["c_w_i"] = dw_i.reshape(N_CHIPS, 2, LRU_BW // 2, LRU_BW)
            du_raw, dconv_w, dconv_b = rg_conv_bwd(f"{tag}_rg_conv", s["gu2"], duc, w["c_conv_w"][o])
            grads["c_b_a"][o] = db_a.reshape(LRU_BLOCKS, LRU_BW)
            grads["c_b_i"][o] = db_i.reshape(LRU_BLOCKS, LRU_BW)
            grads["c_lam"][o] = dlam.reshape(-1)
            grads["c_conv_w"][o] = dconv_w
            grads["c_conv_b"][o] = dconv_b.reshape(-1)
            dgu2 = jnp.concatenate([dgate, du_raw], axis=1)
            dh1 = mm(f"{tag}_c_in_dx", dgu2, lw["c_w_in"], "nt", F32)
            lg["c_w_in"] = dw(f"{tag}_c_in_dw", s["h1"], dgu2, cols_cut=True)
        if reduce_hook is None:
            big[layer] = {**rest_grads, **lg}
        g_mix_pre = hook(layer, "mix", lg, dh1, g_mix_pre)
        if layer > 0:
            dx, dy3, grads["g_mix_pre"][layer], grads["g_ffn_post"][layer - 1] = pre_post_bwd(
                f"{tag}_mix_pre", s["x0"], dh1, dx, g_mix_pre, layer, saved[layer - 1]["y3"], w["g_ffn_post"], layer - 1)
        else:
            dx, grads["g_mix_pre"][layer] = pre_bwd(f"{tag}_mix_pre", s["x0"], dh1, dx, g_mix_pre, layer)

    for k in list(grads):
        if k.startswith("g_"):
            grads[k] = [g.reshape(-1) for g in grads[k]]
        grads[k] = jnp.stack(grads[k])
    return sq_cols, dx, grads, big


CHIP_FLIPS = ((1, 0), (0, 1), (1, 1))
HBM_SPEC = pl.BlockSpec(memory_space=pltpu.HBM)
VMEM_SPEC = pl.BlockSpec(memory_space=pltpu.VMEM)


def _place():
    return lax.axis_index("x"), lax.axis_index("y"), lax.axis_index("c")


def _flip(v, f):
    return 1 - v if f else v


def _remote(src, dst, send_sem, recv_sem, target):
    return pltpu.make_async_remote_copy(src_ref=src, dst_ref=dst, send_sem=send_sem, recv_sem=recv_sem,
                                        device_id=target, device_id_type=MESH)


SEM_SPEC = pl.BlockSpec(memory_space=pltpu.SEMAPHORE)


def _swap_copies(srcs, lands, send_sems, recv_sems):
    x, y, c = _place()
    return [_remote(src.at[:, 1 - c], land, send_sems.at[len(CHIP_FLIPS) * a], recv_sems.at[len(CHIP_FLIPS) * a],
                    (x, y, 1 - c)) for a, (src, land) in enumerate(zip(srcs, lands))]


def _exchange_copies(srcs, lands, send_sems, recv_sems):
    x, y, c = _place()
    p = 2 * x + y
    cps = []
    for a, (src, land) in enumerate(zip(srcs, lands)):
        for k, (fx, fy) in enumerate(CHIP_FLIPS):
            qx, qy = _flip(x, fx), _flip(y, fy)
            sem = len(CHIP_FLIPS) * a + k
            cps.append(_remote(src.at[2 * qx + qy], land.at[p], send_sems.at[sem], recv_sems.at[sem], (qx, qy, c)))
    return cps


def _gather_copies(srcs, lands, send_sems, recv_sems):
    x, y, c = _place()
    p = 2 * x + y
    cps = []
    for a, (src, land) in enumerate(zip(srcs, lands)):
        for k, (fx, fy) in enumerate(CHIP_FLIPS):
            sem = len(CHIP_FLIPS) * a + k
            cps.append(_remote(src.at[c], land.at[p, c], send_sems.at[sem], recv_sems.at[sem],
                               (_flip(x, fx), _flip(y, fy), c)))
    return cps


def copies_start(name, make_copies, srcs, land_shapes):
    n = len(srcs)

    def body(*refs):
        for cp in make_copies(refs[:n], refs[n:2 * n], refs[2 * n], refs[2 * n + 1]):
            cp.start()
        refs[-1][...] = jnp.zeros_like(refs[-1])

    thru = [pltpu.HBM(b.shape, b.dtype) for b in srcs] + [pltpu.HBM(sh, b.dtype) for sh, b in zip(land_shapes, srcs)]
    outs = pl.pallas_call(
        body, name=name, in_specs=[HBM_SPEC] * (2 * n),
        out_shape=(pltpu.SemaphoreType.DMA((3 * n,)), pltpu.SemaphoreType.DMA((3 * n,)), *thru,
                   jax.ShapeDtypeStruct((8, 128), F32)),
        out_specs=(SEM_SPEC, SEM_SPEC, *[HBM_SPEC] * (2 * n), VMEM_SPEC),
        input_output_aliases={i: 2 + i for i in range(2 * n)},
        compiler_params=pltpu.CompilerParams(has_side_effects=pltpu.SideEffectType.DATAFLOW_SIDE_EFFECTING),
    )(*[pltpu.with_memory_space_constraint(b, pltpu.HBM) for b in srcs],
      *[pltpu.with_memory_space_constraint(lax.empty(sh, b.dtype), pltpu.HBM) for sh, b in zip(land_shapes, srcs)])
    return outs[:-1], outs[-1]


def copies_wait(name, make_copies, state, after):
    send_sems, recv_sems, *thru = state
    n = len(thru) // 2

    def body(*refs):
        for cp in make_copies(refs[:n], refs[n:2 * n], refs[2 * n], refs[2 * n + 1]):
            cp.wait_send()
            cp.wait_recv()

    outs = pl.pallas_call(
        body, name=name, in_specs=[HBM_SPEC] * (2 * n) + [SEM_SPEC, SEM_SPEC, pl.BlockSpec(memory_space=pl.ANY)],
        out_shape=tuple(pltpu.HBM(t.shape, t.dtype) for t in thru), out_specs=tuple([HBM_SPEC] * (2 * n)),
        input_output_aliases={i: i for i in range(2 * n)},
        compiler_params=pltpu.CompilerParams(has_side_effects=pltpu.SideEffectType.DATAFLOW_SIDE_EFFECTING),
    )(*thru, send_sems, recv_sems, after)
    return outs[:n], outs[n:]


def pass_to_sibling(name, shards, lands):
    n = len(lands)

    def body(*refs):
        own, ins, outs = refs[:n], refs[n:2 * n], refs[2 * n:3 * n]
        send_sems, recv_sems = refs[3 * n:]
        x, y, c = _place()
        sibling = (x, y, 1 - c)
        cps = []
        for a in range(n):
            for k, (fx, fy) in enumerate(CHIP_FLIPS):
                q = 2 * _flip(x, fx) + _flip(y, fy)
                cps.append(_remote(ins[a].at[q, c], outs[a].at[q, c], send_sems.at[a, k], recv_sems.at[a, k], sibling))
            cps.append(_remote(own[a], outs[a].at[2 * x + y], send_sems.at[a, 3], recv_sems.at[a, 3], sibling))
        for cp in cps:
            cp.start()
        for cp in cps:
            cp.wait()

    return pl.pallas_call(
        body, name=name, in_specs=[HBM_SPEC] * (2 * n), out_specs=[HBM_SPEC] * n,
        out_shape=[jax.ShapeDtypeStruct(b.shape, b.dtype) for b in lands],
        scratch_shapes=[pltpu.SemaphoreType.DMA((n, 4)), pltpu.SemaphoreType.DMA((n, 4))],
        input_output_aliases={n + i: i for i in range(n)},
    )(*shards, *lands)


def share_halves(bufs):
    n = len(bufs)

    def body(*refs):
        ins, outs = refs[:n], refs[n:2 * n]
        send_sems, recv_sems = refs[2 * n:]
        x, y, c = _place()
        cps = [_remote(ins[a].at[:, c], outs[a].at[:, c], send_sems.at[a], recv_sems.at[a], (x, y, 1 - c))
               for a in range(n)]
        for cp in cps:
            cp.start()
        for cp in cps:
            cp.wait()

    return pl.pallas_call(
        body, name="share_reduced_halves", in_specs=[HBM_SPEC] * n, out_specs=[HBM_SPEC] * n,
        out_shape=[jax.ShapeDtypeStruct(b.shape, b.dtype) for b in bufs],
        scratch_shapes=[pltpu.SemaphoreType.DMA((n,)), pltpu.SemaphoreType.DMA((n,))],
        input_output_aliases={i: i for i in range(n)},
    )(*bufs)


DEVICE_FLIPS = tuple((fx, fy, fc) for fx in (0, 1) for fy in (0, 1) for fc in (0, 1))[1:]


def gather_small(name, v, reduce):
    r, cdim = v.shape
    n_dev = 8

    def body(v_ref, out_ref, *scratch):
        buf = scratch[0] if reduce else out_ref
        send_sems, recv_sems = scratch[-2:]
        x, y, c = _place()
        me = 4 * x + 2 * y + c
        buf[me] = v_ref[...]
        cps = []
        for k, (fx, fy, fc) in enumerate(DEVICE_FLIPS):
            cps.append(_remote(v_ref, buf.at[me], send_sems.at[k], recv_sems.at[k],
                               (_flip(x, fx), _flip(y, fy), _flip(c, fc))))
        for cp in cps:
            cp.start()
        for cp in cps:
            cp.wait()
        if reduce:
            total = buf[0]
            for d in range(1, n_dev):
                total = total + buf[d]
            out_ref[...] = total

    scratch = [pltpu.SemaphoreType.DMA((7,)), pltpu.SemaphoreType.DMA((7,))]
    if reduce:
        scratch = [pltpu.VMEM((n_dev, r, cdim), F32)] + scratch
    out_shape = jax.ShapeDtypeStruct((r, cdim) if reduce else (n_dev, r, cdim), F32)
    return pl.pallas_call(body, name=name, in_specs=[VMEM_SPEC], out_specs=VMEM_SPEC, out_shape=out_shape,
                          scratch_shapes=scratch)(v)


def pair_sum(name, own, got, core):
    _, hx, cols = got.shape
    tr = _tile(hx, (256, 128, 64, 32, 16))

    def kern(core_ref, a_ref, b_ref, o_ref):
        o_ref[...] = (a_ref[...].astype(F32) + b_ref[...].astype(F32)).astype(BF16)

    grid_spec = pltpu.PrefetchScalarGridSpec(
        num_scalar_prefetch=1, grid=(hx // tr,),
        in_specs=[pl.BlockSpec((N_CHIPS, None, tr, cols), lambda i, cr: (0, cr[0], i, 0)),
                  pl.BlockSpec((N_CHIPS, tr, cols), lambda i, cr: (0, i, 0))],
        out_specs=pl.BlockSpec((N_CHIPS, tr, cols), lambda i, cr: (0, i, 0)))
    return pl.pallas_call(
        kern, name=name, grid_spec=grid_spec, out_shape=jax.ShapeDtypeStruct(got.shape, BF16),
        compiler_params=pltpu.CompilerParams(dimension_semantics=("parallel",), vmem_limit_bytes=VMEM_LIMIT_BYTES),
    )(core, own, got)


def chip_sum(name, mine, parts, place, buf, layer):
    _, hx, yd = parts.shape
    tr = _tile(hx, (256, 128, 64, 32, 16))

    def kern(place_ref, m_ref, p_ref, _, o_ref):
        total = None
        for q in range(N_CHIPS):
            term = jnp.where(place_ref[0] == q, m_ref[...], p_ref[q]).astype(F32)
            total = term if total is None else total + term
        o_ref[...] = total

    grid_spec = pltpu.PrefetchScalarGridSpec(
        num_scalar_prefetch=1, grid=(hx // tr,),
        in_specs=[pl.BlockSpec((None, tr, yd), lambda i, pr: (pr[0], i, 0)),
                  pl.BlockSpec((N_CHIPS, tr, yd), lambda i, pr: (0, i, 0)),
                  pl.BlockSpec(memory_space=pl.ANY)],
        out_specs=pl.BlockSpec((None, None, tr, yd), lambda i, pr: (layer, pr[1], i, 0)))
    return pl.pallas_call(
        kern, name=name, grid_spec=grid_spec, out_shape=jax.ShapeDtypeStruct(buf.shape, buf.dtype),
        input_output_aliases={3: 0},
        compiler_params=pltpu.CompilerParams(dimension_semantics=("parallel",), vmem_limit_bytes=VMEM_LIMIT_BYTES),
    )(place, mine, parts, buf)


WEIGHTS = ("g_mix_pre", "g_mix_post", "g_cross_pre", "g_mem", "g_cross_post", "g_ffn_pre", "g_ffn_post", "w_xq", "w_xkv",
           "w_xo", "w_ffn_gu", "w_ffn_down", "ab_w_in", "ab_b_f", "ab_conv_w", "ab_w_out", "c_w_in", "c_conv_w",
           "c_conv_b", "c_w_a", "c_b_a", "c_w_i", "c_b_i", "c_lam", "c_w_out")
SHARD_DIM = {"w_xq": 1, "w_xkv": 2, "w_xo": 1, "w_ffn_gu": 2, "w_ffn_down": 1, "ab_w_in": 2, "ab_conv_w": 2,
             "ab_w_out": 1, "c_w_in": 2, "c_conv_w": 2, "c_conv_b": 1, "c_w_a": 2, "c_b_a": 2, "c_w_i": 2, "c_b_i": 2,
             "c_lam": 1, "c_w_out": 1}
COMMON_BIG = ("w_xq", "w_xkv", "w_xo", "w_ffn_gu", "w_ffn_down")
EVEN_BIG, ODD_BIG = ("ab_w_in", "ab_w_out"), ("c_w_in", "c_w_a", "c_w_i", "c_w_out")
BIG = COMMON_BIG + EVEN_BIG + ODD_BIG


def layer_big(layer):
    return COMMON_BIG + (ODD_BIG if layer % 2 else EVEN_BIG)


SPLIT_LAYERS = (0,)


def chunk_names(layer, part):
    mixer = layer_big(layer)[len(COMMON_BIG):]
    if layer in SPLIT_LAYERS:
        return mixer if part == "mix" else COMMON_BIG
    return layer_big(layer) if part == "mix" else ()


SMALL_SHARDED = ("ab_conv_w", "c_conv_w", "c_conv_b", "c_b_a", "c_b_i", "c_lam")
REPLICATED = ("g_mix_pre", "g_mix_post", "g_cross_pre", "g_mem", "g_cross_post", "g_ffn_pre", "g_ffn_post", "ab_b_f")
PACK_COLS = 1024


def _unshard(g, d):
    shard = g.shape[1:]
    return jnp.moveaxis(g, 0, d).reshape(shard[:d] + (N_CHIPS * shard[d],) + shard[d + 1:])


def _shardify(full, d):
    s = full.shape
    return jnp.moveaxis(full.reshape(s[:d] + (N_CHIPS, s[d] // N_CHIPS) + s[d + 1:]), d, 0)


def _pack(arrays, rows):
    flat = jnp.concatenate([a.reshape(-1).astype(F32) for a in arrays])
    return jnp.pad(flat, (0, rows * PACK_COLS - flat.shape[0])).reshape(rows, PACK_COLS)


def _unpack(packed, shapes):
    flat = packed.reshape(-1)
    out, at = [], 0
    for s in shapes:
        size = math.prod(s)
        out.append(flat[at:at + size].reshape(s))
        at += size
    return out


def _rows_for(shapes):
    return -(-sum(math.prod(s) for s in shapes) // (8 * PACK_COLS)) * 8


def kernel(x, mem, g_mix_pre, g_mix_post, g_cross_pre, g_mem, g_cross_post, g_ffn_pre, g_ffn_post, w_xq, w_xkv, w_xo, w_ffn_gu, w_ffn_down, ab_w_in, ab_b_f, ab_conv_w, ab_w_out, c_w_in, c_conv_w, c_conv_b, c_w_a, c_b_a, c_w_i, c_b_i, c_lam, c_w_out, loss_target, m_g_mix_pre, m_g_mix_post, m_g_cross_pre, m_g_mem, m_g_cross_post, m_g_ffn_pre, m_g_ffn_post, m_w_xq, m_w_xkv, m_w_xo, m_w_ffn_gu, m_w_ffn_down, m_ab_w_in, m_ab_b_f, m_ab_conv_w, m_ab_w_out, m_c_w_in, m_c_conv_w, m_c_conv_b, m_c_w_a, m_c_b_a, m_c_w_i, m_c_b_i, m_c_lam, m_c_w_out, v_g_mix_pre, v_g_mix_post, v_g_cross_pre, v_g_mem, v_g_cross_post, v_g_ffn_pre, v_g_ffn_post, v_w_xq, v_w_xkv, v_w_xo, v_w_ffn_gu, v_w_ffn_down, v_ab_w_in, v_ab_b_f, v_ab_conv_w, v_ab_w_out, v_c_w_in, v_c_conv_w, v_c_conv_b, v_c_w_a, v_c_b_a, v_c_w_i, v_c_b_i, v_c_lam, v_c_w_out):
    given = dict(locals())
    w = {n: given[n] for n in WEIGHTS}
    m_in = {n: given["m_" + n] for n in WEIGHTS}
    v_in = {n: given["v_" + n] for n in WEIGHTS}
    xi, yi, ci = _place()
    chip = 2 * xi + yi

    full = {}
    small_shapes = [w[n].shape for n in SMALL_SHARDED]
    rows_w = _rows_for(small_shapes)
    assert rows_w * PACK_COLS > sum(math.prod(s) for s in small_shapes)
    every = gather_small("gather_small_weights", _pack([w[n] for n in SMALL_SHARDED], rows_w), reduce=False)
    per_chip = every[0::2].reshape(N_CHIPS, -1)
    at = 0
    for n, s in zip(SMALL_SHARDED, small_shapes):
        size = math.prod(s)
        full[n] = _unshard(per_chip[:, at:at + size].reshape(N_CHIPS, *s), SHARD_DIM[n])
        at += size
    for n in REPLICATED:
        full[n] = w[n]
    after_small = every[0, -1, -1].astype(BF16)

    depth = g_mix_pre.shape[0]
    own, gathers, tokens = {}, {}, []
    for layer in range(depth):
        for part in ("mix", "rest"):
            names = chunk_names(layer, part)
            if names:
                tagp = f"l{layer}_{part}"
                own[tagp] = {n: w[n][layer if n in COMMON_BIG else layer // 2].astype(BF16) + after_small for n in names}
                halves = [a.reshape(2, -1, a.shape[-1]) for a in own[tagp].values()]
                gathers[tagp], token = copies_start(f"gather_start_{tagp}", _gather_copies, halves,
                                                    [(N_CHIPS, *h.shape) for h in halves])
                tokens.append(token[0, 0])

    def layer_weights(layer, part, x_in):
        tagp = f"l{layer}_{part}"
        if tagp not in gathers:
            return {}
        shards, lands = copies_wait(f"gather_wait_{tagp}", _gather_copies, gathers[tagp], x_in)
        lands = pass_to_sibling(f"gather_pass_{tagp}", shards, lands)
        return {n: _unshard(g.reshape(N_CHIPS, *mine.shape), SHARD_DIM[n] - 1)
                for (n, mine), g in zip(own[tagp].items(), lands)}

    core_arr = ci.reshape(1).astype(jnp.int32)
    place_arr = jnp.stack([chip, ci]).astype(jnp.int32)
    in_flight, swapping, held = [], [], {}

    def exchange(after):
        layer, tagp, names, state = swapping.pop()
        mine, got = copies_wait(f"swap_wait_{tagp}", _swap_copies, state, after)
        sums = [pair_sum(f"pair_sum_{tagp}_{n}", o, g, core_arr) for n, o, g in zip(names, mine, got)]
        state, token = copies_start(f"exchange_start_{tagp}", _exchange_copies, sums, [b.shape for b in sums])
        in_flight.append((layer, tagp, names, state))
        return token

    def reduce_hook(layer, part, part_grads, after):
        token = exchange(after)[0, 0] if swapping else None
        held.update(part_grads)
        names = chunk_names(layer, part) if part in ("rest", "mix") else ()
        if names:
            tagp = f"l{layer}_{part}"
            mine = [held.pop(n) for n in names]
            state, started = copies_start(f"swap_start_{tagp}", _swap_copies, mine,
                                          [(m.shape[0], *m.shape[2:]) for m in mine])
            swapping.append((layer, tagp, names, state))
            token = started[0, 0] if token is None else token + started[0, 0]
        return token

    sq_cols, dx, grads, _ = local_step(x[0] + sum(tokens), mem[0], loss_target[0], full, layer_weights, reduce_hook)
    exchange(dx)
    loss = lax.psum(0.5 / D_MODEL * jnp.sum(sq_cols), ("x", "y", "c"))

    reduced = {n: lax.empty((w[n].shape[0], 2, math.prod(w[n].shape[1:-1]) // 2, w[n].shape[-1]), F32) for n in BIG}
    for layer, tagp, names, state in in_flight:
        sums, parts = copies_wait(f"exchange_wait_{tagp}", _exchange_copies, state, dx)
        for n, mine, p in zip(names, sums, parts):
            index = layer if n in COMMON_BIG else layer // 2
            reduced[n] = chip_sum(f"chip_sum_{tagp}_{n}", mine, p, place_arr, reduced[n], index)
    grad_out = {n: g.reshape(w[n].shape) for n, g in zip(BIG, share_halves([reduced[n] for n in BIG]))}

    small_names = REPLICATED + SMALL_SHARDED
    small_full_shapes = [grads[n].shape for n in small_names]
    total = gather_small("reduce_small_grads", _pack([grads[n] for n in small_names], _rows_for(small_full_shapes)),
                         reduce=True)
    for n, g in zip(small_names, _unpack(total, small_full_shapes)):
        if n in SHARD_DIM:
            g = lax.dynamic_index_in_dim(_shardify(g, SHARD_DIM[n]), chip, axis=0, keepdims=False)
        grad_out[n] = g

    delta, new_m, new_v = {}, {}, {}
    for n in BIG:
        two_d = lambda a: a.reshape(-1, a.shape[-1])
        d, m2, v2 = adamw(f"adamw_{n}", two_d(w[n]), two_d(grad_out[n]), two_d(m_in[n]), two_d(v_in[n]))
        delta[n], new_m[n], new_v[n] = (a.reshape(w[n].shape) for a in (d, m2, v2))
    shapes = [w[n].shape for n in small_names]
    rows = _rows_for(shapes)
    packed = [_pack([src[n] for n in small_names], rows) for src in (w, grad_out, m_in, v_in)]
    for dst, res in zip((delta, new_m, new_v), adamw("adamw_small", *packed)):
        for n, a in zip(small_names, _unpack(res, shapes)):
            dst[n] = a

    return (loss, dx[None], *[grad_out[n] for n in WEIGHTS], *[delta[n] for n in WEIGHTS],
            *[new_m[n] for n in WEIGHTS], *[new_v[n] for n in WEIGHTS])
```

```python
import functools
import math

import jax
import jax.numpy as jnp
from jax import lax
from jax.experimental import pallas as pl
from jax.experimental.pallas import tpu as pltpu

F32, BF16 = jnp.float32, jnp.bfloat16
D_MODEL = 1024
EPS = 1e-6
NEG_INF = -1e30
FOX_HEADS, FOX_HEAD_DIM, FOX_WIDTH = 8, 64, 512
SC_WIDTH = 512
AB_IN = 3 * FOX_WIDTH + FOX_HEADS + 3 * SC_WIDTH
AB_IN_PAD = 3200
LRU_BW, LRU_BLOCKS = 256, 4
RG_C = 8.0
MEM_HEADS, MEM_HEAD_DIM = 4, 256
ADAM_LR, ADAM_B1, ADAM_B2, ADAM_EPS, ADAM_WD, ADAM_STEP = 0.001, 0.9, 0.999, 1e-08, 0.01, 10
N_CHIPS = 4
MESH = pl.DeviceIdType.MESH
VMEM_LIMIT_BYTES = 48 * 1024 * 1024
MM_OPERAND_TILE_BYTES = 7 * 1024 * 1024

NN = (((1,), (0,)), ((), ()))
NT = (((1,), (1,)), ((), ()))
TN = (((0,), (0,)), ((), ()))


def _dot(a, b, dn=NN):
    return lax.dot_general(a.astype(BF16), b.astype(BF16), dn, preferred_element_type=F32)


def _tile(n, prefs):
    for p in prefs:
        if n % p == 0:
            return p
    return n


def _pcall(name, kern, grid, ins, in_specs, out_shape, out_specs, sem):
    return pl.pallas_call(
        kern, name=name, grid=grid, in_specs=in_specs, out_specs=out_specs, out_shape=out_shape,
        compiler_params=pltpu.CompilerParams(dimension_semantics=sem, vmem_limit_bytes=VMEM_LIMIT_BYTES),
    )(*ins)


def mm(name, a, b, mode, out_dtype, reduce_layout=False):
    if mode == "nn":
        (m, k), n = a.shape, b.shape[1]
    elif mode == "nt":
        (m, k), n = a.shape, b.shape[0]
    else:
        (k, m), n = a.shape, b.shape[1]
    if reduce_layout:
        tm, tn = m // 2, n // N_CHIPS
    else:
        tn = _tile(n, ((1024,) if mode == "tn" else ()) + (512, 640, 256, 128))
        tm = next(c for c in (2048, 1024, 512, 256, 128, m)
                  if m % c == 0 and 2 * c * k <= MM_OPERAND_TILE_BYTES and 4 * c * tn <= MM_OPERAND_TILE_BYTES)
    dn = {"nn": NN, "nt": NT, "tn": TN}[mode]

    def kern(a_ref, b_ref, o_ref):
        o_ref[...] = _dot(a_ref[...], b_ref[...], dn).astype(o_ref.dtype)

    a_spec = pl.BlockSpec((k, tm), lambda i, j: (0, i)) if mode == "tn" else pl.BlockSpec((tm, k), lambda i, j: (i, 0))
    b_spec = pl.BlockSpec((tn, k), lambda i, j: (j, 0)) if mode == "nt" else pl.BlockSpec((k, tn), lambda i, j: (0, j))
    if reduce_layout:
        out_shape = jax.ShapeDtypeStruct((N_CHIPS, 2, tm, tn), out_dtype)
        o_spec = pl.BlockSpec((None, None, tm, tn), lambda i, j: (j, i, 0, 0))
    else:
        out_shape = jax.ShapeDtypeStruct((m, n), out_dtype)
        o_spec = pl.BlockSpec((tm, tn), lambda i, j: (i, j))
    return _pcall(name, kern, (m // tm, n // tn), (a, b), [a_spec, b_spec], out_shape, o_spec, ("parallel", "parallel"))


def rowwise(name, body, rows, params, outs, accs=(), tr=256):
    t = rows[0].shape[0]
    tr = min(tr, t)
    nr, npar, no = len(rows), len(params), len(outs)

    def kern(*refs):
        acc_refs = refs[nr + npar + no:]
        if acc_refs:
            @pl.when(pl.program_id(0) == 0)
            def _():
                for ar in acc_refs:
                    ar[...] = jnp.zeros_like(ar)
        body(refs[:nr], refs[nr:nr + npar], refs[nr + npar:nr + npar + no], acc_refs)

    in_specs = [pl.BlockSpec((tr, x.shape[1]), lambda i: (i, 0)) for x in rows]
    in_specs += [pl.BlockSpec(p.shape, lambda i: (0, 0)) for p in params]
    out_specs = [pl.BlockSpec((tr, c), lambda i: (i, 0)) for c, _ in outs]
    out_specs += [pl.BlockSpec(s, lambda i: (0, 0)) for s in accs]
    out_shape = [jax.ShapeDtypeStruct((t, c), dt) for c, dt in outs]
    out_shape += [jax.ShapeDtypeStruct(s, F32) for s in accs]
    return _pcall(name, kern, (t // tr,), (*rows, *params), in_specs, out_shape, out_specs,
                  ("arbitrary",) if accs else ("parallel",))


def _rms_stats(x):
    r = lax.rsqrt(jnp.mean(x * x, axis=-1, keepdims=True) + EPS)
    return r, x * r


def _rms_bwd(xh, r, g, dy):
    dxh = dy * g
    dx = r * (dxh - xh * jnp.mean(dxh * xh, axis=-1, keepdims=True))
    return dx, jnp.sum(dy * xh, axis=0, keepdims=True)


def rms_pre(name, x, gains, layer):
    def body(r, p, o, a):
        _, xh = _rms_stats(r[0][...])
        o[0][...] = (xh * p[0][layer:layer + 1, :]).astype(BF16)
    return rowwise(name, body, [x], [gains], [(x.shape[1], BF16)])[0]


def post_add(name, x, y, gains, layer):
    def body(r, p, o, a):
        _, yh = _rms_stats(r[1][...])
        o[0][...] = r[0][...] + yh * p[0][layer:layer + 1, :]
    return rowwise(name, body, [x, y], [gains], [(x.shape[1], F32)])[0]


def post_bwd(name, y, dx, gains, layer):
    def body(r, p, o, a):
        rr, yh = _rms_stats(r[0][...])
        dy, dg = _rms_bwd(yh, rr, p[0][layer:layer + 1, :], r[1][...])
        o[0][...] = dy.astype(BF16)
        a[0][...] += dg
    c = y.shape[1]
    return rowwise(name, body, [y, dx], [gains], [(c, BF16)], [(1, c)])


def pre_bwd(name, x, dh, dx_res, gains, layer):
    def body(r, p, o, a):
        rr, xh = _rms_stats(r[0][...])
        dx, dg = _rms_bwd(xh, rr, p[0][layer:layer + 1, :], r[1][...])
        o[0][...] = r[2][...] + dx
        a[0][...] += dg
    c = x.shape[1]
    return rowwise(name, body, [x, dh, dx_res], [gains], [(c, F32)], [(1, c)])


def post_add_pre(name, x, y, gains_post, layer_post, gains_pre, layer_pre):
    def body(r, p, o, a):
        _, yh = _rms_stats(r[1][...])
        x_new = r[0][...] + yh * p[0][layer_post:layer_post + 1, :]
        o[0][...] = x_new
        _, xh = _rms_stats(x_new)
        o[1][...] = (xh * p[1][layer_pre:layer_pre + 1, :]).astype(BF16)
    c = x.shape[1]
    return rowwise(name, body, [x, y], [gains_post, gains_pre], [(c, F32), (c, BF16)])


def pre_post_bwd(name, x, dh, dx_res, gains_pre, layer_pre, y, gains_post, layer_post):
    def body(r, p, o, a):
        rr, xh = _rms_stats(r[0][...])
        dx_norm, dg_pre = _rms_bwd(xh, rr, p[0][layer_pre:layer_pre + 1, :], r[1][...])
        dx = r[2][...] + dx_norm
        o[0][...] = dx
        a[0][...] += dg_pre
        ry, yh = _rms_stats(r[3][...])
        dy, dg_post = _rms_bwd(yh, ry, p[1][layer_post:layer_post + 1, :], dx)
        o[1][...] = dy.astype(BF16)
        a[1][...] += dg_post
    c = x.shape[1]
    return rowwise(name, body, [x, dh, dx_res, y], [gains_pre, gains_post], [(c, F32), (c, BF16)], [(1, c), (1, c)])


def gain_bwd(name, x, dh):
    def body(r, p, o, a):
        _, xh = _rms_stats(r[0][...])
        a[0][...] += jnp.sum(r[1][...] * xh, axis=0, keepdims=True)
    return rowwise(name, body, [x, dh], [], [], [(1, x.shape[1])])[0]


def _sigmoid(z):
    return 1.0 / (1.0 + jnp.exp(-z))


def swiglu_fwd(name, gu):
    f = gu.shape[1] // 2

    def body(r, p, o, a):
        g = r[0][:, :f].astype(F32)
        u = r[0][:, f:].astype(F32)
        o[0][...] = (g * _sigmoid(g) * u).astype(BF16)
    return rowwise(name, body, [gu], [], [(f, BF16)])[0]


def swiglu_bwd(name, gu, da):
    f = gu.shape[1] // 2

    def body(r, p, o, a):
        g = r[0][:, :f].astype(F32)
        u = r[0][:, f:].astype(F32)
        d = r[1][...].astype(F32)
        sg = _sigmoid(g)
        o[0][:, :f] = (d * u * sg * (1.0 + g * (1.0 - sg))).astype(BF16)
        o[0][:, f:] = (d * g * sg).astype(BF16)
    return rowwise(name, body, [gu, da], [], [(2 * f, BF16)])[0]


def loss_head(name, y, target):
    c = y.shape[1]

    def body(r, p, o, a):
        e = r[0][...] - r[1][...]
        o[0][...] = e * (1.0 / c)
        a[0][...] += jnp.sum(e * e, axis=0, keepdims=True)
    return rowwise(name, body, [y, target], [], [(c, F32)], [(1, c)])


def adamw(name, w, g, m, v):
    c = w.shape[1]

    def body(r, p, o, a):
        wv, gv, mv, vv = (x[...] for x in r)
        m2 = ADAM_B1 * mv + (1.0 - ADAM_B1) * gv
        v2 = ADAM_B2 * vv + (1.0 - ADAM_B2) * (gv * gv)
        m_hat = m2 / (1.0 - ADAM_B1 ** ADAM_STEP)
        v_hat = v2 / (1.0 - ADAM_B2 ** ADAM_STEP)
        o[0][...] = -ADAM_LR * (m_hat / (jnp.sqrt(v_hat) + ADAM_EPS) + ADAM_WD * wv)
        o[1][...] = m2
        o[2][...] = v2
        o[3][...] = gv
    tr = _tile(w.shape[0], (256, 128, 64, 32, 16, 8))
    return rowwise(name, body, [w, g, m, v], [], [(c, F32)] * 4, tr=tr)


def colwise(name, body, cols, params, outs, pouts=(), tc=128):
    t = cols[0][0].shape[0]
    c = params[0].shape[1] if params else cols[0][0].shape[1]
    nc, npar, no = len(cols), len(params), len(outs)

    def kern(*refs):
        body(refs[:nc], refs[nc:nc + npar], refs[nc + npar:nc + npar + no], refs[nc + npar + no:])

    in_specs = [pl.BlockSpec((t, tc), functools.partial(lambda j, off: (0, j + off), off=off)) for _, off in cols]
    in_specs += [pl.BlockSpec((p.shape[0], tc), lambda j: (0, j)) for p in params]
    out_specs = [pl.BlockSpec((t, tc), lambda j: (0, j)) for _ in outs]
    out_specs += [pl.BlockSpec((r, tc), lambda j: (0, j)) for r in pouts]
    out_shape = [jax.ShapeDtypeStruct((t, c), dt) for dt in outs]
    out_shape += [jax.ShapeDtypeStruct((r, c), F32) for r in pouts]
    return _pcall(name, kern, (c // tc,), (*[x for x, _ in cols], *params), in_specs, out_shape, out_specs,
                  ("parallel",))


def _row_index(shape):
    return lax.broadcasted_iota(jnp.int32, shape, 0)


def _shift_down(x, d, rows):
    return jnp.where(rows >= d, pltpu.roll(x, d, 0), 0.0)


def _shift_up(x, d, rows):
    t = x.shape[0]
    return jnp.where(rows < t - d, pltpu.roll(x, t - d, 0), 0.0)


def sconv_fwd(name, proj, col0, conv_w, tc=128):
    nb = SC_WIDTH // tc

    def body(cl, p, o, po):
        b, c, u = (x[...] for x in cl)
        rows = _row_index(b.shape)
        w = p[0][...]
        z = c * u
        conv = w[2:3] * z + w[1:2] * _shift_down(z, 1, rows) + w[0:1] * _shift_down(z, 2, rows)
        o[0][...] = (b * conv).astype(BF16)
    return colwise(name, body, [(proj, col0), (proj, col0 + nb), (proj, col0 + 2 * nb)], [conv_w], [BF16], tc=tc)[0]


def sconv_bwd(name, proj, col0, conv_w, dyb, dcol0, tc=128):
    nb = SC_WIDTH // tc

    def body(cl, p, o, po):
        b, c, u, dy = (x[...] for x in cl)
        rows = _row_index(b.shape)
        w = p[0][...]
        z = c * u
        z1, z2 = _shift_down(z, 1, rows), _shift_down(z, 2, rows)
        conv = w[2:3] * z + w[1:2] * z1 + w[0:1] * z2
        dconv = dy * b
        dz = w[2:3] * dconv + w[1:2] * _shift_up(dconv, 1, rows) + w[0:1] * _shift_up(dconv, 2, rows)
        o[0][...] = (dy * conv).astype(BF16)
        o[1][...] = (dz * u).astype(BF16)
        o[2][...] = (dz * c).astype(BF16)
        po[0][0:1, :] = jnp.sum(dconv * z2, axis=0, keepdims=True)
        po[0][1:2, :] = jnp.sum(dconv * z1, axis=0, keepdims=True)
        po[0][2:3, :] = jnp.sum(dconv * z, axis=0, keepdims=True)
    return colwise(name, body, [(proj, col0), (proj, col0 + nb), (proj, col0 + 2 * nb), (dyb, dcol0)], [conv_w],
                   [BF16, BF16, BF16], [3], tc=tc)


def _expm1(x):
    series = x * (1.0 + 0.5 * x * (1.0 + x * (1.0 / 3.0) * (1.0 + 0.25 * x * (1.0 + 0.2 * x))))
    return jnp.where(jnp.abs(x) < 0.05, series, jnp.exp(x) - 1.0)


def _log1p(x):
    series = x * (1.0 - x * (0.5 - x * (1.0 / 3.0 - 0.25 * x)))
    return jnp.where(jnp.abs(x) < 0.01, series, jnp.log(1.0 + x))


def _softplus_neg(lam):
    sp = jnp.maximum(-lam, 0.0) + _log1p(jnp.exp(-jnp.abs(lam)))
    return sp, -_sigmoid(-lam)


GELU_C = math.sqrt(2.0 / math.pi)


def _gelu(x):
    th = jnp.tanh(GELU_C * (x + 0.044715 * x * x * x))
    val = 0.5 * x * (1.0 + th)
    grad = 0.5 * (1.0 + th) + 0.5 * x * (1.0 - th * th) * GELU_C * (1.0 + 3.0 * 0.044715 * x * x)
    return val, grad


def rg_conv_fwd(name, gu2, conv_w, conv_b, tc=128):
    nb = D_MODEL // tc

    def body(cl, p, o, po):
        u = cl[0][...]
        rows = _row_index(u.shape)
        w = p[0][...]
        o[0][...] = (w[3:4] * u + w[2:3] * _shift_down(u, 1, rows) + w[1:2] * _shift_down(u, 2, rows)
                     + w[0:1] * _shift_down(u, 3, rows) + p[1][...])
    return colwise(name, body, [(gu2, nb)], [conv_w, conv_b], [F32], tc=tc)[0]


def rg_conv_bwd(name, gu2, duc, conv_w, tc=128):
    nb = D_MODEL // tc

    def body(cl, p, o, po):
        u, d = cl[0][...], cl[1][...]
        rows = _row_index(u.shape)
        w = p[0][...]
        o[0][...] = (w[3:4] * d + w[2:3] * _shift_up(d, 1, rows) + w[1:2] * _shift_up(d, 2, rows)
                     + w[0:1] * _shift_up(d, 3, rows)).astype(BF16)
        for k in range(4):
            uk = u if k == 3 else _shift_down(u, 3 - k, rows)
            po[0][k:k + 1, :] = jnp.sum(d * uk, axis=0, keepdims=True)
        po[1][...] = jnp.sum(d, axis=0, keepdims=True)
    return colwise(name, body, [(gu2, nb), (duc, 0)], [conv_w], [BF16], [4, 1], tc=tc)


def rg_gates_fwd(name, uc, w_a, b_a, w_i, b_i, tr=512):
    t = uc.shape[0]
    tr = min(tr, t)

    def kern(u_ref, wa_ref, ba_ref, wi_ref, bi_ref, r_ref, i_ref):
        ub = u_ref[...].astype(BF16)
        r_ref[...] = _sigmoid(_dot(ub, wa_ref[...]) + ba_ref[...])
        i_ref[...] = _sigmoid(_dot(ub, wi_ref[...]) + bi_ref[...])

    blk = pl.BlockSpec((tr, LRU_BW), lambda n, i: (i, n))
    wspec = pl.BlockSpec((None, LRU_BW, LRU_BW), lambda n, i: (n, 0, 0))
    bspec = pl.BlockSpec((1, LRU_BW), lambda n, i: (0, n))
    return _pcall(name, kern, (LRU_BLOCKS, t // tr), (uc, w_a, b_a, w_i, b_i), [blk, wspec, bspec, wspec, bspec],
                  [jax.ShapeDtypeStruct(uc.shape, F32)] * 2, [blk, blk], ("parallel", "parallel"))


def rg_gates_bwd(name, uc, dzr, dzi, duc_part, w_a, w_i):
    t = uc.shape[0]
    rows = LRU_BW // N_CHIPS

    def kern(u_ref, dr_ref, di_ref, dp_ref, wa_ref, wi_ref, duc_ref, dwa_ref, dwi_ref):
        ub = u_ref[...].astype(BF16)
        dr, di = dr_ref[...], di_ref[...]
        dwa, dwi = _dot(ub, dr, TN), _dot(ub, di, TN)
        for p in range(N_CHIPS):
            dwa_ref[p] = dwa[p * rows:(p + 1) * rows].astype(dwa_ref.dtype)
            dwi_ref[p] = dwi[p * rows:(p + 1) * rows].astype(dwi_ref.dtype)
        duc_ref[...] = dp_ref[...] + _dot(dr, wa_ref[...], NT) + _dot(di, wi_ref[...], NT)

    blk = pl.BlockSpec((t, LRU_BW), lambda n: (0, n))
    wspec = pl.BlockSpec((None, LRU_BW, LRU_BW), lambda n: (n, 0, 0))
    gspec = pl.BlockSpec((N_CHIPS, None, rows, LRU_BW), lambda n: (0, n, 0, 0))
    gshape = jax.ShapeDtypeStruct((N_CHIPS, LRU_BLOCKS, rows, LRU_BW), BF16)
    return _pcall(name, kern, (LRU_BLOCKS,), (uc, dzr, dzi, duc_part, w_a, w_i), [blk, blk, blk, blk, wspec, wspec],
                  [jax.ShapeDtypeStruct(uc.shape, F32), gshape, gshape], [blk, gspec, gspec], ("parallel",))


def _rg_decay(r, lam):
    sp, dsp = _softplus_neg(lam)
    la = -RG_C * r * sp
    a = jnp.exp(la)
    sq = jnp.sqrt(-_expm1(2.0 * la))
    return sp, dsp, a, sq


def rg_scan_fwd(name, gu2, uc, r, i, lam, tc=128):
    def body(cl, p, o, po):
        gate, ucv, rv, iv = (x[...] for x in cl)
        t = gate.shape[0]
        rows = _row_index(gate.shape)
        _, _, a, sq = _rg_decay(rv, p[0][...])
        b = sq * (iv * ucv)
        d = 1
        while d < t:
            keep = rows >= d
            b = a * jnp.where(keep, pltpu.roll(b, d, 0), 0.0) + b
            a = a * jnp.where(keep, pltpu.roll(a, d, 0), 1.0)
            d *= 2
        o[0][...] = (_gelu(gate)[0] * b).astype(BF16)
        o[1][...] = b
    return colwise(name, body, [(gu2, 0), (uc, 0), (r, 0), (i, 0)], [lam], [BF16, F32], tc=tc)


def rg_scan_bwd(name, gu2, uc, r, i, hs, dy, lam, tc=128):
    def body(cl, p, o, po):
        gate, ucv, rv, iv, h, dyv = (x[...] for x in cl)
        t = gate.shape[0]
        rows = _row_index(gate.shape)
        sp, dsp, a, sq = _rg_decay(rv, p[0][...])
        gl, dgl = _gelu(gate)
        o[0][...] = (dyv * h * dgl).astype(BF16)
        g = dyv * gl
        am = _shift_up(a, 1, rows)
        d = 1
        while d < t:
            keep = rows < t - d
            g = am * jnp.where(keep, pltpu.roll(g, t - d, 0), 0.0) + g
            am = am * jnp.where(keep, pltpu.roll(am, t - d, 0), 0.0)
            d *= 2
        da = g * _shift_down(h, 1, rows)
        iu = iv * ucv
        d_iu = g * sq
        dla = da * a - (g * iu) * (a * a) / sq
        dzr = dla * (-RG_C * sp) * rv * (1.0 - rv)
        dzi = d_iu * ucv * iv * (1.0 - iv)
        o[1][...] = dzr.astype(BF16)
        o[2][...] = dzi.astype(BF16)
        o[3][...] = d_iu * iv
        po[0][...] = jnp.sum(dzr, axis=0, keepdims=True)
        po[1][...] = jnp.sum(dzi, axis=0, keepdims=True)
        po[2][...] = jnp.sum(dla * rv, axis=0, keepdims=True) * (-RG_C) * dsp
    return colwise(name, body, [(gu2, 0), (uc, 0), (r, 0), (i, 0), (hs, 0), (dy, 0)], [lam],
                   [BF16, BF16, BF16, F32], [1, 1, 1], tc=tc)


def _split3(x):
    hi = x.astype(BF16)
    r1 = x - hi.astype(F32)
    mid = r1.astype(BF16)
    lo = (r1 - mid.astype(F32)).astype(BF16)
    return hi, mid, lo


def _tri_dot(x, tri):
    out = None
    for piece in _split3(x):
        term = lax.dot_general(piece, tri, NN, preferred_element_type=F32)
        out = term if out is None else out + term
    return out


def fox_gates_fwd(name, z_t, b_f):
    h, t = z_t.shape
    tb = min(512, t)

    def kern(z_ref, b_ref, o_ref):
        z = z_ref[...] + b_ref[...]
        logf = jnp.minimum(z, 0.0) - _log1p(jnp.exp(-jnp.abs(z)))
        src = lax.broadcasted_iota(jnp.int32, (t, tb), 0)
        dst = lax.broadcasted_iota(jnp.int32, (t, tb), 1) + pl.program_id(0) * tb
        o_ref[...] = _tri_dot(logf, (src <= dst).astype(BF16))

    return _pcall(name, kern, (t // tb,), (z_t, b_f),
                  [pl.BlockSpec((h, t), lambda j: (0, 0)), pl.BlockSpec((h, 1), lambda j: (0, 0))],
                  jax.ShapeDtypeStruct((h, t), F32), pl.BlockSpec((h, tb), lambda j: (0, j)), ("parallel",))


def fox_gates_bwd(name, z_t, b_f, dcum_t):
    h, t = z_t.shape
    tb = min(512, t)

    def kern(z_ref, b_ref, d_ref, dz_ref, db_ref):
        @pl.when(pl.program_id(0) == 0)
        def _():
            db_ref[...] = jnp.zeros_like(db_ref)
        src = lax.broadcasted_iota(jnp.int32, (t, tb), 0)
        dst = lax.broadcasted_iota(jnp.int32, (t, tb), 1) + pl.program_id(0) * tb
        dlogf = _tri_dot(d_ref[...], (src >= dst).astype(BF16))
        z = z_ref[...] + b_ref[...]
        dz = dlogf * _sigmoid(-z)
        dz_ref[...] = dz
        db_ref[...] += jnp.sum(dz, axis=1, keepdims=True)

    return _pcall(name, kern, (t // tb,), (z_t, b_f, dcum_t),
                  [pl.BlockSpec((h, tb), lambda j: (0, j)), pl.BlockSpec((h, 1), lambda j: (0, 0)),
                   pl.BlockSpec((h, t), lambda j: (0, 0))],
                  [jax.ShapeDtypeStruct((h, t), F32), jax.ShapeDtypeStruct((h, 1), F32)],
                  [pl.BlockSpec((h, tb), lambda j: (0, j)), pl.BlockSpec((h, 1), lambda j: (0, 0))], ("arbitrary",))


def _fox_spans(qs, k_ref, cr_ref, i, tq):
    n0 = i * tq
    sd = _dot(qs, k_ref[n0:n0 + tq, :], NT) - cr_ref[:, n0:n0 + tq]
    row = lax.broadcasted_iota(jnp.int32, (tq, tq), 0)
    col = lax.broadcasted_iota(jnp.int32, (tq, tq), 1)
    spans = [(n0, tq, jnp.where(row >= col, sd, NEG_INF))]
    if i > 0:
        spans.append((0, n0, _dot(qs, k_ref[0:n0, :], NT) - cr_ref[:, 0:n0]))
    return spans


def fox_fwd(name, q, k, v, cum_r, tq=256):
    h, t, dh = q.shape
    tq = min(tq, t)
    scale = FOX_HEAD_DIM ** -0.5

    def kern(q_ref, k_ref, v_ref, cr_ref, o_ref, lse_ref):
        for i in range(t // tq):
            rows = slice(i * tq, (i + 1) * tq)
            spans = _fox_spans(q_ref[rows, :] * scale, k_ref, cr_ref, i, tq)
            m = functools.reduce(jnp.maximum, [jnp.max(s, axis=-1, keepdims=True) for _, _, s in spans])
            l, acc = 0.0, 0.0
            for k0, kn, s in spans:
                p = jnp.exp(s - m)
                l = l + jnp.sum(p, axis=-1, keepdims=True)
                acc = acc + _dot(p, v_ref[k0:k0 + kn, :])
            o_ref[rows, :] = (acc / l).astype(o_ref.dtype)
            lse_ref[rows, :] = m + jnp.log(l)

    hspec = pl.BlockSpec((None, t, dh), lambda a: (a, 0, 0))
    cspec = pl.BlockSpec((None, t, 1), lambda a: (a, 0, 0))
    rspec = pl.BlockSpec((None, 1, t), lambda a: (a, 0, 0))
    return _pcall(name, kern, (h,), (q, k, v, cum_r), [hspec, hspec, hspec, rspec],
                  [jax.ShapeDtypeStruct((h, t, dh), BF16), jax.ShapeDtypeStruct((h, t, 1), F32)],
                  [hspec, cspec], ("parallel",))


def fox_bwd(name, q, k, v, do, lse, cum_r, tq=256):
    h, t, dh = q.shape
    tq = min(tq, t)
    scale = FOX_HEAD_DIM ** -0.5

    def kern(q_ref, k_ref, v_ref, do_ref, lse_ref, cr_ref, dq_ref, dk_ref, dv_ref, dc_ref):
        dk_ref[...] = jnp.zeros_like(dk_ref)
        dv_ref[...] = jnp.zeros_like(dv_ref)
        dc_ref[...] = jnp.zeros_like(dc_ref)
        for i in range(t // tq):
            rows = slice(i * tq, (i + 1) * tq)
            qs, dov, lse_v = q_ref[rows, :] * scale, do_ref[rows, :], lse_ref[rows, :]
            spans = _fox_spans(qs, k_ref, cr_ref, i, tq)
            probs = [jnp.exp(s - lse_v) for _, _, s in spans]
            dps = [_dot(dov, v_ref[k0:k0 + kn, :], NT) for k0, kn, _ in spans]
            rowdot = sum(jnp.sum(dp * p, axis=-1, keepdims=True) for dp, p in zip(dps, probs))
            dq = 0.0
            for (k0, kn, _), p, dp in zip(spans, probs, dps):
                ds = p * (dp - rowdot)
                dq = dq + _dot(ds, k_ref[k0:k0 + kn, :])
                dk_ref[k0:k0 + kn, :] += _dot(ds, qs, TN)
                dv_ref[k0:k0 + kn, :] += _dot(p, dov, TN)
                dc_ref[:, k0:k0 + kn] -= jnp.sum(ds, axis=0, keepdims=True)
            dq_ref[rows, :] = (dq * scale).astype(dq_ref.dtype)

    hspec = pl.BlockSpec((None, t, dh), lambda a: (a, 0, 0))
    cspec = pl.BlockSpec((None, t, 1), lambda a: (a, 0, 0))
    rspec = pl.BlockSpec((None, 1, t), lambda a: (a, 0, 0))
    return _pcall(name, kern, (h,), (q, k, v, do, lse, cum_r), [hspec, hspec, hspec, hspec, cspec, rspec],
                  [jax.ShapeDtypeStruct((h, t, dh), BF16), jax.ShapeDtypeStruct((h, t, dh), F32),
                   jax.ShapeDtypeStruct((h, t, dh), F32), jax.ShapeDtypeStruct((h, 1, t), F32)],
                  [hspec, hspec, hspec, rspec], ("parallel",))


def _xattn_probs(q, k):
    s = _dot(q, k, NT) * (MEM_HEAD_DIM ** -0.5)
    p = jnp.exp(s - jnp.max(s, axis=-1, keepdims=True))
    return p / jnp.sum(p, axis=-1, keepdims=True)


def xattn_fwd(name, q, kv, tq=512):
    t = q.shape[0]
    tq = min(tq, t)
    ml = kv.shape[0]

    def kern(q_ref, k_ref, v_ref, o_ref):
        o_ref[...] = _dot(_xattn_probs(q_ref[...], k_ref[...]), v_ref[...]).astype(o_ref.dtype)

    qspec = pl.BlockSpec((tq, MEM_HEAD_DIM), lambda i, a: (i, a))
    return _pcall(name, kern, (t // tq, MEM_HEADS), (q, kv, kv),
                  [qspec, pl.BlockSpec((ml, MEM_HEAD_DIM), lambda i, a: (0, a)),
                   pl.BlockSpec((ml, MEM_HEAD_DIM), lambda i, a: (0, MEM_HEADS + a))],
                  jax.ShapeDtypeStruct(q.shape, BF16), qspec, ("parallel", "parallel"))


def xattn_bwd(name, q, kv, do, tq=512):
    t = q.shape[0]
    tq = min(tq, t)
    ml = kv.shape[0]
    scale = MEM_HEAD_DIM ** -0.5

    def kern(q_ref, k_ref, v_ref, do_ref, dq_ref, dk_ref, dv_ref):
        @pl.when(pl.program_id(1) == 0)
        def _():
            dk_ref[...] = jnp.zeros_like(dk_ref)
            dv_ref[...] = jnp.zeros_like(dv_ref)
        qv, kv_, dov = q_ref[...], k_ref[...], do_ref[...]
        p = _xattn_probs(qv, kv_)
        dp = _dot(dov, v_ref[...], NT)
        ds = p * (dp - jnp.sum(dp * p, axis=-1, keepdims=True)) * scale
        dq_ref[...] = _dot(ds, kv_).astype(dq_ref.dtype)
        dk_ref[...] += _dot(ds, qv, TN)
        dv_ref[...] += _dot(p, dov, TN)

    qspec = pl.BlockSpec((tq, MEM_HEAD_DIM), lambda a, i: (i, a))
    kspec = pl.BlockSpec((ml, MEM_HEAD_DIM), lambda a, i: (0, a))
    return _pcall(name, kern, (MEM_HEADS, t // tq), (q, kv, kv, do),
                  [qspec, kspec, pl.BlockSpec((ml, MEM_HEAD_DIM), lambda a, i: (0, MEM_HEADS + a)), qspec],
                  [jax.ShapeDtypeStruct(q.shape, BF16), jax.ShapeDtypeStruct((ml, D_MODEL), F32),
                   jax.ShapeDtypeStruct((ml, D_MODEL), F32)],
                  [qspec, kspec, kspec], ("parallel", "arbitrary"))


def _heads(x):
    t = x.shape[0]
    return x.reshape(t, FOX_HEADS, FOX_HEAD_DIM).transpose(1, 0, 2)


def _unheads(x):
    return x.transpose(1, 0, 2).reshape(x.shape[1], FOX_WIDTH)


def _row_cut(dw):
    return dw.reshape(N_CHIPS, 2, dw.shape[0] // (2 * N_CHIPS), dw.shape[1])


def local_step(x, mem, target, w, layer_weights=None, reduce_hook=None):
    depth = w["g_mix_pre"].shape[0]
    t = x.shape[0]
    saved = []
    i1, i2, i3 = 3 * FOX_WIDTH, 3 * FOX_WIDTH + FOX_HEADS, AB_IN
    ncol = 128

    def stacked_weights(layer, part, _):
        names = COMMON_BIG if part == "rest" else layer_big(layer)[len(COMMON_BIG):]
        return {n: w[n][layer if n in COMMON_BIG else layer // 2] for n in names}

    get_weights = layer_weights or stacked_weights
    h1 = rms_pre("l0_mix_pre", x, w["g_mix_pre"], 0)
    for layer in range(depth):
        lw = dict(get_weights(layer, "mix", x))
        s = {"x0": x, "lw": lw}
        tag = f"l{layer}"
        s["h1"] = h1
        if layer % 2 == 0:
            e = layer // 2
            w_in = jnp.pad(lw["ab_w_in"], ((0, 0), (0, AB_IN_PAD - AB_IN)))
            proj = mm(f"{tag}_ab_in", h1, w_in, "nn", F32)
            qkv = proj[:, :i1].astype(BF16).reshape(t, 3, FOX_HEADS, FOX_HEAD_DIM).transpose(1, 2, 0, 3)
            z_t = proj[:, i1:i2].T
            b_f = w["ab_b_f"][e].reshape(FOX_HEADS, 1)
            cum_t = fox_gates_fwd(f"{tag}_fox_gates", z_t, b_f)
            cum_r = cum_t[:, None, :]
            oh, lse = fox_fwd(f"{tag}_fox", qkv[0], qkv[1], qkv[2], cum_r)
            bcu = proj[:, i2:i3]
            y_b = sconv_fwd(f"{tag}_sconv", bcu, 0, w["ab_conv_w"][e])
            ycat = jnp.concatenate([_unheads(oh), y_b], axis=1)
            y1 = mm(f"{tag}_ab_out", ycat, lw["ab_w_out"], "nn", F32)
            s.update(w_in=w_in, qkv=qkv, z_t=z_t, b_f=b_f, cum_r=cum_r, lse=lse, bcu=bcu, ycat=ycat)
        else:
            o = layer // 2
            gu2 = mm(f"{tag}_c_in", h1, lw["c_w_in"], "nn", F32)
            conv_b = w["c_conv_b"][o].reshape(1, -1)
            uc = rg_conv_fwd(f"{tag}_rg_conv", gu2, w["c_conv_w"][o], conv_b)
            b_a, b_i = w["c_b_a"][o].reshape(1, -1), w["c_b_i"][o].reshape(1, -1)
            r, i = rg_gates_fwd(f"{tag}_rg_gates", uc, lw["c_w_a"], b_a, lw["c_w_i"], b_i)
            lam = w["c_lam"][o].reshape(1, -1)
            ymix, hs = rg_scan_fwd(f"{tag}_rg_scan", gu2, uc, r, i, lam)
            y1 = mm(f"{tag}_c_out", ymix, lw["c_w_out"], "nn", F32)
            s.update(gu2=gu2, uc=uc, r=r, i=i, lam=lam, hs=hs, ymix=ymix)
        s["y1"] = y1
        x, h2 = post_add_pre(f"{tag}_mix_post", x, y1, w["g_mix_post"], layer, w["g_cross_pre"], layer)
        lw.update(get_weights(layer, "rest", x))
        s["x1"] = x
        m = rms_pre(f"{tag}_mem_pre", mem, w["g_mem"], layer)
        q = mm(f"{tag}_xq", h2, lw["w_xq"], "nn", BF16)
        kv = mm(f"{tag}_xkv", m, lw["w_xkv"], "nn", BF16)
        o_att = xattn_fwd(f"{tag}_xattn", q, kv)
        y2 = mm(f"{tag}_xo", o_att, lw["w_xo"], "nn", F32)
        s.update(h2=h2, m=m, q=q, kv=kv, o_att=o_att, y2=y2)
        x, h3 = post_add_pre(f"{tag}_cross_post", x, y2, w["g_cross_post"], layer, w["g_ffn_pre"], layer)
        s["x2"] = x
        gu = mm(f"{tag}_ffn_gu", h3, lw["w_ffn_gu"], "nn", BF16)
        act = swiglu_fwd(f"{tag}_swiglu", gu)
        y3 = mm(f"{tag}_ffn_down", act, lw["w_ffn_down"], "nn", F32)
        s.update(h3=h3, gu=gu, act=act, y3=y3)
        if layer + 1 < depth:
            x, h1 = post_add_pre(f"{tag}_ffn_post", x, y3, w["g_ffn_post"], layer, w["g_mix_pre"], layer + 1)
        else:
            x = post_add(f"{tag}_ffn_post", x, y3, w["g_ffn_post"], layer)
        saved.append(s)

    dx, sq_cols = loss_head("loss_head", x, target)

    grads = {k: [None] * v.shape[0] for k, v in w.items() if k not in BIG}
    big = {}

    def dw(name, a, b, cols_cut=False):
        return mm(name, a, b, "tn", BF16, reduce_layout=True) if cols_cut else _row_cut(mm(name, a, b, "tn", BF16))

    def hook(layer, part, part_grads, after, gains):
        token = None if reduce_hook is None else reduce_hook(layer, part, part_grads, after)
        return gains if token is None else gains + token

    dy3, grads["g_ffn_post"][depth - 1] = post_bwd(f"b{depth - 1}_ffn_post", saved[-1]["y3"], dx, w["g_ffn_post"], depth - 1)
    for layer in reversed(range(depth)):
        s = saved[layer]
        lw = s["lw"]
        tag = f"b{layer}"
        lg = {}
        dact = mm(f"{tag}_ffn_down_dx", dy3, lw["w_ffn_down"], "nt", BF16)
        lg["w_ffn_down"] = dw(f"{tag}_ffn_down_dw", s["act"], dy3)
        dgu = swiglu_bwd(f"{tag}_swiglu", s["gu"], dact)
        dh3 = mm(f"{tag}_ffn_gu_dx", dgu, lw["w_ffn_gu"], "nt", F32)
        lg["w_ffn_gu"] = dw(f"{tag}_ffn_gu_dw", s["h3"], dgu, cols_cut=True)
        g_ffn_pre = hook(layer, "ffn", {}, dh3, w["g_ffn_pre"])
        dx, dy2, grads["g_ffn_pre"][layer], grads["g_cross_post"][layer] = pre_post_bwd(
            f"{tag}_ffn_pre", s["x2"], dh3, dx, g_ffn_pre, layer, s["y2"], w["g_cross_post"], layer)
        do = mm(f"{tag}_xo_dx", dy2, lw["w_xo"], "nt", BF16)
        lg["w_xo"] = dw(f"{tag}_xo_dw", s["o_att"], dy2)
        dq, dk, dv = xattn_bwd(f"{tag}_xattn", s["q"], s["kv"], do)
        dh2 = mm(f"{tag}_xq_dx", dq, lw["w_xq"], "nt", F32)
        lg["w_xq"] = dw(f"{tag}_xq_dw", s["h2"], dq)
        dkv = jnp.concatenate([dk, dv], axis=1).astype(BF16)
        dm = mm(f"{tag}_xkv_dx", dkv, lw["w_xkv"], "nt", F32)
        lg["w_xkv"] = dw(f"{tag}_xkv_dw", s["m"], dkv, cols_cut=True)
        grads["g_mem"][layer] = gain_bwd(f"{tag}_mem_pre", mem, dm)
        g_cross_pre = hook(layer, "rest", lg, dh2, w["g_cross_pre"])
        dx, dy1, grads["g_cross_pre"][layer], grads["g_mix_post"][layer] = pre_post_bwd(
            f"{tag}_cross_pre", s["x1"], dh2, dx, g_cross_pre, layer, s["y1"], w["g_mix_post"], layer)
        rest_grads, lg = lg, {}
        if layer % 2 == 0:
            e = layer // 2
            dycat = mm(f"{tag}_ab_out_dx", dy1, lw["ab_w_out"], "nt", F32)
            g_mix_pre = hook(layer, "mixer", {}, dycat, w["g_mix_pre"])
            lg["ab_w_out"] = dw(f"{tag}_ab_out_dw", s["ycat"], dy1)
            do_h = _heads(dycat[:, :FOX_WIDTH].astype(BF16))
            qkv = s["qkv"]
            dqh, dkh, dvh, dcum = fox_bwd(f"{tag}_fox", qkv[0], qkv[1], qkv[2], do_h, s["lse"], s["cum_r"])
            dz_t, db_f = fox_gates_bwd(f"{tag}_fox_gates", s["z_t"], s["b_f"], dcum.reshape(FOX_HEADS, t))
            grads["ab_b_f"][e] = db_f.reshape(FOX_HEADS)
            db, dc, du, dconv_w = sconv_bwd(f"{tag}_sconv", s["bcu"], 0, w["ab_conv_w"][e], dycat, FOX_WIDTH // ncol)
            grads["ab_conv_w"][e] = dconv_w
            dproj = jnp.concatenate(
                [_unheads(dqh), _unheads(dkh).astype(BF16), _unheads(dvh).astype(BF16), dz_t.T.astype(BF16), db, dc, du,
                 jnp.zeros((t, AB_IN_PAD - AB_IN), BF16)], axis=1)
            dh1 = mm(f"{tag}_ab_in_dx", dproj, s["w_in"], "nt", F32)
            dw_in = mm(f"{tag}_ab_in_dw", s["h1"], dproj, "tn", F32)[:, :AB_IN]
            lg["ab_w_in"] = dw_in.reshape(2, D_MODEL // 2, N_CHIPS, AB_IN // N_CHIPS).transpose(2, 0, 1, 3).astype(BF16)
        else:
            o = layer // 2
            dymix = mm(f"{tag}_c_out_dx", dy1, lw["c_w_out"], "nt", F32)
            g_mix_pre = hook(layer, "mixer", {}, dymix, w["g_mix_pre"])
            lg["c_w_out"] = dw(f"{tag}_c_out_dw", s["ymix"], dy1)
            dgate, dzr, dzi, duc_part, db_a, db_i, dlam = rg_scan_bwd(
                f"{tag}_rg_scan", s["gu2"], s["uc"], s["r"], s["i"], s["hs"], dymix, s["lam"])
            duc, dw_a, dw_i = rg_gates_bwd(f"{tag}_rg_gates", s["uc"], dzr, dzi, duc_part, lw["c_w_a"], lw["c_w_i"])
            lg["c_w_a"] = dw_a.reshape(N_CHIPS, 2, LRU_BW // 2, LRU_BW)
            lg["c_w_i"] = dw_i.reshape(N_CHIPS, 2, LRU_BW // 2, LRU_BW)
            du_raw, dconv_w, dconv_b = rg_conv_bwd(f"{tag}_rg_conv", s["gu2"], duc, w["c_conv_w"][o])
            grads["c_b_a"][o] = db_a.reshape(LRU_BLOCKS, LRU_BW)
            grads["c_b_i"][o] = db_i.reshape(LRU_BLOCKS, LRU_BW)
            grads["c_lam"][o] = dlam.reshape(-1)
            grads["c_conv_w"][o] = dconv_w
            grads["c_conv_b"][o] = dconv_b.reshape(-1)
            dgu2 = jnp.concatenate([dgate, du_raw], axis=1)
            dh1 = mm(f"{tag}_c_in_dx", dgu2, lw["c_w_in"], "nt", F32)
            lg["c_w_in"] = dw(f"{tag}_c_in_dw", s["h1"], dgu2, cols_cut=True)
        if reduce_hook is None:
            big[layer] = {**rest_grads, **lg}
        g_mix_pre = hook(layer, "mix", lg, dh1, g_mix_pre)
        if layer > 0:
            dx, dy3, grads["g_mix_pre"][layer], grads["g_ffn_post"][layer - 1] = pre_post_bwd(
                f"{tag}_mix_pre", s["x0"], dh1, dx, g_mix_pre, layer, saved[layer - 1]["y3"], w["g_ffn_post"], layer - 1)
        else:
            dx, grads["g_mix_pre"][layer] = pre_bwd(f"{tag}_mix_pre", s["x0"], dh1, dx, g_mix_pre, layer)

    for k in list(grads):
        if k.startswith("g_"):
            grads[k] = [g.reshape(-1) for g in grads[k]]
        grads[k] = jnp.stack(grads[k])
    return sq_cols, dx, grads, big


CHIP_FLIPS = ((1, 0), (0, 1), (1, 1))
HBM_SPEC = pl.BlockSpec(memory_space=pltpu.HBM)
VMEM_SPEC = pl.BlockSpec(memory_space=pltpu.VMEM)


def _place():
    return lax.axis_index("x"), lax.axis_index("y"), lax.axis_index("c")


def _flip(v, f):
    return 1 - v if f else v


def _remote(src, dst, send_sem, recv_sem, target):
    return pltpu.make_async_remote_copy(src_ref=src, dst_ref=dst, send_sem=send_sem, recv_sem=recv_sem,
                                        device_id=target, device_id_type=MESH)


SEM_SPEC = pl.BlockSpec(memory_space=pltpu.SEMAPHORE)


def _swap_copies(srcs, lands, send_sems, recv_sems):
    x, y, c = _place()
    return [_remote(src.at[:, 1 - c], land, send_sems.at[len(CHIP_FLIPS) * a], recv_sems.at[len(CHIP_FLIPS) * a],
                    (x, y, 1 - c)) for a, (src, land) in enumerate(zip(srcs, lands))]


def _exchange_copies(srcs, lands, send_sems, recv_sems):
    x, y, c = _place()
    p = 2 * x + y
    cps = []
    for a, (src, land) in enumerate(zip(srcs, lands)):
        for k, (fx, fy) in enumerate(CHIP_FLIPS):
            qx, qy = _flip(x, fx), _flip(y, fy)
            sem = len(CHIP_FLIPS) * a + k
            cps.append(_remote(src.at[2 * qx + qy], land.at[p], send_sems.at[sem], recv_sems.at[sem], (qx, qy, c)))
    return cps


def _gather_copies(srcs, lands, send_sems, recv_sems):
    x, y, c = _place()
    p = 2 * x + y
    cps = []
    for a, (src, land) in enumerate(zip(srcs, lands)):
        for k, (fx, fy) in enumerate(CHIP_FLIPS):
            sem = len(CHIP_FLIPS) * a + k
            cps.append(_remote(src.at[c], land.at[p, c], send_sems.at[sem], recv_sems.at[sem],
                               (_flip(x, fx), _flip(y, fy), c)))
    return cps


def copies_start(name, make_copies, srcs, land_shapes):
    n = len(srcs)

    def body(*refs):
        for cp in make_copies(refs[:n], refs[n:2 * n], refs[2 * n], refs[2 * n + 1]):
            cp.start()
        refs[-1][...] = jnp.zeros_like(refs[-1])

    thru = [pltpu.HBM(b.shape, b.dtype) for b in srcs] + [pltpu.HBM(sh, b.dtype) for sh, b in zip(land_shapes, srcs)]
    outs = pl.pallas_call(
        body, name=name, in_specs=[HBM_SPEC] * (2 * n),
        out_shape=(pltpu.SemaphoreType.DMA((3 * n,)), pltpu.SemaphoreType.DMA((3 * n,)), *thru,
                   jax.ShapeDtypeStruct((8, 128), F32)),
        out_specs=(SEM_SPEC, SEM_SPEC, *[HBM_SPEC] * (2 * n), VMEM_SPEC),
        input_output_aliases={i: 2 + i for i in range(2 * n)},
        compiler_params=pltpu.CompilerParams(has_side_effects=pltpu.SideEffectType.DATAFLOW_SIDE_EFFECTING),
    )(*[pltpu.with_memory_space_constraint(b, pltpu.HBM) for b in srcs],
      *[pltpu.with_memory_space_constraint(lax.empty(sh, b.dtype), pltpu.HBM) for sh, b in zip(land_shapes, srcs)])
    return outs[:-1], outs[-1]


def copies_wait(name, make_copies, state, after):
    send_sems, recv_sems, *thru = state
    n = len(thru) // 2

    def body(*refs):
        for cp in make_copies(refs[:n], refs[n:2 * n], refs[2 * n], refs[2 * n + 1]):
            cp.wait_send()
            cp.wait_recv()

    outs = pl.pallas_call(
        body, name=name, in_specs=[HBM_SPEC] * (2 * n) + [SEM_SPEC, SEM_SPEC, pl.BlockSpec(memory_space=pl.ANY)],
        out_shape=tuple(pltpu.HBM(t.shape, t.dtype) for t in thru), out_specs=tuple([HBM_SPEC] * (2 * n)),
        input_output_aliases={i: i for i in range(2 * n)},
        compiler_params=pltpu.CompilerParams(has_side_effects=pltpu.SideEffectType.DATAFLOW_SIDE_EFFECTING),
    )(*thru, send_sems, recv_sems, after)
    return outs[:n], outs[n:]


def pass_to_sibling(name, shards, lands):
    n = len(lands)

    def body(*refs):
        own, ins, outs = refs[:n], refs[n:2 * n], refs[2 * n:3 * n]
        send_sems, recv_sems = refs[3 * n:]
        x, y, c = _place()
        sibling = (x, y, 1 - c)
        cps = []
        for a in range(n):
            for k, (fx, fy) in enumerate(CHIP_FLIPS):
                q = 2 * _flip(x, fx) + _flip(y, fy)
                cps.append(_remote(ins[a].at[q, c], outs[a].at[q, c], send_sems.at[a, k], recv_sems.at[a, k], sibling))
            cps.append(_remote(own[a], outs[a].at[2 * x + y], send_sems.at[a, 3], recv_sems.at[a, 3], sibling))
        for cp in cps:
            cp.start()
        for cp in cps:
            cp.wait()

    return pl.pallas_call(
        body, name=name, in_specs=[HBM_SPEC] * (2 * n), out_specs=[HBM_SPEC] * n,
        out_shape=[jax.ShapeDtypeStruct(b.shape, b.dtype) for b in lands],
        scratch_shapes=[pltpu.SemaphoreType.DMA((n, 4)), pltpu.SemaphoreType.DMA((n, 4))],
        input_output_aliases={n + i: i for i in range(n)},
    )(*shards, *lands)


def share_halves(bufs):
    n = len(bufs)

    def body(*refs):
        ins, outs, token = refs[:n], refs[n:2 * n], refs[2 * n]
        send_sems, recv_sems = refs[2 * n + 1:]
        x, y, c = _place()
        cps = [_remote(ins[a].at[:, c], outs[a].at[:, c], send_sems.at[a], recv_sems.at[a], (x, y, 1 - c))
               for a in range(n)]
        for cp in cps:
            cp.start()
        token[...] = jnp.zeros_like(token)
        for cp in cps:
            cp.wait()

    outs = pl.pallas_call(
        body, name="share_reduced_halves", in_specs=[HBM_SPEC] * n, out_specs=[HBM_SPEC] * n + [VMEM_SPEC],
        out_shape=[jax.ShapeDtypeStruct(b.shape, b.dtype) for b in bufs] + [jax.ShapeDtypeStruct((8, 128), F32)],
        scratch_shapes=[pltpu.SemaphoreType.DMA((n,)), pltpu.SemaphoreType.DMA((n,))],
        input_output_aliases={i: i for i in range(n)},
    )(*bufs)
    return outs[:n], outs[n]


DEVICE_FLIPS = tuple((fx, fy, fc) for fx in (0, 1) for fy in (0, 1) for fc in (0, 1))[1:]


def gather_small(name, v, reduce):
    r, cdim = v.shape
    n_dev = 8

    def body(v_ref, out_ref, *scratch):
        buf = scratch[0] if reduce else out_ref
        send_sems, recv_sems = scratch[-2:]
        x, y, c = _place()
        me = 4 * x + 2 * y + c
        buf[me] = v_ref[...]
        cps = []
        for k, (fx, fy, fc) in enumerate(DEVICE_FLIPS):
            cps.append(_remote(v_ref, buf.at[me], send_sems.at[k], recv_sems.at[k],
                               (_flip(x, fx), _flip(y, fy), _flip(c, fc))))
        for cp in cps:
            cp.start()
        for cp in cps:
            cp.wait()
        if reduce:
            total = buf[0]
            for d in range(1, n_dev):
                total = total + buf[d]
            out_ref[...] = total

    scratch = [pltpu.SemaphoreType.DMA((7,)), pltpu.SemaphoreType.DMA((7,))]
    if reduce:
        scratch = [pltpu.VMEM((n_dev, r, cdim), F32)] + scratch
    out_shape = jax.ShapeDtypeStruct((r, cdim) if reduce else (n_dev, r, cdim), F32)
    return pl.pallas_call(body, name=name, in_specs=[VMEM_SPEC], out_specs=VMEM_SPEC, out_shape=out_shape,
                          scratch_shapes=scratch)(v)


def pair_sum(name, own, got, core):
    _, hx, cols = got.shape
    tr = _tile(hx, (256, 128, 64, 32, 16))

    def kern(core_ref, a_ref, b_ref, o_ref):
        o_ref[...] = (a_ref[...].astype(F32) + b_ref[...].astype(F32)).astype(BF16)

    grid_spec = pltpu.PrefetchScalarGridSpec(
        num_scalar_prefetch=1, grid=(hx // tr,),
        in_specs=[pl.BlockSpec((N_CHIPS, None, tr, cols), lambda i, cr: (0, cr[0], i, 0)),
                  pl.BlockSpec((N_CHIPS, tr, cols), lambda i, cr: (0, i, 0))],
        out_specs=pl.BlockSpec((N_CHIPS, tr, cols), lambda i, cr: (0, i, 0)))
    return pl.pallas_call(
        kern, name=name, grid_spec=grid_spec, out_shape=jax.ShapeDtypeStruct(got.shape, BF16),
        compiler_params=pltpu.CompilerParams(dimension_semantics=("parallel",), vmem_limit_bytes=VMEM_LIMIT_BYTES),
    )(core, own, got)


def chip_sum(name, mine, parts, place, buf, layer):
    _, hx, yd = parts.shape
    tr = _tile(hx, (256, 128, 64, 32, 16))

    def kern(place_ref, m_ref, p_ref, _, o_ref):
        total = None
        for q in range(N_CHIPS):
            term = jnp.where(place_ref[0] == q, m_ref[...], p_ref[q]).astype(F32)
            total = term if total is None else total + term
        o_ref[...] = total

    grid_spec = pltpu.PrefetchScalarGridSpec(
        num_scalar_prefetch=1, grid=(hx // tr,),
        in_specs=[pl.BlockSpec((None, tr, yd), lambda i, pr: (pr[0], i, 0)),
                  pl.BlockSpec((N_CHIPS, tr, yd), lambda i, pr: (0, i, 0)),
                  pl.BlockSpec(memory_space=pl.ANY)],
        out_specs=pl.BlockSpec((None, None, tr, yd), lambda i, pr: (layer, pr[1], i, 0)))
    return pl.pallas_call(
        kern, name=name, grid_spec=grid_spec, out_shape=jax.ShapeDtypeStruct(buf.shape, buf.dtype),
        input_output_aliases={3: 0},
        compiler_params=pltpu.CompilerParams(dimension_semantics=("parallel",), vmem_limit_bytes=VMEM_LIMIT_BYTES),
    )(place, mine, parts, buf)


WEIGHTS = ("g_mix_pre", "g_mix_post", "g_cross_pre", "g_mem", "g_cross_post", "g_ffn_pre", "g_ffn_post", "w_xq", "w_xkv",
           "w_xo", "w_ffn_gu", "w_ffn_down", "ab_w_in", "ab_b_f", "ab_conv_w", "ab_w_out", "c_w_in", "c_conv_w",
           "c_conv_b", "c_w_a", "c_b_a", "c_w_i", "c_b_i", "c_lam", "c_w_out")
SHARD_DIM = {"w_xq": 1, "w_xkv": 2, "w_xo": 1, "w_ffn_gu": 2, "w_ffn_down": 1, "ab_w_in": 2, "ab_conv_w": 2,
             "ab_w_out": 1, "c_w_in": 2, "c_conv_w": 2, "c_conv_b": 1, "c_w_a": 2, "c_b_a": 2, "c_w_i": 2, "c_b_i": 2,
             "c_lam": 1, "c_w_out": 1}
COMMON_BIG = ("w_xq", "w_xkv", "w_xo", "w_ffn_gu", "w_ffn_down")
EVEN_BIG, ODD_BIG = ("ab_w_in", "ab_w_out"), ("c_w_in", "c_w_a", "c_w_i", "c_w_out")
BIG = COMMON_BIG + EVEN_BIG + ODD_BIG


def layer_big(layer):
    return COMMON_BIG + (ODD_BIG if layer % 2 else EVEN_BIG)


SPLIT_LAYERS = (0,)


def chunk_names(layer, part):
    mixer = layer_big(layer)[len(COMMON_BIG):]
    if layer in SPLIT_LAYERS:
        return mixer if part == "mix" else COMMON_BIG
    return layer_big(layer) if part == "mix" else ()


SMALL_SHARDED = ("ab_conv_w", "c_conv_w", "c_conv_b", "c_b_a", "c_b_i", "c_lam")
REPLICATED = ("g_mix_pre", "g_mix_post", "g_cross_pre", "g_mem", "g_cross_post", "g_ffn_pre", "g_ffn_post", "ab_b_f")
PACK_COLS = 1024


def _unshard(g, d):
    shard = g.shape[1:]
    return jnp.moveaxis(g, 0, d).reshape(shard[:d] + (N_CHIPS * shard[d],) + shard[d + 1:])


def _shardify(full, d):
    s = full.shape
    return jnp.moveaxis(full.reshape(s[:d] + (N_CHIPS, s[d] // N_CHIPS) + s[d + 1:]), d, 0)


def _pack(arrays, rows):
    flat = jnp.concatenate([a.reshape(-1).astype(F32) for a in arrays])
    return jnp.pad(flat, (0, rows * PACK_COLS - flat.shape[0])).reshape(rows, PACK_COLS)


def _unpack(packed, shapes):
    flat = packed.reshape(-1)
    out, at = [], 0
    for s in shapes:
        size = math.prod(s)
        out.append(flat[at:at + size].reshape(s))
        at += size
    return out


def _rows_for(shapes):
    return -(-sum(math.prod(s) for s in shapes) // (8 * PACK_COLS)) * 8


def kernel(x, mem, g_mix_pre, g_mix_post, g_cross_pre, g_mem, g_cross_post, g_ffn_pre, g_ffn_post, w_xq, w_xkv, w_xo, w_ffn_gu, w_ffn_down, ab_w_in, ab_b_f, ab_conv_w, ab_w_out, c_w_in, c_conv_w, c_conv_b, c_w_a, c_b_a, c_w_i, c_b_i, c_lam, c_w_out, loss_target, m_g_mix_pre, m_g_mix_post, m_g_cross_pre, m_g_mem, m_g_cross_post, m_g_ffn_pre, m_g_ffn_post, m_w_xq, m_w_xkv, m_w_xo, m_w_ffn_gu, m_w_ffn_down, m_ab_w_in, m_ab_b_f, m_ab_conv_w, m_ab_w_out, m_c_w_in, m_c_conv_w, m_c_conv_b, m_c_w_a, m_c_b_a, m_c_w_i, m_c_b_i, m_c_lam, m_c_w_out, v_g_mix_pre, v_g_mix_post, v_g_cross_pre, v_g_mem, v_g_cross_post, v_g_ffn_pre, v_g_ffn_post, v_w_xq, v_w_xkv, v_w_xo, v_w_ffn_gu, v_w_ffn_down, v_ab_w_in, v_ab_b_f, v_ab_conv_w, v_ab_w_out, v_c_w_in, v_c_conv_w, v_c_conv_b, v_c_w_a, v_c_b_a, v_c_w_i, v_c_b_i, v_c_lam, v_c_w_out):
    given = dict(locals())
    w = {n: given[n] for n in WEIGHTS}
    m_in = {n: given["m_" + n] for n in WEIGHTS}
    v_in = {n: given["v_" + n] for n in WEIGHTS}
    xi, yi, ci = _place()
    chip = 2 * xi + yi

    full = {}
    small_shapes = [w[n].shape for n in SMALL_SHARDED]
    rows_w = _rows_for(small_shapes)
    assert rows_w * PACK_COLS > sum(math.prod(s) for s in small_shapes)
    every = gather_small("gather_small_weights", _pack([w[n] for n in SMALL_SHARDED], rows_w), reduce=False)
    per_chip = every[0::2].reshape(N_CHIPS, -1)
    at = 0
    for n, s in zip(SMALL_SHARDED, small_shapes):
        size = math.prod(s)
        full[n] = _unshard(per_chip[:, at:at + size].reshape(N_CHIPS, *s), SHARD_DIM[n])
        at += size
    for n in REPLICATED:
        full[n] = w[n]
    after_small = every[0, -1, -1].astype(BF16)

    depth = g_mix_pre.shape[0]
    own, gathers, tokens = {}, {}, []
    for layer in range(depth):
        for part in ("mix", "rest"):
            names = chunk_names(layer, part)
            if names:
                tagp = f"l{layer}_{part}"
                own[tagp] = {n: w[n][layer if n in COMMON_BIG else layer // 2].astype(BF16) + after_small for n in names}
                halves = [a.reshape(2, -1, a.shape[-1]) for a in own[tagp].values()]
                gathers[tagp], token = copies_start(f"gather_start_{tagp}", _gather_copies, halves,
                                                    [(N_CHIPS, *h.shape) for h in halves])
                tokens.append(token[0, 0])

    def layer_weights(layer, part, x_in):
        tagp = f"l{layer}_{part}"
        if tagp not in gathers:
            return {}
        shards, lands = copies_wait(f"gather_wait_{tagp}", _gather_copies, gathers[tagp], x_in)
        lands = pass_to_sibling(f"gather_pass_{tagp}", shards, lands)
        return {n: _unshard(g.reshape(N_CHIPS, *mine.shape), SHARD_DIM[n] - 1)
                for (n, mine), g in zip(own[tagp].items(), lands)}

    core_arr = ci.reshape(1).astype(jnp.int32)
    place_arr = jnp.stack([chip, ci]).astype(jnp.int32)
    in_flight, swapping, held = [], [], {}

    def exchange(after):
        layer, tagp, names, state = swapping.pop()
        mine, got = copies_wait(f"swap_wait_{tagp}", _swap_copies, state, after)
        sums = [pair_sum(f"pair_sum_{tagp}_{n}", o, g, core_arr) for n, o, g in zip(names, mine, got)]
        state, token = copies_start(f"exchange_start_{tagp}", _exchange_copies, sums, [b.shape for b in sums])
        in_flight.append((layer, tagp, names, state))
        return token

    def reduce_hook(layer, part, part_grads, after):
        token = exchange(after)[0, 0] if swapping else None
        held.update(part_grads)
        names = chunk_names(layer, part) if part in ("rest", "mix") else ()
        if names:
            tagp = f"l{layer}_{part}"
            mine = [held.pop(n) for n in names]
            state, started = copies_start(f"swap_start_{tagp}", _swap_copies, mine,
                                          [(m.shape[0], *m.shape[2:]) for m in mine])
            swapping.append((layer, tagp, names, state))
            token = started[0, 0] if token is None else token + started[0, 0]
        return token

    sq_cols, dx, grads, _ = local_step(x[0] + sum(tokens), mem[0], loss_target[0], full, layer_weights, reduce_hook)
    exchange(dx)
    loss = lax.psum(0.5 / D_MODEL * jnp.sum(sq_cols), ("x", "y", "c"))

    reduced = {n: lax.empty((w[n].shape[0], 2, math.prod(w[n].shape[1:-1]) // 2, w[n].shape[-1]), F32) for n in BIG}
    for layer, tagp, names, state in in_flight:
        sums, parts = copies_wait(f"exchange_wait_{tagp}", _exchange_copies, state, dx)
        for n, mine, p in zip(names, sums, parts):
            index = layer if n in COMMON_BIG else layer // 2
            reduced[n] = chip_sum(f"chip_sum_{tagp}_{n}", mine, p, place_arr, reduced[n], index)
    shared, after_big = share_halves([reduced[n] for n in BIG])
    grad_out = {n: g.reshape(w[n].shape) for n, g in zip(BIG, shared)}

    small_names = REPLICATED + SMALL_SHARDED
    small_full_shapes = [grads[n].shape for n in small_names]
    total = gather_small("reduce_small_grads",
                         _pack([grads[n] for n in small_names], _rows_for(small_full_shapes)) + after_big[0, 0],
                         reduce=True)
    for n, g in zip(small_names, _unpack(total, small_full_shapes)):
        if n in SHARD_DIM:
            g = lax.dynamic_index_in_dim(_shardify(g, SHARD_DIM[n]), chip, axis=0, keepdims=False)
        grad_out[n] = g

    delta, new_m, new_v = {}, {}, {}
    for n in BIG:
        two_d = lambda a: a.reshape(-1, a.shape[-1])
        results = adamw(f"adamw_{n}", two_d(w[n]), two_d(grad_out[n]), two_d(m_in[n]), two_d(v_in[n]))
        delta[n], new_m[n], new_v[n], grad_out[n] = (a.reshape(w[n].shape) for a in results)
    shapes = [w[n].shape for n in small_names]
    rows = _rows_for(shapes)
    packed = [_pack([src[n] for n in small_names], rows) for src in (w, grad_out, m_in, v_in)]
    for dst, res in zip((delta, new_m, new_v), adamw("adamw_small", *packed)[:3]):
        for n, a in zip(small_names, _unpack(res, shapes)):
            dst[n] = a

    return (loss, dx[None], *[grad_out[n] for n in WEIGHTS], *[delta[n] for n in WEIGHTS],
            *[new_m[n] for n in WEIGHTS], *[new_v[n] for n in WEIGHTS])
```

```python
import functools
import math

import jax
import jax.numpy as jnp
from jax import lax
from jax.experimental import pallas as pl
from jax.experimental.pallas import tpu as pltpu

F32, BF16 = jnp.float32, jnp.bfloat16
D_MODEL = 1024
EPS = 1e-6
NEG_INF = -1e30
FOX_HEADS, FOX_HEAD_DIM, FOX_WIDTH = 8, 64, 512
SC_WIDTH = 512
AB_IN = 3 * FOX_WIDTH + FOX_HEADS + 3 * SC_WIDTH
AB_IN_PAD = 3200
LRU_BW, LRU_BLOCKS = 256, 4
RG_C = 8.0
MEM_HEADS, MEM_HEAD_DIM = 4, 256
ADAM_LR, ADAM_B1, ADAM_B2, ADAM_EPS, ADAM_WD, ADAM_STEP = 0.001, 0.9, 0.999, 1e-08, 0.01, 10
N_CHIPS = 4
MESH = pl.DeviceIdType.MESH
VMEM_LIMIT_BYTES = 48 * 1024 * 1024
MM_OPERAND_TILE_BYTES = 7 * 1024 * 1024

NN = (((1,), (0,)), ((), ()))
NT = (((1,), (1,)), ((), ()))
TN = (((0,), (0,)), ((), ()))


def _dot(a, b, dn=NN):
    return lax.dot_general(a.astype(BF16), b.astype(BF16), dn, preferred_element_type=F32)


def _tile(n, prefs):
    for p in prefs:
        if n % p == 0:
            return p
    return n


def _pcall(name, kern, grid, ins, in_specs, out_shape, out_specs, sem):
    return pl.pallas_call(
        kern, name=name, grid=grid, in_specs=in_specs, out_specs=out_specs, out_shape=out_shape,
        compiler_params=pltpu.CompilerParams(dimension_semantics=sem, vmem_limit_bytes=VMEM_LIMIT_BYTES),
    )(*ins)


def mm(name, a, b, mode, out_dtype, reduce_layout=False):
    if mode == "nn":
        (m, k), n = a.shape, b.shape[1]
    elif mode == "nt":
        (m, k), n = a.shape, b.shape[0]
    else:
        (k, m), n = a.shape, b.shape[1]
    if reduce_layout:
        tm, tn = m // 2, n // N_CHIPS
    else:
        tn = _tile(n, ((1024,) if mode == "tn" else ()) + (512, 640, 256, 128))
        tm = next(c for c in (2048, 1024, 512, 256, 128, m)
                  if m % c == 0 and 2 * c * k <= MM_OPERAND_TILE_BYTES and 4 * c * tn <= MM_OPERAND_TILE_BYTES)
    dn = {"nn": NN, "nt": NT, "tn": TN}[mode]

    def kern(a_ref, b_ref, o_ref):
        o_ref[...] = _dot(a_ref[...], b_ref[...], dn).astype(o_ref.dtype)

    a_spec = pl.BlockSpec((k, tm), lambda i, j: (0, i)) if mode == "tn" else pl.BlockSpec((tm, k), lambda i, j: (i, 0))
    b_spec = pl.BlockSpec((tn, k), lambda i, j: (j, 0)) if mode == "nt" else pl.BlockSpec((k, tn), lambda i, j: (0, j))
    if reduce_layout:
        out_shape = jax.ShapeDtypeStruct((N_CHIPS, 2, tm, tn), out_dtype)
        o_spec = pl.BlockSpec((None, None, tm, tn), lambda i, j: (j, i, 0, 0))
    else:
        out_shape = jax.ShapeDtypeStruct((m, n), out_dtype)
        o_spec = pl.BlockSpec((tm, tn), lambda i, j: (i, j))
    return _pcall(name, kern, (m // tm, n // tn), (a, b), [a_spec, b_spec], out_shape, o_spec, ("parallel", "parallel"))


def rowwise(name, body, rows, params, outs, accs=(), tr=256):
    t = rows[0].shape[0]
    tr = min(tr, t)
    nr, npar, no = len(rows), len(params), len(outs)

    def kern(*refs):
        acc_refs = refs[nr + npar + no:]
        if acc_refs:
            @pl.when(pl.program_id(0) == 0)
            def _():
                for ar in acc_refs:
                    ar[...] = jnp.zeros_like(ar)
        body(refs[:nr], refs[nr:nr + npar], refs[nr + npar:nr + npar + no], acc_refs)

    in_specs = [pl.BlockSpec((tr, x.shape[1]), lambda i: (i, 0)) for x in rows]
    in_specs += [pl.BlockSpec(p.shape, lambda i: (0, 0)) for p in params]
    out_specs = [pl.BlockSpec((tr, c), lambda i: (i, 0)) for c, _ in outs]
    out_specs += [pl.BlockSpec(s, lambda i: (0, 0)) for s in accs]
    out_shape = [jax.ShapeDtypeStruct((t, c), dt) for c, dt in outs]
    out_shape += [jax.ShapeDtypeStruct(s, F32) for s in accs]
    return _pcall(name, kern, (t // tr,), (*rows, *params), in_specs, out_shape, out_specs,
                  ("arbitrary",) if accs else ("parallel",))


def _rms_stats(x):
    r = lax.rsqrt(jnp.mean(x * x, axis=-1, keepdims=True) + EPS)
    return r, x * r


def _rms_bwd(xh, r, g, dy):
    dxh = dy * g
    dx = r * (dxh - xh * jnp.mean(dxh * xh, axis=-1, keepdims=True))
    return dx, jnp.sum(dy * xh, axis=0, keepdims=True)


def rms_pre(name, x, gains, layer):
    def body(r, p, o, a):
        _, xh = _rms_stats(r[0][...])
        o[0][...] = (xh * p[0][layer:layer + 1, :]).astype(BF16)
    return rowwise(name, body, [x], [gains], [(x.shape[1], BF16)])[0]


def post_add(name, x, y, gains, layer):
    def body(r, p, o, a):
        _, yh = _rms_stats(r[1][...])
        o[0][...] = r[0][...] + yh * p[0][layer:layer + 1, :]
    return rowwise(name, body, [x, y], [gains], [(x.shape[1], F32)])[0]


def post_bwd(name, y, dx, gains, layer):
    def body(r, p, o, a):
        rr, yh = _rms_stats(r[0][...])
        dy, dg = _rms_bwd(yh, rr, p[0][layer:layer + 1, :], r[1][...])
        o[0][...] = dy.astype(BF16)
        a[0][...] += dg
    c = y.shape[1]
    return rowwise(name, body, [y, dx], [gains], [(c, BF16)], [(1, c)])


def pre_bwd(name, x, dh, dx_res, gains, layer):
    def body(r, p, o, a):
        rr, xh = _rms_stats(r[0][...])
        dx, dg = _rms_bwd(xh, rr, p[0][layer:layer + 1, :], r[1][...])
        o[0][...] = r[2][...] + dx
        a[0][...] += dg
    c = x.shape[1]
    return rowwise(name, body, [x, dh, dx_res], [gains], [(c, F32)], [(1, c)])


def post_add_pre(name, x, y, gains_post, layer_post, gains_pre, layer_pre):
    def body(r, p, o, a):
        _, yh = _rms_stats(r[1][...])
        x_new = r[0][...] + yh * p[0][layer_post:layer_post + 1, :]
        o[0][...] = x_new
        _, xh = _rms_stats(x_new)
        o[1][...] = (xh * p[1][layer_pre:layer_pre + 1, :]).astype(BF16)
    c = x.shape[1]
    return rowwise(name, body, [x, y], [gains_post, gains_pre], [(c, F32), (c, BF16)])


def pre_post_bwd(name, x, dh, dx_res, gains_pre, layer_pre, y, gains_post, layer_post):
    def body(r, p, o, a):
        rr, xh = _rms_stats(r[0][...])
        dx_norm, dg_pre = _rms_bwd(xh, rr, p[0][layer_pre:layer_pre + 1, :], r[1][...])
        dx = r[2][...] + dx_norm
        o[0][...] = dx
        a[0][...] += dg_pre
        ry, yh = _rms_stats(r[3][...])
        dy, dg_post = _rms_bwd(yh, ry, p[1][layer_post:layer_post + 1, :], dx)
        o[1][...] = dy.astype(BF16)
        a[1][...] += dg_post
    c = x.shape[1]
    return rowwise(name, body, [x, dh, dx_res, y], [gains_pre, gains_post], [(c, F32), (c, BF16)], [(1, c), (1, c)])


def gain_bwd(name, x, dh):
    def body(r, p, o, a):
        _, xh = _rms_stats(r[0][...])
        a[0][...] += jnp.sum(r[1][...] * xh, axis=0, keepdims=True)
    return rowwise(name, body, [x, dh], [], [], [(1, x.shape[1])])[0]


def _sigmoid(z):
    return 1.0 / (1.0 + jnp.exp(-z))


def swiglu_fwd(name, gu):
    f = gu.shape[1] // 2

    def body(r, p, o, a):
        g = r[0][:, :f].astype(F32)
        u = r[0][:, f:].astype(F32)
        o[0][...] = (g * _sigmoid(g) * u).astype(BF16)
    return rowwise(name, body, [gu], [], [(f, BF16)])[0]


def swiglu_bwd(name, gu, da):
    f = gu.shape[1] // 2

    def body(r, p, o, a):
        g = r[0][:, :f].astype(F32)
        u = r[0][:, f:].astype(F32)
        d = r[1][...].astype(F32)
        sg = _sigmoid(g)
        o[0][:, :f] = (d * u * sg * (1.0 + g * (1.0 - sg))).astype(BF16)
        o[0][:, f:] = (d * g * sg).astype(BF16)
    return rowwise(name, body, [gu, da], [], [(2 * f, BF16)])[0]


def loss_head(name, y, target):
    c = y.shape[1]

    def body(r, p, o, a):
        e = r[0][...] - r[1][...]
        o[0][...] = e * (1.0 / c)
        a[0][...] += jnp.sum(e * e, axis=0, keepdims=True)
    return rowwise(name, body, [y, target], [], [(c, F32)], [(1, c)])


def adamw(name, w, g, m, v):
    c = w.shape[1]

    def body(r, p, o, a):
        wv, gv, mv, vv = (x[...] for x in r)
        m2 = ADAM_B1 * mv + (1.0 - ADAM_B1) * gv
        v2 = ADAM_B2 * vv + (1.0 - ADAM_B2) * (gv * gv)
        m_hat = m2 / (1.0 - ADAM_B1 ** ADAM_STEP)
        v_hat = v2 / (1.0 - ADAM_B2 ** ADAM_STEP)
        o[0][...] = -ADAM_LR * (m_hat / (jnp.sqrt(v_hat) + ADAM_EPS) + ADAM_WD * wv)
        o[1][...] = m2
        o[2][...] = v2
        o[3][...] = gv
    tr = _tile(w.shape[0], (256, 128, 64, 32, 16, 8))
    return rowwise(name, body, [w, g, m, v], [], [(c, F32)] * 4, tr=tr)


def colwise(name, body, cols, params, outs, pouts=(), tc=128):
    t = cols[0][0].shape[0]
    c = params[0].shape[1] if params else cols[0][0].shape[1]
    nc, npar, no = len(cols), len(params), len(outs)

    def kern(*refs):
        body(refs[:nc], refs[nc:nc + npar], refs[nc + npar:nc + npar + no], refs[nc + npar + no:])

    in_specs = [pl.BlockSpec((t, tc), functools.partial(lambda j, off: (0, j + off), off=off)) for _, off in cols]
    in_specs += [pl.BlockSpec((p.shape[0], tc), lambda j: (0, j)) for p in params]
    out_specs = [pl.BlockSpec((t, tc), lambda j: (0, j)) for _ in outs]
    out_specs += [pl.BlockSpec((r, tc), lambda j: (0, j)) for r in pouts]
    out_shape = [jax.ShapeDtypeStruct((t, c), dt) for dt in outs]
    out_shape += [jax.ShapeDtypeStruct((r, c), F32) for r in pouts]
    return _pcall(name, kern, (c // tc,), (*[x for x, _ in cols], *params), in_specs, out_shape, out_specs,
                  ("parallel",))


def _row_index(shape):
    return lax.broadcasted_iota(jnp.int32, shape, 0)


def _shift_down(x, d, rows):
    return jnp.where(rows >= d, pltpu.roll(x, d, 0), 0.0)


def _shift_up(x, d, rows):
    t = x.shape[0]
    return jnp.where(rows < t - d, pltpu.roll(x, t - d, 0), 0.0)


def sconv_fwd(name, proj, col0, conv_w, tc=128):
    nb = SC_WIDTH // tc

    def body(cl, p, o, po):
        b, c, u = (x[...] for x in cl)
        rows = _row_index(b.shape)
        w = p[0][...]
        z = c * u
        conv = w[2:3] * z + w[1:2] * _shift_down(z, 1, rows) + w[0:1] * _shift_down(z, 2, rows)
        o[0][...] = (b * conv).astype(BF16)
    return colwise(name, body, [(proj, col0), (proj, col0 + nb), (proj, col0 + 2 * nb)], [conv_w], [BF16], tc=tc)[0]


def sconv_bwd(name, proj, col0, conv_w, dyb, dcol0, tc=128):
    nb = SC_WIDTH // tc

    def body(cl, p, o, po):
        b, c, u, dy = (x[...] for x in cl)
        rows = _row_index(b.shape)
        w = p[0][...]
        z = c * u
        z1, z2 = _shift_down(z, 1, rows), _shift_down(z, 2, rows)
        conv = w[2:3] * z + w[1:2] * z1 + w[0:1] * z2
        dconv = dy * b
        dz = w[2:3] * dconv + w[1:2] * _shift_up(dconv, 1, rows) + w[0:1] * _shift_up(dconv, 2, rows)
        o[0][...] = (dy * conv).astype(BF16)
        o[1][...] = (dz * u).astype(BF16)
        o[2][...] = (dz * c).astype(BF16)
        po[0][0:1, :] = jnp.sum(dconv * z2, axis=0, keepdims=True)
        po[0][1:2, :] = jnp.sum(dconv * z1, axis=0, keepdims=True)
        po[0][2:3, :] = jnp.sum(dconv * z, axis=0, keepdims=True)
    return colwise(name, body, [(proj, col0), (proj, col0 + nb), (proj, col0 + 2 * nb), (dyb, dcol0)], [conv_w],
                   [BF16, BF16, BF16], [3], tc=tc)


def _expm1(x):
    series = x * (1.0 + 0.5 * x * (1.0 + x * (1.0 / 3.0) * (1.0 + 0.25 * x * (1.0 + 0.2 * x))))
    return jnp.where(jnp.abs(x) < 0.05, series, jnp.exp(x) - 1.0)


def _log1p(x):
    series = x * (1.0 - x * (0.5 - x * (1.0 / 3.0 - 0.25 * x)))
    return jnp.where(jnp.abs(x) < 0.01, series, jnp.log(1.0 + x))


def _softplus_neg(lam):
    sp = jnp.maximum(-lam, 0.0) + _log1p(jnp.exp(-jnp.abs(lam)))
    return sp, -_sigmoid(-lam)


GELU_C = math.sqrt(2.0 / math.pi)


def _gelu(x):
    th = jnp.tanh(GELU_C * (x + 0.044715 * x * x * x))
    val = 0.5 * x * (1.0 + th)
    grad = 0.5 * (1.0 + th) + 0.5 * x * (1.0 - th * th) * GELU_C * (1.0 + 3.0 * 0.044715 * x * x)
    return val, grad


def rg_conv_fwd(name, gu2, conv_w, conv_b, tc=128):
    nb = D_MODEL // tc

    def body(cl, p, o, po):
        u = cl[0][...]
        rows = _row_index(u.shape)
        w = p[0][...]
        o[0][...] = (w[3:4] * u + w[2:3] * _shift_down(u, 1, rows) + w[1:2] * _shift_down(u, 2, rows)
                     + w[0:1] * _shift_down(u, 3, rows) + p[1][...])
    return colwise(name, body, [(gu2, nb)], [conv_w, conv_b], [F32], tc=tc)[0]


def rg_conv_bwd(name, gu2, duc, conv_w, tc=128):
    nb = D_MODEL // tc

    def body(cl, p, o, po):
        u, d = cl[0][...], cl[1][...]
        rows = _row_index(u.shape)
        w = p[0][...]
        o[0][...] = (w[3:4] * d + w[2:3] * _shift_up(d, 1, rows) + w[1:2] * _shift_up(d, 2, rows)
                     + w[0:1] * _shift_up(d, 3, rows)).astype(BF16)
        for k in range(4):
            uk = u if k == 3 else _shift_down(u, 3 - k, rows)
            po[0][k:k + 1, :] = jnp.sum(d * uk, axis=0, keepdims=True)
        po[1][...] = jnp.sum(d, axis=0, keepdims=True)
    return colwise(name, body, [(gu2, nb), (duc, 0)], [conv_w], [BF16], [4, 1], tc=tc)


def rg_gates_fwd(name, uc, w_a, b_a, w_i, b_i, tr=512):
    t = uc.shape[0]
    tr = min(tr, t)

    def kern(u_ref, wa_ref, ba_ref, wi_ref, bi_ref, r_ref, i_ref):
        ub = u_ref[...].astype(BF16)
        r_ref[...] = _sigmoid(_dot(ub, wa_ref[...]) + ba_ref[...])
        i_ref[...] = _sigmoid(_dot(ub, wi_ref[...]) + bi_ref[...])

    blk = pl.BlockSpec((tr, LRU_BW), lambda n, i: (i, n))
    wspec = pl.BlockSpec((None, LRU_BW, LRU_BW), lambda n, i: (n, 0, 0))
    bspec = pl.BlockSpec((1, LRU_BW), lambda n, i: (0, n))
    return _pcall(name, kern, (LRU_BLOCKS, t // tr), (uc, w_a, b_a, w_i, b_i), [blk, wspec, bspec, wspec, bspec],
                  [jax.ShapeDtypeStruct(uc.shape, F32)] * 2, [blk, blk], ("parallel", "parallel"))


def rg_gates_bwd(name, uc, dzr, dzi, duc_part, w_a, w_i):
    t = uc.shape[0]
    rows = LRU_BW // N_CHIPS

    def kern(u_ref, dr_ref, di_ref, dp_ref, wa_ref, wi_ref, duc_ref, dwa_ref, dwi_ref):
        ub = u_ref[...].astype(BF16)
        dr, di = dr_ref[...], di_ref[...]
        dwa, dwi = _dot(ub, dr, TN), _dot(ub, di, TN)
        for p in range(N_CHIPS):
            dwa_ref[p] = dwa[p * rows:(p + 1) * rows].astype(dwa_ref.dtype)
            dwi_ref[p] = dwi[p * rows:(p + 1) * rows].astype(dwi_ref.dtype)
        duc_ref[...] = dp_ref[...] + _dot(dr, wa_ref[...], NT) + _dot(di, wi_ref[...], NT)

    blk = pl.BlockSpec((t, LRU_BW), lambda n: (0, n))
    wspec = pl.BlockSpec((None, LRU_BW, LRU_BW), lambda n: (n, 0, 0))
    gspec = pl.BlockSpec((N_CHIPS, None, rows, LRU_BW), lambda n: (0, n, 0, 0))
    gshape = jax.ShapeDtypeStruct((N_CHIPS, LRU_BLOCKS, rows, LRU_BW), BF16)
    return _pcall(name, kern, (LRU_BLOCKS,), (uc, dzr, dzi, duc_part, w_a, w_i), [blk, blk, blk, blk, wspec, wspec],
                  [jax.ShapeDtypeStruct(uc.shape, F32), gshape, gshape], [blk, gspec, gspec], ("parallel",))


def _rg_decay(r, lam):
    sp, dsp = _softplus_neg(lam)
    la = -RG_C * r * sp
    a = jnp.exp(la)
    sq = jnp.sqrt(-_expm1(2.0 * la))
    return sp, dsp, a, sq


def rg_scan_fwd(name, gu2, uc, r, i, lam, tc=128):
    def body(cl, p, o, po):
        gate, ucv, rv, iv = (x[...] for x in cl)
        t = gate.shape[0]
        rows = _row_index(gate.shape)
        _, _, a, sq = _rg_decay(rv, p[0][...])
        b = sq * (iv * ucv)
        d = 1
        while d < t:
            keep = rows >= d
            b = a * jnp.where(keep, pltpu.roll(b, d, 0), 0.0) + b
            a = a * jnp.where(keep, pltpu.roll(a, d, 0), 1.0)
            d *= 2
        o[0][...] = (_gelu(gate)[0] * b).astype(BF16)
        o[1][...] = b
    return colwise(name, body, [(gu2, 0), (uc, 0), (r, 0), (i, 0)], [lam], [BF16, F32], tc=tc)


def rg_scan_bwd(name, gu2, uc, r, i, hs, dy, lam, tc=128):
    def body(cl, p, o, po):
        gate, ucv, rv, iv, h, dyv = (x[...] for x in cl)
        t = gate.shape[0]
        rows = _row_index(gate.shape)
        sp, dsp, a, sq = _rg_decay(rv, p[0][...])
        gl, dgl = _gelu(gate)
        o[0][...] = (dyv * h * dgl).astype(BF16)
        g = dyv * gl
        am = _shift_up(a, 1, rows)
        d = 1
        while d < t:
            keep = rows < t - d
            g = am * jnp.where(keep, pltpu.roll(g, t - d, 0), 0.0) + g
            am = am * jnp.where(keep, pltpu.roll(am, t - d, 0), 0.0)
            d *= 2
        da = g * _shift_down(h, 1, rows)
        iu = iv * ucv
        d_iu = g * sq
        dla = da * a - (g * iu) * (a * a) / sq
        dzr = dla * (-RG_C * sp) * rv * (1.0 - rv)
        dzi = d_iu * ucv * iv * (1.0 - iv)
        o[1][...] = dzr.astype(BF16)
        o[2][...] = dzi.astype(BF16)
        o[3][...] = d_iu * iv
        po[0][...] = jnp.sum(dzr, axis=0, keepdims=True)
        po[1][...] = jnp.sum(dzi, axis=0, keepdims=True)
        po[2][...] = jnp.sum(dla * rv, axis=0, keepdims=True) * (-RG_C) * dsp
    return colwise(name, body, [(gu2, 0), (uc, 0), (r, 0), (i, 0), (hs, 0), (dy, 0)], [lam],
                   [BF16, BF16, BF16, F32], [1, 1, 1], tc=tc)


def _split3(x):
    hi = x.astype(BF16)
    r1 = x - hi.astype(F32)
    mid = r1.astype(BF16)
    lo = (r1 - mid.astype(F32)).astype(BF16)
    return hi, mid, lo


def _tri_dot(x, tri):
    out = None
    for piece in _split3(x):
        term = lax.dot_general(piece, tri, NN, preferred_element_type=F32)
        out = term if out is None else out + term
    return out


def fox_gates_fwd(name, z_t, b_f):
    h, t = z_t.shape
    tb = min(512, t)

    def kern(z_ref, b_ref, o_ref):
        z = z_ref[...] + b_ref[...]
        logf = jnp.minimum(z, 0.0) - _log1p(jnp.exp(-jnp.abs(z)))
        src = lax.broadcasted_iota(jnp.int32, (t, tb), 0)
        dst = lax.broadcasted_iota(jnp.int32, (t, tb), 1) + pl.program_id(0) * tb
        o_ref[...] = _tri_dot(logf, (src <= dst).astype(BF16))

    return _pcall(name, kern, (t // tb,), (z_t, b_f),
                  [pl.BlockSpec((h, t), lambda j: (0, 0)), pl.BlockSpec((h, 1), lambda j: (0, 0))],
                  jax.ShapeDtypeStruct((h, t), F32), pl.BlockSpec((h, tb), lambda j: (0, j)), ("parallel",))


def fox_gates_bwd(name, z_t, b_f, dcum_t):
    h, t = z_t.shape
    tb = min(512, t)

    def kern(z_ref, b_ref, d_ref, dz_ref, db_ref):
        @pl.when(pl.program_id(0) == 0)
        def _():
            db_ref[...] = jnp.zeros_like(db_ref)
        src = lax.broadcasted_iota(jnp.int32, (t, tb), 0)
        dst = lax.broadcasted_iota(jnp.int32, (t, tb), 1) + pl.program_id(0) * tb
        dlogf = _tri_dot(d_ref[...], (src >= dst).astype(BF16))
        z = z_ref[...] + b_ref[...]
        dz = dlogf * _sigmoid(-z)
        dz_ref[...] = dz
        db_ref[...] += jnp.sum(dz, axis=1, keepdims=True)

    return _pcall(name, kern, (t // tb,), (z_t, b_f, dcum_t),
                  [pl.BlockSpec((h, tb), lambda j: (0, j)), pl.BlockSpec((h, 1), lambda j: (0, 0)),
                   pl.BlockSpec((h, t), lambda j: (0, 0))],
                  [jax.ShapeDtypeStruct((h, t), F32), jax.ShapeDtypeStruct((h, 1), F32)],
                  [pl.BlockSpec((h, tb), lambda j: (0, j)), pl.BlockSpec((h, 1), lambda j: (0, 0))], ("arbitrary",))


def _fox_spans(qs, k_ref, cr_ref, i, tq):
    n0 = i * tq
    sd = _dot(qs, k_ref[n0:n0 + tq, :], NT) - cr_ref[:, n0:n0 + tq]
    row = lax.broadcasted_iota(jnp.int32, (tq, tq), 0)
    col = lax.broadcasted_iota(jnp.int32, (tq, tq), 1)
    spans = [(n0, tq, jnp.where(row >= col, sd, NEG_INF))]
    if i > 0:
        spans.append((0, n0, _dot(qs, k_ref[0:n0, :], NT) - cr_ref[:, 0:n0]))
    return spans


def fox_fwd(name, q, k, v, cum_r, tq=256):
    h, t, dh = q.shape
    tq = min(tq, t)
    scale = FOX_HEAD_DIM ** -0.5

    def kern(q_ref, k_ref, v_ref, cr_ref, o_ref, lse_ref):
        for i in range(t // tq):
            rows = slice(i * tq, (i + 1) * tq)
            spans = _fox_spans(q_ref[rows, :] * scale, k_ref, cr_ref, i, tq)
            m = functools.reduce(jnp.maximum, [jnp.max(s, axis=-1, keepdims=True) for _, _, s in spans])
            l, acc = 0.0, 0.0
            for k0, kn, s in spans:
                p = jnp.exp(s - m)
                l = l + jnp.sum(p, axis=-1, keepdims=True)
                acc = acc + _dot(p, v_ref[k0:k0 + kn, :])
            o_ref[rows, :] = (acc / l).astype(o_ref.dtype)
            lse_ref[rows, :] = m + jnp.log(l)

    hspec = pl.BlockSpec((None, t, dh), lambda a: (a, 0, 0))
    cspec = pl.BlockSpec((None, t, 1), lambda a: (a, 0, 0))
    rspec = pl.BlockSpec((None, 1, t), lambda a: (a, 0, 0))
    return _pcall(name, kern, (h,), (q, k, v, cum_r), [hspec, hspec, hspec, rspec],
                  [jax.ShapeDtypeStruct((h, t, dh), BF16), jax.ShapeDtypeStruct((h, t, 1), F32)],
                  [hspec, cspec], ("parallel",))


def fox_bwd(name, q, k, v, do, lse, cum_r, tq=256):
    h, t, dh = q.shape
    tq = min(tq, t)
    scale = FOX_HEAD_DIM ** -0.5

    def kern(q_ref, k_ref, v_ref, do_ref, lse_ref, cr_ref, dq_ref, dk_ref, dv_ref, dc_ref):
        dk_ref[...] = jnp.zeros_like(dk_ref)
        dv_ref[...] = jnp.zeros_like(dv_ref)
        dc_ref[...] = jnp.zeros_like(dc_ref)
        for i in range(t // tq):
            rows = slice(i * tq, (i + 1) * tq)
            qs, dov, lse_v = q_ref[rows, :] * scale, do_ref[rows, :], lse_ref[rows, :]
            spans = _fox_spans(qs, k_ref, cr_ref, i, tq)
            probs = [jnp.exp(s - lse_v) for _, _, s in spans]
            dps = [_dot(dov, v_ref[k0:k0 + kn, :], NT) for k0, kn, _ in spans]
            rowdot = sum(jnp.sum(dp * p, axis=-1, keepdims=True) for dp, p in zip(dps, probs))
            dq = 0.0
            for (k0, kn, _), p, dp in zip(spans, probs, dps):
                ds = p * (dp - rowdot)
                dq = dq + _dot(ds, k_ref[k0:k0 + kn, :])
                dk_ref[k0:k0 + kn, :] += _dot(ds, qs, TN)
                dv_ref[k0:k0 + kn, :] += _dot(p, dov, TN)
                dc_ref[:, k0:k0 + kn] -= jnp.sum(ds, axis=0, keepdims=True)
            dq_ref[rows, :] = (dq * scale).astype(dq_ref.dtype)

    hspec = pl.BlockSpec((None, t, dh), lambda a: (a, 0, 0))
    cspec = pl.BlockSpec((None, t, 1), lambda a: (a, 0, 0))
    rspec = pl.BlockSpec((None, 1, t), lambda a: (a, 0, 0))
    return _pcall(name, kern, (h,), (q, k, v, do, lse, cum_r), [hspec, hspec, hspec, hspec, cspec, rspec],
                  [jax.ShapeDtypeStruct((h, t, dh), BF16), jax.ShapeDtypeStruct((h, t, dh), F32),
                   jax.ShapeDtypeStruct((h, t, dh), F32), jax.ShapeDtypeStruct((h, 1, t), F32)],
                  [hspec, hspec, hspec, rspec], ("parallel",))


def _xattn_probs(q, k):
    s = _dot(q, k, NT) * (MEM_HEAD_DIM ** -0.5)
    p = jnp.exp(s - jnp.max(s, axis=-1, keepdims=True))
    return p / jnp.sum(p, axis=-1, keepdims=True)


def xattn_fwd(name, q, kv, tq=512):
    t = q.shape[0]
    tq = min(tq, t)
    ml = kv.shape[0]

    def kern(q_ref, k_ref, v_ref, o_ref):
        o_ref[...] = _dot(_xattn_probs(q_ref[...], k_ref[...]), v_ref[...]).astype(o_ref.dtype)

    qspec = pl.BlockSpec((tq, MEM_HEAD_DIM), lambda i, a: (i, a))
    return _pcall(name, kern, (t // tq, MEM_HEADS), (q, kv, kv),
                  [qspec, pl.BlockSpec((ml, MEM_HEAD_DIM), lambda i, a: (0, a)),
                   pl.BlockSpec((ml, MEM_HEAD_DIM), lambda i, a: (0, MEM_HEADS + a))],
                  jax.ShapeDtypeStruct(q.shape, BF16), qspec, ("parallel", "parallel"))


def xattn_bwd(name, q, kv, do, tq=512):
    t = q.shape[0]
    tq = min(tq, t)
    ml = kv.shape[0]
    scale = MEM_HEAD_DIM ** -0.5

    def kern(q_ref, k_ref, v_ref, do_ref, dq_ref, dk_ref, dv_ref):
        @pl.when(pl.program_id(1) == 0)
        def _():
            dk_ref[...] = jnp.zeros_like(dk_ref)
            dv_ref[...] = jnp.zeros_like(dv_ref)
        qv, kv_, dov = q_ref[...], k_ref[...], do_ref[...]
        p = _xattn_probs(qv, kv_)
        dp = _dot(dov, v_ref[...], NT)
        ds = p * (dp - jnp.sum(dp * p, axis=-1, keepdims=True)) * scale
        dq_ref[...] = _dot(ds, kv_).astype(dq_ref.dtype)
        dk_ref[...] += _dot(ds, qv, TN)
        dv_ref[...] += _dot(p, dov, TN)

    qspec = pl.BlockSpec((tq, MEM_HEAD_DIM), lambda a, i: (i, a))
    kspec = pl.BlockSpec((ml, MEM_HEAD_DIM), lambda a, i: (0, a))
    return _pcall(name, kern, (MEM_HEADS, t // tq), (q, kv, kv, do),
                  [qspec, kspec, pl.BlockSpec((ml, MEM_HEAD_DIM), lambda a, i: (0, MEM_HEADS + a)), qspec],
                  [jax.ShapeDtypeStruct(q.shape, BF16), jax.ShapeDtypeStruct((ml, D_MODEL), F32),
                   jax.ShapeDtypeStruct((ml, D_MODEL), F32)],
                  [qspec, kspec, kspec], ("parallel", "arbitrary"))


def _heads(x):
    t = x.shape[0]
    return x.reshape(t, FOX_HEADS, FOX_HEAD_DIM).transpose(1, 0, 2)


def _unheads(x):
    return x.transpose(1, 0, 2).reshape(x.shape[1], FOX_WIDTH)


def _row_cut(dw):
    return dw.reshape(N_CHIPS, 2, dw.shape[0] // (2 * N_CHIPS), dw.shape[1])


def local_step(x, mem, target, w, layer_weights=None, reduce_hook=None):
    depth = w["g_mix_pre"].shape[0]
    t = x.shape[0]
    saved = []
    i1, i2, i3 = 3 * FOX_WIDTH, 3 * FOX_WIDTH + FOX_HEADS, AB_IN
    ncol = 128

    def stacked_weights(layer, part, _):
        names = COMMON_BIG if part == "rest" else layer_big(layer)[len(COMMON_BIG):]
        return {n: w[n][layer if n in COMMON_BIG else layer // 2] for n in names}

    get_weights = layer_weights or stacked_weights
    h1 = rms_pre("l0_mix_pre", x, w["g_mix_pre"], 0)
    for layer in range(depth):
        lw = dict(get_weights(layer, "mix", x))
        s = {"x0": x, "lw": lw}
        tag = f"l{layer}"
        s["h1"] = h1
        if layer % 2 == 0:
            e = layer // 2
            w_in = jnp.pad(lw["ab_w_in"], ((0, 0), (0, AB_IN_PAD - AB_IN)))
            proj = mm(f"{tag}_ab_in", h1, w_in, "nn", F32)
            qkv = proj[:, :i1].astype(BF16).reshape(t, 3, FOX_HEADS, FOX_HEAD_DIM).transpose(1, 2, 0, 3)
            z_t = proj[:, i1:i2].T
            b_f = w["ab_b_f"][e].reshape(FOX_HEADS, 1)
            cum_t = fox_gates_fwd(f"{tag}_fox_gates", z_t, b_f)
            cum_r = cum_t[:, None, :]
            oh, lse = fox_fwd(f"{tag}_fox", qkv[0], qkv[1], qkv[2], cum_r)
            bcu = proj[:, i2:i3]
            y_b = sconv_fwd(f"{tag}_sconv", bcu, 0, w["ab_conv_w"][e])
            ycat = jnp.concatenate([_unheads(oh), y_b], axis=1)
            y1 = mm(f"{tag}_ab_out", ycat, lw["ab_w_out"], "nn", F32)
            s.update(w_in=w_in, qkv=qkv, z_t=z_t, b_f=b_f, cum_r=cum_r, lse=lse, bcu=bcu, ycat=ycat)
        else:
            o = layer // 2
            gu2 = mm(f"{tag}_c_in", h1, lw["c_w_in"], "nn", F32)
            conv_b = w["c_conv_b"][o].reshape(1, -1)
            uc = rg_conv_fwd(f"{tag}_rg_conv", gu2, w["c_conv_w"][o], conv_b)
            b_a, b_i = w["c_b_a"][o].reshape(1, -1), w["c_b_i"][o].reshape(1, -1)
            r, i = rg_gates_fwd(f"{tag}_rg_gates", uc, lw["c_w_a"], b_a, lw["c_w_i"], b_i)
            lam = w["c_lam"][o].reshape(1, -1)
            ymix, hs = rg_scan_fwd(f"{tag}_rg_scan", gu2, uc, r, i, lam)
            y1 = mm(f"{tag}_c_out", ymix, lw["c_w_out"], "nn", F32)
            s.update(gu2=gu2, uc=uc, r=r, i=i, lam=lam, hs=hs, ymix=ymix)
        s["y1"] = y1
        x, h2 = post_add_pre(f"{tag}_mix_post", x, y1, w["g_mix_post"], layer, w["g_cross_pre"], layer)
        lw.update(get_weights(layer, "rest", x))
        s["x1"] = x
        m = rms_pre(f"{tag}_mem_pre", mem, w["g_mem"], layer)
        q = mm(f"{tag}_xq", h2, lw["w_xq"], "nn", BF16)
        kv = mm(f"{tag}_xkv", m, lw["w_xkv"], "nn", BF16)
        o_att = xattn_fwd(f"{tag}_xattn", q, kv)
        y2 = mm(f"{tag}_xo", o_att, lw["w_xo"], "nn", F32)
        s.update(h2=h2, m=m, q=q, kv=kv, o_att=o_att, y2=y2)
        x, h3 = post_add_pre(f"{tag}_cross_post", x, y2, w["g_cross_post"], layer, w["g_ffn_pre"], layer)
        s["x2"] = x
        gu = mm(f"{tag}_ffn_gu", h3, lw["w_ffn_gu"], "nn", BF16)
        act = swiglu_fwd(f"{tag}_swiglu", gu)
        y3 = mm(f"{tag}_ffn_down", act, lw["w_ffn_down"], "nn", F32)
        s.update(h3=h3, gu=gu, act=act, y3=y3)
        if layer + 1 < depth:
            x, h1 = post_add_pre(f"{tag}_ffn_post", x, y3, w["g_ffn_post"], layer, w["g_mix_pre"], layer + 1)
        else:
            x = post_add(f"{tag}_ffn_post", x, y3, w["g_ffn_post"], layer)
        saved.append(s)

    dx, sq_cols = loss_head("loss_head", x, target)

    grads = {k: [None] * v.shape[0] for k, v in w.items() if k not in BIG}
    big = {}

    def dw(name, a, b, cols_cut=False):
        return mm(name, a, b, "tn", BF16, reduce_layout=True) if cols_cut else _row_cut(mm(name, a, b, "tn", BF16))

    def hook(layer, part, part_grads, after, gains):
        token = None if reduce_hook is None else reduce_hook(layer, part, part_grads, after)
        return gains if token is None else gains + token

    dy3, grads["g_ffn_post"][depth - 1] = post_bwd(f"b{depth - 1}_ffn_post", saved[-1]["y3"], dx, w["g_ffn_post"], depth - 1)
    for layer in reversed(range(depth)):
        s = saved[layer]
        lw = s["lw"]
        tag = f"b{layer}"
        lg = {}
        dact = mm(f"{tag}_ffn_down_dx", dy3, lw["w_ffn_down"], "nt", BF16)
        lg["w_ffn_down"] = dw(f"{tag}_ffn_down_dw", s["act"], dy3)
        dgu = swiglu_bwd(f"{tag}_swiglu", s["gu"], dact)
        dh3 = mm(f"{tag}_ffn_gu_dx", dgu, lw["w_ffn_gu"], "nt", F32)
        lg["w_ffn_gu"] = dw(f"{tag}_ffn_gu_dw", s["h3"], dgu, cols_cut=True)
        g_ffn_pre = hook(layer, "ffn", lg, dh3, w["g_ffn_pre"])
        ffn_grads, lg = lg, {}
        dx, dy2, grads["g_ffn_pre"][layer], grads["g_cross_post"][layer] = pre_post_bwd(
            f"{tag}_ffn_pre", s["x2"], dh3, dx, g_ffn_pre, layer, s["y2"], w["g_cross_post"], layer)
        do = mm(f"{tag}_xo_dx", dy2, lw["w_xo"], "nt", BF16)
        lg["w_xo"] = dw(f"{tag}_xo_dw", s["o_att"], dy2)
        dq, dk, dv = xattn_bwd(f"{tag}_xattn", s["q"], s["kv"], do)
        dh2 = mm(f"{tag}_xq_dx", dq, lw["w_xq"], "nt", F32)
        lg["w_xq"] = dw(f"{tag}_xq_dw", s["h2"], dq)
        dkv = jnp.concatenate([dk, dv], axis=1).astype(BF16)
        dm = mm(f"{tag}_xkv_dx", dkv, lw["w_xkv"], "nt", F32)
        lg["w_xkv"] = dw(f"{tag}_xkv_dw", s["m"], dkv, cols_cut=True)
        grads["g_mem"][layer] = gain_bwd(f"{tag}_mem_pre", mem, dm)
        g_cross_pre = hook(layer, "rest", lg, dh2, w["g_cross_pre"])
        dx, dy1, grads["g_cross_pre"][layer], grads["g_mix_post"][layer] = pre_post_bwd(
            f"{tag}_cross_pre", s["x1"], dh2, dx, g_cross_pre, layer, s["y1"], w["g_mix_post"], layer)
        rest_grads, lg = lg, {}
        if layer % 2 == 0:
            e = layer // 2
            dycat = mm(f"{tag}_ab_out_dx", dy1, lw["ab_w_out"], "nt", F32)
            g_mix_pre = hook(layer, "mixer", {}, dycat, w["g_mix_pre"])
            lg["ab_w_out"] = dw(f"{tag}_ab_out_dw", s["ycat"], dy1)
            do_h = _heads(dycat[:, :FOX_WIDTH].astype(BF16))
            qkv = s["qkv"]
            dqh, dkh, dvh, dcum = fox_bwd(f"{tag}_fox", qkv[0], qkv[1], qkv[2], do_h, s["lse"], s["cum_r"])
            dz_t, db_f = fox_gates_bwd(f"{tag}_fox_gates", s["z_t"], s["b_f"], dcum.reshape(FOX_HEADS, t))
            grads["ab_b_f"][e] = db_f.reshape(FOX_HEADS)
            db, dc, du, dconv_w = sconv_bwd(f"{tag}_sconv", s["bcu"], 0, w["ab_conv_w"][e], dycat, FOX_WIDTH // ncol)
            grads["ab_conv_w"][e] = dconv_w
            dproj = jnp.concatenate(
                [_unheads(dqh), _unheads(dkh).astype(BF16), _unheads(dvh).astype(BF16), dz_t.T.astype(BF16), db, dc, du,
                 jnp.zeros((t, AB_IN_PAD - AB_IN), BF16)], axis=1)
            dh1 = mm(f"{tag}_ab_in_dx", dproj, s["w_in"], "nt", F32)
            dw_in = mm(f"{tag}_ab_in_dw", s["h1"], dproj, "tn", F32)[:, :AB_IN]
            lg["ab_w_in"] = dw_in.reshape(2, D_MODEL // 2, N_CHIPS, AB_IN // N_CHIPS).transpose(2, 0, 1, 3).astype(BF16)
        else:
            o = layer // 2
            dymix = mm(f"{tag}_c_out_dx", dy1, lw["c_w_out"], "nt", F32)
            g_mix_pre = hook(layer, "mixer", {}, dymix, w["g_mix_pre"])
            lg["c_w_out"] = dw(f"{tag}_c_out_dw", s["ymix"], dy1)
            dgate, dzr, dzi, duc_part, db_a, db_i, dlam = rg_scan_bwd(
                f"{tag}_rg_scan", s["gu2"], s["uc"], s["r"], s["i"], s["hs"], dymix, s["lam"])
            duc, dw_a, dw_i = rg_gates_bwd(f"{tag}_rg_gates", s["uc"], dzr, dzi, duc_part, lw["c_w_a"], lw["c_w_i"])
            lg["c_w_a"] = dw_a.reshape(N_CHIPS, 2, LRU_BW // 2, LRU_BW)
            lg["c_w_i"] = dw_i.reshape(N_CHIPS, 2, LRU_BW // 2, LRU_BW)
            du_raw, dconv_w, dconv_b = rg_conv_bwd(f"{tag}_rg_conv", s["gu2"], duc, w["c_conv_w"][o])
            grads["c_b_a"][o] = db_a.reshape(LRU_BLOCKS, LRU_BW)
            grads["c_b_i"][o] = db_i.reshape(LRU_BLOCKS, LRU_BW)
            grads["c_lam"][o] = dlam.reshape(-1)
            grads["c_conv_w"][o] = dconv_w
            grads["c_conv_b"][o] = dconv_b.reshape(-1)
            dgu2 = jnp.concatenate([dgate, du_raw], axis=1)
            dh1 = mm(f"{tag}_c_in_dx", dgu2, lw["c_w_in"], "nt", F32)
            lg["c_w_in"] = dw(f"{tag}_c_in_dw", s["h1"], dgu2, cols_cut=True)
        if reduce_hook is None:
            big[layer] = {**ffn_grads, **rest_grads, **lg}
        g_mix_pre = hook(layer, "mix", lg, dh1, g_mix_pre)
        if layer > 0:
            dx, dy3, grads["g_mix_pre"][layer], grads["g_ffn_post"][layer - 1] = pre_post_bwd(
                f"{tag}_mix_pre", s["x0"], dh1, dx, g_mix_pre, layer, saved[layer - 1]["y3"], w["g_ffn_post"], layer - 1)
        else:
            dx, grads["g_mix_pre"][layer] = pre_bwd(f"{tag}_mix_pre", s["x0"], dh1, dx, g_mix_pre, layer)

    for k in list(grads):
        if k.startswith("g_"):
            grads[k] = [g.reshape(-1) for g in grads[k]]
        grads[k] = jnp.stack(grads[k])
    return sq_cols, dx, grads, big


CHIP_FLIPS = ((1, 0), (0, 1), (1, 1))
HBM_SPEC = pl.BlockSpec(memory_space=pltpu.HBM)
VMEM_SPEC = pl.BlockSpec(memory_space=pltpu.VMEM)


def _place():
    return lax.axis_index("x"), lax.axis_index("y"), lax.axis_index("c")


def _flip(v, f):
    return 1 - v if f else v


def _remote(src, dst, send_sem, recv_sem, target):
    return pltpu.make_async_remote_copy(src_ref=src, dst_ref=dst, send_sem=send_sem, recv_sem=recv_sem,
                                        device_id=target, device_id_type=MESH)


SEM_SPEC = pl.BlockSpec(memory_space=pltpu.SEMAPHORE)


def _swap_copies(srcs, lands, send_sems, recv_sems):
    x, y, c = _place()
    return [_remote(src.at[:, 1 - c], land, send_sems.at[len(CHIP_FLIPS) * a], recv_sems.at[len(CHIP_FLIPS) * a],
                    (x, y, 1 - c)) for a, (src, land) in enumerate(zip(srcs, lands))]


def _exchange_copies(srcs, lands, send_sems, recv_sems):
    x, y, c = _place()
    p = 2 * x + y
    cps = []
    for a, (src, land) in enumerate(zip(srcs, lands)):
        for k, (fx, fy) in enumerate(CHIP_FLIPS):
            qx, qy = _flip(x, fx), _flip(y, fy)
            sem = len(CHIP_FLIPS) * a + k
            cps.append(_remote(src.at[2 * qx + qy], land.at[p], send_sems.at[sem], recv_sems.at[sem], (qx, qy, c)))
    return cps


def _gather_copies(srcs, lands, send_sems, recv_sems):
    x, y, c = _place()
    p = 2 * x + y
    cps = []
    for a, (src, land) in enumerate(zip(srcs, lands)):
        for k, (fx, fy) in enumerate(CHIP_FLIPS):
            sem = len(CHIP_FLIPS) * a + k
            cps.append(_remote(src.at[c], land.at[p, c], send_sems.at[sem], recv_sems.at[sem],
                               (_flip(x, fx), _flip(y, fy), c)))
    return cps


def copies_start(name, make_copies, srcs, land_shapes):
    n = len(srcs)

    def body(*refs):
        for cp in make_copies(refs[:n], refs[n:2 * n], refs[2 * n], refs[2 * n + 1]):
            cp.start()
        refs[-1][...] = jnp.zeros_like(refs[-1])

    thru = [pltpu.HBM(b.shape, b.dtype) for b in srcs] + [pltpu.HBM(sh, b.dtype) for sh, b in zip(land_shapes, srcs)]
    outs = pl.pallas_call(
        body, name=name, in_specs=[HBM_SPEC] * (2 * n),
        out_shape=(pltpu.SemaphoreType.DMA((3 * n,)), pltpu.SemaphoreType.DMA((3 * n,)), *thru,
                   jax.ShapeDtypeStruct((8, 128), F32)),
        out_specs=(SEM_SPEC, SEM_SPEC, *[HBM_SPEC] * (2 * n), VMEM_SPEC),
        input_output_aliases={i: 2 + i for i in range(2 * n)},
        compiler_params=pltpu.CompilerParams(has_side_effects=pltpu.SideEffectType.DATAFLOW_SIDE_EFFECTING),
    )(*[pltpu.with_memory_space_constraint(b, pltpu.HBM) for b in srcs],
      *[pltpu.with_memory_space_constraint(lax.empty(sh, b.dtype), pltpu.HBM) for sh, b in zip(land_shapes, srcs)])
    return outs[:-1], outs[-1]


def copies_wait(name, make_copies, state, after):
    send_sems, recv_sems, *thru = state
    n = len(thru) // 2

    def body(*refs):
        for cp in make_copies(refs[:n], refs[n:2 * n], refs[2 * n], refs[2 * n + 1]):
            cp.wait_send()
            cp.wait_recv()

    outs = pl.pallas_call(
        body, name=name, in_specs=[HBM_SPEC] * (2 * n) + [SEM_SPEC, SEM_SPEC, pl.BlockSpec(memory_space=pl.ANY)],
        out_shape=tuple(pltpu.HBM(t.shape, t.dtype) for t in thru), out_specs=tuple([HBM_SPEC] * (2 * n)),
        input_output_aliases={i: i for i in range(2 * n)},
        compiler_params=pltpu.CompilerParams(has_side_effects=pltpu.SideEffectType.DATAFLOW_SIDE_EFFECTING),
    )(*thru, send_sems, recv_sems, after)
    return outs[:n], outs[n:]


def pass_to_sibling(name, shards, lands):
    n = len(lands)

    def body(*refs):
        own, ins, outs = refs[:n], refs[n:2 * n], refs[2 * n:3 * n]
        send_sems, recv_sems = refs[3 * n:]
        x, y, c = _place()
        sibling = (x, y, 1 - c)
        cps = []
        for a in range(n):
            for k, (fx, fy) in enumerate(CHIP_FLIPS):
                q = 2 * _flip(x, fx) + _flip(y, fy)
                cps.append(_remote(ins[a].at[q, c], outs[a].at[q, c], send_sems.at[a, k], recv_sems.at[a, k], sibling))
            cps.append(_remote(own[a], outs[a].at[2 * x + y], send_sems.at[a, 3], recv_sems.at[a, 3], sibling))
        for cp in cps:
            cp.start()
        for cp in cps:
            cp.wait()

    return pl.pallas_call(
        body, name=name, in_specs=[HBM_SPEC] * (2 * n), out_specs=[HBM_SPEC] * n,
        out_shape=[jax.ShapeDtypeStruct(b.shape, b.dtype) for b in lands],
        scratch_shapes=[pltpu.SemaphoreType.DMA((n, 4)), pltpu.SemaphoreType.DMA((n, 4))],
        input_output_aliases={n + i: i for i in range(n)},
    )(*shards, *lands)


def share_halves(bufs):
    n = len(bufs)

    def body(*refs):
        ins, outs, token = refs[:n], refs[n:2 * n], refs[2 * n]
        send_sems, recv_sems = refs[2 * n + 1:]
        x, y, c = _place()
        cps = [_remote(ins[a].at[:, c], outs[a].at[:, c], send_sems.at[a], recv_sems.at[a], (x, y, 1 - c))
               for a in range(n)]
        for cp in cps:
            cp.start()
        token[...] = jnp.zeros_like(token)
        for cp in cps:
            cp.wait()

    outs = pl.pallas_call(
        body, name="share_reduced_halves", in_specs=[HBM_SPEC] * n, out_specs=[HBM_SPEC] * n + [VMEM_SPEC],
        out_shape=[jax.ShapeDtypeStruct(b.shape, b.dtype) for b in bufs] + [jax.ShapeDtypeStruct((8, 128), F32)],
        scratch_shapes=[pltpu.SemaphoreType.DMA((n,)), pltpu.SemaphoreType.DMA((n,))],
        input_output_aliases={i: i for i in range(n)},
    )(*bufs)
    return outs[:n], outs[n]


DEVICE_FLIPS = tuple((fx, fy, fc) for fx in (0, 1) for fy in (0, 1) for fc in (0, 1))[1:]


def gather_small(name, v, reduce):
    r, cdim = v.shape
    n_dev = 8

    def body(v_ref, out_ref, *scratch):
        buf = scratch[0] if reduce else out_ref
        send_sems, recv_sems = scratch[-2:]
        x, y, c = _place()
        me = 4 * x + 2 * y + c
        buf[me] = v_ref[...]
        cps = []
        for k, (fx, fy, fc) in enumerate(DEVICE_FLIPS):
            cps.append(_remote(v_ref, buf.at[me], send_sems.at[k], recv_sems.at[k],
                               (_flip(x, fx), _flip(y, fy), _flip(c, fc))))
        for cp in cps:
            cp.start()
        for cp in cps:
            cp.wait()
        if reduce:
            total = buf[0]
            for d in range(1, n_dev):
                total = total + buf[d]
            out_ref[...] = total

    scratch = [pltpu.SemaphoreType.DMA((7,)), pltpu.SemaphoreType.DMA((7,))]
    if reduce:
        scratch = [pltpu.VMEM((n_dev, r, cdim), F32)] + scratch
    out_shape = jax.ShapeDtypeStruct((r, cdim) if reduce else (n_dev, r, cdim), F32)
    return pl.pallas_call(body, name=name, in_specs=[VMEM_SPEC], out_specs=VMEM_SPEC, out_shape=out_shape,
                          scratch_shapes=scratch)(v)


def pair_sum(name, own, got, core):
    _, hx, cols = got.shape
    tr = _tile(hx, (256, 128, 64, 32, 16))

    def kern(core_ref, a_ref, b_ref, o_ref):
        o_ref[...] = (a_ref[...].astype(F32) + b_ref[...].astype(F32)).astype(BF16)

    grid_spec = pltpu.PrefetchScalarGridSpec(
        num_scalar_prefetch=1, grid=(hx // tr,),
        in_specs=[pl.BlockSpec((N_CHIPS, None, tr, cols), lambda i, cr: (0, cr[0], i, 0)),
                  pl.BlockSpec((N_CHIPS, tr, cols), lambda i, cr: (0, i, 0))],
        out_specs=pl.BlockSpec((N_CHIPS, tr, cols), lambda i, cr: (0, i, 0)))
    return pl.pallas_call(
        kern, name=name, grid_spec=grid_spec, out_shape=jax.ShapeDtypeStruct(got.shape, BF16),
        compiler_params=pltpu.CompilerParams(dimension_semantics=("parallel",), vmem_limit_bytes=VMEM_LIMIT_BYTES),
    )(core, own, got)


def chip_sum(name, mine, parts, place, buf, layer):
    _, hx, yd = parts.shape
    tr = _tile(hx, (256, 128, 64, 32, 16))

    def kern(place_ref, m_ref, p_ref, _, o_ref):
        total = None
        for q in range(N_CHIPS):
            term = jnp.where(place_ref[0] == q, m_ref[...], p_ref[q]).astype(F32)
            total = term if total is None else total + term
        o_ref[...] = total

    grid_spec = pltpu.PrefetchScalarGridSpec(
        num_scalar_prefetch=1, grid=(hx // tr,),
        in_specs=[pl.BlockSpec((None, tr, yd), lambda i, pr: (pr[0], i, 0)),
                  pl.BlockSpec((N_CHIPS, tr, yd), lambda i, pr: (0, i, 0)),
                  pl.BlockSpec(memory_space=pl.ANY)],
        out_specs=pl.BlockSpec((None, None, tr, yd), lambda i, pr: (layer, pr[1], i, 0)))
    return pl.pallas_call(
        kern, name=name, grid_spec=grid_spec, out_shape=jax.ShapeDtypeStruct(buf.shape, buf.dtype),
        input_output_aliases={3: 0},
        compiler_params=pltpu.CompilerParams(dimension_semantics=("parallel",), vmem_limit_bytes=VMEM_LIMIT_BYTES),
    )(place, mine, parts, buf)


WEIGHTS = ("g_mix_pre", "g_mix_post", "g_cross_pre", "g_mem", "g_cross_post", "g_ffn_pre", "g_ffn_post", "w_xq", "w_xkv",
           "w_xo", "w_ffn_gu", "w_ffn_down", "ab_w_in", "ab_b_f", "ab_conv_w", "ab_w_out", "c_w_in", "c_conv_w",
           "c_conv_b", "c_w_a", "c_b_a", "c_w_i", "c_b_i", "c_lam", "c_w_out")
SHARD_DIM = {"w_xq": 1, "w_xkv": 2, "w_xo": 1, "w_ffn_gu": 2, "w_ffn_down": 1, "ab_w_in": 2, "ab_conv_w": 2,
             "ab_w_out": 1, "c_w_in": 2, "c_conv_w": 2, "c_conv_b": 1, "c_w_a": 2, "c_b_a": 2, "c_w_i": 2, "c_b_i": 2,
             "c_lam": 1, "c_w_out": 1}
COMMON_BIG = ("w_xq", "w_xkv", "w_xo", "w_ffn_gu", "w_ffn_down")
EVEN_BIG, ODD_BIG = ("ab_w_in", "ab_w_out"), ("c_w_in", "c_w_a", "c_w_i", "c_w_out")
BIG = COMMON_BIG + EVEN_BIG + ODD_BIG


def layer_big(layer):
    return COMMON_BIG + (ODD_BIG if layer % 2 else EVEN_BIG)


SPLIT_LAYERS = (0,)


FINE_REDUCE_LAYERS = (0, 1)


def reduce_chunk(layer, part):
    if layer not in FINE_REDUCE_LAYERS:
        return layer_big(layer) if part == "mix" else ()
    return {"ffn": ("w_ffn_gu", "w_ffn_down"), "rest": ("w_xq", "w_xkv", "w_xo"), "mixer": (),
            "mix": layer_big(layer)[len(COMMON_BIG):]}[part]


def chunk_names(layer, part):
    mixer = layer_big(layer)[len(COMMON_BIG):]
    if layer in SPLIT_LAYERS:
        return mixer if part == "mix" else COMMON_BIG
    return layer_big(layer) if part == "mix" else ()


SMALL_SHARDED = ("ab_conv_w", "c_conv_w", "c_conv_b", "c_b_a", "c_b_i", "c_lam")
REPLICATED = ("g_mix_pre", "g_mix_post", "g_cross_pre", "g_mem", "g_cross_post", "g_ffn_pre", "g_ffn_post", "ab_b_f")
PACK_COLS = 1024


def _unshard(g, d):
    shard = g.shape[1:]
    return jnp.moveaxis(g, 0, d).reshape(shard[:d] + (N_CHIPS * shard[d],) + shard[d + 1:])


def _shardify(full, d):
    s = full.shape
    return jnp.moveaxis(full.reshape(s[:d] + (N_CHIPS, s[d] // N_CHIPS) + s[d + 1:]), d, 0)


def _pack(arrays, rows):
    flat = jnp.concatenate([a.reshape(-1).astype(F32) for a in arrays])
    return jnp.pad(flat, (0, rows * PACK_COLS - flat.shape[0])).reshape(rows, PACK_COLS)


def _unpack(packed, shapes):
    flat = packed.reshape(-1)
    out, at = [], 0
    for s in shapes:
        size = math.prod(s)
        out.append(flat[at:at + size].reshape(s))
        at += size
    return out


def _rows_for(shapes):
    return -(-sum(math.prod(s) for s in shapes) // (8 * PACK_COLS)) * 8


def kernel(x, mem, g_mix_pre, g_mix_post, g_cross_pre, g_mem, g_cross_post, g_ffn_pre, g_ffn_post, w_xq, w_xkv, w_xo, w_ffn_gu, w_ffn_down, ab_w_in, ab_b_f, ab_conv_w, ab_w_out, c_w_in, c_conv_w, c_conv_b, c_w_a, c_b_a, c_w_i, c_b_i, c_lam, c_w_out, loss_target, m_g_mix_pre, m_g_mix_post, m_g_cross_pre, m_g_mem, m_g_cross_post, m_g_ffn_pre, m_g_ffn_post, m_w_xq, m_w_xkv, m_w_xo, m_w_ffn_gu, m_w_ffn_down, m_ab_w_in, m_ab_b_f, m_ab_conv_w, m_ab_w_out, m_c_w_in, m_c_conv_w, m_c_conv_b, m_c_w_a, m_c_b_a, m_c_w_i, m_c_b_i, m_c_lam, m_c_w_out, v_g_mix_pre, v_g_mix_post, v_g_cross_pre, v_g_mem, v_g_cross_post, v_g_ffn_pre, v_g_ffn_post, v_w_xq, v_w_xkv, v_w_xo, v_w_ffn_gu, v_w_ffn_down, v_ab_w_in, v_ab_b_f, v_ab_conv_w, v_ab_w_out, v_c_w_in, v_c_conv_w, v_c_conv_b, v_c_w_a, v_c_b_a, v_c_w_i, v_c_b_i, v_c_lam, v_c_w_out):
    given = dict(locals())
    w = {n: given[n] for n in WEIGHTS}
    m_in = {n: given["m_" + n] for n in WEIGHTS}
    v_in = {n: given["v_" + n] for n in WEIGHTS}
    xi, yi, ci = _place()
    chip = 2 * xi + yi

    full = {}
    small_shapes = [w[n].shape for n in SMALL_SHARDED]
    rows_w = _rows_for(small_shapes)
    assert rows_w * PACK_COLS > sum(math.prod(s) for s in small_shapes)
    every = gather_small("gather_small_weights", _pack([w[n] for n in SMALL_SHARDED], rows_w), reduce=False)
    per_chip = every[0::2].reshape(N_CHIPS, -1)
    at = 0
    for n, s in zip(SMALL_SHARDED, small_shapes):
        size = math.prod(s)
        full[n] = _unshard(per_chip[:, at:at + size].reshape(N_CHIPS, *s), SHARD_DIM[n])
        at += size
    for n in REPLICATED:
        full[n] = w[n]
    after_small = every[0, -1, -1].astype(BF16)

    depth = g_mix_pre.shape[0]
    own, gathers, tokens = {}, {}, []
    for layer in range(depth):
        for part in ("mix", "rest"):
            names = chunk_names(layer, part)
            if names:
                tagp = f"l{layer}_{part}"
                own[tagp] = {n: w[n][layer if n in COMMON_BIG else layer // 2].astype(BF16) + after_small for n in names}
                halves = [a.reshape(2, -1, a.shape[-1]) for a in own[tagp].values()]
                gathers[tagp], token = copies_start(f"gather_start_{tagp}", _gather_copies, halves,
                                                    [(N_CHIPS, *h.shape) for h in halves])
                tokens.append(token[0, 0])

    def layer_weights(layer, part, x_in):
        tagp = f"l{layer}_{part}"
        if tagp not in gathers:
            return {}
        shards, lands = copies_wait(f"gather_wait_{tagp}", _gather_copies, gathers[tagp], x_in)
        lands = pass_to_sibling(f"gather_pass_{tagp}", shards, lands)
        return {n: _unshard(g.reshape(N_CHIPS, *mine.shape), SHARD_DIM[n] - 1)
                for (n, mine), g in zip(own[tagp].items(), lands)}

    core_arr = ci.reshape(1).astype(jnp.int32)
    place_arr = jnp.stack([chip, ci]).astype(jnp.int32)
    in_flight, swapping, held = [], [], {}

    def exchange(after):
        layer, tagp, names, state = swapping.pop()
        mine, got = copies_wait(f"swap_wait_{tagp}", _swap_copies, state, after)
        sums = [pair_sum(f"pair_sum_{tagp}_{n}", o, g, core_arr) for n, o, g in zip(names, mine, got)]
        state, token = copies_start(f"exchange_start_{tagp}", _exchange_copies, sums, [b.shape for b in sums])
        in_flight.append((layer, tagp, names, state))
        return token

    def reduce_hook(layer, part, part_grads, after):
        token = exchange(after)[0, 0] if swapping else None
        held.update(part_grads)
        names = reduce_chunk(layer, part)
        if names:
            tagp = f"l{layer}_{part}"
            mine = [held.pop(n) for n in names]
            state, started = copies_start(f"swap_start_{tagp}", _swap_copies, mine,
                                          [(m.shape[0], *m.shape[2:]) for m in mine])
            swapping.append((layer, tagp, names, state))
            token = started[0, 0] if token is None else token + started[0, 0]
        return token

    sq_cols, dx, grads, _ = local_step(x[0] + sum(tokens), mem[0], loss_target[0], full, layer_weights, reduce_hook)
    exchange(dx)
    loss = lax.psum(0.5 / D_MODEL * jnp.sum(sq_cols), ("x", "y", "c"))

    reduced = {n: lax.empty((w[n].shape[0], 2, math.prod(w[n].shape[1:-1]) // 2, w[n].shape[-1]), F32) for n in BIG}
    for layer, tagp, names, state in in_flight:
        sums, parts = copies_wait(f"exchange_wait_{tagp}", _exchange_copies, state, dx)
        for n, mine, p in zip(names, sums, parts):
            index = layer if n in COMMON_BIG else layer // 2
            reduced[n] = chip_sum(f"chip_sum_{tagp}_{n}", mine, p, place_arr, reduced[n], index)
    shared, after_big = share_halves([reduced[n] for n in BIG])
    grad_out = {n: g.reshape(w[n].shape) for n, g in zip(BIG, shared)}

    small_names = REPLICATED + SMALL_SHARDED
    small_full_shapes = [grads[n].shape for n in small_names]
    total = gather_small("reduce_small_grads",
                         _pack([grads[n] for n in small_names], _rows_for(small_full_shapes)) + after_big[0, 0],
                         reduce=True)
    for n, g in zip(small_names, _unpack(total, small_full_shapes)):
        if n in SHARD_DIM:
            g = lax.dynamic_index_in_dim(_shardify(g, SHARD_DIM[n]), chip, axis=0, keepdims=False)
        grad_out[n] = g

    delta, new_m, new_v = {}, {}, {}
    for n in BIG:
        two_d = lambda a: a.reshape(-1, a.shape[-1])
        results = adamw(f"adamw_{n}", two_d(w[n]), two_d(grad_out[n]), two_d(m_in[n]), two_d(v_in[n]))
        delta[n], new_m[n], new_v[n], grad_out[n] = (a.reshape(w[n].shape) for a in results)
    shapes = [w[n].shape for n in small_names]
    rows = _rows_for(shapes)
    packed = [_pack([src[n] for n in small_names], rows) for src in (w, grad_out, m_in, v_in)]
    for dst, res in zip((delta, new_m, new_v), adamw("adamw_small", *packed)[:3]):
        for n, a in zip(small_names, _unpack(res, shapes)):
            dst[n] = a

    return (loss, dx[None], *[grad_out[n] for n in WEIGHTS], *[delta[n] for n in WEIGHTS],
            *[new_m[n] for n in WEIGHTS], *[new_v[n] for n in WEIGHTS])
```

```python
import functools
import math

import jax
import jax.numpy as jnp
from jax import lax
from jax.experimental import pallas as pl
from jax.experimental.pallas import tpu as pltpu

F32, BF16 = jnp.float32, jnp.bfloat16
D_MODEL = 1024
EPS = 1e-6
NEG_INF = -1e30
FOX_HEADS, FOX_HEAD_DIM, FOX_WIDTH = 8, 64, 512
SC_WIDTH = 512
AB_IN = 3 * FOX_WIDTH + FOX_HEADS + 3 * SC_WIDTH
AB_IN_PAD = 3200
LRU_BW, LRU_BLOCKS = 256, 4
RG_C = 8.0
MEM_HEADS, MEM_HEAD_DIM = 4, 256
ADAM_LR, ADAM_B1, ADAM_B2, ADAM_EPS, ADAM_WD, ADAM_STEP = 0.001, 0.9, 0.999, 1e-08, 0.01, 10
N_CHIPS = 4
MESH = pl.DeviceIdType.MESH
VMEM_LIMIT_BYTES = 48 * 1024 * 1024
MM_OPERAND_TILE_BYTES = 7 * 1024 * 1024

NN = (((1,), (0,)), ((), ()))
NT = (((1,), (1,)), ((), ()))
TN = (((0,), (0,)), ((), ()))


def _dot(a, b, dn=NN):
    return lax.dot_general(a.astype(BF16), b.astype(BF16), dn, preferred_element_type=F32)


def _tile(n, prefs):
    for p in prefs:
        if n % p == 0:
            return p
    return n


def _pcall(name, kern, grid, ins, in_specs, out_shape, out_specs, sem):
    return pl.pallas_call(
        kern, name=name, grid=grid, in_specs=in_specs, out_specs=out_specs, out_shape=out_shape,
        compiler_params=pltpu.CompilerParams(dimension_semantics=sem, vmem_limit_bytes=VMEM_LIMIT_BYTES),
    )(*ins)


def mm(name, a, b, mode, out_dtype, reduce_layout=False):
    if mode == "nn":
        (m, k), n = a.shape, b.shape[1]
    elif mode == "nt":
        (m, k), n = a.shape, b.shape[0]
    else:
        (k, m), n = a.shape, b.shape[1]
    if reduce_layout:
        tm, tn = m // 2, n // N_CHIPS
    else:
        tn = _tile(n, ((1024,) if mode == "tn" else ()) + (512, 640, 256, 128))
        tm = next(c for c in (2048, 1024, 512, 256, 128, m)
                  if m % c == 0 and 2 * c * k <= MM_OPERAND_TILE_BYTES and 4 * c * tn <= MM_OPERAND_TILE_BYTES)
    dn = {"nn": NN, "nt": NT, "tn": TN}[mode]

    def kern(a_ref, b_ref, o_ref):
        o_ref[...] = _dot(a_ref[...], b_ref[...], dn).astype(o_ref.dtype)

    a_spec = pl.BlockSpec((k, tm), lambda i, j: (0, i)) if mode == "tn" else pl.BlockSpec((tm, k), lambda i, j: (i, 0))
    b_spec = pl.BlockSpec((tn, k), lambda i, j: (j, 0)) if mode == "nt" else pl.BlockSpec((k, tn), lambda i, j: (0, j))
    if reduce_layout:
        out_shape = jax.ShapeDtypeStruct((N_CHIPS, 2, tm, tn), out_dtype)
        o_spec = pl.BlockSpec((None, None, tm, tn), lambda i, j: (j, i, 0, 0))
    else:
        out_shape = jax.ShapeDtypeStruct((m, n), out_dtype)
        o_spec = pl.BlockSpec((tm, tn), lambda i, j: (i, j))
    return _pcall(name, kern, (m // tm, n // tn), (a, b), [a_spec, b_spec], out_shape, o_spec, ("parallel", "parallel"))


def rowwise(name, body, rows, params, outs, accs=(), tr=256):
    t = rows[0].shape[0]
    tr = min(tr, t)
    nr, npar, no = len(rows), len(params), len(outs)

    def kern(*refs):
        acc_refs = refs[nr + npar + no:]
        if acc_refs:
            @pl.when(pl.program_id(0) == 0)
            def _():
                for ar in acc_refs:
                    ar[...] = jnp.zeros_like(ar)
        body(refs[:nr], refs[nr:nr + npar], refs[nr + npar:nr + npar + no], acc_refs)

    in_specs = [pl.BlockSpec((tr, x.shape[1]), lambda i: (i, 0)) for x in rows]
    in_specs += [pl.BlockSpec(p.shape, lambda i: (0, 0)) for p in params]
    out_specs = [pl.BlockSpec((tr, c), lambda i: (i, 0)) for c, _ in outs]
    out_specs += [pl.BlockSpec(s, lambda i: (0, 0)) for s in accs]
    out_shape = [jax.ShapeDtypeStruct((t, c), dt) for c, dt in outs]
    out_shape += [jax.ShapeDtypeStruct(s, F32) for s in accs]
    return _pcall(name, kern, (t // tr,), (*rows, *params), in_specs, out_shape, out_specs,
                  ("arbitrary",) if accs else ("parallel",))


def _rms_stats(x):
    x = x.astype(F32)
    r = lax.rsqrt(jnp.mean(x * x, axis=-1, keepdims=True) + EPS)
    return r, x * r


def _rms_bwd(xh, r, g, dy):
    dxh = dy * g
    dx = r * (dxh - xh * jnp.mean(dxh * xh, axis=-1, keepdims=True))
    return dx, jnp.sum(dy * xh, axis=0, keepdims=True)


def rms_pre(name, x, gains, layer):
    def body(r, p, o, a):
        _, xh = _rms_stats(r[0][...])
        o[0][...] = (xh * p[0][layer:layer + 1, :]).astype(BF16)
    return rowwise(name, body, [x], [gains], [(x.shape[1], BF16)])[0]


def post_add(name, x, y, gains, layer):
    def body(r, p, o, a):
        _, yh = _rms_stats(r[1][...])
        o[0][...] = r[0][...] + yh * p[0][layer:layer + 1, :]
    return rowwise(name, body, [x, y], [gains], [(x.shape[1], F32)])[0]


def post_bwd(name, y, dx, gains, layer):
    def body(r, p, o, a):
        rr, yh = _rms_stats(r[0][...])
        dy, dg = _rms_bwd(yh, rr, p[0][layer:layer + 1, :], r[1][...])
        o[0][...] = dy.astype(BF16)
        a[0][...] += dg
    c = y.shape[1]
    return rowwise(name, body, [y, dx], [gains], [(c, BF16)], [(1, c)])


def pre_bwd(name, x, dh, dx_res, gains, layer):
    def body(r, p, o, a):
        rr, xh = _rms_stats(r[0][...])
        dx, dg = _rms_bwd(xh, rr, p[0][layer:layer + 1, :], r[1][...])
        o[0][...] = r[2][...] + dx
        a[0][...] += dg
    c = x.shape[1]
    return rowwise(name, body, [x, dh, dx_res], [gains], [(c, F32)], [(1, c)])


def post_add_pre(name, x, y, gains_post, layer_post, gains_pre, layer_pre):
    def body(r, p, o, a):
        _, yh = _rms_stats(r[1][...])
        x_new = r[0][...] + yh * p[0][layer_post:layer_post + 1, :]
        o[0][...] = x_new
        _, xh = _rms_stats(x_new)
        o[1][...] = (xh * p[1][layer_pre:layer_pre + 1, :]).astype(BF16)
    c = x.shape[1]
    return rowwise(name, body, [x, y], [gains_post, gains_pre], [(c, F32), (c, BF16)])


def pre_post_bwd(name, x, dh, dx_res, gains_pre, layer_pre, y, gains_post, layer_post):
    def body(r, p, o, a):
        rr, xh = _rms_stats(r[0][...])
        dx_norm, dg_pre = _rms_bwd(xh, rr, p[0][layer_pre:layer_pre + 1, :], r[1][...])
        dx = r[2][...] + dx_norm
        o[0][...] = dx
        a[0][...] += dg_pre
        ry, yh = _rms_stats(r[3][...])
        dy, dg_post = _rms_bwd(yh, ry, p[1][layer_post:layer_post + 1, :], dx)
        o[1][...] = dy.astype(BF16)
        a[1][...] += dg_post
    c = x.shape[1]
    return rowwise(name, body, [x, dh, dx_res, y], [gains_pre, gains_post], [(c, F32), (c, BF16)], [(1, c), (1, c)])


def gain_bwd(name, x, dh):
    def body(r, p, o, a):
        _, xh = _rms_stats(r[0][...])
        a[0][...] += jnp.sum(r[1][...] * xh, axis=0, keepdims=True)
    return rowwise(name, body, [x, dh], [], [], [(1, x.shape[1])])[0]


def _sigmoid(z):
    return 1.0 / (1.0 + jnp.exp(-z))


def swiglu_fwd(name, gu):
    f = gu.shape[1] // 2

    def body(r, p, o, a):
        g = r[0][:, :f].astype(F32)
        u = r[0][:, f:].astype(F32)
        o[0][...] = (g * _sigmoid(g) * u).astype(BF16)
    return rowwise(name, body, [gu], [], [(f, BF16)])[0]


def swiglu_bwd(name, gu, da):
    f = gu.shape[1] // 2

    def body(r, p, o, a):
        g = r[0][:, :f].astype(F32)
        u = r[0][:, f:].astype(F32)
        d = r[1][...].astype(F32)
        sg = _sigmoid(g)
        o[0][:, :f] = (d * u * sg * (1.0 + g * (1.0 - sg))).astype(BF16)
        o[0][:, f:] = (d * g * sg).astype(BF16)
    return rowwise(name, body, [gu, da], [], [(2 * f, BF16)])[0]


def loss_head(name, y, target):
    c = y.shape[1]

    def body(r, p, o, a):
        e = r[0][...] - r[1][...]
        o[0][...] = e * (1.0 / c)
        a[0][...] += jnp.sum(e * e, axis=0, keepdims=True)
    return rowwise(name, body, [y, target], [], [(c, F32)], [(1, c)])


def adamw(name, w, g, m, v):
    c = w.shape[1]

    def body(r, p, o, a):
        wv, gv, mv, vv = (x[...] for x in r)
        m2 = ADAM_B1 * mv + (1.0 - ADAM_B1) * gv
        v2 = ADAM_B2 * vv + (1.0 - ADAM_B2) * (gv * gv)
        m_hat = m2 / (1.0 - ADAM_B1 ** ADAM_STEP)
        v_hat = v2 / (1.0 - ADAM_B2 ** ADAM_STEP)
        o[0][...] = -ADAM_LR * (m_hat / (jnp.sqrt(v_hat) + ADAM_EPS) + ADAM_WD * wv)
        o[1][...] = m2
        o[2][...] = v2
        o[3][...] = gv
    tr = _tile(w.shape[0], (256, 128, 64, 32, 16, 8))
    return rowwise(name, body, [w, g, m, v], [], [(c, F32)] * 4, tr=tr)


def colwise(name, body, cols, params, outs, pouts=(), tc=128):
    t = cols[0][0].shape[0]
    c = params[0].shape[1] if params else cols[0][0].shape[1]
    nc, npar, no = len(cols), len(params), len(outs)

    def kern(*refs):
        body(refs[:nc], refs[nc:nc + npar], refs[nc + npar:nc + npar + no], refs[nc + npar + no:])

    in_specs = [pl.BlockSpec((t, tc), functools.partial(lambda j, off: (0, j + off), off=off)) for _, off in cols]
    in_specs += [pl.BlockSpec((p.shape[0], tc), lambda j: (0, j)) for p in params]
    out_specs = [pl.BlockSpec((t, tc), lambda j: (0, j)) for _ in outs]
    out_specs += [pl.BlockSpec((r, tc), lambda j: (0, j)) for r in pouts]
    out_shape = [jax.ShapeDtypeStruct((t, c), dt) for dt in outs]
    out_shape += [jax.ShapeDtypeStruct((r, c), F32) for r in pouts]
    return _pcall(name, kern, (c // tc,), (*[x for x, _ in cols], *params), in_specs, out_shape, out_specs,
                  ("parallel",))


def _row_index(shape):
    return lax.broadcasted_iota(jnp.int32, shape, 0)


def _shift_down(x, d, rows):
    return jnp.where(rows >= d, pltpu.roll(x, d, 0), 0.0)


def _shift_up(x, d, rows):
    t = x.shape[0]
    return jnp.where(rows < t - d, pltpu.roll(x, t - d, 0), 0.0)


def sconv_fwd(name, proj, col0, conv_w, tc=128):
    nb = SC_WIDTH // tc

    def body(cl, p, o, po):
        b, c, u = (x[...] for x in cl)
        rows = _row_index(b.shape)
        w = p[0][...]
        z = c * u
        conv = w[2:3] * z + w[1:2] * _shift_down(z, 1, rows) + w[0:1] * _shift_down(z, 2, rows)
        o[0][...] = (b * conv).astype(BF16)
    return colwise(name, body, [(proj, col0), (proj, col0 + nb), (proj, col0 + 2 * nb)], [conv_w], [BF16], tc=tc)[0]


def sconv_bwd(name, proj, col0, conv_w, dyb, dcol0, tc=128):
    nb = SC_WIDTH // tc

    def body(cl, p, o, po):
        b, c, u, dy = (x[...] for x in cl)
        rows = _row_index(b.shape)
        w = p[0][...]
        z = c * u
        z1, z2 = _shift_down(z, 1, rows), _shift_down(z, 2, rows)
        conv = w[2:3] * z + w[1:2] * z1 + w[0:1] * z2
        dconv = dy * b
        dz = w[2:3] * dconv + w[1:2] * _shift_up(dconv, 1, rows) + w[0:1] * _shift_up(dconv, 2, rows)
        o[0][...] = (dy * conv).astype(BF16)
        o[1][...] = (dz * u).astype(BF16)
        o[2][...] = (dz * c).astype(BF16)
        po[0][0:1, :] = jnp.sum(dconv * z2, axis=0, keepdims=True)
        po[0][1:2, :] = jnp.sum(dconv * z1, axis=0, keepdims=True)
        po[0][2:3, :] = jnp.sum(dconv * z, axis=0, keepdims=True)
    return colwise(name, body, [(proj, col0), (proj, col0 + nb), (proj, col0 + 2 * nb), (dyb, dcol0)], [conv_w],
                   [BF16, BF16, BF16], [3], tc=tc)


def _expm1(x):
    series = x * (1.0 + 0.5 * x * (1.0 + x * (1.0 / 3.0) * (1.0 + 0.25 * x * (1.0 + 0.2 * x))))
    return jnp.where(jnp.abs(x) < 0.05, series, jnp.exp(x) - 1.0)


def _log1p(x):
    series = x * (1.0 - x * (0.5 - x * (1.0 / 3.0 - 0.25 * x)))
    return jnp.where(jnp.abs(x) < 0.01, series, jnp.log(1.0 + x))


def _softplus_neg(lam):
    sp = jnp.maximum(-lam, 0.0) + _log1p(jnp.exp(-jnp.abs(lam)))
    return sp, -_sigmoid(-lam)


GELU_C = math.sqrt(2.0 / math.pi)


def _gelu(x):
    th = jnp.tanh(GELU_C * (x + 0.044715 * x * x * x))
    val = 0.5 * x * (1.0 + th)
    grad = 0.5 * (1.0 + th) + 0.5 * x * (1.0 - th * th) * GELU_C * (1.0 + 3.0 * 0.044715 * x * x)
    return val, grad


def rg_conv_fwd(name, gu2, conv_w, conv_b, tc=128):
    nb = D_MODEL // tc

    def body(cl, p, o, po):
        u = cl[0][...]
        rows = _row_index(u.shape)
        w = p[0][...]
        o[0][...] = (w[3:4] * u + w[2:3] * _shift_down(u, 1, rows) + w[1:2] * _shift_down(u, 2, rows)
                     + w[0:1] * _shift_down(u, 3, rows) + p[1][...])
    return colwise(name, body, [(gu2, nb)], [conv_w, conv_b], [F32], tc=tc)[0]


def rg_conv_bwd(name, gu2, duc, conv_w, tc=128):
    nb = D_MODEL // tc

    def body(cl, p, o, po):
        u, d = cl[0][...], cl[1][...]
        rows = _row_index(u.shape)
        w = p[0][...]
        o[0][...] = (w[3:4] * d + w[2:3] * _shift_up(d, 1, rows) + w[1:2] * _shift_up(d, 2, rows)
                     + w[0:1] * _shift_up(d, 3, rows)).astype(BF16)
        for k in range(4):
            uk = u if k == 3 else _shift_down(u, 3 - k, rows)
            po[0][k:k + 1, :] = jnp.sum(d * uk, axis=0, keepdims=True)
        po[1][...] = jnp.sum(d, axis=0, keepdims=True)
    return colwise(name, body, [(gu2, nb), (duc, 0)], [conv_w], [BF16], [4, 1], tc=tc)


def rg_gates_fwd(name, uc, w_a, b_a, w_i, b_i, tr=512):
    t = uc.shape[0]
    tr = min(tr, t)

    def kern(u_ref, wa_ref, ba_ref, wi_ref, bi_ref, r_ref, i_ref):
        ub = u_ref[...].astype(BF16)
        r_ref[...] = _sigmoid(_dot(ub, wa_ref[...]) + ba_ref[...])
        i_ref[...] = _sigmoid(_dot(ub, wi_ref[...]) + bi_ref[...])

    blk = pl.BlockSpec((tr, LRU_BW), lambda n, i: (i, n))
    wspec = pl.BlockSpec((None, LRU_BW, LRU_BW), lambda n, i: (n, 0, 0))
    bspec = pl.BlockSpec((1, LRU_BW), lambda n, i: (0, n))
    return _pcall(name, kern, (LRU_BLOCKS, t // tr), (uc, w_a, b_a, w_i, b_i), [blk, wspec, bspec, wspec, bspec],
                  [jax.ShapeDtypeStruct(uc.shape, F32)] * 2, [blk, blk], ("parallel", "parallel"))


def rg_gates_bwd(name, uc, dzr, dzi, duc_part, w_a, w_i):
    t = uc.shape[0]
    rows = LRU_BW // N_CHIPS

    def kern(u_ref, dr_ref, di_ref, dp_ref, wa_ref, wi_ref, duc_ref, dwa_ref, dwi_ref):
        ub = u_ref[...].astype(BF16)
        dr, di = dr_ref[...], di_ref[...]
        dwa, dwi = _dot(ub, dr, TN), _dot(ub, di, TN)
        for p in range(N_CHIPS):
            dwa_ref[p] = dwa[p * rows:(p + 1) * rows].astype(dwa_ref.dtype)
            dwi_ref[p] = dwi[p * rows:(p + 1) * rows].astype(dwi_ref.dtype)
        duc_ref[...] = dp_ref[...] + _dot(dr, wa_ref[...], NT) + _dot(di, wi_ref[...], NT)

    blk = pl.BlockSpec((t, LRU_BW), lambda n: (0, n))
    wspec = pl.BlockSpec((None, LRU_BW, LRU_BW), lambda n: (n, 0, 0))
    gspec = pl.BlockSpec((N_CHIPS, None, rows, LRU_BW), lambda n: (0, n, 0, 0))
    gshape = jax.ShapeDtypeStruct((N_CHIPS, LRU_BLOCKS, rows, LRU_BW), BF16)
    return _pcall(name, kern, (LRU_BLOCKS,), (uc, dzr, dzi, duc_part, w_a, w_i), [blk, blk, blk, blk, wspec, wspec],
                  [jax.ShapeDtypeStruct(uc.shape, F32), gshape, gshape], [blk, gspec, gspec], ("parallel",))


def _rg_decay(r, lam):
    sp, dsp = _softplus_neg(lam)
    la = -RG_C * r * sp
    a = jnp.exp(la)
    sq = jnp.sqrt(-_expm1(2.0 * la))
    return sp, dsp, a, sq


def rg_scan_fwd(name, gu2, uc, r, i, lam, tc=128):
    def body(cl, p, o, po):
        gate, ucv, rv, iv = (x[...] for x in cl)
        t = gate.shape[0]
        rows = _row_index(gate.shape)
        _, _, a, sq = _rg_decay(rv, p[0][...])
        b = sq * (iv * ucv)
        d = 1
        while d < t:
            keep = rows >= d
            b = a * jnp.where(keep, pltpu.roll(b, d, 0), 0.0) + b
            a = a * jnp.where(keep, pltpu.roll(a, d, 0), 1.0)
            d *= 2
        o[0][...] = (_gelu(gate)[0] * b).astype(BF16)
        o[1][...] = b
    return colwise(name, body, [(gu2, 0), (uc, 0), (r, 0), (i, 0)], [lam], [BF16, F32], tc=tc)


def rg_scan_bwd(name, gu2, uc, r, i, hs, dy, lam, tc=128):
    def body(cl, p, o, po):
        gate, ucv, rv, iv, h, dyv = (x[...] for x in cl)
        t = gate.shape[0]
        rows = _row_index(gate.shape)
        sp, dsp, a, sq = _rg_decay(rv, p[0][...])
        gl, dgl = _gelu(gate)
        o[0][...] = (dyv * h * dgl).astype(BF16)
        g = dyv * gl
        am = _shift_up(a, 1, rows)
        d = 1
        while d < t:
            keep = rows < t - d
            g = am * jnp.where(keep, pltpu.roll(g, t - d, 0), 0.0) + g
            am = am * jnp.where(keep, pltpu.roll(am, t - d, 0), 0.0)
            d *= 2
        da = g * _shift_down(h, 1, rows)
        iu = iv * ucv
        d_iu = g * sq
        dla = da * a - (g * iu) * (a * a) / sq
        dzr = dla * (-RG_C * sp) * rv * (1.0 - rv)
        dzi = d_iu * ucv * iv * (1.0 - iv)
        o[1][...] = dzr.astype(BF16)
        o[2][...] = dzi.astype(BF16)
        o[3][...] = d_iu * iv
        po[0][...] = jnp.sum(dzr, axis=0, keepdims=True)
        po[1][...] = jnp.sum(dzi, axis=0, keepdims=True)
        po[2][...] = jnp.sum(dla * rv, axis=0, keepdims=True) * (-RG_C) * dsp
    return colwise(name, body, [(gu2, 0), (uc, 0), (r, 0), (i, 0), (hs, 0), (dy, 0)], [lam],
                   [BF16, BF16, BF16, F32], [1, 1, 1], tc=tc)


def _split3(x):
    hi = x.astype(BF16)
    r1 = x - hi.astype(F32)
    mid = r1.astype(BF16)
    lo = (r1 - mid.astype(F32)).astype(BF16)
    return hi, mid, lo


def _tri_dot(x, tri):
    out = None
    for piece in _split3(x):
        term = lax.dot_general(piece, tri, NN, preferred_element_type=F32)
        out = term if out is None else out + term
    return out


def fox_gates_fwd(name, z_t, b_f):
    h, t = z_t.shape
    tb = min(512, t)

    def kern(z_ref, b_ref, o_ref):
        z = z_ref[...] + b_ref[...]
        logf = jnp.minimum(z, 0.0) - _log1p(jnp.exp(-jnp.abs(z)))
        src = lax.broadcasted_iota(jnp.int32, (t, tb), 0)
        dst = lax.broadcasted_iota(jnp.int32, (t, tb), 1) + pl.program_id(0) * tb
        o_ref[...] = _tri_dot(logf, (src <= dst).astype(BF16))

    return _pcall(name, kern, (t // tb,), (z_t, b_f),
                  [pl.BlockSpec((h, t), lambda j: (0, 0)), pl.BlockSpec((h, 1), lambda j: (0, 0))],
                  jax.ShapeDtypeStruct((h, t), F32), pl.BlockSpec((h, tb), lambda j: (0, j)), ("parallel",))


def fox_gates_bwd(name, z_t, b_f, dcum_t):
    h, t = z_t.shape
    tb = min(512, t)

    def kern(z_ref, b_ref, d_ref, dz_ref, db_ref):
        @pl.when(pl.program_id(0) == 0)
        def _():
            db_ref[...] = jnp.zeros_like(db_ref)
        src = lax.broadcasted_iota(jnp.int32, (t, tb), 0)
        dst = lax.broadcasted_iota(jnp.int32, (t, tb), 1) + pl.program_id(0) * tb
        dlogf = _tri_dot(d_ref[...], (src >= dst).astype(BF16))
        z = z_ref[...] + b_ref[...]
        dz = dlogf * _sigmoid(-z)
        dz_ref[...] = dz
        db_ref[...] += jnp.sum(dz, axis=1, keepdims=True)

    return _pcall(name, kern, (t // tb,), (z_t, b_f, dcum_t),
                  [pl.BlockSpec((h, tb), lambda j: (0, j)), pl.BlockSpec((h, 1), lambda j: (0, 0)),
                   pl.BlockSpec((h, t), lambda j: (0, 0))],
                  [jax.ShapeDtypeStruct((h, t), F32), jax.ShapeDtypeStruct((h, 1), F32)],
                  [pl.BlockSpec((h, tb), lambda j: (0, j)), pl.BlockSpec((h, 1), lambda j: (0, 0))], ("arbitrary",))


def _fox_spans(qs, k_ref, cr_ref, i, tq):
    n0 = i * tq
    sd = _dot(qs, k_ref[n0:n0 + tq, :], NT) - cr_ref[:, n0:n0 + tq]
    row = lax.broadcasted_iota(jnp.int32, (tq, tq), 0)
    col = lax.broadcasted_iota(jnp.int32, (tq, tq), 1)
    spans = [(n0, tq, jnp.where(row >= col, sd, NEG_INF))]
    if i > 0:
        spans.append((0, n0, _dot(qs, k_ref[0:n0, :], NT) - cr_ref[:, 0:n0]))
    return spans


def fox_fwd(name, q, k, v, cum_r, tq=256):
    h, t, dh = q.shape
    tq = min(tq, t)
    scale = FOX_HEAD_DIM ** -0.5

    def kern(q_ref, k_ref, v_ref, cr_ref, o_ref, lse_ref):
        for i in range(t // tq):
            rows = slice(i * tq, (i + 1) * tq)
            spans = _fox_spans(q_ref[rows, :] * scale, k_ref, cr_ref, i, tq)
            m = functools.reduce(jnp.maximum, [jnp.max(s, axis=-1, keepdims=True) for _, _, s in spans])
            l, acc = 0.0, 0.0
            for k0, kn, s in spans:
                p = jnp.exp(s - m)
                l = l + jnp.sum(p, axis=-1, keepdims=True)
                acc = acc + _dot(p, v_ref[k0:k0 + kn, :])
            o_ref[rows, :] = (acc / l).astype(o_ref.dtype)
            lse_ref[rows, :] = m + jnp.log(l)

    hspec = pl.BlockSpec((None, t, dh), lambda a: (a, 0, 0))
    cspec = pl.BlockSpec((None, t, 1), lambda a: (a, 0, 0))
    rspec = pl.BlockSpec((None, 1, t), lambda a: (a, 0, 0))
    return _pcall(name, kern, (h,), (q, k, v, cum_r), [hspec, hspec, hspec, rspec],
                  [jax.ShapeDtypeStruct((h, t, dh), BF16), jax.ShapeDtypeStruct((h, t, 1), F32)],
                  [hspec, cspec], ("parallel",))


def fox_bwd(name, q, k, v, do, lse, cum_r, tq=256):
    h, t, dh = q.shape
    tq = min(tq, t)
    scale = FOX_HEAD_DIM ** -0.5

    def kern(q_ref, k_ref, v_ref, do_ref, lse_ref, cr_ref, dq_ref, dk_ref, dv_ref, dc_ref):
        dk_ref[...] = jnp.zeros_like(dk_ref)
        dv_ref[...] = jnp.zeros_like(dv_ref)
        dc_ref[...] = jnp.zeros_like(dc_ref)
        for i in range(t // tq):
            rows = slice(i * tq, (i + 1) * tq)
            qs, dov, lse_v = q_ref[rows, :] * scale, do_ref[rows, :], lse_ref[rows, :]
            spans = _fox_spans(qs, k_ref, cr_ref, i, tq)
            probs = [jnp.exp(s - lse_v) for _, _, s in spans]
            dps = [_dot(dov, v_ref[k0:k0 + kn, :], NT) for k0, kn, _ in spans]
            rowdot = sum(jnp.sum(dp * p, axis=-1, keepdims=True) for dp, p in zip(dps, probs))
            dq = 0.0
            for (k0, kn, _), p, dp in zip(spans, probs, dps):
                ds = p * (dp - rowdot)
                dq = dq + _dot(ds, k_ref[k0:k0 + kn, :])
                dk_ref[k0:k0 + kn, :] += _dot(ds, qs, TN)
                dv_ref[k0:k0 + kn, :] += _dot(p, dov, TN)
                dc_ref[:, k0:k0 + kn] -= jnp.sum(ds, axis=0, keepdims=True)
            dq_ref[rows, :] = (dq * scale).astype(dq_ref.dtype)

    hspec = pl.BlockSpec((None, t, dh), lambda a: (a, 0, 0))
    cspec = pl.BlockSpec((None, t, 1), lambda a: (a, 0, 0))
    rspec = pl.BlockSpec((None, 1, t), lambda a: (a, 0, 0))
    return _pcall(name, kern, (h,), (q, k, v, do, lse, cum_r), [hspec, hspec, hspec, hspec, cspec, rspec],
                  [jax.ShapeDtypeStruct((h, t, dh), BF16), jax.ShapeDtypeStruct((h, t, dh), F32),
                   jax.ShapeDtypeStruct((h, t, dh), F32), jax.ShapeDtypeStruct((h, 1, t), F32)],
                  [hspec, hspec, hspec, rspec], ("parallel",))


def _xattn_probs(q, k):
    s = _dot(q, k, NT) * (MEM_HEAD_DIM ** -0.5)
    p = jnp.exp(s - jnp.max(s, axis=-1, keepdims=True))
    return p / jnp.sum(p, axis=-1, keepdims=True)


def xattn_fwd(name, q, kv, tq=512):
    t = q.shape[0]
    tq = min(tq, t)
    ml = kv.shape[0]

    def kern(q_ref, k_ref, v_ref, o_ref):
        o_ref[...] = _dot(_xattn_probs(q_ref[...], k_ref[...]), v_ref[...]).astype(o_ref.dtype)

    qspec = pl.BlockSpec((tq, MEM_HEAD_DIM), lambda i, a: (i, a))
    return _pcall(name, kern, (t // tq, MEM_HEADS), (q, kv, kv),
                  [qspec, pl.BlockSpec((ml, MEM_HEAD_DIM), lambda i, a: (0, a)),
                   pl.BlockSpec((ml, MEM_HEAD_DIM), lambda i, a: (0, MEM_HEADS + a))],
                  jax.ShapeDtypeStruct(q.shape, BF16), qspec, ("parallel", "parallel"))


def xattn_bwd(name, q, kv, do, tq=512):
    t = q.shape[0]
    tq = min(tq, t)
    ml = kv.shape[0]
    scale = MEM_HEAD_DIM ** -0.5

    def kern(q_ref, k_ref, v_ref, do_ref, dq_ref, dk_ref, dv_ref):
        @pl.when(pl.program_id(1) == 0)
        def _():
            dk_ref[...] = jnp.zeros_like(dk_ref)
            dv_ref[...] = jnp.zeros_like(dv_ref)
        qv, kv_, dov = q_ref[...], k_ref[...], do_ref[...]
        p = _xattn_probs(qv, kv_)
        dp = _dot(dov, v_ref[...], NT)
        ds = p * (dp - jnp.sum(dp * p, axis=-1, keepdims=True)) * scale
        dq_ref[...] = _dot(ds, kv_).astype(dq_ref.dtype)
        dk_ref[...] += _dot(ds, qv, TN)
        dv_ref[...] += _dot(p, dov, TN)

    qspec = pl.BlockSpec((tq, MEM_HEAD_DIM), lambda a, i: (i, a))
    kspec = pl.BlockSpec((ml, MEM_HEAD_DIM), lambda a, i: (0, a))
    return _pcall(name, kern, (MEM_HEADS, t // tq), (q, kv, kv, do),
                  [qspec, kspec, pl.BlockSpec((ml, MEM_HEAD_DIM), lambda a, i: (0, MEM_HEADS + a)), qspec],
                  [jax.ShapeDtypeStruct(q.shape, BF16), jax.ShapeDtypeStruct((ml, D_MODEL), F32),
                   jax.ShapeDtypeStruct((ml, D_MODEL), F32)],
                  [qspec, kspec, kspec], ("parallel", "arbitrary"))


def _heads(x):
    t = x.shape[0]
    return x.reshape(t, FOX_HEADS, FOX_HEAD_DIM).transpose(1, 0, 2)


def _unheads(x):
    return x.transpose(1, 0, 2).reshape(x.shape[1], FOX_WIDTH)


def _row_cut(dw):
    return dw.reshape(N_CHIPS, 2, dw.shape[0] // (2 * N_CHIPS), dw.shape[1])


def local_step(x, mem, target, w, layer_weights=None, reduce_hook=None):
    depth = w["g_mix_pre"].shape[0]
    t = x.shape[0]
    saved = []
    i1, i2, i3 = 3 * FOX_WIDTH, 3 * FOX_WIDTH + FOX_HEADS, AB_IN
    ncol = 128

    def stacked_weights(layer, part, _):
        names = COMMON_BIG if part == "rest" else layer_big(layer)[len(COMMON_BIG):]
        return {n: w[n][layer if n in COMMON_BIG else layer // 2] for n in names}

    get_weights = layer_weights or stacked_weights
    h1 = rms_pre("l0_mix_pre", x, w["g_mix_pre"], 0)
    for layer in range(depth):
        lw = dict(get_weights(layer, "mix", x))
        s = {"x0": x, "lw": lw}
        tag = f"l{layer}"
        s["h1"] = h1
        if layer % 2 == 0:
            e = layer // 2
            w_in = jnp.pad(lw["ab_w_in"], ((0, 0), (0, AB_IN_PAD - AB_IN)))
            proj = mm(f"{tag}_ab_in", h1, w_in, "nn", F32)
            qkv = proj[:, :i1].astype(BF16).reshape(t, 3, FOX_HEADS, FOX_HEAD_DIM).transpose(1, 2, 0, 3)
            z_t = proj[:, i1:i2].T
            b_f = w["ab_b_f"][e].reshape(FOX_HEADS, 1)
            cum_t = fox_gates_fwd(f"{tag}_fox_gates", z_t, b_f)
            cum_r = cum_t[:, None, :]
            oh, lse = fox_fwd(f"{tag}_fox", qkv[0], qkv[1], qkv[2], cum_r)
            bcu = proj[:, i2:i3]
            y_b = sconv_fwd(f"{tag}_sconv", bcu, 0, w["ab_conv_w"][e])
            ycat = jnp.concatenate([_unheads(oh), y_b], axis=1)
            y1 = mm(f"{tag}_ab_out", ycat, lw["ab_w_out"], "nn", BF16)
            s.update(w_in=w_in, qkv=qkv, z_t=z_t, b_f=b_f, cum_r=cum_r, lse=lse, bcu=bcu, ycat=ycat)
        else:
            o = layer // 2
            gu2 = mm(f"{tag}_c_in", h1, lw["c_w_in"], "nn", F32)
            conv_b = w["c_conv_b"][o].reshape(1, -1)
            uc = rg_conv_fwd(f"{tag}_rg_conv", gu2, w["c_conv_w"][o], conv_b)
            b_a, b_i = w["c_b_a"][o].reshape(1, -1), w["c_b_i"][o].reshape(1, -1)
            r, i = rg_gates_fwd(f"{tag}_rg_gates", uc, lw["c_w_a"], b_a, lw["c_w_i"], b_i)
            lam = w["c_lam"][o].reshape(1, -1)
            ymix, hs = rg_scan_fwd(f"{tag}_rg_scan", gu2, uc, r, i, lam)
            y1 = mm(f"{tag}_c_out", ymix, lw["c_w_out"], "nn", BF16)
            s.update(gu2=gu2, uc=uc, r=r, i=i, lam=lam, hs=hs, ymix=ymix)
        s["y1"] = y1
        x, h2 = post_add_pre(f"{tag}_mix_post", x, y1, w["g_mix_post"], layer, w["g_cross_pre"], layer)
        lw.update(get_weights(layer, "rest", x))
        s["x1"] = x
        m = rms_pre(f"{tag}_mem_pre", mem, w["g_mem"], layer)
        q = mm(f"{tag}_xq", h2, lw["w_xq"], "nn", BF16)
        kv = mm(f"{tag}_xkv", m, lw["w_xkv"], "nn", BF16)
        o_att = xattn_fwd(f"{tag}_xattn", q, kv)
        y2 = mm(f"{tag}_xo", o_att, lw["w_xo"], "nn", BF16)
        s.update(h2=h2, m=m, q=q, kv=kv, o_att=o_att, y2=y2)
        x, h3 = post_add_pre(f"{tag}_cross_post", x, y2, w["g_cross_post"], layer, w["g_ffn_pre"], layer)
        s["x2"] = x
        gu = mm(f"{tag}_ffn_gu", h3, lw["w_ffn_gu"], "nn", BF16)
        act = swiglu_fwd(f"{tag}_swiglu", gu)
        y3 = mm(f"{tag}_ffn_down", act, lw["w_ffn_down"], "nn", BF16)
        s.update(h3=h3, gu=gu, act=act, y3=y3)
        if layer + 1 < depth:
            x, h1 = post_add_pre(f"{tag}_ffn_post", x, y3, w["g_ffn_post"], layer, w["g_mix_pre"], layer + 1)
        else:
            x = post_add(f"{tag}_ffn_post", x, y3, w["g_ffn_post"], layer)
        saved.append(s)

    dx, sq_cols = loss_head("loss_head", x, target)

    grads = {k: [None] * v.shape[0] for k, v in w.items() if k not in BIG}
    big = {}

    def dw(name, a, b, cols_cut=False):
        return mm(name, a, b, "tn", BF16, reduce_layout=True) if cols_cut else _row_cut(mm(name, a, b, "tn", BF16))

    def hook(layer, part, part_grads, after, gains):
        token = None if reduce_hook is None else reduce_hook(layer, part, part_grads, after)
        return gains if token is None else gains + token

    dy3, grads["g_ffn_post"][depth - 1] = post_bwd(f"b{depth - 1}_ffn_post", saved[-1]["y3"], dx, w["g_ffn_post"], depth - 1)
    for layer in reversed(range(depth)):
        s = saved[layer]
        lw = s["lw"]
        tag = f"b{layer}"
        lg = {}
        dact = mm(f"{tag}_ffn_down_dx", dy3, lw["w_ffn_down"], "nt", BF16)
        lg["w_ffn_down"] = dw(f"{tag}_ffn_down_dw", s["act"], dy3)
        dgu = swiglu_bwd(f"{tag}_swiglu", s["gu"], dact)
        dh3 = mm(f"{tag}_ffn_gu_dx", dgu, lw["w_ffn_gu"], "nt", F32)
        lg["w_ffn_gu"] = dw(f"{tag}_ffn_gu_dw", s["h3"], dgu, cols_cut=True)
        g_ffn_pre = hook(layer, "ffn", lg, dh3, w["g_ffn_pre"])
        ffn_grads, lg = lg, {}
        dx, dy2, grads["g_ffn_pre"][layer], grads["g_cross_post"][layer] = pre_post_bwd(
            f"{tag}_ffn_pre", s["x2"], dh3, dx, g_ffn_pre, layer, s["y2"], w["g_cross_post"], layer)
        do = mm(f"{tag}_xo_dx", dy2, lw["w_xo"], "nt", BF16)
        lg["w_xo"] = dw(f"{tag}_xo_dw", s["o_att"], dy2)
        dq, dk, dv = xattn_bwd(f"{tag}_xattn", s["q"], s["kv"], do)
        dh2 = mm(f"{tag}_xq_dx", dq, lw["w_xq"], "nt", F32)
        lg["w_xq"] = dw(f"{tag}_xq_dw", s["h2"], dq)
        dkv = jnp.concatenate([dk, dv], axis=1).astype(BF16)
        dm = mm(f"{tag}_xkv_dx", dkv, lw["w_xkv"], "nt", F32)
        lg["w_xkv"] = dw(f"{tag}_xkv_dw", s["m"], dkv, cols_cut=True)
        grads["g_mem"][layer] = gain_bwd(f"{tag}_mem_pre", mem, dm)
        g_cross_pre = hook(layer, "rest", lg, dh2, w["g_cross_pre"])
        dx, dy1, grads["g_cross_pre"][layer], grads["g_mix_post"][layer] = pre_post_bwd(
            f"{tag}_cross_pre", s["x1"], dh2, dx, g_cross_pre, layer, s["y1"], w["g_mix_post"], layer)
        rest_grads, lg = lg, {}
        if layer % 2 == 0:
            e = layer // 2
            dycat = mm(f"{tag}_ab_out_dx", dy1, lw["ab_w_out"], "nt", F32)
            g_mix_pre = hook(layer, "mixer", {}, dycat, w["g_mix_pre"])
            lg["ab_w_out"] = dw(f"{tag}_ab_out_dw", s["ycat"], dy1)
            do_h = _heads(dycat[:, :FOX_WIDTH].astype(BF16))
            qkv = s["qkv"]
            dqh, dkh, dvh, dcum = fox_bwd(f"{tag}_fox", qkv[0], qkv[1], qkv[2], do_h, s["lse"], s["cum_r"])
            dz_t, db_f = fox_gates_bwd(f"{tag}_fox_gates", s["z_t"], s["b_f"], dcum.reshape(FOX_HEADS, t))
            grads["ab_b_f"][e] = db_f.reshape(FOX_HEADS)
            db, dc, du, dconv_w = sconv_bwd(f"{tag}_sconv", s["bcu"], 0, w["ab_conv_w"][e], dycat, FOX_WIDTH // ncol)
            grads["ab_conv_w"][e] = dconv_w
            dproj = jnp.concatenate(
                [_unheads(dqh), _unheads(dkh).astype(BF16), _unheads(dvh).astype(BF16), dz_t.T.astype(BF16), db, dc, du,
                 jnp.zeros((t, AB_IN_PAD - AB_IN), BF16)], axis=1)
            dh1 = mm(f"{tag}_ab_in_dx", dproj, s["w_in"], "nt", F32)
            dw_in = mm(f"{tag}_ab_in_dw", s["h1"], dproj, "tn", F32)[:, :AB_IN]
            lg["ab_w_in"] = dw_in.reshape(2, D_MODEL // 2, N_CHIPS, AB_IN // N_CHIPS).transpose(2, 0, 1, 3).astype(BF16)
        else:
            o = layer // 2
            dymix = mm(f"{tag}_c_out_dx", dy1, lw["c_w_out"], "nt", F32)
            g_mix_pre = hook(layer, "mixer", {}, dymix, w["g_mix_pre"])
            lg["c_w_out"] = dw(f"{tag}_c_out_dw", s["ymix"], dy1)
            dgate, dzr, dzi, duc_part, db_a, db_i, dlam = rg_scan_bwd(
                f"{tag}_rg_scan", s["gu2"], s["uc"], s["r"], s["i"], s["hs"], dymix, s["lam"])
            duc, dw_a, dw_i = rg_gates_bwd(f"{tag}_rg_gates", s["uc"], dzr, dzi, duc_part, lw["c_w_a"], lw["c_w_i"])
            lg["c_w_a"] = dw_a.reshape(N_CHIPS, 2, LRU_BW // 2, LRU_BW)
            lg["c_w_i"] = dw_i.reshape(N_CHIPS, 2, LRU_BW // 2, LRU_BW)
            du_raw, dconv_w, dconv_b = rg_conv_bwd(f"{tag}_rg_conv", s["gu2"], duc, w["c_conv_w"][o])
            grads["c_b_a"][o] = db_a.reshape(LRU_BLOCKS, LRU_BW)
            grads["c_b_i"][o] = db_i.reshape(LRU_BLOCKS, LRU_BW)
            grads["c_lam"][o] = dlam.reshape(-1)
            grads["c_conv_w"][o] = dconv_w
            grads["c_conv_b"][o] = dconv_b.reshape(-1)
            dgu2 = jnp.concatenate([dgate, du_raw], axis=1)
            dh1 = mm(f"{tag}_c_in_dx", dgu2, lw["c_w_in"], "nt", F32)
            lg["c_w_in"] = dw(f"{tag}_c_in_dw", s["h1"], dgu2, cols_cut=True)
        if reduce_hook is None:
            big[layer] = {**ffn_grads, **rest_grads, **lg}
        g_mix_pre = hook(layer, "mix", lg, dh1, g_mix_pre)
        if layer > 0:
            dx, dy3, grads["g_mix_pre"][layer], grads["g_ffn_post"][layer - 1] = pre_post_bwd(
                f"{tag}_mix_pre", s["x0"], dh1, dx, g_mix_pre, layer, saved[layer - 1]["y3"], w["g_ffn_post"], layer - 1)
        else:
            dx, grads["g_mix_pre"][layer] = pre_bwd(f"{tag}_mix_pre", s["x0"], dh1, dx, g_mix_pre, layer)

    for k in list(grads):
        if k.startswith("g_"):
            grads[k] = [g.reshape(-1) for g in grads[k]]
        grads[k] = jnp.stack(grads[k])
    return sq_cols, dx, grads, big


CHIP_FLIPS = ((1, 0), (0, 1), (1, 1))
HBM_SPEC = pl.BlockSpec(memory_space=pltpu.HBM)
VMEM_SPEC = pl.BlockSpec(memory_space=pltpu.VMEM)


def _place():
    return lax.axis_index("x"), lax.axis_index("y"), lax.axis_index("c")


def _flip(v, f):
    return 1 - v if f else v


def _remote(src, dst, send_sem, recv_sem, target):
    return pltpu.make_async_remote_copy(src_ref=src, dst_ref=dst, send_sem=send_sem, recv_sem=recv_sem,
                                        device_id=target, device_id_type=MESH)


SEM_SPEC = pl.BlockSpec(memory_space=pltpu.SEMAPHORE)


def _swap_copies(srcs, lands, send_sems, recv_sems):
    x, y, c = _place()
    return [_remote(src.at[:, 1 - c], land, send_sems.at[len(CHIP_FLIPS) * a], recv_sems.at[len(CHIP_FLIPS) * a],
                    (x, y, 1 - c)) for a, (src, land) in enumerate(zip(srcs, lands))]


def _exchange_copies(srcs, lands, send_sems, recv_sems):
    x, y, c = _place()
    p = 2 * x + y
    cps = []
    for a, (src, land) in enumerate(zip(srcs, lands)):
        for k, (fx, fy) in enumerate(CHIP_FLIPS):
            qx, qy = _flip(x, fx), _flip(y, fy)
            sem = len(CHIP_FLIPS) * a + k
            cps.append(_remote(src.at[2 * qx + qy], land.at[p], send_sems.at[sem], recv_sems.at[sem], (qx, qy, c)))
    return cps


def _gather_copies(srcs, lands, send_sems, recv_sems):
    x, y, c = _place()
    p = 2 * x + y
    cps = []
    for a, (src, land) in enumerate(zip(srcs, lands)):
        for k, (fx, fy) in enumerate(CHIP_FLIPS):
            sem = len(CHIP_FLIPS) * a + k
            cps.append(_remote(src.at[c], land.at[p, c], send_sems.at[sem], recv_sems.at[sem],
                               (_flip(x, fx), _flip(y, fy), c)))
    return cps


def copies_start(name, make_copies, srcs, land_shapes):
    n = len(srcs)

    def body(*refs):
        for cp in make_copies(refs[:n], refs[n:2 * n], refs[2 * n], refs[2 * n + 1]):
            cp.start()
        refs[-1][...] = jnp.zeros_like(refs[-1])

    thru = [pltpu.HBM(b.shape, b.dtype) for b in srcs] + [pltpu.HBM(sh, b.dtype) for sh, b in zip(land_shapes, srcs)]
    outs = pl.pallas_call(
        body, name=name, in_specs=[HBM_SPEC] * (2 * n),
        out_shape=(pltpu.SemaphoreType.DMA((3 * n,)), pltpu.SemaphoreType.DMA((3 * n,)), *thru,
                   jax.ShapeDtypeStruct((8, 128), F32)),
        out_specs=(SEM_SPEC, SEM_SPEC, *[HBM_SPEC] * (2 * n), VMEM_SPEC),
        input_output_aliases={i: 2 + i for i in range(2 * n)},
        compiler_params=pltpu.CompilerParams(has_side_effects=pltpu.SideEffectType.DATAFLOW_SIDE_EFFECTING),
    )(*[pltpu.with_memory_space_constraint(b, pltpu.HBM) for b in srcs],
      *[pltpu.with_memory_space_constraint(lax.empty(sh, b.dtype), pltpu.HBM) for sh, b in zip(land_shapes, srcs)])
    return outs[:-1], outs[-1]


def copies_wait(name, make_copies, state, after):
    send_sems, recv_sems, *thru = state
    n = len(thru) // 2

    def body(*refs):
        for cp in make_copies(refs[:n], refs[n:2 * n], refs[2 * n], refs[2 * n + 1]):
            cp.wait_send()
            cp.wait_recv()

    outs = pl.pallas_call(
        body, name=name, in_specs=[HBM_SPEC] * (2 * n) + [SEM_SPEC, SEM_SPEC, pl.BlockSpec(memory_space=pl.ANY)],
        out_shape=tuple(pltpu.HBM(t.shape, t.dtype) for t in thru), out_specs=tuple([HBM_SPEC] * (2 * n)),
        input_output_aliases={i: i for i in range(2 * n)},
        compiler_params=pltpu.CompilerParams(has_side_effects=pltpu.SideEffectType.DATAFLOW_SIDE_EFFECTING),
    )(*thru, send_sems, recv_sems, after)
    return outs[:n], outs[n:]


def pass_to_sibling(name, shards, lands):
    n = len(lands)

    def body(*refs):
        own, ins, outs = refs[:n], refs[n:2 * n], refs[2 * n:3 * n]
        send_sems, recv_sems = refs[3 * n:]
        x, y, c = _place()
        sibling = (x, y, 1 - c)
        cps = []
        for a in range(n):
            for k, (fx, fy) in enumerate(CHIP_FLIPS):
                q = 2 * _flip(x, fx) + _flip(y, fy)
                cps.append(_remote(ins[a].at[q, c], outs[a].at[q, c], send_sems.at[a, k], recv_sems.at[a, k], sibling))
            cps.append(_remote(own[a], outs[a].at[2 * x + y], send_sems.at[a, 3], recv_sems.at[a, 3], sibling))
        for cp in cps:
            cp.start()
        for cp in cps:
            cp.wait()

    return pl.pallas_call(
        body, name=name, in_specs=[HBM_SPEC] * (2 * n), out_specs=[HBM_SPEC] * n,
        out_shape=[jax.ShapeDtypeStruct(b.shape, b.dtype) for b in lands],
        scratch_shapes=[pltpu.SemaphoreType.DMA((n, 4)), pltpu.SemaphoreType.DMA((n, 4))],
        input_output_aliases={n + i: i for i in range(n)},
    )(*shards, *lands)


def share_halves(bufs):
    n = len(bufs)

    def body(*refs):
        ins, outs, token = refs[:n], refs[n:2 * n], refs[2 * n]
        send_sems, recv_sems = refs[2 * n + 1:]
        x, y, c = _place()
        cps = [_remote(ins[a].at[:, c], outs[a].at[:, c], send_sems.at[a], recv_sems.at[a], (x, y, 1 - c))
               for a in range(n)]
        for cp in cps:
            cp.start()
        token[...] = jnp.zeros_like(token)
        for cp in cps:
            cp.wait()

    outs = pl.pallas_call(
        body, name="share_reduced_halves", in_specs=[HBM_SPEC] * n, out_specs=[HBM_SPEC] * n + [VMEM_SPEC],
        out_shape=[jax.ShapeDtypeStruct(b.shape, b.dtype) for b in bufs] + [jax.ShapeDtypeStruct((8, 128), F32)],
        scratch_shapes=[pltpu.SemaphoreType.DMA((n,)), pltpu.SemaphoreType.DMA((n,))],
        input_output_aliases={i: i for i in range(n)},
    )(*bufs)
    return outs[:n], outs[n]


DEVICE_FLIPS = tuple((fx, fy, fc) for fx in (0, 1) for fy in (0, 1) for fc in (0, 1))[1:]


def gather_small(name, v, reduce):
    r, cdim = v.shape
    n_dev = 8

    def body(v_ref, out_ref, *scratch):
        buf = scratch[0] if reduce else out_ref
        send_sems, recv_sems = scratch[-2:]
        x, y, c = _place()
        me = 4 * x + 2 * y + c
        buf[me] = v_ref[...]
        cps = []
        for k, (fx, fy, fc) in enumerate(DEVICE_FLIPS):
            cps.append(_remote(v_ref, buf.at[me], send_sems.at[k], recv_sems.at[k],
                               (_flip(x, fx), _flip(y, fy), _flip(c, fc))))
        for cp in cps:
            cp.start()
        for cp in cps:
            cp.wait()
        if reduce:
            total = buf[0]
            for d in range(1, n_dev):
                total = total + buf[d]
            out_ref[...] = total

    scratch = [pltpu.SemaphoreType.DMA((7,)), pltpu.SemaphoreType.DMA((7,))]
    if reduce:
        scratch = [pltpu.VMEM((n_dev, r, cdim), F32)] + scratch
    out_shape = jax.ShapeDtypeStruct((r, cdim) if reduce else (n_dev, r, cdim), F32)
    return pl.pallas_call(body, name=name, in_specs=[VMEM_SPEC], out_specs=VMEM_SPEC, out_shape=out_shape,
                          scratch_shapes=scratch)(v)


def pair_sum(name, own, got, core):
    _, hx, cols = got.shape
    tr = _tile(hx, (256, 128, 64, 32, 16))

    def kern(core_ref, a_ref, b_ref, o_ref):
        o_ref[...] = (a_ref[...].astype(F32) + b_ref[...].astype(F32)).astype(BF16)

    grid_spec = pltpu.PrefetchScalarGridSpec(
        num_scalar_prefetch=1, grid=(hx // tr,),
        in_specs=[pl.BlockSpec((N_CHIPS, None, tr, cols), lambda i, cr: (0, cr[0], i, 0)),
                  pl.BlockSpec((N_CHIPS, tr, cols), lambda i, cr: (0, i, 0))],
        out_specs=pl.BlockSpec((N_CHIPS, tr, cols), lambda i, cr: (0, i, 0)))
    return pl.pallas_call(
        kern, name=name, grid_spec=grid_spec, out_shape=jax.ShapeDtypeStruct(got.shape, BF16),
        compiler_params=pltpu.CompilerParams(dimension_semantics=("parallel",), vmem_limit_bytes=VMEM_LIMIT_BYTES),
    )(core, own, got)


def chip_sum(name, mine, parts, place, buf, layer):
    _, hx, yd = parts.shape
    tr = _tile(hx, (256, 128, 64, 32, 16))

    def kern(place_ref, m_ref, p_ref, _, o_ref):
        total = None
        for q in range(N_CHIPS):
            term = jnp.where(place_ref[0] == q, m_ref[...], p_ref[q]).astype(F32)
            total = term if total is None else total + term
        o_ref[...] = total

    grid_spec = pltpu.PrefetchScalarGridSpec(
        num_scalar_prefetch=1, grid=(hx // tr,),
        in_specs=[pl.BlockSpec((None, tr, yd), lambda i, pr: (pr[0], i, 0)),
                  pl.BlockSpec((N_CHIPS, tr, yd), lambda i, pr: (0, i, 0)),
                  pl.BlockSpec(memory_space=pl.ANY)],
        out_specs=pl.BlockSpec((None, None, tr, yd), lambda i, pr: (layer, pr[1], i, 0)))
    return pl.pallas_call(
        kern, name=name, grid_spec=grid_spec, out_shape=jax.ShapeDtypeStruct(buf.shape, buf.dtype),
        input_output_aliases={3: 0},
        compiler_params=pltpu.CompilerParams(dimension_semantics=("parallel",), vmem_limit_bytes=VMEM_LIMIT_BYTES),
    )(place, mine, parts, buf)


WEIGHTS = ("g_mix_pre", "g_mix_post", "g_cross_pre", "g_mem", "g_cross_post", "g_ffn_pre", "g_ffn_post", "w_xq", "w_xkv",
           "w_xo", "w_ffn_gu", "w_ffn_down", "ab_w_in", "ab_b_f", "ab_conv_w", "ab_w_out", "c_w_in", "c_conv_w",
           "c_conv_b", "c_w_a", "c_b_a", "c_w_i", "c_b_i", "c_lam", "c_w_out")
SHARD_DIM = {"w_xq": 1, "w_xkv": 2, "w_xo": 1, "w_ffn_gu": 2, "w_ffn_down": 1, "ab_w_in": 2, "ab_conv_w": 2,
             "ab_w_out": 1, "c_w_in": 2, "c_conv_w": 2, "c_conv_b": 1, "c_w_a": 2, "c_b_a": 2, "c_w_i": 2, "c_b_i": 2,
             "c_lam": 1, "c_w_out": 1}
COMMON_BIG = ("w_xq", "w_xkv", "w_xo", "w_ffn_gu", "w_ffn_down")
EVEN_BIG, ODD_BIG = ("ab_w_in", "ab_w_out"), ("c_w_in", "c_w_a", "c_w_i", "c_w_out")
BIG = COMMON_BIG + EVEN_BIG + ODD_BIG


def layer_big(layer):
    return COMMON_BIG + (ODD_BIG if layer % 2 else EVEN_BIG)


SPLIT_LAYERS = (0,)


FINE_REDUCE_LAYERS = (0, 1)


def reduce_chunk(layer, part):
    if layer not in FINE_REDUCE_LAYERS:
        return layer_big(layer) if part == "mix" else ()
    return {"ffn": ("w_ffn_gu", "w_ffn_down"), "rest": ("w_xq", "w_xkv", "w_xo"), "mixer": (),
            "mix": layer_big(layer)[len(COMMON_BIG):]}[part]


def chunk_names(layer, part):
    mixer = layer_big(layer)[len(COMMON_BIG):]
    if layer in SPLIT_LAYERS:
        return mixer if part == "mix" else COMMON_BIG
    return layer_big(layer) if part == "mix" else ()


SMALL_SHARDED = ("ab_conv_w", "c_conv_w", "c_conv_b", "c_b_a", "c_b_i", "c_lam")
REPLICATED = ("g_mix_pre", "g_mix_post", "g_cross_pre", "g_mem", "g_cross_post", "g_ffn_pre", "g_ffn_post", "ab_b_f")
PACK_COLS = 1024


def _unshard(g, d):
    shard = g.shape[1:]
    return jnp.moveaxis(g, 0, d).reshape(shard[:d] + (N_CHIPS * shard[d],) + shard[d + 1:])


def _shardify(full, d):
    s = full.shape
    return jnp.moveaxis(full.reshape(s[:d] + (N_CHIPS, s[d] // N_CHIPS) + s[d + 1:]), d, 0)


def _pack(arrays, rows):
    flat = jnp.concatenate([a.reshape(-1).astype(F32) for a in arrays])
    return jnp.pad(flat, (0, rows * PACK_COLS - flat.shape[0])).reshape(rows, PACK_COLS)


def _unpack(packed, shapes):
    flat = packed.reshape(-1)
    out, at = [], 0
    for s in shapes:
        size = math.prod(s)
        out.append(flat[at:at + size].reshape(s))
        at += size
    return out


def _rows_for(shapes):
    return -(-sum(math.prod(s) for s in shapes) // (8 * PACK_COLS)) * 8


def kernel(x, mem, g_mix_pre, g_mix_post, g_cross_pre, g_mem, g_cross_post, g_ffn_pre, g_ffn_post, w_xq, w_xkv, w_xo, w_ffn_gu, w_ffn_down, ab_w_in, ab_b_f, ab_conv_w, ab_w_out, c_w_in, c_conv_w, c_conv_b, c_w_a, c_b_a, c_w_i, c_b_i, c_lam, c_w_out, loss_target, m_g_mix_pre, m_g_mix_post, m_g_cross_pre, m_g_mem, m_g_cross_post, m_g_ffn_pre, m_g_ffn_post, m_w_xq, m_w_xkv, m_w_xo, m_w_ffn_gu, m_w_ffn_down, m_ab_w_in, m_ab_b_f, m_ab_conv_w, m_ab_w_out, m_c_w_in, m_c_conv_w, m_c_conv_b, m_c_w_a, m_c_b_a, m_c_w_i, m_c_b_i, m_c_lam, m_c_w_out, v_g_mix_pre, v_g_mix_post, v_g_cross_pre, v_g_mem, v_g_cross_post, v_g_ffn_pre, v_g_ffn_post, v_w_xq, v_w_xkv, v_w_xo, v_w_ffn_gu, v_w_ffn_down, v_ab_w_in, v_ab_b_f, v_ab_conv_w, v_ab_w_out, v_c_w_in, v_c_conv_w, v_c_conv_b, v_c_w_a, v_c_b_a, v_c_w_i, v_c_b_i, v_c_lam, v_c_w_out):
    given = dict(locals())
    w = {n: given[n] for n in WEIGHTS}
    m_in = {n: given["m_" + n] for n in WEIGHTS}
    v_in = {n: given["v_" + n] for n in WEIGHTS}
    xi, yi, ci = _place()
    chip = 2 * xi + yi

    full = {}
    small_shapes = [w[n].shape for n in SMALL_SHARDED]
    rows_w = _rows_for(small_shapes)
    assert rows_w * PACK_COLS > sum(math.prod(s) for s in small_shapes)
    every = gather_small("gather_small_weights", _pack([w[n] for n in SMALL_SHARDED], rows_w), reduce=False)
    per_chip = every[0::2].reshape(N_CHIPS, -1)
    at = 0
    for n, s in zip(SMALL_SHARDED, small_shapes):
        size = math.prod(s)
        full[n] = _unshard(per_chip[:, at:at + size].reshape(N_CHIPS, *s), SHARD_DIM[n])
        at += size
    for n in REPLICATED:
        full[n] = w[n]
    after_small = every[0, -1, -1].astype(BF16)

    depth = g_mix_pre.shape[0]
    own, gathers, tokens = {}, {}, []
    for layer in range(depth):
        for part in ("mix", "rest"):
            names = chunk_names(layer, part)
            if names:
                tagp = f"l{layer}_{part}"
                own[tagp] = {n: w[n][layer if n in COMMON_BIG else layer // 2].astype(BF16) + after_small for n in names}
                halves = [a.reshape(2, -1, a.shape[-1]) for a in own[tagp].values()]
                gathers[tagp], token = copies_start(f"gather_start_{tagp}", _gather_copies, halves,
                                                    [(N_CHIPS, *h.shape) for h in halves])
                tokens.append(token[0, 0])

    def layer_weights(layer, part, x_in):
        tagp = f"l{layer}_{part}"
        if tagp not in gathers:
            return {}
        shards, lands = copies_wait(f"gather_wait_{tagp}", _gather_copies, gathers[tagp], x_in)
        lands = pass_to_sibling(f"gather_pass_{tagp}", shards, lands)
        return {n: _unshard(g.reshape(N_CHIPS, *mine.shape), SHARD_DIM[n] - 1)
                for (n, mine), g in zip(own[tagp].items(), lands)}

    core_arr = ci.reshape(1).astype(jnp.int32)
    place_arr = jnp.stack([chip, ci]).astype(jnp.int32)
    in_flight, swapping, held = [], [], {}

    def exchange(after):
        layer, tagp, names, state = swapping.pop()
        mine, got = copies_wait(f"swap_wait_{tagp}", _swap_copies, state, after)
        sums = [pair_sum(f"pair_sum_{tagp}_{n}", o, g, core_arr) for n, o, g in zip(names, mine, got)]
        state, token = copies_start(f"exchange_start_{tagp}", _exchange_copies, sums, [b.shape for b in sums])
        in_flight.append((layer, tagp, names, state))
        return token

    def reduce_hook(layer, part, part_grads, after):
        token = exchange(after)[0, 0] if swapping else None
        held.update(part_grads)
        names = reduce_chunk(layer, part)
        if names:
            tagp = f"l{layer}_{part}"
            mine = [held.pop(n) for n in names]
            state, started = copies_start(f"swap_start_{tagp}", _swap_copies, mine,
                                          [(m.shape[0], *m.shape[2:]) for m in mine])
            swapping.append((layer, tagp, names, state))
            token = started[0, 0] if token is None else token + started[0, 0]
        return token

    sq_cols, dx, grads, _ = local_step(x[0] + sum(tokens), mem[0], loss_target[0], full, layer_weights, reduce_hook)
    exchange(dx)
    loss = lax.psum(0.5 / D_MODEL * jnp.sum(sq_cols), ("x", "y", "c"))

    reduced = {n: lax.empty((w[n].shape[0], 2, math.prod(w[n].shape[1:-1]) // 2, w[n].shape[-1]), F32) for n in BIG}
    for layer, tagp, names, state in in_flight:
        sums, parts = copies_wait(f"exchange_wait_{tagp}", _exchange_copies, state, dx)
        for n, mine, p in zip(names, sums, parts):
            index = layer if n in COMMON_BIG else layer // 2
            reduced[n] = chip_sum(f"chip_sum_{tagp}_{n}", mine, p, place_arr, reduced[n], index)
    shared, after_big = share_halves([reduced[n] for n in BIG])
    grad_out = {n: g.reshape(w[n].shape) for n, g in zip(BIG, shared)}

    small_names = REPLICATED + SMALL_SHARDED
    small_full_shapes = [grads[n].shape for n in small_names]
    total = gather_small("reduce_small_grads",
                         _pack([grads[n] for n in small_names], _rows_for(small_full_shapes)) + after_big[0, 0],
                         reduce=True)
    for n, g in zip(small_names, _unpack(total, small_full_shapes)):
        if n in SHARD_DIM:
            g = lax.dynamic_index_in_dim(_shardify(g, SHARD_DIM[n]), chip, axis=0, keepdims=False)
        grad_out[n] = g

    delta, new_m, new_v = {}, {}, {}
    for n in BIG:
        two_d = lambda a: a.reshape(-1, a.shape[-1])
        results = adamw(f"adamw_{n}", two_d(w[n]), two_d(grad_out[n]), two_d(m_in[n]), two_d(v_in[n]))
        delta[n], new_m[n], new_v[n], grad_out[n] = (a.reshape(w[n].shape) for a in results)
    shapes = [w[n].shape for n in small_names]
    rows = _rows_for(shapes)
    packed = [_pack([src[n] for n in small_names], rows) for src in (w, grad_out, m_in, v_in)]
    for dst, res in zip((delta, new_m, new_v), adamw("adamw_small", *packed)[:3]):
        for n, a in zip(small_names, _unpack(res, shapes)):
            dst[n] = a

    return (loss, dx[None], *[grad_out[n] for n in WEIGHTS], *[delta[n] for n in WEIGHTS],
            *[new_m[n] for n in WEIGHTS], *[new_v[n] for n in WEIGHTS])
```

```python
import functools
import math

import jax
import jax.numpy as jnp
from jax import lax
from jax.experimental import pallas as pl
from jax.experimental.pallas import tpu as pltpu

F32, BF16 = jnp.float32, jnp.bfloat16
D_MODEL = 1024
EPS = 1e-6
NEG_INF = -1e30
FOX_HEADS, FOX_HEAD_DIM, FOX_WIDTH = 8, 64, 512
SC_WIDTH = 512
AB_IN = 3 * FOX_WIDTH + FOX_HEADS + 3 * SC_WIDTH
AB_IN_PAD = 3200
LRU_BW, LRU_BLOCKS = 256, 4
RG_C = 8.0
MEM_HEADS, MEM_HEAD_DIM = 4, 256
ADAM_LR, ADAM_B1, ADAM_B2, ADAM_EPS, ADAM_WD, ADAM_STEP = 0.001, 0.9, 0.999, 1e-08, 0.01, 10
N_CHIPS = 4
MESH = pl.DeviceIdType.MESH
VMEM_LIMIT_BYTES = 48 * 1024 * 1024
MM_OPERAND_TILE_BYTES = 7 * 1024 * 1024

NN = (((1,), (0,)), ((), ()))
NT = (((1,), (1,)), ((), ()))
TN = (((0,), (0,)), ((), ()))


def _dot(a, b, dn=NN):
    return lax.dot_general(a.astype(BF16), b.astype(BF16), dn, preferred_element_type=F32)


def _tile(n, prefs):
    for p in prefs:
        if n % p == 0:
            return p
    return n


def _pcall(name, kern, grid, ins, in_specs, out_shape, out_specs, sem):
    return pl.pallas_call(
        kern, name=name, grid=grid, in_specs=in_specs, out_specs=out_specs, out_shape=out_shape,
        compiler_params=pltpu.CompilerParams(dimension_semantics=sem, vmem_limit_bytes=VMEM_LIMIT_BYTES),
    )(*ins)


def mm(name, a, b, mode, out_dtype, reduce_layout=False):
    if mode == "nn":
        (m, k), n = a.shape, b.shape[1]
    elif mode == "nt":
        (m, k), n = a.shape, b.shape[0]
    else:
        (k, m), n = a.shape, b.shape[1]
    if reduce_layout:
        tm, tn = m // 2, n // N_CHIPS
    else:
        tn = _tile(n, ((1024,) if mode == "tn" else ()) + (512, 640, 256, 128))
        tm = next(c for c in (2048, 1024, 512, 256, 128, m)
                  if m % c == 0 and 2 * c * k <= MM_OPERAND_TILE_BYTES and 4 * c * tn <= MM_OPERAND_TILE_BYTES)
    dn = {"nn": NN, "nt": NT, "tn": TN}[mode]

    def kern(a_ref, b_ref, o_ref):
        o_ref[...] = _dot(a_ref[...], b_ref[...], dn).astype(o_ref.dtype)

    a_spec = pl.BlockSpec((k, tm), lambda i, j: (0, i)) if mode == "tn" else pl.BlockSpec((tm, k), lambda i, j: (i, 0))
    b_spec = pl.BlockSpec((tn, k), lambda i, j: (j, 0)) if mode == "nt" else pl.BlockSpec((k, tn), lambda i, j: (0, j))
    if reduce_layout:
        out_shape = jax.ShapeDtypeStruct((N_CHIPS, 2, tm, tn), out_dtype)
        o_spec = pl.BlockSpec((None, None, tm, tn), lambda i, j: (j, i, 0, 0))
    else:
        out_shape = jax.ShapeDtypeStruct((m, n), out_dtype)
        o_spec = pl.BlockSpec((tm, tn), lambda i, j: (i, j))
    return _pcall(name, kern, (m // tm, n // tn), (a, b), [a_spec, b_spec], out_shape, o_spec, ("parallel", "parallel"))


def rowwise(name, body, rows, params, outs, accs=(), tr=256):
    t = rows[0].shape[0]
    tr = min(tr, t)
    nr, npar, no = len(rows), len(params), len(outs)

    def kern(*refs):
        acc_refs = refs[nr + npar + no:]
        if acc_refs:
            @pl.when(pl.program_id(0) == 0)
            def _():
                for ar in acc_refs:
                    ar[...] = jnp.zeros_like(ar)
        body(refs[:nr], refs[nr:nr + npar], refs[nr + npar:nr + npar + no], acc_refs)

    in_specs = [pl.BlockSpec((tr, x.shape[1]), lambda i: (i, 0)) for x in rows]
    in_specs += [pl.BlockSpec(p.shape, lambda i: (0, 0)) for p in params]
    out_specs = [pl.BlockSpec((tr, c), lambda i: (i, 0)) for c, _ in outs]
    out_specs += [pl.BlockSpec(s, lambda i: (0, 0)) for s in accs]
    out_shape = [jax.ShapeDtypeStruct((t, c), dt) for c, dt in outs]
    out_shape += [jax.ShapeDtypeStruct(s, F32) for s in accs]
    return _pcall(name, kern, (t // tr,), (*rows, *params), in_specs, out_shape, out_specs,
                  ("arbitrary",) if accs else ("parallel",))


def _rms_stats(x):
    x = x.astype(F32)
    r = lax.rsqrt(jnp.mean(x * x, axis=-1, keepdims=True) + EPS)
    return r, x * r


def _rms_bwd(xh, r, g, dy):
    dxh = dy * g
    dx = r * (dxh - xh * jnp.mean(dxh * xh, axis=-1, keepdims=True))
    return dx, jnp.sum(dy * xh, axis=0, keepdims=True)


def rms_pre(name, x, gains, layer):
    def body(r, p, o, a):
        _, xh = _rms_stats(r[0][...])
        o[0][...] = (xh * p[0][layer:layer + 1, :]).astype(BF16)
    return rowwise(name, body, [x], [gains], [(x.shape[1], BF16)])[0]


def post_add(name, x, y, gains, layer):
    def body(r, p, o, a):
        _, yh = _rms_stats(r[1][...])
        o[0][...] = r[0][...] + yh * p[0][layer:layer + 1, :]
    return rowwise(name, body, [x, y], [gains], [(x.shape[1], F32)])[0]


def post_bwd(name, y, dx, gains, layer):
    def body(r, p, o, a):
        rr, yh = _rms_stats(r[0][...])
        dy, dg = _rms_bwd(yh, rr, p[0][layer:layer + 1, :], r[1][...])
        o[0][...] = dy.astype(BF16)
        a[0][...] += dg
    c = y.shape[1]
    return rowwise(name, body, [y, dx], [gains], [(c, BF16)], [(1, c)])


def pre_bwd(name, x, dh, dx_res, gains, layer):
    def body(r, p, o, a):
        rr, xh = _rms_stats(r[0][...])
        dx, dg = _rms_bwd(xh, rr, p[0][layer:layer + 1, :], r[1][...].astype(F32))
        o[0][...] = r[2][...] + dx
        a[0][...] += dg
    c = x.shape[1]
    return rowwise(name, body, [x, dh, dx_res], [gains], [(c, F32)], [(1, c)])


def post_add_pre(name, x, y, gains_post, layer_post, gains_pre, layer_pre):
    def body(r, p, o, a):
        _, yh = _rms_stats(r[1][...])
        x_new = r[0][...] + yh * p[0][layer_post:layer_post + 1, :]
        o[0][...] = x_new
        _, xh = _rms_stats(x_new)
        o[1][...] = (xh * p[1][layer_pre:layer_pre + 1, :]).astype(BF16)
    c = x.shape[1]
    return rowwise(name, body, [x, y], [gains_post, gains_pre], [(c, F32), (c, BF16)])


def pre_post_bwd(name, x, dh, dx_res, gains_pre, layer_pre, y, gains_post, layer_post):
    def body(r, p, o, a):
        rr, xh = _rms_stats(r[0][...])
        dx_norm, dg_pre = _rms_bwd(xh, rr, p[0][layer_pre:layer_pre + 1, :], r[1][...].astype(F32))
        dx = r[2][...] + dx_norm
        o[0][...] = dx
        a[0][...] += dg_pre
        ry, yh = _rms_stats(r[3][...])
        dy, dg_post = _rms_bwd(yh, ry, p[1][layer_post:layer_post + 1, :], dx)
        o[1][...] = dy.astype(BF16)
        a[1][...] += dg_post
    c = x.shape[1]
    return rowwise(name, body, [x, dh, dx_res, y], [gains_pre, gains_post], [(c, F32), (c, BF16)], [(1, c), (1, c)])


def gain_bwd(name, x, dh):
    def body(r, p, o, a):
        _, xh = _rms_stats(r[0][...])
        a[0][...] += jnp.sum(r[1][...] * xh, axis=0, keepdims=True)
    return rowwise(name, body, [x, dh], [], [], [(1, x.shape[1])])[0]


def _sigmoid(z):
    return 1.0 / (1.0 + jnp.exp(-z))


def swiglu_fwd(name, gu):
    f = gu.shape[1] // 2

    def body(r, p, o, a):
        g = r[0][:, :f].astype(F32)
        u = r[0][:, f:].astype(F32)
        o[0][...] = (g * _sigmoid(g) * u).astype(BF16)
    return rowwise(name, body, [gu], [], [(f, BF16)])[0]


def swiglu_bwd(name, gu, da):
    f = gu.shape[1] // 2

    def body(r, p, o, a):
        g = r[0][:, :f].astype(F32)
        u = r[0][:, f:].astype(F32)
        d = r[1][...].astype(F32)
        sg = _sigmoid(g)
        o[0][:, :f] = (d * u * sg * (1.0 + g * (1.0 - sg))).astype(BF16)
        o[0][:, f:] = (d * g * sg).astype(BF16)
    return rowwise(name, body, [gu, da], [], [(2 * f, BF16)])[0]


def loss_head(name, y, target):
    c = y.shape[1]

    def body(r, p, o, a):
        e = r[0][...] - r[1][...]
        o[0][...] = e * (1.0 / c)
        a[0][...] += jnp.sum(e * e, axis=0, keepdims=True)
    return rowwise(name, body, [y, target], [], [(c, F32)], [(1, c)])


def adamw(name, w, g, m, v):
    c = w.shape[1]

    def body(r, p, o, a):
        wv, gv, mv, vv = (x[...] for x in r)
        m2 = ADAM_B1 * mv + (1.0 - ADAM_B1) * gv
        v2 = ADAM_B2 * vv + (1.0 - ADAM_B2) * (gv * gv)
        m_hat = m2 / (1.0 - ADAM_B1 ** ADAM_STEP)
        v_hat = v2 / (1.0 - ADAM_B2 ** ADAM_STEP)
        o[0][...] = -ADAM_LR * (m_hat / (jnp.sqrt(v_hat) + ADAM_EPS) + ADAM_WD * wv)
        o[1][...] = m2
        o[2][...] = v2
        o[3][...] = gv
    tr = _tile(w.shape[0], (256, 128, 64, 32, 16, 8))
    return rowwise(name, body, [w, g, m, v], [], [(c, F32)] * 4, tr=tr)


def colwise(name, body, cols, params, outs, pouts=(), tc=128):
    t = cols[0][0].shape[0]
    c = params[0].shape[1] if params else cols[0][0].shape[1]
    nc, npar, no = len(cols), len(params), len(outs)

    def kern(*refs):
        body(refs[:nc], refs[nc:nc + npar], refs[nc + npar:nc + npar + no], refs[nc + npar + no:])

    in_specs = [pl.BlockSpec((t, tc), functools.partial(lambda j, off: (0, j + off), off=off)) for _, off in cols]
    in_specs += [pl.BlockSpec((p.shape[0], tc), lambda j: (0, j)) for p in params]
    out_specs = [pl.BlockSpec((t, tc), lambda j: (0, j)) for _ in outs]
    out_specs += [pl.BlockSpec((r, tc), lambda j: (0, j)) for r in pouts]
    out_shape = [jax.ShapeDtypeStruct((t, c), dt) for dt in outs]
    out_shape += [jax.ShapeDtypeStruct((r, c), F32) for r in pouts]
    return _pcall(name, kern, (c // tc,), (*[x for x, _ in cols], *params), in_specs, out_shape, out_specs,
                  ("parallel",))


def _row_index(shape):
    return lax.broadcasted_iota(jnp.int32, shape, 0)


def _shift_down(x, d, rows):
    return jnp.where(rows >= d, pltpu.roll(x, d, 0), 0.0)


def _shift_up(x, d, rows):
    t = x.shape[0]
    return jnp.where(rows < t - d, pltpu.roll(x, t - d, 0), 0.0)


def sconv_fwd(name, proj, col0, conv_w, tc=128):
    nb = SC_WIDTH // tc

    def body(cl, p, o, po):
        b, c, u = (x[...] for x in cl)
        rows = _row_index(b.shape)
        w = p[0][...]
        z = c * u
        conv = w[2:3] * z + w[1:2] * _shift_down(z, 1, rows) + w[0:1] * _shift_down(z, 2, rows)
        o[0][...] = (b * conv).astype(BF16)
    return colwise(name, body, [(proj, col0), (proj, col0 + nb), (proj, col0 + 2 * nb)], [conv_w], [BF16], tc=tc)[0]


def sconv_bwd(name, proj, col0, conv_w, dyb, dcol0, tc=128):
    nb = SC_WIDTH // tc

    def body(cl, p, o, po):
        b, c, u, dy = (x[...] for x in cl)
        rows = _row_index(b.shape)
        w = p[0][...]
        z = c * u
        z1, z2 = _shift_down(z, 1, rows), _shift_down(z, 2, rows)
        conv = w[2:3] * z + w[1:2] * z1 + w[0:1] * z2
        dconv = dy * b
        dz = w[2:3] * dconv + w[1:2] * _shift_up(dconv, 1, rows) + w[0:1] * _shift_up(dconv, 2, rows)
        o[0][...] = (dy * conv).astype(BF16)
        o[1][...] = (dz * u).astype(BF16)
        o[2][...] = (dz * c).astype(BF16)
        po[0][0:1, :] = jnp.sum(dconv * z2, axis=0, keepdims=True)
        po[0][1:2, :] = jnp.sum(dconv * z1, axis=0, keepdims=True)
        po[0][2:3, :] = jnp.sum(dconv * z, axis=0, keepdims=True)
    return colwise(name, body, [(proj, col0), (proj, col0 + nb), (proj, col0 + 2 * nb), (dyb, dcol0)], [conv_w],
                   [BF16, BF16, BF16], [3], tc=tc)


def _expm1(x):
    series = x * (1.0 + 0.5 * x * (1.0 + x * (1.0 / 3.0) * (1.0 + 0.25 * x * (1.0 + 0.2 * x))))
    return jnp.where(jnp.abs(x) < 0.05, series, jnp.exp(x) - 1.0)


def _log1p(x):
    series = x * (1.0 - x * (0.5 - x * (1.0 / 3.0 - 0.25 * x)))
    return jnp.where(jnp.abs(x) < 0.01, series, jnp.log(1.0 + x))


def _softplus_neg(lam):
    sp = jnp.maximum(-lam, 0.0) + _log1p(jnp.exp(-jnp.abs(lam)))
    return sp, -_sigmoid(-lam)


GELU_C = math.sqrt(2.0 / math.pi)


def _gelu(x):
    th = jnp.tanh(GELU_C * (x + 0.044715 * x * x * x))
    val = 0.5 * x * (1.0 + th)
    grad = 0.5 * (1.0 + th) + 0.5 * x * (1.0 - th * th) * GELU_C * (1.0 + 3.0 * 0.044715 * x * x)
    return val, grad


def rg_conv_fwd(name, gu2, conv_w, conv_b, tc=128):
    nb = D_MODEL // tc

    def body(cl, p, o, po):
        u = cl[0][...]
        rows = _row_index(u.shape)
        w = p[0][...]
        o[0][...] = (w[3:4] * u + w[2:3] * _shift_down(u, 1, rows) + w[1:2] * _shift_down(u, 2, rows)
                     + w[0:1] * _shift_down(u, 3, rows) + p[1][...])
    return colwise(name, body, [(gu2, nb)], [conv_w, conv_b], [F32], tc=tc)[0]


def rg_conv_bwd(name, gu2, duc, conv_w, tc=128):
    nb = D_MODEL // tc

    def body(cl, p, o, po):
        u, d = cl[0][...], cl[1][...]
        rows = _row_index(u.shape)
        w = p[0][...]
        o[0][...] = (w[3:4] * d + w[2:3] * _shift_up(d, 1, rows) + w[1:2] * _shift_up(d, 2, rows)
                     + w[0:1] * _shift_up(d, 3, rows)).astype(BF16)
        for k in range(4):
            uk = u if k == 3 else _shift_down(u, 3 - k, rows)
            po[0][k:k + 1, :] = jnp.sum(d * uk, axis=0, keepdims=True)
        po[1][...] = jnp.sum(d, axis=0, keepdims=True)
    return colwise(name, body, [(gu2, nb), (duc, 0)], [conv_w], [BF16], [4, 1], tc=tc)


def rg_gates_fwd(name, uc, w_a, b_a, w_i, b_i, tr=512):
    t = uc.shape[0]
    tr = min(tr, t)

    def kern(u_ref, wa_ref, ba_ref, wi_ref, bi_ref, r_ref, i_ref):
        ub = u_ref[...].astype(BF16)
        r_ref[...] = _sigmoid(_dot(ub, wa_ref[...]) + ba_ref[...])
        i_ref[...] = _sigmoid(_dot(ub, wi_ref[...]) + bi_ref[...])

    blk = pl.BlockSpec((tr, LRU_BW), lambda n, i: (i, n))
    wspec = pl.BlockSpec((None, LRU_BW, LRU_BW), lambda n, i: (n, 0, 0))
    bspec = pl.BlockSpec((1, LRU_BW), lambda n, i: (0, n))
    return _pcall(name, kern, (LRU_BLOCKS, t // tr), (uc, w_a, b_a, w_i, b_i), [blk, wspec, bspec, wspec, bspec],
                  [jax.ShapeDtypeStruct(uc.shape, F32)] * 2, [blk, blk], ("parallel", "parallel"))


def rg_gates_bwd(name, uc, dzr, dzi, duc_part, w_a, w_i):
    t = uc.shape[0]
    rows = LRU_BW // N_CHIPS

    def kern(u_ref, dr_ref, di_ref, dp_ref, wa_ref, wi_ref, duc_ref, dwa_ref, dwi_ref):
        ub = u_ref[...].astype(BF16)
        dr, di = dr_ref[...], di_ref[...]
        dwa, dwi = _dot(ub, dr, TN), _dot(ub, di, TN)
        for p in range(N_CHIPS):
            dwa_ref[p] = dwa[p * rows:(p + 1) * rows].astype(dwa_ref.dtype)
            dwi_ref[p] = dwi[p * rows:(p + 1) * rows].astype(dwi_ref.dtype)
        duc_ref[...] = dp_ref[...] + _dot(dr, wa_ref[...], NT) + _dot(di, wi_ref[...], NT)

    blk = pl.BlockSpec((t, LRU_BW), lambda n: (0, n))
    wspec = pl.BlockSpec((None, LRU_BW, LRU_BW), lambda n: (n, 0, 0))
    gspec = pl.BlockSpec((N_CHIPS, None, rows, LRU_BW), lambda n: (0, n, 0, 0))
    gshape = jax.ShapeDtypeStruct((N_CHIPS, LRU_BLOCKS, rows, LRU_BW), BF16)
    return _pcall(name, kern, (LRU_BLOCKS,), (uc, dzr, dzi, duc_part, w_a, w_i), [blk, blk, blk, blk, wspec, wspec],
                  [jax.ShapeDtypeStruct(uc.shape, F32), gshape, gshape], [blk, gspec, gspec], ("parallel",))


def _rg_decay(r, lam):
    sp, dsp = _softplus_neg(lam)
    la = -RG_C * r * sp
    a = jnp.exp(la)
    sq = jnp.sqrt(-_expm1(2.0 * la))
    return sp, dsp, a, sq


def rg_scan_fwd(name, gu2, uc, r, i, lam, tc=128):
    def body(cl, p, o, po):
        gate, ucv, rv, iv = (x[...] for x in cl)
        t = gate.shape[0]
        rows = _row_index(gate.shape)
        _, _, a, sq = _rg_decay(rv, p[0][...])
        b = sq * (iv * ucv)
        d = 1
        while d < t:
            keep = rows >= d
            b = a * jnp.where(keep, pltpu.roll(b, d, 0), 0.0) + b
            a = a * jnp.where(keep, pltpu.roll(a, d, 0), 1.0)
            d *= 2
        o[0][...] = (_gelu(gate)[0] * b).astype(BF16)
        o[1][...] = b
    return colwise(name, body, [(gu2, 0), (uc, 0), (r, 0), (i, 0)], [lam], [BF16, F32], tc=tc)


def rg_scan_bwd(name, gu2, uc, r, i, hs, dy, lam, tc=128):
    def body(cl, p, o, po):
        gate, ucv, rv, iv, h, dyv = (x[...] for x in cl)
        t = gate.shape[0]
        rows = _row_index(gate.shape)
        sp, dsp, a, sq = _rg_decay(rv, p[0][...])
        gl, dgl = _gelu(gate)
        o[0][...] = (dyv * h * dgl).astype(BF16)
        g = dyv * gl
        am = _shift_up(a, 1, rows)
        d = 1
        while d < t:
            keep = rows < t - d
            g = am * jnp.where(keep, pltpu.roll(g, t - d, 0), 0.0) + g
            am = am * jnp.where(keep, pltpu.roll(am, t - d, 0), 0.0)
            d *= 2
        da = g * _shift_down(h, 1, rows)
        iu = iv * ucv
        d_iu = g * sq
        dla = da * a - (g * iu) * (a * a) / sq
        dzr = dla * (-RG_C * sp) * rv * (1.0 - rv)
        dzi = d_iu * ucv * iv * (1.0 - iv)
        o[1][...] = dzr.astype(BF16)
        o[2][...] = dzi.astype(BF16)
        o[3][...] = d_iu * iv
        po[0][...] = jnp.sum(dzr, axis=0, keepdims=True)
        po[1][...] = jnp.sum(dzi, axis=0, keepdims=True)
        po[2][...] = jnp.sum(dla * rv, axis=0, keepdims=True) * (-RG_C) * dsp
    return colwise(name, body, [(gu2, 0), (uc, 0), (r, 0), (i, 0), (hs, 0), (dy, 0)], [lam],
                   [BF16, BF16, BF16, F32], [1, 1, 1], tc=tc)


def _split3(x):
    hi = x.astype(BF16)
    r1 = x - hi.astype(F32)
    mid = r1.astype(BF16)
    lo = (r1 - mid.astype(F32)).astype(BF16)
    return hi, mid, lo


def _tri_dot(x, tri):
    out = None
    for piece in _split3(x):
        term = lax.dot_general(piece, tri, NN, preferred_element_type=F32)
        out = term if out is None else out + term
    return out


def fox_gates_fwd(name, z_t, b_f):
    h, t = z_t.shape
    tb = min(512, t)

    def kern(z_ref, b_ref, o_ref):
        z = z_ref[...] + b_ref[...]
        logf = jnp.minimum(z, 0.0) - _log1p(jnp.exp(-jnp.abs(z)))
        src = lax.broadcasted_iota(jnp.int32, (t, tb), 0)
        dst = lax.broadcasted_iota(jnp.int32, (t, tb), 1) + pl.program_id(0) * tb
        o_ref[...] = _tri_dot(logf, (src <= dst).astype(BF16))

    return _pcall(name, kern, (t // tb,), (z_t, b_f),
                  [pl.BlockSpec((h, t), lambda j: (0, 0)), pl.BlockSpec((h, 1), lambda j: (0, 0))],
                  jax.ShapeDtypeStruct((h, t), F32), pl.BlockSpec((h, tb), lambda j: (0, j)), ("parallel",))


def fox_gates_bwd(name, z_t, b_f, dcum_t):
    h, t = z_t.shape
    tb = min(512, t)

    def kern(z_ref, b_ref, d_ref, dz_ref, db_ref):
        @pl.when(pl.program_id(0) == 0)
        def _():
            db_ref[...] = jnp.zeros_like(db_ref)
        src = lax.broadcasted_iota(jnp.int32, (t, tb), 0)
        dst = lax.broadcasted_iota(jnp.int32, (t, tb), 1) + pl.program_id(0) * tb
        dlogf = _tri_dot(d_ref[...], (src >= dst).astype(BF16))
        z = z_ref[...] + b_ref[...]
        dz = dlogf * _sigmoid(-z)
        dz_ref[...] = dz
        db_ref[...] += jnp.sum(dz, axis=1, keepdims=True)

    return _pcall(name, kern, (t // tb,), (z_t, b_f, dcum_t),
                  [pl.BlockSpec((h, tb), lambda j: (0, j)), pl.BlockSpec((h, 1), lambda j: (0, 0)),
                   pl.BlockSpec((h, t), lambda j: (0, 0))],
                  [jax.ShapeDtypeStruct((h, t), F32), jax.ShapeDtypeStruct((h, 1), F32)],
                  [pl.BlockSpec((h, tb), lambda j: (0, j)), pl.BlockSpec((h, 1), lambda j: (0, 0))], ("arbitrary",))


def _fox_spans(qs, k_ref, cr_ref, i, tq):
    n0 = i * tq
    sd = _dot(qs, k_ref[n0:n0 + tq, :], NT) - cr_ref[:, n0:n0 + tq]
    row = lax.broadcasted_iota(jnp.int32, (tq, tq), 0)
    col = lax.broadcasted_iota(jnp.int32, (tq, tq), 1)
    spans = [(n0, tq, jnp.where(row >= col, sd, NEG_INF))]
    if i > 0:
        spans.append((0, n0, _dot(qs, k_ref[0:n0, :], NT) - cr_ref[:, 0:n0]))
    return spans


def fox_fwd(name, q, k, v, cum_r, tq=256):
    h, t, dh = q.shape
    tq = min(tq, t)
    scale = FOX_HEAD_DIM ** -0.5

    def kern(q_ref, k_ref, v_ref, cr_ref, o_ref, lse_ref):
        for i in range(t // tq):
            rows = slice(i * tq, (i + 1) * tq)
            spans = _fox_spans(q_ref[rows, :] * scale, k_ref, cr_ref, i, tq)
            m = functools.reduce(jnp.maximum, [jnp.max(s, axis=-1, keepdims=True) for _, _, s in spans])
            l, acc = 0.0, 0.0
            for k0, kn, s in spans:
                p = jnp.exp(s - m)
                l = l + jnp.sum(p, axis=-1, keepdims=True)
                acc = acc + _dot(p, v_ref[k0:k0 + kn, :])
            o_ref[rows, :] = (acc / l).astype(o_ref.dtype)
            lse_ref[rows, :] = m + jnp.log(l)

    hspec = pl.BlockSpec((None, t, dh), lambda a: (a, 0, 0))
    cspec = pl.BlockSpec((None, t, 1), lambda a: (a, 0, 0))
    rspec = pl.BlockSpec((None, 1, t), lambda a: (a, 0, 0))
    return _pcall(name, kern, (h,), (q, k, v, cum_r), [hspec, hspec, hspec, rspec],
                  [jax.ShapeDtypeStruct((h, t, dh), BF16), jax.ShapeDtypeStruct((h, t, 1), F32)],
                  [hspec, cspec], ("parallel",))


def fox_bwd(name, q, k, v, do, lse, cum_r, tq=256):
    h, t, dh = q.shape
    tq = min(tq, t)
    scale = FOX_HEAD_DIM ** -0.5

    def kern(q_ref, k_ref, v_ref, do_ref, lse_ref, cr_ref, dq_ref, dk_ref, dv_ref, dc_ref):
        dk_ref[...] = jnp.zeros_like(dk_ref)
        dv_ref[...] = jnp.zeros_like(dv_ref)
        dc_ref[...] = jnp.zeros_like(dc_ref)
        for i in range(t // tq):
            rows = slice(i * tq, (i + 1) * tq)
            qs, dov, lse_v = q_ref[rows, :] * scale, do_ref[rows, :], lse_ref[rows, :]
            spans = _fox_spans(qs, k_ref, cr_ref, i, tq)
            probs = [jnp.exp(s - lse_v) for _, _, s in spans]
            dps = [_dot(dov, v_ref[k0:k0 + kn, :], NT) for k0, kn, _ in spans]
            rowdot = sum(jnp.sum(dp * p, axis=-1, keepdims=True) for dp, p in zip(dps, probs))
            dq = 0.0
            for (k0, kn, _), p, dp in zip(spans, probs, dps):
                ds = p * (dp - rowdot)
                dq = dq + _dot(ds, k_ref[k0:k0 + kn, :])
                dk_ref[k0:k0 + kn, :] += _dot(ds, qs, TN)
                dv_ref[k0:k0 + kn, :] += _dot(p, dov, TN)
                dc_ref[:, k0:k0 + kn] -= jnp.sum(ds, axis=0, keepdims=True)
            dq_ref[rows, :] = (dq * scale).astype(dq_ref.dtype)

    hspec = pl.BlockSpec((None, t, dh), lambda a: (a, 0, 0))
    cspec = pl.BlockSpec((None, t, 1), lambda a: (a, 0, 0))
    rspec = pl.BlockSpec((None, 1, t), lambda a: (a, 0, 0))
    return _pcall(name, kern, (h,), (q, k, v, do, lse, cum_r), [hspec, hspec, hspec, hspec, cspec, rspec],
                  [jax.ShapeDtypeStruct((h, t, dh), BF16), jax.ShapeDtypeStruct((h, t, dh), F32),
                   jax.ShapeDtypeStruct((h, t, dh), F32), jax.ShapeDtypeStruct((h, 1, t), F32)],
                  [hspec, hspec, hspec, rspec], ("parallel",))


def _xattn_probs(q, k):
    s = _dot(q, k, NT) * (MEM_HEAD_DIM ** -0.5)
    p = jnp.exp(s - jnp.max(s, axis=-1, keepdims=True))
    return p / jnp.sum(p, axis=-1, keepdims=True)


def xattn_fwd(name, q, kv, tq=512):
    t = q.shape[0]
    tq = min(tq, t)
    ml = kv.shape[0]

    def kern(q_ref, k_ref, v_ref, o_ref):
        o_ref[...] = _dot(_xattn_probs(q_ref[...], k_ref[...]), v_ref[...]).astype(o_ref.dtype)

    qspec = pl.BlockSpec((tq, MEM_HEAD_DIM), lambda i, a: (i, a))
    return _pcall(name, kern, (t // tq, MEM_HEADS), (q, kv, kv),
                  [qspec, pl.BlockSpec((ml, MEM_HEAD_DIM), lambda i, a: (0, a)),
                   pl.BlockSpec((ml, MEM_HEAD_DIM), lambda i, a: (0, MEM_HEADS + a))],
                  jax.ShapeDtypeStruct(q.shape, BF16), qspec, ("parallel", "parallel"))


def xattn_bwd(name, q, kv, do, tq=512):
    t = q.shape[0]
    tq = min(tq, t)
    ml = kv.shape[0]
    scale = MEM_HEAD_DIM ** -0.5

    def kern(q_ref, k_ref, v_ref, do_ref, dq_ref, dk_ref, dv_ref):
        @pl.when(pl.program_id(1) == 0)
        def _():
            dk_ref[...] = jnp.zeros_like(dk_ref)
            dv_ref[...] = jnp.zeros_like(dv_ref)
        qv, kv_, dov = q_ref[...], k_ref[...], do_ref[...]
        p = _xattn_probs(qv, kv_)
        dp = _dot(dov, v_ref[...], NT)
        ds = p * (dp - jnp.sum(dp * p, axis=-1, keepdims=True)) * scale
        dq_ref[...] = _dot(ds, kv_).astype(dq_ref.dtype)
        dk_ref[...] += _dot(ds, qv, TN)
        dv_ref[...] += _dot(p, dov, TN)

    qspec = pl.BlockSpec((tq, MEM_HEAD_DIM), lambda a, i: (i, a))
    kspec = pl.BlockSpec((ml, MEM_HEAD_DIM), lambda a, i: (0, a))
    return _pcall(name, kern, (MEM_HEADS, t // tq), (q, kv, kv, do),
                  [qspec, kspec, pl.BlockSpec((ml, MEM_HEAD_DIM), lambda a, i: (0, MEM_HEADS + a)), qspec],
                  [jax.ShapeDtypeStruct(q.shape, BF16), jax.ShapeDtypeStruct((ml, D_MODEL), F32),
                   jax.ShapeDtypeStruct((ml, D_MODEL), F32)],
                  [qspec, kspec, kspec], ("parallel", "arbitrary"))


def _heads(x):
    t = x.shape[0]
    return x.reshape(t, FOX_HEADS, FOX_HEAD_DIM).transpose(1, 0, 2)


def _unheads(x):
    return x.transpose(1, 0, 2).reshape(x.shape[1], FOX_WIDTH)


def _row_cut(dw):
    return dw.reshape(N_CHIPS, 2, dw.shape[0] // (2 * N_CHIPS), dw.shape[1])


def local_step(x, mem, target, w, layer_weights=None, reduce_hook=None):
    depth = w["g_mix_pre"].shape[0]
    t = x.shape[0]
    saved = []
    i1, i2, i3 = 3 * FOX_WIDTH, 3 * FOX_WIDTH + FOX_HEADS, AB_IN
    ncol = 128

    def stacked_weights(layer, part, _):
        names = COMMON_BIG if part == "rest" else layer_big(layer)[len(COMMON_BIG):]
        return {n: w[n][layer if n in COMMON_BIG else layer // 2] for n in names}

    get_weights = layer_weights or stacked_weights
    h1 = rms_pre("l0_mix_pre", x, w["g_mix_pre"], 0)
    for layer in range(depth):
        lw = dict(get_weights(layer, "mix", x))
        s = {"x0": x, "lw": lw}
        tag = f"l{layer}"
        s["h1"] = h1
        if layer % 2 == 0:
            e = layer // 2
            w_in = jnp.pad(lw["ab_w_in"], ((0, 0), (0, AB_IN_PAD - AB_IN)))
            proj = mm(f"{tag}_ab_in", h1, w_in, "nn", F32)
            qkv = proj[:, :i1].astype(BF16).reshape(t, 3, FOX_HEADS, FOX_HEAD_DIM).transpose(1, 2, 0, 3)
            z_t = proj[:, i1:i2].T
            b_f = w["ab_b_f"][e].reshape(FOX_HEADS, 1)
            cum_t = fox_gates_fwd(f"{tag}_fox_gates", z_t, b_f)
            cum_r = cum_t[:, None, :]
            oh, lse = fox_fwd(f"{tag}_fox", qkv[0], qkv[1], qkv[2], cum_r)
            bcu = proj[:, i2:i3]
            y_b = sconv_fwd(f"{tag}_sconv", bcu, 0, w["ab_conv_w"][e])
            ycat = jnp.concatenate([_unheads(oh), y_b], axis=1)
            y1 = mm(f"{tag}_ab_out", ycat, lw["ab_w_out"], "nn", BF16)
            s.update(w_in=w_in, qkv=qkv, z_t=z_t, b_f=b_f, cum_r=cum_r, lse=lse, bcu=bcu, ycat=ycat)
        else:
            o = layer // 2
            gu2 = mm(f"{tag}_c_in", h1, lw["c_w_in"], "nn", F32)
            conv_b = w["c_conv_b"][o].reshape(1, -1)
            uc = rg_conv_fwd(f"{tag}_rg_conv", gu2, w["c_conv_w"][o], conv_b)
            b_a, b_i = w["c_b_a"][o].reshape(1, -1), w["c_b_i"][o].reshape(1, -1)
            r, i = rg_gates_fwd(f"{tag}_rg_gates", uc, lw["c_w_a"], b_a, lw["c_w_i"], b_i)
            lam = w["c_lam"][o].reshape(1, -1)
            ymix, hs = rg_scan_fwd(f"{tag}_rg_scan", gu2, uc, r, i, lam)
            y1 = mm(f"{tag}_c_out", ymix, lw["c_w_out"], "nn", BF16)
            s.update(gu2=gu2, uc=uc, r=r, i=i, lam=lam, hs=hs, ymix=ymix)
        s["y1"] = y1
        x, h2 = post_add_pre(f"{tag}_mix_post", x, y1, w["g_mix_post"], layer, w["g_cross_pre"], layer)
        lw.update(get_weights(layer, "rest", x))
        s["x1"] = x
        m = rms_pre(f"{tag}_mem_pre", mem, w["g_mem"], layer)
        q = mm(f"{tag}_xq", h2, lw["w_xq"], "nn", BF16)
        kv = mm(f"{tag}_xkv", m, lw["w_xkv"], "nn", BF16)
        o_att = xattn_fwd(f"{tag}_xattn", q, kv)
        y2 = mm(f"{tag}_xo", o_att, lw["w_xo"], "nn", BF16)
        s.update(h2=h2, m=m, q=q, kv=kv, o_att=o_att, y2=y2)
        x, h3 = post_add_pre(f"{tag}_cross_post", x, y2, w["g_cross_post"], layer, w["g_ffn_pre"], layer)
        s["x2"] = x
        gu = mm(f"{tag}_ffn_gu", h3, lw["w_ffn_gu"], "nn", BF16)
        act = swiglu_fwd(f"{tag}_swiglu", gu)
        y3 = mm(f"{tag}_ffn_down", act, lw["w_ffn_down"], "nn", BF16)
        s.update(h3=h3, gu=gu, act=act, y3=y3)
        if layer + 1 < depth:
            x, h1 = post_add_pre(f"{tag}_ffn_post", x, y3, w["g_ffn_post"], layer, w["g_mix_pre"], layer + 1)
        else:
            x = post_add(f"{tag}_ffn_post", x, y3, w["g_ffn_post"], layer)
        saved.append(s)

    dx, sq_cols = loss_head("loss_head", x, target)

    grads = {k: [None] * v.shape[0] for k, v in w.items() if k not in BIG}
    big = {}

    def dw(name, a, b, cols_cut=False):
        return mm(name, a, b, "tn", BF16, reduce_layout=True) if cols_cut else _row_cut(mm(name, a, b, "tn", BF16))

    def hook(layer, part, part_grads, after, gains):
        token = None if reduce_hook is None else reduce_hook(layer, part, part_grads, after)
        return gains if token is None else gains + token

    dy3, grads["g_ffn_post"][depth - 1] = post_bwd(f"b{depth - 1}_ffn_post", saved[-1]["y3"], dx, w["g_ffn_post"], depth - 1)
    for layer in reversed(range(depth)):
        s = saved[layer]
        lw = s["lw"]
        tag = f"b{layer}"
        lg = {}
        dact = mm(f"{tag}_ffn_down_dx", dy3, lw["w_ffn_down"], "nt", BF16)
        lg["w_ffn_down"] = dw(f"{tag}_ffn_down_dw", s["act"], dy3)
        dgu = swiglu_bwd(f"{tag}_swiglu", s["gu"], dact)
        dh3 = mm(f"{tag}_ffn_gu_dx", dgu, lw["w_ffn_gu"], "nt", BF16)
        lg["w_ffn_gu"] = dw(f"{tag}_ffn_gu_dw", s["h3"], dgu, cols_cut=True)
        g_ffn_pre = hook(layer, "ffn", lg, dh3, w["g_ffn_pre"])
        ffn_grads, lg = lg, {}
        dx, dy2, grads["g_ffn_pre"][layer], grads["g_cross_post"][layer] = pre_post_bwd(
            f"{tag}_ffn_pre", s["x2"], dh3, dx, g_ffn_pre, layer, s["y2"], w["g_cross_post"], layer)
        do = mm(f"{tag}_xo_dx", dy2, lw["w_xo"], "nt", BF16)
        lg["w_xo"] = dw(f"{tag}_xo_dw", s["o_att"], dy2)
        dq, dk, dv = xattn_bwd(f"{tag}_xattn", s["q"], s["kv"], do)
        dh2 = mm(f"{tag}_xq_dx", dq, lw["w_xq"], "nt", BF16)
        lg["w_xq"] = dw(f"{tag}_xq_dw", s["h2"], dq)
        dkv = jnp.concatenate([dk, dv], axis=1).astype(BF16)
        dm = mm(f"{tag}_xkv_dx", dkv, lw["w_xkv"], "nt", F32)
        lg["w_xkv"] = dw(f"{tag}_xkv_dw", s["m"], dkv, cols_cut=True)
        grads["g_mem"][layer] = gain_bwd(f"{tag}_mem_pre", mem, dm)
        g_cross_pre = hook(layer, "rest", lg, dh2, w["g_cross_pre"])
        dx, dy1, grads["g_cross_pre"][layer], grads["g_mix_post"][layer] = pre_post_bwd(
            f"{tag}_cross_pre", s["x1"], dh2, dx, g_cross_pre, layer, s["y1"], w["g_mix_post"], layer)
        rest_grads, lg = lg, {}
        if layer % 2 == 0:
            e = layer // 2
            dycat = mm(f"{tag}_ab_out_dx", dy1, lw["ab_w_out"], "nt", F32)
            g_mix_pre = hook(layer, "mixer", {}, dycat, w["g_mix_pre"])
            lg["ab_w_out"] = dw(f"{tag}_ab_out_dw", s["ycat"], dy1)
            do_h = _heads(dycat[:, :FOX_WIDTH].astype(BF16))
            qkv = s["qkv"]
            dqh, dkh, dvh, dcum = fox_bwd(f"{tag}_fox", qkv[0], qkv[1], qkv[2], do_h, s["lse"], s["cum_r"])
            dz_t, db_f = fox_gates_bwd(f"{tag}_fox_gates", s["z_t"], s["b_f"], dcum.reshape(FOX_HEADS, t))
            grads["ab_b_f"][e] = db_f.reshape(FOX_HEADS)
            db, dc, du, dconv_w = sconv_bwd(f"{tag}_sconv", s["bcu"], 0, w["ab_conv_w"][e], dycat, FOX_WIDTH // ncol)
            grads["ab_conv_w"][e] = dconv_w
            dproj = jnp.concatenate(
                [_unheads(dqh), _unheads(dkh).astype(BF16), _unheads(dvh).astype(BF16), dz_t.T.astype(BF16), db, dc, du,
                 jnp.zeros((t, AB_IN_PAD - AB_IN), BF16)], axis=1)
            dh1 = mm(f"{tag}_ab_in_dx", dproj, s["w_in"], "nt", BF16)
            dw_in = mm(f"{tag}_ab_in_dw", s["h1"], dproj, "tn", F32)[:, :AB_IN]
            lg["ab_w_in"] = dw_in.reshape(2, D_MODEL // 2, N_CHIPS, AB_IN // N_CHIPS).transpose(2, 0, 1, 3).astype(BF16)
        else:
            o = layer // 2
            dymix = mm(f"{tag}_c_out_dx", dy1, lw["c_w_out"], "nt", F32)
            g_mix_pre = hook(layer, "mixer", {}, dymix, w["g_mix_pre"])
            lg["c_w_out"] = dw(f"{tag}_c_out_dw", s["ymix"], dy1)
            dgate, dzr, dzi, duc_part, db_a, db_i, dlam = rg_scan_bwd(
                f"{tag}_rg_scan", s["gu2"], s["uc"], s["r"], s["i"], s["hs"], dymix, s["lam"])
            duc, dw_a, dw_i = rg_gates_bwd(f"{tag}_rg_gates", s["uc"], dzr, dzi, duc_part, lw["c_w_a"], lw["c_w_i"])
            lg["c_w_a"] = dw_a.reshape(N_CHIPS, 2, LRU_BW // 2, LRU_BW)
            lg["c_w_i"] = dw_i.reshape(N_CHIPS, 2, LRU_BW // 2, LRU_BW)
            du_raw, dconv_w, dconv_b = rg_conv_bwd(f"{tag}_rg_conv", s["gu2"], duc, w["c_conv_w"][o])
            grads["c_b_a"][o] = db_a.reshape(LRU_BLOCKS, LRU_BW)
            grads["c_b_i"][o] = db_i.reshape(LRU_BLOCKS, LRU_BW)
            grads["c_lam"][o] = dlam.reshape(-1)
            grads["c_conv_w"][o] = dconv_w
            grads["c_conv_b"][o] = dconv_b.reshape(-1)
            dgu2 = jnp.concatenate([dgate, du_raw], axis=1)
            dh1 = mm(f"{tag}_c_in_dx", dgu2, lw["c_w_in"], "nt", BF16)
            lg["c_w_in"] = dw(f"{tag}_c_in_dw", s["h1"], dgu2, cols_cut=True)
        if reduce_hook is None:
            big[layer] = {**ffn_grads, **rest_grads, **lg}
        g_mix_pre = hook(layer, "mix", lg, dh1, g_mix_pre)
        if layer > 0:
            dx, dy3, grads["g_mix_pre"][layer], grads["g_ffn_post"][layer - 1] = pre_post_bwd(
                f"{tag}_mix_pre", s["x0"], dh1, dx, g_mix_pre, layer, saved[layer - 1]["y3"], w["g_ffn_post"], layer - 1)
        else:
            dx, grads["g_mix_pre"][layer] = pre_bwd(f"{tag}_mix_pre", s["x0"], dh1, dx, g_mix_pre, layer)

    for k in list(grads):
        if k.startswith("g_"):
            grads[k] = [g.reshape(-1) for g in grads[k]]
        grads[k] = jnp.stack(grads[k])
    return sq_cols, dx, grads, big


CHIP_FLIPS = ((1, 0), (0, 1), (1, 1))
HBM_SPEC = pl.BlockSpec(memory_space=pltpu.HBM)
VMEM_SPEC = pl.BlockSpec(memory_space=pltpu.VMEM)


def _place():
    return lax.axis_index("x"), lax.axis_index("y"), lax.axis_index("c")


def _flip(v, f):
    return 1 - v if f else v


def _remote(src, dst, send_sem, recv_sem, target):
    return pltpu.make_async_remote_copy(src_ref=src, dst_ref=dst, send_sem=send_sem, recv_sem=recv_sem,
                                        device_id=target, device_id_type=MESH)


SEM_SPEC = pl.BlockSpec(memory_space=pltpu.SEMAPHORE)


def _swap_copies(srcs, lands, send_sems, recv_sems):
    x, y, c = _place()
    return [_remote(src.at[:, 1 - c], land, send_sems.at[len(CHIP_FLIPS) * a], recv_sems.at[len(CHIP_FLIPS) * a],
                    (x, y, 1 - c)) for a, (src, land) in enumerate(zip(srcs, lands))]


def _exchange_copies(srcs, lands, send_sems, recv_sems):
    x, y, c = _place()
    p = 2 * x + y
    cps = []
    for a, (src, land) in enumerate(zip(srcs, lands)):
        for k, (fx, fy) in enumerate(CHIP_FLIPS):
            qx, qy = _flip(x, fx), _flip(y, fy)
            sem = len(CHIP_FLIPS) * a + k
            cps.append(_remote(src.at[2 * qx + qy], land.at[p], send_sems.at[sem], recv_sems.at[sem], (qx, qy, c)))
    return cps


def _gather_copies(srcs, lands, send_sems, recv_sems):
    x, y, c = _place()
    p = 2 * x + y
    cps = []
    for a, (src, land) in enumerate(zip(srcs, lands)):
        for k, (fx, fy) in enumerate(CHIP_FLIPS):
            sem = len(CHIP_FLIPS) * a + k
            cps.append(_remote(src.at[c], land.at[p, c], send_sems.at[sem], recv_sems.at[sem],
                               (_flip(x, fx), _flip(y, fy), c)))
    return cps


def copies_start(name, make_copies, srcs, land_shapes):
    n = len(srcs)

    def body(*refs):
        for cp in make_copies(refs[:n], refs[n:2 * n], refs[2 * n], refs[2 * n + 1]):
            cp.start()
        refs[-1][...] = jnp.zeros_like(refs[-1])

    thru = [pltpu.HBM(b.shape, b.dtype) for b in srcs] + [pltpu.HBM(sh, b.dtype) for sh, b in zip(land_shapes, srcs)]
    outs = pl.pallas_call(
        body, name=name, in_specs=[HBM_SPEC] * (2 * n),
        out_shape=(pltpu.SemaphoreType.DMA((3 * n,)), pltpu.SemaphoreType.DMA((3 * n,)), *thru,
                   jax.ShapeDtypeStruct((8, 128), F32)),
        out_specs=(SEM_SPEC, SEM_SPEC, *[HBM_SPEC] * (2 * n), VMEM_SPEC),
        input_output_aliases={i: 2 + i for i in range(2 * n)},
        compiler_params=pltpu.CompilerParams(has_side_effects=pltpu.SideEffectType.DATAFLOW_SIDE_EFFECTING),
    )(*[pltpu.with_memory_space_constraint(b, pltpu.HBM) for b in srcs],
      *[pltpu.with_memory_space_constraint(lax.empty(sh, b.dtype), pltpu.HBM) for sh, b in zip(land_shapes, srcs)])
    return outs[:-1], outs[-1]


def copies_wait(name, make_copies, state, after):
    send_sems, recv_sems, *thru = state
    n = len(thru) // 2

    def body(*refs):
        for cp in make_copies(refs[:n], refs[n:2 * n], refs[2 * n], refs[2 * n + 1]):
            cp.wait_send()
            cp.wait_recv()

    outs = pl.pallas_call(
        body, name=name, in_specs=[HBM_SPEC] * (2 * n) + [SEM_SPEC, SEM_SPEC, pl.BlockSpec(memory_space=pl.ANY)],
        out_shape=tuple(pltpu.HBM(t.shape, t.dtype) for t in thru), out_specs=tuple([HBM_SPEC] * (2 * n)),
        input_output_aliases={i: i for i in range(2 * n)},
        compiler_params=pltpu.CompilerParams(has_side_effects=pltpu.SideEffectType.DATAFLOW_SIDE_EFFECTING),
    )(*thru, send_sems, recv_sems, after)
    return outs[:n], outs[n:]


def pass_to_sibling(name, shards, lands):
    n = len(lands)

    def body(*refs):
        own, ins, outs = refs[:n], refs[n:2 * n], refs[2 * n:3 * n]
        send_sems, recv_sems = refs[3 * n:]
        x, y, c = _place()
        sibling = (x, y, 1 - c)
        cps = []
        for a in range(n):
            for k, (fx, fy) in enumerate(CHIP_FLIPS):
                q = 2 * _flip(x, fx) + _flip(y, fy)
                cps.append(_remote(ins[a].at[q, c], outs[a].at[q, c], send_sems.at[a, k], recv_sems.at[a, k], sibling))
            cps.append(_remote(own[a], outs[a].at[2 * x + y], send_sems.at[a, 3], recv_sems.at[a, 3], sibling))
        for cp in cps:
            cp.start()
        for cp in cps:
            cp.wait()

    return pl.pallas_call(
        body, name=name, in_specs=[HBM_SPEC] * (2 * n), out_specs=[HBM_SPEC] * n,
        out_shape=[jax.ShapeDtypeStruct(b.shape, b.dtype) for b in lands],
        scratch_shapes=[pltpu.SemaphoreType.DMA((n, 4)), pltpu.SemaphoreType.DMA((n, 4))],
        input_output_aliases={n + i: i for i in range(n)},
    )(*shards, *lands)


def share_halves(bufs):
    n = len(bufs)

    def body(*refs):
        ins, outs, token = refs[:n], refs[n:2 * n], refs[2 * n]
        send_sems, recv_sems = refs[2 * n + 1:]
        x, y, c = _place()
        cps = [_remote(ins[a].at[:, c], outs[a].at[:, c], send_sems.at[a], recv_sems.at[a], (x, y, 1 - c))
               for a in range(n)]
        for cp in cps:
            cp.start()
        token[...] = jnp.zeros_like(token)
        for cp in cps:
            cp.wait()

    outs = pl.pallas_call(
        body, name="share_reduced_halves", in_specs=[HBM_SPEC] * n, out_specs=[HBM_SPEC] * n + [VMEM_SPEC],
        out_shape=[jax.ShapeDtypeStruct(b.shape, b.dtype) for b in bufs] + [jax.ShapeDtypeStruct((8, 128), F32)],
        scratch_shapes=[pltpu.SemaphoreType.DMA((n,)), pltpu.SemaphoreType.DMA((n,))],
        input_output_aliases={i: i for i in range(n)},
    )(*bufs)
    return outs[:n], outs[n]


DEVICE_FLIPS = tuple((fx, fy, fc) for fx in (0, 1) for fy in (0, 1) for fc in (0, 1))[1:]


def gather_small(name, v, reduce):
    r, cdim = v.shape
    n_dev = 8

    def body(v_ref, out_ref, *scratch):
        buf = scratch[0] if reduce else out_ref
        send_sems, recv_sems = scratch[-2:]
        x, y, c = _place()
        me = 4 * x + 2 * y + c
        buf[me] = v_ref[...]
        cps = []
        for k, (fx, fy, fc) in enumerate(DEVICE_FLIPS):
            cps.append(_remote(v_ref, buf.at[me], send_sems.at[k], recv_sems.at[k],
                               (_flip(x, fx), _flip(y, fy), _flip(c, fc))))
        for cp in cps:
            cp.start()
        for cp in cps:
            cp.wait()
        if reduce:
            total = buf[0]
            for d in range(1, n_dev):
                total = total + buf[d]
            out_ref[...] = total

    scratch = [pltpu.SemaphoreType.DMA((7,)), pltpu.SemaphoreType.DMA((7,))]
    if reduce:
        scratch = [pltpu.VMEM((n_dev, r, cdim), F32)] + scratch
    out_shape = jax.ShapeDtypeStruct((r, cdim) if reduce else (n_dev, r, cdim), F32)
    return pl.pallas_call(body, name=name, in_specs=[VMEM_SPEC], out_specs=VMEM_SPEC, out_shape=out_shape,
                          scratch_shapes=scratch)(v)


def pair_sum(name, own, got, core):
    _, hx, cols = got.shape
    tr = _tile(hx, (256, 128, 64, 32, 16))

    def kern(core_ref, a_ref, b_ref, o_ref):
        o_ref[...] = (a_ref[...].astype(F32) + b_ref[...].astype(F32)).astype(BF16)

    grid_spec = pltpu.PrefetchScalarGridSpec(
        num_scalar_prefetch=1, grid=(hx // tr,),
        in_specs=[pl.BlockSpec((N_CHIPS, None, tr, cols), lambda i, cr: (0, cr[0], i, 0)),
                  pl.BlockSpec((N_CHIPS, tr, cols), lambda i, cr: (0, i, 0))],
        out_specs=pl.BlockSpec((N_CHIPS, tr, cols), lambda i, cr: (0, i, 0)))
    return pl.pallas_call(
        kern, name=name, grid_spec=grid_spec, out_shape=jax.ShapeDtypeStruct(got.shape, BF16),
        compiler_params=pltpu.CompilerParams(dimension_semantics=("parallel",), vmem_limit_bytes=VMEM_LIMIT_BYTES),
    )(core, own, got)


def chip_sum(name, mine, parts, place, buf, layer):
    _, hx, yd = parts.shape
    tr = _tile(hx, (256, 128, 64, 32, 16))

    def kern(place_ref, m_ref, p_ref, _, o_ref):
        total = None
        for q in range(N_CHIPS):
            term = jnp.where(place_ref[0] == q, m_ref[...], p_ref[q]).astype(F32)
            total = term if total is None else total + term
        o_ref[...] = total

    grid_spec = pltpu.PrefetchScalarGridSpec(
        num_scalar_prefetch=1, grid=(hx // tr,),
        in_specs=[pl.BlockSpec((None, tr, yd), lambda i, pr: (pr[0], i, 0)),
                  pl.BlockSpec((N_CHIPS, tr, yd), lambda i, pr: (0, i, 0)),
                  pl.BlockSpec(memory_space=pl.ANY)],
        out_specs=pl.BlockSpec((None, None, tr, yd), lambda i, pr: (layer, pr[1], i, 0)))
    return pl.pallas_call(
        kern, name=name, grid_spec=grid_spec, out_shape=jax.ShapeDtypeStruct(buf.shape, buf.dtype),
        input_output_aliases={3: 0},
        compiler_params=pltpu.CompilerParams(dimension_semantics=("parallel",), vmem_limit_bytes=VMEM_LIMIT_BYTES),
    )(place, mine, parts, buf)


WEIGHTS = ("g_mix_pre", "g_mix_post", "g_cross_pre", "g_mem", "g_cross_post", "g_ffn_pre", "g_ffn_post", "w_xq", "w_xkv",
           "w_xo", "w_ffn_gu", "w_ffn_down", "ab_w_in", "ab_b_f", "ab_conv_w", "ab_w_out", "c_w_in", "c_conv_w",
           "c_conv_b", "c_w_a", "c_b_a", "c_w_i", "c_b_i", "c_lam", "c_w_out")
SHARD_DIM = {"w_xq": 1, "w_xkv": 2, "w_xo": 1, "w_ffn_gu": 2, "w_ffn_down": 1, "ab_w_in": 2, "ab_conv_w": 2,
             "ab_w_out": 1, "c_w_in": 2, "c_conv_w": 2, "c_conv_b": 1, "c_w_a": 2, "c_b_a": 2, "c_w_i": 2, "c_b_i": 2,
             "c_lam": 1, "c_w_out": 1}
COMMON_BIG = ("w_xq", "w_xkv", "w_xo", "w_ffn_gu", "w_ffn_down")
EVEN_BIG, ODD_BIG = ("ab_w_in", "ab_w_out"), ("c_w_in", "c_w_a", "c_w_i", "c_w_out")
BIG = COMMON_BIG + EVEN_BIG + ODD_BIG


def layer_big(layer):
    return COMMON_BIG + (ODD_BIG if layer % 2 else EVEN_BIG)


SPLIT_LAYERS = (0,)


FINE_REDUCE_LAYERS = (0, 1)


def reduce_chunk(layer, part):
    if layer not in FINE_REDUCE_LAYERS:
        return layer_big(layer) if part == "mix" else ()
    return {"ffn": ("w_ffn_gu", "w_ffn_down"), "rest": ("w_xq", "w_xkv", "w_xo"), "mixer": (),
            "mix": layer_big(layer)[len(COMMON_BIG):]}[part]


def chunk_names(layer, part):
    mixer = layer_big(layer)[len(COMMON_BIG):]
    if layer in SPLIT_LAYERS:
        return mixer if part == "mix" else COMMON_BIG
    return layer_big(layer) if part == "mix" else ()


SMALL_SHARDED = ("ab_conv_w", "c_conv_w", "c_conv_b", "c_b_a", "c_b_i", "c_lam")
REPLICATED = ("g_mix_pre", "g_mix_post", "g_cross_pre", "g_mem", "g_cross_post", "g_ffn_pre", "g_ffn_post", "ab_b_f")
PACK_COLS = 1024


def _unshard(g, d):
    shard = g.shape[1:]
    return jnp.moveaxis(g, 0, d).reshape(shard[:d] + (N_CHIPS * shard[d],) + shard[d + 1:])


def _shardify(full, d):
    s = full.shape
    return jnp.moveaxis(full.reshape(s[:d] + (N_CHIPS, s[d] // N_CHIPS) + s[d + 1:]), d, 0)


def _pack(arrays, rows):
    flat = jnp.concatenate([a.reshape(-1).astype(F32) for a in arrays])
    return jnp.pad(flat, (0, rows * PACK_COLS - flat.shape[0])).reshape(rows, PACK_COLS)


def _unpack(packed, shapes):
    flat = packed.reshape(-1)
    out, at = [], 0
    for s in shapes:
        size = math.prod(s)
        out.append(flat[at:at + size].reshape(s))
        at += size
    return out


def _rows_for(shapes):
    return -(-sum(math.prod(s) for s in shapes) // (8 * PACK_COLS)) * 8


def kernel(x, mem, g_mix_pre, g_mix_post, g_cross_pre, g_mem, g_cross_post, g_ffn_pre, g_ffn_post, w_xq, w_xkv, w_xo, w_ffn_gu, w_ffn_down, ab_w_in, ab_b_f, ab_conv_w, ab_w_out, c_w_in, c_conv_w, c_conv_b, c_w_a, c_b_a, c_w_i, c_b_i, c_lam, c_w_out, loss_target, m_g_mix_pre, m_g_mix_post, m_g_cross_pre, m_g_mem, m_g_cross_post, m_g_ffn_pre, m_g_ffn_post, m_w_xq, m_w_xkv, m_w_xo, m_w_ffn_gu, m_w_ffn_down, m_ab_w_in, m_ab_b_f, m_ab_conv_w, m_ab_w_out, m_c_w_in, m_c_conv_w, m_c_conv_b, m_c_w_a, m_c_b_a, m_c_w_i, m_c_b_i, m_c_lam, m_c_w_out, v_g_mix_pre, v_g_mix_post, v_g_cross_pre, v_g_mem, v_g_cross_post, v_g_ffn_pre, v_g_ffn_post, v_w_xq, v_w_xkv, v_w_xo, v_w_ffn_gu, v_w_ffn_down, v_ab_w_in, v_ab_b_f, v_ab_conv_w, v_ab_w_out, v_c_w_in, v_c_conv_w, v_c_conv_b, v_c_w_a, v_c_b_a, v_c_w_i, v_c_b_i, v_c_lam, v_c_w_out):
    given = dict(locals())
    w = {n: given[n] for n in WEIGHTS}
    m_in = {n: given["m_" + n] for n in WEIGHTS}
    v_in = {n: given["v_" + n] for n in WEIGHTS}
    xi, yi, ci = _place()
    chip = 2 * xi + yi

    full = {}
    small_shapes = [w[n].shape for n in SMALL_SHARDED]
    rows_w = _rows_for(small_shapes)
    assert rows_w * PACK_COLS > sum(math.prod(s) for s in small_shapes)
    every = gather_small("gather_small_weights", _pack([w[n] for n in SMALL_SHARDED], rows_w), reduce=False)
    per_chip = every[0::2].reshape(N_CHIPS, -1)
    at = 0
    for n, s in zip(SMALL_SHARDED, small_shapes):
        size = math.prod(s)
        full[n] = _unshard(per_chip[:, at:at + size].reshape(N_CHIPS, *s), SHARD_DIM[n])
        at += size
    for n in REPLICATED:
        full[n] = w[n]
    after_small = every[0, -1, -1].astype(BF16)

    depth = g_mix_pre.shape[0]
    own, gathers, tokens = {}, {}, []
    for layer in range(depth):
        for part in ("mix", "rest"):
            names = chunk_names(layer, part)
            if names:
                tagp = f"l{layer}_{part}"
                own[tagp] = {n: w[n][layer if n in COMMON_BIG else layer // 2].astype(BF16) + after_small for n in names}
                halves = [a.reshape(2, -1, a.shape[-1]) for a in own[tagp].values()]
                gathers[tagp], token = copies_start(f"gather_start_{tagp}", _gather_copies, halves,
                                                    [(N_CHIPS, *h.shape) for h in halves])
                tokens.append(token[0, 0])

    def layer_weights(layer, part, x_in):
        tagp = f"l{layer}_{part}"
        if tagp not in gathers:
            return {}
        shards, lands = copies_wait(f"gather_wait_{tagp}", _gather_copies, gathers[tagp], x_in)
        lands = pass_to_sibling(f"gather_pass_{tagp}", shards, lands)
        return {n: _unshard(g.reshape(N_CHIPS, *mine.shape), SHARD_DIM[n] - 1)
                for (n, mine), g in zip(own[tagp].items(), lands)}

    core_arr = ci.reshape(1).astype(jnp.int32)
    place_arr = jnp.stack([chip, ci]).astype(jnp.int32)
    in_flight, swapping, held = [], [], {}

    def exchange(after):
        layer, tagp, names, state = swapping.pop()
        mine, got = copies_wait(f"swap_wait_{tagp}", _swap_copies, state, after)
        sums = [pair_sum(f"pair_sum_{tagp}_{n}", o, g, core_arr) for n, o, g in zip(names, mine, got)]
        state, token = copies_start(f"exchange_start_{tagp}", _exchange_copies, sums, [b.shape for b in sums])
        in_flight.append((layer, tagp, names, state))
        return token

    def reduce_hook(layer, part, part_grads, after):
        token = exchange(after)[0, 0] if swapping else None
        held.update(part_grads)
        names = reduce_chunk(layer, part)
        if names:
            tagp = f"l{layer}_{part}"
            mine = [held.pop(n) for n in names]
            state, started = copies_start(f"swap_start_{tagp}", _swap_copies, mine,
                                          [(m.shape[0], *m.shape[2:]) for m in mine])
            swapping.append((layer, tagp, names, state))
            token = started[0, 0] if token is None else token + started[0, 0]
        return token

    sq_cols, dx, grads, _ = local_step(x[0] + sum(tokens), mem[0], loss_target[0], full, layer_weights, reduce_hook)
    exchange(dx)
    loss = lax.psum(0.5 / D_MODEL * jnp.sum(sq_cols), ("x", "y", "c"))

    reduced = {n: lax.empty((w[n].shape[0], 2, math.prod(w[n].shape[1:-1]) // 2, w[n].shape[-1]), F32) for n in BIG}
    for layer, tagp, names, state in in_flight:
        sums, parts = copies_wait(f"exchange_wait_{tagp}", _exchange_copies, state, dx)
        for n, mine, p in zip(names, sums, parts):
            index = layer if n in COMMON_BIG else layer // 2
            reduced[n] = chip_sum(f"chip_sum_{tagp}_{n}", mine, p, place_arr, reduced[n], index)
    shared, after_big = share_halves([reduced[n] for n in BIG])
    grad_out = {n: g.reshape(w[n].shape) for n, g in zip(BIG, shared)}

    small_names = REPLICATED + SMALL_SHARDED
    small_full_shapes = [grads[n].shape for n in small_names]
    total = gather_small("reduce_small_grads",
                         _pack([grads[n] for n in small_names], _rows_for(small_full_shapes)) + after_big[0, 0],
                         reduce=True)
    for n, g in zip(small_names, _unpack(total, small_full_shapes)):
        if n in SHARD_DIM:
            g = lax.dynamic_index_in_dim(_shardify(g, SHARD_DIM[n]), chip, axis=0, keepdims=False)
        grad_out[n] = g

    delta, new_m, new_v = {}, {}, {}
    for n in BIG:
        two_d = lambda a: a.reshape(-1, a.shape[-1])
        results = adamw(f"adamw_{n}", two_d(w[n]), two_d(grad_out[n]), two_d(m_in[n]), two_d(v_in[n]))
        delta[n], new_m[n], new_v[n], grad_out[n] = (a.reshape(w[n].shape) for a in results)
    shapes = [w[n].shape for n in small_names]
    rows = _rows_for(shapes)
    packed = [_pack([src[n] for n in small_names], rows) for src in (w, grad_out, m_in, v_in)]
    for dst, res in zip((delta, new_m, new_v), adamw("adamw_small", *packed)[:3]):
        for n, a in zip(small_names, _unpack(res, shapes)):
            dst[n] = a

    return (loss, dx[None], *[grad_out[n] for n in WEIGHTS], *[delta[n] for n in WEIGHTS],
            *[new_m[n] for n in WEIGHTS], *[new_v[n] for n in WEIGHTS])
```

```python
import functools
import math

import jax
import jax.numpy as jnp
from jax import lax
from jax.experimental import pallas as pl
from jax.experimental.pallas import tpu as pltpu

F32, BF16 = jnp.float32, jnp.bfloat16
D_MODEL = 1024
EPS = 1e-6
NEG_INF = -1e30
FOX_HEADS, FOX_HEAD_DIM, FOX_WIDTH = 8, 64, 512
SC_WIDTH = 512
AB_IN = 3 * FOX_WIDTH + FOX_HEADS + 3 * SC_WIDTH
AB_IN_PAD = 3200
LRU_BW, LRU_BLOCKS = 256, 4
RG_C = 8.0
MEM_HEADS, MEM_HEAD_DIM = 4, 256
ADAM_LR, ADAM_B1, ADAM_B2, ADAM_EPS, ADAM_WD, ADAM_STEP = 0.001, 0.9, 0.999, 1e-08, 0.01, 10
N_CHIPS = 4
MESH = pl.DeviceIdType.MESH
VMEM_LIMIT_BYTES = 48 * 1024 * 1024
MM_OPERAND_TILE_BYTES = 7 * 1024 * 1024

NN = (((1,), (0,)), ((), ()))
NT = (((1,), (1,)), ((), ()))
TN = (((0,), (0,)), ((), ()))


def _dot(a, b, dn=NN):
    return lax.dot_general(a.astype(BF16), b.astype(BF16), dn, preferred_element_type=F32)


def _tile(n, prefs):
    for p in prefs:
        if n % p == 0:
            return p
    return n


def _pcall(name, kern, grid, ins, in_specs, out_shape, out_specs, sem):
    return pl.pallas_call(
        kern, name=name, grid=grid, in_specs=in_specs, out_specs=out_specs, out_shape=out_shape,
        compiler_params=pltpu.CompilerParams(dimension_semantics=sem, vmem_limit_bytes=VMEM_LIMIT_BYTES),
    )(*ins)


def mm(name, a, b, mode, out_dtype, reduce_layout=False):
    if mode == "nn":
        (m, k), n = a.shape, b.shape[1]
    elif mode == "nt":
        (m, k), n = a.shape, b.shape[0]
    else:
        (k, m), n = a.shape, b.shape[1]
    if reduce_layout:
        tm, tn = m // 2, n // N_CHIPS
    else:
        tn = _tile(n, ((1024,) if mode == "tn" else ()) + (512, 640, 256, 128))
        tm = next(c for c in (2048, 1024, 512, 256, 128, m)
                  if m % c == 0 and 2 * c * k <= MM_OPERAND_TILE_BYTES and 4 * c * tn <= MM_OPERAND_TILE_BYTES)
    dn = {"nn": NN, "nt": NT, "tn": TN}[mode]

    def kern(a_ref, b_ref, o_ref):
        o_ref[...] = _dot(a_ref[...], b_ref[...], dn).astype(o_ref.dtype)

    a_spec = pl.BlockSpec((k, tm), lambda i, j: (0, i)) if mode == "tn" else pl.BlockSpec((tm, k), lambda i, j: (i, 0))
    b_spec = pl.BlockSpec((tn, k), lambda i, j: (j, 0)) if mode == "nt" else pl.BlockSpec((k, tn), lambda i, j: (0, j))
    if reduce_layout:
        out_shape = jax.ShapeDtypeStruct((N_CHIPS, 2, tm, tn), out_dtype)
        o_spec = pl.BlockSpec((None, None, tm, tn), lambda i, j: (j, i, 0, 0))
    else:
        out_shape = jax.ShapeDtypeStruct((m, n), out_dtype)
        o_spec = pl.BlockSpec((tm, tn), lambda i, j: (i, j))
    return _pcall(name, kern, (m // tm, n // tn), (a, b), [a_spec, b_spec], out_shape, o_spec, ("parallel", "parallel"))


def rowwise(name, body, rows, params, outs, accs=(), tr=256):
    t = rows[0].shape[0]
    tr = min(tr, t)
    nr, npar, no = len(rows), len(params), len(outs)

    def kern(*refs):
        acc_refs = refs[nr + npar + no:]
        if acc_refs:
            @pl.when(pl.program_id(0) == 0)
            def _():
                for ar in acc_refs:
                    ar[...] = jnp.zeros_like(ar)
        body(refs[:nr], refs[nr:nr + npar], refs[nr + npar:nr + npar + no], acc_refs)

    in_specs = [pl.BlockSpec((tr, x.shape[1]), lambda i: (i, 0)) for x in rows]
    in_specs += [pl.BlockSpec(p.shape, lambda i: (0, 0)) for p in params]
    out_specs = [pl.BlockSpec((tr, c), lambda i: (i, 0)) for c, _ in outs]
    out_specs += [pl.BlockSpec(s, lambda i: (0, 0)) for s in accs]
    out_shape = [jax.ShapeDtypeStruct((t, c), dt) for c, dt in outs]
    out_shape += [jax.ShapeDtypeStruct(s, F32) for s in accs]
    return _pcall(name, kern, (t // tr,), (*rows, *params), in_specs, out_shape, out_specs,
                  ("arbitrary",) if accs else ("parallel",))


def _rms_stats(x):
    x = x.astype(F32)
    r = lax.rsqrt(jnp.mean(x * x, axis=-1, keepdims=True) + EPS)
    return r, x * r


def _rms_bwd(xh, r, g, dy):
    dxh = dy * g
    dx = r * (dxh - xh * jnp.mean(dxh * xh, axis=-1, keepdims=True))
    return dx, jnp.sum(dy * xh, axis=0, keepdims=True)


def rms_pre(name, x, gains, layer):
    def body(r, p, o, a):
        _, xh = _rms_stats(r[0][...])
        o[0][...] = (xh * p[0][layer:layer + 1, :]).astype(BF16)
    return rowwise(name, body, [x], [gains], [(x.shape[1], BF16)])[0]


def post_add(name, x, y, gains, layer):
    def body(r, p, o, a):
        _, yh = _rms_stats(r[1][...])
        o[0][...] = r[0][...] + yh * p[0][layer:layer + 1, :]
    return rowwise(name, body, [x, y], [gains], [(x.shape[1], F32)])[0]


def post_bwd(name, y, dx, gains, layer):
    def body(r, p, o, a):
        rr, yh = _rms_stats(r[0][...])
        dy, dg = _rms_bwd(yh, rr, p[0][layer:layer + 1, :], r[1][...])
        o[0][...] = dy.astype(BF16)
        a[0][...] += dg
    c = y.shape[1]
    return rowwise(name, body, [y, dx], [gains], [(c, BF16)], [(1, c)])


def pre_bwd(name, x, dh, dx_res, gains, layer):
    def body(r, p, o, a):
        rr, xh = _rms_stats(r[0][...])
        dx, dg = _rms_bwd(xh, rr, p[0][layer:layer + 1, :], r[1][...].astype(F32))
        o[0][...] = r[2][...] + dx
        a[0][...] += dg
    c = x.shape[1]
    return rowwise(name, body, [x, dh, dx_res], [gains], [(c, F32)], [(1, c)])


def post_add_pre(name, x, y, gains_post, layer_post, gains_pre, layer_pre):
    def body(r, p, o, a):
        _, yh = _rms_stats(r[1][...])
        x_new = r[0][...] + yh * p[0][layer_post:layer_post + 1, :]
        o[0][...] = x_new
        _, xh = _rms_stats(x_new)
        o[1][...] = (xh * p[1][layer_pre:layer_pre + 1, :]).astype(BF16)
    c = x.shape[1]
    return rowwise(name, body, [x, y], [gains_post, gains_pre], [(c, F32), (c, BF16)])


def pre_post_bwd(name, x, dh, dx_res, gains_pre, layer_pre, y, gains_post, layer_post):
    def body(r, p, o, a):
        rr, xh = _rms_stats(r[0][...])
        dx_norm, dg_pre = _rms_bwd(xh, rr, p[0][layer_pre:layer_pre + 1, :], r[1][...].astype(F32))
        dx = r[2][...] + dx_norm
        o[0][...] = dx
        a[0][...] += dg_pre
        ry, yh = _rms_stats(r[3][...])
        dy, dg_post = _rms_bwd(yh, ry, p[1][layer_post:layer_post + 1, :], dx)
        o[1][...] = dy.astype(BF16)
        a[1][...] += dg_post
    c = x.shape[1]
    return rowwise(name, body, [x, dh, dx_res, y], [gains_pre, gains_post], [(c, F32), (c, BF16)], [(1, c), (1, c)])


def gain_bwd(name, x, dh):
    def body(r, p, o, a):
        _, xh = _rms_stats(r[0][...])
        a[0][...] += jnp.sum(r[1][...] * xh, axis=0, keepdims=True)
    return rowwise(name, body, [x, dh], [], [], [(1, x.shape[1])])[0]


def _sigmoid(z):
    return 1.0 / (1.0 + jnp.exp(-z))


def swiglu_fwd(name, gu):
    f = gu.shape[1] // 2

    def body(r, p, o, a):
        g = r[0][:, :f].astype(F32)
        u = r[0][:, f:].astype(F32)
        o[0][...] = (g * _sigmoid(g) * u).astype(BF16)
    return rowwise(name, body, [gu], [], [(f, BF16)])[0]


def swiglu_bwd(name, gu, da):
    f = gu.shape[1] // 2

    def body(r, p, o, a):
        g = r[0][:, :f].astype(F32)
        u = r[0][:, f:].astype(F32)
        d = r[1][...].astype(F32)
        sg = _sigmoid(g)
        o[0][:, :f] = (d * u * sg * (1.0 + g * (1.0 - sg))).astype(BF16)
        o[0][:, f:] = (d * g * sg).astype(BF16)
    return rowwise(name, body, [gu, da], [], [(2 * f, BF16)])[0]


def loss_head(name, y, target):
    c = y.shape[1]

    def body(r, p, o, a):
        e = r[0][...] - r[1][...]
        o[0][...] = e * (1.0 / c)
        a[0][...] += jnp.sum(e * e, axis=0, keepdims=True)
    return rowwise(name, body, [y, target], [], [(c, F32)], [(1, c)])


def adamw(name, w, g, m, v):
    c = w.shape[1]

    def body(r, p, o, a):
        wv, gv, mv, vv = (x[...] for x in r)
        m2 = ADAM_B1 * mv + (1.0 - ADAM_B1) * gv
        v2 = ADAM_B2 * vv + (1.0 - ADAM_B2) * (gv * gv)
        m_hat = m2 / (1.0 - ADAM_B1 ** ADAM_STEP)
        v_hat = v2 / (1.0 - ADAM_B2 ** ADAM_STEP)
        o[0][...] = -ADAM_LR * (m_hat / (jnp.sqrt(v_hat) + ADAM_EPS) + ADAM_WD * wv)
        o[1][...] = m2
        o[2][...] = v2
        o[3][...] = gv
    tr = _tile(w.shape[0], (256, 128, 64, 32, 16, 8))
    return rowwise(name, body, [w, g, m, v], [], [(c, F32)] * 4, tr=tr)


def colwise(name, body, cols, params, outs, pouts=(), tc=128):
    t = cols[0][0].shape[0]
    c = params[0].shape[1] if params else cols[0][0].shape[1]
    nc, npar, no = len(cols), len(params), len(outs)

    def kern(*refs):
        body(refs[:nc], refs[nc:nc + npar], refs[nc + npar:nc + npar + no], refs[nc + npar + no:])

    in_specs = [pl.BlockSpec((t, tc), functools.partial(lambda j, off: (0, j + off), off=off)) for _, off in cols]
    in_specs += [pl.BlockSpec((p.shape[0], tc), lambda j: (0, j)) for p in params]
    out_specs = [pl.BlockSpec((t, tc), lambda j: (0, j)) for _ in outs]
    out_specs += [pl.BlockSpec((r, tc), lambda j: (0, j)) for r in pouts]
    out_shape = [jax.ShapeDtypeStruct((t, c), dt) for dt in outs]
    out_shape += [jax.ShapeDtypeStruct((r, c), F32) for r in pouts]
    return _pcall(name, kern, (c // tc,), (*[x for x, _ in cols], *params), in_specs, out_shape, out_specs,
                  ("parallel",))


def _row_index(shape):
    return lax.broadcasted_iota(jnp.int32, shape, 0)


def _shift_down(x, d, rows):
    return jnp.where(rows >= d, pltpu.roll(x, d, 0), 0.0)


def _shift_up(x, d, rows):
    t = x.shape[0]
    return jnp.where(rows < t - d, pltpu.roll(x, t - d, 0), 0.0)


def sconv_fwd(name, proj, col0, conv_w, tc=128):
    nb = SC_WIDTH // tc

    def body(cl, p, o, po):
        b, c, u = (x[...] for x in cl)
        rows = _row_index(b.shape)
        w = p[0][...]
        z = c * u
        conv = w[2:3] * z + w[1:2] * _shift_down(z, 1, rows) + w[0:1] * _shift_down(z, 2, rows)
        o[0][...] = (b * conv).astype(BF16)
    return colwise(name, body, [(proj, col0), (proj, col0 + nb), (proj, col0 + 2 * nb)], [conv_w], [BF16], tc=tc)[0]


def sconv_bwd(name, proj, col0, conv_w, dyb, dcol0, tc=128):
    nb = SC_WIDTH // tc

    def body(cl, p, o, po):
        b, c, u, dy = (x[...] for x in cl)
        rows = _row_index(b.shape)
        w = p[0][...]
        z = c * u
        z1, z2 = _shift_down(z, 1, rows), _shift_down(z, 2, rows)
        conv = w[2:3] * z + w[1:2] * z1 + w[0:1] * z2
        dconv = dy * b
        dz = w[2:3] * dconv + w[1:2] * _shift_up(dconv, 1, rows) + w[0:1] * _shift_up(dconv, 2, rows)
        o[0][...] = (dy * conv).astype(BF16)
        o[1][...] = (dz * u).astype(BF16)
        o[2][...] = (dz * c).astype(BF16)
        po[0][0:1, :] = jnp.sum(dconv * z2, axis=0, keepdims=True)
        po[0][1:2, :] = jnp.sum(dconv * z1, axis=0, keepdims=True)
        po[0][2:3, :] = jnp.sum(dconv * z, axis=0, keepdims=True)
    return colwise(name, body, [(proj, col0), (proj, col0 + nb), (proj, col0 + 2 * nb), (dyb, dcol0)], [conv_w],
                   [BF16, BF16, BF16], [3], tc=tc)


def _expm1(x):
    series = x * (1.0 + 0.5 * x * (1.0 + x * (1.0 / 3.0) * (1.0 + 0.25 * x * (1.0 + 0.2 * x))))
    return jnp.where(jnp.abs(x) < 0.05, series, jnp.exp(x) - 1.0)


def _log1p(x):
    series = x * (1.0 - x * (0.5 - x * (1.0 / 3.0 - 0.25 * x)))
    return jnp.where(jnp.abs(x) < 0.01, series, jnp.log(1.0 + x))


def _softplus_neg(lam):
    sp = jnp.maximum(-lam, 0.0) + _log1p(jnp.exp(-jnp.abs(lam)))
    return sp, -_sigmoid(-lam)


GELU_C = math.sqrt(2.0 / math.pi)


def _gelu(x):
    th = jnp.tanh(GELU_C * (x + 0.044715 * x * x * x))
    val = 0.5 * x * (1.0 + th)
    grad = 0.5 * (1.0 + th) + 0.5 * x * (1.0 - th * th) * GELU_C * (1.0 + 3.0 * 0.044715 * x * x)
    return val, grad


def rg_conv_fwd(name, gu2, conv_w, conv_b, tc=128):
    nb = D_MODEL // tc

    def body(cl, p, o, po):
        u = cl[0][...]
        rows = _row_index(u.shape)
        w = p[0][...]
        o[0][...] = (w[3:4] * u + w[2:3] * _shift_down(u, 1, rows) + w[1:2] * _shift_down(u, 2, rows)
                     + w[0:1] * _shift_down(u, 3, rows) + p[1][...])
    return colwise(name, body, [(gu2, nb)], [conv_w, conv_b], [F32], tc=tc)[0]


def rg_conv_bwd(name, gu2, duc, conv_w, tc=128):
    nb = D_MODEL // tc

    def body(cl, p, o, po):
        u, d = cl[0][...], cl[1][...]
        rows = _row_index(u.shape)
        w = p[0][...]
        o[0][...] = (w[3:4] * d + w[2:3] * _shift_up(d, 1, rows) + w[1:2] * _shift_up(d, 2, rows)
                     + w[0:1] * _shift_up(d, 3, rows)).astype(BF16)
        for k in range(4):
            uk = u if k == 3 else _shift_down(u, 3 - k, rows)
            po[0][k:k + 1, :] = jnp.sum(d * uk, axis=0, keepdims=True)
        po[1][...] = jnp.sum(d, axis=0, keepdims=True)
    return colwise(name, body, [(gu2, nb), (duc, 0)], [conv_w], [BF16], [4, 1], tc=tc)


def rg_gates_fwd(name, uc, w_a, b_a, w_i, b_i, tr=512):
    t = uc.shape[0]
    tr = min(tr, t)

    def kern(u_ref, wa_ref, ba_ref, wi_ref, bi_ref, r_ref, i_ref):
        ub = u_ref[...].astype(BF16)
        r_ref[...] = _sigmoid(_dot(ub, wa_ref[...]) + ba_ref[...])
        i_ref[...] = _sigmoid(_dot(ub, wi_ref[...]) + bi_ref[...])

    blk = pl.BlockSpec((tr, LRU_BW), lambda n, i: (i, n))
    wspec = pl.BlockSpec((None, LRU_BW, LRU_BW), lambda n, i: (n, 0, 0))
    bspec = pl.BlockSpec((1, LRU_BW), lambda n, i: (0, n))
    return _pcall(name, kern, (LRU_BLOCKS, t // tr), (uc, w_a, b_a, w_i, b_i), [blk, wspec, bspec, wspec, bspec],
                  [jax.ShapeDtypeStruct(uc.shape, F32)] * 2, [blk, blk], ("parallel", "parallel"))


def rg_gates_bwd(name, uc, dzr, dzi, duc_part, w_a, w_i):
    t = uc.shape[0]
    rows = LRU_BW // N_CHIPS

    def kern(u_ref, dr_ref, di_ref, dp_ref, wa_ref, wi_ref, duc_ref, dwa_ref, dwi_ref):
        ub = u_ref[...].astype(BF16)
        dr, di = dr_ref[...], di_ref[...]
        dwa, dwi = _dot(ub, dr, TN), _dot(ub, di, TN)
        for p in range(N_CHIPS):
            dwa_ref[p] = dwa[p * rows:(p + 1) * rows].astype(dwa_ref.dtype)
            dwi_ref[p] = dwi[p * rows:(p + 1) * rows].astype(dwi_ref.dtype)
        duc_ref[...] = dp_ref[...] + _dot(dr, wa_ref[...], NT) + _dot(di, wi_ref[...], NT)

    blk = pl.BlockSpec((t, LRU_BW), lambda n: (0, n))
    wspec = pl.BlockSpec((None, LRU_BW, LRU_BW), lambda n: (n, 0, 0))
    gspec = pl.BlockSpec((N_CHIPS, None, rows, LRU_BW), lambda n: (0, n, 0, 0))
    gshape = jax.ShapeDtypeStruct((N_CHIPS, LRU_BLOCKS, rows, LRU_BW), BF16)
    return _pcall(name, kern, (LRU_BLOCKS,), (uc, dzr, dzi, duc_part, w_a, w_i), [blk, blk, blk, blk, wspec, wspec],
                  [jax.ShapeDtypeStruct(uc.shape, F32), gshape, gshape], [blk, gspec, gspec], ("parallel",))


def _rg_decay(r, lam):
    sp, dsp = _softplus_neg(lam)
    la = -RG_C * r * sp
    a = jnp.exp(la)
    sq = jnp.sqrt(-_expm1(2.0 * la))
    return sp, dsp, a, sq


def rg_scan_fwd(name, gu2, uc, r, i, lam, tc=128):
    def body(cl, p, o, po):
        gate, ucv, rv, iv = (x[...] for x in cl)
        t = gate.shape[0]
        rows = _row_index(gate.shape)
        _, _, a, sq = _rg_decay(rv, p[0][...])
        b = sq * (iv * ucv)
        d = 1
        while d < t:
            keep = rows >= d
            b = a * jnp.where(keep, pltpu.roll(b, d, 0), 0.0) + b
            a = a * jnp.where(keep, pltpu.roll(a, d, 0), 1.0)
            d *= 2
        o[0][...] = (_gelu(gate)[0] * b).astype(BF16)
        o[1][...] = b
    return colwise(name, body, [(gu2, 0), (uc, 0), (r, 0), (i, 0)], [lam], [BF16, F32], tc=tc)


def rg_scan_bwd(name, gu2, uc, r, i, hs, dy, lam, tc=128):
    def body(cl, p, o, po):
        gate, ucv, rv, iv, h, dyv = (x[...] for x in cl)
        t = gate.shape[0]
        rows = _row_index(gate.shape)
        sp, dsp, a, sq = _rg_decay(rv, p[0][...])
        gl, dgl = _gelu(gate)
        o[0][...] = (dyv * h * dgl).astype(BF16)
        g = dyv * gl
        am = _shift_up(a, 1, rows)
        d = 1
        while d < t:
            keep = rows < t - d
            g = am * jnp.where(keep, pltpu.roll(g, t - d, 0), 0.0) + g
            am = am * jnp.where(keep, pltpu.roll(am, t - d, 0), 0.0)
            d *= 2
        da = g * _shift_down(h, 1, rows)
        iu = iv * ucv
        d_iu = g * sq
        dla = da * a - (g * iu) * (a * a) / sq
        dzr = dla * (-RG_C * sp) * rv * (1.0 - rv)
        dzi = d_iu * ucv * iv * (1.0 - iv)
        o[1][...] = dzr.astype(BF16)
        o[2][...] = dzi.astype(BF16)
        o[3][...] = d_iu * iv
        po[0][...] = jnp.sum(dzr, axis=0, keepdims=True)
        po[1][...] = jnp.sum(dzi, axis=0, keepdims=True)
        po[2][...] = jnp.sum(dla * rv, axis=0, keepdims=True) * (-RG_C) * dsp
    return colwise(name, body, [(gu2, 0), (uc, 0), (r, 0), (i, 0), (hs, 0), (dy, 0)], [lam],
                   [BF16, BF16, BF16, F32], [1, 1, 1], tc=tc)


def _split3(x):
    hi = x.astype(BF16)
    r1 = x - hi.astype(F32)
    mid = r1.astype(BF16)
    lo = (r1 - mid.astype(F32)).astype(BF16)
    return hi, mid, lo


def _tri_dot(x, tri):
    out = None
    for piece in _split3(x):
        term = lax.dot_general(piece, tri, NN, preferred_element_type=F32)
        out = term if out is None else out + term
    return out


def fox_gates_fwd(name, z_t, b_f):
    h, t = z_t.shape
    tb = min(512, t)

    def kern(z_ref, b_ref, o_ref):
        z = z_ref[...] + b_ref[...]
        logf = jnp.minimum(z, 0.0) - _log1p(jnp.exp(-jnp.abs(z)))
        src = lax.broadcasted_iota(jnp.int32, (t, tb), 0)
        dst = lax.broadcasted_iota(jnp.int32, (t, tb), 1) + pl.program_id(0) * tb
        o_ref[...] = _tri_dot(logf, (src <= dst).astype(BF16))

    return _pcall(name, kern, (t // tb,), (z_t, b_f),
                  [pl.BlockSpec((h, t), lambda j: (0, 0)), pl.BlockSpec((h, 1), lambda j: (0, 0))],
                  jax.ShapeDtypeStruct((h, t), F32), pl.BlockSpec((h, tb), lambda j: (0, j)), ("parallel",))


def fox_gates_bwd(name, z_t, b_f, dcum_t):
    h, t = z_t.shape
    tb = min(512, t)

    def kern(z_ref, b_ref, d_ref, dz_ref, db_ref):
        @pl.when(pl.program_id(0) == 0)
        def _():
            db_ref[...] = jnp.zeros_like(db_ref)
        src = lax.broadcasted_iota(jnp.int32, (t, tb), 0)
        dst = lax.broadcasted_iota(jnp.int32, (t, tb), 1) + pl.program_id(0) * tb
        dlogf = _tri_dot(d_ref[...], (src >= dst).astype(BF16))
        z = z_ref[...] + b_ref[...]
        dz = dlogf * _sigmoid(-z)
        dz_ref[...] = dz
        db_ref[...] += jnp.sum(dz, axis=1, keepdims=True)

    return _pcall(name, kern, (t // tb,), (z_t, b_f, dcum_t),
                  [pl.BlockSpec((h, tb), lambda j: (0, j)), pl.BlockSpec((h, 1), lambda j: (0, 0)),
                   pl.BlockSpec((h, t), lambda j: (0, 0))],
                  [jax.ShapeDtypeStruct((h, t), F32), jax.ShapeDtypeStruct((h, 1), F32)],
                  [pl.BlockSpec((h, tb), lambda j: (0, j)), pl.BlockSpec((h, 1), lambda j: (0, 0))], ("arbitrary",))


def _fox_spans(qs, k_ref, cr_ref, i, tq):
    n0 = i * tq
    sd = _dot(qs, k_ref[n0:n0 + tq, :], NT) - cr_ref[:, n0:n0 + tq]
    row = lax.broadcasted_iota(jnp.int32, (tq, tq), 0)
    col = lax.broadcasted_iota(jnp.int32, (tq, tq), 1)
    spans = [(n0, tq, jnp.where(row >= col, sd, NEG_INF))]
    if i > 0:
        spans.append((0, n0, _dot(qs, k_ref[0:n0, :], NT) - cr_ref[:, 0:n0]))
    return spans


def fox_fwd(name, q, k, v, cum_r, tq=256):
    h, t, dh = q.shape
    tq = min(tq, t)
    scale = FOX_HEAD_DIM ** -0.5

    def kern(q_ref, k_ref, v_ref, cr_ref, o_ref, lse_ref):
        for i in range(t // tq):
            rows = slice(i * tq, (i + 1) * tq)
            spans = _fox_spans(q_ref[rows, :] * scale, k_ref, cr_ref, i, tq)
            m = functools.reduce(jnp.maximum, [jnp.max(s, axis=-1, keepdims=True) for _, _, s in spans])
            l, acc = 0.0, 0.0
            for k0, kn, s in spans:
                p = jnp.exp(s - m)
                l = l + jnp.sum(p, axis=-1, keepdims=True)
                acc = acc + _dot(p, v_ref[k0:k0 + kn, :])
            o_ref[rows, :] = (acc / l).astype(o_ref.dtype)
            lse_ref[rows, :] = m + jnp.log(l)

    hspec = pl.BlockSpec((None, t, dh), lambda a: (a, 0, 0))
    cspec = pl.BlockSpec((None, t, 1), lambda a: (a, 0, 0))
    rspec = pl.BlockSpec((None, 1, t), lambda a: (a, 0, 0))
    return _pcall(name, kern, (h,), (q, k, v, cum_r), [hspec, hspec, hspec, rspec],
                  [jax.ShapeDtypeStruct((h, t, dh), BF16), jax.ShapeDtypeStruct((h, t, 1), F32)],
                  [hspec, cspec], ("parallel",))


def fox_bwd(name, q, k, v, do, lse, cum_r, tq=256):
    h, t, dh = q.shape
    tq = min(tq, t)
    scale = FOX_HEAD_DIM ** -0.5

    def kern(q_ref, k_ref, v_ref, do_ref, lse_ref, cr_ref, dq_ref, dk_ref, dv_ref, dc_ref):
        dk_ref[...] = jnp.zeros_like(dk_ref)
        dv_ref[...] = jnp.zeros_like(dv_ref)
        dc_ref[...] = jnp.zeros_like(dc_ref)
        for i in range(t // tq):
            rows = slice(i * tq, (i + 1) * tq)
            qs, dov, lse_v = q_ref[rows, :] * scale, do_ref[rows, :], lse_ref[rows, :]
            spans = _fox_spans(qs, k_ref, cr_ref, i, tq)
            probs = [jnp.exp(s - lse_v) for _, _, s in spans]
            dps = [_dot(dov, v_ref[k0:k0 + kn, :], NT) for k0, kn, _ in spans]
            rowdot = sum(jnp.sum(dp * p, axis=-1, keepdims=True) for dp, p in zip(dps, probs))
            dq = 0.0
            for (k0, kn, _), p, dp in zip(spans, probs, dps):
                ds = p * (dp - rowdot)
                dq = dq + _dot(ds, k_ref[k0:k0 + kn, :])
                dk_ref[k0:k0 + kn, :] += _dot(ds, qs, TN)
                dv_ref[k0:k0 + kn, :] += _dot(p, dov, TN)
                dc_ref[:, k0:k0 + kn] -= jnp.sum(ds, axis=0, keepdims=True)
            dq_ref[rows, :] = (dq * scale).astype(dq_ref.dtype)

    hspec = pl.BlockSpec((None, t, dh), lambda a: (a, 0, 0))
    cspec = pl.BlockSpec((None, t, 1), lambda a: (a, 0, 0))
    rspec = pl.BlockSpec((None, 1, t), lambda a: (a, 0, 0))
    return _pcall(name, kern, (h,), (q, k, v, do, lse, cum_r), [hspec, hspec, hspec, hspec, cspec, rspec],
                  [jax.ShapeDtypeStruct((h, t, dh), BF16), jax.ShapeDtypeStruct((h, t, dh), F32),
                   jax.ShapeDtypeStruct((h, t, dh), F32), jax.ShapeDtypeStruct((h, 1, t), F32)],
                  [hspec, hspec, hspec, rspec], ("parallel",))


def _xattn_probs(q, k):
    s = _dot(q, k, NT) * (MEM_HEAD_DIM ** -0.5)
    p = jnp.exp(s - jnp.max(s, axis=-1, keepdims=True))
    return p / jnp.sum(p, axis=-1, keepdims=True)


def xattn_fwd(name, q, kv, tq=512):
    t = q.shape[0]
    tq = min(tq, t)
    ml = kv.shape[0]

    def kern(q_ref, k_ref, v_ref, o_ref):
        o_ref[...] = _dot(_xattn_probs(q_ref[...], k_ref[...]), v_ref[...]).astype(o_ref.dtype)

    qspec = pl.BlockSpec((tq, MEM_HEAD_DIM), lambda i, a: (i, a))
    return _pcall(name, kern, (t // tq, MEM_HEADS), (q, kv, kv),
                  [qspec, pl.BlockSpec((ml, MEM_HEAD_DIM), lambda i, a: (0, a)),
                   pl.BlockSpec((ml, MEM_HEAD_DIM), lambda i, a: (0, MEM_HEADS + a))],
                  jax.ShapeDtypeStruct(q.shape, BF16), qspec, ("parallel", "parallel"))


def xattn_bwd(name, q, kv, do, tq=512):
    t = q.shape[0]
    tq = min(tq, t)
    ml = kv.shape[0]
    scale = MEM_HEAD_DIM ** -0.5

    def kern(q_ref, k_ref, v_ref, do_ref, dq_ref, dk_ref, dv_ref):
        @pl.when(pl.program_id(1) == 0)
        def _():
            dk_ref[...] = jnp.zeros_like(dk_ref)
            dv_ref[...] = jnp.zeros_like(dv_ref)
        qv, kv_, dov = q_ref[...], k_ref[...], do_ref[...]
        p = _xattn_probs(qv, kv_)
        dp = _dot(dov, v_ref[...], NT)
        ds = p * (dp - jnp.sum(dp * p, axis=-1, keepdims=True)) * scale
        dq_ref[...] = _dot(ds, kv_).astype(dq_ref.dtype)
        dk_ref[...] += _dot(ds, qv, TN)
        dv_ref[...] += _dot(p, dov, TN)

    qspec = pl.BlockSpec((tq, MEM_HEAD_DIM), lambda a, i: (i, a))
    kspec = pl.BlockSpec((ml, MEM_HEAD_DIM), lambda a, i: (0, a))
    return _pcall(name, kern, (MEM_HEADS, t // tq), (q, kv, kv, do),
                  [qspec, kspec, pl.BlockSpec((ml, MEM_HEAD_DIM), lambda a, i: (0, MEM_HEADS + a)), qspec],
                  [jax.ShapeDtypeStruct(q.shape, BF16), jax.ShapeDtypeStruct((ml, D_MODEL), F32),
                   jax.ShapeDtypeStruct((ml, D_MODEL), F32)],
                  [qspec, kspec, kspec], ("parallel", "arbitrary"))


def _heads(x):
    t = x.shape[0]
    return x.reshape(t, FOX_HEADS, FOX_HEAD_DIM).transpose(1, 0, 2)


def _unheads(x):
    return x.transpose(1, 0, 2).reshape(x.shape[1], FOX_WIDTH)


def _row_cut(dw):
    return dw.reshape(N_CHIPS, 2, dw.shape[0] // (2 * N_CHIPS), dw.shape[1])


def local_step(x, mem, target, w, layer_weights=None, reduce_hook=None):
    depth = w["g_mix_pre"].shape[0]
    t = x.shape[0]
    saved = []
    i1, i2, i3 = 3 * FOX_WIDTH, 3 * FOX_WIDTH + FOX_HEADS, AB_IN
    ncol = 128

    def stacked_weights(layer, part, _):
        names = COMMON_BIG if part == "rest" else layer_big(layer)[len(COMMON_BIG):]
        return {n: w[n][layer if n in COMMON_BIG else layer // 2] for n in names}

    get_weights = layer_weights or stacked_weights
    h1 = rms_pre("l0_mix_pre", x, w["g_mix_pre"], 0)
    for layer in range(depth):
        lw = dict(get_weights(layer, "mix", x))
        s = {"x0": x, "lw": lw}
        tag = f"l{layer}"
        s["h1"] = h1
        if layer % 2 == 0:
            e = layer // 2
            w_in = jnp.pad(lw["ab_w_in"], ((0, 0), (0, AB_IN_PAD - AB_IN)))
            proj = mm(f"{tag}_ab_in", h1, w_in, "nn", F32)
            qkv = proj[:, :i1].astype(BF16).reshape(t, 3, FOX_HEADS, FOX_HEAD_DIM).transpose(1, 2, 0, 3)
            z_t = proj[:, i1:i2].T
            b_f = w["ab_b_f"][e].reshape(FOX_HEADS, 1)
            cum_t = fox_gates_fwd(f"{tag}_fox_gates", z_t, b_f)
            cum_r = cum_t[:, None, :]
            oh, lse = fox_fwd(f"{tag}_fox", qkv[0], qkv[1], qkv[2], cum_r)
            bcu = proj[:, i2:i3]
            y_b = sconv_fwd(f"{tag}_sconv", bcu, 0, w["ab_conv_w"][e])
            ycat = jnp.concatenate([_unheads(oh), y_b], axis=1)
            y1 = mm(f"{tag}_ab_out", ycat, lw["ab_w_out"], "nn", BF16)
            s.update(w_in=w_in, qkv=qkv, z_t=z_t, b_f=b_f, cum_r=cum_r, lse=lse, bcu=bcu, ycat=ycat)
        else:
            o = layer // 2
            gu2 = mm(f"{tag}_c_in", h1, lw["c_w_in"], "nn", F32)
            conv_b = w["c_conv_b"][o].reshape(1, -1)
            uc = rg_conv_fwd(f"{tag}_rg_conv", gu2, w["c_conv_w"][o], conv_b)
            b_a, b_i = w["c_b_a"][o].reshape(1, -1), w["c_b_i"][o].reshape(1, -1)
            r, i = rg_gates_fwd(f"{tag}_rg_gates", uc, lw["c_w_a"], b_a, lw["c_w_i"], b_i)
            lam = w["c_lam"][o].reshape(1, -1)
            ymix, hs = rg_scan_fwd(f"{tag}_rg_scan", gu2, uc, r, i, lam)
            y1 = mm(f"{tag}_c_out", ymix, lw["c_w_out"], "nn", BF16)
            s.update(gu2=gu2, uc=uc, r=r, i=i, lam=lam, hs=hs, ymix=ymix)
        s["y1"] = y1
        x, h2 = post_add_pre(f"{tag}_mix_post", x, y1, w["g_mix_post"], layer, w["g_cross_pre"], layer)
        lw.update(get_weights(layer, "rest", x))
        s["x1"] = x
        m = rms_pre(f"{tag}_mem_pre", mem, w["g_mem"], layer)
        q = mm(f"{tag}_xq", h2, lw["w_xq"], "nn", BF16)
        kv = mm(f"{tag}_xkv", m, lw["w_xkv"], "nn", BF16)
        o_att = xattn_fwd(f"{tag}_xattn", q, kv)
        y2 = mm(f"{tag}_xo", o_att, lw["w_xo"], "nn", BF16)
        s.update(h2=h2, m=m, q=q, kv=kv, o_att=o_att, y2=y2)
        x, h3 = post_add_pre(f"{tag}_cross_post", x, y2, w["g_cross_post"], layer, w["g_ffn_pre"], layer)
        s["x2"] = x
        gu = mm(f"{tag}_ffn_gu", h3, lw["w_ffn_gu"], "nn", BF16)
        act = swiglu_fwd(f"{tag}_swiglu", gu)
        y3 = mm(f"{tag}_ffn_down", act, lw["w_ffn_down"], "nn", BF16)
        s.update(h3=h3, gu=gu, act=act, y3=y3)
        if layer + 1 < depth:
            x, h1 = post_add_pre(f"{tag}_ffn_post", x, y3, w["g_ffn_post"], layer, w["g_mix_pre"], layer + 1)
        else:
            x = post_add(f"{tag}_ffn_post", x, y3, w["g_ffn_post"], layer)
        saved.append(s)

    dx, sq_cols = loss_head("loss_head", x, target)

    grads = {k: [None] * v.shape[0] for k, v in w.items() if k not in BIG}
    big = {}

    def dw(name, a, b, cols_cut=False):
        return mm(name, a, b, "tn", BF16, reduce_layout=True) if cols_cut else _row_cut(mm(name, a, b, "tn", BF16))

    def hook(layer, part, part_grads, after, gains):
        token = None if reduce_hook is None else reduce_hook(layer, part, part_grads, after)
        return gains if token is None else gains + token

    dy3, grads["g_ffn_post"][depth - 1] = post_bwd(f"b{depth - 1}_ffn_post", saved[-1]["y3"], dx, w["g_ffn_post"], depth - 1)
    for layer in reversed(range(depth)):
        s = saved[layer]
        lw = s["lw"]
        tag = f"b{layer}"
        lg = {}
        dact = mm(f"{tag}_ffn_down_dx", dy3, lw["w_ffn_down"], "nt", BF16)
        lg["w_ffn_down"] = dw(f"{tag}_ffn_down_dw", s["act"], dy3)
        dgu = swiglu_bwd(f"{tag}_swiglu", s["gu"], dact)
        dh3 = mm(f"{tag}_ffn_gu_dx", dgu, lw["w_ffn_gu"], "nt", BF16)
        lg["w_ffn_gu"] = dw(f"{tag}_ffn_gu_dw", s["h3"], dgu, cols_cut=True)
        g_ffn_pre = hook(layer, "ffn", lg, dh3, w["g_ffn_pre"])
        ffn_grads, lg = lg, {}
        dx, dy2, grads["g_ffn_pre"][layer], grads["g_cross_post"][layer] = pre_post_bwd(
            f"{tag}_ffn_pre", s["x2"], dh3, dx, g_ffn_pre, layer, s["y2"], w["g_cross_post"], layer)
        do = mm(f"{tag}_xo_dx", dy2, lw["w_xo"], "nt", BF16)
        lg["w_xo"] = dw(f"{tag}_xo_dw", s["o_att"], dy2)
        dq, dk, dv = xattn_bwd(f"{tag}_xattn", s["q"], s["kv"], do)
        dh2 = mm(f"{tag}_xq_dx", dq, lw["w_xq"], "nt", BF16)
        lg["w_xq"] = dw(f"{tag}_xq_dw", s["h2"], dq)
        dkv = jnp.concatenate([dk, dv], axis=1).astype(BF16)
        dm = mm(f"{tag}_xkv_dx", dkv, lw["w_xkv"], "nt", F32)
        lg["w_xkv"] = dw(f"{tag}_xkv_dw", s["m"], dkv, cols_cut=True)
        grads["g_mem"][layer] = gain_bwd(f"{tag}_mem_pre", mem, dm)
        g_cross_pre = hook(layer, "rest", lg, dh2, w["g_cross_pre"])
        dx, dy1, grads["g_cross_pre"][layer], grads["g_mix_post"][layer] = pre_post_bwd(
            f"{tag}_cross_pre", s["x1"], dh2, dx, g_cross_pre, layer, s["y1"], w["g_mix_post"], layer)
        rest_grads, lg = lg, {}
        if layer % 2 == 0:
            e = layer // 2
            dycat = mm(f"{tag}_ab_out_dx", dy1, lw["ab_w_out"], "nt", BF16)
            g_mix_pre = hook(layer, "mixer", {}, dycat, w["g_mix_pre"])
            lg["ab_w_out"] = dw(f"{tag}_ab_out_dw", s["ycat"], dy1)
            do_h = _heads(dycat[:, :FOX_WIDTH].astype(BF16))
            qkv = s["qkv"]
            dqh, dkh, dvh, dcum = fox_bwd(f"{tag}_fox", qkv[0], qkv[1], qkv[2], do_h, s["lse"], s["cum_r"])
            dz_t, db_f = fox_gates_bwd(f"{tag}_fox_gates", s["z_t"], s["b_f"], dcum.reshape(FOX_HEADS, t))
            grads["ab_b_f"][e] = db_f.reshape(FOX_HEADS)
            db, dc, du, dconv_w = sconv_bwd(f"{tag}_sconv", s["bcu"], 0, w["ab_conv_w"][e], dycat, FOX_WIDTH // ncol)
            grads["ab_conv_w"][e] = dconv_w
            dproj = jnp.concatenate(
                [_unheads(dqh), _unheads(dkh).astype(BF16), _unheads(dvh).astype(BF16), dz_t.T.astype(BF16), db, dc, du,
                 jnp.zeros((t, AB_IN_PAD - AB_IN), BF16)], axis=1)
            dh1 = mm(f"{tag}_ab_in_dx", dproj, s["w_in"], "nt", BF16)
            dw_in = mm(f"{tag}_ab_in_dw", s["h1"], dproj, "tn", F32)[:, :AB_IN]
            lg["ab_w_in"] = dw_in.reshape(2, D_MODEL // 2, N_CHIPS, AB_IN // N_CHIPS).transpose(2, 0, 1, 3).astype(BF16)
        else:
            o = layer // 2
            dymix = mm(f"{tag}_c_out_dx", dy1, lw["c_w_out"], "nt", BF16)
            g_mix_pre = hook(layer, "mixer", {}, dymix, w["g_mix_pre"])
            lg["c_w_out"] = dw(f"{tag}_c_out_dw", s["ymix"], dy1)
            dgate, dzr, dzi, duc_part, db_a, db_i, dlam = rg_scan_bwd(
                f"{tag}_rg_scan", s["gu2"], s["uc"], s["r"], s["i"], s["hs"], dymix, s["lam"])
            duc, dw_a, dw_i = rg_gates_bwd(f"{tag}_rg_gates", s["uc"], dzr, dzi, duc_part, lw["c_w_a"], lw["c_w_i"])
            lg["c_w_a"] = dw_a.reshape(N_CHIPS, 2, LRU_BW // 2, LRU_BW)
            lg["c_w_i"] = dw_i.reshape(N_CHIPS, 2, LRU_BW // 2, LRU_BW)
            du_raw, dconv_w, dconv_b = rg_conv_bwd(f"{tag}_rg_conv", s["gu2"], duc, w["c_conv_w"][o])
            grads["c_b_a"][o] = db_a.reshape(LRU_BLOCKS, LRU_BW)
            grads["c_b_i"][o] = db_i.reshape(LRU_BLOCKS, LRU_BW)
            grads["c_lam"][o] = dlam.reshape(-1)
            grads["c_conv_w"][o] = dconv_w
            grads["c_conv_b"][o] = dconv_b.reshape(-1)
            dgu2 = jnp.concatenate([dgate, du_raw], axis=1)
            dh1 = mm(f"{tag}_c_in_dx", dgu2, lw["c_w_in"], "nt", BF16)
            lg["c_w_in"] = dw(f"{tag}_c_in_dw", s["h1"], dgu2, cols_cut=True)
        if reduce_hook is None:
            big[layer] = {**ffn_grads, **rest_grads, **lg}
        g_mix_pre = hook(layer, "mix", lg, dh1, g_mix_pre)
        if layer > 0:
            dx, dy3, grads["g_mix_pre"][layer], grads["g_ffn_post"][layer - 1] = pre_post_bwd(
                f"{tag}_mix_pre", s["x0"], dh1, dx, g_mix_pre, layer, saved[layer - 1]["y3"], w["g_ffn_post"], layer - 1)
        else:
            dx, grads["g_mix_pre"][layer] = pre_bwd(f"{tag}_mix_pre", s["x0"], dh1, dx, g_mix_pre, layer)

    for k in list(grads):
        if k.startswith("g_"):
            grads[k] = [g.reshape(-1) for g in grads[k]]
        grads[k] = jnp.stack(grads[k])
    return sq_cols, dx, grads, big


CHIP_FLIPS = ((1, 0), (0, 1), (1, 1))
HBM_SPEC = pl.BlockSpec(memory_space=pltpu.HBM)
VMEM_SPEC = pl.BlockSpec(memory_space=pltpu.VMEM)


def _place():
    return lax.axis_index("x"), lax.axis_index("y"), lax.axis_index("c")


def _flip(v, f):
    return 1 - v if f else v


def _remote(src, dst, send_sem, recv_sem, target):
    return pltpu.make_async_remote_copy(src_ref=src, dst_ref=dst, send_sem=send_sem, recv_sem=recv_sem,
                                        device_id=target, device_id_type=MESH)


SEM_SPEC = pl.BlockSpec(memory_space=pltpu.SEMAPHORE)


def _swap_copies(srcs, lands, send_sems, recv_sems):
    x, y, c = _place()
    return [_remote(src.at[:, 1 - c], land, send_sems.at[len(CHIP_FLIPS) * a], recv_sems.at[len(CHIP_FLIPS) * a],
                    (x, y, 1 - c)) for a, (src, land) in enumerate(zip(srcs, lands))]


def _exchange_copies(srcs, lands, send_sems, recv_sems):
    x, y, c = _place()
    p = 2 * x + y
    cps = []
    for a, (src, land) in enumerate(zip(srcs, lands)):
        for k, (fx, fy) in enumerate(CHIP_FLIPS):
            qx, qy = _flip(x, fx), _flip(y, fy)
            sem = len(CHIP_FLIPS) * a + k
            cps.append(_remote(src.at[2 * qx + qy], land.at[p], send_sems.at[sem], recv_sems.at[sem], (qx, qy, c)))
    return cps


def _gather_copies(srcs, lands, send_sems, recv_sems):
    x, y, c = _place()
    p = 2 * x + y
    cps = []
    for a, (src, land) in enumerate(zip(srcs, lands)):
        for k, (fx, fy) in enumerate(CHIP_FLIPS):
            sem = len(CHIP_FLIPS) * a + k
            cps.append(_remote(src.at[c], land.at[p, c], send_sems.at[sem], recv_sems.at[sem],
                               (_flip(x, fx), _flip(y, fy), c)))
    return cps


def copies_start(name, make_copies, srcs, land_shapes):
    n = len(srcs)

    def body(*refs):
        for cp in make_copies(refs[:n], refs[n:2 * n], refs[2 * n], refs[2 * n + 1]):
            cp.start()
        refs[-1][...] = jnp.zeros_like(refs[-1])

    thru = [pltpu.HBM(b.shape, b.dtype) for b in srcs] + [pltpu.HBM(sh, b.dtype) for sh, b in zip(land_shapes, srcs)]
    outs = pl.pallas_call(
        body, name=name, in_specs=[HBM_SPEC] * (2 * n),
        out_shape=(pltpu.SemaphoreType.DMA((3 * n,)), pltpu.SemaphoreType.DMA((3 * n,)), *thru,
                   jax.ShapeDtypeStruct((8, 128), F32)),
        out_specs=(SEM_SPEC, SEM_SPEC, *[HBM_SPEC] * (2 * n), VMEM_SPEC),
        input_output_aliases={i: 2 + i for i in range(2 * n)},
        compiler_params=pltpu.CompilerParams(has_side_effects=pltpu.SideEffectType.DATAFLOW_SIDE_EFFECTING),
    )(*[pltpu.with_memory_space_constraint(b, pltpu.HBM) for b in srcs],
      *[pltpu.with_memory_space_constraint(lax.empty(sh, b.dtype), pltpu.HBM) for sh, b in zip(land_shapes, srcs)])
    return outs[:-1], outs[-1]


def copies_wait(name, make_copies, state, after):
    send_sems, recv_sems, *thru = state
    n = len(thru) // 2

    def body(*refs):
        for cp in make_copies(refs[:n], refs[n:2 * n], refs[2 * n], refs[2 * n + 1]):
            cp.wait_send()
            cp.wait_recv()

    outs = pl.pallas_call(
        body, name=name, in_specs=[HBM_SPEC] * (2 * n) + [SEM_SPEC, SEM_SPEC, pl.BlockSpec(memory_space=pl.ANY)],
        out_shape=tuple(pltpu.HBM(t.shape, t.dtype) for t in thru), out_specs=tuple([HBM_SPEC] * (2 * n)),
        input_output_aliases={i: i for i in range(2 * n)},
        compiler_params=pltpu.CompilerParams(has_side_effects=pltpu.SideEffectType.DATAFLOW_SIDE_EFFECTING),
    )(*thru, send_sems, recv_sems, after)
    return outs[:n], outs[n:]


def pass_to_sibling(name, shards, lands):
    n = len(lands)

    def body(*refs):
        own, ins, outs = refs[:n], refs[n:2 * n], refs[2 * n:3 * n]
        send_sems, recv_sems = refs[3 * n:]
        x, y, c = _place()
        sibling = (x, y, 1 - c)
        cps = []
        for a in range(n):
            for k, (fx, fy) in enumerate(CHIP_FLIPS):
                q = 2 * _flip(x, fx) + _flip(y, fy)
                cps.append(_remote(ins[a].at[q, c], outs[a].at[q, c], send_sems.at[a, k], recv_sems.at[a, k], sibling))
            cps.append(_remote(own[a], outs[a].at[2 * x + y], send_sems.at[a, 3], recv_sems.at[a, 3], sibling))
        for cp in cps:
            cp.start()
        for cp in cps:
            cp.wait()

    return pl.pallas_call(
        body, name=name, in_specs=[HBM_SPEC] * (2 * n), out_specs=[HBM_SPEC] * n,
        out_shape=[jax.ShapeDtypeStruct(b.shape, b.dtype) for b in lands],
        scratch_shapes=[pltpu.SemaphoreType.DMA((n, 4)), pltpu.SemaphoreType.DMA((n, 4))],
        input_output_aliases={n + i: i for i in range(n)},
    )(*shards, *lands)


def share_halves(bufs):
    n = len(bufs)

    def body(*refs):
        ins, outs, token = refs[:n], refs[n:2 * n], refs[2 * n]
        send_sems, recv_sems = refs[2 * n + 1:]
        x, y, c = _place()
        cps = [_remote(ins[a].at[:, c], outs[a].at[:, c], send_sems.at[a], recv_sems.at[a], (x, y, 1 - c))
               for a in range(n)]
        for cp in cps:
            cp.start()
        token[...] = jnp.zeros_like(token)
        for cp in cps:
            cp.wait()

    outs = pl.pallas_call(
        body, name="share_reduced_halves", in_specs=[HBM_SPEC] * n, out_specs=[HBM_SPEC] * n + [VMEM_SPEC],
        out_shape=[jax.ShapeDtypeStruct(b.shape, b.dtype) for b in bufs] + [jax.ShapeDtypeStruct((8, 128), F32)],
        scratch_shapes=[pltpu.SemaphoreType.DMA((n,)), pltpu.SemaphoreType.DMA((n,))],
        input_output_aliases={i: i for i in range(n)},
    )(*bufs)
    return outs[:n], outs[n]


DEVICE_FLIPS = tuple((fx, fy, fc) for fx in (0, 1) for fy in (0, 1) for fc in (0, 1))[1:]


def gather_small(name, v, reduce):
    r, cdim = v.shape
    n_dev = 8

    def body(v_ref, out_ref, *scratch):
        buf = scratch[0] if reduce else out_ref
        send_sems, recv_sems = scratch[-2:]
        x, y, c = _place()
        me = 4 * x + 2 * y + c
        buf[me] = v_ref[...]
        cps = []
        for k, (fx, fy, fc) in enumerate(DEVICE_FLIPS):
            cps.append(_remote(v_ref, buf.at[me], send_sems.at[k], recv_sems.at[k],
                               (_flip(x, fx), _flip(y, fy), _flip(c, fc))))
        for cp in cps:
            cp.start()
        for cp in cps:
            cp.wait()
        if reduce:
            total = buf[0]
            for d in range(1, n_dev):
                total = total + buf[d]
            out_ref[...] = total

    scratch = [pltpu.SemaphoreType.DMA((7,)), pltpu.SemaphoreType.DMA((7,))]
    if reduce:
        scratch = [pltpu.VMEM((n_dev, r, cdim), F32)] + scratch
    out_shape = jax.ShapeDtypeStruct((r, cdim) if reduce else (n_dev, r, cdim), F32)
    return pl.pallas_call(body, name=name, in_specs=[VMEM_SPEC], out_specs=VMEM_SPEC, out_shape=out_shape,
                          scratch_shapes=scratch)(v)


def pair_sum(name, own, got, core):
    _, hx, cols = got.shape
    tr = _tile(hx, (256, 128, 64, 32, 16))

    def kern(core_ref, a_ref, b_ref, o_ref):
        o_ref[...] = (a_ref[...].astype(F32) + b_ref[...].astype(F32)).astype(BF16)

    grid_spec = pltpu.PrefetchScalarGridSpec(
        num_scalar_prefetch=1, grid=(hx // tr,),
        in_specs=[pl.BlockSpec((N_CHIPS, None, tr, cols), lambda i, cr: (0, cr[0], i, 0)),
                  pl.BlockSpec((N_CHIPS, tr, cols), lambda i, cr: (0, i, 0))],
        out_specs=pl.BlockSpec((N_CHIPS, tr, cols), lambda i, cr: (0, i, 0)))
    return pl.pallas_call(
        kern, name=name, grid_spec=grid_spec, out_shape=jax.ShapeDtypeStruct(got.shape, BF16),
        compiler_params=pltpu.CompilerParams(dimension_semantics=("parallel",), vmem_limit_bytes=VMEM_LIMIT_BYTES),
    )(core, own, got)


def chip_sum(name, mine, parts, place, buf, layer):
    _, hx, yd = parts.shape
    tr = _tile(hx, (256, 128, 64, 32, 16))

    def kern(place_ref, m_ref, p_ref, _, o_ref):
        total = None
        for q in range(N_CHIPS):
            term = jnp.where(place_ref[0] == q, m_ref[...], p_ref[q]).astype(F32)
            total = term if total is None else total + term
        o_ref[...] = total

    grid_spec = pltpu.PrefetchScalarGridSpec(
        num_scalar_prefetch=1, grid=(hx // tr,),
        in_specs=[pl.BlockSpec((None, tr, yd), lambda i, pr: (pr[0], i, 0)),
                  pl.BlockSpec((N_CHIPS, tr, yd), lambda i, pr: (0, i, 0)),
                  pl.BlockSpec(memory_space=pl.ANY)],
        out_specs=pl.BlockSpec((None, None, tr, yd), lambda i, pr: (layer, pr[1], i, 0)))
    return pl.pallas_call(
        kern, name=name, grid_spec=grid_spec, out_shape=jax.ShapeDtypeStruct(buf.shape, buf.dtype),
        input_output_aliases={3: 0},
        compiler_params=pltpu.CompilerParams(dimension_semantics=("parallel",), vmem_limit_bytes=VMEM_LIMIT_BYTES),
    )(place, mine, parts, buf)


WEIGHTS = ("g_mix_pre", "g_mix_post", "g_cross_pre", "g_mem", "g_cross_post", "g_ffn_pre", "g_ffn_post", "w_xq", "w_xkv",
           "w_xo", "w_ffn_gu", "w_ffn_down", "ab_w_in", "ab_b_f", "ab_conv_w", "ab_w_out", "c_w_in", "c_conv_w",
           "c_conv_b", "c_w_a", "c_b_a", "c_w_i", "c_b_i", "c_lam", "c_w_out")
SHARD_DIM = {"w_xq": 1, "w_xkv": 2, "w_xo": 1, "w_ffn_gu": 2, "w_ffn_down": 1, "ab_w_in": 2, "ab_conv_w": 2,
             "ab_w_out": 1, "c_w_in": 2, "c_conv_w": 2, "c_conv_b": 1, "c_w_a": 2, "c_b_a": 2, "c_w_i": 2, "c_b_i": 2,
             "c_lam": 1, "c_w_out": 1}
COMMON_BIG = ("w_xq", "w_xkv", "w_xo", "w_ffn_gu", "w_ffn_down")
EVEN_BIG, ODD_BIG = ("ab_w_in", "ab_w_out"), ("c_w_in", "c_w_a", "c_w_i", "c_w_out")
BIG = COMMON_BIG + EVEN_BIG + ODD_BIG


def layer_big(layer):
    return COMMON_BIG + (ODD_BIG if layer % 2 else EVEN_BIG)


SPLIT_LAYERS = (0,)


FINE_REDUCE_LAYERS = (0, 1)


def reduce_chunk(layer, part):
    if layer not in FINE_REDUCE_LAYERS:
        return layer_big(layer) if part == "mix" else ()
    return {"ffn": ("w_ffn_gu", "w_ffn_down"), "rest": ("w_xq", "w_xkv", "w_xo"), "mixer": (),
            "mix": layer_big(layer)[len(COMMON_BIG):]}[part]


def chunk_names(layer, part):
    mixer = layer_big(layer)[len(COMMON_BIG):]
    if layer in SPLIT_LAYERS:
        return mixer if part == "mix" else COMMON_BIG
    return layer_big(layer) if part == "mix" else ()


SMALL_SHARDED = ("ab_conv_w", "c_conv_w", "c_conv_b", "c_b_a", "c_b_i", "c_lam")
REPLICATED = ("g_mix_pre", "g_mix_post", "g_cross_pre", "g_mem", "g_cross_post", "g_ffn_pre", "g_ffn_post", "ab_b_f")
PACK_COLS = 1024


def _unshard(g, d):
    shard = g.shape[1:]
    return jnp.moveaxis(g, 0, d).reshape(shard[:d] + (N_CHIPS * shard[d],) + shard[d + 1:])


def _shardify(full, d):
    s = full.shape
    return jnp.moveaxis(full.reshape(s[:d] + (N_CHIPS, s[d] // N_CHIPS) + s[d + 1:]), d, 0)


def _pack(arrays, rows):
    flat = jnp.concatenate([a.reshape(-1).astype(F32) for a in arrays])
    return jnp.pad(flat, (0, rows * PACK_COLS - flat.shape[0])).reshape(rows, PACK_COLS)


def _unpack(packed, shapes):
    flat = packed.reshape(-1)
    out, at = [], 0
    for s in shapes:
        size = math.prod(s)
        out.append(flat[at:at + size].reshape(s))
        at += size
    return out


def _rows_for(shapes):
    return -(-sum(math.prod(s) for s in shapes) // (8 * PACK_COLS)) * 8


def kernel(x, mem, g_mix_pre, g_mix_post, g_cross_pre, g_mem, g_cross_post, g_ffn_pre, g_ffn_post, w_xq, w_xkv, w_xo, w_ffn_gu, w_ffn_down, ab_w_in, ab_b_f, ab_conv_w, ab_w_out, c_w_in, c_conv_w, c_conv_b, c_w_a, c_b_a, c_w_i, c_b_i, c_lam, c_w_out, loss_target, m_g_mix_pre, m_g_mix_post, m_g_cross_pre, m_g_mem, m_g_cross_post, m_g_ffn_pre, m_g_ffn_post, m_w_xq, m_w_xkv, m_w_xo, m_w_ffn_gu, m_w_ffn_down, m_ab_w_in, m_ab_b_f, m_ab_conv_w, m_ab_w_out, m_c_w_in, m_c_conv_w, m_c_conv_b, m_c_w_a, m_c_b_a, m_c_w_i, m_c_b_i, m_c_lam, m_c_w_out, v_g_mix_pre, v_g_mix_post, v_g_cross_pre, v_g_mem, v_g_cross_post, v_g_ffn_pre, v_g_ffn_post, v_w_xq, v_w_xkv, v_w_xo, v_w_ffn_gu, v_w_ffn_down, v_ab_w_in, v_ab_b_f, v_ab_conv_w, v_ab_w_out, v_c_w_in, v_c_conv_w, v_c_conv_b, v_c_w_a, v_c_b_a, v_c_w_i, v_c_b_i, v_c_lam, v_c_w_out):
    given = dict(locals())
    w = {n: given[n] for n in WEIGHTS}
    m_in = {n: given["m_" + n] for n in WEIGHTS}
    v_in = {n: given["v_" + n] for n in WEIGHTS}
    xi, yi, ci = _place()
    chip = 2 * xi + yi

    full = {}
    small_shapes = [w[n].shape for n in SMALL_SHARDED]
    rows_w = _rows_for(small_shapes)
    assert rows_w * PACK_COLS > sum(math.prod(s) for s in small_shapes)
    every = gather_small("gather_small_weights", _pack([w[n] for n in SMALL_SHARDED], rows_w), reduce=False)
    per_chip = every[0::2].reshape(N_CHIPS, -1)
    at = 0
    for n, s in zip(SMALL_SHARDED, small_shapes):
        size = math.prod(s)
        full[n] = _unshard(per_chip[:, at:at + size].reshape(N_CHIPS, *s), SHARD_DIM[n])
        at += size
    for n in REPLICATED:
        full[n] = w[n]
    after_small = every[0, -1, -1].astype(BF16)

    depth = g_mix_pre.shape[0]
    own, gathers, tokens = {}, {}, []
    for layer in range(depth):
        for part in ("mix", "rest"):
            names = chunk_names(layer, part)
            if names:
                tagp = f"l{layer}_{part}"
                own[tagp] = {n: w[n][layer if n in COMMON_BIG else layer // 2].astype(BF16) + after_small for n in names}
                halves = [a.reshape(2, -1, a.shape[-1]) for a in own[tagp].values()]
                gathers[tagp], token = copies_start(f"gather_start_{tagp}", _gather_copies, halves,
                                                    [(N_CHIPS, *h.shape) for h in halves])
                tokens.append(token[0, 0])

    def layer_weights(layer, part, x_in):
        tagp = f"l{layer}_{part}"
        if tagp not in gathers:
            return {}
        shards, lands = copies_wait(f"gather_wait_{tagp}", _gather_copies, gathers[tagp], x_in)
        lands = pass_to_sibling(f"gather_pass_{tagp}", shards, lands)
        return {n: _unshard(g.reshape(N_CHIPS, *mine.shape), SHARD_DIM[n] - 1)
                for (n, mine), g in zip(own[tagp].items(), lands)}

    core_arr = ci.reshape(1).astype(jnp.int32)
    place_arr = jnp.stack([chip, ci]).astype(jnp.int32)
    in_flight, swapping, held = [], [], {}

    def exchange(after):
        layer, tagp, names, state = swapping.pop()
        mine, got = copies_wait(f"swap_wait_{tagp}", _swap_copies, state, after)
        sums = [pair_sum(f"pair_sum_{tagp}_{n}", o, g, core_arr) for n, o, g in zip(names, mine, got)]
        state, token = copies_start(f"exchange_start_{tagp}", _exchange_copies, sums, [b.shape for b in sums])
        in_flight.append((layer, tagp, names, state))
        return token

    def reduce_hook(layer, part, part_grads, after):
        token = exchange(after)[0, 0] if swapping else None
        held.update(part_grads)
        names = reduce_chunk(layer, part)
        if names:
            tagp = f"l{layer}_{part}"
            mine = [held.pop(n) for n in names]
            state, started = copies_start(f"swap_start_{tagp}", _swap_copies, mine,
                                          [(m.shape[0], *m.shape[2:]) for m in mine])
            swapping.append((layer, tagp, names, state))
            token = started[0, 0] if token is None else token + started[0, 0]
        return token

    sq_cols, dx, grads, _ = local_step(x[0] + sum(tokens), mem[0], loss_target[0], full, layer_weights, reduce_hook)
    exchange(dx)
    loss = lax.psum(0.5 / D_MODEL * jnp.sum(sq_cols), ("x", "y", "c"))

    reduced = {n: lax.empty((w[n].shape[0], 2, math.prod(w[n].shape[1:-1]) // 2, w[n].shape[-1]), F32) for n in BIG}
    for layer, tagp, names, state in in_flight:
        sums, parts = copies_wait(f"exchange_wait_{tagp}", _exchange_copies, state, dx)
        for n, mine, p in zip(names, sums, parts):
            index = layer if n in COMMON_BIG else layer // 2
            reduced[n] = chip_sum(f"chip_sum_{tagp}_{n}", mine, p, place_arr, reduced[n], index)
    shared, after_big = share_halves([reduced[n] for n in BIG])
    grad_out = {n: g.reshape(w[n].shape) for n, g in zip(BIG, shared)}

    small_names = REPLICATED + SMALL_SHARDED
    small_full_shapes = [grads[n].shape for n in small_names]
    total = gather_small("reduce_small_grads",
                         _pack([grads[n] for n in small_names], _rows_for(small_full_shapes)) + after_big[0, 0],
                         reduce=True)
    for n, g in zip(small_names, _unpack(total, small_full_shapes)):
        if n in SHARD_DIM:
            g = lax.dynamic_index_in_dim(_shardify(g, SHARD_DIM[n]), chip, axis=0, keepdims=False)
        grad_out[n] = g

    delta, new_m, new_v = {}, {}, {}
    for n in BIG:
        two_d = lambda a: a.reshape(-1, a.shape[-1])
        results = adamw(f"adamw_{n}", two_d(w[n]), two_d(grad_out[n]), two_d(m_in[n]), two_d(v_in[n]))
        delta[n], new_m[n], new_v[n], grad_out[n] = (a.reshape(w[n].shape) for a in results)
    shapes = [w[n].shape for n in small_names]
    rows = _rows_for(shapes)
    packed = [_pack([src[n] for n in small_names], rows) for src in (w, grad_out, m_in, v_in)]
    for dst, res in zip((delta, new_m, new_v), adamw("adamw_small", *packed)[:3]):
        for n, a in zip(small_names, _unpack(res, shapes)):
            dst[n] = a

    return (loss, dx[None], *[grad_out[n] for n in WEIGHTS], *[delta[n] for n in WEIGHTS],
            *[new_m[n] for n in WEIGHTS], *[new_v[n] for n in WEIGHTS])
```
